```python
import math
import jax, jax.numpy as jnp
from jax import lax
import numpy as np

D_MODEL = 1024
BATCH = 4
SEQ = 8192
DEPTH = 1

N_Q_HEADS = 8
N_KV_HEADS = 2
GROUP = N_Q_HEADS // N_KV_HEADS
HEAD_DIM = 64
ROT_DIM = HEAD_DIM // 4
ROPE_THETA = 500000.0
WINDOW = 128
ATTN_BLOCK = 128
ATTN_WIDTH = N_Q_HEADS * HEAD_DIM
KV_WIDTH = N_KV_HEADS * HEAD_DIM

HGRN_HEADS = 4
HGRN_KEY_DIM = 128
HGRN_VAL_DIM = 128
HGRN_KEY_WIDTH = HGRN_HEADS * HGRN_KEY_DIM
HGRN_WIDTH = HGRN_HEADS * HGRN_VAL_DIM
HGRN_CHUNK = 64

IN_SIZES = (ATTN_WIDTH, KV_WIDTH, KV_WIDTH,
            HGRN_KEY_WIDTH, HGRN_KEY_WIDTH, HGRN_WIDTH, HGRN_WIDTH,
            D_MODEL, D_MODEL)
IN_WIDTH = sum(IN_SIZES)

N_GROUPS = 4
EXPERTS_PER_GROUP = 8
N_EXPERTS = N_GROUPS * EXPERTS_PER_GROUP
TOP_K = 2
EXPERT_FF = 512
MOE_BLOCK = 128

NORM_EPS = 1e-6

kernel_name = "hybrid_swa_hgrn2_hmoe_block"


def rms_norm(x, w):
    xf = x.astype(jnp.float32)
    y = xf * lax.rsqrt(jnp.mean(xf * xf, axis=-1, keepdims=True) + NORM_EPS)
    return (y * w.astype(jnp.float32)).astype(x.dtype)


def rope_tables(positions):
    inv_freq = ROPE_THETA ** (-jnp.arange(0, ROT_DIM, 2, dtype=jnp.float32) / ROT_DIM)
    ang = positions.astype(jnp.float32)[..., None] * inv_freq
    return jnp.cos(ang)[:, :, None, :], jnp.sin(ang)[:, :, None, :]


def partial_rope(x, cos, sin):
    half = ROT_DIM // 2
    xf = x.astype(jnp.float32)
    x1, x2 = xf[..., :half], xf[..., half:ROT_DIM]
    out = jnp.concatenate([x1 * cos - x2 * sin, x2 * cos + x1 * sin, xf[..., ROT_DIM:]], axis=-1)
    return out.astype(x.dtype)


def sliding_window_gqa(q, k, v, sinks):
    B, S = q.shape[0], q.shape[1]
    nb = S // ATTN_BLOCK
    qb = q.reshape(B, nb, ATTN_BLOCK, N_KV_HEADS, GROUP, HEAD_DIM).astype(jnp.float32)
    kb = k.reshape(B, nb, ATTN_BLOCK, N_KV_HEADS, HEAD_DIM).astype(jnp.float32)
    vb = v.reshape(B, nb, ATTN_BLOCK, N_KV_HEADS, HEAD_DIM).astype(jnp.float32)
    pad = ((0, 0), (1, 0), (0, 0), (0, 0), (0, 0))
    kk = jnp.concatenate([jnp.pad(kb[:, :-1], pad), kb], axis=2)
    vv = jnp.concatenate([jnp.pad(vb[:, :-1], pad), vb], axis=2)
    s = jnp.einsum('bnqhgd,bnkhd->bnhgqk', qb, kk) * (HEAD_DIM ** -0.5)
    qi = jnp.arange(ATTN_BLOCK)[:, None] + ATTN_BLOCK
    kj = jnp.arange(2 * ATTN_BLOCK)[None, :]
    band = (kj <= qi) & (qi - kj < WINDOW)
    has_prev = (jnp.arange(nb)[:, None, None] > 0) | (kj[None] >= ATTN_BLOCK)
    valid = band[None] & has_prev
    s = jnp.where(valid[None, :, None, None], s, -jnp.inf)
    sink = sinks.astype(jnp.float32).reshape(N_KV_HEADS, GROUP)[None, None, :, :, None, None]
    m = jnp.maximum(jnp.max(s, axis=-1, keepdims=True), sink)
    p = jnp.exp(s - m)
    denom = jnp.sum(p, axis=-1, keepdims=True) + jnp.exp(sink - m)
    o = jnp.einsum('bnhgqk,bnkhd->bnqhgd', p / denom, vv)
    return o.reshape(B, S, ATTN_WIDTH).astype(q.dtype)


def hgrn2_chunkwise(q, k, v, log_f):
    B, S = q.shape[0], q.shape[1]
    nc = S // HGRN_CHUNK

    def to_chunks(t):
        return t.reshape(B, nc, HGRN_CHUNK, HGRN_HEADS, t.shape[-1]).transpose(1, 0, 3, 2, 4)

    causal = jnp.tril(jnp.ones((HGRN_CHUNK, HGRN_CHUNK), dtype=bool))

    def step(state, inp):
        qc, kc, vc, gc = inp
        b = jnp.cumsum(gc, axis=2)
        o_inter = jnp.einsum('bhtk,bhkv->bhtv', qc * jnp.exp(b), state)
        diff = b[:, :, :, None, :] - b[:, :, None, :, :]
        decay = jnp.exp(jnp.where(causal[:, :, None], diff, -jnp.inf))
        a = jnp.einsum('bhtk,bhsk,bhtsk->bhts', qc, kc, decay)
        o_intra = jnp.einsum('bhts,bhsv->bhtv', a, vc)
        b_last = b[:, :, -1:, :]
        state = jnp.exp(b_last[:, :, 0])[..., None] * state + jnp.einsum(
            'bhsk,bhsv->bhkv', kc * jnp.exp(b_last - b), vc)
        return state, o_inter + o_intra

    init = jnp.zeros((B, HGRN_HEADS, HGRN_KEY_DIM, HGRN_VAL_DIM), jnp.float32)
    _, o = lax.scan(step, init, (to_chunks(q), to_chunks(k), to_chunks(v), to_chunks(log_f)))
    return o.transpose(1, 0, 3, 2, 4).reshape(B, S, HGRN_HEADS, HGRN_VAL_DIM)


def hierarchical_moe(xn, w_rg, b_rg, w_re, b_re, w_gate, w_up, w_down):
    T, D = xn.shape
    lg = (xn @ w_rg).astype(jnp.float32) + b_rg.astype(jnp.float32)
    pg = jax.nn.softmax(lg, axis=-1)
    gsel = jnp.argmax(pg, axis=-1)
    pgsel = jnp.take_along_axis(pg, gsel[:, None], axis=1)[:, 0]
    le = ((xn @ w_re).astype(jnp.float32) + b_re.astype(jnp.float32)).reshape(T, N_GROUPS, EXPERTS_PER_GROUP)
    le_sel = jnp.take_along_axis(le, gsel[:, None, None], axis=1)[:, 0]
    pe = jax.nn.softmax(le_sel, axis=-1)
    topv, topi = lax.top_k(pe, TOP_K)
    gate_w = pgsel[:, None] * topv / jnp.sum(topv, axis=-1, keepdims=True)
    eid = gsel[:, None] * EXPERTS_PER_GROUP + topi

    TK = T * TOP_K
    flat_e = eid.reshape(-1).astype(jnp.int32)
    flat_tok = jnp.repeat(jnp.arange(T, dtype=jnp.int32), TOP_K)
    flat_w = gate_w.reshape(-1)
    order = jnp.argsort(flat_e)
    se = flat_e[order]
    counts = jnp.bincount(flat_e, length=N_EXPERTS)
    starts = jnp.cumsum(counts) - counts
    padded = ((counts + MOE_BLOCK - 1) // MOE_BLOCK) * MOE_BLOCK
    pends = jnp.cumsum(padded)
    pstarts = pends - padded
    dest = pstarts[se] + jnp.arange(TK, dtype=jnp.int32) - starts[se]
    P = TK + N_EXPERTS * MOE_BLOCK
    P = ((P + MOE_BLOCK - 1) // MOE_BLOCK) * MOE_BLOCK
    nblk = P // MOE_BLOCK
    row_tok = jnp.full((P,), T, jnp.int32).at[dest].set(flat_tok[order])
    row_w = jnp.zeros((P,), jnp.float32).at[dest].set(flat_w[order])
    blk_e = jnp.minimum(jnp.searchsorted(pends, jnp.arange(nblk, dtype=jnp.int32) * MOE_BLOCK, side='right'),
                        N_EXPERTS - 1)
    x_pad = jnp.concatenate([xn, jnp.zeros((1, D), xn.dtype)], axis=0)
    x_blocks = x_pad[row_tok].reshape(nblk, MOE_BLOCK, D)

    def expert_block(args):
        xb, e = args
        h = jax.nn.silu(xb @ w_gate[e]) * (xb @ w_up[e])
        return h @ w_down[e]

    y_blocks = lax.map(expert_block, (x_blocks, blk_e))
    y_rows = y_blocks.reshape(P, D) * row_w[:, None].astype(y_blocks.dtype)
    return jax.ops.segment_sum(y_rows, row_tok, num_segments=T + 1)[:T]


def setup_inputs(seed: int = 0) -> dict:
    key = jax.random.key(seed)
    ks = jax.random.split(key, 24)
    f32 = jnp.float32
    nrm = lambda k, shape, scale: jax.random.normal(k, shape, f32) * scale
    x = jax.random.normal(ks[0], (BATCH, SEQ, D_MODEL), f32)
    offs = jax.random.randint(ks[1], (BATCH, 1), 0, 4096, dtype=jnp.int32)
    positions = offs + jnp.arange(SEQ, dtype=jnp.int32)[None, :]
    return {
        "x": x,
        "positions": positions,
        "norm1_w": 1.0 + nrm(ks[2], (DEPTH, D_MODEL), 0.02),
        "w_in": nrm(ks[3], (DEPTH, D_MODEL, IN_WIDTH), D_MODEL ** -0.5),
        "q_norm_w": 1.0 + nrm(ks[4], (DEPTH, HEAD_DIM), 0.02),
        "k_norm_w": 1.0 + nrm(ks[5], (DEPTH, HEAD_DIM), 0.02),
        "attn_sinks": nrm(ks[6], (DEPTH, N_Q_HEADS), 0.5),
        "hgrn_lower_bounds": nrm(ks[7], (DEPTH + 1, HGRN_KEY_WIDTH), 0.5),
        "hgrn_norm_w": 1.0 + nrm(ks[8], (DEPTH, HGRN_VAL_DIM), 0.02),
        "w_branch_attn": nrm(ks[9], (DEPTH, ATTN_WIDTH, D_MODEL), ATTN_WIDTH ** -0.5),
        "w_branch_hgrn": nrm(ks[10], (DEPTH, HGRN_WIDTH, D_MODEL), HGRN_WIDTH ** -0.5),
        "w_out": nrm(ks[11], (DEPTH, D_MODEL, D_MODEL), D_MODEL ** -0.5),
        "norm2_w": 1.0 + nrm(ks[12], (DEPTH, D_MODEL), 0.02),
        "w_router_group": nrm(ks[13], (DEPTH, D_MODEL, N_GROUPS), D_MODEL ** -0.5),
        "b_router_group": nrm(ks[14], (DEPTH, N_GROUPS), 0.01),
        "w_router_expert": nrm(ks[15], (DEPTH, D_MODEL, N_EXPERTS), D_MODEL ** -0.5),
        "b_router_expert": nrm(ks[16], (DEPTH, N_EXPERTS), 0.01),
        "w_gate_experts": nrm(ks[17], (DEPTH, N_EXPERTS, D_MODEL, EXPERT_FF), D_MODEL ** -0.5),
        "w_up_experts": nrm(ks[18], (DEPTH, N_EXPERTS, D_MODEL, EXPERT_FF), D_MODEL ** -0.5),
        "w_down_experts": nrm(ks[19], (DEPTH, N_EXPERTS, EXPERT_FF, D_MODEL), EXPERT_FF ** -0.5),
    }


def reference(x, positions, norm1_w, w_in, q_norm_w, k_norm_w, attn_sinks, hgrn_lower_bounds,
              hgrn_norm_w, w_branch_attn, w_branch_hgrn, w_out, norm2_w, w_router_group,
              b_router_group, w_router_expert, b_router_expert, w_gate_experts, w_up_experts,
              w_down_experts):
    B, S, D = x.shape
    cos, sin = rope_tables(positions)
    split_points = [int(v) for v in np.cumsum(IN_SIZES)[:-1]]
    lower_bounds = jnp.cumsum(jax.nn.softmax(hgrn_lower_bounds.astype(jnp.float32), axis=0), axis=0)
    for l in range(DEPTH):
        xn = rms_norm(x, norm1_w[l])
        proj = xn @ w_in[l]
        q_a, k_a, v_a, q_h, f_h, i_h, g_h, z_a, z_b = jnp.split(proj, split_points, axis=-1)

        q_a = partial_rope(rms_norm(q_a.reshape(B, S, N_Q_HEADS, HEAD_DIM), q_norm_w[l]), cos, sin)
        k_a = partial_rope(rms_norm(k_a.reshape(B, S, N_KV_HEADS, HEAD_DIM), k_norm_w[l]), cos, sin)
        v_a = v_a.reshape(B, S, N_KV_HEADS, HEAD_DIM)
        attn = sliding_window_gqa(q_a, k_a, v_a, attn_sinks[l])

        lb = lower_bounds[l]
        fg = lb + (1.0 - lb) * jax.nn.sigmoid(f_h.astype(jnp.float32))
        hq = jax.nn.silu(q_h.astype(jnp.float32)).reshape(B, S, HGRN_HEADS, HGRN_KEY_DIM)
        hk = (1.0 - fg).reshape(B, S, HGRN_HEADS, HGRN_KEY_DIM)
        hlogf = jnp.log(fg).reshape(B, S, HGRN_HEADS, HGRN_KEY_DIM)
        hv = i_h.astype(jnp.float32).reshape(B, S, HGRN_HEADS, HGRN_VAL_DIM)
        ho = rms_norm(hgrn2_chunkwise(hq, hk, hv, hlogf), hgrn_norm_w[l]).reshape(B, S, HGRN_WIDTH)
        hgrn = (ho * jax.nn.silu(g_h.astype(jnp.float32))).astype(x.dtype)

        mixed = (jax.nn.sigmoid(z_a) * (attn @ w_branch_attn[l])
                 + jax.nn.sigmoid(z_b) * (hgrn @ w_branch_hgrn[l]))
        x = x + mixed @ w_out[l]

        hn = rms_norm(x, norm2_w[l]).reshape(B * S, D)
        moe = hierarchical_moe(hn, w_router_group[l], b_router_group[l], w_router_expert[l],
                               b_router_expert[l], w_gate_experts[l], w_up_experts[l], w_down_experts[l])
        x = x + moe.reshape(B, S, D).astype(x.dtype)
    return x
```

```python
import functools

import numpy as np
import jax
import jax.numpy as jnp
from jax import lax
from jax.experimental import pallas as pl
from jax.experimental.pallas import tpu as pltpu

F32 = jnp.float32
BF16 = jnp.bfloat16

D_MODEL = 1024
N_Q_HEADS = 8
N_KV_HEADS = 2
GROUP = N_Q_HEADS // N_KV_HEADS
HEAD_DIM = 64
ROT_DIM = HEAD_DIM // 4
ROT_HALF = ROT_DIM // 2
ROPE_THETA = 500000.0
WINDOW = 128
ATTN_WIDTH = N_Q_HEADS * HEAD_DIM
KV_WIDTH = N_KV_HEADS * HEAD_DIM

HGRN_HEADS = 4
HGRN_DIM = 128
HGRN_WIDTH = HGRN_HEADS * HGRN_DIM
CHUNK = 64

N_GROUPS = 4
EXPERTS_PER_GROUP = 8
N_EXPERTS = N_GROUPS * EXPERTS_PER_GROUP
TOP_K = 2
EXPERT_FF = 512
MOE_BLOCK = 128
NORM_EPS = 1e-6

LANES = 128
NEG_BIG = -1e30

_OFF_Q, _OFF_K, _OFF_V = 0, ATTN_WIDTH, ATTN_WIDTH + KV_WIDTH
_OFF_HQ = ATTN_WIDTH + 2 * KV_WIDTH
_OFF_HF = _OFF_HQ + HGRN_WIDTH
_OFF_HI = _OFF_HF + HGRN_WIDTH
_OFF_HG = _OFF_HI + HGRN_WIDTH
_OFF_Z = _OFF_HG + HGRN_WIDTH

VMEM_LIMIT = 56 * 1024 * 1024


def _split3(a):
    hi = a.astype(BF16)
    r1 = a - hi.astype(F32)
    mid = r1.astype(BF16)
    lo = (r1 - mid.astype(F32)).astype(BF16)
    return hi, mid, lo


def _dot(a, b):
    return jnp.dot(a, b, preferred_element_type=F32)


def _dot_nt(a, b):
    return lax.dot_general(a, b, (((1,), (1,)), ((), ())), preferred_element_type=F32)


def _dot_tn(a, b):
    return lax.dot_general(a, b, (((0,), (0,)), ((), ())), preferred_element_type=F32)


def _exact_lhs_dot(m01, a):
    hi, mid, lo = _split3(a)
    return _dot(m01, hi) + _dot(m01, mid) + _dot(m01, lo)


def _exact_rhs_dot(a, m01):
    hi, mid, lo = _split3(a)
    return _dot(hi, m01) + _dot(mid, m01) + _dot(lo, m01)


def _sigmoid(x):
    return 1.0 / (1.0 + jnp.exp(-x))


def _rms(x, w):
    ms = jnp.mean(x * x, axis=-1, keepdims=True)
    return x * lax.rsqrt(ms + NORM_EPS) * w


def _inproj_kernel(x_ref, n1w_ref, w_ref, cs_ref, rope_e_ref, rope_c0_ref, qw_ref, kw_ref,
                   mq_ref, mk_ref, lbp_ref,
                   q_out, k_out, v_out, hq_out, lf_out, hv_out, hg_out):
    xn = _rms(x_ref[...], n1w_ref[...]).astype(BF16)

    def proj(off, width):
        return _dot(xn, w_ref[:, off:off + width])

    tabs = _exact_rhs_dot(cs_ref[...], rope_e_ref[...])
    c_tab = tabs[:, 0:LANES] + rope_c0_ref[...]
    s1_tab = tabs[:, LANES:2 * LANES]
    s2_tab = tabs[:, 2 * LANES:3 * LANES]

    def norm_rope(t, mavg_ref, w_row, scale):
        ms = _dot((t * t).astype(BF16), mavg_ref[...])
        tn = t * lax.rsqrt(ms + NORM_EPS) * w_row
        if scale != 1.0:
            tn = tn * scale
        outs = []
        for j in range(t.shape[1] // LANES):
            c = tn[:, j * LANES:(j + 1) * LANES]
            outs.append(c * c_tab
                        + pltpu.roll(c, LANES - ROT_HALF, 1) * s1_tab
                        + pltpu.roll(c, ROT_HALF, 1) * s2_tab)
        return outs[0] if len(outs) == 1 else jnp.concatenate(outs, axis=1)

    q_out[...] = norm_rope(proj(_OFF_Q, ATTN_WIDTH), mq_ref, qw_ref[...], HEAD_DIM ** -0.5).astype(BF16)
    k_out[...] = norm_rope(proj(_OFF_K, KV_WIDTH), mk_ref, kw_ref[...], 1.0).astype(BF16)
    v_out[...] = proj(_OFF_V, KV_WIDTH).astype(BF16)

    hq = proj(_OFF_HQ, HGRN_WIDTH)
    hq_out[...] = (hq * _sigmoid(hq)).astype(BF16)
    h0 = lbp_ref[0:1, :]
    h1 = lbp_ref[1:2, :]
    hm = jnp.maximum(h0, h1)
    e0 = jnp.exp(h0 - hm)
    e1 = jnp.exp(h1 - hm)
    lb = e0 / (e0 + e1)
    fg = lb + (1.0 - lb) * _sigmoid(proj(_OFF_HF, HGRN_WIDTH))
    lf_out[...] = jnp.log(fg)
    hv_out[...] = proj(_OFF_HI, HGRN_WIDTH).astype(BF16)
    hg = proj(_OFF_HG, HGRN_WIDTH)
    hg_out[...] = (hg * _sigmoid(hg)).astype(BF16)


def _rope_constants():
    e = np.zeros((2 * ROT_HALF, 3 * LANES), np.float32)
    c0 = np.zeros((1, LANES), np.float32)
    for lane in range(LANES):
        d = lane % HEAD_DIM
        if d < ROT_HALF:
            e[d, lane] = 1.0
            e[ROT_HALF + d, LANES + lane] = -1.0
        elif d < ROT_DIM:
            e[d - ROT_HALF, lane] = 1.0
            e[ROT_HALF + d - ROT_HALF, 2 * LANES + lane] = 1.0
        else:
            c0[0, lane] = 1.0
    return jnp.asarray(e, BF16), jnp.asarray(c0, F32)


def _head_mean_matrix(width):
    idx = np.arange(width) // HEAD_DIM
    m = (idx[:, None] == idx[None, :]).astype(np.float32) / HEAD_DIM
    return jnp.asarray(m, BF16)


def _inproj(x2, norm1_w, w_in_a, cs, q_norm_w, k_norm_w, lbp, tm):
    T = x2.shape[0]
    rope_e, rope_c0 = _rope_constants()
    qw = jnp.tile(q_norm_w.astype(F32), N_Q_HEADS)[None, :]
    kw = jnp.tile(k_norm_w.astype(F32), N_KV_HEADS)[None, :]
    mq = _head_mean_matrix(ATTN_WIDTH)
    mk = _head_mean_matrix(KV_WIDTH)
    row = lambda w: pl.BlockSpec((tm, w), lambda i: (i, 0))
    full = lambda a: pl.BlockSpec(a.shape, lambda i: (0,) * a.ndim)
    ins = [x2, norm1_w[None, :], w_in_a, cs, rope_e, rope_c0, qw, kw, mq, mk, lbp]
    in_specs = [row(D_MODEL), full(ins[1]), full(w_in_a), row(cs.shape[1])] + [full(a) for a in ins[4:]]
    outs = [(ATTN_WIDTH, BF16), (KV_WIDTH, BF16), (KV_WIDTH, BF16), (HGRN_WIDTH, BF16),
            (HGRN_WIDTH, F32), (HGRN_WIDTH, BF16), (HGRN_WIDTH, BF16)]
    return pl.pallas_call(
        _inproj_kernel,
        out_shape=[jax.ShapeDtypeStruct((T, w), dt) for w, dt in outs],
        grid=(T // tm,),
        in_specs=in_specs,
        out_specs=[row(w) for w, _ in outs],
        compiler_params=pltpu.CompilerParams(dimension_semantics=("arbitrary",),
                                             vmem_limit_bytes=VMEM_LIMIT),
        name="inproj",
    )(*ins)


def _attn_kernel(sink_ref, q_ref, kc_ref, kp_ref, vc_ref, vp_ref, o_ref):
    has_prev = pl.program_id(1) > 0
    qi = lax.broadcasted_iota(jnp.int32, (WINDOW, WINDOW), 0)
    kj = lax.broadcasted_iota(jnp.int32, (WINDOW, WINDOW), 1)
    mask_c = kj <= qi
    mask_p = (kj > qi) & has_prev
    outs = []
    for h in range(N_KV_HEADS):
        sl = slice(h * HEAD_DIM, (h + 1) * HEAD_DIM)
        kc, kp, vc, vp = kc_ref[:, sl], kp_ref[:, sl], vc_ref[:, sl], vp_ref[:, sl]
        for g in range(GROUP):
            hq = h * GROUP + g
            q = q_ref[:, hq * HEAD_DIM:(hq + 1) * HEAD_DIM]
            sc = jnp.where(mask_c, _dot_nt(q, kc), NEG_BIG)
            sp = jnp.where(mask_p, _dot_nt(q, kp), NEG_BIG)
            sink = sink_ref[hq]
            m = jnp.maximum(jnp.max(jnp.maximum(sc, sp), axis=-1, keepdims=True), sink)
            pc = jnp.exp(sc - m)
            pp = jnp.exp(sp - m)
            denom = (jnp.sum(pc, axis=-1, keepdims=True) + jnp.sum(pp, axis=-1, keepdims=True)
                     + jnp.exp(sink - m))
            o = _dot(pc.astype(BF16), vc) + _dot(pp.astype(BF16), vp)
            outs.append(o / denom)
    o_ref[...] = jnp.concatenate(outs, axis=1).astype(BF16)


def _attention(q, k, v, sinks, B, S):
    nb = S // WINDOW
    cur = lambda b, n: (b * nb + n, 0)
    prev = lambda b, n: (b * nb + jnp.maximum(n - 1, 0), 0)
    return pl.pallas_call(
        _attn_kernel,
        out_shape=jax.ShapeDtypeStruct((B * S, ATTN_WIDTH), BF16),
        grid=(B, nb),
        in_specs=[pl.BlockSpec(memory_space=pltpu.SMEM),
                  pl.BlockSpec((WINDOW, ATTN_WIDTH), cur),
                  pl.BlockSpec((WINDOW, KV_WIDTH), cur),
                  pl.BlockSpec((WINDOW, KV_WIDTH), prev),
                  pl.BlockSpec((WINDOW, KV_WIDTH), cur),
                  pl.BlockSpec((WINDOW, KV_WIDTH), prev)],
        out_specs=pl.BlockSpec((WINDOW, ATTN_WIDTH), cur),
        compiler_params=pltpu.CompilerParams(dimension_semantics=("arbitrary", "arbitrary"),
                                             vmem_limit_bytes=VMEM_LIMIT),
        name="swa_attention",
    )(sinks, q, k, k, v, v)


_LEVEL_HALVES = (32, 16, 8, 4, 2)


def _hgrn_chunk(q, lf, v, st_ref, h, b_ref, tri):
    row = lax.broadcasted_iota(jnp.int32, (CHUNK, HGRN_DIM), 0)
    ti = lax.broadcasted_iota(jnp.int32, (CHUNK, CHUNK), 0)
    si = lax.broadcasted_iota(jnp.int32, (CHUNK, CHUNK), 1)
    f = jnp.exp(lf)
    k = 1.0 - f
    b = _exact_lhs_dot(tri, lf)
    b_ref[...] = b
    b_last = b[CHUNK - 1:CHUNK, :]
    v_bf = v.astype(BF16)

    st = st_ref[h]
    q_in = (q * jnp.exp(b)).astype(BF16)
    o = _dot_nt(q_in, st.astype(BF16))
    k_out = (k * jnp.exp(b_last - b)).astype(BF16)
    st_ref[h] = st * jnp.exp(b_last) + _dot_tn(v_bf, k_out)

    o = o + jnp.sum(q * k, axis=-1, keepdims=True) * v
    a = jnp.zeros((CHUNK, CHUNK), F32)
    for half in (1,) + _LEVEL_HALVES[::-1]:
        upper = (row & half) != 0
        if half == 1:
            e = jnp.where(upper, f, 1.0)
        else:
            blk = 2 * half
            if half >= 4:
                pieces = [jnp.broadcast_to(b_ref[s + half - 1:s + half, :], (blk if blk >= 8 else 8, HGRN_DIM))
                          for s in range(0, CHUNK, max(blk, 8))]
            else:
                pieces = []
                for s in range(0, CHUNK, 8):
                    lo_ref = jnp.broadcast_to(b_ref[s + 1:s + 2, :], (8, HGRN_DIM))
                    hi_ref = jnp.broadcast_to(b_ref[s + 5:s + 6, :], (8, HGRN_DIM))
                    r8 = lax.broadcasted_iota(jnp.int32, (8, HGRN_DIM), 0)
                    pieces.append(jnp.where(r8 < 4, lo_ref, hi_ref))
            bref = pieces[0] if len(pieces) == 1 else jnp.concatenate(pieces, axis=0)
            e = jnp.exp(-jnp.abs(b - bref))
        qe = jnp.where(upper, q * e, 0.0).astype(BF16)
        ke = jnp.where(upper, 0.0, k * e).astype(BF16)
        same_blk = (ti // (2 * half)) == (si // (2 * half))
        a = a + jnp.where(same_blk, _dot_nt(qe, ke), 0.0)
    return o + _dot(a.astype(BF16), v_bf)


def _hgrn_kernel(hq_ref, lf_ref, hv_ref, hg_ref, nw_ref, tri_ref, o_ref, st_ref, b_ref):
    @pl.when(pl.program_id(1) == 0)
    def _():
        st_ref[...] = jnp.zeros_like(st_ref)

    tri = tri_ref[...]
    nchunks = hq_ref.shape[0] // CHUNK
    for c in range(nchunks):
        rows = slice(c * CHUNK, (c + 1) * CHUNK)
        for h in range(HGRN_HEADS):
            cols = slice(h * HGRN_DIM, (h + 1) * HGRN_DIM)
            o = _hgrn_chunk(hq_ref[rows, cols].astype(F32), lf_ref[rows, cols],
                            hv_ref[rows, cols].astype(F32), st_ref, h, b_ref, tri)
            y = _rms(o, nw_ref[...]) * hg_ref[rows, cols].astype(F32)
            o_ref[rows, cols] = y.astype(BF16)


def _hgrn(hq, lf, hv, hg, hgrn_norm_w, B, S, tb):
    nt = S // tb
    blk = pl.BlockSpec((tb, HGRN_WIDTH), lambda b, n: (b * nt + n, 0))
    tri = jnp.asarray(np.tril(np.ones((CHUNK, CHUNK), np.float32)), BF16)
    return pl.pallas_call(
        _hgrn_kernel,
        out_shape=jax.ShapeDtypeStruct((B * S, HGRN_WIDTH), BF16),
        grid=(B, nt),
        in_specs=[blk, blk, blk, blk,
                  pl.BlockSpec((1, HGRN_DIM), lambda b, n: (0, 0)),
                  pl.BlockSpec((CHUNK, CHUNK), lambda b, n: (0, 0))],
        out_specs=blk,
        scratch_shapes=[pltpu.VMEM((HGRN_HEADS, HGRN_DIM, HGRN_DIM), F32),
                        pltpu.VMEM((CHUNK, HGRN_DIM), F32)],
        compiler_params=pltpu.CompilerParams(dimension_semantics=("arbitrary", "arbitrary"),
                                             vmem_limit_bytes=VMEM_LIMIT),
        name="hgrn2",
    )(hq, lf, hv, hg, hgrn_norm_w[None, :].astype(F32), tri)


def _merge_kernel(x_ref, attn_ref, hgrn_ref, n1w_ref, wz_ref, wba_ref, wbh_ref, wout_ref, n2w_ref,
                  wr_hi_ref, wr_lo_ref, br_ref, x1_out, hn_out, route_out):
    x = x_ref[...]
    xn = _rms(x, n1w_ref[...]).astype(BF16)
    za = _sigmoid(_dot(xn, wz_ref[:, 0:D_MODEL]))
    zb = _sigmoid(_dot(xn, wz_ref[:, D_MODEL:2 * D_MODEL]))
    mixed = za * _dot(attn_ref[...], wba_ref[...]) + zb * _dot(hgrn_ref[...], wbh_ref[...])
    x1 = x + _dot(mixed.astype(BF16), wout_ref[...])
    x1_out[...] = x1
    hn = _rms(x1, n2w_ref[...])
    hn_out[...] = hn

    hn_hi = hn.astype(BF16)
    hn_lo = (hn - hn_hi.astype(F32)).astype(BF16)
    logits = (_dot(hn_hi, wr_hi_ref[...]) + _dot(hn_lo, wr_hi_ref[...]) + _dot(hn_hi, wr_lo_ref[...])
              + br_ref[...])
    lane = lax.broadcasted_iota(jnp.int32, logits.shape, 1).astype(F32)
    far = float(LANES)

    lg = jnp.where(lane < N_GROUPS, logits, NEG_BIG)
    mg = jnp.max(lg, axis=-1, keepdims=True)
    gsel = jnp.min(jnp.where(lg == mg, lane, far), axis=-1, keepdims=True)
    pgsel = 1.0 / jnp.sum(jnp.exp(lg - mg), axis=-1, keepdims=True)

    lo = N_GROUPS + EXPERTS_PER_GROUP * gsel
    le = jnp.where((lane >= lo) & (lane < lo + EXPERTS_PER_GROUP), logits, NEG_BIG)
    m1 = jnp.max(le, axis=-1, keepdims=True)
    i1 = jnp.min(jnp.where(le == m1, lane, far), axis=-1, keepdims=True)
    se = jnp.sum(jnp.exp(le - m1), axis=-1, keepdims=True)
    le2 = jnp.where(lane == i1, NEG_BIG, le)
    m2 = jnp.max(le2, axis=-1, keepdims=True)
    i2 = jnp.min(jnp.where(le2 == m2, lane, far), axis=-1, keepdims=True)
    top0 = 1.0 / se
    top1 = jnp.exp(m2 - m1) / se
    tsum = top0 + top1
    w0 = pgsel * top0 / tsum
    w1 = pgsel * top1 / tsum
    route = jnp.where(lane == 0, i1 - N_GROUPS,
                      jnp.where(lane == 1, i2 - N_GROUPS,
                                jnp.where(lane == 2, w0, jnp.where(lane == 3, w1, 0.0))))
    route_out[...] = route


def _merge(x2, attn, hgrn, norm1_w, w_z, w_ba, w_bh, w_out, norm2_w, wr_hi, wr_lo, br, tm):
    T = x2.shape[0]
    row = lambda w: pl.BlockSpec((tm, w), lambda i: (i, 0))
    full = lambda a: pl.BlockSpec(a.shape, lambda i: (0,) * a.ndim)
    ins = [x2, attn, hgrn, norm1_w[None, :], w_z, w_ba, w_bh, w_out, norm2_w[None, :], wr_hi, wr_lo, br]
    in_specs = [row(D_MODEL), row(ATTN_WIDTH), row(HGRN_WIDTH)] + [full(a) for a in ins[3:]]
    return pl.pallas_call(
        _merge_kernel,
        out_shape=[jax.ShapeDtypeStruct((T, D_MODEL), F32), jax.ShapeDtypeStruct((T, D_MODEL), F32),
                   jax.ShapeDtypeStruct((T, LANES), F32)],
        grid=(T // tm,),
        in_specs=in_specs,
        out_specs=[row(D_MODEL), row(D_MODEL), row(LANES)],
        compiler_params=pltpu.CompilerParams(dimension_semantics=("arbitrary",),
                                             vmem_limit_bytes=VMEM_LIMIT),
        name="merge_router",
    )(*ins)


def _gather_rows(idx_ref, n, src_hbm, dst_ref, sem):
    def issue(r, carry):
        pltpu.make_async_copy(src_hbm.at[pl.ds(idx_ref[0, 0, r], 1), :],
                              dst_ref.at[pl.ds(r, 1), :], sem).start()
        return carry
    lax.fori_loop(0, n, issue, 0)

    def drain(r, carry):
        pltpu.make_async_copy(src_hbm.at[pl.ds(0, 1), :], dst_ref.at[pl.ds(r, 1), :], sem).wait()
        return carry
    lax.fori_loop(0, n, drain, 0)


def _expert_kernel(blk_e_ref, tok_ref, hn_hbm, wg_ref, wu_ref, wd_ref, y_ref,
                   xbuf, wg_s, wu_s, wd_s, sem):
    i = pl.program_id(0)
    _gather_rows(tok_ref, MOE_BLOCK, hn_hbm, xbuf, sem)

    changed = (i == 0) | (blk_e_ref[i] != blk_e_ref[jnp.maximum(i - 1, 0)])

    @pl.when(changed)
    def _():
        wg_s[...] = wg_ref[0].astype(BF16)
        wu_s[...] = wu_ref[0].astype(BF16)
        wd_s[...] = wd_ref[0].astype(BF16)

    xb = xbuf[...].astype(BF16)
    gate = _dot(xb, wg_s[...])
    up = _dot(xb, wu_s[...])
    hmid = (gate * _sigmoid(gate) * up).astype(BF16)
    y_ref[...] = _dot(hmid, wd_s[...])


def _experts(blk_e, row_tok, hn, w_gate, w_up, w_down):
    nblk = blk_e.shape[0]
    tok3 = row_tok.reshape(nblk, 1, MOE_BLOCK)
    wspec = lambda shape: pl.BlockSpec((1,) + shape, lambda i, be: (be[i], 0, 0))
    return pl.pallas_call(
        _expert_kernel,
        out_shape=jax.ShapeDtypeStruct((nblk * MOE_BLOCK, D_MODEL), F32),
        grid_spec=pltpu.PrefetchScalarGridSpec(
            num_scalar_prefetch=1,
            grid=(nblk,),
            in_specs=[pl.BlockSpec((1, 1, MOE_BLOCK), lambda i, be: (i, 0, 0), memory_space=pltpu.SMEM),
                      pl.BlockSpec(memory_space=pl.ANY),
                      wspec((D_MODEL, EXPERT_FF)), wspec((D_MODEL, EXPERT_FF)), wspec((EXPERT_FF, D_MODEL))],
            out_specs=pl.BlockSpec((MOE_BLOCK, D_MODEL), lambda i, be: (i, 0)),
            scratch_shapes=[pltpu.VMEM((MOE_BLOCK, D_MODEL), F32),
                            pltpu.VMEM((D_MODEL, EXPERT_FF), BF16),
                            pltpu.VMEM((D_MODEL, EXPERT_FF), BF16),
                            pltpu.VMEM((EXPERT_FF, D_MODEL), BF16),
                            pltpu.SemaphoreType.DMA]),
        compiler_params=pltpu.CompilerParams(dimension_semantics=("arbitrary",),
                                             vmem_limit_bytes=VMEM_LIMIT),
        name="moe_experts",
    )(blk_e, tok3, hn, w_gate, w_up, w_down)


def _combine_kernel(pos_ref, x1_ref, route_ref, y_hbm, o_ref, ybuf, sem):
    tk = x1_ref.shape[0]
    _gather_rows(pos_ref, 2 * tk, y_hbm, ybuf, sem)
    w0 = route_ref[:, 2:3]
    w1 = route_ref[:, 3:4]
    o_ref[...] = x1_ref[...] + w0 * ybuf[0:tk, :] + w1 * ybuf[tk:2 * tk, :]


def _combine(pos, x1, route, y, tk):
    T = x1.shape[0]
    nt = T // tk
    pos3 = pos.reshape(nt, tk, TOP_K).transpose(0, 2, 1).reshape(nt, 1, TOP_K * tk)
    return pl.pallas_call(
        _combine_kernel,
        out_shape=jax.ShapeDtypeStruct((T, D_MODEL), F32),
        grid=(nt,),
        in_specs=[pl.BlockSpec((1, 1, TOP_K * tk), lambda i: (i, 0, 0), memory_space=pltpu.SMEM),
                  pl.BlockSpec((tk, D_MODEL), lambda i: (i, 0)),
                  pl.BlockSpec((tk, LANES), lambda i: (i, 0)),
                  pl.BlockSpec(memory_space=pl.ANY)],
        out_specs=pl.BlockSpec((tk, D_MODEL), lambda i: (i, 0)),
        scratch_shapes=[pltpu.VMEM((TOP_K * tk, D_MODEL), F32), pltpu.SemaphoreType.DMA],
        compiler_params=pltpu.CompilerParams(dimension_semantics=("arbitrary",),
                                             vmem_limit_bytes=VMEM_LIMIT),
        name="moe_combine",
    )(pos3, x1, route, y)


def _routing_tables(eid, T):
    tk_total = T * TOP_K
    flat_e = eid.reshape(-1)
    onehot = (flat_e[:, None] == jnp.arange(N_EXPERTS, dtype=jnp.int32)[None, :]).astype(jnp.int32)
    csum = jnp.cumsum(onehot, axis=0)
    counts = csum[-1]
    rank = jnp.take_along_axis(csum, flat_e[:, None], axis=1)[:, 0] - 1
    padded = ((counts + MOE_BLOCK - 1) // MOE_BLOCK) * MOE_BLOCK
    pends = jnp.cumsum(padded)
    pstarts = pends - padded
    pos = pstarts[flat_e] + rank
    nblk = tk_total // MOE_BLOCK + N_EXPERTS
    flat_tok = jnp.repeat(jnp.arange(T, dtype=jnp.int32), TOP_K)
    row_tok = jnp.zeros((nblk * MOE_BLOCK,), jnp.int32).at[pos].set(flat_tok)
    blk_e = jnp.minimum(jnp.searchsorted(pends, jnp.arange(nblk, dtype=jnp.int32) * MOE_BLOCK, side='right'),
                        N_EXPERTS - 1).astype(jnp.int32)
    return pos.reshape(T, TOP_K).astype(jnp.int32), row_tok, blk_e


def _pick_tile(n, pref):
    t = pref
    while n % t:
        t //= 2
    return t


def kernel(x, positions, norm1_w, w_in, q_norm_w, k_norm_w, attn_sinks, hgrn_lower_bounds, hgrn_norm_w,
           w_branch_attn, w_branch_hgrn, w_out, norm2_w, w_router_group, b_router_group, w_router_expert,
           b_router_expert, w_gate_experts, w_up_experts, w_down_experts):
    B, S, D = x.shape
    T = B * S
    x2 = x.reshape(T, D)
    tm = _pick_tile(T, 256)

    inv_freq = ROPE_THETA ** (-jnp.arange(0, ROT_DIM, 2, dtype=F32) / ROT_DIM)
    ang = positions.astype(F32).reshape(T, 1) * inv_freq[None, :]
    cs = jnp.concatenate([jnp.cos(ang), jnp.sin(ang)], axis=-1)

    w_in0 = w_in[0]
    w_in_a = w_in0[:, :_OFF_Z].astype(BF16)
    w_z = w_in0[:, _OFF_Z:].astype(BF16)

    q, k, v, hq, lf, hv, hg = _inproj(x2, norm1_w[0], w_in_a, cs, q_norm_w[0], k_norm_w[0],
                                      hgrn_lower_bounds.astype(F32), tm)
    attn = _attention(q, k, v, attn_sinks[0].astype(F32), B, S)
    hgrn = _hgrn(hq, lf, hv, hg, hgrn_norm_w[0], B, S, _pick_tile(S, 128))

    w_r = jnp.zeros((D, LANES), F32)
    w_r = w_r.at[:, :N_GROUPS].set(w_router_group[0]).at[:, N_GROUPS:N_GROUPS + N_EXPERTS].set(w_router_expert[0])
    wr_hi = w_r.astype(BF16)
    wr_lo = (w_r - wr_hi.astype(F32)).astype(BF16)
    b_r = jnp.zeros((1, LANES), F32)
    b_r = b_r.at[0, :N_GROUPS].set(b_router_group[0]).at[0, N_GROUPS:N_GROUPS + N_EXPERTS].set(b_router_expert[0])

    x1, hn, route = _merge(x2, attn, hgrn, norm1_w[0], w_z, w_branch_attn[0].astype(BF16),
                           w_branch_hgrn[0].astype(BF16), w_out[0].astype(BF16), norm2_w[0],
                           wr_hi, wr_lo, b_r, tm)

    eid = route[:, 0:TOP_K].astype(jnp.int32)
    pos, row_tok, blk_e = _routing_tables(eid, T)
    y = _experts(blk_e, row_tok, hn, w_gate_experts[0], w_up_experts[0], w_down_experts[0])
    out = _combine(pos, x1, route, y, _pick_tile(T, 128))
    return out.reshape(B, S, D)
```

```python
import functools

import numpy as np
import jax
import jax.numpy as jnp
from jax import lax
from jax.experimental import pallas as pl
from jax.experimental.pallas import tpu as pltpu

F32 = jnp.float32
BF16 = jnp.bfloat16

D_MODEL = 1024
N_Q_HEADS = 8
N_KV_HEADS = 2
GROUP = N_Q_HEADS // N_KV_HEADS
HEAD_DIM = 64
ROT_DIM = HEAD_DIM // 4
ROT_HALF = ROT_DIM // 2
ROPE_THETA = 500000.0
WINDOW = 128
ATTN_WIDTH = N_Q_HEADS * HEAD_DIM
KV_WIDTH = N_KV_HEADS * HEAD_DIM

HGRN_HEADS = 4
HGRN_DIM = 128
HGRN_WIDTH = HGRN_HEADS * HGRN_DIM
CHUNK = 64

N_GROUPS = 4
EXPERTS_PER_GROUP = 8
N_EXPERTS = N_GROUPS * EXPERTS_PER_GROUP
TOP_K = 2
EXPERT_FF = 512
MOE_BLOCK = 128
NORM_EPS = 1e-6

LANES = 128
NEG_BIG = -1e30

_OFF_Q, _OFF_K, _OFF_V = 0, ATTN_WIDTH, ATTN_WIDTH + KV_WIDTH
_OFF_HQ = ATTN_WIDTH + 2 * KV_WIDTH
_OFF_HF = _OFF_HQ + HGRN_WIDTH
_OFF_HI = _OFF_HF + HGRN_WIDTH
_OFF_HG = _OFF_HI + HGRN_WIDTH
_OFF_Z = _OFF_HG + HGRN_WIDTH

VMEM_LIMIT = 56 * 1024 * 1024


def _split3(a):
    hi = a.astype(BF16)
    r1 = a - hi.astype(F32)
    mid = r1.astype(BF16)
    lo = (r1 - mid.astype(F32)).astype(BF16)
    return hi, mid, lo


def _dot(a, b):
    return jnp.dot(a, b, preferred_element_type=F32)


def _dot_nt(a, b):
    return lax.dot_general(a, b, (((1,), (1,)), ((), ())), preferred_element_type=F32)


def _dot_tn(a, b):
    return lax.dot_general(a, b, (((0,), (0,)), ((), ())), preferred_element_type=F32)


def _exact_lhs_dot(m01, a):
    hi, mid, lo = _split3(a)
    return _dot(m01, hi) + _dot(m01, mid) + _dot(m01, lo)


def _exact_rhs_dot(a, m01):
    hi, mid, lo = _split3(a)
    return _dot(hi, m01) + _dot(mid, m01) + _dot(lo, m01)


def _sigmoid(x):
    return 1.0 / (1.0 + jnp.exp(-x))


def _rms(x, w):
    ms = jnp.mean(x * x, axis=-1, keepdims=True)
    return x * lax.rsqrt(ms + NORM_EPS) * w


def _inproj_kernel(x_ref, n1w_ref, w_ref, cs_ref, rope_e_ref, rope_c0_ref, qw_ref, kw_ref,
                   mq_ref, mk_ref, lbp_ref,
                   q_out, k_out, v_out, hq_out, lf_out, hv_out, hg_out):
    xn = _rms(x_ref[...], n1w_ref[...]).astype(BF16)

    def proj(off, width):
        return _dot(xn, w_ref[:, off:off + width])

    tabs = _exact_rhs_dot(cs_ref[...], rope_e_ref[...])
    c_tab = tabs[:, 0:LANES] + rope_c0_ref[...]
    s1_tab = tabs[:, LANES:2 * LANES]
    s2_tab = tabs[:, 2 * LANES:3 * LANES]

    def norm_rope(t, mavg_ref, w_row, scale):
        ms = _dot((t * t).astype(BF16), mavg_ref[...])
        tn = t * lax.rsqrt(ms + NORM_EPS) * w_row
        if scale != 1.0:
            tn = tn * scale
        outs = []
        for j in range(t.shape[1] // LANES):
            c = tn[:, j * LANES:(j + 1) * LANES]
            outs.append(c * c_tab
                        + pltpu.roll(c, LANES - ROT_HALF, 1) * s1_tab
                        + pltpu.roll(c, ROT_HALF, 1) * s2_tab)
        return outs[0] if len(outs) == 1 else jnp.concatenate(outs, axis=1)

    q_out[...] = norm_rope(proj(_OFF_Q, ATTN_WIDTH), mq_ref, qw_ref[...], HEAD_DIM ** -0.5).astype(BF16)
    k_out[...] = norm_rope(proj(_OFF_K, KV_WIDTH), mk_ref, kw_ref[...], 1.0).astype(BF16)
    v_out[...] = proj(_OFF_V, KV_WIDTH).astype(BF16)

    hq = proj(_OFF_HQ, HGRN_WIDTH)
    hq_out[...] = (hq * _sigmoid(hq)).astype(BF16)
    h0 = lbp_ref[0:1, :]
    h1 = lbp_ref[1:2, :]
    hm = jnp.maximum(h0, h1)
    e0 = jnp.exp(h0 - hm)
    e1 = jnp.exp(h1 - hm)
    lb = e0 / (e0 + e1)
    fg = lb + (1.0 - lb) * _sigmoid(proj(_OFF_HF, HGRN_WIDTH))
    lf_out[...] = jnp.log(fg)
    hv_out[...] = proj(_OFF_HI, HGRN_WIDTH).astype(BF16)
    hg = proj(_OFF_HG, HGRN_WIDTH)
    hg_out[...] = (hg * _sigmoid(hg)).astype(BF16)


def _rope_constants():
    e = np.zeros((2 * ROT_HALF, 3 * LANES), np.float32)
    c0 = np.zeros((1, LANES), np.float32)
    for lane in range(LANES):
        d = lane % HEAD_DIM
        if d < ROT_HALF:
            e[d, lane] = 1.0
            e[ROT_HALF + d, LANES + lane] = -1.0
        elif d < ROT_DIM:
            e[d - ROT_HALF, lane] = 1.0
            e[ROT_HALF + d - ROT_HALF, 2 * LANES + lane] = 1.0
        else:
            c0[0, lane] = 1.0
    return jnp.asarray(e, BF16), jnp.asarray(c0, F32)


def _head_mean_matrix(width):
    idx = np.arange(width) // HEAD_DIM
    m = (idx[:, None] == idx[None, :]).astype(np.float32) / HEAD_DIM
    return jnp.asarray(m, BF16)


def _inproj(x2, norm1_w, w_in_a, cs, q_norm_w, k_norm_w, lbp, tm):
    T = x2.shape[0]
    rope_e, rope_c0 = _rope_constants()
    qw = jnp.tile(q_norm_w.astype(F32), N_Q_HEADS)[None, :]
    kw = jnp.tile(k_norm_w.astype(F32), N_KV_HEADS)[None, :]
    mq = _head_mean_matrix(ATTN_WIDTH)
    mk = _head_mean_matrix(KV_WIDTH)
    row = lambda w: pl.BlockSpec((tm, w), lambda i: (i, 0))
    full = lambda a: pl.BlockSpec(a.shape, lambda i: (0,) * a.ndim)
    ins = [x2, norm1_w[None, :], w_in_a, cs, rope_e, rope_c0, qw, kw, mq, mk, lbp]
    in_specs = [row(D_MODEL), full(ins[1]), full(w_in_a), row(cs.shape[1])] + [full(a) for a in ins[4:]]
    outs = [(ATTN_WIDTH, BF16), (KV_WIDTH, BF16), (KV_WIDTH, BF16), (HGRN_WIDTH, BF16),
            (HGRN_WIDTH, F32), (HGRN_WIDTH, BF16), (HGRN_WIDTH, BF16)]
    return pl.pallas_call(
        _inproj_kernel,
        out_shape=[jax.ShapeDtypeStruct((T, w), dt) for w, dt in outs],
        grid=(T // tm,),
        in_specs=in_specs,
        out_specs=[row(w) for w, _ in outs],
        compiler_params=pltpu.CompilerParams(dimension_semantics=("arbitrary",),
                                             vmem_limit_bytes=VMEM_LIMIT),
        name="inproj",
    )(*ins)


def _attn_kernel(sink_ref, q_ref, kc_ref, kp_ref, vc_ref, vp_ref, o_ref):
    has_prev = pl.program_id(1) > 0
    qi = lax.broadcasted_iota(jnp.int32, (WINDOW, WINDOW), 0)
    kj = lax.broadcasted_iota(jnp.int32, (WINDOW, WINDOW), 1)
    mask_c = kj <= qi
    mask_p = (kj > qi) & has_prev
    outs = []
    for h in range(N_KV_HEADS):
        sl = slice(h * HEAD_DIM, (h + 1) * HEAD_DIM)
        kc, kp, vc, vp = kc_ref[:, sl], kp_ref[:, sl], vc_ref[:, sl], vp_ref[:, sl]
        for g in range(GROUP):
            hq = h * GROUP + g
            q = q_ref[:, hq * HEAD_DIM:(hq + 1) * HEAD_DIM]
            sc = jnp.where(mask_c, _dot_nt(q, kc), NEG_BIG)
            sp = jnp.where(mask_p, _dot_nt(q, kp), NEG_BIG)
            sink = sink_ref[hq]
            m = jnp.maximum(jnp.max(jnp.maximum(sc, sp), axis=-1, keepdims=True), sink)
            pc = jnp.exp(sc - m)
            pp = jnp.exp(sp - m)
            denom = (jnp.sum(pc, axis=-1, keepdims=True) + jnp.sum(pp, axis=-1, keepdims=True)
                     + jnp.exp(sink - m))
            o = _dot(pc.astype(BF16), vc) + _dot(pp.astype(BF16), vp)
            outs.append(o / denom)
    o_ref[...] = jnp.concatenate(outs, axis=1).astype(BF16)


def _attention(q, k, v, sinks, B, S):
    nb = S // WINDOW
    cur = lambda b, n: (b * nb + n, 0)
    prev = lambda b, n: (b * nb + jnp.maximum(n - 1, 0), 0)
    return pl.pallas_call(
        _attn_kernel,
        out_shape=jax.ShapeDtypeStruct((B * S, ATTN_WIDTH), BF16),
        grid=(B, nb),
        in_specs=[pl.BlockSpec(memory_space=pltpu.SMEM),
                  pl.BlockSpec((WINDOW, ATTN_WIDTH), cur),
                  pl.BlockSpec((WINDOW, KV_WIDTH), cur),
                  pl.BlockSpec((WINDOW, KV_WIDTH), prev),
                  pl.BlockSpec((WINDOW, KV_WIDTH), cur),
                  pl.BlockSpec((WINDOW, KV_WIDTH), prev)],
        out_specs=pl.BlockSpec((WINDOW, ATTN_WIDTH), cur),
        compiler_params=pltpu.CompilerParams(dimension_semantics=("arbitrary", "arbitrary"),
                                             vmem_limit_bytes=VMEM_LIMIT),
        name="swa_attention",
    )(sinks, q, k, k, v, v)


_LEVEL_HALVES = (32, 16, 8, 4, 2)


def _hgrn_chunk(q, lf, v, st_ref, h, b_ref, tri):
    row = lax.broadcasted_iota(jnp.int32, (CHUNK, HGRN_DIM), 0)
    ti = lax.broadcasted_iota(jnp.int32, (CHUNK, CHUNK), 0)
    si = lax.broadcasted_iota(jnp.int32, (CHUNK, CHUNK), 1)
    f = jnp.exp(lf)
    k = 1.0 - f
    b = _exact_lhs_dot(tri, lf)
    b_ref[...] = b
    b_last = b[CHUNK - 1:CHUNK, :]
    v_bf = v.astype(BF16)

    st = st_ref[h]
    q_in = (q * jnp.exp(b)).astype(BF16)
    o = _dot_nt(q_in, st.astype(BF16))
    k_out = (k * jnp.exp(b_last - b)).astype(BF16)
    st_ref[h] = st * jnp.exp(b_last) + _dot_tn(v_bf, k_out)

    o = o + jnp.sum(q * k, axis=-1, keepdims=True) * v
    a = jnp.zeros((CHUNK, CHUNK), F32)
    for half in (1,) + _LEVEL_HALVES[::-1]:
        upper = (row & half) != 0
        if half == 1:
            e = jnp.where(upper, f, 1.0)
        else:
            blk = 2 * half
            if half >= 4:
                pieces = [jnp.broadcast_to(b_ref[s + half - 1:s + half, :], (blk if blk >= 8 else 8, HGRN_DIM))
                          for s in range(0, CHUNK, max(blk, 8))]
            else:
                pieces = []
                for s in range(0, CHUNK, 8):
                    lo_ref = jnp.broadcast_to(b_ref[s + 1:s + 2, :], (8, HGRN_DIM))
                    hi_ref = jnp.broadcast_to(b_ref[s + 5:s + 6, :], (8, HGRN_DIM))
                    r8 = lax.broadcasted_iota(jnp.int32, (8, HGRN_DIM), 0)
                    pieces.append(jnp.where(r8 < 4, lo_ref, hi_ref))
            bref = pieces[0] if len(pieces) == 1 else jnp.concatenate(pieces, axis=0)
            e = jnp.exp(-jnp.abs(b - bref))
        qe = jnp.where(upper, q * e, 0.0).astype(BF16)
        ke = jnp.where(upper, 0.0, k * e).astype(BF16)
        same_blk = (ti // (2 * half)) == (si // (2 * half))
        a = a + jnp.where(same_blk, _dot_nt(qe, ke), 0.0)
    return o + _dot(a.astype(BF16), v_bf)


def _hgrn_kernel(hq_ref, lf_ref, hv_ref, hg_ref, nw_ref, tri_ref, o_ref, st_ref, b_ref):
    @pl.when(pl.program_id(1) == 0)
    def _():
        st_ref[...] = jnp.zeros_like(st_ref)

    tri = tri_ref[...]
    nchunks = hq_ref.shape[0] // CHUNK
    for c in range(nchunks):
        rows = slice(c * CHUNK, (c + 1) * CHUNK)
        for h in range(HGRN_HEADS):
            cols = slice(h * HGRN_DIM, (h + 1) * HGRN_DIM)
            o = _hgrn_chunk(hq_ref[rows, cols].astype(F32), lf_ref[rows, cols],
                            hv_ref[rows, cols].astype(F32), st_ref, h, b_ref, tri)
            y = _rms(o, nw_ref[...]) * hg_ref[rows, cols].astype(F32)
            o_ref[rows, cols] = y.astype(BF16)


def _hgrn(hq, lf, hv, hg, hgrn_norm_w, B, S, tb):
    nt = S // tb
    blk = pl.BlockSpec((tb, HGRN_WIDTH), lambda b, n: (b * nt + n, 0))
    tri = jnp.asarray(np.tril(np.ones((CHUNK, CHUNK), np.float32)), BF16)
    return pl.pallas_call(
        _hgrn_kernel,
        out_shape=jax.ShapeDtypeStruct((B * S, HGRN_WIDTH), BF16),
        grid=(B, nt),
        in_specs=[blk, blk, blk, blk,
                  pl.BlockSpec((1, HGRN_DIM), lambda b, n: (0, 0)),
                  pl.BlockSpec((CHUNK, CHUNK), lambda b, n: (0, 0))],
        out_specs=blk,
        scratch_shapes=[pltpu.VMEM((HGRN_HEADS, HGRN_DIM, HGRN_DIM), F32),
                        pltpu.VMEM((CHUNK, HGRN_DIM), F32)],
        compiler_params=pltpu.CompilerParams(dimension_semantics=("arbitrary", "arbitrary"),
                                             vmem_limit_bytes=VMEM_LIMIT),
        name="hgrn2",
    )(hq, lf, hv, hg, hgrn_norm_w[None, :].astype(F32), tri)


def _merge_kernel(x_ref, attn_ref, hgrn_ref, n1w_ref, wz_ref, wba_ref, wbh_ref, wout_ref, n2w_ref,
                  wr_hi_ref, wr_lo_ref, br_ref, x1_out, hn_out, route_out):
    x = x_ref[...]
    xn = _rms(x, n1w_ref[...]).astype(BF16)
    za = _sigmoid(_dot(xn, wz_ref[:, 0:D_MODEL]))
    zb = _sigmoid(_dot(xn, wz_ref[:, D_MODEL:2 * D_MODEL]))
    mixed = za * _dot(attn_ref[...], wba_ref[...]) + zb * _dot(hgrn_ref[...], wbh_ref[...])
    x1 = x + _dot(mixed.astype(BF16), wout_ref[...])
    x1_out[...] = x1
    hn = _rms(x1, n2w_ref[...])
    hn_out[...] = hn

    hn_hi = hn.astype(BF16)
    hn_lo = (hn - hn_hi.astype(F32)).astype(BF16)
    logits = (_dot(hn_hi, wr_hi_ref[...]) + _dot(hn_lo, wr_hi_ref[...]) + _dot(hn_hi, wr_lo_ref[...])
              + br_ref[...])
    lane = lax.broadcasted_iota(jnp.int32, logits.shape, 1).astype(F32)
    far = float(LANES)

    lg = jnp.where(lane < N_GROUPS, logits, NEG_BIG)
    mg = jnp.max(lg, axis=-1, keepdims=True)
    gsel = jnp.min(jnp.where(lg == mg, lane, far), axis=-1, keepdims=True)
    pgsel = 1.0 / jnp.sum(jnp.exp(lg - mg), axis=-1, keepdims=True)

    lo = N_GROUPS + EXPERTS_PER_GROUP * gsel
    le = jnp.where((lane >= lo) & (lane < lo + EXPERTS_PER_GROUP), logits, NEG_BIG)
    m1 = jnp.max(le, axis=-1, keepdims=True)
    i1 = jnp.min(jnp.where(le == m1, lane, far), axis=-1, keepdims=True)
    se = jnp.sum(jnp.exp(le - m1), axis=-1, keepdims=True)
    le2 = jnp.where(lane == i1, NEG_BIG, le)
    m2 = jnp.max(le2, axis=-1, keepdims=True)
    i2 = jnp.min(jnp.where(le2 == m2, lane, far), axis=-1, keepdims=True)
    top0 = 1.0 / se
    top1 = jnp.exp(m2 - m1) / se
    tsum = top0 + top1
    w0 = pgsel * top0 / tsum
    w1 = pgsel * top1 / tsum
    route = jnp.where(lane == 0, i1 - N_GROUPS,
                      jnp.where(lane == 1, i2 - N_GROUPS,
                                jnp.where(lane == 2, w0, jnp.where(lane == 3, w1, 0.0))))
    route_out[...] = route


def _merge(x2, attn, hgrn, norm1_w, w_z, w_ba, w_bh, w_out, norm2_w, wr_hi, wr_lo, br, tm):
    T = x2.shape[0]
    row = lambda w: pl.BlockSpec((tm, w), lambda i: (i, 0))
    full = lambda a: pl.BlockSpec(a.shape, lambda i: (0,) * a.ndim)
    ins = [x2, attn, hgrn, norm1_w[None, :], w_z, w_ba, w_bh, w_out, norm2_w[None, :], wr_hi, wr_lo, br]
    in_specs = [row(D_MODEL), row(ATTN_WIDTH), row(HGRN_WIDTH)] + [full(a) for a in ins[3:]]
    return pl.pallas_call(
        _merge_kernel,
        out_shape=[jax.ShapeDtypeStruct((T, D_MODEL), F32), jax.ShapeDtypeStruct((T, D_MODEL), F32),
                   jax.ShapeDtypeStruct((T, LANES), F32)],
        grid=(T // tm,),
        in_specs=in_specs,
        out_specs=[row(D_MODEL), row(D_MODEL), row(LANES)],
        compiler_params=pltpu.CompilerParams(dimension_semantics=("arbitrary",),
                                             vmem_limit_bytes=VMEM_LIMIT),
        name="merge_router",
    )(*ins)


def _gather_rows(idx_ref, n, src_hbm, dst_ref, sem):
    def issue(r, carry):
        pltpu.make_async_copy(src_hbm.at[pl.ds(idx_ref[0, 0, r], 1), :],
                              dst_ref.at[pl.ds(r, 1), :], sem).start()
        return carry
    lax.fori_loop(0, n, issue, 0, unroll=16)
    pltpu.make_async_copy(src_hbm.at[pl.ds(0, n), :], dst_ref.at[pl.ds(0, n), :], sem).wait()


def _expert_kernel(blk_e_ref, tok_ref, hn_hbm, wg_ref, wu_ref, wd_ref, y_ref,
                   xbuf, wg_s, wu_s, wd_s, sem):
    i = pl.program_id(0)
    _gather_rows(tok_ref, MOE_BLOCK, hn_hbm, xbuf, sem)

    changed = (i == 0) | (blk_e_ref[i] != blk_e_ref[jnp.maximum(i - 1, 0)])

    @pl.when(changed)
    def _():
        wg_s[...] = wg_ref[0].astype(BF16)
        wu_s[...] = wu_ref[0].astype(BF16)
        wd_s[...] = wd_ref[0].astype(BF16)

    xb = xbuf[...].astype(BF16)
    gate = _dot(xb, wg_s[...])
    up = _dot(xb, wu_s[...])
    hmid = (gate * _sigmoid(gate) * up).astype(BF16)
    y_ref[...] = _dot(hmid, wd_s[...])


def _experts(blk_e, row_tok, hn, w_gate, w_up, w_down):
    nblk = blk_e.shape[0]
    tok3 = row_tok.reshape(nblk, 1, MOE_BLOCK)
    wspec = lambda shape: pl.BlockSpec((1,) + shape, lambda i, be: (be[i], 0, 0))
    return pl.pallas_call(
        _expert_kernel,
        out_shape=jax.ShapeDtypeStruct((nblk * MOE_BLOCK, D_MODEL), F32),
        grid_spec=pltpu.PrefetchScalarGridSpec(
            num_scalar_prefetch=1,
            grid=(nblk,),
            in_specs=[pl.BlockSpec((1, 1, MOE_BLOCK), lambda i, be: (i, 0, 0), memory_space=pltpu.SMEM),
                      pl.BlockSpec(memory_space=pl.ANY),
                      wspec((D_MODEL, EXPERT_FF)), wspec((D_MODEL, EXPERT_FF)), wspec((EXPERT_FF, D_MODEL))],
            out_specs=pl.BlockSpec((MOE_BLOCK, D_MODEL), lambda i, be: (i, 0)),
            scratch_shapes=[pltpu.VMEM((MOE_BLOCK, D_MODEL), F32),
                            pltpu.VMEM((D_MODEL, EXPERT_FF), BF16),
                            pltpu.VMEM((D_MODEL, EXPERT_FF), BF16),
                            pltpu.VMEM((EXPERT_FF, D_MODEL), BF16),
                            pltpu.SemaphoreType.DMA]),
        compiler_params=pltpu.CompilerParams(dimension_semantics=("arbitrary",),
                                             vmem_limit_bytes=VMEM_LIMIT),
        name="moe_experts",
    )(blk_e, tok3, hn, w_gate, w_up, w_down)


def _combine_kernel(pos_ref, x1_ref, route_ref, y_hbm, o_ref, ybuf, sem):
    tk = x1_ref.shape[0]
    _gather_rows(pos_ref, 2 * tk, y_hbm, ybuf, sem)
    w0 = route_ref[:, 2:3]
    w1 = route_ref[:, 3:4]
    o_ref[...] = x1_ref[...] + w0 * ybuf[0:tk, :] + w1 * ybuf[tk:2 * tk, :]


def _combine(pos, x1, route, y, tk):
    T = x1.shape[0]
    nt = T // tk
    pos3 = pos.reshape(nt, tk, TOP_K).transpose(0, 2, 1).reshape(nt, 1, TOP_K * tk)
    return pl.pallas_call(
        _combine_kernel,
        out_shape=jax.ShapeDtypeStruct((T, D_MODEL), F32),
        grid=(nt,),
        in_specs=[pl.BlockSpec((1, 1, TOP_K * tk), lambda i: (i, 0, 0), memory_space=pltpu.SMEM),
                  pl.BlockSpec((tk, D_MODEL), lambda i: (i, 0)),
                  pl.BlockSpec((tk, LANES), lambda i: (i, 0)),
                  pl.BlockSpec(memory_space=pl.ANY)],
        out_specs=pl.BlockSpec((tk, D_MODEL), lambda i: (i, 0)),
        scratch_shapes=[pltpu.VMEM((TOP_K * tk, D_MODEL), F32), pltpu.SemaphoreType.DMA],
        compiler_params=pltpu.CompilerParams(dimension_semantics=("arbitrary",),
                                             vmem_limit_bytes=VMEM_LIMIT),
        name="moe_combine",
    )(pos3, x1, route, y)


def _routing_tables(eid, T):
    tk_total = T * TOP_K
    flat_e = eid.reshape(-1)
    onehot = (flat_e[:, None] == jnp.arange(N_EXPERTS, dtype=jnp.int32)[None, :]).astype(jnp.int32)
    csum = jnp.cumsum(onehot, axis=0)
    counts = csum[-1]
    rank = jnp.take_along_axis(csum, flat_e[:, None], axis=1)[:, 0] - 1
    padded = ((counts + MOE_BLOCK - 1) // MOE_BLOCK) * MOE_BLOCK
    pends = jnp.cumsum(padded)
    pstarts = pends - padded
    pos = pstarts[flat_e] + rank
    nblk = tk_total // MOE_BLOCK + N_EXPERTS
    flat_tok = jnp.repeat(jnp.arange(T, dtype=jnp.int32), TOP_K)
    row_tok = jnp.zeros((nblk * MOE_BLOCK,), jnp.int32).at[pos].set(flat_tok)
    blk_e = jnp.minimum(jnp.searchsorted(pends, jnp.arange(nblk, dtype=jnp.int32) * MOE_BLOCK, side='right'),
                        N_EXPERTS - 1).astype(jnp.int32)
    return pos.reshape(T, TOP_K).astype(jnp.int32), row_tok, blk_e


def _pick_tile(n, pref):
    t = pref
    while n % t:
        t //= 2
    return t


def kernel(x, positions, norm1_w, w_in, q_norm_w, k_norm_w, attn_sinks, hgrn_lower_bounds, hgrn_norm_w,
           w_branch_attn, w_branch_hgrn, w_out, norm2_w, w_router_group, b_router_group, w_router_expert,
           b_router_expert, w_gate_experts, w_up_experts, w_down_experts):
    B, S, D = x.shape
    T = B * S
    x2 = x.reshape(T, D)
    tm = _pick_tile(T, 256)

    inv_freq = ROPE_THETA ** (-jnp.arange(0, ROT_DIM, 2, dtype=F32) / ROT_DIM)
    ang = positions.astype(F32).reshape(T, 1) * inv_freq[None, :]
    cs = jnp.concatenate([jnp.cos(ang), jnp.sin(ang)], axis=-1)

    w_in0 = w_in[0]
    w_in_a = w_in0[:, :_OFF_Z].astype(BF16)
    w_z = w_in0[:, _OFF_Z:].astype(BF16)

    q, k, v, hq, lf, hv, hg = _inproj(x2, norm1_w[0], w_in_a, cs, q_norm_w[0], k_norm_w[0],
                                      hgrn_lower_bounds.astype(F32), tm)
    attn = _attention(q, k, v, attn_sinks[0].astype(F32), B, S)
    hgrn = _hgrn(hq, lf, hv, hg, hgrn_norm_w[0], B, S, _pick_tile(S, 128))

    w_r = jnp.zeros((D, LANES), F32)
    w_r = w_r.at[:, :N_GROUPS].set(w_router_group[0]).at[:, N_GROUPS:N_GROUPS + N_EXPERTS].set(w_router_expert[0])
    wr_hi = w_r.astype(BF16)
    wr_lo = (w_r - wr_hi.astype(F32)).astype(BF16)
    b_r = jnp.zeros((1, LANES), F32)
    b_r = b_r.at[0, :N_GROUPS].set(b_router_group[0]).at[0, N_GROUPS:N_GROUPS + N_EXPERTS].set(b_router_expert[0])

    x1, hn, route = _merge(x2, attn, hgrn, norm1_w[0], w_z, w_branch_attn[0].astype(BF16),
                           w_branch_hgrn[0].astype(BF16), w_out[0].astype(BF16), norm2_w[0],
                           wr_hi, wr_lo, b_r, tm)

    eid = route[:, 0:TOP_K].astype(jnp.int32)
    pos, row_tok, blk_e = _routing_tables(eid, T)
    y = _experts(blk_e, row_tok, hn, w_gate_experts[0], w_up_experts[0], w_down_experts[0])
    out = _combine(pos, x1, route, y, _pick_tile(T, 128))
    return out.reshape(B, S, D)
```

```python
import numpy as np
import jax
import jax.numpy as jnp
from jax import lax
from jax.experimental import pallas as pl
from jax.experimental.pallas import tpu as pltpu

F32 = jnp.float32
BF16 = jnp.bfloat16

D_MODEL = 1024
N_Q_HEADS = 8
N_KV_HEADS = 2
GROUP = N_Q_HEADS // N_KV_HEADS
HEAD_DIM = 64
ROT_DIM = HEAD_DIM // 4
ROT_HALF = ROT_DIM // 2
ROPE_THETA = 500000.0
WINDOW = 128
ATTN_WIDTH = N_Q_HEADS * HEAD_DIM
KV_WIDTH = N_KV_HEADS * HEAD_DIM

HGRN_HEADS = 4
HGRN_DIM = 128
HGRN_WIDTH = HGRN_HEADS * HGRN_DIM
CHUNK = 64

N_GROUPS = 4
EXPERTS_PER_GROUP = 8
N_EXPERTS = N_GROUPS * EXPERTS_PER_GROUP
TOP_K = 2
EXPERT_FF = 512
MOE_BLOCK = 128
NORM_EPS = 1e-6
ROUTER_ROWS = 40
ROUTE_ROWS = 8

LANES = 128
NEG_BIG = -1e30

_OFF_Q, _OFF_K, _OFF_V = 0, ATTN_WIDTH, ATTN_WIDTH + KV_WIDTH
_OFF_HQ = ATTN_WIDTH + 2 * KV_WIDTH
_OFF_HF = _OFF_HQ + HGRN_WIDTH
_OFF_HI = _OFF_HF + HGRN_WIDTH
_OFF_HG = _OFF_HI + HGRN_WIDTH
_OFF_Z = _OFF_HG + HGRN_WIDTH

VMEM_LIMIT = 56 * 1024 * 1024


def _split3(a):
    hi = a.astype(BF16)
    r1 = a - hi.astype(F32)
    mid = r1.astype(BF16)
    lo = (r1 - mid.astype(F32)).astype(BF16)
    return hi, mid, lo


def _dot(a, b):
    return jnp.dot(a, b, preferred_element_type=F32)


def _dot_nt(a, b):
    return lax.dot_general(a, b, (((1,), (1,)), ((), ())), preferred_element_type=F32)


def _dot_tn(a, b):
    return lax.dot_general(a, b, (((0,), (0,)), ((), ())), preferred_element_type=F32)


def _exact_lhs_dot(m01, a):
    hi, mid, lo = _split3(a)
    return _dot(m01, hi) + _dot(m01, mid) + _dot(m01, lo)


def _exact_rhs_dot(a, m01):
    hi, mid, lo = _split3(a)
    return _dot(hi, m01) + _dot(mid, m01) + _dot(lo, m01)


def _sigmoid(x):
    return 1.0 / (1.0 + jnp.exp(-x))


def _rms(x, w):
    ms = jnp.mean(x * x, axis=-1, keepdims=True)
    return x * lax.rsqrt(ms + NORM_EPS) * w


def _inproj_kernel(x_ref, n1w_ref, w_ref, cs_ref, rope_e_ref, rope_c0_ref, qw_ref, kw_ref,
                   mq_ref, mk_ref, lbp_ref,
                   q_out, k_out, v_out, hq_out, lf_out, hv_out, hg_out):
    xn = _rms(x_ref[...], n1w_ref[...]).astype(BF16)

    def proj(off, width):
        return _dot(xn, w_ref[:, off:off + width])

    tabs = _exact_rhs_dot(cs_ref[...], rope_e_ref[...])
    c_tab = tabs[:, 0:LANES] + rope_c0_ref[...]
    s1_tab = tabs[:, LANES:2 * LANES]
    s2_tab = tabs[:, 2 * LANES:3 * LANES]

    def norm_rope(t, mavg_ref, w_row, scale):
        ms = _dot((t * t).astype(BF16), mavg_ref[...])
        tn = t * lax.rsqrt(ms + NORM_EPS) * w_row
        if scale != 1.0:
            tn = tn * scale
        outs = []
        for j in range(t.shape[1] // LANES):
            c = tn[:, j * LANES:(j + 1) * LANES]
            outs.append(c * c_tab
                        + pltpu.roll(c, LANES - ROT_HALF, 1) * s1_tab
                        + pltpu.roll(c, ROT_HALF, 1) * s2_tab)
        return outs[0] if len(outs) == 1 else jnp.concatenate(outs, axis=1)

    q_out[...] = norm_rope(proj(_OFF_Q, ATTN_WIDTH), mq_ref, qw_ref[...], HEAD_DIM ** -0.5).astype(BF16)
    k_out[...] = norm_rope(proj(_OFF_K, KV_WIDTH), mk_ref, kw_ref[...], 1.0).astype(BF16)
    v_out[...] = proj(_OFF_V, KV_WIDTH).astype(BF16)

    hq = proj(_OFF_HQ, HGRN_WIDTH)
    hq_out[...] = (hq * _sigmoid(hq)).astype(BF16)
    h0 = lbp_ref[0:1, :]
    h1 = lbp_ref[1:2, :]
    hm = jnp.maximum(h0, h1)
    e0 = jnp.exp(h0 - hm)
    e1 = jnp.exp(h1 - hm)
    lb = e0 / (e0 + e1)
    fg = lb + (1.0 - lb) * _sigmoid(proj(_OFF_HF, HGRN_WIDTH))
    lf_out[...] = jnp.log(fg)
    hv_out[...] = proj(_OFF_HI, HGRN_WIDTH).astype(BF16)
    hg = proj(_OFF_HG, HGRN_WIDTH)
    hg_out[...] = (hg * _sigmoid(hg)).astype(BF16)


def _rope_constants():
    e = np.zeros((2 * ROT_HALF, 3 * LANES), np.float32)
    c0 = np.zeros((1, LANES), np.float32)
    for lane in range(LANES):
        d = lane % HEAD_DIM
        if d < ROT_HALF:
            e[d, lane] = 1.0
            e[ROT_HALF + d, LANES + lane] = -1.0
        elif d < ROT_DIM:
            e[d - ROT_HALF, lane] = 1.0
            e[ROT_HALF + d - ROT_HALF, 2 * LANES + lane] = 1.0
        else:
            c0[0, lane] = 1.0
    return jnp.asarray(e, BF16), jnp.asarray(c0, F32)


def _head_mean_matrix(width):
    idx = np.arange(width) // HEAD_DIM
    m = (idx[:, None] == idx[None, :]).astype(np.float32) / HEAD_DIM
    return jnp.asarray(m, BF16)


def _inproj(x2, norm1_w, w_in_a, cs, q_norm_w, k_norm_w, lbp, tm):
    T = x2.shape[0]
    rope_e, rope_c0 = _rope_constants()
    qw = jnp.tile(q_norm_w.astype(F32), N_Q_HEADS)[None, :]
    kw = jnp.tile(k_norm_w.astype(F32), N_KV_HEADS)[None, :]
    mq = _head_mean_matrix(ATTN_WIDTH)
    mk = _head_mean_matrix(KV_WIDTH)
    row = lambda w: pl.BlockSpec((tm, w), lambda i: (i, 0))
    full = lambda a: pl.BlockSpec(a.shape, lambda i: (0,) * a.ndim)
    ins = [x2, norm1_w[None, :], w_in_a, cs, rope_e, rope_c0, qw, kw, mq, mk, lbp]
    in_specs = [row(D_MODEL), full(ins[1]), full(w_in_a), row(cs.shape[1])] + [full(a) for a in ins[4:]]
    outs = [(ATTN_WIDTH, BF16), (KV_WIDTH, BF16), (KV_WIDTH, BF16), (HGRN_WIDTH, BF16),
            (HGRN_WIDTH, F32), (HGRN_WIDTH, BF16), (HGRN_WIDTH, BF16)]
    return pl.pallas_call(
        _inproj_kernel,
        out_shape=[jax.ShapeDtypeStruct((T, w), dt) for w, dt in outs],
        grid=(T // tm,),
        in_specs=in_specs,
        out_specs=[row(w) for w, _ in outs],
        compiler_params=pltpu.CompilerParams(dimension_semantics=("arbitrary",),
                                             vmem_limit_bytes=VMEM_LIMIT),
        name="inproj",
    )(*ins)


def _attn_kernel(sink_ref, q_ref, kc_ref, kp_ref, vc_ref, vp_ref, o_ref):
    has_prev = pl.program_id(1) > 0
    qi = lax.broadcasted_iota(jnp.int32, (WINDOW, WINDOW), 0)
    kj = lax.broadcasted_iota(jnp.int32, (WINDOW, WINDOW), 1)
    mask_c = kj <= qi
    mask_p = (kj > qi) & has_prev
    outs = []
    for h in range(N_KV_HEADS):
        sl = slice(h * HEAD_DIM, (h + 1) * HEAD_DIM)
        kc, kp, vc, vp = kc_ref[:, sl], kp_ref[:, sl], vc_ref[:, sl], vp_ref[:, sl]
        for g in range(GROUP):
            hq = h * GROUP + g
            q = q_ref[:, hq * HEAD_DIM:(hq + 1) * HEAD_DIM]
            sc = jnp.where(mask_c, _dot_nt(q, kc), NEG_BIG)
            sp = jnp.where(mask_p, _dot_nt(q, kp), NEG_BIG)
            sink = sink_ref[hq]
            m = jnp.maximum(jnp.max(jnp.maximum(sc, sp), axis=-1, keepdims=True), sink)
            pc = jnp.exp(sc - m)
            pp = jnp.exp(sp - m)
            denom = (jnp.sum(pc, axis=-1, keepdims=True) + jnp.sum(pp, axis=-1, keepdims=True)
                     + jnp.exp(sink - m))
            o = _dot(pc.astype(BF16), vc) + _dot(pp.astype(BF16), vp)
            outs.append(o / denom)
    o_ref[...] = jnp.concatenate(outs, axis=1).astype(BF16)


def _attention(q, k, v, sinks, B, S):
    nb = S // WINDOW
    cur = lambda b, n: (b * nb + n, 0)
    prev = lambda b, n: (b * nb + jnp.maximum(n - 1, 0), 0)
    return pl.pallas_call(
        _attn_kernel,
        out_shape=jax.ShapeDtypeStruct((B * S, ATTN_WIDTH), BF16),
        grid=(B, nb),
        in_specs=[pl.BlockSpec(memory_space=pltpu.SMEM),
                  pl.BlockSpec((WINDOW, ATTN_WIDTH), cur),
                  pl.BlockSpec((WINDOW, KV_WIDTH), cur),
                  pl.BlockSpec((WINDOW, KV_WIDTH), prev),
                  pl.BlockSpec((WINDOW, KV_WIDTH), cur),
                  pl.BlockSpec((WINDOW, KV_WIDTH), prev)],
        out_specs=pl.BlockSpec((WINDOW, ATTN_WIDTH), cur),
        compiler_params=pltpu.CompilerParams(dimension_semantics=("arbitrary", "arbitrary"),
                                             vmem_limit_bytes=VMEM_LIMIT),
        name="swa_attention",
    )(sinks, q, k, k, v, v)


_LEVEL_HALVES = (32, 16, 8, 4, 2)


def _hgrn_chunk(q, lf, v, st_ref, h, b_ref, tri):
    row = lax.broadcasted_iota(jnp.int32, (CHUNK, HGRN_DIM), 0)
    ti = lax.broadcasted_iota(jnp.int32, (CHUNK, CHUNK), 0)
    si = lax.broadcasted_iota(jnp.int32, (CHUNK, CHUNK), 1)
    f = jnp.exp(lf)
    k = 1.0 - f
    b = _exact_lhs_dot(tri, lf)
    b_ref[...] = b
    b_last = b[CHUNK - 1:CHUNK, :]
    v_bf = v.astype(BF16)

    st = st_ref[h]
    q_in = (q * jnp.exp(b)).astype(BF16)
    o = _dot_nt(q_in, st.astype(BF16))
    k_out = (k * jnp.exp(b_last - b)).astype(BF16)
    st_ref[h] = st * jnp.exp(b_last) + _dot_tn(v_bf, k_out)

    o = o + jnp.sum(q * k, axis=-1, keepdims=True) * v
    a = jnp.zeros((CHUNK, CHUNK), F32)
    for half in (1,) + _LEVEL_HALVES[::-1]:
        upper = (row & half) != 0
        if half == 1:
            e = jnp.where(upper, f, 1.0)
        else:
            blk = 2 * half
            if half >= 4:
                pieces = [jnp.broadcast_to(b_ref[s + half - 1:s + half, :], (blk if blk >= 8 else 8, HGRN_DIM))
                          for s in range(0, CHUNK, max(blk, 8))]
            else:
                pieces = []
                for s in range(0, CHUNK, 8):
                    lo_ref = jnp.broadcast_to(b_ref[s + 1:s + 2, :], (8, HGRN_DIM))
                    hi_ref = jnp.broadcast_to(b_ref[s + 5:s + 6, :], (8, HGRN_DIM))
                    r8 = lax.broadcasted_iota(jnp.int32, (8, HGRN_DIM), 0)
                    pieces.append(jnp.where(r8 < 4, lo_ref, hi_ref))
            bref = pieces[0] if len(pieces) == 1 else jnp.concatenate(pieces, axis=0)
            e = jnp.exp(-jnp.abs(b - bref))
        qe = jnp.where(upper, q * e, 0.0).astype(BF16)
        ke = jnp.where(upper, 0.0, k * e).astype(BF16)
        same_blk = (ti // (2 * half)) == (si // (2 * half))
        a = a + jnp.where(same_blk, _dot_nt(qe, ke), 0.0)
    return o + _dot(a.astype(BF16), v_bf)


def _hgrn_kernel(hq_ref, lf_ref, hv_ref, hg_ref, nw_ref, tri_ref, o_ref, st_ref, b_ref):
    @pl.when(pl.program_id(1) == 0)
    def _():
        st_ref[...] = jnp.zeros_like(st_ref)

    tri = tri_ref[...]
    nchunks = hq_ref.shape[0] // CHUNK
    for c in range(nchunks):
        rows = slice(c * CHUNK, (c + 1) * CHUNK)
        for h in range(HGRN_HEADS):
            cols = slice(h * HGRN_DIM, (h + 1) * HGRN_DIM)
            o = _hgrn_chunk(hq_ref[rows, cols].astype(F32), lf_ref[rows, cols],
                            hv_ref[rows, cols].astype(F32), st_ref, h, b_ref, tri)
            y = _rms(o, nw_ref[...]) * hg_ref[rows, cols].astype(F32)
            o_ref[rows, cols] = y.astype(BF16)


def _hgrn(hq, lf, hv, hg, hgrn_norm_w, B, S, tb):
    nt = S // tb
    blk = pl.BlockSpec((tb, HGRN_WIDTH), lambda b, n: (b * nt + n, 0))
    tri = jnp.asarray(np.tril(np.ones((CHUNK, CHUNK), np.float32)), BF16)
    return pl.pallas_call(
        _hgrn_kernel,
        out_shape=jax.ShapeDtypeStruct((B * S, HGRN_WIDTH), BF16),
        grid=(B, nt),
        in_specs=[blk, blk, blk, blk,
                  pl.BlockSpec((1, HGRN_DIM), lambda b, n: (0, 0)),
                  pl.BlockSpec((CHUNK, CHUNK), lambda b, n: (0, 0))],
        out_specs=blk,
        scratch_shapes=[pltpu.VMEM((HGRN_HEADS, HGRN_DIM, HGRN_DIM), F32),
                        pltpu.VMEM((CHUNK, HGRN_DIM), F32)],
        compiler_params=pltpu.CompilerParams(dimension_semantics=("arbitrary", "arbitrary"),
                                             vmem_limit_bytes=VMEM_LIMIT),
        name="hgrn2",
    )(hq, lf, hv, hg, hgrn_norm_w[None, :].astype(F32), tri)


def _merge_kernel(x_ref, attn_ref, hgrn_ref, n1w_ref, wz_ref, wba_ref, wbh_ref, wout_ref, n2w_ref,
                  wr_hi_ref, wr_lo_ref, br_ref, utri_ref, x1_out, hn_out, route_out, count_out, run_ref):
    @pl.when(pl.program_id(0) == 0)
    def _():
        run_ref[...] = jnp.zeros_like(run_ref)

    x = x_ref[...]
    xn = _rms(x, n1w_ref[...]).astype(BF16)
    za = _sigmoid(_dot(xn, wz_ref[:, 0:D_MODEL]))
    zb = _sigmoid(_dot(xn, wz_ref[:, D_MODEL:2 * D_MODEL]))
    mixed = za * _dot(attn_ref[...], wba_ref[...]) + zb * _dot(hgrn_ref[...], wbh_ref[...])
    x1 = x + _dot(mixed.astype(BF16), wout_ref[...])
    x1_out[...] = x1
    hn = _rms(x1, n2w_ref[...])
    hn_out[...] = hn

    hn_hi = hn.astype(BF16)
    hn_lo = (hn - hn_hi.astype(F32)).astype(BF16)
    logits = (_dot_nt(wr_hi_ref[...], hn_hi) + _dot_nt(wr_hi_ref[...], hn_lo) + _dot_nt(wr_lo_ref[...], hn_hi)
              + br_ref[...])
    r = lax.broadcasted_iota(jnp.int32, logits.shape, 0).astype(F32)
    far = float(ROUTER_ROWS)
    cmax = lambda a: jnp.max(a, axis=0, keepdims=True)
    cmin = lambda a: jnp.min(a, axis=0, keepdims=True)
    csum = lambda a: jnp.sum(a, axis=0, keepdims=True)

    lg = jnp.where(r < N_GROUPS, logits, NEG_BIG)
    mg = cmax(lg)
    gsel = cmin(jnp.where(lg == mg, r, far))
    pgsel = 1.0 / csum(jnp.exp(lg - mg))

    lo = N_GROUPS + EXPERTS_PER_GROUP * gsel
    le = jnp.where((r >= lo) & (r < lo + EXPERTS_PER_GROUP), logits, NEG_BIG)
    m1 = cmax(le)
    i1 = cmin(jnp.where(le == m1, r, far))
    se = csum(jnp.exp(le - m1))
    le2 = jnp.where(r == i1, NEG_BIG, le)
    m2 = cmax(le2)
    i2 = cmin(jnp.where(le2 == m2, r, far))
    top0 = 1.0 / se
    top1 = jnp.exp(m2 - m1) / se
    tsum = top0 + top1
    w0 = pgsel * top0 / tsum
    w1 = pgsel * top1 / tsum

    sel1 = r == i1
    sel2 = r == i2
    onehot = (sel1 | sel2).astype(BF16)
    before = _dot(onehot, utri_ref[...]) + run_ref[...]
    r0 = csum(jnp.where(sel1, before, 0.0))
    r1 = csum(jnp.where(sel2, before, 0.0))
    run_new = run_ref[...] + _dot(onehot, jnp.ones(utri_ref.shape, BF16))
    run_ref[...] = run_new
    count_out[...] = run_new

    row8 = lax.broadcasted_iota(jnp.int32, route_out.shape, 0)
    vals = (i1 - N_GROUPS, i2 - N_GROUPS, w0, w1, r0, r1)
    route = jnp.zeros(route_out.shape, F32)
    for j, val in enumerate(vals):
        route = jnp.where(row8 == j, val, route)
    route_out[...] = route


def _merge(x2, attn, hgrn, norm1_w, w_z, w_ba, w_bh, w_out, norm2_w, wr_hi, wr_lo, br, tm):
    T = x2.shape[0]
    row = lambda w: pl.BlockSpec((tm, w), lambda i: (i, 0))
    full = lambda a: pl.BlockSpec(a.shape, lambda i: (0,) * a.ndim)
    utri = jnp.asarray(np.triu(np.ones((tm, tm), np.float32), 1), BF16)
    ins = [x2, attn, hgrn, norm1_w[None, :], w_z, w_ba, w_bh, w_out, norm2_w[None, :], wr_hi, wr_lo, br, utri]
    in_specs = [row(D_MODEL), row(ATTN_WIDTH), row(HGRN_WIDTH)] + [full(a) for a in ins[3:]]
    return pl.pallas_call(
        _merge_kernel,
        out_shape=[jax.ShapeDtypeStruct((T, D_MODEL), F32), jax.ShapeDtypeStruct((T, D_MODEL), F32),
                   jax.ShapeDtypeStruct((ROUTE_ROWS, T), F32), jax.ShapeDtypeStruct((ROUTER_ROWS, tm), F32)],
        grid=(T // tm,),
        in_specs=in_specs,
        out_specs=[row(D_MODEL), row(D_MODEL), pl.BlockSpec((ROUTE_ROWS, tm), lambda i: (0, i)),
                   pl.BlockSpec((ROUTER_ROWS, tm), lambda i: (0, 0))],
        scratch_shapes=[pltpu.VMEM((ROUTER_ROWS, tm), F32)],
        compiler_params=pltpu.CompilerParams(dimension_semantics=("arbitrary",),
                                             vmem_limit_bytes=VMEM_LIMIT),
        name="merge_router",
    )(*ins)


def _gather_rows(idx_ref, n, src_hbm, dst_ref, sem):
    def issue(r, carry):
        pltpu.make_async_copy(src_hbm.at[pl.ds(idx_ref[0, 0, r], 1), :],
                              dst_ref.at[pl.ds(r, 1), :], sem).start()
        return carry
    lax.fori_loop(0, n, issue, 0, unroll=16)
    pltpu.make_async_copy(src_hbm.at[pl.ds(0, n), :], dst_ref.at[pl.ds(0, n), :], sem).wait()


def _dispatch_kernel(pos_ref, hn_ref, xs_hbm, sem):
    tk = hn_ref.shape[0]

    for k in range(TOP_K):
        def issue(r, carry):
            pltpu.make_async_copy(hn_ref.at[pl.ds(r, 1), :],
                                  xs_hbm.at[pl.ds(pos_ref[0, 0, k * tk + r], 1), :], sem).start()
            return carry
        lax.fori_loop(0, tk, issue, 0, unroll=16)
    for k in range(TOP_K):
        pltpu.make_async_copy(hn_ref, xs_hbm.at[pl.ds(0, tk), :], sem).wait()


def _tile_positions(pos, tk):
    nt = pos.shape[0] // tk
    return pos.reshape(nt, tk, TOP_K).transpose(0, 2, 1).reshape(nt, 1, TOP_K * tk)


def _dispatch(pos, hn, tk):
    T = hn.shape[0]
    return pl.pallas_call(
        _dispatch_kernel,
        out_shape=jax.ShapeDtypeStruct((TOP_K * T, D_MODEL), F32),
        grid=(T // tk,),
        in_specs=[pl.BlockSpec((1, 1, TOP_K * tk), lambda i: (i, 0, 0), memory_space=pltpu.SMEM),
                  pl.BlockSpec((tk, D_MODEL), lambda i: (i, 0))],
        out_specs=pl.BlockSpec(memory_space=pl.ANY),
        scratch_shapes=[pltpu.SemaphoreType.DMA],
        compiler_params=pltpu.CompilerParams(dimension_semantics=("arbitrary",),
                                             vmem_limit_bytes=VMEM_LIMIT),
        name="moe_dispatch",
    )(_tile_positions(pos, tk), hn)


def _expert_kernel(tile_ref, exp_ref, lo_ref, hi_ref, x_ref, wg_ref, wu_ref, wd_ref, y_ref,
                   wg_s, wu_s, wd_s):
    w = pl.program_id(0)
    lo = lo_ref[w]
    hi = hi_ref[w]
    changed = (w == 0) | (exp_ref[w] != exp_ref[jnp.maximum(w - 1, 0)])

    @pl.when(changed)
    def _():
        wg_s[...] = wg_ref[0].astype(BF16)
        wu_s[...] = wu_ref[0].astype(BF16)
        wd_s[...] = wd_ref[0].astype(BF16)

    @pl.when(hi > lo)
    def _():
        xb = x_ref[...].astype(BF16)
        gate = _dot(xb, wg_s[...])
        up = _dot(xb, wu_s[...])
        hmid = (gate * _sigmoid(gate) * up).astype(BF16)
        y = _dot(hmid, wd_s[...])
        rowi = lax.broadcasted_iota(jnp.int32, y.shape, 0)
        mine = (rowi >= lo) & (rowi < hi)

        @pl.when(lo == 0)
        def _():
            y_ref[...] = jnp.where(mine, y, 0.0)

        @pl.when(lo > 0)
        def _():
            y_ref[...] = jnp.where(mine, y, y_ref[...])


def _experts(items, xs, w_gate, w_up, w_down):
    tile, exp, lo, hi = items
    wspec = lambda shape: pl.BlockSpec((1,) + shape, lambda w, t, e, l, h: (e[w], 0, 0))
    xspec = pl.BlockSpec((MOE_BLOCK, D_MODEL), lambda w, t, e, l, h: (t[w], 0))
    return pl.pallas_call(
        _expert_kernel,
        out_shape=jax.ShapeDtypeStruct(xs.shape, F32),
        grid_spec=pltpu.PrefetchScalarGridSpec(
            num_scalar_prefetch=4,
            grid=(tile.shape[0],),
            in_specs=[xspec, wspec((D_MODEL, EXPERT_FF)), wspec((D_MODEL, EXPERT_FF)),
                      wspec((EXPERT_FF, D_MODEL))],
            out_specs=xspec,
            scratch_shapes=[pltpu.VMEM((D_MODEL, EXPERT_FF), BF16),
                            pltpu.VMEM((D_MODEL, EXPERT_FF), BF16),
                            pltpu.VMEM((EXPERT_FF, D_MODEL), BF16)]),
        compiler_params=pltpu.CompilerParams(dimension_semantics=("arbitrary",),
                                             vmem_limit_bytes=VMEM_LIMIT),
        name="moe_experts",
    )(tile, exp, lo, hi, xs, w_gate, w_up, w_down)


def _combine_kernel(pos_ref, x1_ref, w_ref, y_hbm, o_ref, ybuf, sem):
    tk = x1_ref.shape[0]
    _gather_rows(pos_ref, TOP_K * tk, y_hbm, ybuf, sem)
    o_ref[...] = x1_ref[...] + w_ref[:, 0:1] * ybuf[0:tk, :] + w_ref[:, 1:2] * ybuf[tk:2 * tk, :]


def _combine(pos, x1, gate_w, y, tk):
    T = x1.shape[0]
    return pl.pallas_call(
        _combine_kernel,
        out_shape=jax.ShapeDtypeStruct((T, D_MODEL), F32),
        grid=(T // tk,),
        in_specs=[pl.BlockSpec((1, 1, TOP_K * tk), lambda i: (i, 0, 0), memory_space=pltpu.SMEM),
                  pl.BlockSpec((tk, D_MODEL), lambda i: (i, 0)),
                  pl.BlockSpec((tk, TOP_K), lambda i: (i, 0)),
                  pl.BlockSpec(memory_space=pl.ANY)],
        out_specs=pl.BlockSpec((tk, D_MODEL), lambda i: (i, 0)),
        scratch_shapes=[pltpu.VMEM((TOP_K * tk, D_MODEL), F32), pltpu.SemaphoreType.DMA],
        compiler_params=pltpu.CompilerParams(dimension_semantics=("arbitrary",),
                                             vmem_limit_bytes=VMEM_LIMIT),
        name="moe_combine",
    )(_tile_positions(pos, tk), x1, gate_w, y)


def _routing_tables(route, counts, T):
    e = route[0:TOP_K].astype(jnp.int32)
    rank = route[4:4 + TOP_K].astype(jnp.int32)
    counts = counts.astype(jnp.int32)
    starts = jnp.cumsum(counts) - counts
    ids = jnp.arange(N_EXPERTS, dtype=jnp.int32)
    pos = rank + jnp.sum(jnp.where(e[:, :, None] == ids, starts, 0), axis=-1)
    n_rows = TOP_K * T
    cuts = jnp.sort(jnp.concatenate([jnp.arange(0, n_rows, MOE_BLOCK, dtype=jnp.int32), starts[1:]]))
    ends = jnp.concatenate([cuts[1:], jnp.full((1,), n_rows, jnp.int32)])
    tile = jnp.minimum(cuts // MOE_BLOCK, n_rows // MOE_BLOCK - 1)
    exp = jnp.clip(jnp.sum(starts[None, :] <= cuts[:, None], axis=1) - 1, 0, N_EXPERTS - 1).astype(jnp.int32)
    lo = cuts - tile * MOE_BLOCK
    hi = ends - tile * MOE_BLOCK
    return pos.T, (tile, exp, lo, hi)


def _pick_tile(n, pref):
    t = pref
    while n % t:
        t //= 2
    return t


def kernel(x, positions, norm1_w, w_in, q_norm_w, k_norm_w, attn_sinks, hgrn_lower_bounds, hgrn_norm_w,
           w_branch_attn, w_branch_hgrn, w_out, norm2_w, w_router_group, b_router_group, w_router_expert,
           b_router_expert, w_gate_experts, w_up_experts, w_down_experts):
    B, S, D = x.shape
    T = B * S
    x2 = x.reshape(T, D)
    tm = _pick_tile(T, 256)

    inv_freq = ROPE_THETA ** (-jnp.arange(0, ROT_DIM, 2, dtype=F32) / ROT_DIM)
    ang = positions.astype(F32).reshape(T, 1) * inv_freq[None, :]
    cs = jnp.concatenate([jnp.cos(ang), jnp.sin(ang)], axis=-1)

    w_in0 = w_in[0]
    w_in_a = w_in0[:, :_OFF_Z].astype(BF16)
    w_z = w_in0[:, _OFF_Z:].astype(BF16)

    q, k, v, hq, lf, hv, hg = _inproj(x2, norm1_w[0], w_in_a, cs, q_norm_w[0], k_norm_w[0],
                                      hgrn_lower_bounds.astype(F32), tm)
    attn = _attention(q, k, v, attn_sinks[0].astype(F32), B, S)
    hgrn = _hgrn(hq, lf, hv, hg, hgrn_norm_w[0], B, S, _pick_tile(S, 128))

    pad = ROUTER_ROWS - N_GROUPS - N_EXPERTS
    w_r = jnp.concatenate([w_router_group[0].T, w_router_expert[0].T, jnp.zeros((pad, D), F32)], axis=0)
    wr_hi = w_r.astype(BF16)
    wr_lo = (w_r - wr_hi.astype(F32)).astype(BF16)
    b_r = jnp.concatenate([b_router_group[0], b_router_expert[0], jnp.zeros((pad,), F32)]).astype(F32)
    b_r = jnp.broadcast_to(b_r[:, None], (ROUTER_ROWS, tm))

    x1, hn, route, counts = _merge(x2, attn, hgrn, norm1_w[0], w_z, w_branch_attn[0].astype(BF16),
                                   w_branch_hgrn[0].astype(BF16), w_out[0].astype(BF16), norm2_w[0],
                                   wr_hi, wr_lo, b_r, tm)

    pos, items = _routing_tables(route, counts[N_GROUPS:N_GROUPS + N_EXPERTS, 0], T)
    tk = _pick_tile(T, 128)
    xs = _dispatch(pos, hn, tk)
    y = _experts(items, xs, w_gate_experts[0], w_up_experts[0], w_down_experts[0])
    out = _combine(pos, x1, route[2:2 + TOP_K].T, y, tk)
    return out.reshape(B, S, D)
```

```python
import numpy as np
import jax
import jax.numpy as jnp
from jax import lax
from jax.experimental import pallas as pl
from jax.experimental.pallas import tpu as pltpu

F32 = jnp.float32
BF16 = jnp.bfloat16

D_MODEL = 1024
N_Q_HEADS = 8
N_KV_HEADS = 2
GROUP = N_Q_HEADS // N_KV_HEADS
HEAD_DIM = 64
ROT_DIM = HEAD_DIM // 4
ROT_HALF = ROT_DIM // 2
ROPE_THETA = 500000.0
WINDOW = 128
ATTN_WIDTH = N_Q_HEADS * HEAD_DIM
KV_WIDTH = N_KV_HEADS * HEAD_DIM

HGRN_HEADS = 4
HGRN_DIM = 128
HGRN_WIDTH = HGRN_HEADS * HGRN_DIM
CHUNK = 64

N_GROUPS = 4
EXPERTS_PER_GROUP = 8
N_EXPERTS = N_GROUPS * EXPERTS_PER_GROUP
TOP_K = 2
EXPERT_FF = 512
MOE_BLOCK = 128
NORM_EPS = 1e-6
ROUTER_ROWS = 40
ROUTE_ROWS = 8

LANES = 128
NEG_BIG = -1e30
LOG2_E = 1.4426950408889634

_OFF_Q, _OFF_K, _OFF_V = 0, ATTN_WIDTH, ATTN_WIDTH + KV_WIDTH
_OFF_HQ = ATTN_WIDTH + 2 * KV_WIDTH
_OFF_HF = _OFF_HQ + HGRN_WIDTH
_OFF_HI = _OFF_HF + HGRN_WIDTH
_OFF_HG = _OFF_HI + HGRN_WIDTH
_OFF_Z = _OFF_HG + HGRN_WIDTH

VMEM_LIMIT = 56 * 1024 * 1024


def _split3(a):
    hi = a.astype(BF16)
    r1 = a - hi.astype(F32)
    mid = r1.astype(BF16)
    lo = (r1 - mid.astype(F32)).astype(BF16)
    return hi, mid, lo


def _dot(a, b):
    return jnp.dot(a, b, preferred_element_type=F32)


def _dot_nt(a, b):
    return lax.dot_general(a, b, (((1,), (1,)), ((), ())), preferred_element_type=F32)


def _dot_tn(a, b):
    return lax.dot_general(a, b, (((0,), (0,)), ((), ())), preferred_element_type=F32)


def _exact_lhs_dot(m01, a):
    hi, mid, lo = _split3(a)
    return _dot(m01, hi) + _dot(m01, mid) + _dot(m01, lo)


def _exact_rhs_dot(a, m01):
    hi, mid, lo = _split3(a)
    return _dot(hi, m01) + _dot(mid, m01) + _dot(lo, m01)


def _sigmoid(x):
    return 1.0 / (1.0 + jnp.exp(-x))


def _rms(x, w):
    ms = jnp.mean(x * x, axis=-1, keepdims=True)
    return x * lax.rsqrt(ms + NORM_EPS) * w


def _inproj_kernel(x_ref, n1w_ref, w_ref, cs_ref, rope_e_ref, rope_c0_ref, qw_ref, kw_ref,
                   mq_ref, mk_ref, lbp_ref, dup_ref,
                   q_out, k_out, v_out, hq_out, lf_out, hv_out, hg_out):
    xn = _rms(x_ref[...], n1w_ref[...]).astype(BF16)

    def proj(off, width):
        return _dot(xn, w_ref[:, off:off + width])

    tabs = _exact_rhs_dot(cs_ref[...], rope_e_ref[...])
    c_tab = tabs[:, 0:LANES] + rope_c0_ref[...]
    s1_tab = tabs[:, LANES:2 * LANES]
    s2_tab = tabs[:, 2 * LANES:3 * LANES]

    def norm_rope(t, mavg_ref, w_row, scale):
        ms = _dot((t * t).astype(BF16), mavg_ref[...])
        tn = t * lax.rsqrt(ms + NORM_EPS) * w_row
        if scale != 1.0:
            tn = tn * scale
        outs = []
        for j in range(t.shape[1] // LANES):
            c = tn[:, j * LANES:(j + 1) * LANES]
            outs.append(c * c_tab
                        + pltpu.roll(c, LANES - ROT_HALF, 1) * s1_tab
                        + pltpu.roll(c, ROT_HALF, 1) * s2_tab)
        return outs[0] if len(outs) == 1 else jnp.concatenate(outs, axis=1)

    q_out[...] = norm_rope(proj(_OFF_Q, ATTN_WIDTH), mq_ref, qw_ref[...], HEAD_DIM ** -0.5).astype(BF16)
    k_rot = norm_rope(proj(_OFF_K, KV_WIDTH), mk_ref, kw_ref[...], 1.0).astype(BF16)
    k_out[...] = _dot(k_rot, dup_ref[...]).astype(BF16)
    v_out[...] = _dot(proj(_OFF_V, KV_WIDTH).astype(BF16), dup_ref[...]).astype(BF16)

    hq = proj(_OFF_HQ, HGRN_WIDTH)
    hq_out[...] = (hq * _sigmoid(hq)).astype(BF16)
    h0 = lbp_ref[0:1, :]
    h1 = lbp_ref[1:2, :]
    hm = jnp.maximum(h0, h1)
    e0 = jnp.exp(h0 - hm)
    e1 = jnp.exp(h1 - hm)
    lb = e0 / (e0 + e1)
    fg = lb + (1.0 - lb) * _sigmoid(proj(_OFF_HF, HGRN_WIDTH))
    lf_out[...] = jnp.log(fg) * LOG2_E
    hv_out[...] = proj(_OFF_HI, HGRN_WIDTH).astype(BF16)
    hg = proj(_OFF_HG, HGRN_WIDTH)
    hg_out[...] = (hg * _sigmoid(hg)).astype(BF16)


def _rope_constants():
    e = np.zeros((2 * ROT_HALF, 3 * LANES), np.float32)
    c0 = np.zeros((1, LANES), np.float32)
    for lane in range(LANES):
        d = lane % HEAD_DIM
        if d < ROT_HALF:
            e[d, lane] = 1.0
            e[ROT_HALF + d, LANES + lane] = -1.0
        elif d < ROT_DIM:
            e[d - ROT_HALF, lane] = 1.0
            e[ROT_HALF + d - ROT_HALF, 2 * LANES + lane] = 1.0
        else:
            c0[0, lane] = 1.0
    return jnp.asarray(e, BF16), jnp.asarray(c0, F32)


def _head_mean_matrix(width):
    idx = np.arange(width) // HEAD_DIM
    m = (idx[:, None] == idx[None, :]).astype(np.float32) / HEAD_DIM
    return jnp.asarray(m, BF16)


def _inproj(x2, norm1_w, w_in_a, cs, q_norm_w, k_norm_w, lbp, tm):
    T = x2.shape[0]
    rope_e, rope_c0 = _rope_constants()
    qw = jnp.tile(q_norm_w.astype(F32), N_Q_HEADS)[None, :]
    kw = jnp.tile(k_norm_w.astype(F32), N_KV_HEADS)[None, :]
    mq = _head_mean_matrix(ATTN_WIDTH)
    mk = _head_mean_matrix(KV_WIDTH)
    row = lambda w: pl.BlockSpec((tm, w), lambda i: (i, 0))
    full = lambda a: pl.BlockSpec(a.shape, lambda i: (0,) * a.ndim)
    src = HEAD_DIM * (np.arange(2 * KV_WIDTH) // LANES) + np.arange(2 * KV_WIDTH) % HEAD_DIM
    dup = jnp.asarray(np.arange(KV_WIDTH)[:, None] == src[None, :], BF16)
    ins = [x2, norm1_w[None, :], w_in_a, cs, rope_e, rope_c0, qw, kw, mq, mk, lbp, dup]
    in_specs = [row(D_MODEL), full(ins[1]), full(w_in_a), row(cs.shape[1])] + [full(a) for a in ins[4:]]
    outs = [(ATTN_WIDTH, BF16), (2 * KV_WIDTH, BF16), (2 * KV_WIDTH, BF16), (HGRN_WIDTH, BF16),
            (HGRN_WIDTH, F32), (HGRN_WIDTH, BF16), (HGRN_WIDTH, BF16)]
    return pl.pallas_call(
        _inproj_kernel,
        out_shape=[jax.ShapeDtypeStruct((T, w), dt) for w, dt in outs],
        grid=(T // tm,),
        in_specs=in_specs,
        out_specs=[row(w) for w, _ in outs],
        compiler_params=pltpu.CompilerParams(dimension_semantics=("arbitrary",),
                                             vmem_limit_bytes=VMEM_LIMIT),
        name="inproj",
    )(*ins)


def _attn_kernel(sink_ref, q_ref, kc_ref, kp_ref, vc_ref, vp_ref, half_ref, o_ref):
    has_prev = pl.program_id(1) > 0
    qi = lax.broadcasted_iota(jnp.int32, (WINDOW, 2 * WINDOW), 0)
    kj = lax.broadcasted_iota(jnp.int32, (WINDOW, 2 * WINDOW), 1)
    valid = ((kj < WINDOW) & (kj > qi) & has_prev) | ((kj >= WINDOW) & (kj - WINDOW <= qi))
    left = lax.broadcasted_iota(jnp.int32, (WINDOW, LANES), 1) < HEAD_DIM
    half = (half_ref[0], half_ref[1])
    outs = []
    for h in range(N_KV_HEADS):
        cols = slice(h * LANES, (h + 1) * LANES)
        kcat = jnp.concatenate([kp_ref[:, cols], kc_ref[:, cols]], axis=0)
        vcat = jnp.concatenate([vp_ref[:, cols], vc_ref[:, cols]], axis=0)
        ks = [kcat * hm for hm in half]
        rhs = [jnp.concatenate([vcat * hm, hm], axis=1) for hm in half]
        for pr in range(GROUP // 2):
            pair = h * (GROUP // 2) + pr
            qp = q_ref[:, pair * LANES:(pair + 1) * LANES]
            acc = None
            m_side = []
            for side in range(2):
                s = jnp.where(valid, _dot_nt(qp, ks[side]), NEG_BIG)
                sink = sink_ref[2 * pair + side]
                m = jnp.maximum(jnp.max(s, axis=-1, keepdims=True), sink)
                p = jnp.exp(s - m).astype(BF16)
                d = _dot(p, rhs[side])
                acc = d if acc is None else acc + d
                m_side.append(jnp.exp(sink - m))
            den = acc[:, LANES:2 * LANES] + jnp.where(left, m_side[0], m_side[1])
            outs.append(acc[:, 0:LANES] / den)
    o_ref[...] = jnp.concatenate(outs, axis=1).astype(BF16)


def _attention(q, k, v, sinks, B, S):
    nb = S // WINDOW
    cur = lambda b, n: (b * nb + n, 0)
    prev = lambda b, n: (b * nb + jnp.maximum(n - 1, 0), 0)
    lane_left = np.arange(LANES) < HEAD_DIM
    half = jnp.asarray(np.broadcast_to(np.stack([lane_left, ~lane_left])[:, None, :],
                                       (2, 2 * WINDOW, LANES)), BF16)
    return pl.pallas_call(
        _attn_kernel,
        out_shape=jax.ShapeDtypeStruct((B * S, ATTN_WIDTH), BF16),
        grid=(B, nb),
        in_specs=[pl.BlockSpec(memory_space=pltpu.SMEM),
                  pl.BlockSpec((WINDOW, ATTN_WIDTH), cur),
                  pl.BlockSpec((WINDOW, 2 * KV_WIDTH), cur),
                  pl.BlockSpec((WINDOW, 2 * KV_WIDTH), prev),
                  pl.BlockSpec((WINDOW, 2 * KV_WIDTH), cur),
                  pl.BlockSpec((WINDOW, 2 * KV_WIDTH), prev),
                  pl.BlockSpec(half.shape, lambda b, n: (0, 0, 0))],
        out_specs=pl.BlockSpec((WINDOW, ATTN_WIDTH), cur),
        compiler_params=pltpu.CompilerParams(dimension_semantics=("arbitrary", "arbitrary"),
                                             vmem_limit_bytes=VMEM_LIMIT),
        name="swa_attention",
    )(sinks, q, k, k, v, v, half)


_LEVEL_HALVES = (1, 2, 4, 8, 16, 32)


def _hgrn_level_masks():
    t = np.arange(CHUNK)[:, None]
    s = np.arange(CHUNK)[None, :]
    masks = [((t // (2 * h)) == (s // (2 * h))) & ((t & h) != 0) & ((s & h) == 0) for h in _LEVEL_HALVES]
    return jnp.asarray(np.stack(masks), F32)


def _level_reference(b_ref, slot, half):
    if half >= 4:
        span = max(2 * half, 8)
        pieces = [jnp.broadcast_to(b_ref[slot, s + half - 1:s + half, :], (span, HGRN_DIM))
                  for s in range(0, CHUNK, span)]
    else:
        r8 = lax.broadcasted_iota(jnp.int32, (8, HGRN_DIM), 0)
        pieces = [jnp.where(r8 < 4,
                            jnp.broadcast_to(b_ref[slot, s + 1:s + 2, :], (8, HGRN_DIM)),
                            jnp.broadcast_to(b_ref[slot, s + 5:s + 6, :], (8, HGRN_DIM)))
                  for s in range(0, CHUNK, 8)]
    return pieces[0] if len(pieces) == 1 else jnp.concatenate(pieces, axis=0)


def _hgrn_kernel(hq_ref, lf_ref, hv_ref, hg_ref, nw_ref, tri_ref, lm_ref, o_ref, st_ref, b_ref):
    @pl.when(pl.program_id(1) == 0)
    def _():
        st_ref[...] = jnp.zeros_like(st_ref)

    tri2 = tri_ref[...]
    odd = (lax.broadcasted_iota(jnp.int32, (CHUNK, HGRN_DIM), 0) & 1) != 0
    masks = [lm_ref[li] != 0.0 for li in range(len(_LEVEL_HALVES))]
    units = [(c, h) for c in range(hq_ref.shape[0] // CHUNK) for h in range(HGRN_HEADS)]
    sl = lambda c, h: (slice(c * CHUNK, (c + 1) * CHUNK), slice(h * HGRN_DIM, (h + 1) * HGRN_DIM))

    bs = []
    for u, (c, h) in enumerate(units):
        lf2 = lf_ref[sl(c, h)]
        hi = lf2.astype(BF16)
        lo = (lf2 - hi.astype(F32)).astype(BF16)
        b = _dot(tri2, jnp.concatenate([hi, lo], axis=0))
        b_ref[u] = b
        bs.append(b)

    outs, qs, ks, fs = [], [], [], []
    for u, (c, h) in enumerate(units):
        b = bs[u]
        q = hq_ref[sl(c, h)].astype(F32)
        v_bf = hv_ref[sl(c, h)]
        f = jnp.exp2(lf_ref[sl(c, h)])
        k = 1.0 - f
        b_last = b[CHUNK - 1:CHUNK, :]
        st = st_ref[h]
        o = _dot_nt((q * jnp.exp2(b)).astype(BF16), st.astype(BF16))
        k_out = (k * jnp.exp2(b_last - b)).astype(BF16)
        st_ref[h] = st * jnp.exp2(b_last) + _dot_tn(v_bf, k_out)
        outs.append(o + jnp.sum(q * k, axis=-1, keepdims=True) * v_bf.astype(F32))
        qs.append(q)
        ks.append(k)
        fs.append(f)

    accs = [jnp.zeros((CHUNK, CHUNK), F32) for _ in units]
    for li, half in enumerate(_LEVEL_HALVES):
        for u in range(len(units)):
            if half == 1:
                e = jnp.where(odd, fs[u], 1.0)
            else:
                e = jnp.exp2(-jnp.abs(bs[u] - _level_reference(b_ref, u, half)))
            accs[u] = jnp.where(masks[li], _dot_nt((qs[u] * e).astype(BF16), (ks[u] * e).astype(BF16)), accs[u])

    for u, (c, h) in enumerate(units):
        o = outs[u] + _dot(accs[u].astype(BF16), hv_ref[sl(c, h)])
        y = _rms(o, nw_ref[...]) * hg_ref[sl(c, h)].astype(F32)
        o_ref[sl(c, h)] = y.astype(BF16)


def _hgrn(hq, lf, hv, hg, hgrn_norm_w, B, S, tb):
    nt = S // tb
    blk = pl.BlockSpec((tb, HGRN_WIDTH), lambda b, n: (b * nt + n, 0))
    tri = np.tril(np.ones((CHUNK, CHUNK), np.float32))
    tri = jnp.asarray(np.concatenate([tri, tri], axis=1), BF16)
    lm = _hgrn_level_masks()
    return pl.pallas_call(
        _hgrn_kernel,
        out_shape=jax.ShapeDtypeStruct((B * S, HGRN_WIDTH), BF16),
        grid=(B, nt),
        in_specs=[blk, blk, blk, blk,
                  pl.BlockSpec((1, HGRN_DIM), lambda b, n: (0, 0)),
                  pl.BlockSpec(tri.shape, lambda b, n: (0, 0)),
                  pl.BlockSpec(lm.shape, lambda b, n: (0, 0, 0))],
        out_specs=blk,
        scratch_shapes=[pltpu.VMEM((HGRN_HEADS, HGRN_DIM, HGRN_DIM), F32),
                        pltpu.VMEM((tb // CHUNK * HGRN_HEADS, CHUNK, HGRN_DIM), F32)],
        compiler_params=pltpu.CompilerParams(dimension_semantics=("arbitrary", "arbitrary"),
                                             vmem_limit_bytes=VMEM_LIMIT),
        name="hgrn2",
    )(hq, lf, hv, hg, hgrn_norm_w[None, :].astype(F32), tri, lm)


def _merge_kernel(x_ref, attn_ref, hgrn_ref, n1w_ref, wz_ref, wba_ref, wbh_ref, wout_ref, n2w_ref,
                  wr_hi_ref, wr_lo_ref, br_ref, utri_ref, x1_out, hn_out, route_out, count_out, run_ref):
    @pl.when(pl.program_id(0) == 0)
    def _():
        run_ref[...] = jnp.zeros_like(run_ref)

    x = x_ref[...]
    xn = _rms(x, n1w_ref[...]).astype(BF16)
    za = _sigmoid(_dot(xn, wz_ref[:, 0:D_MODEL]))
    zb = _sigmoid(_dot(xn, wz_ref[:, D_MODEL:2 * D_MODEL]))
    mixed = za * _dot(attn_ref[...], wba_ref[...]) + zb * _dot(hgrn_ref[...], wbh_ref[...])
    x1 = x + _dot(mixed.astype(BF16), wout_ref[...])
    x1_out[...] = x1
    hn = _rms(x1, n2w_ref[...])
    hn_out[...] = hn

    hn_hi = hn.astype(BF16)
    hn_lo = (hn - hn_hi.astype(F32)).astype(BF16)
    logits = (_dot_nt(wr_hi_ref[...], hn_hi) + _dot_nt(wr_hi_ref[...], hn_lo) + _dot_nt(wr_lo_ref[...], hn_hi)
              + br_ref[...])
    r = lax.broadcasted_iota(jnp.int32, logits.shape, 0).astype(F32)
    far = float(ROUTER_ROWS)
    cmax = lambda a: jnp.max(a, axis=0, keepdims=True)
    cmin = lambda a: jnp.min(a, axis=0, keepdims=True)
    csum = lambda a: jnp.sum(a, axis=0, keepdims=True)

    lg = jnp.where(r < N_GROUPS, logits, NEG_BIG)
    mg = cmax(lg)
    gsel = cmin(jnp.where(lg == mg, r, far))
    pgsel = 1.0 / csum(jnp.exp(lg - mg))

    lo = N_GROUPS + EXPERTS_PER_GROUP * gsel
    le = jnp.where((r >= lo) & (r < lo + EXPERTS_PER_GROUP), logits, NEG_BIG)
    m1 = cmax(le)
    i1 = cmin(jnp.where(le == m1, r, far))
    se = csum(jnp.exp(le - m1))
    le2 = jnp.where(r == i1, NEG_BIG, le)
    m2 = cmax(le2)
    i2 = cmin(jnp.where(le2 == m2, r, far))
    top0 = 1.0 / se
    top1 = jnp.exp(m2 - m1) / se
    tsum = top0 + top1
    w0 = pgsel * top0 / tsum
    w1 = pgsel * top1 / tsum

    sel1 = r == i1
    sel2 = r == i2
    onehot = (sel1 | sel2).astype(BF16)
    before = _dot(onehot, utri_ref[...]) + run_ref[...]
    r0 = csum(jnp.where(sel1, before, 0.0))
    r1 = csum(jnp.where(sel2, before, 0.0))
    run_new = run_ref[...] + _dot(onehot, jnp.ones(utri_ref.shape, BF16))
    run_ref[...] = run_new
    count_out[...] = run_new

    row8 = lax.broadcasted_iota(jnp.int32, route_out.shape, 0)
    vals = (i1 - N_GROUPS, i2 - N_GROUPS, w0, w1, r0, r1)
    route = jnp.zeros(route_out.shape, F32)
    for j, val in enumerate(vals):
        route = jnp.where(row8 == j, val, route)
    route_out[...] = route


def _merge(x2, attn, hgrn, norm1_w, w_z, w_ba, w_bh, w_out, norm2_w, wr_hi, wr_lo, br, tm):
    T = x2.shape[0]
    row = lambda w: pl.BlockSpec((tm, w), lambda i: (i, 0))
    full = lambda a: pl.BlockSpec(a.shape, lambda i: (0,) * a.ndim)
    utri = jnp.asarray(np.triu(np.ones((tm, tm), np.float32), 1), BF16)
    ins = [x2, attn, hgrn, norm1_w[None, :], w_z, w_ba, w_bh, w_out, norm2_w[None, :], wr_hi, wr_lo, br, utri]
    in_specs = [row(D_MODEL), row(ATTN_WIDTH), row(HGRN_WIDTH)] + [full(a) for a in ins[3:]]
    return pl.pallas_call(
        _merge_kernel,
        out_shape=[jax.ShapeDtypeStruct((T, D_MODEL), F32), jax.ShapeDtypeStruct((T, D_MODEL), F32),
                   jax.ShapeDtypeStruct((ROUTE_ROWS, T), F32), jax.ShapeDtypeStruct((ROUTER_ROWS, tm), F32)],
        grid=(T // tm,),
        in_specs=in_specs,
        out_specs=[row(D_MODEL), row(D_MODEL), pl.BlockSpec((ROUTE_ROWS, tm), lambda i: (0, i)),
                   pl.BlockSpec((ROUTER_ROWS, tm), lambda i: (0, 0))],
        scratch_shapes=[pltpu.VMEM((ROUTER_ROWS, tm), F32)],
        compiler_params=pltpu.CompilerParams(dimension_semantics=("arbitrary",),
                                             vmem_limit_bytes=VMEM_LIMIT),
        name="merge_router",
    )(*ins)


def _gather_rows(idx_ref, n, src_hbm, dst_ref, sem):
    def issue(r, carry):
        pltpu.make_async_copy(src_hbm.at[pl.ds(idx_ref[0, 0, r], 1), :],
                              dst_ref.at[pl.ds(r, 1), :], sem).start()
        return carry
    lax.fori_loop(0, n, issue, 0, unroll=16)
    pltpu.make_async_copy(src_hbm.at[pl.ds(0, n), :], dst_ref.at[pl.ds(0, n), :], sem).wait()


def _dispatch_kernel(pos_ref, hn_ref, xs_hbm, sem):
    tk = hn_ref.shape[0]

    for k in range(TOP_K):
        def issue(r, carry):
            pltpu.make_async_copy(hn_ref.at[pl.ds(r, 1), :],
                                  xs_hbm.at[pl.ds(pos_ref[0, 0, k * tk + r], 1), :], sem).start()
            return carry
        lax.fori_loop(0, tk, issue, 0, unroll=16)
    for k in range(TOP_K):
        pltpu.make_async_copy(hn_ref, xs_hbm.at[pl.ds(0, tk), :], sem).wait()


def _tile_positions(pos, tk):
    nt = pos.shape[0] // tk
    return pos.reshape(nt, tk, TOP_K).transpose(0, 2, 1).reshape(nt, 1, TOP_K * tk)


def _dispatch(pos, hn, tk):
    T = hn.shape[0]
    return pl.pallas_call(
        _dispatch_kernel,
        out_shape=jax.ShapeDtypeStruct((TOP_K * T, D_MODEL), F32),
        grid=(T // tk,),
        in_specs=[pl.BlockSpec((1, 1, TOP_K * tk), lambda i: (i, 0, 0), memory_space=pltpu.SMEM),
                  pl.BlockSpec((tk, D_MODEL), lambda i: (i, 0))],
        out_specs=pl.BlockSpec(memory_space=pl.ANY),
        scratch_shapes=[pltpu.SemaphoreType.DMA],
        compiler_params=pltpu.CompilerParams(dimension_semantics=("arbitrary",),
                                             vmem_limit_bytes=VMEM_LIMIT),
        name="moe_dispatch",
    )(_tile_positions(pos, tk), hn)


def _expert_kernel(tile_ref, exp_ref, lo_ref, hi_ref, x_ref, wg_ref, wu_ref, wd_ref, y_ref,
                   wg_s, wu_s, wd_s):
    w = pl.program_id(0)
    lo = lo_ref[w]
    hi = hi_ref[w]
    changed = (w == 0) | (exp_ref[w] != exp_ref[jnp.maximum(w - 1, 0)])

    @pl.when(changed)
    def _():
        wg_s[...] = wg_ref[0].astype(BF16)
        wu_s[...] = wu_ref[0].astype(BF16)
        wd_s[...] = wd_ref[0].astype(BF16)

    @pl.when(hi > lo)
    def _():
        xb = x_ref[...].astype(BF16)
        gate = _dot(xb, wg_s[...])
        up = _dot(xb, wu_s[...])
        hmid = (gate * _sigmoid(gate) * up).astype(BF16)
        y = _dot(hmid, wd_s[...])
        rowi = lax.broadcasted_iota(jnp.int32, y.shape, 0)
        mine = (rowi >= lo) & (rowi < hi)

        @pl.when(lo == 0)
        def _():
            y_ref[...] = jnp.where(mine, y, 0.0)

        @pl.when(lo > 0)
        def _():
            y_ref[...] = jnp.where(mine, y, y_ref[...])


def _experts(items, xs, w_gate, w_up, w_down):
    tile, exp, lo, hi = items
    wspec = lambda shape: pl.BlockSpec((1,) + shape, lambda w, t, e, l, h: (e[w], 0, 0))
    xspec = pl.BlockSpec((MOE_BLOCK, D_MODEL), lambda w, t, e, l, h: (t[w], 0))
    return pl.pallas_call(
        _expert_kernel,
        out_shape=jax.ShapeDtypeStruct(xs.shape, F32),
        grid_spec=pltpu.PrefetchScalarGridSpec(
            num_scalar_prefetch=4,
            grid=(tile.shape[0],),
            in_specs=[xspec, wspec((D_MODEL, EXPERT_FF)), wspec((D_MODEL, EXPERT_FF)),
                      wspec((EXPERT_FF, D_MODEL))],
            out_specs=xspec,
            scratch_shapes=[pltpu.VMEM((D_MODEL, EXPERT_FF), BF16),
                            pltpu.VMEM((D_MODEL, EXPERT_FF), BF16),
                            pltpu.VMEM((EXPERT_FF, D_MODEL), BF16)]),
        compiler_params=pltpu.CompilerParams(dimension_semantics=("arbitrary",),
                                             vmem_limit_bytes=VMEM_LIMIT),
        name="moe_experts",
    )(tile, exp, lo, hi, xs, w_gate, w_up, w_down)


def _combine_kernel(pos_ref, x1_ref, w_ref, y_hbm, o_ref, ybuf, sem):
    tk = x1_ref.shape[0]
    _gather_rows(pos_ref, TOP_K * tk, y_hbm, ybuf, sem)
    o_ref[...] = x1_ref[...] + w_ref[:, 0:1] * ybuf[0:tk, :] + w_ref[:, 1:2] * ybuf[tk:2 * tk, :]


def _combine(pos, x1, gate_w, y, tk):
    T = x1.shape[0]
    return pl.pallas_call(
        _combine_kernel,
        out_shape=jax.ShapeDtypeStruct((T, D_MODEL), F32),
        grid=(T // tk,),
        in_specs=[pl.BlockSpec((1, 1, TOP_K * tk), lambda i: (i, 0, 0), memory_space=pltpu.SMEM),
                  pl.BlockSpec((tk, D_MODEL), lambda i: (i, 0)),
                  pl.BlockSpec((tk, TOP_K), lambda i: (i, 0)),
                  pl.BlockSpec(memory_space=pl.ANY)],
        out_specs=pl.BlockSpec((tk, D_MODEL), lambda i: (i, 0)),
        scratch_shapes=[pltpu.VMEM((TOP_K * tk, D_MODEL), F32), pltpu.SemaphoreType.DMA],
        compiler_params=pltpu.CompilerParams(dimension_semantics=("arbitrary",),
                                             vmem_limit_bytes=VMEM_LIMIT),
        name="moe_combine",
    )(_tile_positions(pos, tk), x1, gate_w, y)


def _routing_tables(route, counts, T):
    e = route[0:TOP_K].astype(jnp.int32)
    rank = route[4:4 + TOP_K].astype(jnp.int32)
    counts = counts.astype(jnp.int32)
    starts = jnp.cumsum(counts) - counts
    ids = jnp.arange(N_EXPERTS, dtype=jnp.int32)
    pos = rank + jnp.sum(jnp.where(e[:, :, None] == ids, starts, 0), axis=-1)
    n_rows = TOP_K * T
    cuts = jnp.sort(jnp.concatenate([jnp.arange(0, n_rows, MOE_BLOCK, dtype=jnp.int32), starts[1:]]))
    ends = jnp.concatenate([cuts[1:], jnp.full((1,), n_rows, jnp.int32)])
    tile = jnp.minimum(cuts // MOE_BLOCK, n_rows // MOE_BLOCK - 1)
    exp = jnp.clip(jnp.sum(starts[None, :] <= cuts[:, None], axis=1) - 1, 0, N_EXPERTS - 1).astype(jnp.int32)
    lo = cuts - tile * MOE_BLOCK
    hi = ends - tile * MOE_BLOCK
    return pos.T, (tile, exp, lo, hi)


def _pick_tile(n, pref):
    t = pref
    while n % t:
        t //= 2
    return t


def kernel(x, positions, norm1_w, w_in, q_norm_w, k_norm_w, attn_sinks, hgrn_lower_bounds, hgrn_norm_w,
           w_branch_attn, w_branch_hgrn, w_out, norm2_w, w_router_group, b_router_group, w_router_expert,
           b_router_expert, w_gate_experts, w_up_experts, w_down_experts):
    B, S, D = x.shape
    T = B * S
    x2 = x.reshape(T, D)
    tm = _pick_tile(T, 256)

    inv_freq = ROPE_THETA ** (-jnp.arange(0, ROT_DIM, 2, dtype=F32) / ROT_DIM)
    ang = positions.astype(F32).reshape(T, 1) * inv_freq[None, :]
    cs = jnp.concatenate([jnp.cos(ang), jnp.sin(ang)], axis=-1)

    w_in0 = w_in[0]
    w_in_a = w_in0[:, :_OFF_Z].astype(BF16)
    w_z = w_in0[:, _OFF_Z:].astype(BF16)

    q, k, v, hq, lf, hv, hg = _inproj(x2, norm1_w[0], w_in_a, cs, q_norm_w[0], k_norm_w[0],
                                      hgrn_lower_bounds.astype(F32), tm)
    attn = _attention(q, k, v, attn_sinks[0].astype(F32), B, S)
    hgrn = _hgrn(hq, lf, hv, hg, hgrn_norm_w[0], B, S, _pick_tile(S, 256))

    pad = ROUTER_ROWS - N_GROUPS - N_EXPERTS
    w_r = jnp.concatenate([w_router_group[0].T, w_router_expert[0].T, jnp.zeros((pad, D), F32)], axis=0)
    wr_hi = w_r.astype(BF16)
    wr_lo = (w_r - wr_hi.astype(F32)).astype(BF16)
    b_r = jnp.concatenate([b_router_group[0], b_router_expert[0], jnp.zeros((pad,), F32)]).astype(F32)
    b_r = jnp.broadcast_to(b_r[:, None], (ROUTER_ROWS, tm))

    x1, hn, route, counts = _merge(x2, attn, hgrn, norm1_w[0], w_z, w_branch_attn[0].astype(BF16),
                                   w_branch_hgrn[0].astype(BF16), w_out[0].astype(BF16), norm2_w[0],
                                   wr_hi, wr_lo, b_r, tm)

    pos, items = _routing_tables(route, counts[N_GROUPS:N_GROUPS + N_EXPERTS, 0], T)
    tk = _pick_tile(T, 128)
    xs = _dispatch(pos, hn, tk)
    y = _experts(items, xs, w_gate_experts[0], w_up_experts[0], w_down_experts[0])
    out = _combine(pos, x1, route[2:2 + TOP_K].T, y, tk)
    return out.reshape(B, S, D)
```

```python
import numpy as np
import jax
import jax.numpy as jnp
from jax import lax
from jax.experimental import pallas as pl
from jax.experimental.pallas import tpu as pltpu

F32 = jnp.float32
BF16 = jnp.bfloat16

D_MODEL = 1024
N_Q_HEADS = 8
N_KV_HEADS = 2
GROUP = N_Q_HEADS // N_KV_HEADS
HEAD_DIM = 64
ROT_DIM = HEAD_DIM // 4
ROT_HALF = ROT_DIM // 2
ROPE_THETA = 500000.0
WINDOW = 128
ATTN_WIDTH = N_Q_HEADS * HEAD_DIM
KV_WIDTH = N_KV_HEADS * HEAD_DIM

HGRN_HEADS = 4
HGRN_DIM = 128
HGRN_WIDTH = HGRN_HEADS * HGRN_DIM
CHUNK = 64

N_GROUPS = 4
EXPERTS_PER_GROUP = 8
N_EXPERTS = N_GROUPS * EXPERTS_PER_GROUP
TOP_K = 2
EXPERT_FF = 512
MOE_BLOCK = 128
EXPERT_TILE = 512
NORM_EPS = 1e-6
ROUTER_ROWS = 40
ROUTE_ROWS = 8

LANES = 128
NEG_BIG = -1e30
LOG2_E = 1.4426950408889634

_OFF_Q, _OFF_K, _OFF_V = 0, ATTN_WIDTH, ATTN_WIDTH + KV_WIDTH
_OFF_HQ = ATTN_WIDTH + 2 * KV_WIDTH
_OFF_HF = _OFF_HQ + HGRN_WIDTH
_OFF_HI = _OFF_HF + HGRN_WIDTH
_OFF_HG = _OFF_HI + HGRN_WIDTH
_OFF_Z = _OFF_HG + HGRN_WIDTH

VMEM_LIMIT = 56 * 1024 * 1024


def _split3(a):
    hi = a.astype(BF16)
    r1 = a - hi.astype(F32)
    mid = r1.astype(BF16)
    lo = (r1 - mid.astype(F32)).astype(BF16)
    return hi, mid, lo


def _dot(a, b):
    return jnp.dot(a, b, preferred_element_type=F32)


def _dot_nt(a, b):
    return lax.dot_general(a, b, (((1,), (1,)), ((), ())), preferred_element_type=F32)


def _dot_tn(a, b):
    return lax.dot_general(a, b, (((0,), (0,)), ((), ())), preferred_element_type=F32)


def _exact_lhs_dot(m01, a):
    hi, mid, lo = _split3(a)
    return _dot(m01, hi) + _dot(m01, mid) + _dot(m01, lo)


def _exact_rhs_dot(a, m01):
    hi, mid, lo = _split3(a)
    return _dot(hi, m01) + _dot(mid, m01) + _dot(lo, m01)


def _sigmoid(x):
    return 1.0 / (1.0 + jnp.exp(-x))


def _rms(x, w):
    ms = jnp.mean(x * x, axis=-1, keepdims=True)
    return x * lax.rsqrt(ms + NORM_EPS) * w


def _inproj_kernel(x_ref, n1w_ref, w_ref, cs_ref, rope_e_ref, rope_c0_ref, qw_ref, kw_ref,
                   mq_ref, mk_ref, lbp_ref, dup_ref,
                   q_out, k_out, v_out, hq_out, lf_out, hv_out, hg_out):
    xn = _rms(x_ref[...], n1w_ref[...]).astype(BF16)

    def proj(off, width):
        return _dot(xn, w_ref[:, off:off + width])

    tabs = _exact_rhs_dot(cs_ref[...], rope_e_ref[...])
    c_tab = tabs[:, 0:LANES] + rope_c0_ref[...]
    s1_tab = tabs[:, LANES:2 * LANES]
    s2_tab = tabs[:, 2 * LANES:3 * LANES]

    def norm_rope(t, mavg_ref, w_row, scale):
        ms = _dot((t * t).astype(BF16), mavg_ref[...])
        tn = t * lax.rsqrt(ms + NORM_EPS) * w_row
        if scale != 1.0:
            tn = tn * scale
        outs = []
        for j in range(t.shape[1] // LANES):
            c = tn[:, j * LANES:(j + 1) * LANES]
            outs.append(c * c_tab
                        + pltpu.roll(c, LANES - ROT_HALF, 1) * s1_tab
                        + pltpu.roll(c, ROT_HALF, 1) * s2_tab)
        return outs[0] if len(outs) == 1 else jnp.concatenate(outs, axis=1)

    q_out[...] = norm_rope(proj(_OFF_Q, ATTN_WIDTH), mq_ref, qw_ref[...], HEAD_DIM ** -0.5).astype(BF16)
    k_rot = norm_rope(proj(_OFF_K, KV_WIDTH), mk_ref, kw_ref[...], 1.0).astype(BF16)
    k_out[...] = _dot(k_rot, dup_ref[...]).astype(BF16)
    v_out[...] = _dot(proj(_OFF_V, KV_WIDTH).astype(BF16), dup_ref[...]).astype(BF16)

    hq = proj(_OFF_HQ, HGRN_WIDTH)
    hq_out[...] = (hq * _sigmoid(hq)).astype(BF16)
    h0 = lbp_ref[0:1, :]
    h1 = lbp_ref[1:2, :]
    hm = jnp.maximum(h0, h1)
    e0 = jnp.exp(h0 - hm)
    e1 = jnp.exp(h1 - hm)
    lb = e0 / (e0 + e1)
    fg = lb + (1.0 - lb) * _sigmoid(proj(_OFF_HF, HGRN_WIDTH))
    lf_out[...] = jnp.log(fg) * LOG2_E
    hv_out[...] = proj(_OFF_HI, HGRN_WIDTH).astype(BF16)
    hg = proj(_OFF_HG, HGRN_WIDTH)
    hg_out[...] = (hg * _sigmoid(hg)).astype(BF16)


def _rope_constants():
    e = np.zeros((2 * ROT_HALF, 3 * LANES), np.float32)
    c0 = np.zeros((1, LANES), np.float32)
    for lane in range(LANES):
        d = lane % HEAD_DIM
        if d < ROT_HALF:
            e[d, lane] = 1.0
            e[ROT_HALF + d, LANES + lane] = -1.0
        elif d < ROT_DIM:
            e[d - ROT_HALF, lane] = 1.0
            e[ROT_HALF + d - ROT_HALF, 2 * LANES + lane] = 1.0
        else:
            c0[0, lane] = 1.0
    return jnp.asarray(e, BF16), jnp.asarray(c0, F32)


def _head_mean_matrix(width):
    idx = np.arange(width) // HEAD_DIM
    m = (idx[:, None] == idx[None, :]).astype(np.float32) / HEAD_DIM
    return jnp.asarray(m, BF16)


def _inproj(x2, norm1_w, w_in_a, cs, q_norm_w, k_norm_w, lbp, tm):
    T = x2.shape[0]
    rope_e, rope_c0 = _rope_constants()
    qw = jnp.tile(q_norm_w.astype(F32), N_Q_HEADS)[None, :]
    kw = jnp.tile(k_norm_w.astype(F32), N_KV_HEADS)[None, :]
    mq = _head_mean_matrix(ATTN_WIDTH)
    mk = _head_mean_matrix(KV_WIDTH)
    row = lambda w: pl.BlockSpec((tm, w), lambda i: (i, 0))
    full = lambda a: pl.BlockSpec(a.shape, lambda i: (0,) * a.ndim)
    src = HEAD_DIM * (np.arange(2 * KV_WIDTH) // LANES) + np.arange(2 * KV_WIDTH) % HEAD_DIM
    dup = jnp.asarray(np.arange(KV_WIDTH)[:, None] == src[None, :], BF16)
    ins = [x2, norm1_w[None, :], w_in_a, cs, rope_e, rope_c0, qw, kw, mq, mk, lbp, dup]
    in_specs = [row(D_MODEL), full(ins[1]), full(w_in_a), row(cs.shape[1])] + [full(a) for a in ins[4:]]
    outs = [(ATTN_WIDTH, BF16), (2 * KV_WIDTH, BF16), (2 * KV_WIDTH, BF16), (HGRN_WIDTH, BF16),
            (HGRN_WIDTH, F32), (HGRN_WIDTH, BF16), (HGRN_WIDTH, BF16)]
    return pl.pallas_call(
        _inproj_kernel,
        out_shape=[jax.ShapeDtypeStruct((T, w), dt) for w, dt in outs],
        grid=(T // tm,),
        in_specs=in_specs,
        out_specs=[row(w) for w, _ in outs],
        compiler_params=pltpu.CompilerParams(dimension_semantics=("arbitrary",),
                                             vmem_limit_bytes=VMEM_LIMIT),
        name="inproj",
    )(*ins)


def _attn_kernel(sink_ref, q_ref, kc_ref, kp_ref, vc_ref, vp_ref, half_ref, o_ref):
    has_prev = pl.program_id(1) > 0
    qi = lax.broadcasted_iota(jnp.int32, (WINDOW, 2 * WINDOW), 0)
    kj = lax.broadcasted_iota(jnp.int32, (WINDOW, 2 * WINDOW), 1)
    valid = ((kj < WINDOW) & (kj > qi) & has_prev) | ((kj >= WINDOW) & (kj - WINDOW <= qi))
    left = lax.broadcasted_iota(jnp.int32, (WINDOW, LANES), 1) < HEAD_DIM
    half = (half_ref[0], half_ref[1])
    outs = []
    for h in range(N_KV_HEADS):
        cols = slice(h * LANES, (h + 1) * LANES)
        kcat = jnp.concatenate([kp_ref[:, cols], kc_ref[:, cols]], axis=0)
        vcat = jnp.concatenate([vp_ref[:, cols], vc_ref[:, cols]], axis=0)
        ks = [kcat * hm for hm in half]
        rhs = [jnp.concatenate([vcat * hm, hm], axis=1) for hm in half]
        for pr in range(GROUP // 2):
            pair = h * (GROUP // 2) + pr
            qp = q_ref[:, pair * LANES:(pair + 1) * LANES]
            acc = None
            m_side = []
            for side in range(2):
                s = jnp.where(valid, _dot_nt(qp, ks[side]), NEG_BIG)
                sink = sink_ref[2 * pair + side]
                m = jnp.maximum(jnp.max(s, axis=-1, keepdims=True), sink)
                p = jnp.exp(s - m).astype(BF16)
                d = _dot(p, rhs[side])
                acc = d if acc is None else acc + d
                m_side.append(jnp.exp(sink - m))
            den = acc[:, LANES:2 * LANES] + jnp.where(left, m_side[0], m_side[1])
            outs.append(acc[:, 0:LANES] / den)
    o_ref[...] = jnp.concatenate(outs, axis=1).astype(BF16)


def _attention(q, k, v, sinks, B, S):
    nb = S // WINDOW
    cur = lambda b, n: (b * nb + n, 0)
    prev = lambda b, n: (b * nb + jnp.maximum(n - 1, 0), 0)
    lane_left = np.arange(LANES) < HEAD_DIM
    half = jnp.asarray(np.broadcast_to(np.stack([lane_left, ~lane_left])[:, None, :],
                                       (2, 2 * WINDOW, LANES)), BF16)
    return pl.pallas_call(
        _attn_kernel,
        out_shape=jax.ShapeDtypeStruct((B * S, ATTN_WIDTH), BF16),
        grid=(B, nb),
        in_specs=[pl.BlockSpec(memory_space=pltpu.SMEM),
                  pl.BlockSpec((WINDOW, ATTN_WIDTH), cur),
                  pl.BlockSpec((WINDOW, 2 * KV_WIDTH), cur),
                  pl.BlockSpec((WINDOW, 2 * KV_WIDTH), prev),
                  pl.BlockSpec((WINDOW, 2 * KV_WIDTH), cur),
                  pl.BlockSpec((WINDOW, 2 * KV_WIDTH), prev),
                  pl.BlockSpec(half.shape, lambda b, n: (0, 0, 0))],
        out_specs=pl.BlockSpec((WINDOW, ATTN_WIDTH), cur),
        compiler_params=pltpu.CompilerParams(dimension_semantics=("arbitrary", "arbitrary"),
                                             vmem_limit_bytes=VMEM_LIMIT),
        name="swa_attention",
    )(sinks, q, k, k, v, v, half)


_LEVEL_HALVES = (1, 2, 4, 8, 16, 32)


def _hgrn_level_masks():
    t = np.arange(CHUNK)[:, None]
    s = np.arange(CHUNK)[None, :]
    masks = [((t // (2 * h)) == (s // (2 * h))) & ((t & h) != 0) & ((s & h) == 0) for h in _LEVEL_HALVES]
    return jnp.asarray(np.stack(masks), F32)


def _level_reference(b_ref, slot, half):
    if half >= 4:
        span = max(2 * half, 8)
        pieces = [jnp.broadcast_to(b_ref[slot, s + half - 1:s + half, :], (span, HGRN_DIM))
                  for s in range(0, CHUNK, span)]
    else:
        r8 = lax.broadcasted_iota(jnp.int32, (8, HGRN_DIM), 0)
        pieces = [jnp.where(r8 < 4,
                            jnp.broadcast_to(b_ref[slot, s + 1:s + 2, :], (8, HGRN_DIM)),
                            jnp.broadcast_to(b_ref[slot, s + 5:s + 6, :], (8, HGRN_DIM)))
                  for s in range(0, CHUNK, 8)]
    return pieces[0] if len(pieces) == 1 else jnp.concatenate(pieces, axis=0)


def _hgrn_kernel(hq_ref, lf_ref, hv_ref, hg_ref, nw_ref, tri_ref, lm_ref, o_ref, st_ref, b_ref):
    @pl.when(pl.program_id(1) == 0)
    def _():
        st_ref[...] = jnp.zeros_like(st_ref)

    tri2 = tri_ref[...]
    odd = (lax.broadcasted_iota(jnp.int32, (CHUNK, HGRN_DIM), 0) & 1) != 0
    masks = [lm_ref[li] != 0.0 for li in range(len(_LEVEL_HALVES))]
    units = [(c, h) for c in range(hq_ref.shape[0] // CHUNK) for h in range(HGRN_HEADS)]
    sl = lambda c, h: (slice(c * CHUNK, (c + 1) * CHUNK), slice(h * HGRN_DIM, (h + 1) * HGRN_DIM))

    bs = []
    for u, (c, h) in enumerate(units):
        lf2 = lf_ref[sl(c, h)]
        hi = lf2.astype(BF16)
        lo = (lf2 - hi.astype(F32)).astype(BF16)
        b = _dot(tri2, jnp.concatenate([hi, lo], axis=0))
        b_ref[u] = b
        bs.append(b)

    outs, qs, ks, fs = [], [], [], []
    for u, (c, h) in enumerate(units):
        b = bs[u]
        q = hq_ref[sl(c, h)].astype(F32)
        v_bf = hv_ref[sl(c, h)]
        f = jnp.exp2(lf_ref[sl(c, h)])
        k = 1.0 - f
        b_last = b[CHUNK - 1:CHUNK, :]
        st = st_ref[h]
        o = _dot_nt((q * jnp.exp2(b)).astype(BF16), st.astype(BF16))
        k_out = (k * jnp.exp2(b_last - b)).astype(BF16)
        st_ref[h] = st * jnp.exp2(b_last) + _dot_tn(v_bf, k_out)
        outs.append(o + jnp.sum(q * k, axis=-1, keepdims=True) * v_bf.astype(F32))
        qs.append(q)
        ks.append(k)
        fs.append(f)

    accs = [jnp.zeros((CHUNK, CHUNK), F32) for _ in units]
    for li, half in enumerate(_LEVEL_HALVES):
        for u in range(len(units)):
            if half == 1:
                e = jnp.where(odd, fs[u], 1.0)
            else:
                e = jnp.exp2(-jnp.abs(bs[u] - _level_reference(b_ref, u, half)))
            accs[u] = jnp.where(masks[li], _dot_nt((qs[u] * e).astype(BF16), (ks[u] * e).astype(BF16)), accs[u])

    for u, (c, h) in enumerate(units):
        o = outs[u] + _dot(accs[u].astype(BF16), hv_ref[sl(c, h)])
        y = _rms(o, nw_ref[...]) * hg_ref[sl(c, h)].astype(F32)
        o_ref[sl(c, h)] = y.astype(BF16)


def _hgrn(hq, lf, hv, hg, hgrn_norm_w, B, S, tb):
    nt = S // tb
    blk = pl.BlockSpec((tb, HGRN_WIDTH), lambda b, n: (b * nt + n, 0))
    tri = np.tril(np.ones((CHUNK, CHUNK), np.float32))
    tri = jnp.asarray(np.concatenate([tri, tri], axis=1), BF16)
    lm = _hgrn_level_masks()
    return pl.pallas_call(
        _hgrn_kernel,
        out_shape=jax.ShapeDtypeStruct((B * S, HGRN_WIDTH), BF16),
        grid=(B, nt),
        in_specs=[blk, blk, blk, blk,
                  pl.BlockSpec((1, HGRN_DIM), lambda b, n: (0, 0)),
                  pl.BlockSpec(tri.shape, lambda b, n: (0, 0)),
                  pl.BlockSpec(lm.shape, lambda b, n: (0, 0, 0))],
        out_specs=blk,
        scratch_shapes=[pltpu.VMEM((HGRN_HEADS, HGRN_DIM, HGRN_DIM), F32),
                        pltpu.VMEM((tb // CHUNK * HGRN_HEADS, CHUNK, HGRN_DIM), F32)],
        compiler_params=pltpu.CompilerParams(dimension_semantics=("arbitrary", "arbitrary"),
                                             vmem_limit_bytes=VMEM_LIMIT),
        name="hgrn2",
    )(hq, lf, hv, hg, hgrn_norm_w[None, :].astype(F32), tri, lm)


def _merge_kernel(x_ref, attn_ref, hgrn_ref, n1w_ref, wz_ref, wba_ref, wbh_ref, wout_ref, n2w_ref,
                  wr_hi_ref, wr_lo_ref, br_ref, utri_ref, x1_out, hn_out, route_out, count_out, run_ref):
    @pl.when(pl.program_id(0) == 0)
    def _():
        run_ref[...] = jnp.zeros_like(run_ref)

    x = x_ref[...]
    xn = _rms(x, n1w_ref[...]).astype(BF16)
    za = _sigmoid(_dot(xn, wz_ref[:, 0:D_MODEL]))
    zb = _sigmoid(_dot(xn, wz_ref[:, D_MODEL:2 * D_MODEL]))
    mixed = za * _dot(attn_ref[...], wba_ref[...]) + zb * _dot(hgrn_ref[...], wbh_ref[...])
    x1 = x + _dot(mixed.astype(BF16), wout_ref[...])
    x1_out[...] = x1
    hn = _rms(x1, n2w_ref[...])
    hn_out[...] = hn

    hn_hi = hn.astype(BF16)
    hn_lo = (hn - hn_hi.astype(F32)).astype(BF16)
    logits = (_dot_nt(wr_hi_ref[...], hn_hi) + _dot_nt(wr_hi_ref[...], hn_lo) + _dot_nt(wr_lo_ref[...], hn_hi)
              + br_ref[...])
    r = lax.broadcasted_iota(jnp.int32, logits.shape, 0).astype(F32)
    far = float(ROUTER_ROWS)
    cmax = lambda a: jnp.max(a, axis=0, keepdims=True)
    cmin = lambda a: jnp.min(a, axis=0, keepdims=True)
    csum = lambda a: jnp.sum(a, axis=0, keepdims=True)

    lg = jnp.where(r < N_GROUPS, logits, NEG_BIG)
    mg = cmax(lg)
    gsel = cmin(jnp.where(lg == mg, r, far))
    pgsel = 1.0 / csum(jnp.exp(lg - mg))

    lo = N_GROUPS + EXPERTS_PER_GROUP * gsel
    le = jnp.where((r >= lo) & (r < lo + EXPERTS_PER_GROUP), logits, NEG_BIG)
    m1 = cmax(le)
    i1 = cmin(jnp.where(le == m1, r, far))
    se = csum(jnp.exp(le - m1))
    le2 = jnp.where(r == i1, NEG_BIG, le)
    m2 = cmax(le2)
    i2 = cmin(jnp.where(le2 == m2, r, far))
    top0 = 1.0 / se
    top1 = jnp.exp(m2 - m1) / se
    tsum = top0 + top1
    w0 = pgsel * top0 / tsum
    w1 = pgsel * top1 / tsum

    sel1 = r == i1
    sel2 = r == i2
    onehot = (sel1 | sel2).astype(BF16)
    before = _dot(onehot, utri_ref[...]) + run_ref[...]
    r0 = csum(jnp.where(sel1, before, 0.0))
    r1 = csum(jnp.where(sel2, before, 0.0))
    run_new = run_ref[...] + _dot(onehot, jnp.ones(utri_ref.shape, BF16))
    run_ref[...] = run_new
    count_out[...] = run_new

    row8 = lax.broadcasted_iota(jnp.int32, route_out.shape, 0)
    vals = (i1 - N_GROUPS, i2 - N_GROUPS, w0, w1, r0, r1)
    route = jnp.zeros(route_out.shape, F32)
    for j, val in enumerate(vals):
        route = jnp.where(row8 == j, val, route)
    route_out[...] = route


def _merge(x2, attn, hgrn, norm1_w, w_z, w_ba, w_bh, w_out, norm2_w, wr_hi, wr_lo, br, tm):
    T = x2.shape[0]
    row = lambda w: pl.BlockSpec((tm, w), lambda i: (i, 0))
    full = lambda a: pl.BlockSpec(a.shape, lambda i: (0,) * a.ndim)
    utri = jnp.asarray(np.triu(np.ones((tm, tm), np.float32), 1), BF16)
    ins = [x2, attn, hgrn, norm1_w[None, :], w_z, w_ba, w_bh, w_out, norm2_w[None, :], wr_hi, wr_lo, br, utri]
    in_specs = [row(D_MODEL), row(ATTN_WIDTH), row(HGRN_WIDTH)] + [full(a) for a in ins[3:]]
    return pl.pallas_call(
        _merge_kernel,
        out_shape=[jax.ShapeDtypeStruct((T, D_MODEL), F32), jax.ShapeDtypeStruct((T, D_MODEL), F32),
                   jax.ShapeDtypeStruct((ROUTE_ROWS, T), F32), jax.ShapeDtypeStruct((ROUTER_ROWS, tm), F32)],
        grid=(T // tm,),
        in_specs=in_specs,
        out_specs=[row(D_MODEL), row(D_MODEL), pl.BlockSpec((ROUTE_ROWS, tm), lambda i: (0, i)),
                   pl.BlockSpec((ROUTER_ROWS, tm), lambda i: (0, 0))],
        scratch_shapes=[pltpu.VMEM((ROUTER_ROWS, tm), F32)],
        compiler_params=pltpu.CompilerParams(dimension_semantics=("arbitrary",),
                                             vmem_limit_bytes=VMEM_LIMIT),
        name="merge_router",
    )(*ins)


def _gather_rows(idx_ref, n, src_hbm, dst_ref, sem):
    def issue(r, carry):
        pltpu.make_async_copy(src_hbm.at[pl.ds(idx_ref[0, 0, r], 1), :],
                              dst_ref.at[pl.ds(r, 1), :], sem).start()
        return carry
    lax.fori_loop(0, n, issue, 0, unroll=16)
    pltpu.make_async_copy(src_hbm.at[pl.ds(0, n), :], dst_ref.at[pl.ds(0, n), :], sem).wait()


def _dispatch_kernel(pos_ref, hn_ref, xs_hbm, sem):
    tk = hn_ref.shape[0]

    for k in range(TOP_K):
        def issue(r, carry):
            pltpu.make_async_copy(hn_ref.at[pl.ds(r, 1), :],
                                  xs_hbm.at[pl.ds(pos_ref[0, 0, k * tk + r], 1), :], sem).start()
            return carry
        lax.fori_loop(0, tk, issue, 0, unroll=16)
    for k in range(TOP_K):
        pltpu.make_async_copy(hn_ref, xs_hbm.at[pl.ds(0, tk), :], sem).wait()


def _tile_positions(pos, tk):
    nt = pos.shape[0] // tk
    return pos.reshape(nt, tk, TOP_K).transpose(0, 2, 1).reshape(nt, 1, TOP_K * tk)


def _dispatch(pos, hn, tk):
    T = hn.shape[0]
    return pl.pallas_call(
        _dispatch_kernel,
        out_shape=jax.ShapeDtypeStruct((TOP_K * T, D_MODEL), F32),
        grid=(T // tk,),
        in_specs=[pl.BlockSpec((1, 1, TOP_K * tk), lambda i: (i, 0, 0), memory_space=pltpu.SMEM),
                  pl.BlockSpec((tk, D_MODEL), lambda i: (i, 0))],
        out_specs=pl.BlockSpec(memory_space=pl.ANY),
        scratch_shapes=[pltpu.SemaphoreType.DMA],
        compiler_params=pltpu.CompilerParams(dimension_semantics=("arbitrary",),
                                             vmem_limit_bytes=VMEM_LIMIT),
        name="moe_dispatch",
    )(_tile_positions(pos, tk), hn)


def _expert_kernel(tile_ref, exp_ref, lo_ref, hi_ref, x_ref, wg_ref, wu_ref, wd_ref, y_ref,
                   wg_s, wu_s, wd_s):
    w = pl.program_id(0)
    lo = lo_ref[w]
    hi = hi_ref[w]
    changed = (w == 0) | (exp_ref[w] != exp_ref[jnp.maximum(w - 1, 0)])

    @pl.when(changed)
    def _():
        wg_s[...] = wg_ref[0].astype(BF16)
        wu_s[...] = wu_ref[0].astype(BF16)
        wd_s[...] = wd_ref[0].astype(BF16)

    def ffn(x):
        xb = x.astype(BF16)
        gate = _dot(xb, wg_s[...])
        up = _dot(xb, wu_s[...])
        return _dot((gate * _sigmoid(gate) * up).astype(BF16), wd_s[...])

    whole = (lo == 0) & (hi == EXPERT_TILE)

    @pl.when(whole)
    def _():
        y_ref[...] = ffn(x_ref[...])

    for j in range(EXPERT_TILE // MOE_BLOCK):
        r0 = j * MOE_BLOCK
        rows = slice(r0, r0 + MOE_BLOCK)

        @pl.when(jnp.logical_not(whole) & (lo < r0 + MOE_BLOCK) & (hi > r0))
        def _():
            y = ffn(x_ref[rows, :])
            rowi = lax.broadcasted_iota(jnp.int32, y.shape, 0) + r0
            mine = (rowi >= lo) & (rowi < hi)

            @pl.when(lo <= r0)
            def _():
                y_ref[rows, :] = jnp.where(mine, y, 0.0)

            @pl.when(lo > r0)
            def _():
                y_ref[rows, :] = jnp.where(mine, y, y_ref[rows, :])


def _experts(items, xs, w_gate, w_up, w_down):
    tile, exp, lo, hi = items
    wspec = lambda shape: pl.BlockSpec((1,) + shape, lambda w, t, e, l, h: (e[w], 0, 0))
    xspec = pl.BlockSpec((EXPERT_TILE, D_MODEL), lambda w, t, e, l, h: (t[w], 0))
    return pl.pallas_call(
        _expert_kernel,
        out_shape=jax.ShapeDtypeStruct(xs.shape, F32),
        grid_spec=pltpu.PrefetchScalarGridSpec(
            num_scalar_prefetch=4,
            grid=(tile.shape[0],),
            in_specs=[xspec, wspec((D_MODEL, EXPERT_FF)), wspec((D_MODEL, EXPERT_FF)),
                      wspec((EXPERT_FF, D_MODEL))],
            out_specs=xspec,
            scratch_shapes=[pltpu.VMEM((D_MODEL, EXPERT_FF), BF16),
                            pltpu.VMEM((D_MODEL, EXPERT_FF), BF16),
                            pltpu.VMEM((EXPERT_FF, D_MODEL), BF16)]),
        compiler_params=pltpu.CompilerParams(dimension_semantics=("arbitrary",),
                                             vmem_limit_bytes=VMEM_LIMIT),
        name="moe_experts",
    )(tile, exp, lo, hi, xs, w_gate, w_up, w_down)


def _combine_kernel(pos_ref, x1_ref, w_ref, y_hbm, o_ref, ybuf, sem):
    tk = x1_ref.shape[0]
    _gather_rows(pos_ref, TOP_K * tk, y_hbm, ybuf, sem)
    o_ref[...] = x1_ref[...] + w_ref[:, 0:1] * ybuf[0:tk, :] + w_ref[:, 1:2] * ybuf[tk:2 * tk, :]


def _combine(pos, x1, gate_w, y, tk):
    T = x1.shape[0]
    return pl.pallas_call(
        _combine_kernel,
        out_shape=jax.ShapeDtypeStruct((T, D_MODEL), F32),
        grid=(T // tk,),
        in_specs=[pl.BlockSpec((1, 1, TOP_K * tk), lambda i: (i, 0, 0), memory_space=pltpu.SMEM),
                  pl.BlockSpec((tk, D_MODEL), lambda i: (i, 0)),
                  pl.BlockSpec((tk, TOP_K), lambda i: (i, 0)),
                  pl.BlockSpec(memory_space=pl.ANY)],
        out_specs=pl.BlockSpec((tk, D_MODEL), lambda i: (i, 0)),
        scratch_shapes=[pltpu.VMEM((TOP_K * tk, D_MODEL), F32), pltpu.SemaphoreType.DMA],
        compiler_params=pltpu.CompilerParams(dimension_semantics=("arbitrary",),
                                             vmem_limit_bytes=VMEM_LIMIT),
        name="moe_combine",
    )(_tile_positions(pos, tk), x1, gate_w, y)


def _routing_tables(route, counts, T):
    e = route[0:TOP_K].astype(jnp.int32)
    rank = route[4:4 + TOP_K].astype(jnp.int32)
    counts = counts.astype(jnp.int32)
    starts = jnp.cumsum(counts) - counts
    ids = jnp.arange(N_EXPERTS, dtype=jnp.int32)
    pos = rank + jnp.sum(jnp.where(e[:, :, None] == ids, starts, 0), axis=-1)
    n_rows = TOP_K * T
    cuts = jnp.sort(jnp.concatenate([jnp.arange(0, n_rows, EXPERT_TILE, dtype=jnp.int32), starts[1:]]))
    ends = jnp.concatenate([cuts[1:], jnp.full((1,), n_rows, jnp.int32)])
    tile = jnp.minimum(cuts // EXPERT_TILE, n_rows // EXPERT_TILE - 1)
    exp = jnp.clip(jnp.sum(starts[None, :] <= cuts[:, None], axis=1) - 1, 0, N_EXPERTS - 1).astype(jnp.int32)
    lo = cuts - tile * EXPERT_TILE
    hi = ends - tile * EXPERT_TILE
    return pos.T, (tile, exp, lo, hi)


def _pick_tile(n, pref):
    t = pref
    while n % t:
        t //= 2
    return t


def kernel(x, positions, norm1_w, w_in, q_norm_w, k_norm_w, attn_sinks, hgrn_lower_bounds, hgrn_norm_w,
           w_branch_attn, w_branch_hgrn, w_out, norm2_w, w_router_group, b_router_group, w_router_expert,
           b_router_expert, w_gate_experts, w_up_experts, w_down_experts):
    B, S, D = x.shape
    T = B * S
    x2 = x.reshape(T, D)
    tm = _pick_tile(T, 512)

    inv_freq = ROPE_THETA ** (-jnp.arange(0, ROT_DIM, 2, dtype=F32) / ROT_DIM)
    ang = positions.astype(F32).reshape(T, 1) * inv_freq[None, :]
    cs = jnp.concatenate([jnp.cos(ang), jnp.sin(ang)], axis=-1)

    w_in0 = w_in[0]
    w_in_a = w_in0[:, :_OFF_Z].astype(BF16)
    w_z = w_in0[:, _OFF_Z:].astype(BF16)

    q, k, v, hq, lf, hv, hg = _inproj(x2, norm1_w[0], w_in_a, cs, q_norm_w[0], k_norm_w[0],
                                      hgrn_lower_bounds.astype(F32), tm)
    attn = _attention(q, k, v, attn_sinks[0].astype(F32), B, S)
    hgrn = _hgrn(hq, lf, hv, hg, hgrn_norm_w[0], B, S, _pick_tile(S, 256))

    pad = ROUTER_ROWS - N_GROUPS - N_EXPERTS
    w_r = jnp.concatenate([w_router_group[0].T, w_router_expert[0].T, jnp.zeros((pad, D), F32)], axis=0)
    wr_hi = w_r.astype(BF16)
    wr_lo = (w_r - wr_hi.astype(F32)).astype(BF16)
    b_r = jnp.concatenate([b_router_group[0], b_router_expert[0], jnp.zeros((pad,), F32)]).astype(F32)
    b_r = jnp.broadcast_to(b_r[:, None], (ROUTER_ROWS, tm))

    x1, hn, route, counts = _merge(x2, attn, hgrn, norm1_w[0], w_z, w_branch_attn[0].astype(BF16),
                                   w_branch_hgrn[0].astype(BF16), w_out[0].astype(BF16), norm2_w[0],
                                   wr_hi, wr_lo, b_r, tm)

    pos, items = _routing_tables(route, counts[N_GROUPS:N_GROUPS + N_EXPERTS, 0], T)
    tk = _pick_tile(T, 128)
    xs = _dispatch(pos, hn, tk)
    y = _experts(items, xs, w_gate_experts[0], w_up_experts[0], w_down_experts[0])
    out = _combine(pos, x1, route[2:2 + TOP_K].T, y, tk)
    return out.reshape(B, S, D)
```

```python
import numpy as np
import jax
import jax.numpy as jnp
from jax import lax
from jax.experimental import pallas as pl
from jax.experimental.pallas import tpu as pltpu

F32 = jnp.float32
BF16 = jnp.bfloat16

D_MODEL = 1024
N_Q_HEADS = 8
N_KV_HEADS = 2
GROUP = N_Q_HEADS // N_KV_HEADS
HEAD_DIM = 64
ROT_DIM = HEAD_DIM // 4
ROT_HALF = ROT_DIM // 2
ROPE_THETA = 500000.0
WINDOW = 128
ATTN_WIDTH = N_Q_HEADS * HEAD_DIM
KV_WIDTH = N_KV_HEADS * HEAD_DIM

HGRN_HEADS = 4
HGRN_DIM = 128
HGRN_WIDTH = HGRN_HEADS * HGRN_DIM
CHUNK = 64

N_GROUPS = 4
EXPERTS_PER_GROUP = 8
N_EXPERTS = N_GROUPS * EXPERTS_PER_GROUP
TOP_K = 2
EXPERT_FF = 512
MOE_BLOCK = 128
EXPERT_TILE = 512
NORM_EPS = 1e-6
ROUTER_ROWS = 40
ROUTE_ROWS = 8

LANES = 128
NEG_BIG = -1e30
LOG2_E = 1.4426950408889634

_OFF_Q, _OFF_K, _OFF_V = 0, ATTN_WIDTH, ATTN_WIDTH + KV_WIDTH
_OFF_HQ = ATTN_WIDTH + 2 * KV_WIDTH
_OFF_HF = _OFF_HQ + HGRN_WIDTH
_OFF_HI = _OFF_HF + HGRN_WIDTH
_OFF_HG = _OFF_HI + HGRN_WIDTH
_OFF_Z = _OFF_HG + HGRN_WIDTH

VMEM_LIMIT = 56 * 1024 * 1024


def _split3(a):
    hi = a.astype(BF16)
    r1 = a - hi.astype(F32)
    mid = r1.astype(BF16)
    lo = (r1 - mid.astype(F32)).astype(BF16)
    return hi, mid, lo


def _dot(a, b):
    return jnp.dot(a, b, preferred_element_type=F32)


def _dot_nt(a, b):
    return lax.dot_general(a, b, (((1,), (1,)), ((), ())), preferred_element_type=F32)


def _dot_tn(a, b):
    return lax.dot_general(a, b, (((0,), (0,)), ((), ())), preferred_element_type=F32)


def _exact_lhs_dot(m01, a):
    hi, mid, lo = _split3(a)
    return _dot(m01, hi) + _dot(m01, mid) + _dot(m01, lo)


def _exact_rhs_dot(a, m01):
    hi, mid, lo = _split3(a)
    return _dot(hi, m01) + _dot(mid, m01) + _dot(lo, m01)


def _sigmoid(x):
    return 1.0 / (1.0 + jnp.exp(-x))


def _rms(x, w):
    ms = jnp.mean(x * x, axis=-1, keepdims=True)
    return x * lax.rsqrt(ms + NORM_EPS) * w


def _inproj_kernel(x_ref, n1w_ref, w_ref, cs_ref, rope_e_ref, rope_c0_ref, qw_ref, kw_ref,
                   mq_ref, mk_ref, lbp_ref, dup_ref,
                   q_out, k_out, v_out, hq_out, lf_out, hv_out, hg_out):
    xn = _rms(x_ref[...], n1w_ref[...]).astype(BF16)

    def proj(off, width):
        return _dot(xn, w_ref[:, off:off + width])

    tabs = _exact_rhs_dot(cs_ref[...], rope_e_ref[...])
    c_tab = tabs[:, 0:LANES] + rope_c0_ref[...]
    s1_tab = tabs[:, LANES:2 * LANES]
    s2_tab = tabs[:, 2 * LANES:3 * LANES]

    def norm_rope(t, mavg_ref, w_row, scale):
        ms = _dot((t * t).astype(BF16), mavg_ref[...])
        tn = t * lax.rsqrt(ms + NORM_EPS) * w_row
        if scale != 1.0:
            tn = tn * scale
        outs = []
        for j in range(t.shape[1] // LANES):
            c = tn[:, j * LANES:(j + 1) * LANES]
            outs.append(c * c_tab
                        + pltpu.roll(c, LANES - ROT_HALF, 1) * s1_tab
                        + pltpu.roll(c, ROT_HALF, 1) * s2_tab)
        return outs[0] if len(outs) == 1 else jnp.concatenate(outs, axis=1)

    q_out[...] = norm_rope(proj(_OFF_Q, ATTN_WIDTH), mq_ref, qw_ref[...], HEAD_DIM ** -0.5).astype(BF16)
    k_rot = norm_rope(proj(_OFF_K, KV_WIDTH), mk_ref, kw_ref[...], 1.0).astype(BF16)
    k_out[...] = _dot(k_rot, dup_ref[...]).astype(BF16)
    v_out[...] = _dot(proj(_OFF_V, KV_WIDTH).astype(BF16), dup_ref[...]).astype(BF16)

    hq = proj(_OFF_HQ, HGRN_WIDTH)
    hq_out[...] = (hq * _sigmoid(hq)).astype(BF16)
    h0 = lbp_ref[0:1, :]
    h1 = lbp_ref[1:2, :]
    hm = jnp.maximum(h0, h1)
    e0 = jnp.exp(h0 - hm)
    e1 = jnp.exp(h1 - hm)
    lb = e0 / (e0 + e1)
    fg = lb + (1.0 - lb) * _sigmoid(proj(_OFF_HF, HGRN_WIDTH))
    lf_out[...] = jnp.log(fg) * LOG2_E
    hv_out[...] = proj(_OFF_HI, HGRN_WIDTH).astype(BF16)
    hg = proj(_OFF_HG, HGRN_WIDTH)
    hg_out[...] = (hg * _sigmoid(hg)).astype(BF16)


def _rope_constants():
    e = np.zeros((2 * ROT_HALF, 3 * LANES), np.float32)
    c0 = np.zeros((1, LANES), np.float32)
    for lane in range(LANES):
        d = lane % HEAD_DIM
        if d < ROT_HALF:
            e[d, lane] = 1.0
            e[ROT_HALF + d, LANES + lane] = -1.0
        elif d < ROT_DIM:
            e[d - ROT_HALF, lane] = 1.0
            e[ROT_HALF + d - ROT_HALF, 2 * LANES + lane] = 1.0
        else:
            c0[0, lane] = 1.0
    return jnp.asarray(e, BF16), jnp.asarray(c0, F32)


def _head_mean_matrix(width):
    idx = np.arange(width) // HEAD_DIM
    m = (idx[:, None] == idx[None, :]).astype(np.float32) / HEAD_DIM
    return jnp.asarray(m, BF16)


def _inproj(x2, norm1_w, w_in_a, cs, q_norm_w, k_norm_w, lbp, tm):
    T = x2.shape[0]
    rope_e, rope_c0 = _rope_constants()
    qw = jnp.tile(q_norm_w.astype(F32), N_Q_HEADS)[None, :]
    kw = jnp.tile(k_norm_w.astype(F32), N_KV_HEADS)[None, :]
    mq = _head_mean_matrix(ATTN_WIDTH)
    mk = _head_mean_matrix(KV_WIDTH)
    row = lambda w: pl.BlockSpec((tm, w), lambda i: (i, 0))
    full = lambda a: pl.BlockSpec(a.shape, lambda i: (0,) * a.ndim)
    src = HEAD_DIM * (np.arange(2 * KV_WIDTH) // LANES) + np.arange(2 * KV_WIDTH) % HEAD_DIM
    dup = jnp.asarray(np.arange(KV_WIDTH)[:, None] == src[None, :], BF16)
    ins = [x2, norm1_w[None, :], w_in_a, cs, rope_e, rope_c0, qw, kw, mq, mk, lbp, dup]
    in_specs = [row(D_MODEL), full(ins[1]), full(w_in_a), row(cs.shape[1])] + [full(a) for a in ins[4:]]
    outs = [(ATTN_WIDTH, BF16), (2 * KV_WIDTH, BF16), (2 * KV_WIDTH, BF16), (HGRN_WIDTH, BF16),
            (HGRN_WIDTH, F32), (HGRN_WIDTH, BF16), (HGRN_WIDTH, BF16)]
    return pl.pallas_call(
        _inproj_kernel,
        out_shape=[jax.ShapeDtypeStruct((T, w), dt) for w, dt in outs],
        grid=(T // tm,),
        in_specs=in_specs,
        out_specs=[row(w) for w, _ in outs],
        compiler_params=pltpu.CompilerParams(dimension_semantics=("arbitrary",),
                                             vmem_limit_bytes=VMEM_LIMIT),
        name="inproj",
    )(*ins)


def _attn_kernel(sink_ref, q_ref, kc_ref, kp_ref, vc_ref, vp_ref, half_ref, o_ref):
    has_prev = pl.program_id(1) > 0
    qi = lax.broadcasted_iota(jnp.int32, (WINDOW, 2 * WINDOW), 0)
    kj = lax.broadcasted_iota(jnp.int32, (WINDOW, 2 * WINDOW), 1)
    valid = ((kj < WINDOW) & (kj > qi) & has_prev) | ((kj >= WINDOW) & (kj - WINDOW <= qi))
    left = lax.broadcasted_iota(jnp.int32, (WINDOW, LANES), 1) < HEAD_DIM
    half = (half_ref[0], half_ref[1])
    outs = []
    for h in range(N_KV_HEADS):
        cols = slice(h * LANES, (h + 1) * LANES)
        kcat = jnp.concatenate([kp_ref[:, cols], kc_ref[:, cols]], axis=0)
        vcat = jnp.concatenate([vp_ref[:, cols], vc_ref[:, cols]], axis=0)
        ks = [kcat * hm for hm in half]
        rhs = [jnp.concatenate([vcat * hm, hm], axis=1) for hm in half]
        for pr in range(GROUP // 2):
            pair = h * (GROUP // 2) + pr
            qp = q_ref[:, pair * LANES:(pair + 1) * LANES]
            acc = None
            m_side = []
            for side in range(2):
                s = jnp.where(valid, _dot_nt(qp, ks[side]), NEG_BIG)
                sink = sink_ref[2 * pair + side]
                m = jnp.maximum(jnp.max(s, axis=-1, keepdims=True), sink)
                p = jnp.exp(s - m).astype(BF16)
                d = _dot(p, rhs[side])
                acc = d if acc is None else acc + d
                m_side.append(jnp.exp(sink - m))
            den = acc[:, LANES:2 * LANES] + jnp.where(left, m_side[0], m_side[1])
            outs.append(acc[:, 0:LANES] / den)
    o_ref[...] = jnp.concatenate(outs, axis=1).astype(BF16)


def _attention(q, k, v, sinks, B, S):
    nb = S // WINDOW
    cur = lambda b, n: (b * nb + n, 0)
    prev = lambda b, n: (b * nb + jnp.maximum(n - 1, 0), 0)
    lane_left = np.arange(LANES) < HEAD_DIM
    half = jnp.asarray(np.broadcast_to(np.stack([lane_left, ~lane_left])[:, None, :],
                                       (2, 2 * WINDOW, LANES)), BF16)
    return pl.pallas_call(
        _attn_kernel,
        out_shape=jax.ShapeDtypeStruct((B * S, ATTN_WIDTH), BF16),
        grid=(B, nb),
        in_specs=[pl.BlockSpec(memory_space=pltpu.SMEM),
                  pl.BlockSpec((WINDOW, ATTN_WIDTH), cur),
                  pl.BlockSpec((WINDOW, 2 * KV_WIDTH), cur),
                  pl.BlockSpec((WINDOW, 2 * KV_WIDTH), prev),
                  pl.BlockSpec((WINDOW, 2 * KV_WIDTH), cur),
                  pl.BlockSpec((WINDOW, 2 * KV_WIDTH), prev),
                  pl.BlockSpec(half.shape, lambda b, n: (0, 0, 0))],
        out_specs=pl.BlockSpec((WINDOW, ATTN_WIDTH), cur),
        compiler_params=pltpu.CompilerParams(dimension_semantics=("arbitrary", "arbitrary"),
                                             vmem_limit_bytes=VMEM_LIMIT),
        name="swa_attention",
    )(sinks, q, k, k, v, v, half)


_LEVEL_HALVES = (1, 2, 4, 8, 16, 32)


def _hgrn_level_masks():
    t = np.arange(CHUNK)[:, None]
    s = np.arange(CHUNK)[None, :]
    masks = [((t // (2 * h)) == (s // (2 * h))) & ((t & h) != 0) & ((s & h) == 0) for h in _LEVEL_HALVES]
    return jnp.asarray(np.stack(masks), F32)


def _level_reference(b_ref, slot, half):
    if half >= 4:
        span = max(2 * half, 8)
        pieces = [jnp.broadcast_to(b_ref[slot, s + half - 1:s + half, :], (span, HGRN_DIM))
                  for s in range(0, CHUNK, span)]
    else:
        r8 = lax.broadcasted_iota(jnp.int32, (8, HGRN_DIM), 0)
        pieces = [jnp.where(r8 < 4,
                            jnp.broadcast_to(b_ref[slot, s + 1:s + 2, :], (8, HGRN_DIM)),
                            jnp.broadcast_to(b_ref[slot, s + 5:s + 6, :], (8, HGRN_DIM)))
                  for s in range(0, CHUNK, 8)]
    return pieces[0] if len(pieces) == 1 else jnp.concatenate(pieces, axis=0)


def _hgrn_kernel(hq_ref, lf_ref, hv_ref, hg_ref, nw_ref, tri_ref, lm_ref, o_ref, st_ref, b_ref):
    @pl.when(pl.program_id(1) == 0)
    def _():
        st_ref[...] = jnp.zeros_like(st_ref)

    tri2 = tri_ref[...]
    odd = (lax.broadcasted_iota(jnp.int32, (CHUNK, HGRN_DIM), 0) & 1) != 0
    masks = [lm_ref[li] != 0.0 for li in range(len(_LEVEL_HALVES))]
    units = [(c, h) for c in range(hq_ref.shape[0] // CHUNK) for h in range(HGRN_HEADS)]
    sl = lambda c, h: (slice(c * CHUNK, (c + 1) * CHUNK), slice(h * HGRN_DIM, (h + 1) * HGRN_DIM))

    bs = []
    for u, (c, h) in enumerate(units):
        lf2 = lf_ref[sl(c, h)]
        hi = lf2.astype(BF16)
        lo = (lf2 - hi.astype(F32)).astype(BF16)
        b = _dot(tri2, jnp.concatenate([hi, lo], axis=0))
        b_ref[u] = b
        bs.append(b)

    outs, qs, ks, fs = [], [], [], []
    for u, (c, h) in enumerate(units):
        b = bs[u]
        q = hq_ref[sl(c, h)].astype(F32)
        v_bf = hv_ref[sl(c, h)]
        f = jnp.exp2(lf_ref[sl(c, h)])
        k = 1.0 - f
        b_last = b[CHUNK - 1:CHUNK, :]
        st = st_ref[h]
        o = _dot_nt((q * jnp.exp2(b)).astype(BF16), st.astype(BF16))
        k_out = (k * jnp.exp2(b_last - b)).astype(BF16)
        st_ref[h] = st * jnp.exp2(b_last) + _dot_tn(v_bf, k_out)
        outs.append(o + jnp.sum(q * k, axis=-1, keepdims=True) * v_bf.astype(F32))
        qs.append(q)
        ks.append(k)
        fs.append(f)

    accs = [jnp.zeros((CHUNK, CHUNK), F32) for _ in units]
    for li, half in enumerate(_LEVEL_HALVES):
        for u in range(len(units)):
            if half == 1:
                e = jnp.where(odd, fs[u], 1.0)
            else:
                e = jnp.exp2(-jnp.abs(bs[u] - _level_reference(b_ref, u, half)))
            accs[u] = jnp.where(masks[li], _dot_nt((qs[u] * e).astype(BF16), (ks[u] * e).astype(BF16)), accs[u])

    for u, (c, h) in enumerate(units):
        o = outs[u] + _dot(accs[u].astype(BF16), hv_ref[sl(c, h)])
        y = _rms(o, nw_ref[...]) * hg_ref[sl(c, h)].astype(F32)
        o_ref[sl(c, h)] = y.astype(BF16)


def _hgrn(hq, lf, hv, hg, hgrn_norm_w, B, S, tb):
    nt = S // tb
    blk = pl.BlockSpec((tb, HGRN_WIDTH), lambda b, n: (b * nt + n, 0))
    tri = np.tril(np.ones((CHUNK, CHUNK), np.float32))
    tri = jnp.asarray(np.concatenate([tri, tri], axis=1), BF16)
    lm = _hgrn_level_masks()
    return pl.pallas_call(
        _hgrn_kernel,
        out_shape=jax.ShapeDtypeStruct((B * S, HGRN_WIDTH), BF16),
        grid=(B, nt),
        in_specs=[blk, blk, blk, blk,
                  pl.BlockSpec((1, HGRN_DIM), lambda b, n: (0, 0)),
                  pl.BlockSpec(tri.shape, lambda b, n: (0, 0)),
                  pl.BlockSpec(lm.shape, lambda b, n: (0, 0, 0))],
        out_specs=blk,
        scratch_shapes=[pltpu.VMEM((HGRN_HEADS, HGRN_DIM, HGRN_DIM), F32),
                        pltpu.VMEM((tb // CHUNK * HGRN_HEADS, CHUNK, HGRN_DIM), F32)],
        compiler_params=pltpu.CompilerParams(dimension_semantics=("arbitrary", "arbitrary"),
                                             vmem_limit_bytes=VMEM_LIMIT),
        name="hgrn2",
    )(hq, lf, hv, hg, hgrn_norm_w[None, :].astype(F32), tri, lm)


def _merge_kernel(x_ref, attn_ref, hgrn_ref, n1w_ref, wz_ref, wba_ref, wbh_ref, wout_ref, n2w_ref,
                  wr_hi_ref, wr_lo_ref, br_ref, utri_ref, x1_out, hn_out, route_out, count_out, run_ref):
    @pl.when(pl.program_id(0) == 0)
    def _():
        run_ref[...] = jnp.zeros_like(run_ref)

    x = x_ref[...]
    xn = _rms(x, n1w_ref[...]).astype(BF16)
    za = _sigmoid(_dot(xn, wz_ref[:, 0:D_MODEL]))
    zb = _sigmoid(_dot(xn, wz_ref[:, D_MODEL:2 * D_MODEL]))
    mixed = za * _dot(attn_ref[...], wba_ref[...]) + zb * _dot(hgrn_ref[...], wbh_ref[...])
    x1 = x + _dot(mixed.astype(BF16), wout_ref[...])
    x1_out[...] = x1
    hn = _rms(x1, n2w_ref[...])
    hn_out[...] = hn

    hn_hi = hn.astype(BF16)
    hn_lo = (hn - hn_hi.astype(F32)).astype(BF16)
    logits = (_dot_nt(wr_hi_ref[...], hn_hi) + _dot_nt(wr_hi_ref[...], hn_lo) + _dot_nt(wr_lo_ref[...], hn_hi)
              + br_ref[...])
    r = lax.broadcasted_iota(jnp.int32, logits.shape, 0).astype(F32)
    far = float(ROUTER_ROWS)
    cmax = lambda a: jnp.max(a, axis=0, keepdims=True)
    cmin = lambda a: jnp.min(a, axis=0, keepdims=True)
    csum = lambda a: jnp.sum(a, axis=0, keepdims=True)

    lg = jnp.where(r < N_GROUPS, logits, NEG_BIG)
    mg = cmax(lg)
    gsel = cmin(jnp.where(lg == mg, r, far))
    pgsel = 1.0 / csum(jnp.exp(lg - mg))

    lo = N_GROUPS + EXPERTS_PER_GROUP * gsel
    le = jnp.where((r >= lo) & (r < lo + EXPERTS_PER_GROUP), logits, NEG_BIG)
    m1 = cmax(le)
    i1 = cmin(jnp.where(le == m1, r, far))
    se = csum(jnp.exp(le - m1))
    le2 = jnp.where(r == i1, NEG_BIG, le)
    m2 = cmax(le2)
    i2 = cmin(jnp.where(le2 == m2, r, far))
    top0 = 1.0 / se
    top1 = jnp.exp(m2 - m1) / se
    tsum = top0 + top1
    w0 = pgsel * top0 / tsum
    w1 = pgsel * top1 / tsum

    sel1 = r == i1
    sel2 = r == i2
    onehot = (sel1 | sel2).astype(BF16)
    before = _dot(onehot, utri_ref[...]) + run_ref[...]
    r0 = csum(jnp.where(sel1, before, 0.0))
    r1 = csum(jnp.where(sel2, before, 0.0))
    run_new = run_ref[...] + _dot(onehot, jnp.ones(utri_ref.shape, BF16))
    run_ref[...] = run_new
    count_out[...] = run_new

    row8 = lax.broadcasted_iota(jnp.int32, route_out.shape, 0)
    vals = (i1 - N_GROUPS, i2 - N_GROUPS, w0, w1, r0, r1)
    route = jnp.zeros(route_out.shape, F32)
    for j, val in enumerate(vals):
        route = jnp.where(row8 == j, val, route)
    route_out[...] = route


def _merge(x2, attn, hgrn, norm1_w, w_z, w_ba, w_bh, w_out, norm2_w, wr_hi, wr_lo, br, tm):
    T = x2.shape[0]
    row = lambda w: pl.BlockSpec((tm, w), lambda i: (i, 0))
    full = lambda a: pl.BlockSpec(a.shape, lambda i: (0,) * a.ndim)
    utri = jnp.asarray(np.triu(np.ones((tm, tm), np.float32), 1), BF16)
    ins = [x2, attn, hgrn, norm1_w[None, :], w_z, w_ba, w_bh, w_out, norm2_w[None, :], wr_hi, wr_lo, br, utri]
    in_specs = [row(D_MODEL), row(ATTN_WIDTH), row(HGRN_WIDTH)] + [full(a) for a in ins[3:]]
    return pl.pallas_call(
        _merge_kernel,
        out_shape=[jax.ShapeDtypeStruct((T, D_MODEL), F32), jax.ShapeDtypeStruct((T, D_MODEL), F32),
                   jax.ShapeDtypeStruct((ROUTE_ROWS, T), F32), jax.ShapeDtypeStruct((ROUTER_ROWS, tm), F32)],
        grid=(T // tm,),
        in_specs=in_specs,
        out_specs=[row(D_MODEL), row(D_MODEL), pl.BlockSpec((ROUTE_ROWS, tm), lambda i: (0, i)),
                   pl.BlockSpec((ROUTER_ROWS, tm), lambda i: (0, 0))],
        scratch_shapes=[pltpu.VMEM((ROUTER_ROWS, tm), F32)],
        compiler_params=pltpu.CompilerParams(dimension_semantics=("arbitrary",),
                                             vmem_limit_bytes=VMEM_LIMIT),
        name="merge_router",
    )(*ins)


DMA_PRIORITIES = 2


def _gather_rows(idx_ref, n, src_hbm, dst_ref, sem):
    def issue(i, carry):
        for p in range(DMA_PRIORITIES):
            r = i * DMA_PRIORITIES + p
            pltpu.make_async_copy(src_hbm.at[pl.ds(idx_ref[0, 0, r], 1), :],
                                  dst_ref.at[pl.ds(r, 1), :], sem).start(priority=p)
        return carry
    lax.fori_loop(0, n // DMA_PRIORITIES, issue, 0, unroll=8)
    pltpu.make_async_copy(src_hbm.at[pl.ds(0, n), :], dst_ref.at[pl.ds(0, n), :], sem).wait()


def _dispatch_kernel(pos_ref, hn_ref, xs_hbm, sem):
    tk = hn_ref.shape[0]

    for k in range(TOP_K):
        def issue(i, carry):
            for p in range(DMA_PRIORITIES):
                r = i * DMA_PRIORITIES + p
                pltpu.make_async_copy(hn_ref.at[pl.ds(r, 1), :],
                                      xs_hbm.at[pl.ds(pos_ref[0, 0, k * tk + r], 1), :], sem).start(priority=p)
            return carry
        lax.fori_loop(0, tk // DMA_PRIORITIES, issue, 0, unroll=8)
    for k in range(TOP_K):
        pltpu.make_async_copy(hn_ref, xs_hbm.at[pl.ds(0, tk), :], sem).wait()


def _tile_positions(pos, tk):
    nt = pos.shape[0] // tk
    return pos.reshape(nt, tk, TOP_K).transpose(0, 2, 1).reshape(nt, 1, TOP_K * tk)


def _dispatch(pos, hn, tk):
    T = hn.shape[0]
    return pl.pallas_call(
        _dispatch_kernel,
        out_shape=jax.ShapeDtypeStruct((TOP_K * T, D_MODEL), F32),
        grid=(T // tk,),
        in_specs=[pl.BlockSpec((1, 1, TOP_K * tk), lambda i: (i, 0, 0), memory_space=pltpu.SMEM),
                  pl.BlockSpec((tk, D_MODEL), lambda i: (i, 0))],
        out_specs=pl.BlockSpec(memory_space=pl.ANY),
        scratch_shapes=[pltpu.SemaphoreType.DMA],
        compiler_params=pltpu.CompilerParams(dimension_semantics=("arbitrary",),
                                             vmem_limit_bytes=VMEM_LIMIT),
        name="moe_dispatch",
    )(_tile_positions(pos, tk), hn)


def _expert_kernel(tile_ref, exp_ref, lo_ref, hi_ref, x_ref, wg_ref, wu_ref, wd_ref, y_ref,
                   wg_s, wu_s, wd_s):
    w = pl.program_id(0)
    lo = lo_ref[w]
    hi = hi_ref[w]
    changed = (w == 0) | (exp_ref[w] != exp_ref[jnp.maximum(w - 1, 0)])

    @pl.when(changed)
    def _():
        wg_s[...] = wg_ref[0].astype(BF16)
        wu_s[...] = wu_ref[0].astype(BF16)
        wd_s[...] = wd_ref[0].astype(BF16)

    def ffn(x):
        xb = x.astype(BF16)
        gate = _dot(xb, wg_s[...])
        up = _dot(xb, wu_s[...])
        return _dot((gate * _sigmoid(gate) * up).astype(BF16), wd_s[...])

    whole = (lo == 0) & (hi == EXPERT_TILE)

    @pl.when(whole)
    def _():
        y_ref[...] = ffn(x_ref[...])

    for j in range(EXPERT_TILE // MOE_BLOCK):
        r0 = j * MOE_BLOCK
        rows = slice(r0, r0 + MOE_BLOCK)

        @pl.when(jnp.logical_not(whole) & (lo < r0 + MOE_BLOCK) & (hi > r0))
        def _():
            y = ffn(x_ref[rows, :])
            rowi = lax.broadcasted_iota(jnp.int32, y.shape, 0) + r0
            mine = (rowi >= lo) & (rowi < hi)

            @pl.when(lo <= r0)
            def _():
                y_ref[rows, :] = jnp.where(mine, y, 0.0)

            @pl.when(lo > r0)
            def _():
                y_ref[rows, :] = jnp.where(mine, y, y_ref[rows, :])


def _experts(items, xs, w_gate, w_up, w_down):
    tile, exp, lo, hi = items
    wspec = lambda shape: pl.BlockSpec((1,) + shape, lambda w, t, e, l, h: (e[w], 0, 0))
    xspec = pl.BlockSpec((EXPERT_TILE, D_MODEL), lambda w, t, e, l, h: (t[w], 0))
    return pl.pallas_call(
        _expert_kernel,
        out_shape=jax.ShapeDtypeStruct(xs.shape, F32),
        grid_spec=pltpu.PrefetchScalarGridSpec(
            num_scalar_prefetch=4,
            grid=(tile.shape[0],),
            in_specs=[xspec, wspec((D_MODEL, EXPERT_FF)), wspec((D_MODEL, EXPERT_FF)),
                      wspec((EXPERT_FF, D_MODEL))],
            out_specs=xspec,
            scratch_shapes=[pltpu.VMEM((D_MODEL, EXPERT_FF), BF16),
                            pltpu.VMEM((D_MODEL, EXPERT_FF), BF16),
                            pltpu.VMEM((EXPERT_FF, D_MODEL), BF16)]),
        compiler_params=pltpu.CompilerParams(dimension_semantics=("arbitrary",),
                                             vmem_limit_bytes=VMEM_LIMIT),
        name="moe_experts",
    )(tile, exp, lo, hi, xs, w_gate, w_up, w_down)


def _combine_kernel(pos_ref, x1_ref, w_ref, y_hbm, o_ref, ybuf, sem):
    tk = x1_ref.shape[0]
    _gather_rows(pos_ref, TOP_K * tk, y_hbm, ybuf, sem)
    o_ref[...] = x1_ref[...] + w_ref[:, 0:1] * ybuf[0:tk, :] + w_ref[:, 1:2] * ybuf[tk:2 * tk, :]


def _combine(pos, x1, gate_w, y, tk):
    T = x1.shape[0]
    return pl.pallas_call(
        _combine_kernel,
        out_shape=jax.ShapeDtypeStruct((T, D_MODEL), F32),
        grid=(T // tk,),
        in_specs=[pl.BlockSpec((1, 1, TOP_K * tk), lambda i: (i, 0, 0), memory_space=pltpu.SMEM),
                  pl.BlockSpec((tk, D_MODEL), lambda i: (i, 0)),
                  pl.BlockSpec((tk, TOP_K), lambda i: (i, 0)),
                  pl.BlockSpec(memory_space=pl.ANY)],
        out_specs=pl.BlockSpec((tk, D_MODEL), lambda i: (i, 0)),
        scratch_shapes=[pltpu.VMEM((TOP_K * tk, D_MODEL), F32), pltpu.SemaphoreType.DMA],
        compiler_params=pltpu.CompilerParams(dimension_semantics=("arbitrary",),
                                             vmem_limit_bytes=VMEM_LIMIT),
        name="moe_combine",
    )(_tile_positions(pos, tk), x1, gate_w, y)


def _routing_tables(route, counts, T):
    e = route[0:TOP_K].astype(jnp.int32)
    rank = route[4:4 + TOP_K].astype(jnp.int32)
    counts = counts.astype(jnp.int32)
    starts = jnp.cumsum(counts) - counts
    ids = jnp.arange(N_EXPERTS, dtype=jnp.int32)
    pos = rank + jnp.sum(jnp.where(e[:, :, None] == ids, starts, 0), axis=-1)
    n_rows = TOP_K * T
    cuts = jnp.sort(jnp.concatenate([jnp.arange(0, n_rows, EXPERT_TILE, dtype=jnp.int32), starts[1:]]))
    ends = jnp.concatenate([cuts[1:], jnp.full((1,), n_rows, jnp.int32)])
    tile = jnp.minimum(cuts // EXPERT_TILE, n_rows // EXPERT_TILE - 1)
    exp = jnp.clip(jnp.sum(starts[None, :] <= cuts[:, None], axis=1) - 1, 0, N_EXPERTS - 1).astype(jnp.int32)
    lo = cuts - tile * EXPERT_TILE
    hi = ends - tile * EXPERT_TILE
    return pos.T, (tile, exp, lo, hi)


def _pick_tile(n, pref):
    t = pref
    while n % t:
        t //= 2
    return t


def kernel(x, positions, norm1_w, w_in, q_norm_w, k_norm_w, attn_sinks, hgrn_lower_bounds, hgrn_norm_w,
           w_branch_attn, w_branch_hgrn, w_out, norm2_w, w_router_group, b_router_group, w_router_expert,
           b_router_expert, w_gate_experts, w_up_experts, w_down_experts):
    B, S, D = x.shape
    T = B * S
    x2 = x.reshape(T, D)
    tm = _pick_tile(T, 512)

    inv_freq = ROPE_THETA ** (-jnp.arange(0, ROT_DIM, 2, dtype=F32) / ROT_DIM)
    ang = positions.astype(F32).reshape(T, 1) * inv_freq[None, :]
    cs = jnp.concatenate([jnp.cos(ang), jnp.sin(ang)], axis=-1)

    w_in0 = w_in[0]
    w_in_a = w_in0[:, :_OFF_Z].astype(BF16)
    w_z = w_in0[:, _OFF_Z:].astype(BF16)

    q, k, v, hq, lf, hv, hg = _inproj(x2, norm1_w[0], w_in_a, cs, q_norm_w[0], k_norm_w[0],
                                      hgrn_lower_bounds.astype(F32), tm)
    attn = _attention(q, k, v, attn_sinks[0].astype(F32), B, S)
    hgrn = _hgrn(hq, lf, hv, hg, hgrn_norm_w[0], B, S, _pick_tile(S, 256))

    pad = ROUTER_ROWS - N_GROUPS - N_EXPERTS
    w_r = jnp.concatenate([w_router_group[0].T, w_router_expert[0].T, jnp.zeros((pad, D), F32)], axis=0)
    wr_hi = w_r.astype(BF16)
    wr_lo = (w_r - wr_hi.astype(F32)).astype(BF16)
    b_r = jnp.concatenate([b_router_group[0], b_router_expert[0], jnp.zeros((pad,), F32)]).astype(F32)
    b_r = jnp.broadcast_to(b_r[:, None], (ROUTER_ROWS, tm))

    x1, hn, route, counts = _merge(x2, attn, hgrn, norm1_w[0], w_z, w_branch_attn[0].astype(BF16),
                                   w_branch_hgrn[0].astype(BF16), w_out[0].astype(BF16), norm2_w[0],
                                   wr_hi, wr_lo, b_r, tm)

    pos, items = _routing_tables(route, counts[N_GROUPS:N_GROUPS + N_EXPERTS, 0], T)
    tk = _pick_tile(T, 128)
    xs = _dispatch(pos, hn, tk)
    y = _experts(items, xs, w_gate_experts[0], w_up_experts[0], w_down_experts[0])
    out = _combine(pos, x1, route[2:2 + TOP_K].T, y, tk)
    return out.reshape(B, S, D)
```

```python
import numpy as np
import jax
import jax.numpy as jnp
from jax import lax
from jax.experimental import pallas as pl
from jax.experimental.pallas import tpu as pltpu

F32 = jnp.float32
BF16 = jnp.bfloat16

D_MODEL = 1024
N_Q_HEADS = 8
N_KV_HEADS = 2
GROUP = N_Q_HEADS // N_KV_HEADS
HEAD_DIM = 64
ROT_DIM = HEAD_DIM // 4
ROT_HALF = ROT_DIM // 2
ROPE_THETA = 500000.0
WINDOW = 128
ATTN_WIDTH = N_Q_HEADS * HEAD_DIM
KV_WIDTH = N_KV_HEADS * HEAD_DIM

HGRN_HEADS = 4
HGRN_DIM = 128
HGRN_WIDTH = HGRN_HEADS * HGRN_DIM
CHUNK = 64

N_GROUPS = 4
EXPERTS_PER_GROUP = 8
N_EXPERTS = N_GROUPS * EXPERTS_PER_GROUP
TOP_K = 2
EXPERT_FF = 512
MOE_BLOCK = 128
EXPERT_TILE = 512
NORM_EPS = 1e-6
ROUTER_ROWS = 40
ROUTE_ROWS = 8

LANES = 128
NEG_BIG = -1e30
LOG2_E = 1.4426950408889634

_OFF_Q, _OFF_K, _OFF_V = 0, ATTN_WIDTH, ATTN_WIDTH + KV_WIDTH
_OFF_HQ = ATTN_WIDTH + 2 * KV_WIDTH
_OFF_HF = _OFF_HQ + HGRN_WIDTH
_OFF_HI = _OFF_HF + HGRN_WIDTH
_OFF_HG = _OFF_HI + HGRN_WIDTH
_OFF_Z = _OFF_HG + HGRN_WIDTH

VMEM_LIMIT = 56 * 1024 * 1024


def _split3(a):
    hi = a.astype(BF16)
    r1 = a - hi.astype(F32)
    mid = r1.astype(BF16)
    lo = (r1 - mid.astype(F32)).astype(BF16)
    return hi, mid, lo


def _dot(a, b):
    return jnp.dot(a, b, preferred_element_type=F32)


def _dot_nt(a, b):
    return lax.dot_general(a, b, (((1,), (1,)), ((), ())), preferred_element_type=F32)


def _dot_tn(a, b):
    return lax.dot_general(a, b, (((0,), (0,)), ((), ())), preferred_element_type=F32)


def _exact_lhs_dot(m01, a):
    hi, mid, lo = _split3(a)
    return _dot(m01, hi) + _dot(m01, mid) + _dot(m01, lo)


def _exact_rhs_dot(a, m01):
    hi, mid, lo = _split3(a)
    return _dot(hi, m01) + _dot(mid, m01) + _dot(lo, m01)


def _sigmoid(x):
    return 1.0 / (1.0 + jnp.exp(-x))


def _rms(x, w):
    ms = jnp.mean(x * x, axis=-1, keepdims=True)
    return x * lax.rsqrt(ms + NORM_EPS) * w


def _inproj_kernel(x_ref, n1w_ref, w_ref, cs_ref, rope_e_ref, rope_c0_ref, qw_ref, kw_ref,
                   mq_ref, mk_ref, lbp_ref, dup_ref,
                   q_out, k_out, v_out, hq_out, lf_out, hv_out, hg_out):
    xn = _rms(x_ref[...], n1w_ref[...]).astype(BF16)

    def proj(off, width):
        return _dot(xn, w_ref[:, off:off + width])

    tabs = _exact_rhs_dot(cs_ref[...], rope_e_ref[...])
    c_tab = tabs[:, 0:LANES] + rope_c0_ref[...]
    s1_tab = tabs[:, LANES:2 * LANES]
    s2_tab = tabs[:, 2 * LANES:3 * LANES]

    def norm_rope(t, mavg_ref, w_row, scale):
        ms = _dot((t * t).astype(BF16), mavg_ref[...])
        tn = t * lax.rsqrt(ms + NORM_EPS) * w_row
        if scale != 1.0:
            tn = tn * scale
        outs = []
        for j in range(t.shape[1] // LANES):
            c = tn[:, j * LANES:(j + 1) * LANES]
            outs.append(c * c_tab
                        + pltpu.roll(c, LANES - ROT_HALF, 1) * s1_tab
                        + pltpu.roll(c, ROT_HALF, 1) * s2_tab)
        return outs[0] if len(outs) == 1 else jnp.concatenate(outs, axis=1)

    q_out[...] = norm_rope(proj(_OFF_Q, ATTN_WIDTH), mq_ref, qw_ref[...], HEAD_DIM ** -0.5).astype(BF16)
    k_rot = norm_rope(proj(_OFF_K, KV_WIDTH), mk_ref, kw_ref[...], 1.0).astype(BF16)
    k_out[...] = _dot(k_rot, dup_ref[...]).astype(BF16)
    v_out[...] = _dot(proj(_OFF_V, KV_WIDTH).astype(BF16), dup_ref[...]).astype(BF16)

    hq = proj(_OFF_HQ, HGRN_WIDTH)
    hq_out[...] = (hq * _sigmoid(hq)).astype(BF16)
    h0 = lbp_ref[0:1, :]
    h1 = lbp_ref[1:2, :]
    hm = jnp.maximum(h0, h1)
    e0 = jnp.exp(h0 - hm)
    e1 = jnp.exp(h1 - hm)
    lb = e0 / (e0 + e1)
    fg = lb + (1.0 - lb) * _sigmoid(proj(_OFF_HF, HGRN_WIDTH))
    lf_out[...] = jnp.log(fg) * LOG2_E
    hv_out[...] = proj(_OFF_HI, HGRN_WIDTH).astype(BF16)
    hg = proj(_OFF_HG, HGRN_WIDTH)
    hg_out[...] = (hg * _sigmoid(hg)).astype(BF16)


def _rope_constants():
    e = np.zeros((2 * ROT_HALF, 3 * LANES), np.float32)
    c0 = np.zeros((1, LANES), np.float32)
    for lane in range(LANES):
        d = lane % HEAD_DIM
        if d < ROT_HALF:
            e[d, lane] = 1.0
            e[ROT_HALF + d, LANES + lane] = -1.0
        elif d < ROT_DIM:
            e[d - ROT_HALF, lane] = 1.0
            e[ROT_HALF + d - ROT_HALF, 2 * LANES + lane] = 1.0
        else:
            c0[0, lane] = 1.0
    return jnp.asarray(e, BF16), jnp.asarray(c0, F32)


def _head_mean_matrix(width):
    idx = np.arange(width) // HEAD_DIM
    m = (idx[:, None] == idx[None, :]).astype(np.float32) / HEAD_DIM
    return jnp.asarray(m, BF16)


def _inproj(x2, norm1_w, w_in_a, cs, q_norm_w, k_norm_w, lbp, tm):
    T = x2.shape[0]
    rope_e, rope_c0 = _rope_constants()
    qw = jnp.tile(q_norm_w.astype(F32), N_Q_HEADS)[None, :]
    kw = jnp.tile(k_norm_w.astype(F32), N_KV_HEADS)[None, :]
    mq = _head_mean_matrix(ATTN_WIDTH)
    mk = _head_mean_matrix(KV_WIDTH)
    row = lambda w: pl.BlockSpec((tm, w), lambda i: (i, 0))
    full = lambda a: pl.BlockSpec(a.shape, lambda i: (0,) * a.ndim)
    src = HEAD_DIM * (np.arange(2 * KV_WIDTH) // LANES) + np.arange(2 * KV_WIDTH) % HEAD_DIM
    dup = jnp.asarray(np.arange(KV_WIDTH)[:, None] == src[None, :], BF16)
    ins = [x2, norm1_w[None, :], w_in_a, cs, rope_e, rope_c0, qw, kw, mq, mk, lbp, dup]
    in_specs = [row(D_MODEL), full(ins[1]), full(w_in_a), row(cs.shape[1])] + [full(a) for a in ins[4:]]
    outs = [(ATTN_WIDTH, BF16), (2 * KV_WIDTH, BF16), (2 * KV_WIDTH, BF16), (HGRN_WIDTH, BF16),
            (HGRN_WIDTH, F32), (HGRN_WIDTH, BF16), (HGRN_WIDTH, BF16)]
    return pl.pallas_call(
        _inproj_kernel,
        out_shape=[jax.ShapeDtypeStruct((T, w), dt) for w, dt in outs],
        grid=(T // tm,),
        in_specs=in_specs,
        out_specs=[row(w) for w, _ in outs],
        compiler_params=pltpu.CompilerParams(dimension_semantics=("arbitrary",),
                                             vmem_limit_bytes=VMEM_LIMIT),
        name="inproj",
    )(*ins)


def _attn_kernel(sink_ref, q_ref, kc_ref, kp_ref, vc_ref, vp_ref, half_ref, o_ref):
    has_prev = pl.program_id(1) > 0
    qi = lax.broadcasted_iota(jnp.int32, (WINDOW, 2 * WINDOW), 0)
    kj = lax.broadcasted_iota(jnp.int32, (WINDOW, 2 * WINDOW), 1)
    valid = ((kj < WINDOW) & (kj > qi) & has_prev) | ((kj >= WINDOW) & (kj - WINDOW <= qi))
    left = lax.broadcasted_iota(jnp.int32, (WINDOW, LANES), 1) < HEAD_DIM
    half = (half_ref[0], half_ref[1])
    outs = []
    for h in range(N_KV_HEADS):
        cols = slice(h * LANES, (h + 1) * LANES)
        kcat = jnp.concatenate([kp_ref[:, cols], kc_ref[:, cols]], axis=0)
        vcat = jnp.concatenate([vp_ref[:, cols], vc_ref[:, cols]], axis=0)
        ks = [kcat * hm for hm in half]
        rhs = [jnp.concatenate([vcat * hm, hm], axis=1) for hm in half]
        for pr in range(GROUP // 2):
            pair = h * (GROUP // 2) + pr
            qp = q_ref[:, pair * LANES:(pair + 1) * LANES]
            acc = None
            m_side = []
            for side in range(2):
                s = jnp.where(valid, _dot_nt(qp, ks[side]), NEG_BIG)
                sink = sink_ref[2 * pair + side]
                m = jnp.maximum(jnp.max(s, axis=-1, keepdims=True), sink)
                p = jnp.exp(s - m).astype(BF16)
                d = _dot(p, rhs[side])
                acc = d if acc is None else acc + d
                m_side.append(jnp.exp(sink - m))
            den = acc[:, LANES:2 * LANES] + jnp.where(left, m_side[0], m_side[1])
            outs.append(acc[:, 0:LANES] / den)
    o_ref[...] = jnp.concatenate(outs, axis=1).astype(BF16)


def _attention(q, k, v, sinks, B, S):
    nb = S // WINDOW
    cur = lambda b, n: (b * nb + n, 0)
    prev = lambda b, n: (b * nb + jnp.maximum(n - 1, 0), 0)
    lane_left = np.arange(LANES) < HEAD_DIM
    half = jnp.asarray(np.broadcast_to(np.stack([lane_left, ~lane_left])[:, None, :],
                                       (2, 2 * WINDOW, LANES)), BF16)
    return pl.pallas_call(
        _attn_kernel,
        out_shape=jax.ShapeDtypeStruct((B * S, ATTN_WIDTH), BF16),
        grid=(B, nb),
        in_specs=[pl.BlockSpec(memory_space=pltpu.SMEM),
                  pl.BlockSpec((WINDOW, ATTN_WIDTH), cur),
                  pl.BlockSpec((WINDOW, 2 * KV_WIDTH), cur),
                  pl.BlockSpec((WINDOW, 2 * KV_WIDTH), prev),
                  pl.BlockSpec((WINDOW, 2 * KV_WIDTH), cur),
                  pl.BlockSpec((WINDOW, 2 * KV_WIDTH), prev),
                  pl.BlockSpec(half.shape, lambda b, n: (0, 0, 0))],
        out_specs=pl.BlockSpec((WINDOW, ATTN_WIDTH), cur),
        compiler_params=pltpu.CompilerParams(dimension_semantics=("arbitrary", "arbitrary"),
                                             vmem_limit_bytes=VMEM_LIMIT),
        name="swa_attention",
    )(sinks, q, k, k, v, v, half)


_LEVEL_HALVES = (1, 2, 4, 8, 16, 32)


def _hgrn_level_masks():
    t = np.arange(CHUNK)[:, None]
    s = np.arange(CHUNK)[None, :]
    masks = [((t // (2 * h)) == (s // (2 * h))) & ((t & h) != 0) & ((s & h) == 0) for h in _LEVEL_HALVES]
    return jnp.asarray(np.stack(masks), F32)


def _level_reference(b_ref, slot, half):
    if half >= 4:
        span = max(2 * half, 8)
        pieces = [jnp.broadcast_to(b_ref[slot, s + half - 1:s + half, :], (span, HGRN_DIM))
                  for s in range(0, CHUNK, span)]
    else:
        r8 = lax.broadcasted_iota(jnp.int32, (8, HGRN_DIM), 0)
        pieces = [jnp.where(r8 < 4,
                            jnp.broadcast_to(b_ref[slot, s + 1:s + 2, :], (8, HGRN_DIM)),
                            jnp.broadcast_to(b_ref[slot, s + 5:s + 6, :], (8, HGRN_DIM)))
                  for s in range(0, CHUNK, 8)]
    return pieces[0] if len(pieces) == 1 else jnp.concatenate(pieces, axis=0)


def _hgrn_kernel(hq_ref, lf_ref, hv_ref, hg_ref, nw_ref, tri_ref, lm_ref, o_ref, st_ref, b_ref):
    @pl.when(pl.program_id(1) == 0)
    def _():
        st_ref[...] = jnp.zeros_like(st_ref)

    tri2 = tri_ref[...]
    odd = (lax.broadcasted_iota(jnp.int32, (CHUNK, HGRN_DIM), 0) & 1) != 0
    masks = [lm_ref[li] != 0.0 for li in range(len(_LEVEL_HALVES))]
    units = [(c, h) for c in range(hq_ref.shape[0] // CHUNK) for h in range(HGRN_HEADS)]
    sl = lambda c, h: (slice(c * CHUNK, (c + 1) * CHUNK), slice(h * HGRN_DIM, (h + 1) * HGRN_DIM))

    bs = []
    for u, (c, h) in enumerate(units):
        lf2 = lf_ref[sl(c, h)]
        hi = lf2.astype(BF16)
        lo = (lf2 - hi.astype(F32)).astype(BF16)
        b = _dot(tri2, jnp.concatenate([hi, lo], axis=0))
        b_ref[u] = b
        bs.append(b)

    outs, qs, ks, fs = [], [], [], []
    for u, (c, h) in enumerate(units):
        b = bs[u]
        q = hq_ref[sl(c, h)].astype(F32)
        v_bf = hv_ref[sl(c, h)]
        f = jnp.exp2(lf_ref[sl(c, h)])
        k = 1.0 - f
        b_last = b[CHUNK - 1:CHUNK, :]
        st = st_ref[h]
        o = _dot_nt((q * jnp.exp2(b)).astype(BF16), st.astype(BF16))
        k_out = (k * jnp.exp2(b_last - b)).astype(BF16)
        st_ref[h] = st * jnp.exp2(b_last) + _dot_tn(v_bf, k_out)
        outs.append(o + jnp.sum(q * k, axis=-1, keepdims=True) * v_bf.astype(F32))
        qs.append(q)
        ks.append(k)
        fs.append(f)

    accs = [jnp.zeros((CHUNK, CHUNK), F32) for _ in units]
    for li, half in enumerate(_LEVEL_HALVES):
        for u in range(len(units)):
            if half == 1:
                e = jnp.where(odd, fs[u], 1.0)
            else:
                e = jnp.exp2(-jnp.abs(bs[u] - _level_reference(b_ref, u, half)))
            accs[u] = jnp.where(masks[li], _dot_nt((qs[u] * e).astype(BF16), (ks[u] * e).astype(BF16)), accs[u])

    for u, (c, h) in enumerate(units):
        o = outs[u] + _dot(accs[u].astype(BF16), hv_ref[sl(c, h)])
        y = _rms(o, nw_ref[...]) * hg_ref[sl(c, h)].astype(F32)
        o_ref[sl(c, h)] = y.astype(BF16)


def _hgrn(hq, lf, hv, hg, hgrn_norm_w, B, S, tb):
    nt = S // tb
    blk = pl.BlockSpec((tb, HGRN_WIDTH), lambda b, n: (b * nt + n, 0))
    tri = np.tril(np.ones((CHUNK, CHUNK), np.float32))
    tri = jnp.asarray(np.concatenate([tri, tri], axis=1), BF16)
    lm = _hgrn_level_masks()
    return pl.pallas_call(
        _hgrn_kernel,
        out_shape=jax.ShapeDtypeStruct((B * S, HGRN_WIDTH), BF16),
        grid=(B, nt),
        in_specs=[blk, blk, blk, blk,
                  pl.BlockSpec((1, HGRN_DIM), lambda b, n: (0, 0)),
                  pl.BlockSpec(tri.shape, lambda b, n: (0, 0)),
                  pl.BlockSpec(lm.shape, lambda b, n: (0, 0, 0))],
        out_specs=blk,
        scratch_shapes=[pltpu.VMEM((HGRN_HEADS, HGRN_DIM, HGRN_DIM), F32),
                        pltpu.VMEM((tb // CHUNK * HGRN_HEADS, CHUNK, HGRN_DIM), F32)],
        compiler_params=pltpu.CompilerParams(dimension_semantics=("arbitrary", "arbitrary"),
                                             vmem_limit_bytes=VMEM_LIMIT),
        name="hgrn2",
    )(hq, lf, hv, hg, hgrn_norm_w[None, :].astype(F32), tri, lm)


def _merge_kernel(x_ref, attn_ref, hgrn_ref, n1w_ref, wz_ref, wba_ref, wbh_ref, wout_ref, n2w_ref,
                  wr_hi_ref, wr_lo_ref, br_ref, utri_ref, x1_out, hn_out, route_out, count_out, run_ref):
    @pl.when(pl.program_id(0) == 0)
    def _():
        run_ref[...] = jnp.zeros_like(run_ref)

    x = x_ref[...]
    xn = _rms(x, n1w_ref[...]).astype(BF16)
    za = _sigmoid(_dot(xn, wz_ref[:, 0:D_MODEL]))
    zb = _sigmoid(_dot(xn, wz_ref[:, D_MODEL:2 * D_MODEL]))
    mixed = za * _dot(attn_ref[...], wba_ref[...]) + zb * _dot(hgrn_ref[...], wbh_ref[...])
    x1 = x + _dot(mixed.astype(BF16), wout_ref[...])
    x1_out[...] = x1
    hn = _rms(x1, n2w_ref[...])
    hn_out[...] = hn

    hn_hi = hn.astype(BF16)
    hn_lo = (hn - hn_hi.astype(F32)).astype(BF16)
    logits = (_dot_nt(wr_hi_ref[...], hn_hi) + _dot_nt(wr_hi_ref[...], hn_lo) + _dot_nt(wr_lo_ref[...], hn_hi)
              + br_ref[...])
    r = lax.broadcasted_iota(jnp.int32, logits.shape, 0).astype(F32)
    far = float(ROUTER_ROWS)
    cmax = lambda a: jnp.max(a, axis=0, keepdims=True)
    cmin = lambda a: jnp.min(a, axis=0, keepdims=True)
    csum = lambda a: jnp.sum(a, axis=0, keepdims=True)

    lg = jnp.where(r < N_GROUPS, logits, NEG_BIG)
    mg = cmax(lg)
    gsel = cmin(jnp.where(lg == mg, r, far))
    pgsel = 1.0 / csum(jnp.exp(lg - mg))

    lo = N_GROUPS + EXPERTS_PER_GROUP * gsel
    le = jnp.where((r >= lo) & (r < lo + EXPERTS_PER_GROUP), logits, NEG_BIG)
    m1 = cmax(le)
    i1 = cmin(jnp.where(le == m1, r, far))
    se = csum(jnp.exp(le - m1))
    le2 = jnp.where(r == i1, NEG_BIG, le)
    m2 = cmax(le2)
    i2 = cmin(jnp.where(le2 == m2, r, far))
    top0 = 1.0 / se
    top1 = jnp.exp(m2 - m1) / se
    tsum = top0 + top1
    w0 = pgsel * top0 / tsum
    w1 = pgsel * top1 / tsum

    sel1 = r == i1
    sel2 = r == i2
    onehot = (sel1 | sel2).astype(BF16)
    before = _dot(onehot, utri_ref[...]) + run_ref[...]
    r0 = csum(jnp.where(sel1, before, 0.0))
    r1 = csum(jnp.where(sel2, before, 0.0))
    run_new = run_ref[...] + _dot(onehot, jnp.ones(utri_ref.shape, BF16))
    run_ref[...] = run_new
    count_out[...] = run_new

    row8 = lax.broadcasted_iota(jnp.int32, route_out.shape, 0)
    vals = (i1 - N_GROUPS, i2 - N_GROUPS, w0, w1, r0, r1)
    route = jnp.zeros(route_out.shape, F32)
    for j, val in enumerate(vals):
        route = jnp.where(row8 == j, val, route)
    route_out[...] = route


def _merge(x2, attn, hgrn, norm1_w, w_z, w_ba, w_bh, w_out, norm2_w, wr_hi, wr_lo, br, tm):
    T = x2.shape[0]
    row = lambda w: pl.BlockSpec((tm, w), lambda i: (i, 0))
    full = lambda a: pl.BlockSpec(a.shape, lambda i: (0,) * a.ndim)
    utri = jnp.asarray(np.triu(np.ones((tm, tm), np.float32), 1), BF16)
    ins = [x2, attn, hgrn, norm1_w[None, :], w_z, w_ba, w_bh, w_out, norm2_w[None, :], wr_hi, wr_lo, br, utri]
    in_specs = [row(D_MODEL), row(ATTN_WIDTH), row(HGRN_WIDTH)] + [full(a) for a in ins[3:]]
    return pl.pallas_call(
        _merge_kernel,
        out_shape=[jax.ShapeDtypeStruct((T, D_MODEL), F32), jax.ShapeDtypeStruct((T, D_MODEL), F32),
                   jax.ShapeDtypeStruct((ROUTE_ROWS, T), F32), jax.ShapeDtypeStruct((ROUTER_ROWS, tm), F32)],
        grid=(T // tm,),
        in_specs=in_specs,
        out_specs=[row(D_MODEL), row(D_MODEL), pl.BlockSpec((ROUTE_ROWS, tm), lambda i: (0, i)),
                   pl.BlockSpec((ROUTER_ROWS, tm), lambda i: (0, 0))],
        scratch_shapes=[pltpu.VMEM((ROUTER_ROWS, tm), F32)],
        compiler_params=pltpu.CompilerParams(dimension_semantics=("arbitrary",),
                                             vmem_limit_bytes=VMEM_LIMIT),
        name="merge_router",
    )(*ins)


def _issue_row_gather(idx_ref, n, src_hbm, dst_ref, sem):
    def issue(r, carry):
        pltpu.make_async_copy(src_hbm.at[pl.ds(idx_ref[0, 0, r], 1), :],
                              dst_ref.at[pl.ds(r, 1), :], sem).start()
        return carry
    lax.fori_loop(0, n, issue, 0, unroll=16)


def _wait_row_gather(n, src_hbm, dst_ref, sem):
    pltpu.make_async_copy(src_hbm.at[pl.ds(0, n), :], dst_ref.at[pl.ds(0, n), :], sem).wait()


def _gather_rows(idx_ref, n, src_hbm, dst_ref, sem):
    _issue_row_gather(idx_ref, n, src_hbm, dst_ref, sem)
    _wait_row_gather(n, src_hbm, dst_ref, sem)


def _tile_positions(pos, tk):
    nt = pos.shape[0] // tk
    return pos.reshape(nt, tk, TOP_K).transpose(0, 2, 1).reshape(nt, 1, TOP_K * tk)


def _invmap_kernel(pos_ref, tok_ref):
    tokens = pos_ref.shape[2] // TOP_K
    base = pl.program_id(0) * tokens

    def body(j, carry):
        for k in range(TOP_K):
            tok_ref[pos_ref[0, 0, j * TOP_K + k]] = base + j
        return carry
    lax.fori_loop(0, tokens, body, 0, unroll=8)


def _invmap(pos, nb):
    n = pos.size
    return pl.pallas_call(
        _invmap_kernel,
        out_shape=jax.ShapeDtypeStruct((n,), jnp.int32),
        grid=(n // nb,),
        in_specs=[pl.BlockSpec((1, 1, nb), lambda i: (i, 0, 0), memory_space=pltpu.SMEM)],
        out_specs=pl.BlockSpec(memory_space=pltpu.SMEM),
        compiler_params=pltpu.CompilerParams(dimension_semantics=("arbitrary",)),
        name="moe_row_tokens",
    )(pos.reshape(n // nb, 1, nb))


def _expert_kernel(tile_ref, exp_ref, lo_ref, hi_ref, tok_ref, tokn_ref, hn_hbm, wg_ref, wu_ref, wd_ref, y_ref,
                   xbuf, wg_s, wu_s, wd_s, sem):
    w = pl.program_id(0)
    lo = lo_ref[w]
    hi = hi_ref[w]
    tile = tile_ref[w]
    slot = tile % 2
    n_tiles = hn_hbm.shape[0] * TOP_K // EXPERT_TILE
    prev = jnp.maximum(w - 1, 0)

    @pl.when(w == 0)
    def _():
        _issue_row_gather(tok_ref, EXPERT_TILE, hn_hbm, xbuf.at[0], sem.at[0])

    @pl.when((w == 0) | (tile != tile_ref[prev]))
    def _():
        _wait_row_gather(EXPERT_TILE, hn_hbm, xbuf.at[slot], sem.at[slot])

        @pl.when(tile + 1 < n_tiles)
        def _():
            _issue_row_gather(tokn_ref, EXPERT_TILE, hn_hbm, xbuf.at[1 - slot], sem.at[1 - slot])

    @pl.when((w == 0) | (exp_ref[w] != exp_ref[prev]))
    def _():
        wg_s[...] = wg_ref[0].astype(BF16)
        wu_s[...] = wu_ref[0].astype(BF16)
        wd_s[...] = wd_ref[0].astype(BF16)

    def ffn(x):
        xb = x.astype(BF16)
        gate = _dot(xb, wg_s[...])
        up = _dot(xb, wu_s[...])
        return _dot((gate * _sigmoid(gate) * up).astype(BF16), wd_s[...])

    x_ref = xbuf.at[slot]
    whole = (lo == 0) & (hi == EXPERT_TILE)

    @pl.when(whole)
    def _():
        y_ref[...] = ffn(x_ref[...])

    for j in range(EXPERT_TILE // MOE_BLOCK):
        r0 = j * MOE_BLOCK
        rows = slice(r0, r0 + MOE_BLOCK)

        @pl.when(jnp.logical_not(whole) & (lo < r0 + MOE_BLOCK) & (hi > r0))
        def _():
            y = ffn(x_ref[rows, :])
            rowi = lax.broadcasted_iota(jnp.int32, y.shape, 0) + r0
            mine = (rowi >= lo) & (rowi < hi)

            @pl.when(lo <= r0)
            def _():
                y_ref[rows, :] = jnp.where(mine, y, 0.0)

            @pl.when(lo > r0)
            def _():
                y_ref[rows, :] = jnp.where(mine, y, y_ref[rows, :])


def _experts(items, row_tok, hn, w_gate, w_up, w_down):
    tile, exp, lo, hi = items
    n_rows = row_tok.shape[0]
    n_tiles = n_rows // EXPERT_TILE
    tok3 = row_tok.reshape(n_tiles, 1, EXPERT_TILE)
    wspec = lambda shape: pl.BlockSpec((1,) + shape, lambda w, t, e, l, h: (e[w], 0, 0))
    tok_spec = lambda ahead: pl.BlockSpec(
        (1, 1, EXPERT_TILE), lambda w, t, e, l, h: (jnp.minimum(t[w] + ahead, n_tiles - 1), 0, 0),
        memory_space=pltpu.SMEM)
    return pl.pallas_call(
        _expert_kernel,
        out_shape=jax.ShapeDtypeStruct((n_rows, D_MODEL), F32),
        grid_spec=pltpu.PrefetchScalarGridSpec(
            num_scalar_prefetch=4,
            grid=(tile.shape[0],),
            in_specs=[tok_spec(0), tok_spec(1), pl.BlockSpec(memory_space=pl.ANY),
                      wspec((D_MODEL, EXPERT_FF)), wspec((D_MODEL, EXPERT_FF)), wspec((EXPERT_FF, D_MODEL))],
            out_specs=pl.BlockSpec((EXPERT_TILE, D_MODEL), lambda w, t, e, l, h: (t[w], 0)),
            scratch_shapes=[pltpu.VMEM((2, EXPERT_TILE, D_MODEL), F32),
                            pltpu.VMEM((D_MODEL, EXPERT_FF), BF16),
                            pltpu.VMEM((D_MODEL, EXPERT_FF), BF16),
                            pltpu.VMEM((EXPERT_FF, D_MODEL), BF16),
                            pltpu.SemaphoreType.DMA((2,))]),
        compiler_params=pltpu.CompilerParams(dimension_semantics=("arbitrary",),
                                             vmem_limit_bytes=VMEM_LIMIT),
        name="moe_experts",
    )(tile, exp, lo, hi, tok3, tok3, hn, w_gate, w_up, w_down)


def _combine_kernel(pos_ref, x1_ref, w_ref, y_hbm, o_ref, ybuf, sem):
    tk = x1_ref.shape[0]
    _gather_rows(pos_ref, TOP_K * tk, y_hbm, ybuf, sem)
    o_ref[...] = x1_ref[...] + w_ref[:, 0:1] * ybuf[0:tk, :] + w_ref[:, 1:2] * ybuf[tk:2 * tk, :]


def _combine(pos, x1, gate_w, y, tk):
    T = x1.shape[0]
    return pl.pallas_call(
        _combine_kernel,
        out_shape=jax.ShapeDtypeStruct((T, D_MODEL), F32),
        grid=(T // tk,),
        in_specs=[pl.BlockSpec((1, 1, TOP_K * tk), lambda i: (i, 0, 0), memory_space=pltpu.SMEM),
                  pl.BlockSpec((tk, D_MODEL), lambda i: (i, 0)),
                  pl.BlockSpec((tk, TOP_K), lambda i: (i, 0)),
                  pl.BlockSpec(memory_space=pl.ANY)],
        out_specs=pl.BlockSpec((tk, D_MODEL), lambda i: (i, 0)),
        scratch_shapes=[pltpu.VMEM((TOP_K * tk, D_MODEL), F32), pltpu.SemaphoreType.DMA],
        compiler_params=pltpu.CompilerParams(dimension_semantics=("arbitrary",),
                                             vmem_limit_bytes=VMEM_LIMIT),
        name="moe_combine",
    )(_tile_positions(pos, tk), x1, gate_w, y)


def _routing_tables(route, counts, T):
    e = route[0:TOP_K].astype(jnp.int32)
    rank = route[4:4 + TOP_K].astype(jnp.int32)
    counts = counts.astype(jnp.int32)
    starts = jnp.cumsum(counts) - counts
    ids = jnp.arange(N_EXPERTS, dtype=jnp.int32)
    pos = rank + jnp.sum(jnp.where(e[:, :, None] == ids, starts, 0), axis=-1)
    n_rows = TOP_K * T
    cuts = jnp.sort(jnp.concatenate([jnp.arange(0, n_rows, EXPERT_TILE, dtype=jnp.int32), starts[1:]]))
    ends = jnp.concatenate([cuts[1:], jnp.full((1,), n_rows, jnp.int32)])
    tile = jnp.minimum(cuts // EXPERT_TILE, n_rows // EXPERT_TILE - 1)
    exp = jnp.clip(jnp.sum(starts[None, :] <= cuts[:, None], axis=1) - 1, 0, N_EXPERTS - 1).astype(jnp.int32)
    lo = cuts - tile * EXPERT_TILE
    hi = ends - tile * EXPERT_TILE
    return pos.T, (tile, exp, lo, hi)


def _pick_tile(n, pref):
    t = pref
    while n % t:
        t //= 2
    return t


def kernel(x, positions, norm1_w, w_in, q_norm_w, k_norm_w, attn_sinks, hgrn_lower_bounds, hgrn_norm_w,
           w_branch_attn, w_branch_hgrn, w_out, norm2_w, w_router_group, b_router_group, w_router_expert,
           b_router_expert, w_gate_experts, w_up_experts, w_down_experts):
    B, S, D = x.shape
    T = B * S
    x2 = x.reshape(T, D)
    tm = _pick_tile(T, 512)

    inv_freq = ROPE_THETA ** (-jnp.arange(0, ROT_DIM, 2, dtype=F32) / ROT_DIM)
    ang = positions.astype(F32).reshape(T, 1) * inv_freq[None, :]
    cs = jnp.concatenate([jnp.cos(ang), jnp.sin(ang)], axis=-1)

    w_in0 = w_in[0]
    w_in_a = w_in0[:, :_OFF_Z].astype(BF16)
    w_z = w_in0[:, _OFF_Z:].astype(BF16)

    q, k, v, hq, lf, hv, hg = _inproj(x2, norm1_w[0], w_in_a, cs, q_norm_w[0], k_norm_w[0],
                                      hgrn_lower_bounds.astype(F32), tm)
    attn = _attention(q, k, v, attn_sinks[0].astype(F32), B, S)
    hgrn = _hgrn(hq, lf, hv, hg, hgrn_norm_w[0], B, S, _pick_tile(S, 256))

    pad = ROUTER_ROWS - N_GROUPS - N_EXPERTS
    w_r = jnp.concatenate([w_router_group[0].T, w_router_expert[0].T, jnp.zeros((pad, D), F32)], axis=0)
    wr_hi = w_r.astype(BF16)
    wr_lo = (w_r - wr_hi.astype(F32)).astype(BF16)
    b_r = jnp.concatenate([b_router_group[0], b_router_expert[0], jnp.zeros((pad,), F32)]).astype(F32)
    b_r = jnp.broadcast_to(b_r[:, None], (ROUTER_ROWS, tm))

    x1, hn, route, counts = _merge(x2, attn, hgrn, norm1_w[0], w_z, w_branch_attn[0].astype(BF16),
                                   w_branch_hgrn[0].astype(BF16), w_out[0].astype(BF16), norm2_w[0],
                                   wr_hi, wr_lo, b_r, tm)

    pos, items = _routing_tables(route, counts[N_GROUPS:N_GROUPS + N_EXPERTS, 0], T)
    tk = _pick_tile(T, 128)
    row_tok = _invmap(pos, _pick_tile(pos.size, 4096))
    y = _experts(items, row_tok, hn, w_gate_experts[0], w_up_experts[0], w_down_experts[0])
    out = _combine(pos, x1, route[2:2 + TOP_K].T, y, tk)
    return out.reshape(B, S, D)
```

```python
import numpy as np
import jax
import jax.numpy as jnp
from jax import lax
from jax.experimental import pallas as pl
from jax.experimental.pallas import tpu as pltpu

F32 = jnp.float32
BF16 = jnp.bfloat16

D_MODEL = 1024
N_Q_HEADS = 8
N_KV_HEADS = 2
GROUP = N_Q_HEADS // N_KV_HEADS
HEAD_DIM = 64
ROT_DIM = HEAD_DIM // 4
ROT_HALF = ROT_DIM // 2
ROPE_THETA = 500000.0
WINDOW = 128
ATTN_WIDTH = N_Q_HEADS * HEAD_DIM
KV_WIDTH = N_KV_HEADS * HEAD_DIM

HGRN_HEADS = 4
HGRN_DIM = 128
HGRN_WIDTH = HGRN_HEADS * HGRN_DIM
CHUNK = 64

N_GROUPS = 4
EXPERTS_PER_GROUP = 8
N_EXPERTS = N_GROUPS * EXPERTS_PER_GROUP
TOP_K = 2
EXPERT_FF = 512
MOE_BLOCK = 128
EXPERT_TILE = 512
NORM_EPS = 1e-6
ROUTER_ROWS = 40
ROUTE_ROWS = 8

LANES = 128
NEG_BIG = -1e30
LOG2_E = 1.4426950408889634

_OFF_Q, _OFF_K, _OFF_V = 0, ATTN_WIDTH, ATTN_WIDTH + KV_WIDTH
_OFF_HQ = ATTN_WIDTH + 2 * KV_WIDTH
_OFF_HF = _OFF_HQ + HGRN_WIDTH
_OFF_HI = _OFF_HF + HGRN_WIDTH
_OFF_HG = _OFF_HI + HGRN_WIDTH
_OFF_Z = _OFF_HG + HGRN_WIDTH

VMEM_LIMIT = 56 * 1024 * 1024


def _split3(a):
    hi = a.astype(BF16)
    r1 = a - hi.astype(F32)
    mid = r1.astype(BF16)
    lo = (r1 - mid.astype(F32)).astype(BF16)
    return hi, mid, lo


def _dot(a, b):
    return jnp.dot(a, b, preferred_element_type=F32)


def _dot_nt(a, b):
    return lax.dot_general(a, b, (((1,), (1,)), ((), ())), preferred_element_type=F32)


def _dot_tn(a, b):
    return lax.dot_general(a, b, (((0,), (0,)), ((), ())), preferred_element_type=F32)


def _exact_lhs_dot(m01, a):
    hi, mid, lo = _split3(a)
    return _dot(m01, hi) + _dot(m01, mid) + _dot(m01, lo)


def _exact_rhs_dot(a, m01):
    hi, mid, lo = _split3(a)
    return _dot(hi, m01) + _dot(mid, m01) + _dot(lo, m01)


def _sigmoid(x):
    return 1.0 / (1.0 + jnp.exp(-x))


def _rms(x, w):
    ms = jnp.mean(x * x, axis=-1, keepdims=True)
    return x * lax.rsqrt(ms + NORM_EPS) * w


def _inproj_kernel(x_ref, n1w_ref, w_ref, cs_ref, rope_e_ref, rope_c0_ref, qw_ref, kw_ref,
                   mq_ref, mk_ref, lbp_ref, dup_ref,
                   q_out, k_out, v_out, hq_out, lf_out, hv_out, hg_out):
    xn = _rms(x_ref[...], n1w_ref[...]).astype(BF16)

    def proj(off, width):
        return _dot(xn, w_ref[:, off:off + width])

    tabs = _exact_rhs_dot(cs_ref[...], rope_e_ref[...])
    c_tab = tabs[:, 0:LANES] + rope_c0_ref[...]
    s1_tab = tabs[:, LANES:2 * LANES]
    s2_tab = tabs[:, 2 * LANES:3 * LANES]

    def norm_rope(t, mavg_ref, w_row, scale):
        ms = _dot((t * t).astype(BF16), mavg_ref[...])
        tn = t * lax.rsqrt(ms + NORM_EPS) * w_row
        if scale != 1.0:
            tn = tn * scale
        outs = []
        for j in range(t.shape[1] // LANES):
            c = tn[:, j * LANES:(j + 1) * LANES]
            outs.append(c * c_tab
                        + pltpu.roll(c, LANES - ROT_HALF, 1) * s1_tab
                        + pltpu.roll(c, ROT_HALF, 1) * s2_tab)
        return outs[0] if len(outs) == 1 else jnp.concatenate(outs, axis=1)

    q_out[...] = norm_rope(proj(_OFF_Q, ATTN_WIDTH), mq_ref, qw_ref[...], HEAD_DIM ** -0.5).astype(BF16)
    k_rot = norm_rope(proj(_OFF_K, KV_WIDTH), mk_ref, kw_ref[...], 1.0).astype(BF16)
    k_out[...] = _dot(k_rot, dup_ref[...]).astype(BF16)
    v_out[...] = _dot(proj(_OFF_V, KV_WIDTH).astype(BF16), dup_ref[...]).astype(BF16)

    hq = proj(_OFF_HQ, HGRN_WIDTH)
    hq_out[...] = (hq * _sigmoid(hq)).astype(BF16)
    h0 = lbp_ref[0:1, :]
    h1 = lbp_ref[1:2, :]
    hm = jnp.maximum(h0, h1)
    e0 = jnp.exp(h0 - hm)
    e1 = jnp.exp(h1 - hm)
    lb = e0 / (e0 + e1)
    fg = lb + (1.0 - lb) * _sigmoid(proj(_OFF_HF, HGRN_WIDTH))
    lf_out[...] = jnp.log(fg) * LOG2_E
    hv_out[...] = proj(_OFF_HI, HGRN_WIDTH).astype(BF16)
    hg = proj(_OFF_HG, HGRN_WIDTH)
    hg_out[...] = (hg * _sigmoid(hg)).astype(BF16)


def _rope_constants():
    e = np.zeros((2 * ROT_HALF, 3 * LANES), np.float32)
    c0 = np.zeros((1, LANES), np.float32)
    for lane in range(LANES):
        d = lane % HEAD_DIM
        if d < ROT_HALF:
            e[d, lane] = 1.0
            e[ROT_HALF + d, LANES + lane] = -1.0
        elif d < ROT_DIM:
            e[d - ROT_HALF, lane] = 1.0
            e[ROT_HALF + d - ROT_HALF, 2 * LANES + lane] = 1.0
        else:
            c0[0, lane] = 1.0
    return jnp.asarray(e, BF16), jnp.asarray(c0, F32)


def _head_mean_matrix(width):
    idx = np.arange(width) // HEAD_DIM
    m = (idx[:, None] == idx[None, :]).astype(np.float32) / HEAD_DIM
    return jnp.asarray(m, BF16)


def _inproj(x2, norm1_w, w_in_a, cs, q_norm_w, k_norm_w, lbp, tm):
    T = x2.shape[0]
    rope_e, rope_c0 = _rope_constants()
    qw = jnp.tile(q_norm_w.astype(F32), N_Q_HEADS)[None, :]
    kw = jnp.tile(k_norm_w.astype(F32), N_KV_HEADS)[None, :]
    mq = _head_mean_matrix(ATTN_WIDTH)
    mk = _head_mean_matrix(KV_WIDTH)
    row = lambda w: pl.BlockSpec((tm, w), lambda i: (i, 0))
    full = lambda a: pl.BlockSpec(a.shape, lambda i: (0,) * a.ndim)
    src = HEAD_DIM * (np.arange(2 * KV_WIDTH) // LANES) + np.arange(2 * KV_WIDTH) % HEAD_DIM
    dup = jnp.asarray(np.arange(KV_WIDTH)[:, None] == src[None, :], BF16)
    ins = [x2, norm1_w[None, :], w_in_a, cs, rope_e, rope_c0, qw, kw, mq, mk, lbp, dup]
    in_specs = [row(D_MODEL), full(ins[1]), full(w_in_a), row(cs.shape[1])] + [full(a) for a in ins[4:]]
    outs = [(ATTN_WIDTH, BF16), (2 * KV_WIDTH, BF16), (2 * KV_WIDTH, BF16), (HGRN_WIDTH, BF16),
            (HGRN_WIDTH, F32), (HGRN_WIDTH, BF16), (HGRN_WIDTH, BF16)]
    return pl.pallas_call(
        _inproj_kernel,
        out_shape=[jax.ShapeDtypeStruct((T, w), dt) for w, dt in outs],
        grid=(T // tm,),
        in_specs=in_specs,
        out_specs=[row(w) for w, _ in outs],
        compiler_params=pltpu.CompilerParams(dimension_semantics=("arbitrary",),
                                             vmem_limit_bytes=VMEM_LIMIT),
        name="inproj",
    )(*ins)


def _attn_kernel(sink_ref, q_ref, kc_ref, kp_ref, vc_ref, vp_ref, half_ref, o_ref):
    has_prev = pl.program_id(1) > 0
    qi = lax.broadcasted_iota(jnp.int32, (WINDOW, 2 * WINDOW), 0)
    kj = lax.broadcasted_iota(jnp.int32, (WINDOW, 2 * WINDOW), 1)
    valid = ((kj < WINDOW) & (kj > qi) & has_prev) | ((kj >= WINDOW) & (kj - WINDOW <= qi))
    left = lax.broadcasted_iota(jnp.int32, (WINDOW, LANES), 1) < HEAD_DIM
    half = (half_ref[0], half_ref[1])
    outs = []
    for h in range(N_KV_HEADS):
        cols = slice(h * LANES, (h + 1) * LANES)
        kcat = jnp.concatenate([kp_ref[:, cols], kc_ref[:, cols]], axis=0)
        vcat = jnp.concatenate([vp_ref[:, cols], vc_ref[:, cols]], axis=0)
        ks = [kcat * hm for hm in half]
        rhs = [jnp.concatenate([vcat * hm, hm], axis=1) for hm in half]
        for pr in range(GROUP // 2):
            pair = h * (GROUP // 2) + pr
            qp = q_ref[:, pair * LANES:(pair + 1) * LANES]
            acc = None
            m_side = []
            for side in range(2):
                s = jnp.where(valid, _dot_nt(qp, ks[side]), NEG_BIG)
                sink = sink_ref[2 * pair + side]
                m = jnp.maximum(jnp.max(s, axis=-1, keepdims=True), sink)
                p = jnp.exp(s - m).astype(BF16)
                d = _dot(p, rhs[side])
                acc = d if acc is None else acc + d
                m_side.append(jnp.exp(sink - m))
            den = acc[:, LANES:2 * LANES] + jnp.where(left, m_side[0], m_side[1])
            outs.append(acc[:, 0:LANES] / den)
    o_ref[...] = jnp.concatenate(outs, axis=1).astype(BF16)


def _attention(q, k, v, sinks, B, S):
    nb = S // WINDOW
    cur = lambda b, n: (b * nb + n, 0)
    prev = lambda b, n: (b * nb + jnp.maximum(n - 1, 0), 0)
    lane_left = np.arange(LANES) < HEAD_DIM
    half = jnp.asarray(np.broadcast_to(np.stack([lane_left, ~lane_left])[:, None, :],
                                       (2, 2 * WINDOW, LANES)), BF16)
    return pl.pallas_call(
        _attn_kernel,
        out_shape=jax.ShapeDtypeStruct((B * S, ATTN_WIDTH), BF16),
        grid=(B, nb),
        in_specs=[pl.BlockSpec(memory_space=pltpu.SMEM),
                  pl.BlockSpec((WINDOW, ATTN_WIDTH), cur),
                  pl.BlockSpec((WINDOW, 2 * KV_WIDTH), cur),
                  pl.BlockSpec((WINDOW, 2 * KV_WIDTH), prev),
                  pl.BlockSpec((WINDOW, 2 * KV_WIDTH), cur),
                  pl.BlockSpec((WINDOW, 2 * KV_WIDTH), prev),
                  pl.BlockSpec(half.shape, lambda b, n: (0, 0, 0))],
        out_specs=pl.BlockSpec((WINDOW, ATTN_WIDTH), cur),
        compiler_params=pltpu.CompilerParams(dimension_semantics=("arbitrary", "arbitrary"),
                                             vmem_limit_bytes=VMEM_LIMIT),
        name="swa_attention",
    )(sinks, q, k, k, v, v, half)


_LEVEL_HALVES = (1, 2, 4, 8, 16, 32)


def _hgrn_level_masks():
    t = np.arange(CHUNK)[:, None]
    s = np.arange(CHUNK)[None, :]
    masks = [((t // (2 * h)) == (s // (2 * h))) & ((t & h) != 0) & ((s & h) == 0) for h in _LEVEL_HALVES]
    return jnp.asarray(np.stack(masks), F32)


def _level_reference(b_ref, slot, half):
    if half >= 4:
        span = max(2 * half, 8)
        pieces = [jnp.broadcast_to(b_ref[slot, s + half - 1:s + half, :], (span, HGRN_DIM))
                  for s in range(0, CHUNK, span)]
    else:
        r8 = lax.broadcasted_iota(jnp.int32, (8, HGRN_DIM), 0)
        pieces = [jnp.where(r8 < 4,
                            jnp.broadcast_to(b_ref[slot, s + 1:s + 2, :], (8, HGRN_DIM)),
                            jnp.broadcast_to(b_ref[slot, s + 5:s + 6, :], (8, HGRN_DIM)))
                  for s in range(0, CHUNK, 8)]
    return pieces[0] if len(pieces) == 1 else jnp.concatenate(pieces, axis=0)


def _hgrn_kernel(hq_ref, lf_ref, hv_ref, hg_ref, nw_ref, tri_ref, lm_ref, o_ref, st_ref, b_ref):
    @pl.when(pl.program_id(1) == 0)
    def _():
        st_ref[...] = jnp.zeros_like(st_ref)

    tri2 = tri_ref[...]
    odd = (lax.broadcasted_iota(jnp.int32, (CHUNK, HGRN_DIM), 0) & 1) != 0
    masks = [lm_ref[li] != 0.0 for li in range(len(_LEVEL_HALVES))]
    units = [(c, h) for c in range(hq_ref.shape[0] // CHUNK) for h in range(HGRN_HEADS)]
    sl = lambda c, h: (slice(c * CHUNK, (c + 1) * CHUNK), slice(h * HGRN_DIM, (h + 1) * HGRN_DIM))

    bs = []
    for u, (c, h) in enumerate(units):
        lf2 = lf_ref[sl(c, h)]
        hi = lf2.astype(BF16)
        lo = (lf2 - hi.astype(F32)).astype(BF16)
        b = _dot(tri2, jnp.concatenate([hi, lo], axis=0))
        b_ref[u] = b
        bs.append(b)

    outs, qs, ks, fs = [], [], [], []
    for u, (c, h) in enumerate(units):
        b = bs[u]
        q = hq_ref[sl(c, h)].astype(F32)
        v_bf = hv_ref[sl(c, h)]
        f = jnp.exp2(lf_ref[sl(c, h)])
        k = 1.0 - f
        b_last = b[CHUNK - 1:CHUNK, :]
        st = st_ref[h]
        o = _dot_nt((q * jnp.exp2(b)).astype(BF16), st.astype(BF16))
        k_out = (k * jnp.exp2(b_last - b)).astype(BF16)
        st_ref[h] = st * jnp.exp2(b_last) + _dot_tn(v_bf, k_out)
        outs.append(o + jnp.sum(q * k, axis=-1, keepdims=True) * v_bf.astype(F32))
        qs.append(q)
        ks.append(k)
        fs.append(f)

    accs = [jnp.zeros((CHUNK, CHUNK), F32) for _ in units]
    for li, half in enumerate(_LEVEL_HALVES):
        for u in range(len(units)):
            if half == 1:
                e = jnp.where(odd, fs[u], 1.0)
            else:
                e = jnp.exp2(-jnp.abs(bs[u] - _level_reference(b_ref, u, half)))
            accs[u] = jnp.where(masks[li], _dot_nt((qs[u] * e).astype(BF16), (ks[u] * e).astype(BF16)), accs[u])

    for u, (c, h) in enumerate(units):
        o = outs[u] + _dot(accs[u].astype(BF16), hv_ref[sl(c, h)])
        y = _rms(o, nw_ref[...]) * hg_ref[sl(c, h)].astype(F32)
        o_ref[sl(c, h)] = y.astype(BF16)


def _hgrn(hq, lf, hv, hg, hgrn_norm_w, B, S, tb):
    nt = S // tb
    blk = pl.BlockSpec((tb, HGRN_WIDTH), lambda b, n: (b * nt + n, 0))
    tri = np.tril(np.ones((CHUNK, CHUNK), np.float32))
    tri = jnp.asarray(np.concatenate([tri, tri], axis=1), BF16)
    lm = _hgrn_level_masks()
    return pl.pallas_call(
        _hgrn_kernel,
        out_shape=jax.ShapeDtypeStruct((B * S, HGRN_WIDTH), BF16),
        grid=(B, nt),
        in_specs=[blk, blk, blk, blk,
                  pl.BlockSpec((1, HGRN_DIM), lambda b, n: (0, 0)),
                  pl.BlockSpec(tri.shape, lambda b, n: (0, 0)),
                  pl.BlockSpec(lm.shape, lambda b, n: (0, 0, 0))],
        out_specs=blk,
        scratch_shapes=[pltpu.VMEM((HGRN_HEADS, HGRN_DIM, HGRN_DIM), F32),
                        pltpu.VMEM((tb // CHUNK * HGRN_HEADS, CHUNK, HGRN_DIM), F32)],
        compiler_params=pltpu.CompilerParams(dimension_semantics=("arbitrary", "arbitrary"),
                                             vmem_limit_bytes=VMEM_LIMIT),
        name="hgrn2",
    )(hq, lf, hv, hg, hgrn_norm_w[None, :].astype(F32), tri, lm)


def _merge_kernel(x_ref, attn_ref, hgrn_ref, n1w_ref, wz_ref, wba_ref, wbh_ref, wout_ref, n2w_ref,
                  wr_hi_ref, wr_lo_ref, br_ref, utri_ref, x1_out, hn_out, route_out, count_out, run_ref):
    @pl.when(pl.program_id(0) == 0)
    def _():
        run_ref[...] = jnp.zeros_like(run_ref)

    x = x_ref[...]
    xn = _rms(x, n1w_ref[...]).astype(BF16)
    za = _sigmoid(_dot(xn, wz_ref[:, 0:D_MODEL]))
    zb = _sigmoid(_dot(xn, wz_ref[:, D_MODEL:2 * D_MODEL]))
    mixed = za * _dot(attn_ref[...], wba_ref[...]) + zb * _dot(hgrn_ref[...], wbh_ref[...])
    x1 = x + _dot(mixed.astype(BF16), wout_ref[...])
    x1_out[...] = x1
    hn = _rms(x1, n2w_ref[...])
    _rows_to_tiles(hn_out, slice(None), hn)

    hn_hi = hn.astype(BF16)
    hn_lo = (hn - hn_hi.astype(F32)).astype(BF16)
    logits = (_dot_nt(wr_hi_ref[...], hn_hi) + _dot_nt(wr_hi_ref[...], hn_lo) + _dot_nt(wr_lo_ref[...], hn_hi)
              + br_ref[...])
    r = lax.broadcasted_iota(jnp.int32, logits.shape, 0).astype(F32)
    far = float(ROUTER_ROWS)
    cmax = lambda a: jnp.max(a, axis=0, keepdims=True)
    cmin = lambda a: jnp.min(a, axis=0, keepdims=True)
    csum = lambda a: jnp.sum(a, axis=0, keepdims=True)

    lg = jnp.where(r < N_GROUPS, logits, NEG_BIG)
    mg = cmax(lg)
    gsel = cmin(jnp.where(lg == mg, r, far))
    pgsel = 1.0 / csum(jnp.exp(lg - mg))

    lo = N_GROUPS + EXPERTS_PER_GROUP * gsel
    le = jnp.where((r >= lo) & (r < lo + EXPERTS_PER_GROUP), logits, NEG_BIG)
    m1 = cmax(le)
    i1 = cmin(jnp.where(le == m1, r, far))
    se = csum(jnp.exp(le - m1))
    le2 = jnp.where(r == i1, NEG_BIG, le)
    m2 = cmax(le2)
    i2 = cmin(jnp.where(le2 == m2, r, far))
    top0 = 1.0 / se
    top1 = jnp.exp(m2 - m1) / se
    tsum = top0 + top1
    w0 = pgsel * top0 / tsum
    w1 = pgsel * top1 / tsum

    sel1 = r == i1
    sel2 = r == i2
    onehot = (sel1 | sel2).astype(BF16)
    before = _dot(onehot, utri_ref[...]) + run_ref[...]
    r0 = csum(jnp.where(sel1, before, 0.0))
    r1 = csum(jnp.where(sel2, before, 0.0))
    run_new = run_ref[...] + _dot(onehot, jnp.ones(utri_ref.shape, BF16))
    run_ref[...] = run_new
    count_out[...] = run_new

    row8 = lax.broadcasted_iota(jnp.int32, route_out.shape, 0)
    vals = (i1 - N_GROUPS, i2 - N_GROUPS, w0, w1, r0, r1)
    route = jnp.zeros(route_out.shape, F32)
    for j, val in enumerate(vals):
        route = jnp.where(row8 == j, val, route)
    route_out[...] = route


def _merge(x2, attn, hgrn, norm1_w, w_z, w_ba, w_bh, w_out, norm2_w, wr_hi, wr_lo, br, tm):
    T = x2.shape[0]
    row = lambda w: pl.BlockSpec((tm, w), lambda i: (i, 0))
    full = lambda a: pl.BlockSpec(a.shape, lambda i: (0,) * a.ndim)
    utri = jnp.asarray(np.triu(np.ones((tm, tm), np.float32), 1), BF16)
    ins = [x2, attn, hgrn, norm1_w[None, :], w_z, w_ba, w_bh, w_out, norm2_w[None, :], wr_hi, wr_lo, br, utri]
    in_specs = [row(D_MODEL), row(ATTN_WIDTH), row(HGRN_WIDTH)] + [full(a) for a in ins[3:]]
    return pl.pallas_call(
        _merge_kernel,
        out_shape=[jax.ShapeDtypeStruct((T, D_MODEL), F32), jax.ShapeDtypeStruct((T,) + ROW_TILE, F32),
                   jax.ShapeDtypeStruct((ROUTE_ROWS, T), F32), jax.ShapeDtypeStruct((ROUTER_ROWS, tm), F32)],
        grid=(T // tm,),
        in_specs=in_specs,
        out_specs=[row(D_MODEL), pl.BlockSpec((tm,) + ROW_TILE, lambda i: (i, 0, 0)),
                   pl.BlockSpec((ROUTE_ROWS, tm), lambda i: (0, i)),
                   pl.BlockSpec((ROUTER_ROWS, tm), lambda i: (0, 0))],
        scratch_shapes=[pltpu.VMEM((ROUTER_ROWS, tm), F32)],
        compiler_params=pltpu.CompilerParams(dimension_semantics=("arbitrary",),
                                             vmem_limit_bytes=VMEM_LIMIT),
        name="merge_router",
    )(*ins)


ROW_TILE = (D_MODEL // LANES, LANES)


def _rows_to_tiles(ref, rows, x):
    for j in range(ROW_TILE[0]):
        ref[rows, j, :] = x[:, j * LANES:(j + 1) * LANES]


def _tiles_to_rows(ref, rows):
    return jnp.concatenate([ref[rows, j, :] for j in range(ROW_TILE[0])], axis=1)


def _issue_row_gather(idx_ref, n, src_hbm, dst_ref, sem):
    def issue(r, carry):
        pltpu.make_async_copy(src_hbm.at[idx_ref[0, 0, r]], dst_ref.at[r], sem).start()
        return carry
    lax.fori_loop(0, n, issue, 0, unroll=16)


def _wait_row_gather(n, src_hbm, dst_ref, sem):
    pltpu.make_async_copy(src_hbm.at[pl.ds(0, n)], dst_ref.at[pl.ds(0, n)], sem).wait()


def _gather_rows(idx_ref, n, src_hbm, dst_ref, sem):
    _issue_row_gather(idx_ref, n, src_hbm, dst_ref, sem)
    _wait_row_gather(n, src_hbm, dst_ref, sem)


def _tile_positions(pos, tk):
    nt = pos.shape[0] // tk
    return pos.reshape(nt, tk, TOP_K).transpose(0, 2, 1).reshape(nt, 1, TOP_K * tk)


def _invmap_kernel(pos_ref, tok_ref):
    tokens = pos_ref.shape[2] // TOP_K
    base = pl.program_id(0) * tokens

    def body(j, carry):
        for k in range(TOP_K):
            tok_ref[pos_ref[0, 0, j * TOP_K + k]] = base + j
        return carry
    lax.fori_loop(0, tokens, body, 0, unroll=8)


def _invmap(pos, nb):
    n = pos.size
    return pl.pallas_call(
        _invmap_kernel,
        out_shape=jax.ShapeDtypeStruct((n,), jnp.int32),
        grid=(n // nb,),
        in_specs=[pl.BlockSpec((1, 1, nb), lambda i: (i, 0, 0), memory_space=pltpu.SMEM)],
        out_specs=pl.BlockSpec(memory_space=pltpu.SMEM),
        compiler_params=pltpu.CompilerParams(dimension_semantics=("arbitrary",)),
        name="moe_row_tokens",
    )(pos.reshape(n // nb, 1, nb))


def _expert_kernel(tile_ref, exp_ref, lo_ref, hi_ref, tok_ref, tokn_ref, hn_hbm, wg_ref, wu_ref, wd_ref, y_ref,
                   xbuf, wg_s, wu_s, wd_s, sem):
    w = pl.program_id(0)
    lo = lo_ref[w]
    hi = hi_ref[w]
    tile = tile_ref[w]
    slot = tile % 2
    n_tiles = hn_hbm.shape[0] * TOP_K // EXPERT_TILE
    prev = jnp.maximum(w - 1, 0)

    entered = (w == 0) | (tile != tile_ref[prev])
    whole = (lo == 0) & (hi == EXPERT_TILE)

    @pl.when(w == 0)
    def _():
        _issue_row_gather(tok_ref, EXPERT_TILE, hn_hbm, xbuf.at[0], sem.at[0])

    @pl.when(entered)
    def _():
        _wait_row_gather(EXPERT_TILE, hn_hbm, xbuf.at[slot], sem.at[slot])

        _issue_row_gather(tokn_ref, EXPERT_TILE, hn_hbm, xbuf.at[1 - slot], sem.at[1 - slot])

    @pl.when((w == 0) | (exp_ref[w] != exp_ref[prev]))
    def _():
        wg_s[...] = wg_ref[0].astype(BF16)
        wu_s[...] = wu_ref[0].astype(BF16)
        wd_s[...] = wd_ref[0].astype(BF16)

    def ffn(x):
        xb = x.astype(BF16)
        gate = _dot(xb, wg_s[...])
        up = _dot(xb, wu_s[...])
        return _dot((gate * _sigmoid(gate) * up).astype(BF16), wd_s[...])

    x_ref = xbuf.at[slot]

    @pl.when(whole)
    def _():
        _rows_to_tiles(y_ref, slice(None), ffn(_tiles_to_rows(x_ref, slice(None))))

    for j in range(EXPERT_TILE // MOE_BLOCK):
        r0 = j * MOE_BLOCK
        rows = slice(r0, r0 + MOE_BLOCK)

        @pl.when(jnp.logical_not(whole) & (lo < r0 + MOE_BLOCK) & (hi > r0))
        def _():
            y = ffn(_tiles_to_rows(x_ref, rows))
            rowi = lax.broadcasted_iota(jnp.int32, y.shape, 0) + r0
            mine = (rowi >= lo) & (rowi < hi)

            @pl.when(lo <= r0)
            def _():
                _rows_to_tiles(y_ref, rows, jnp.where(mine, y, 0.0))

            @pl.when(lo > r0)
            def _():
                _rows_to_tiles(y_ref, rows, jnp.where(mine, y, _tiles_to_rows(y_ref, rows)))

    @pl.when(w == pl.num_programs(0) - 1)
    def _():
        _wait_row_gather(EXPERT_TILE, hn_hbm, xbuf.at[1 - slot], sem.at[1 - slot])


def _experts(items, row_tok, hn, w_gate, w_up, w_down):
    tile, exp, lo, hi = items
    n_rows = row_tok.shape[0]
    n_tiles = n_rows // EXPERT_TILE
    tok3 = row_tok.reshape(n_tiles, 1, EXPERT_TILE)
    wspec = lambda shape: pl.BlockSpec((1,) + shape, lambda w, t, e, l, h: (e[w], 0, 0))
    tok_spec = lambda ahead: pl.BlockSpec(
        (1, 1, EXPERT_TILE), lambda w, t, e, l, h: (jnp.minimum(t[w] + ahead, n_tiles - 1), 0, 0),
        memory_space=pltpu.SMEM)
    return pl.pallas_call(
        _expert_kernel,
        out_shape=jax.ShapeDtypeStruct((n_rows,) + ROW_TILE, F32),
        grid_spec=pltpu.PrefetchScalarGridSpec(
            num_scalar_prefetch=4,
            grid=(tile.shape[0],),
            in_specs=[tok_spec(0), tok_spec(1), pl.BlockSpec(memory_space=pl.ANY),
                      wspec((D_MODEL, EXPERT_FF)), wspec((D_MODEL, EXPERT_FF)), wspec((EXPERT_FF, D_MODEL))],
            out_specs=pl.BlockSpec((EXPERT_TILE,) + ROW_TILE, lambda w, t, e, l, h: (t[w], 0, 0)),
            scratch_shapes=[pltpu.VMEM((2, EXPERT_TILE) + ROW_TILE, F32),
                            pltpu.VMEM((D_MODEL, EXPERT_FF), BF16),
                            pltpu.VMEM((D_MODEL, EXPERT_FF), BF16),
                            pltpu.VMEM((EXPERT_FF, D_MODEL), BF16),
                            pltpu.SemaphoreType.DMA((2,))]),
        compiler_params=pltpu.CompilerParams(dimension_semantics=("arbitrary",),
                                             vmem_limit_bytes=VMEM_LIMIT),
        name="moe_experts",
    )(tile, exp, lo, hi, tok3, tok3, hn, w_gate, w_up, w_down)


def _combine_kernel(pos_ref, x1_ref, w_ref, y_hbm, o_ref, ybuf, sem):
    tk = x1_ref.shape[0]
    _gather_rows(pos_ref, TOP_K * tk, y_hbm, ybuf, sem)
    w0 = w_ref[:, 0:1]
    w1 = w_ref[:, 1:2]
    for j in range(ROW_TILE[0]):
        cols = slice(j * LANES, (j + 1) * LANES)
        o_ref[:, cols] = x1_ref[:, cols] + w0 * ybuf[0:tk, j, :] + w1 * ybuf[tk:2 * tk, j, :]


def _combine(pos, x1, gate_w, y, tk):
    T = x1.shape[0]
    return pl.pallas_call(
        _combine_kernel,
        out_shape=jax.ShapeDtypeStruct((T, D_MODEL), F32),
        grid=(T // tk,),
        in_specs=[pl.BlockSpec((1, 1, TOP_K * tk), lambda i: (i, 0, 0), memory_space=pltpu.SMEM),
                  pl.BlockSpec((tk, D_MODEL), lambda i: (i, 0)),
                  pl.BlockSpec((tk, TOP_K), lambda i: (i, 0)),
                  pl.BlockSpec(memory_space=pl.ANY)],
        out_specs=pl.BlockSpec((tk, D_MODEL), lambda i: (i, 0)),
        scratch_shapes=[pltpu.VMEM((TOP_K * tk,) + ROW_TILE, F32), pltpu.SemaphoreType.DMA],
        compiler_params=pltpu.CompilerParams(dimension_semantics=("arbitrary",),
                                             vmem_limit_bytes=VMEM_LIMIT),
        name="moe_combine",
    )(_tile_positions(pos, tk), x1, gate_w, y)


def _routing_tables(route, counts, T):
    e = route[0:TOP_K].astype(jnp.int32)
    rank = route[4:4 + TOP_K].astype(jnp.int32)
    counts = counts.astype(jnp.int32)
    starts = jnp.cumsum(counts) - counts
    ids = jnp.arange(N_EXPERTS, dtype=jnp.int32)
    pos = rank + jnp.sum(jnp.where(e[:, :, None] == ids, starts, 0), axis=-1)
    n_rows = TOP_K * T
    cuts = jnp.sort(jnp.concatenate([jnp.arange(0, n_rows, EXPERT_TILE, dtype=jnp.int32), starts[1:]]))
    ends = jnp.concatenate([cuts[1:], jnp.full((1,), n_rows, jnp.int32)])
    tile = jnp.minimum(cuts // EXPERT_TILE, n_rows // EXPERT_TILE - 1)
    exp = jnp.clip(jnp.sum(starts[None, :] <= cuts[:, None], axis=1) - 1, 0, N_EXPERTS - 1).astype(jnp.int32)
    lo = cuts - tile * EXPERT_TILE
    hi = ends - tile * EXPERT_TILE
    return pos.T, (tile, exp, lo, hi)


def _pick_tile(n, pref):
    t = pref
    while n % t:
        t //= 2
    return t


def kernel(x, positions, norm1_w, w_in, q_norm_w, k_norm_w, attn_sinks, hgrn_lower_bounds, hgrn_norm_w,
           w_branch_attn, w_branch_hgrn, w_out, norm2_w, w_router_group, b_router_group, w_router_expert,
           b_router_expert, w_gate_experts, w_up_experts, w_down_experts):
    B, S, D = x.shape
    T = B * S
    x2 = x.reshape(T, D)
    tm = _pick_tile(T, 512)

    inv_freq = ROPE_THETA ** (-jnp.arange(0, ROT_DIM, 2, dtype=F32) / ROT_DIM)
    ang = positions.astype(F32).reshape(T, 1) * inv_freq[None, :]
    cs = jnp.concatenate([jnp.cos(ang), jnp.sin(ang)], axis=-1)

    w_in0 = w_in[0]
    w_in_a = w_in0[:, :_OFF_Z].astype(BF16)
    w_z = w_in0[:, _OFF_Z:].astype(BF16)

    q, k, v, hq, lf, hv, hg = _inproj(x2, norm1_w[0], w_in_a, cs, q_norm_w[0], k_norm_w[0],
                                      hgrn_lower_bounds.astype(F32), tm)
    attn = _attention(q, k, v, attn_sinks[0].astype(F32), B, S)
    hgrn = _hgrn(hq, lf, hv, hg, hgrn_norm_w[0], B, S, _pick_tile(S, 256))

    pad = ROUTER_ROWS - N_GROUPS - N_EXPERTS
    w_r = jnp.concatenate([w_router_group[0].T, w_router_expert[0].T, jnp.zeros((pad, D), F32)], axis=0)
    wr_hi = w_r.astype(BF16)
    wr_lo = (w_r - wr_hi.astype(F32)).astype(BF16)
    b_r = jnp.concatenate([b_router_group[0], b_router_expert[0], jnp.zeros((pad,), F32)]).astype(F32)
    b_r = jnp.broadcast_to(b_r[:, None], (ROUTER_ROWS, tm))

    x1, hn, route, counts = _merge(x2, attn, hgrn, norm1_w[0], w_z, w_branch_attn[0].astype(BF16),
                                   w_branch_hgrn[0].astype(BF16), w_out[0].astype(BF16), norm2_w[0],
                                   wr_hi, wr_lo, b_r, tm)

    pos, items = _routing_tables(route, counts[N_GROUPS:N_GROUPS + N_EXPERTS, 0], T)
    tk = _pick_tile(T, 128)
    row_tok = _invmap(pos, _pick_tile(pos.size, 4096))
    y = _experts(items, row_tok, hn, w_gate_experts[0], w_up_experts[0], w_down_experts[0])
    out = _combine(pos, x1, route[2:2 + TOP_K].T, y, tk)
    return out.reshape(B, S, D)
```

```python
import numpy as np
import jax
import jax.numpy as jnp
from jax import lax
from jax.experimental import pallas as pl
from jax.experimental.pallas import tpu as pltpu

F32 = jnp.float32
BF16 = jnp.bfloat16

D_MODEL = 1024
N_Q_HEADS = 8
N_KV_HEADS = 2
GROUP = N_Q_HEADS // N_KV_HEADS
HEAD_DIM = 64
ROT_DIM = HEAD_DIM // 4
ROT_HALF = ROT_DIM // 2
ROPE_THETA = 500000.0
WINDOW = 128
ATTN_WIDTH = N_Q_HEADS * HEAD_DIM
KV_WIDTH = N_KV_HEADS * HEAD_DIM

HGRN_HEADS = 4
HGRN_DIM = 128
HGRN_WIDTH = HGRN_HEADS * HGRN_DIM
CHUNK = 64

N_GROUPS = 4
EXPERTS_PER_GROUP = 8
N_EXPERTS = N_GROUPS * EXPERTS_PER_GROUP
TOP_K = 2
EXPERT_FF = 512
MOE_BLOCK = 128
EXPERT_TILE = 512
NORM_EPS = 1e-6
ROUTER_ROWS = 40
ROUTE_ROWS = 8

LANES = 128
NEG_BIG = -1e30
LOG2_E = 1.4426950408889634

_OFF_Q, _OFF_K, _OFF_V = 0, ATTN_WIDTH, ATTN_WIDTH + KV_WIDTH
_OFF_HQ = ATTN_WIDTH + 2 * KV_WIDTH
_OFF_HF = _OFF_HQ + HGRN_WIDTH
_OFF_HI = _OFF_HF + HGRN_WIDTH
_OFF_HG = _OFF_HI + HGRN_WIDTH
_OFF_Z = _OFF_HG + HGRN_WIDTH

VMEM_LIMIT = 56 * 1024 * 1024


def _split3(a):
    hi = a.astype(BF16)
    r1 = a - hi.astype(F32)
    mid = r1.astype(BF16)
    lo = (r1 - mid.astype(F32)).astype(BF16)
    return hi, mid, lo


def _dot(a, b):
    return jnp.dot(a, b, preferred_element_type=F32)


def _dot_nt(a, b):
    return lax.dot_general(a, b, (((1,), (1,)), ((), ())), preferred_element_type=F32)


def _dot_tn(a, b):
    return lax.dot_general(a, b, (((0,), (0,)), ((), ())), preferred_element_type=F32)


def _exact_lhs_dot(m01, a):
    hi, mid, lo = _split3(a)
    return _dot(m01, hi) + _dot(m01, mid) + _dot(m01, lo)


def _exact_rhs_dot(a, m01):
    hi, mid, lo = _split3(a)
    return _dot(hi, m01) + _dot(mid, m01) + _dot(lo, m01)


def _sigmoid(x):
    return 1.0 / (1.0 + jnp.exp(-x))


def _rms(x, w):
    ms = jnp.mean(x * x, axis=-1, keepdims=True)
    return x * lax.rsqrt(ms + NORM_EPS) * w


def _inproj_kernel(x_ref, n1w_ref, w_ref, cs_ref, rope_e_ref, rope_c0_ref, qw_ref, kw_ref,
                   mq_ref, mk_ref, lbp_ref, dup_ref,
                   q_out, k_out, v_out, hq_out, lf_out, hv_out, hg_out):
    xn = _rms(x_ref[...], n1w_ref[...]).astype(BF16)

    def proj(off, width):
        return _dot(xn, w_ref[:, off:off + width])

    tabs = _exact_rhs_dot(cs_ref[...], rope_e_ref[...])
    c_tab = tabs[:, 0:LANES] + rope_c0_ref[...]
    s1_tab = tabs[:, LANES:2 * LANES]
    s2_tab = tabs[:, 2 * LANES:3 * LANES]

    def norm_rope(t, mavg_ref, w_row, scale):
        ms = _dot((t * t).astype(BF16), mavg_ref[...])
        tn = t * lax.rsqrt(ms + NORM_EPS) * w_row
        if scale != 1.0:
            tn = tn * scale
        outs = []
        for j in range(t.shape[1] // LANES):
            c = tn[:, j * LANES:(j + 1) * LANES]
            outs.append(c * c_tab
                        + pltpu.roll(c, LANES - ROT_HALF, 1) * s1_tab
                        + pltpu.roll(c, ROT_HALF, 1) * s2_tab)
        return outs[0] if len(outs) == 1 else jnp.concatenate(outs, axis=1)

    q_out[...] = norm_rope(proj(_OFF_Q, ATTN_WIDTH), mq_ref, qw_ref[...], HEAD_DIM ** -0.5).astype(BF16)
    k_rot = norm_rope(proj(_OFF_K, KV_WIDTH), mk_ref, kw_ref[...], 1.0).astype(BF16)
    k_out[...] = _dot(k_rot, dup_ref[...]).astype(BF16)
    v_out[...] = _dot(proj(_OFF_V, KV_WIDTH).astype(BF16), dup_ref[...]).astype(BF16)

    hq = proj(_OFF_HQ, HGRN_WIDTH)
    hq_out[...] = (hq * _sigmoid(hq)).astype(BF16)
    h0 = lbp_ref[0:1, :]
    h1 = lbp_ref[1:2, :]
    hm = jnp.maximum(h0, h1)
    e0 = jnp.exp(h0 - hm)
    e1 = jnp.exp(h1 - hm)
    lb = e0 / (e0 + e1)
    fg = lb + (1.0 - lb) * _sigmoid(proj(_OFF_HF, HGRN_WIDTH))
    lf_out[...] = jnp.log(fg) * LOG2_E
    hv_out[...] = proj(_OFF_HI, HGRN_WIDTH).astype(BF16)
    hg = proj(_OFF_HG, HGRN_WIDTH)
    hg_out[...] = (hg * _sigmoid(hg)).astype(BF16)


def _rope_constants():
    e = np.zeros((2 * ROT_HALF, 3 * LANES), np.float32)
    c0 = np.zeros((1, LANES), np.float32)
    for lane in range(LANES):
        d = lane % HEAD_DIM
        if d < ROT_HALF:
            e[d, lane] = 1.0
            e[ROT_HALF + d, LANES + lane] = -1.0
        elif d < ROT_DIM:
            e[d - ROT_HALF, lane] = 1.0
            e[ROT_HALF + d - ROT_HALF, 2 * LANES + lane] = 1.0
        else:
            c0[0, lane] = 1.0
    return jnp.asarray(e, BF16), jnp.asarray(c0, F32)


def _head_mean_matrix(width):
    idx = np.arange(width) // HEAD_DIM
    m = (idx[:, None] == idx[None, :]).astype(np.float32) / HEAD_DIM
    return jnp.asarray(m, BF16)


def _inproj(x2, norm1_w, w_in_a, cs, q_norm_w, k_norm_w, lbp, tm):
    T = x2.shape[0]
    rope_e, rope_c0 = _rope_constants()
    qw = jnp.tile(q_norm_w.astype(F32), N_Q_HEADS)[None, :]
    kw = jnp.tile(k_norm_w.astype(F32), N_KV_HEADS)[None, :]
    mq = _head_mean_matrix(ATTN_WIDTH)
    mk = _head_mean_matrix(KV_WIDTH)
    row = lambda w: pl.BlockSpec((tm, w), lambda i: (i, 0))
    full = lambda a: pl.BlockSpec(a.shape, lambda i: (0,) * a.ndim)
    src = HEAD_DIM * (np.arange(2 * KV_WIDTH) // LANES) + np.arange(2 * KV_WIDTH) % HEAD_DIM
    dup = jnp.asarray(np.arange(KV_WIDTH)[:, None] == src[None, :], BF16)
    ins = [x2, norm1_w[None, :], w_in_a, cs, rope_e, rope_c0, qw, kw, mq, mk, lbp, dup]
    in_specs = [row(D_MODEL), full(ins[1]), full(w_in_a), row(cs.shape[1])] + [full(a) for a in ins[4:]]
    outs = [(ATTN_WIDTH, BF16), (2 * KV_WIDTH, BF16), (2 * KV_WIDTH, BF16), (HGRN_WIDTH, BF16),
            (HGRN_WIDTH, F32), (HGRN_WIDTH, BF16), (HGRN_WIDTH, BF16)]
    return pl.pallas_call(
        _inproj_kernel,
        out_shape=[jax.ShapeDtypeStruct((T, w), dt) for w, dt in outs],
        grid=(T // tm,),
        in_specs=in_specs,
        out_specs=[row(w) for w, _ in outs],
        compiler_params=pltpu.CompilerParams(dimension_semantics=("arbitrary",),
                                             vmem_limit_bytes=VMEM_LIMIT),
        name="inproj",
    )(*ins)


def _attn_kernel(sink_ref, q_ref, kc_ref, kp_ref, vc_ref, vp_ref, half_ref, o_ref):
    has_prev = pl.program_id(1) > 0
    qi = lax.broadcasted_iota(jnp.int32, (WINDOW, 2 * WINDOW), 0)
    kj = lax.broadcasted_iota(jnp.int32, (WINDOW, 2 * WINDOW), 1)
    valid = ((kj < WINDOW) & (kj > qi) & has_prev) | ((kj >= WINDOW) & (kj - WINDOW <= qi))
    left = lax.broadcasted_iota(jnp.int32, (WINDOW, LANES), 1) < HEAD_DIM
    half = (half_ref[0], half_ref[1])
    outs = []
    for h in range(N_KV_HEADS):
        cols = slice(h * LANES, (h + 1) * LANES)
        kcat = jnp.concatenate([kp_ref[:, cols], kc_ref[:, cols]], axis=0)
        vcat = jnp.concatenate([vp_ref[:, cols], vc_ref[:, cols]], axis=0)
        ks = [kcat * hm for hm in half]
        rhs = [jnp.concatenate([vcat * hm, hm], axis=1) for hm in half]
        for pr in range(GROUP // 2):
            pair = h * (GROUP // 2) + pr
            qp = q_ref[:, pair * LANES:(pair + 1) * LANES]
            acc = None
            m_side = []
            for side in range(2):
                s = jnp.where(valid, _dot_nt(qp, ks[side]), NEG_BIG)
                sink = sink_ref[2 * pair + side]
                m = jnp.maximum(jnp.max(s, axis=-1, keepdims=True), sink)
                p = jnp.exp(s - m).astype(BF16)
                d = _dot(p, rhs[side])
                acc = d if acc is None else acc + d
                m_side.append(jnp.exp(sink - m))
            den = acc[:, LANES:2 * LANES] + jnp.where(left, m_side[0], m_side[1])
            outs.append(acc[:, 0:LANES] / den)
    o_ref[...] = jnp.concatenate(outs, axis=1).astype(BF16)


def _attention(q, k, v, sinks, B, S):
    nb = S // WINDOW
    cur = lambda b, n: (b * nb + n, 0)
    prev = lambda b, n: (b * nb + jnp.maximum(n - 1, 0), 0)
    lane_left = np.arange(LANES) < HEAD_DIM
    half = jnp.asarray(np.broadcast_to(np.stack([lane_left, ~lane_left])[:, None, :],
                                       (2, 2 * WINDOW, LANES)), BF16)
    return pl.pallas_call(
        _attn_kernel,
        out_shape=jax.ShapeDtypeStruct((B * S, ATTN_WIDTH), BF16),
        grid=(B, nb),
        in_specs=[pl.BlockSpec(memory_space=pltpu.SMEM),
                  pl.BlockSpec((WINDOW, ATTN_WIDTH), cur),
                  pl.BlockSpec((WINDOW, 2 * KV_WIDTH), cur),
                  pl.BlockSpec((WINDOW, 2 * KV_WIDTH), prev),
                  pl.BlockSpec((WINDOW, 2 * KV_WIDTH), cur),
                  pl.BlockSpec((WINDOW, 2 * KV_WIDTH), prev),
                  pl.BlockSpec(half.shape, lambda b, n: (0, 0, 0))],
        out_specs=pl.BlockSpec((WINDOW, ATTN_WIDTH), cur),
        compiler_params=pltpu.CompilerParams(dimension_semantics=("arbitrary", "arbitrary"),
                                             vmem_limit_bytes=VMEM_LIMIT),
        name="swa_attention",
    )(sinks, q, k, k, v, v, half)


_LEVEL_HALVES = (1, 2, 4, 8, 16, 32)


def _hgrn_level_masks():
    t = np.arange(CHUNK)[:, None]
    s = np.arange(CHUNK)[None, :]
    masks = [((t // (2 * h)) == (s // (2 * h))) & ((t & h) != 0) & ((s & h) == 0) for h in _LEVEL_HALVES]
    return jnp.asarray(np.stack(masks), F32)


def _level_reference(b_ref, slot, half):
    if half >= 4:
        span = max(2 * half, 8)
        pieces = [jnp.broadcast_to(b_ref[slot, s + half - 1:s + half, :], (span, HGRN_DIM))
                  for s in range(0, CHUNK, span)]
    else:
        r8 = lax.broadcasted_iota(jnp.int32, (8, HGRN_DIM), 0)
        pieces = [jnp.where(r8 < 4,
                            jnp.broadcast_to(b_ref[slot, s + 1:s + 2, :], (8, HGRN_DIM)),
                            jnp.broadcast_to(b_ref[slot, s + 5:s + 6, :], (8, HGRN_DIM)))
                  for s in range(0, CHUNK, 8)]
    return pieces[0] if len(pieces) == 1 else jnp.concatenate(pieces, axis=0)


def _hgrn_kernel(hq_ref, lf_ref, hv_ref, hg_ref, nw_ref, tri_ref, lm_ref, o_ref, st_ref, b_ref):
    @pl.when(pl.program_id(1) == 0)
    def _():
        st_ref[...] = jnp.zeros_like(st_ref)

    tri2 = tri_ref[...]
    odd = (lax.broadcasted_iota(jnp.int32, (CHUNK, HGRN_DIM), 0) & 1) != 0
    masks = [lm_ref[li] != 0.0 for li in range(len(_LEVEL_HALVES))]
    units = [(c, h) for c in range(hq_ref.shape[0] // CHUNK) for h in range(HGRN_HEADS)]
    sl = lambda c, h: (slice(c * CHUNK, (c + 1) * CHUNK), slice(h * HGRN_DIM, (h + 1) * HGRN_DIM))

    bs = []
    for u, (c, h) in enumerate(units):
        lf2 = lf_ref[sl(c, h)]
        hi = lf2.astype(BF16)
        lo = (lf2 - hi.astype(F32)).astype(BF16)
        b = _dot(tri2, jnp.concatenate([hi, lo], axis=0))
        b_ref[u] = b
        bs.append(b)

    outs, qs, ks, fs = [], [], [], []
    for u, (c, h) in enumerate(units):
        b = bs[u]
        q = hq_ref[sl(c, h)].astype(F32)
        v_bf = hv_ref[sl(c, h)]
        f = jnp.exp2(lf_ref[sl(c, h)])
        k = 1.0 - f
        b_last = b[CHUNK - 1:CHUNK, :]
        st = st_ref[h]
        o = _dot_nt((q * jnp.exp2(b)).astype(BF16), st.astype(BF16))
        k_out = (k * jnp.exp2(b_last - b)).astype(BF16)
        st_ref[h] = st * jnp.exp2(b_last) + _dot_tn(v_bf, k_out)
        outs.append(o + jnp.sum(q * k, axis=-1, keepdims=True) * v_bf.astype(F32))
        qs.append(q)
        ks.append(k)
        fs.append(f)

    accs = [jnp.zeros((CHUNK, CHUNK), F32) for _ in units]
    for li, half in enumerate(_LEVEL_HALVES):
        for u in range(len(units)):
            if half == 1:
                e = jnp.where(odd, fs[u], 1.0)
            else:
                e = jnp.exp2(-jnp.abs(bs[u] - _level_reference(b_ref, u, half)))
            accs[u] = jnp.where(masks[li], _dot_nt((qs[u] * e).astype(BF16), (ks[u] * e).astype(BF16)), accs[u])

    for u, (c, h) in enumerate(units):
        o = outs[u] + _dot(accs[u].astype(BF16), hv_ref[sl(c, h)])
        y = _rms(o, nw_ref[...]) * hg_ref[sl(c, h)].astype(F32)
        o_ref[sl(c, h)] = y.astype(BF16)


def _hgrn(hq, lf, hv, hg, hgrn_norm_w, B, S, tb):
    nt = S // tb
    blk = pl.BlockSpec((tb, HGRN_WIDTH), lambda b, n: (b * nt + n, 0))
    tri = np.tril(np.ones((CHUNK, CHUNK), np.float32))
    tri = jnp.asarray(np.concatenate([tri, tri], axis=1), BF16)
    lm = _hgrn_level_masks()
    return pl.pallas_call(
        _hgrn_kernel,
        out_shape=jax.ShapeDtypeStruct((B * S, HGRN_WIDTH), BF16),
        grid=(B, nt),
        in_specs=[blk, blk, blk, blk,
                  pl.BlockSpec((1, HGRN_DIM), lambda b, n: (0, 0)),
                  pl.BlockSpec(tri.shape, lambda b, n: (0, 0)),
                  pl.BlockSpec(lm.shape, lambda b, n: (0, 0, 0))],
        out_specs=blk,
        scratch_shapes=[pltpu.VMEM((HGRN_HEADS, HGRN_DIM, HGRN_DIM), F32),
                        pltpu.VMEM((tb // CHUNK * HGRN_HEADS, CHUNK, HGRN_DIM), F32)],
        compiler_params=pltpu.CompilerParams(dimension_semantics=("arbitrary", "arbitrary"),
                                             vmem_limit_bytes=VMEM_LIMIT),
        name="hgrn2",
    )(hq, lf, hv, hg, hgrn_norm_w[None, :].astype(F32), tri, lm)


def _merge_kernel(x_ref, attn_ref, hgrn_ref, n1w_ref, wz_ref, wba_ref, wbh_ref, wout_ref, n2w_ref,
                  wr_hi_ref, wr_lo_ref, br_ref, utri_ref, x1_out, hn_out, route_out, count_out, run_ref):
    @pl.when(pl.program_id(0) == 0)
    def _():
        run_ref[...] = jnp.zeros_like(run_ref)

    x = x_ref[...]
    xn = _rms(x, n1w_ref[...]).astype(BF16)
    za = _sigmoid(_dot(xn, wz_ref[:, 0:D_MODEL]))
    zb = _sigmoid(_dot(xn, wz_ref[:, D_MODEL:2 * D_MODEL]))
    mixed = za * _dot(attn_ref[...], wba_ref[...]) + zb * _dot(hgrn_ref[...], wbh_ref[...])
    x1 = x + _dot(mixed.astype(BF16), wout_ref[...])
    x1_out[...] = x1
    hn = _rms(x1, n2w_ref[...])
    hn_out[...] = _pack_bf16_pairs(hn)

    hn_hi = hn.astype(BF16)
    hn_lo = (hn - hn_hi.astype(F32)).astype(BF16)
    logits = (_dot_nt(wr_hi_ref[...], hn_hi) + _dot_nt(wr_hi_ref[...], hn_lo) + _dot_nt(wr_lo_ref[...], hn_hi)
              + br_ref[...])
    r = lax.broadcasted_iota(jnp.int32, logits.shape, 0).astype(F32)
    far = float(ROUTER_ROWS)
    cmax = lambda a: jnp.max(a, axis=0, keepdims=True)
    cmin = lambda a: jnp.min(a, axis=0, keepdims=True)
    csum = lambda a: jnp.sum(a, axis=0, keepdims=True)

    lg = jnp.where(r < N_GROUPS, logits, NEG_BIG)
    mg = cmax(lg)
    gsel = cmin(jnp.where(lg == mg, r, far))
    pgsel = 1.0 / csum(jnp.exp(lg - mg))

    lo = N_GROUPS + EXPERTS_PER_GROUP * gsel
    le = jnp.where((r >= lo) & (r < lo + EXPERTS_PER_GROUP), logits, NEG_BIG)
    m1 = cmax(le)
    i1 = cmin(jnp.where(le == m1, r, far))
    se = csum(jnp.exp(le - m1))
    le2 = jnp.where(r == i1, NEG_BIG, le)
    m2 = cmax(le2)
    i2 = cmin(jnp.where(le2 == m2, r, far))
    top0 = 1.0 / se
    top1 = jnp.exp(m2 - m1) / se
    tsum = top0 + top1
    w0 = pgsel * top0 / tsum
    w1 = pgsel * top1 / tsum

    sel1 = r == i1
    sel2 = r == i2
    onehot = (sel1 | sel2).astype(BF16)
    before = _dot(onehot, utri_ref[...]) + run_ref[...]
    r0 = csum(jnp.where(sel1, before, 0.0))
    r1 = csum(jnp.where(sel2, before, 0.0))
    run_new = run_ref[...] + _dot(onehot, jnp.ones(utri_ref.shape, BF16))
    run_ref[...] = run_new
    count_out[...] = run_new

    row8 = lax.broadcasted_iota(jnp.int32, route_out.shape, 0)
    vals = (i1 - N_GROUPS, i2 - N_GROUPS, w0, w1, r0, r1)
    route = jnp.zeros(route_out.shape, F32)
    for j, val in enumerate(vals):
        route = jnp.where(row8 == j, val, route)
    route_out[...] = route


def _merge(x2, attn, hgrn, norm1_w, w_z, w_ba, w_bh, w_out, norm2_w, wr_hi, wr_lo, br, tm):
    T = x2.shape[0]
    row = lambda w: pl.BlockSpec((tm, w), lambda i: (i, 0))
    full = lambda a: pl.BlockSpec(a.shape, lambda i: (0,) * a.ndim)
    utri = jnp.asarray(np.triu(np.ones((tm, tm), np.float32), 1), BF16)
    ins = [x2, attn, hgrn, norm1_w[None, :], w_z, w_ba, w_bh, w_out, norm2_w[None, :], wr_hi, wr_lo, br, utri]
    in_specs = [row(D_MODEL), row(ATTN_WIDTH), row(HGRN_WIDTH)] + [full(a) for a in ins[3:]]
    return pl.pallas_call(
        _merge_kernel,
        out_shape=[jax.ShapeDtypeStruct((T, D_MODEL), F32), jax.ShapeDtypeStruct((T, HALF), jnp.uint32),
                   jax.ShapeDtypeStruct((ROUTE_ROWS, T), F32), jax.ShapeDtypeStruct((ROUTER_ROWS, tm), F32)],
        grid=(T // tm,),
        in_specs=in_specs,
        out_specs=[row(D_MODEL), row(HALF), pl.BlockSpec((ROUTE_ROWS, tm), lambda i: (0, i)),
                   pl.BlockSpec((ROUTER_ROWS, tm), lambda i: (0, 0))],
        scratch_shapes=[pltpu.VMEM((ROUTER_ROWS, tm), F32)],
        compiler_params=pltpu.CompilerParams(dimension_semantics=("arbitrary",),
                                             vmem_limit_bytes=VMEM_LIMIT),
        name="merge_router",
    )(*ins)


HALF = D_MODEL // 2


def _pack_bf16_pairs(x):
    bits = pltpu.bitcast(x.astype(BF16).astype(F32), jnp.uint32)
    return (bits[:, :HALF] >> 16) | bits[:, HALF:]


def _unpack_bf16_pairs(words):
    lo = pltpu.bitcast(words << 16, F32)
    hi = pltpu.bitcast(words & jnp.uint32(0xFFFF0000), F32)
    return lo, hi


def _issue_row_gather(idx_ref, n, src_hbm, dst_ref, sem):
    def issue(r, carry):
        pltpu.make_async_copy(src_hbm.at[pl.ds(idx_ref[0, 0, r], 1), :],
                              dst_ref.at[pl.ds(r, 1), :], sem).start()
        return carry
    lax.fori_loop(0, n, issue, 0, unroll=16)


def _wait_row_gather(n, src_hbm, dst_ref, sem):
    pltpu.make_async_copy(src_hbm.at[pl.ds(0, n), :], dst_ref.at[pl.ds(0, n), :], sem).wait()


def _gather_rows(idx_ref, n, src_hbm, dst_ref, sem):
    _issue_row_gather(idx_ref, n, src_hbm, dst_ref, sem)
    _wait_row_gather(n, src_hbm, dst_ref, sem)


def _tile_positions(pos, tk):
    nt = pos.shape[0] // tk
    return pos.reshape(nt, tk, TOP_K).transpose(0, 2, 1).reshape(nt, 1, TOP_K * tk)


def _invmap_kernel(pos_ref, tok_ref):
    tokens = pos_ref.shape[2] // TOP_K
    base = pl.program_id(0) * tokens

    def body(j, carry):
        for k in range(TOP_K):
            tok_ref[pos_ref[0, 0, j * TOP_K + k]] = base + j
        return carry
    lax.fori_loop(0, tokens, body, 0, unroll=8)


def _invmap(pos, nb):
    n = pos.size
    return pl.pallas_call(
        _invmap_kernel,
        out_shape=jax.ShapeDtypeStruct((n,), jnp.int32),
        grid=(n // nb,),
        in_specs=[pl.BlockSpec((1, 1, nb), lambda i: (i, 0, 0), memory_space=pltpu.SMEM)],
        out_specs=pl.BlockSpec(memory_space=pltpu.SMEM),
        compiler_params=pltpu.CompilerParams(dimension_semantics=("arbitrary",)),
        name="moe_row_tokens",
    )(pos.reshape(n // nb, 1, nb))


def _expert_kernel(tile_ref, exp_ref, lo_ref, hi_ref, tok_ref, tokn_ref, hn_hbm, wg_ref, wu_ref, wd_ref, y_ref,
                   xbuf, wg_s, wu_s, wd_s, sem):
    w = pl.program_id(0)
    lo = lo_ref[w]
    hi = hi_ref[w]
    tile = tile_ref[w]
    slot = tile % 2
    n_tiles = hn_hbm.shape[0] * TOP_K // EXPERT_TILE
    prev = jnp.maximum(w - 1, 0)

    entered = (w == 0) | (tile != tile_ref[prev])
    whole = (lo == 0) & (hi == EXPERT_TILE)

    @pl.when(w == 0)
    def _():
        _issue_row_gather(tok_ref, EXPERT_TILE, hn_hbm, xbuf.at[0], sem.at[0])

    @pl.when(entered)
    def _():
        _wait_row_gather(EXPERT_TILE, hn_hbm, xbuf.at[slot], sem.at[slot])

        _issue_row_gather(tokn_ref, EXPERT_TILE, hn_hbm, xbuf.at[1 - slot], sem.at[1 - slot])

    @pl.when((w == 0) | (exp_ref[w] != exp_ref[prev]))
    def _():
        wg_s[...] = wg_ref[0].astype(BF16)
        wu_s[...] = wu_ref[0].astype(BF16)
        wd_s[...] = wd_ref[0].astype(BF16)

    def ffn(words):
        lo, hi = _unpack_bf16_pairs(words)
        lo = lo.astype(BF16)
        hi = hi.astype(BF16)
        gate = _dot(lo, wg_s[0:HALF, :]) + _dot(hi, wg_s[HALF:, :])
        up = _dot(lo, wu_s[0:HALF, :]) + _dot(hi, wu_s[HALF:, :])
        return _pack_bf16_pairs(_dot((gate * _sigmoid(gate) * up).astype(BF16), wd_s[...]))

    x_ref = xbuf.at[slot]

    @pl.when(whole)
    def _():
        y_ref[...] = ffn(x_ref[...])

    for j in range(EXPERT_TILE // MOE_BLOCK):
        r0 = j * MOE_BLOCK
        rows = slice(r0, r0 + MOE_BLOCK)

        @pl.when(jnp.logical_not(whole) & (lo < r0 + MOE_BLOCK) & (hi > r0))
        def _():
            y = ffn(x_ref[rows, :])
            rowi = lax.broadcasted_iota(jnp.int32, y.shape, 0) + r0
            mine = (rowi >= lo) & (rowi < hi)

            @pl.when(lo <= r0)
            def _():
                y_ref[rows, :] = jnp.where(mine, y, jnp.uint32(0))

            @pl.when(lo > r0)
            def _():
                y_ref[rows, :] = jnp.where(mine, y, y_ref[rows, :])

    @pl.when(w == pl.num_programs(0) - 1)
    def _():
        _wait_row_gather(EXPERT_TILE, hn_hbm, xbuf.at[1 - slot], sem.at[1 - slot])


def _experts(items, row_tok, hn, w_gate, w_up, w_down):
    tile, exp, lo, hi = items
    n_rows = row_tok.shape[0]
    n_tiles = n_rows // EXPERT_TILE
    tok3 = row_tok.reshape(n_tiles, 1, EXPERT_TILE)
    wspec = lambda shape: pl.BlockSpec((1,) + shape, lambda w, t, e, l, h: (e[w], 0, 0))
    tok_spec = lambda ahead: pl.BlockSpec(
        (1, 1, EXPERT_TILE), lambda w, t, e, l, h: (jnp.minimum(t[w] + ahead, n_tiles - 1), 0, 0),
        memory_space=pltpu.SMEM)
    return pl.pallas_call(
        _expert_kernel,
        out_shape=jax.ShapeDtypeStruct((n_rows, HALF), jnp.uint32),
        grid_spec=pltpu.PrefetchScalarGridSpec(
            num_scalar_prefetch=4,
            grid=(tile.shape[0],),
            in_specs=[tok_spec(0), tok_spec(1), pl.BlockSpec(memory_space=pl.ANY),
                      wspec((D_MODEL, EXPERT_FF)), wspec((D_MODEL, EXPERT_FF)), wspec((EXPERT_FF, D_MODEL))],
            out_specs=pl.BlockSpec((EXPERT_TILE, HALF), lambda w, t, e, l, h: (t[w], 0)),
            scratch_shapes=[pltpu.VMEM((2, EXPERT_TILE, HALF), jnp.uint32),
                            pltpu.VMEM((D_MODEL, EXPERT_FF), BF16),
                            pltpu.VMEM((D_MODEL, EXPERT_FF), BF16),
                            pltpu.VMEM((EXPERT_FF, D_MODEL), BF16),
                            pltpu.SemaphoreType.DMA((2,))]),
        compiler_params=pltpu.CompilerParams(dimension_semantics=("arbitrary",),
                                             vmem_limit_bytes=VMEM_LIMIT),
        name="moe_experts",
    )(tile, exp, lo, hi, tok3, tok3, hn, w_gate, w_up, w_down)


def _combine_kernel(pos_ref, x1_ref, w_ref, y_hbm, o_ref, ybuf, sem):
    tk = x1_ref.shape[0]
    _gather_rows(pos_ref, TOP_K * tk, y_hbm, ybuf, sem)
    w0 = w_ref[:, 0:1]
    w1 = w_ref[:, 1:2]
    lo0, hi0 = _unpack_bf16_pairs(ybuf[0:tk, :])
    lo1, hi1 = _unpack_bf16_pairs(ybuf[tk:2 * tk, :])
    o_ref[:, 0:HALF] = x1_ref[:, 0:HALF] + w0 * lo0 + w1 * lo1
    o_ref[:, HALF:] = x1_ref[:, HALF:] + w0 * hi0 + w1 * hi1


def _combine(pos, x1, gate_w, y, tk):
    T = x1.shape[0]
    return pl.pallas_call(
        _combine_kernel,
        out_shape=jax.ShapeDtypeStruct((T, D_MODEL), F32),
        grid=(T // tk,),
        in_specs=[pl.BlockSpec((1, 1, TOP_K * tk), lambda i: (i, 0, 0), memory_space=pltpu.SMEM),
                  pl.BlockSpec((tk, D_MODEL), lambda i: (i, 0)),
                  pl.BlockSpec((tk, TOP_K), lambda i: (i, 0)),
                  pl.BlockSpec(memory_space=pl.ANY)],
        out_specs=pl.BlockSpec((tk, D_MODEL), lambda i: (i, 0)),
        scratch_shapes=[pltpu.VMEM((TOP_K * tk, HALF), jnp.uint32), pltpu.SemaphoreType.DMA],
        compiler_params=pltpu.CompilerParams(dimension_semantics=("arbitrary",),
                                             vmem_limit_bytes=VMEM_LIMIT),
        name="moe_combine",
    )(_tile_positions(pos, tk), x1, gate_w, y)


def _routing_tables(route, counts, T):
    e = route[0:TOP_K].astype(jnp.int32)
    rank = route[4:4 + TOP_K].astype(jnp.int32)
    counts = counts.astype(jnp.int32)
    starts = jnp.cumsum(counts) - counts
    ids = jnp.arange(N_EXPERTS, dtype=jnp.int32)
    pos = rank + jnp.sum(jnp.where(e[:, :, None] == ids, starts, 0), axis=-1)
    n_rows = TOP_K * T
    cuts = jnp.sort(jnp.concatenate([jnp.arange(0, n_rows, EXPERT_TILE, dtype=jnp.int32), starts[1:]]))
    ends = jnp.concatenate([cuts[1:], jnp.full((1,), n_rows, jnp.int32)])
    tile = jnp.minimum(cuts // EXPERT_TILE, n_rows // EXPERT_TILE - 1)
    exp = jnp.clip(jnp.sum(starts[None, :] <= cuts[:, None], axis=1) - 1, 0, N_EXPERTS - 1).astype(jnp.int32)
    lo = cuts - tile * EXPERT_TILE
    hi = ends - tile * EXPERT_TILE
    return pos.T, (tile, exp, lo, hi)


def _pick_tile(n, pref):
    t = pref
    while n % t:
        t //= 2
    return t


def kernel(x, positions, norm1_w, w_in, q_norm_w, k_norm_w, attn_sinks, hgrn_lower_bounds, hgrn_norm_w,
           w_branch_attn, w_branch_hgrn, w_out, norm2_w, w_router_group, b_router_group, w_router_expert,
           b_router_expert, w_gate_experts, w_up_experts, w_down_experts):
    B, S, D = x.shape
    T = B * S
    x2 = x.reshape(T, D)
    tm = _pick_tile(T, 512)

    inv_freq = ROPE_THETA ** (-jnp.arange(0, ROT_DIM, 2, dtype=F32) / ROT_DIM)
    ang = positions.astype(F32).reshape(T, 1) * inv_freq[None, :]
    cs = jnp.concatenate([jnp.cos(ang), jnp.sin(ang)], axis=-1)

    w_in0 = w_in[0]
    w_in_a = w_in0[:, :_OFF_Z].astype(BF16)
    w_z = w_in0[:, _OFF_Z:].astype(BF16)

    q, k, v, hq, lf, hv, hg = _inproj(x2, norm1_w[0], w_in_a, cs, q_norm_w[0], k_norm_w[0],
                                      hgrn_lower_bounds.astype(F32), tm)
    attn = _attention(q, k, v, attn_sinks[0].astype(F32), B, S)
    hgrn = _hgrn(hq, lf, hv, hg, hgrn_norm_w[0], B, S, _pick_tile(S, 256))

    pad = ROUTER_ROWS - N_GROUPS - N_EXPERTS
    w_r = jnp.concatenate([w_router_group[0].T, w_router_expert[0].T, jnp.zeros((pad, D), F32)], axis=0)
    wr_hi = w_r.astype(BF16)
    wr_lo = (w_r - wr_hi.astype(F32)).astype(BF16)
    b_r = jnp.concatenate([b_router_group[0], b_router_expert[0], jnp.zeros((pad,), F32)]).astype(F32)
    b_r = jnp.broadcast_to(b_r[:, None], (ROUTER_ROWS, tm))

    x1, hn, route, counts = _merge(x2, attn, hgrn, norm1_w[0], w_z, w_branch_attn[0].astype(BF16),
                                   w_branch_hgrn[0].astype(BF16), w_out[0].astype(BF16), norm2_w[0],
                                   wr_hi, wr_lo, b_r, tm)

    pos, items = _routing_tables(route, counts[N_GROUPS:N_GROUPS + N_EXPERTS, 0], T)
    tk = _pick_tile(T, 128)
    row_tok = _invmap(pos, _pick_tile(pos.size, 4096))
    y = _experts(items, row_tok, hn, w_gate_experts[0], w_up_experts[0], w_down_experts[0])
    out = _combine(pos, x1, route[2:2 + TOP_K].T, y, tk)
    return out.reshape(B, S, D)
```

```python
import functools

import numpy as np
import jax
import jax.numpy as jnp
from jax import lax
from jax.experimental import pallas as pl
from jax.experimental.pallas import tpu as pltpu

F32 = jnp.float32
BF16 = jnp.bfloat16

D_MODEL = 1024
N_Q_HEADS = 8
N_KV_HEADS = 2
GROUP = N_Q_HEADS // N_KV_HEADS
HEAD_DIM = 64
ROT_DIM = HEAD_DIM // 4
ROT_HALF = ROT_DIM // 2
ROPE_THETA = 500000.0
WINDOW = 128
ATTN_WIDTH = N_Q_HEADS * HEAD_DIM
KV_WIDTH = N_KV_HEADS * HEAD_DIM

HGRN_HEADS = 4
HGRN_DIM = 128
HGRN_WIDTH = HGRN_HEADS * HGRN_DIM
CHUNK = 64

N_GROUPS = 4
EXPERTS_PER_GROUP = 8
N_EXPERTS = N_GROUPS * EXPERTS_PER_GROUP
TOP_K = 2
EXPERT_FF = 512
MOE_BLOCK = 128
EXPERT_TILE = 512
NORM_EPS = 1e-6
ROUTER_ROWS = 40
ROUTE_ROWS = 8

LANES = 128
NEG_BIG = -1e30
LOG2_E = 1.4426950408889634

_OFF_Q, _OFF_K, _OFF_V = 0, ATTN_WIDTH, ATTN_WIDTH + KV_WIDTH
_OFF_HQ = ATTN_WIDTH + 2 * KV_WIDTH
_OFF_HF = _OFF_HQ + HGRN_WIDTH
_OFF_HI = _OFF_HF + HGRN_WIDTH
_OFF_HG = _OFF_HI + HGRN_WIDTH
_OFF_Z = _OFF_HG + HGRN_WIDTH

VMEM_LIMIT = 56 * 1024 * 1024


def _split3(a):
    hi = a.astype(BF16)
    r1 = a - hi.astype(F32)
    mid = r1.astype(BF16)
    lo = (r1 - mid.astype(F32)).astype(BF16)
    return hi, mid, lo


def _dot(a, b):
    return jnp.dot(a, b, preferred_element_type=F32)


def _dot_nt(a, b):
    return lax.dot_general(a, b, (((1,), (1,)), ((), ())), preferred_element_type=F32)


def _dot_tn(a, b):
    return lax.dot_general(a, b, (((0,), (0,)), ((), ())), preferred_element_type=F32)


def _exact_lhs_dot(m01, a):
    hi, mid, lo = _split3(a)
    return _dot(m01, hi) + _dot(m01, mid) + _dot(m01, lo)


def _exact_rhs_dot(a, m01):
    hi, mid, lo = _split3(a)
    return _dot(hi, m01) + _dot(mid, m01) + _dot(lo, m01)


def _sigmoid(x):
    return 1.0 / (1.0 + jnp.exp(-x))


def _rms(x, w):
    ms = jnp.mean(x * x, axis=-1, keepdims=True)
    return x * lax.rsqrt(ms + NORM_EPS) * w


def _inproj_kernel(x_ref, n1w_ref, w_ref, cs_ref, rope_e_ref, rope_c0_ref, qw_ref, kw_ref,
                   mq_ref, mk_ref, lbp_ref, dup_ref,
                   q_out, k_out, v_out, hq_out, lf_out, hv_out, hg_out):
    xn = _rms(x_ref[...], n1w_ref[...]).astype(BF16)

    def proj(off, width):
        return _dot(xn, w_ref[:, off:off + width])

    tabs = _exact_rhs_dot(cs_ref[...], rope_e_ref[...])
    c_tab = tabs[:, 0:LANES] + rope_c0_ref[...]
    s1_tab = tabs[:, LANES:2 * LANES]
    s2_tab = tabs[:, 2 * LANES:3 * LANES]

    def norm_rope(t, mavg_ref, w_row, scale):
        ms = _dot((t * t).astype(BF16), mavg_ref[...])
        tn = t * lax.rsqrt(ms + NORM_EPS) * w_row
        if scale != 1.0:
            tn = tn * scale
        outs = []
        for j in range(t.shape[1] // LANES):
            c = tn[:, j * LANES:(j + 1) * LANES]
            outs.append(c * c_tab
                        + pltpu.roll(c, LANES - ROT_HALF, 1) * s1_tab
                        + pltpu.roll(c, ROT_HALF, 1) * s2_tab)
        return outs[0] if len(outs) == 1 else jnp.concatenate(outs, axis=1)

    q_out[...] = norm_rope(proj(_OFF_Q, ATTN_WIDTH), mq_ref, qw_ref[...], HEAD_DIM ** -0.5).astype(BF16)
    k_rot = norm_rope(proj(_OFF_K, KV_WIDTH), mk_ref, kw_ref[...], 1.0).astype(BF16)
    k_out[...] = _dot(k_rot, dup_ref[...]).astype(BF16)
    v_out[...] = _dot(proj(_OFF_V, KV_WIDTH).astype(BF16), dup_ref[...]).astype(BF16)

    hq = proj(_OFF_HQ, HGRN_WIDTH)
    hq_out[...] = (hq * _sigmoid(hq)).astype(BF16)
    h0 = lbp_ref[0:1, :]
    h1 = lbp_ref[1:2, :]
    hm = jnp.maximum(h0, h1)
    e0 = jnp.exp(h0 - hm)
    e1 = jnp.exp(h1 - hm)
    lb = e0 / (e0 + e1)
    fg = lb + (1.0 - lb) * _sigmoid(proj(_OFF_HF, HGRN_WIDTH))
    lf_out[...] = jnp.log(fg) * LOG2_E
    hv_out[...] = proj(_OFF_HI, HGRN_WIDTH).astype(BF16)
    hg = proj(_OFF_HG, HGRN_WIDTH)
    hg_out[...] = (hg * _sigmoid(hg)).astype(BF16)


def _rope_constants():
    e = np.zeros((2 * ROT_HALF, 3 * LANES), np.float32)
    c0 = np.zeros((1, LANES), np.float32)
    for lane in range(LANES):
        d = lane % HEAD_DIM
        if d < ROT_HALF:
            e[d, lane] = 1.0
            e[ROT_HALF + d, LANES + lane] = -1.0
        elif d < ROT_DIM:
            e[d - ROT_HALF, lane] = 1.0
            e[ROT_HALF + d - ROT_HALF, 2 * LANES + lane] = 1.0
        else:
            c0[0, lane] = 1.0
    return jnp.asarray(e, BF16), jnp.asarray(c0, F32)


def _head_mean_matrix(width):
    idx = np.arange(width) // HEAD_DIM
    m = (idx[:, None] == idx[None, :]).astype(np.float32) / HEAD_DIM
    return jnp.asarray(m, BF16)


def _inproj(x2, norm1_w, w_in_a, cs, q_norm_w, k_norm_w, lbp, tm):
    T = x2.shape[0]
    rope_e, rope_c0 = _rope_constants()
    qw = jnp.tile(q_norm_w.astype(F32), N_Q_HEADS)[None, :]
    kw = jnp.tile(k_norm_w.astype(F32), N_KV_HEADS)[None, :]
    mq = _head_mean_matrix(ATTN_WIDTH)
    mk = _head_mean_matrix(KV_WIDTH)
    row = lambda w: pl.BlockSpec((tm, w), lambda i: (i, 0))
    full = lambda a: pl.BlockSpec(a.shape, lambda i: (0,) * a.ndim)
    src = HEAD_DIM * (np.arange(2 * KV_WIDTH) // LANES) + np.arange(2 * KV_WIDTH) % HEAD_DIM
    dup = jnp.asarray(np.arange(KV_WIDTH)[:, None] == src[None, :], BF16)
    ins = [x2, norm1_w[None, :], w_in_a, cs, rope_e, rope_c0, qw, kw, mq, mk, lbp, dup]
    in_specs = [row(D_MODEL), full(ins[1]), full(w_in_a), row(cs.shape[1])] + [full(a) for a in ins[4:]]
    outs = [(ATTN_WIDTH, BF16), (2 * KV_WIDTH, BF16), (2 * KV_WIDTH, BF16), (HGRN_WIDTH, BF16),
            (HGRN_WIDTH, F32), (HGRN_WIDTH, BF16), (HGRN_WIDTH, BF16)]
    return pl.pallas_call(
        _inproj_kernel,
        out_shape=[jax.ShapeDtypeStruct((T, w), dt) for w, dt in outs],
        grid=(T // tm,),
        in_specs=in_specs,
        out_specs=[row(w) for w, _ in outs],
        compiler_params=pltpu.CompilerParams(dimension_semantics=("arbitrary",),
                                             vmem_limit_bytes=VMEM_LIMIT),
        name="inproj",
    )(*ins)


def _attn_kernel(sink_ref, q_ref, kc_ref, kp_ref, vc_ref, vp_ref, half_ref, o_ref):
    has_prev = pl.program_id(1) > 0
    qi = lax.broadcasted_iota(jnp.int32, (WINDOW, 2 * WINDOW), 0)
    kj = lax.broadcasted_iota(jnp.int32, (WINDOW, 2 * WINDOW), 1)
    valid = ((kj < WINDOW) & (kj > qi) & has_prev) | ((kj >= WINDOW) & (kj - WINDOW <= qi))
    left = lax.broadcasted_iota(jnp.int32, (WINDOW, LANES), 1) < HEAD_DIM
    half = (half_ref[0], half_ref[1])
    outs = []
    for h in range(N_KV_HEADS):
        cols = slice(h * LANES, (h + 1) * LANES)
        kcat = jnp.concatenate([kp_ref[:, cols], kc_ref[:, cols]], axis=0)
        vcat = jnp.concatenate([vp_ref[:, cols], vc_ref[:, cols]], axis=0)
        ks = [kcat * hm for hm in half]
        rhs = [jnp.concatenate([vcat * hm, hm], axis=1) for hm in half]
        for pr in range(GROUP // 2):
            pair = h * (GROUP // 2) + pr
            qp = q_ref[:, pair * LANES:(pair + 1) * LANES]
            acc = None
            m_side = []
            for side in range(2):
                s = jnp.where(valid, _dot_nt(qp, ks[side]), NEG_BIG)
                sink = sink_ref[2 * pair + side]
                m = jnp.maximum(jnp.max(s, axis=-1, keepdims=True), sink)
                p = jnp.exp(s - m).astype(BF16)
                d = _dot(p, rhs[side])
                acc = d if acc is None else acc + d
                m_side.append(jnp.exp(sink - m))
            den = acc[:, LANES:2 * LANES] + jnp.where(left, m_side[0], m_side[1])
            outs.append(acc[:, 0:LANES] / den)
    o_ref[...] = jnp.concatenate(outs, axis=1).astype(BF16)


def _attention(q, k, v, sinks, B, S):
    nb = S // WINDOW
    cur = lambda b, n: (b * nb + n, 0)
    prev = lambda b, n: (b * nb + jnp.maximum(n - 1, 0), 0)
    lane_left = np.arange(LANES) < HEAD_DIM
    half = jnp.asarray(np.broadcast_to(np.stack([lane_left, ~lane_left])[:, None, :],
                                       (2, 2 * WINDOW, LANES)), BF16)
    return pl.pallas_call(
        _attn_kernel,
        out_shape=jax.ShapeDtypeStruct((B * S, ATTN_WIDTH), BF16),
        grid=(B, nb),
        in_specs=[pl.BlockSpec(memory_space=pltpu.SMEM),
                  pl.BlockSpec((WINDOW, ATTN_WIDTH), cur),
                  pl.BlockSpec((WINDOW, 2 * KV_WIDTH), cur),
                  pl.BlockSpec((WINDOW, 2 * KV_WIDTH), prev),
                  pl.BlockSpec((WINDOW, 2 * KV_WIDTH), cur),
                  pl.BlockSpec((WINDOW, 2 * KV_WIDTH), prev),
                  pl.BlockSpec(half.shape, lambda b, n: (0, 0, 0))],
        out_specs=pl.BlockSpec((WINDOW, ATTN_WIDTH), cur),
        compiler_params=pltpu.CompilerParams(dimension_semantics=("arbitrary", "arbitrary"),
                                             vmem_limit_bytes=VMEM_LIMIT),
        name="swa_attention",
    )(sinks, q, k, k, v, v, half)


_LEVEL_HALVES = (1, 2, 4, 8, 16, 32)


def _hgrn_level_masks():
    t = np.arange(CHUNK)[:, None]
    s = np.arange(CHUNK)[None, :]
    masks = [((t // (2 * h)) == (s // (2 * h))) & ((t & h) != 0) & ((s & h) == 0) for h in _LEVEL_HALVES]
    return jnp.asarray(np.stack(masks), F32)


def _level_reference(b_ref, slot, half):
    if half >= 4:
        span = max(2 * half, 8)
        pieces = [jnp.broadcast_to(b_ref[slot, s + half - 1:s + half, :], (span, HGRN_DIM))
                  for s in range(0, CHUNK, span)]
    else:
        r8 = lax.broadcasted_iota(jnp.int32, (8, HGRN_DIM), 0)
        pieces = [jnp.where(r8 < 4,
                            jnp.broadcast_to(b_ref[slot, s + 1:s + 2, :], (8, HGRN_DIM)),
                            jnp.broadcast_to(b_ref[slot, s + 5:s + 6, :], (8, HGRN_DIM)))
                  for s in range(0, CHUNK, 8)]
    return pieces[0] if len(pieces) == 1 else jnp.concatenate(pieces, axis=0)


def _hgrn_kernel(hq_ref, lf_ref, hv_ref, hg_ref, nw_ref, tri_ref, lm_ref, o_ref, st_ref, b_ref):
    @pl.when(pl.program_id(1) == 0)
    def _():
        st_ref[...] = jnp.zeros_like(st_ref)

    tri2 = tri_ref[...]
    odd = (lax.broadcasted_iota(jnp.int32, (CHUNK, HGRN_DIM), 0) & 1) != 0
    masks = [lm_ref[li] != 0.0 for li in range(len(_LEVEL_HALVES))]
    units = [(c, h) for c in range(hq_ref.shape[0] // CHUNK) for h in range(HGRN_HEADS)]
    sl = lambda c, h: (slice(c * CHUNK, (c + 1) * CHUNK), slice(h * HGRN_DIM, (h + 1) * HGRN_DIM))

    bs = []
    for u, (c, h) in enumerate(units):
        lf2 = lf_ref[sl(c, h)]
        hi = lf2.astype(BF16)
        lo = (lf2 - hi.astype(F32)).astype(BF16)
        b = _dot(tri2, jnp.concatenate([hi, lo], axis=0))
        b_ref[u] = b
        bs.append(b)

    outs, qs, ks, fs = [], [], [], []
    for u, (c, h) in enumerate(units):
        b = bs[u]
        q = hq_ref[sl(c, h)].astype(F32)
        v_bf = hv_ref[sl(c, h)]
        f = jnp.exp2(lf_ref[sl(c, h)])
        k = 1.0 - f
        b_last = b[CHUNK - 1:CHUNK, :]
        st = st_ref[h]
        o = _dot_nt((q * jnp.exp2(b)).astype(BF16), st.astype(BF16))
        k_out = (k * jnp.exp2(b_last - b)).astype(BF16)
        st_ref[h] = st * jnp.exp2(b_last) + _dot_tn(v_bf, k_out)
        outs.append(o + jnp.sum(q * k, axis=-1, keepdims=True) * v_bf.astype(F32))
        qs.append(q)
        ks.append(k)
        fs.append(f)

    accs = [jnp.zeros((CHUNK, CHUNK), F32) for _ in units]
    for li, half in enumerate(_LEVEL_HALVES):
        for u in range(len(units)):
            if half == 1:
                e = jnp.where(odd, fs[u], 1.0)
            else:
                e = jnp.exp2(-jnp.abs(bs[u] - _level_reference(b_ref, u, half)))
            accs[u] = jnp.where(masks[li], _dot_nt((qs[u] * e).astype(BF16), (ks[u] * e).astype(BF16)), accs[u])

    for u, (c, h) in enumerate(units):
        o = outs[u] + _dot(accs[u].astype(BF16), hv_ref[sl(c, h)])
        y = _rms(o, nw_ref[...]) * hg_ref[sl(c, h)].astype(F32)
        o_ref[sl(c, h)] = y.astype(BF16)


def _hgrn(hq, lf, hv, hg, hgrn_norm_w, B, S, tb):
    nt = S // tb
    blk = pl.BlockSpec((tb, HGRN_WIDTH), lambda b, n: (b * nt + n, 0))
    tri = np.tril(np.ones((CHUNK, CHUNK), np.float32))
    tri = jnp.asarray(np.concatenate([tri, tri], axis=1), BF16)
    lm = _hgrn_level_masks()
    return pl.pallas_call(
        _hgrn_kernel,
        out_shape=jax.ShapeDtypeStruct((B * S, HGRN_WIDTH), BF16),
        grid=(B, nt),
        in_specs=[blk, blk, blk, blk,
                  pl.BlockSpec((1, HGRN_DIM), lambda b, n: (0, 0)),
                  pl.BlockSpec(tri.shape, lambda b, n: (0, 0)),
                  pl.BlockSpec(lm.shape, lambda b, n: (0, 0, 0))],
        out_specs=blk,
        scratch_shapes=[pltpu.VMEM((HGRN_HEADS, HGRN_DIM, HGRN_DIM), F32),
                        pltpu.VMEM((tb // CHUNK * HGRN_HEADS, CHUNK, HGRN_DIM), F32)],
        compiler_params=pltpu.CompilerParams(dimension_semantics=("arbitrary", "arbitrary"),
                                             vmem_limit_bytes=VMEM_LIMIT),
        name="hgrn2",
    )(hq, lf, hv, hg, hgrn_norm_w[None, :].astype(F32), tri, lm)


def _merge_kernel(x_ref, attn_ref, hgrn_ref, n1w_ref, wz_ref, wba_ref, wbh_ref, wout_ref, n2w_ref,
                  wr_hi_ref, wr_lo_ref, br_ref, utri_ref, x1_out, hn_out, route_out, count_out, run_ref):
    @pl.when(pl.program_id(0) == 0)
    def _():
        run_ref[...] = jnp.zeros_like(run_ref)

    x = x_ref[...]
    xn = _rms(x, n1w_ref[...]).astype(BF16)
    za = _sigmoid(_dot(xn, wz_ref[:, 0:D_MODEL]))
    zb = _sigmoid(_dot(xn, wz_ref[:, D_MODEL:2 * D_MODEL]))
    mixed = za * _dot(attn_ref[...], wba_ref[...]) + zb * _dot(hgrn_ref[...], wbh_ref[...])
    x1 = x + _dot(mixed.astype(BF16), wout_ref[...])
    x1_out[...] = x1
    hn = _rms(x1, n2w_ref[...])
    hn_out[...] = _pack_bf16_pairs(hn)

    hn_hi = hn.astype(BF16)
    hn_lo = (hn - hn_hi.astype(F32)).astype(BF16)
    logits = (_dot_nt(wr_hi_ref[...], hn_hi) + _dot_nt(wr_hi_ref[...], hn_lo) + _dot_nt(wr_lo_ref[...], hn_hi)
              + br_ref[...])
    r = lax.broadcasted_iota(jnp.int32, logits.shape, 0).astype(F32)
    far = float(ROUTER_ROWS)
    cmax = lambda a: jnp.max(a, axis=0, keepdims=True)
    cmin = lambda a: jnp.min(a, axis=0, keepdims=True)
    csum = lambda a: jnp.sum(a, axis=0, keepdims=True)

    lg = jnp.where(r < N_GROUPS, logits, NEG_BIG)
    mg = cmax(lg)
    gsel = cmin(jnp.where(lg == mg, r, far))
    pgsel = 1.0 / csum(jnp.exp(lg - mg))

    lo = N_GROUPS + EXPERTS_PER_GROUP * gsel
    le = jnp.where((r >= lo) & (r < lo + EXPERTS_PER_GROUP), logits, NEG_BIG)
    m1 = cmax(le)
    i1 = cmin(jnp.where(le == m1, r, far))
    se = csum(jnp.exp(le - m1))
    le2 = jnp.where(r == i1, NEG_BIG, le)
    m2 = cmax(le2)
    i2 = cmin(jnp.where(le2 == m2, r, far))
    top0 = 1.0 / se
    top1 = jnp.exp(m2 - m1) / se
    tsum = top0 + top1
    w0 = pgsel * top0 / tsum
    w1 = pgsel * top1 / tsum

    sel1 = r == i1
    sel2 = r == i2
    onehot = (sel1 | sel2).astype(BF16)
    before = _dot(onehot, utri_ref[...]) + run_ref[...]
    r0 = csum(jnp.where(sel1, before, 0.0))
    r1 = csum(jnp.where(sel2, before, 0.0))
    run_new = run_ref[...] + _dot(onehot, jnp.ones(utri_ref.shape, BF16))
    run_ref[...] = run_new
    count_out[...] = run_new

    row8 = lax.broadcasted_iota(jnp.int32, route_out.shape, 0)
    vals = (i1 - N_GROUPS, i2 - N_GROUPS, w0, w1, r0, r1)
    route = jnp.zeros(route_out.shape, F32)
    for j, val in enumerate(vals):
        route = jnp.where(row8 == j, val, route)
    route_out[...] = route


def _merge(x2, attn, hgrn, norm1_w, w_z, w_ba, w_bh, w_out, norm2_w, wr_hi, wr_lo, br, tm):
    T = x2.shape[0]
    row = lambda w: pl.BlockSpec((tm, w), lambda i: (i, 0))
    full = lambda a: pl.BlockSpec(a.shape, lambda i: (0,) * a.ndim)
    utri = jnp.asarray(np.triu(np.ones((tm, tm), np.float32), 1), BF16)
    ins = [x2, attn, hgrn, norm1_w[None, :], w_z, w_ba, w_bh, w_out, norm2_w[None, :], wr_hi, wr_lo, br, utri]
    in_specs = [row(D_MODEL), row(ATTN_WIDTH), row(HGRN_WIDTH)] + [full(a) for a in ins[3:]]
    return pl.pallas_call(
        _merge_kernel,
        out_shape=[jax.ShapeDtypeStruct((T, D_MODEL), F32), jax.ShapeDtypeStruct((T, HALF), jnp.uint32),
                   jax.ShapeDtypeStruct((ROUTE_ROWS, T), F32), jax.ShapeDtypeStruct((ROUTER_ROWS, tm), F32)],
        grid=(T // tm,),
        in_specs=in_specs,
        out_specs=[row(D_MODEL), row(HALF), pl.BlockSpec((ROUTE_ROWS, tm), lambda i: (0, i)),
                   pl.BlockSpec((ROUTER_ROWS, tm), lambda i: (0, 0))],
        scratch_shapes=[pltpu.VMEM((ROUTER_ROWS, tm), F32)],
        compiler_params=pltpu.CompilerParams(dimension_semantics=("arbitrary",),
                                             vmem_limit_bytes=VMEM_LIMIT),
        name="merge_router",
    )(*ins)


HALF = D_MODEL // 2


def _pack_bf16_pairs(x):
    bits = pltpu.bitcast(x.astype(BF16).astype(F32), jnp.uint32)
    return (bits[:, :HALF] >> 16) | bits[:, HALF:]


def _unpack_bf16_pairs(words):
    lo = pltpu.bitcast(words << 16, F32)
    hi = pltpu.bitcast(words & jnp.uint32(0xFFFF0000), F32)
    return lo, hi


def _issue_row_gather(idx_ref, n, src_hbm, dst_ref, sem):
    for r in range(n):
        pltpu.make_async_copy(src_hbm.at[pl.ds(idx_ref[0, 0, r], 1), :],
                              dst_ref.at[pl.ds(r, 1), :], sem).start()


def _wait_row_gather(n, src_hbm, dst_ref, sem):
    pltpu.make_async_copy(src_hbm.at[pl.ds(0, n), :], dst_ref.at[pl.ds(0, n), :], sem).wait()


def _tile_positions(pos, tk):
    nt = pos.shape[0] // tk
    return pos.reshape(nt, tk, TOP_K).transpose(0, 2, 1).reshape(nt, 1, TOP_K * tk)


def _invmap_kernel(pos_ref, tok_ref):
    tokens = pos_ref.shape[2] // TOP_K
    base = pl.program_id(0) * tokens

    def body(j, carry):
        for k in range(TOP_K):
            tok_ref[pos_ref[0, 0, j * TOP_K + k]] = base + j
        return carry
    lax.fori_loop(0, tokens, body, 0, unroll=8)


def _invmap(pos, nb):
    n = pos.size
    return pl.pallas_call(
        _invmap_kernel,
        out_shape=jax.ShapeDtypeStruct((n,), jnp.int32),
        grid=(n // nb,),
        in_specs=[pl.BlockSpec((1, 1, nb), lambda i: (i, 0, 0), memory_space=pltpu.SMEM)],
        out_specs=pl.BlockSpec(memory_space=pltpu.SMEM),
        compiler_params=pltpu.CompilerParams(dimension_semantics=("arbitrary",)),
        name="moe_row_tokens",
    )(pos.reshape(n // nb, 1, nb))


def _expert_kernel(tile_ref, exp_ref, lo_ref, hi_ref, tok_ref, tokn_ref, hn_hbm, wg_ref, wu_ref, wd_ref, y_ref,
                   xbuf, wg_s, wu_s, wd_s, sem):
    w = pl.program_id(0)
    lo = lo_ref[w]
    hi = hi_ref[w]
    tile = tile_ref[w]
    slot = tile % 2
    n_tiles = hn_hbm.shape[0] * TOP_K // EXPERT_TILE
    prev = jnp.maximum(w - 1, 0)

    entered = (w == 0) | (tile != tile_ref[prev])
    whole = (lo == 0) & (hi == EXPERT_TILE)

    @pl.when(w == 0)
    def _():
        _issue_row_gather(tok_ref, EXPERT_TILE, hn_hbm, xbuf.at[0], sem.at[0])

    for p in range(2):
        @pl.when(entered & (slot == p))
        def _():
            _issue_row_gather(tokn_ref, EXPERT_TILE, hn_hbm, xbuf.at[1 - p], sem.at[1 - p])

    @pl.when(entered)
    def _():
        _wait_row_gather(EXPERT_TILE, hn_hbm, xbuf.at[slot], sem.at[slot])

    @pl.when((w == 0) | (exp_ref[w] != exp_ref[prev]))
    def _():
        wg_s[...] = wg_ref[0].astype(BF16)
        wu_s[...] = wu_ref[0].astype(BF16)
        wd_s[...] = wd_ref[0].astype(BF16)

    def ffn(words):
        lo, hi = _unpack_bf16_pairs(words)
        lo = lo.astype(BF16)
        hi = hi.astype(BF16)
        gate = _dot(lo, wg_s[0:HALF, :]) + _dot(hi, wg_s[HALF:, :])
        up = _dot(lo, wu_s[0:HALF, :]) + _dot(hi, wu_s[HALF:, :])
        return _pack_bf16_pairs(_dot((gate * _sigmoid(gate) * up).astype(BF16), wd_s[...]))

    x_ref = xbuf.at[slot]

    @pl.when(whole)
    def _():
        y_ref[...] = ffn(x_ref[...])

    for j in range(EXPERT_TILE // MOE_BLOCK):
        r0 = j * MOE_BLOCK
        rows = slice(r0, r0 + MOE_BLOCK)

        @pl.when(jnp.logical_not(whole) & (lo < r0 + MOE_BLOCK) & (hi > r0))
        def _():
            y = ffn(x_ref[rows, :])
            rowi = lax.broadcasted_iota(jnp.int32, y.shape, 0) + r0
            mine = (rowi >= lo) & (rowi < hi)

            @pl.when(lo <= r0)
            def _():
                y_ref[rows, :] = jnp.where(mine, y, jnp.uint32(0))

            @pl.when(lo > r0)
            def _():
                y_ref[rows, :] = jnp.where(mine, y, y_ref[rows, :])

    @pl.when(w == pl.num_programs(0) - 1)
    def _():
        _wait_row_gather(EXPERT_TILE, hn_hbm, xbuf.at[1 - slot], sem.at[1 - slot])


def _experts(items, row_tok, hn, w_gate, w_up, w_down):
    tile, exp, lo, hi = items
    n_rows = row_tok.shape[0]
    n_tiles = n_rows // EXPERT_TILE
    tok3 = row_tok.reshape(n_tiles, 1, EXPERT_TILE)
    wspec = lambda shape: pl.BlockSpec((1,) + shape, lambda w, t, e, l, h: (e[w], 0, 0))
    tok_spec = lambda ahead: pl.BlockSpec(
        (1, 1, EXPERT_TILE), lambda w, t, e, l, h: (jnp.minimum(t[w] + ahead, n_tiles - 1), 0, 0),
        memory_space=pltpu.SMEM)
    return pl.pallas_call(
        _expert_kernel,
        out_shape=jax.ShapeDtypeStruct((n_rows, HALF), jnp.uint32),
        grid_spec=pltpu.PrefetchScalarGridSpec(
            num_scalar_prefetch=4,
            grid=(tile.shape[0],),
            in_specs=[tok_spec(0), tok_spec(1), pl.BlockSpec(memory_space=pl.ANY),
                      wspec((D_MODEL, EXPERT_FF)), wspec((D_MODEL, EXPERT_FF)), wspec((EXPERT_FF, D_MODEL))],
            out_specs=pl.BlockSpec((EXPERT_TILE, HALF), lambda w, t, e, l, h: (t[w], 0)),
            scratch_shapes=[pltpu.VMEM((2, EXPERT_TILE, HALF), jnp.uint32),
                            pltpu.VMEM((D_MODEL, EXPERT_FF), BF16),
                            pltpu.VMEM((D_MODEL, EXPERT_FF), BF16),
                            pltpu.VMEM((EXPERT_FF, D_MODEL), BF16),
                            pltpu.SemaphoreType.DMA((2,))]),
        compiler_params=pltpu.CompilerParams(dimension_semantics=("arbitrary",),
                                             vmem_limit_bytes=VMEM_LIMIT),
        name="moe_experts",
    )(tile, exp, lo, hi, tok3, tok3, hn, w_gate, w_up, w_down)


def _combine_kernel(pos_ref, posn_ref, x1_ref, w_ref, y_hbm, o_ref, ybuf0, ybuf1, sem):
    i = pl.program_id(0)
    tk = x1_ref.shape[0]
    rows = TOP_K * tk
    bufs = (ybuf0, ybuf1)

    def step(p):
        cur, nxt = bufs[p], bufs[1 - p]
        if p == 0:
            @pl.when(i == 0)
            def _():
                _issue_row_gather(pos_ref, rows, y_hbm, cur, sem.at[p])

        @pl.when(i + 1 < pl.num_programs(0))
        def _():
            _issue_row_gather(posn_ref, rows, y_hbm, nxt, sem.at[1 - p])

        _wait_row_gather(rows, y_hbm, cur, sem.at[p])
        w0 = w_ref[:, 0:1]
        w1 = w_ref[:, 1:2]
        lo0, hi0 = _unpack_bf16_pairs(cur[0:tk, :])
        lo1, hi1 = _unpack_bf16_pairs(cur[tk:2 * tk, :])
        o_ref[:, 0:HALF] = x1_ref[:, 0:HALF] + w0 * lo0 + w1 * lo1
        o_ref[:, HALF:] = x1_ref[:, HALF:] + w0 * hi0 + w1 * hi1

    for p in range(2):
        pl.when(i % 2 == p)(functools.partial(step, p))


def _combine(pos, x1, gate_w, y, tk):
    T = x1.shape[0]
    nt = T // tk
    pos3 = _tile_positions(pos, tk)
    pos_spec = lambda ahead: pl.BlockSpec((1, 1, TOP_K * tk), lambda i: (jnp.minimum(i + ahead, nt - 1), 0, 0),
                                          memory_space=pltpu.SMEM)
    return pl.pallas_call(
        _combine_kernel,
        out_shape=jax.ShapeDtypeStruct((T, D_MODEL), F32),
        grid=(nt,),
        in_specs=[pos_spec(0), pos_spec(1),
                  pl.BlockSpec((tk, D_MODEL), lambda i: (i, 0)),
                  pl.BlockSpec((tk, TOP_K), lambda i: (i, 0)),
                  pl.BlockSpec(memory_space=pl.ANY)],
        out_specs=pl.BlockSpec((tk, D_MODEL), lambda i: (i, 0)),
        scratch_shapes=[pltpu.VMEM((TOP_K * tk, HALF), jnp.uint32), pltpu.VMEM((TOP_K * tk, HALF), jnp.uint32),
                        pltpu.SemaphoreType.DMA((2,))],
        compiler_params=pltpu.CompilerParams(dimension_semantics=("arbitrary",),
                                             vmem_limit_bytes=VMEM_LIMIT),
        name="moe_combine",
    )(pos3, pos3, x1, gate_w, y)


def _routing_tables(route, counts, T):
    e = route[0:TOP_K].astype(jnp.int32)
    rank = route[4:4 + TOP_K].astype(jnp.int32)
    counts = counts.astype(jnp.int32)
    starts = jnp.cumsum(counts) - counts
    ids = jnp.arange(N_EXPERTS, dtype=jnp.int32)
    pos = rank + jnp.sum(jnp.where(e[:, :, None] == ids, starts, 0), axis=-1)
    n_rows = TOP_K * T
    cuts = jnp.sort(jnp.concatenate([jnp.arange(0, n_rows, EXPERT_TILE, dtype=jnp.int32), starts[1:]]))
    ends = jnp.concatenate([cuts[1:], jnp.full((1,), n_rows, jnp.int32)])
    tile = jnp.minimum(cuts // EXPERT_TILE, n_rows // EXPERT_TILE - 1)
    exp = jnp.clip(jnp.sum(starts[None, :] <= cuts[:, None], axis=1) - 1, 0, N_EXPERTS - 1).astype(jnp.int32)
    lo = cuts - tile * EXPERT_TILE
    hi = ends - tile * EXPERT_TILE
    return pos.T, (tile, exp, lo, hi)


def _pick_tile(n, pref):
    t = pref
    while n % t:
        t //= 2
    return t


def kernel(x, positions, norm1_w, w_in, q_norm_w, k_norm_w, attn_sinks, hgrn_lower_bounds, hgrn_norm_w,
           w_branch_attn, w_branch_hgrn, w_out, norm2_w, w_router_group, b_router_group, w_router_expert,
           b_router_expert, w_gate_experts, w_up_experts, w_down_experts):
    B, S, D = x.shape
    T = B * S
    x2 = x.reshape(T, D)
    tm = _pick_tile(T, 512)

    inv_freq = ROPE_THETA ** (-jnp.arange(0, ROT_DIM, 2, dtype=F32) / ROT_DIM)
    ang = positions.astype(F32).reshape(T, 1) * inv_freq[None, :]
    cs = jnp.concatenate([jnp.cos(ang), jnp.sin(ang)], axis=-1)

    w_in0 = w_in[0]
    w_in_a = w_in0[:, :_OFF_Z].astype(BF16)
    w_z = w_in0[:, _OFF_Z:].astype(BF16)

    q, k, v, hq, lf, hv, hg = _inproj(x2, norm1_w[0], w_in_a, cs, q_norm_w[0], k_norm_w[0],
                                      hgrn_lower_bounds.astype(F32), tm)
    attn = _attention(q, k, v, attn_sinks[0].astype(F32), B, S)
    hgrn = _hgrn(hq, lf, hv, hg, hgrn_norm_w[0], B, S, _pick_tile(S, 256))

    pad = ROUTER_ROWS - N_GROUPS - N_EXPERTS
    w_r = jnp.concatenate([w_router_group[0].T, w_router_expert[0].T, jnp.zeros((pad, D), F32)], axis=0)
    wr_hi = w_r.astype(BF16)
    wr_lo = (w_r - wr_hi.astype(F32)).astype(BF16)
    b_r = jnp.concatenate([b_router_group[0], b_router_expert[0], jnp.zeros((pad,), F32)]).astype(F32)
    b_r = jnp.broadcast_to(b_r[:, None], (ROUTER_ROWS, tm))

    x1, hn, route, counts = _merge(x2, attn, hgrn, norm1_w[0], w_z, w_branch_attn[0].astype(BF16),
                                   w_branch_hgrn[0].astype(BF16), w_out[0].astype(BF16), norm2_w[0],
                                   wr_hi, wr_lo, b_r, tm)

    pos, items = _routing_tables(route, counts[N_GROUPS:N_GROUPS + N_EXPERTS, 0], T)
    tk = _pick_tile(T, 128)
    row_tok = _invmap(pos, _pick_tile(pos.size, 4096))
    y = _experts(items, row_tok, hn, w_gate_experts[0], w_up_experts[0], w_down_experts[0])
    out = _combine(pos, x1, route[2:2 + TOP_K].T, y, tk)
    return out.reshape(B, S, D)
```

```python
import functools

import numpy as np
import jax
import jax.numpy as jnp
from jax import lax
from jax.experimental import pallas as pl
from jax.experimental.pallas import tpu as pltpu

F32 = jnp.float32
BF16 = jnp.bfloat16

D_MODEL = 1024
N_Q_HEADS = 8
N_KV_HEADS = 2
GROUP = N_Q_HEADS // N_KV_HEADS
HEAD_DIM = 64
ROT_DIM = HEAD_DIM // 4
ROT_HALF = ROT_DIM // 2
ROPE_THETA = 500000.0
WINDOW = 128
ATTN_WIDTH = N_Q_HEADS * HEAD_DIM
KV_WIDTH = N_KV_HEADS * HEAD_DIM

HGRN_HEADS = 4
HGRN_DIM = 128
HGRN_WIDTH = HGRN_HEADS * HGRN_DIM
CHUNK = 64

N_GROUPS = 4
EXPERTS_PER_GROUP = 8
N_EXPERTS = N_GROUPS * EXPERTS_PER_GROUP
TOP_K = 2
EXPERT_FF = 512
MOE_BLOCK = 128
EXPERT_TILE = 512
NORM_EPS = 1e-6
ROUTER_ROWS = 40
ROUTE_ROWS = 8

LANES = 128
NEG_BIG = -1e30
LOG2_E = 1.4426950408889634

_OFF_Q, _OFF_K, _OFF_V = 0, ATTN_WIDTH, ATTN_WIDTH + KV_WIDTH
_OFF_HQ = ATTN_WIDTH + 2 * KV_WIDTH
_OFF_HF = _OFF_HQ + HGRN_WIDTH
_OFF_HI = _OFF_HF + HGRN_WIDTH
_OFF_HG = _OFF_HI + HGRN_WIDTH
_OFF_Z = _OFF_HG + HGRN_WIDTH

VMEM_LIMIT = 56 * 1024 * 1024


def _split3(a):
    hi = a.astype(BF16)
    r1 = a - hi.astype(F32)
    mid = r1.astype(BF16)
    lo = (r1 - mid.astype(F32)).astype(BF16)
    return hi, mid, lo


def _dot(a, b):
    return jnp.dot(a, b, preferred_element_type=F32)


def _dot_nt(a, b):
    return lax.dot_general(a, b, (((1,), (1,)), ((), ())), preferred_element_type=F32)


def _dot_tn(a, b):
    return lax.dot_general(a, b, (((0,), (0,)), ((), ())), preferred_element_type=F32)


def _sigmoid(x):
    return 1.0 / (1.0 + jnp.exp(-x))


def _rms(x, w):
    ms = jnp.mean(x * x, axis=-1, keepdims=True)
    return x * lax.rsqrt(ms + NORM_EPS) * w


def _inproj_kernel(x_ref, n1w_ref, w_ref, cs_ref, rope_e_ref, rope_c0_ref, qw_ref, kw_ref,
                   mq_ref, mk_ref, lbp_ref, dup_ref,
                   q_out, k_out, v_out, hq_out, lf_out, hv_out, hg_out):
    xn = _rms(x_ref[...], n1w_ref[...]).astype(BF16)

    def proj(off, width):
        return _dot(xn, w_ref[:, off:off + width])

    tabs = _dot(cs_ref[...], rope_e_ref[...])
    c_tab = tabs[:, 0:LANES] + rope_c0_ref[...]
    s1_tab = tabs[:, LANES:2 * LANES]
    s2_tab = tabs[:, 2 * LANES:3 * LANES]

    def norm_rope(t, mavg_ref, w_row, scale):
        ms = _dot((t * t).astype(BF16), mavg_ref[...])
        tn = t * lax.rsqrt(ms + NORM_EPS) * w_row
        if scale != 1.0:
            tn = tn * scale
        outs = []
        for j in range(t.shape[1] // LANES):
            c = tn[:, j * LANES:(j + 1) * LANES]
            outs.append(c * c_tab
                        + pltpu.roll(c, LANES - ROT_HALF, 1) * s1_tab
                        + pltpu.roll(c, ROT_HALF, 1) * s2_tab)
        return outs[0] if len(outs) == 1 else jnp.concatenate(outs, axis=1)

    q_out[...] = norm_rope(proj(_OFF_Q, ATTN_WIDTH), mq_ref, qw_ref[...], HEAD_DIM ** -0.5).astype(BF16)
    k_rot = norm_rope(proj(_OFF_K, KV_WIDTH), mk_ref, kw_ref[...], 1.0).astype(BF16)
    k_out[...] = _dot(k_rot, dup_ref[...]).astype(BF16)
    v_out[...] = _dot(proj(_OFF_V, KV_WIDTH).astype(BF16), dup_ref[...]).astype(BF16)

    hq = proj(_OFF_HQ, HGRN_WIDTH)
    hq_out[...] = (hq * _sigmoid(hq)).astype(BF16)
    h0 = lbp_ref[0:1, :]
    h1 = lbp_ref[1:2, :]
    hm = jnp.maximum(h0, h1)
    e0 = jnp.exp(h0 - hm)
    e1 = jnp.exp(h1 - hm)
    lb = e0 / (e0 + e1)
    fg = lb + (1.0 - lb) * _sigmoid(proj(_OFF_HF, HGRN_WIDTH))
    lf_out[...] = jnp.log(fg) * LOG2_E
    hv_out[...] = proj(_OFF_HI, HGRN_WIDTH).astype(BF16)
    hg = proj(_OFF_HG, HGRN_WIDTH)
    hg_out[...] = (hg * _sigmoid(hg)).astype(BF16)


def _rope_constants():
    e = np.zeros((2 * ROT_HALF, 3 * LANES), np.float32)
    c0 = np.zeros((1, LANES), np.float32)
    for lane in range(LANES):
        d = lane % HEAD_DIM
        if d < ROT_HALF:
            e[d, lane] = 1.0
            e[ROT_HALF + d, LANES + lane] = -1.0
        elif d < ROT_DIM:
            e[d - ROT_HALF, lane] = 1.0
            e[ROT_HALF + d - ROT_HALF, 2 * LANES + lane] = 1.0
        else:
            c0[0, lane] = 1.0
    return jnp.asarray(np.concatenate([e, e, e], axis=0), BF16), jnp.asarray(c0, F32)


def _head_mean_matrix(width):
    idx = np.arange(width) // HEAD_DIM
    m = (idx[:, None] == idx[None, :]).astype(np.float32) / HEAD_DIM
    return jnp.asarray(m, BF16)


def _inproj(x2, norm1_w, w_in_a, cs, q_norm_w, k_norm_w, lbp, tm):
    T = x2.shape[0]
    rope_e, rope_c0 = _rope_constants()
    qw = jnp.tile(q_norm_w.astype(F32), N_Q_HEADS)[None, :]
    kw = jnp.tile(k_norm_w.astype(F32), N_KV_HEADS)[None, :]
    mq = _head_mean_matrix(ATTN_WIDTH)
    mk = _head_mean_matrix(KV_WIDTH)
    row = lambda w: pl.BlockSpec((tm, w), lambda i: (i, 0))
    full = lambda a: pl.BlockSpec(a.shape, lambda i: (0,) * a.ndim)
    src = HEAD_DIM * (np.arange(2 * KV_WIDTH) // LANES) + np.arange(2 * KV_WIDTH) % HEAD_DIM
    dup = jnp.asarray(np.arange(KV_WIDTH)[:, None] == src[None, :], BF16)
    ins = [x2, norm1_w[None, :], w_in_a, cs, rope_e, rope_c0, qw, kw, mq, mk, lbp, dup]
    in_specs = [row(D_MODEL), full(ins[1]), full(w_in_a), row(cs.shape[1])] + [full(a) for a in ins[4:]]
    outs = [(ATTN_WIDTH, BF16), (2 * KV_WIDTH, BF16), (2 * KV_WIDTH, BF16), (HGRN_WIDTH, BF16),
            (HGRN_WIDTH, F32), (HGRN_WIDTH, BF16), (HGRN_WIDTH, BF16)]
    return pl.pallas_call(
        _inproj_kernel,
        out_shape=[jax.ShapeDtypeStruct((T, w), dt) for w, dt in outs],
        grid=(T // tm,),
        in_specs=in_specs,
        out_specs=[row(w) for w, _ in outs],
        compiler_params=pltpu.CompilerParams(dimension_semantics=("arbitrary",),
                                             vmem_limit_bytes=VMEM_LIMIT),
        name="inproj",
    )(*ins)


ATTN_QBLOCKS = 4


def _attn_kernel(sink_ref, q_ref, kc_ref, kp_ref, vc_ref, vp_ref, half_ref, o_ref):
    has_prev = pl.program_id(1) > 0
    qi = lax.broadcasted_iota(jnp.int32, (WINDOW, 2 * WINDOW), 0)
    kj = lax.broadcasted_iota(jnp.int32, (WINDOW, 2 * WINDOW), 1)
    in_window = ((kj < WINDOW) & (kj > qi)) | ((kj >= WINDOW) & (kj - WINDOW <= qi))
    first_valid = in_window & ((kj >= WINDOW) | has_prev)
    left = lax.broadcasted_iota(jnp.int32, (WINDOW, LANES), 1) < HEAD_DIM
    half = (half_ref[0], half_ref[1])

    ks, rhs = [], []
    for h in range(N_KV_HEADS):
        cols = slice(h * LANES, (h + 1) * LANES)
        kall = jnp.concatenate([kp_ref[:, cols], kc_ref[:, cols]], axis=0)
        vall = jnp.concatenate([vp_ref[:, cols], vc_ref[:, cols]], axis=0)
        ks.append([kall * hm for hm in half])
        rhs.append([jnp.concatenate([vall * hm, hm], axis=1) for hm in half])

    units = [(j, h, pr, side) for j in range(ATTN_QBLOCKS) for h in range(N_KV_HEADS)
             for pr in range(GROUP // 2) for side in range(2)]
    scores = []
    for j, h, pr, side in units:
        pair = h * (GROUP // 2) + pr
        qp = q_ref[j * WINDOW:(j + 1) * WINDOW, pair * LANES:(pair + 1) * LANES]
        s = _dot_nt(qp, ks[h][side][j * WINDOW:(j + 2) * WINDOW])
        scores.append(jnp.where(first_valid if j == 0 else in_window, s, NEG_BIG))
    probs, sink_terms = [], []
    for (j, h, pr, side), s in zip(units, scores):
        sink = sink_ref[2 * (h * (GROUP // 2) + pr) + side]
        m = jnp.maximum(jnp.max(s, axis=-1, keepdims=True), sink)
        probs.append(jnp.exp(s - m).astype(BF16))
        sink_terms.append(jnp.exp(sink - m))
    acc = []
    for (j, h, pr, side), p in zip(units, probs):
        acc.append(_dot(p, rhs[h][side][j * WINDOW:(j + 2) * WINDOW]))
    for j in range(ATTN_QBLOCKS):
        outs = []
        for u in range(0, len(units), 2):
            if units[u][0] == j:
                both = acc[u] + acc[u + 1]
                den = both[:, LANES:2 * LANES] + jnp.where(left, sink_terms[u], sink_terms[u + 1])
                outs.append(both[:, 0:LANES] / den)
        o_ref[j * WINDOW:(j + 1) * WINDOW, :] = jnp.concatenate(outs, axis=1).astype(BF16)


def _attention(q, k, v, sinks, B, S):
    rows = ATTN_QBLOCKS * WINDOW
    nb = S // rows
    cur = lambda b, n: (b * nb + n, 0)
    prev = lambda b, n: (jnp.maximum((b * nb + n) * ATTN_QBLOCKS - 1, 0), 0)
    lane_left = np.arange(LANES) < HEAD_DIM
    half = jnp.asarray(np.broadcast_to(np.stack([lane_left, ~lane_left])[:, None, :],
                                       (2, rows + WINDOW, LANES)), BF16)
    return pl.pallas_call(
        _attn_kernel,
        out_shape=jax.ShapeDtypeStruct((B * S, ATTN_WIDTH), BF16),
        grid=(B, nb),
        in_specs=[pl.BlockSpec(memory_space=pltpu.SMEM),
                  pl.BlockSpec((rows, ATTN_WIDTH), cur),
                  pl.BlockSpec((rows, 2 * KV_WIDTH), cur),
                  pl.BlockSpec((WINDOW, 2 * KV_WIDTH), prev),
                  pl.BlockSpec((rows, 2 * KV_WIDTH), cur),
                  pl.BlockSpec((WINDOW, 2 * KV_WIDTH), prev),
                  pl.BlockSpec(half.shape, lambda b, n: (0, 0, 0))],
        out_specs=pl.BlockSpec((rows, ATTN_WIDTH), cur),
        compiler_params=pltpu.CompilerParams(dimension_semantics=("arbitrary", "arbitrary"),
                                             vmem_limit_bytes=VMEM_LIMIT),
        name="swa_attention",
    )(sinks, q, k, k, v, v, half)


_LEVEL_HALVES = (1, 2, 4, 8, 16, 32)


def _hgrn_level_masks():
    t = np.arange(CHUNK)[:, None]
    s = np.arange(CHUNK)[None, :]
    masks = [((t // (2 * h)) == (s // (2 * h))) & ((t & h) != 0) & ((s & h) == 0) for h in _LEVEL_HALVES]
    return jnp.asarray(np.stack(masks), F32)


def _level_reference(b_ref, slot, half):
    if half >= 4:
        span = max(2 * half, 8)
        pieces = [jnp.broadcast_to(b_ref[slot, s + half - 1:s + half, :], (span, HGRN_DIM))
                  for s in range(0, CHUNK, span)]
    else:
        r8 = lax.broadcasted_iota(jnp.int32, (8, HGRN_DIM), 0)
        pieces = [jnp.where(r8 < 4,
                            jnp.broadcast_to(b_ref[slot, s + 1:s + 2, :], (8, HGRN_DIM)),
                            jnp.broadcast_to(b_ref[slot, s + 5:s + 6, :], (8, HGRN_DIM)))
                  for s in range(0, CHUNK, 8)]
    return pieces[0] if len(pieces) == 1 else jnp.concatenate(pieces, axis=0)


def _hgrn_kernel(hq_ref, lf_ref, hv_ref, hg_ref, nw_ref, tri_ref, lm_ref, o_ref, st_ref, b_ref):
    @pl.when(pl.program_id(1) == 0)
    def _():
        st_ref[...] = jnp.zeros_like(st_ref)

    tri2 = tri_ref[...]
    odd = (lax.broadcasted_iota(jnp.int32, (CHUNK, HGRN_DIM), 0) & 1) != 0
    masks = [lm_ref[li] != 0.0 for li in range(len(_LEVEL_HALVES))]
    units = [(c, h) for c in range(hq_ref.shape[0] // CHUNK) for h in range(HGRN_HEADS)]
    sl = lambda c, h: (slice(c * CHUNK, (c + 1) * CHUNK), slice(h * HGRN_DIM, (h + 1) * HGRN_DIM))

    bs = []
    for u, (c, h) in enumerate(units):
        lf2 = lf_ref[sl(c, h)]
        hi = lf2.astype(BF16)
        lo = (lf2 - hi.astype(F32)).astype(BF16)
        b = _dot(tri2, jnp.concatenate([hi, lo], axis=0))
        b_ref[u] = b
        bs.append(b)

    outs, qs, ks, fs = [], [], [], []
    for u, (c, h) in enumerate(units):
        b = bs[u]
        q = hq_ref[sl(c, h)].astype(F32)
        v_bf = hv_ref[sl(c, h)]
        f = jnp.exp2(lf_ref[sl(c, h)])
        k = 1.0 - f
        b_last = b[CHUNK - 1:CHUNK, :]
        st = st_ref[h]
        o = _dot_nt((q * jnp.exp2(b)).astype(BF16), st.astype(BF16))
        k_out = (k * jnp.exp2(b_last - b)).astype(BF16)
        st_ref[h] = st * jnp.exp2(b_last) + _dot_tn(v_bf, k_out)
        outs.append(o + jnp.sum(q * k, axis=-1, keepdims=True) * v_bf.astype(F32))
        qs.append(q)
        ks.append(k)
        fs.append(f)

    accs = [jnp.zeros((CHUNK, CHUNK), F32) for _ in units]
    for li, half in enumerate(_LEVEL_HALVES):
        for u in range(len(units)):
            if half == 1:
                e = jnp.where(odd, fs[u], 1.0)
            else:
                e = jnp.exp2(-jnp.abs(bs[u] - _level_reference(b_ref, u, half)))
            accs[u] = jnp.where(masks[li], _dot_nt((qs[u] * e).astype(BF16), (ks[u] * e).astype(BF16)), accs[u])

    for u, (c, h) in enumerate(units):
        o = outs[u] + _dot(accs[u].astype(BF16), hv_ref[sl(c, h)])
        y = _rms(o, nw_ref[...]) * hg_ref[sl(c, h)].astype(F32)
        o_ref[sl(c, h)] = y.astype(BF16)


def _hgrn(hq, lf, hv, hg, hgrn_norm_w, B, S, tb):
    nt = S // tb
    blk = pl.BlockSpec((tb, HGRN_WIDTH), lambda b, n: (b * nt + n, 0))
    tri = np.tril(np.ones((CHUNK, CHUNK), np.float32))
    tri = jnp.asarray(np.concatenate([tri, tri], axis=1), BF16)
    lm = _hgrn_level_masks()
    return pl.pallas_call(
        _hgrn_kernel,
        out_shape=jax.ShapeDtypeStruct((B * S, HGRN_WIDTH), BF16),
        grid=(B, nt),
        in_specs=[blk, blk, blk, blk,
                  pl.BlockSpec((1, HGRN_DIM), lambda b, n: (0, 0)),
                  pl.BlockSpec(tri.shape, lambda b, n: (0, 0)),
                  pl.BlockSpec(lm.shape, lambda b, n: (0, 0, 0))],
        out_specs=blk,
        scratch_shapes=[pltpu.VMEM((HGRN_HEADS, HGRN_DIM, HGRN_DIM), F32),
                        pltpu.VMEM((tb // CHUNK * HGRN_HEADS, CHUNK, HGRN_DIM), F32)],
        compiler_params=pltpu.CompilerParams(dimension_semantics=("arbitrary", "arbitrary"),
                                             vmem_limit_bytes=VMEM_LIMIT),
        name="hgrn2",
    )(hq, lf, hv, hg, hgrn_norm_w[None, :].astype(F32), tri, lm)


def _merge_kernel(x_ref, attn_ref, hgrn_ref, n1w_ref, wz_ref, wba_ref, wbh_ref, wout_ref, n2w_ref,
                  wr_ref, br_ref, utri_ref, x1_out, hn_out, route_out, count_out, run_ref):
    @pl.when(pl.program_id(0) == 0)
    def _():
        run_ref[...] = jnp.zeros_like(run_ref)

    x = x_ref[...]
    xn = _rms(x, n1w_ref[...]).astype(BF16)
    za = _sigmoid(_dot(xn, wz_ref[:, 0:D_MODEL]))
    zb = _sigmoid(_dot(xn, wz_ref[:, D_MODEL:2 * D_MODEL]))
    mixed = za * _dot(attn_ref[...], wba_ref[...]) + zb * _dot(hgrn_ref[...], wbh_ref[...])
    x1 = x + _dot(mixed.astype(BF16), wout_ref[...])
    x1_out[...] = x1
    hn = _rms(x1, n2w_ref[...])
    hn_out[...] = _pack_bf16_pairs(hn)

    logits = _dot_nt(wr_ref[...], hn.astype(BF16)) + br_ref[...]
    r = lax.broadcasted_iota(jnp.int32, logits.shape, 0).astype(F32)
    far = float(ROUTER_ROWS)
    cmax = lambda a: jnp.max(a, axis=0, keepdims=True)
    cmin = lambda a: jnp.min(a, axis=0, keepdims=True)
    csum = lambda a: jnp.sum(a, axis=0, keepdims=True)

    lg = jnp.where(r < N_GROUPS, logits, NEG_BIG)
    mg = cmax(lg)
    gsel = cmin(jnp.where(lg == mg, r, far))
    pgsel = 1.0 / csum(jnp.exp(lg - mg))

    lo = N_GROUPS + EXPERTS_PER_GROUP * gsel
    le = jnp.where((r >= lo) & (r < lo + EXPERTS_PER_GROUP), logits, NEG_BIG)
    m1 = cmax(le)
    i1 = cmin(jnp.where(le == m1, r, far))
    se = csum(jnp.exp(le - m1))
    le2 = jnp.where(r == i1, NEG_BIG, le)
    m2 = cmax(le2)
    i2 = cmin(jnp.where(le2 == m2, r, far))
    top0 = 1.0 / se
    top1 = jnp.exp(m2 - m1) / se
    tsum = top0 + top1
    w0 = pgsel * top0 / tsum
    w1 = pgsel * top1 / tsum

    sel1 = r == i1
    sel2 = r == i2
    onehot = (sel1 | sel2).astype(BF16)
    before = _dot(onehot, utri_ref[...]) + run_ref[...]
    r0 = csum(jnp.where(sel1, before, 0.0))
    r1 = csum(jnp.where(sel2, before, 0.0))
    run_new = run_ref[...] + _dot(onehot, jnp.ones(utri_ref.shape, BF16))
    run_ref[...] = run_new
    count_out[...] = run_new

    row8 = lax.broadcasted_iota(jnp.int32, route_out.shape, 0)
    vals = (i1 - N_GROUPS, i2 - N_GROUPS, w0, w1, r0, r1)
    route = jnp.zeros(route_out.shape, F32)
    for j, val in enumerate(vals):
        route = jnp.where(row8 == j, val, route)
    route_out[...] = route


def _merge(x2, attn, hgrn, norm1_w, w_z, w_ba, w_bh, w_out, norm2_w, w_r, br, tm):
    T = x2.shape[0]
    row = lambda w: pl.BlockSpec((tm, w), lambda i: (i, 0))
    full = lambda a: pl.BlockSpec(a.shape, lambda i: (0,) * a.ndim)
    utri = jnp.asarray(np.triu(np.ones((tm, tm), np.float32), 1), BF16)
    ins = [x2, attn, hgrn, norm1_w[None, :], w_z, w_ba, w_bh, w_out, norm2_w[None, :], w_r, br, utri]
    in_specs = [row(D_MODEL), row(ATTN_WIDTH), row(HGRN_WIDTH)] + [full(a) for a in ins[3:]]
    return pl.pallas_call(
        _merge_kernel,
        out_shape=[jax.ShapeDtypeStruct((T, D_MODEL), F32), jax.ShapeDtypeStruct((T, HALF), jnp.uint32),
                   jax.ShapeDtypeStruct((ROUTE_ROWS, T), F32), jax.ShapeDtypeStruct((ROUTER_ROWS, tm), F32)],
        grid=(T // tm,),
        in_specs=in_specs,
        out_specs=[row(D_MODEL), row(HALF), pl.BlockSpec((ROUTE_ROWS, tm), lambda i: (0, i)),
                   pl.BlockSpec((ROUTER_ROWS, tm), lambda i: (0, 0))],
        scratch_shapes=[pltpu.VMEM((ROUTER_ROWS, tm), F32)],
        compiler_params=pltpu.CompilerParams(dimension_semantics=("arbitrary",),
                                             vmem_limit_bytes=VMEM_LIMIT),
        name="merge_router",
    )(*ins)


HALF = D_MODEL // 2


def _pack_bf16_pairs(x):
    bits = pltpu.bitcast(x.astype(BF16).astype(F32), jnp.uint32)
    return (bits[:, :HALF] >> 16) | bits[:, HALF:]


def _unpack_bf16_pairs(words):
    lo = pltpu.bitcast(words << 16, F32)
    hi = pltpu.bitcast(words & jnp.uint32(0xFFFF0000), F32)
    return lo, hi


def _issue_row_gather(idx_ref, n, src_hbm, dst_ref, sem):
    for r in range(n):
        pltpu.make_async_copy(src_hbm.at[pl.ds(idx_ref[0, 0, r], 1), :],
                              dst_ref.at[pl.ds(r, 1), :], sem).start()


def _wait_row_gather(n, src_hbm, dst_ref, sem):
    pltpu.make_async_copy(src_hbm.at[pl.ds(0, n), :], dst_ref.at[pl.ds(0, n), :], sem).wait()


def _tile_positions(pos, tk):
    nt = pos.shape[0] // tk
    return pos.reshape(nt, tk, TOP_K).transpose(0, 2, 1).reshape(nt, 1, TOP_K * tk)


def _invmap_kernel(pos_ref, tok_ref):
    tokens = pos_ref.shape[2] // TOP_K
    base = pl.program_id(0) * tokens

    def body(j, carry):
        for k in range(TOP_K):
            tok_ref[pos_ref[0, 0, j * TOP_K + k]] = base + j
        return carry
    lax.fori_loop(0, tokens, body, 0, unroll=8)


def _invmap(pos, nb):
    n = pos.size
    return pl.pallas_call(
        _invmap_kernel,
        out_shape=jax.ShapeDtypeStruct((n,), jnp.int32),
        grid=(n // nb,),
        in_specs=[pl.BlockSpec((1, 1, nb), lambda i: (i, 0, 0), memory_space=pltpu.SMEM)],
        out_specs=pl.BlockSpec(memory_space=pltpu.SMEM),
        compiler_params=pltpu.CompilerParams(dimension_semantics=("arbitrary",)),
        name="moe_row_tokens",
    )(pos.reshape(n // nb, 1, nb))


def _expert_kernel(tile_ref, exp_ref, lo_ref, hi_ref, tok_ref, tokn_ref, hn_hbm, wg_ref, wu_ref, wd_ref, y_ref,
                   xbuf, wg_s, wu_s, wd_s, sem):
    w = pl.program_id(0)
    lo = lo_ref[w]
    hi = hi_ref[w]
    tile = tile_ref[w]
    slot = tile % 2
    n_tiles = hn_hbm.shape[0] * TOP_K // EXPERT_TILE
    prev = jnp.maximum(w - 1, 0)

    entered = (w == 0) | (tile != tile_ref[prev])
    whole = (lo == 0) & (hi == EXPERT_TILE)

    @pl.when(w == 0)
    def _():
        _issue_row_gather(tok_ref, EXPERT_TILE, hn_hbm, xbuf.at[0], sem.at[0])

    @pl.when((w == 0) | (exp_ref[w] != exp_ref[prev]))
    def _():
        wg_s[...] = wg_ref[0].astype(BF16)
        wu_s[...] = wu_ref[0].astype(BF16)
        wd_s[...] = wd_ref[0].astype(BF16)

    def ffn(words):
        lo, hi = _unpack_bf16_pairs(words)
        lo = lo.astype(BF16)
        hi = hi.astype(BF16)
        gate = _dot(lo, wg_s[0:HALF, :]) + _dot(hi, wg_s[HALF:, :])
        up = _dot(lo, wu_s[0:HALF, :]) + _dot(hi, wu_s[HALF:, :])
        return _pack_bf16_pairs(_dot((gate * _sigmoid(gate) * up).astype(BF16), wd_s[...]))

    for p in range(2):
        @pl.when(entered & (slot == p))
        def _():
            _issue_row_gather(tokn_ref, EXPERT_TILE, hn_hbm, xbuf.at[1 - p], sem.at[1 - p])

    @pl.when(entered)
    def _():
        _wait_row_gather(EXPERT_TILE, hn_hbm, xbuf.at[slot], sem.at[slot])

    x_ref = xbuf.at[slot]

    @pl.when(whole)
    def _():
        y_ref[...] = ffn(x_ref[...])

    for j in range(EXPERT_TILE // MOE_BLOCK):
        r0 = j * MOE_BLOCK
        rows = slice(r0, r0 + MOE_BLOCK)

        @pl.when(jnp.logical_not(whole) & (lo < r0 + MOE_BLOCK) & (hi > r0))
        def _():
            y = ffn(x_ref[rows, :])
            rowi = lax.broadcasted_iota(jnp.int32, y.shape, 0) + r0
            mine = (rowi >= lo) & (rowi < hi)

            @pl.when(lo <= r0)
            def _():
                y_ref[rows, :] = jnp.where(mine, y, jnp.uint32(0))

            @pl.when(lo > r0)
            def _():
                y_ref[rows, :] = jnp.where(mine, y, y_ref[rows, :])

    @pl.when(w == pl.num_programs(0) - 1)
    def _():
        _wait_row_gather(EXPERT_TILE, hn_hbm, xbuf.at[1 - slot], sem.at[1 - slot])


def _experts(items, row_tok, hn, w_gate, w_up, w_down):
    tile, exp, lo, hi = items
    n_rows = row_tok.shape[0]
    n_tiles = n_rows // EXPERT_TILE
    tok3 = row_tok.reshape(n_tiles, 1, EXPERT_TILE)
    wspec = lambda shape: pl.BlockSpec((1,) + shape, lambda w, t, e, l, h: (e[w], 0, 0))
    tok_spec = lambda ahead: pl.BlockSpec(
        (1, 1, EXPERT_TILE), lambda w, t, e, l, h: (jnp.minimum(t[w] + ahead, n_tiles - 1), 0, 0),
        memory_space=pltpu.SMEM)
    return pl.pallas_call(
        _expert_kernel,
        out_shape=jax.ShapeDtypeStruct((n_rows, HALF), jnp.uint32),
        grid_spec=pltpu.PrefetchScalarGridSpec(
            num_scalar_prefetch=4,
            grid=(tile.shape[0],),
            in_specs=[tok_spec(0), tok_spec(1), pl.BlockSpec(memory_space=pl.ANY),
                      wspec((D_MODEL, EXPERT_FF)), wspec((D_MODEL, EXPERT_FF)), wspec((EXPERT_FF, D_MODEL))],
            out_specs=pl.BlockSpec((EXPERT_TILE, HALF), lambda w, t, e, l, h: (t[w], 0)),
            scratch_shapes=[pltpu.VMEM((2, EXPERT_TILE, HALF), jnp.uint32),
                            pltpu.VMEM((D_MODEL, EXPERT_FF), BF16),
                            pltpu.VMEM((D_MODEL, EXPERT_FF), BF16),
                            pltpu.VMEM((EXPERT_FF, D_MODEL), BF16),
                            pltpu.SemaphoreType.DMA((2,))]),
        compiler_params=pltpu.CompilerParams(dimension_semantics=("arbitrary",),
                                             vmem_limit_bytes=VMEM_LIMIT),
        name="moe_experts",
    )(tile, exp, lo, hi, tok3, tok3, hn, w_gate, w_up, w_down)


def _combine_kernel(pos_ref, posn_ref, x1_ref, w_ref, y_hbm, o_ref, ybuf0, ybuf1, sem):
    i = pl.program_id(0)
    tk = x1_ref.shape[0]
    rows = TOP_K * tk
    bufs = (ybuf0, ybuf1)

    def step(p):
        cur, nxt = bufs[p], bufs[1 - p]
        if p == 0:
            @pl.when(i == 0)
            def _():
                _issue_row_gather(pos_ref, rows, y_hbm, cur, sem.at[p])

        @pl.when(i + 1 < pl.num_programs(0))
        def _():
            _issue_row_gather(posn_ref, rows, y_hbm, nxt, sem.at[1 - p])

        _wait_row_gather(rows, y_hbm, cur, sem.at[p])
        w0 = w_ref[:, 0:1]
        w1 = w_ref[:, 1:2]
        lo0, hi0 = _unpack_bf16_pairs(cur[0:tk, :])
        lo1, hi1 = _unpack_bf16_pairs(cur[tk:2 * tk, :])
        o_ref[:, 0:HALF] = x1_ref[:, 0:HALF] + w0 * lo0 + w1 * lo1
        o_ref[:, HALF:] = x1_ref[:, HALF:] + w0 * hi0 + w1 * hi1

    for p in range(2):
        pl.when(i % 2 == p)(functools.partial(step, p))


def _combine(pos, x1, gate_w, y, tk):
    T = x1.shape[0]
    nt = T // tk
    pos3 = _tile_positions(pos, tk)
    pos_spec = lambda ahead: pl.BlockSpec((1, 1, TOP_K * tk), lambda i: (jnp.minimum(i + ahead, nt - 1), 0, 0),
                                          memory_space=pltpu.SMEM)
    return pl.pallas_call(
        _combine_kernel,
        out_shape=jax.ShapeDtypeStruct((T, D_MODEL), F32),
        grid=(nt,),
        in_specs=[pos_spec(0), pos_spec(1),
                  pl.BlockSpec((tk, D_MODEL), lambda i: (i, 0)),
                  pl.BlockSpec((tk, TOP_K), lambda i: (i, 0)),
                  pl.BlockSpec(memory_space=pl.ANY)],
        out_specs=pl.BlockSpec((tk, D_MODEL), lambda i: (i, 0)),
        scratch_shapes=[pltpu.VMEM((TOP_K * tk, HALF), jnp.uint32), pltpu.VMEM((TOP_K * tk, HALF), jnp.uint32),
                        pltpu.SemaphoreType.DMA((2,))],
        compiler_params=pltpu.CompilerParams(dimension_semantics=("arbitrary",),
                                             vmem_limit_bytes=VMEM_LIMIT),
        name="moe_combine",
    )(pos3, pos3, x1, gate_w, y)


def _routing_tables(route, counts, T):
    e = route[0:TOP_K].astype(jnp.int32)
    rank = route[4:4 + TOP_K].astype(jnp.int32)
    counts = counts.astype(jnp.int32)
    starts = jnp.cumsum(counts) - counts
    ids = jnp.arange(N_EXPERTS, dtype=jnp.int32)
    pos = rank + jnp.sum(jnp.where(e[:, :, None] == ids, starts, 0), axis=-1)
    n_rows = TOP_K * T
    cuts = jnp.sort(jnp.concatenate([jnp.arange(0, n_rows, EXPERT_TILE, dtype=jnp.int32), starts[1:]]))
    ends = jnp.concatenate([cuts[1:], jnp.full((1,), n_rows, jnp.int32)])
    tile = jnp.minimum(cuts // EXPERT_TILE, n_rows // EXPERT_TILE - 1)
    exp = jnp.clip(jnp.sum(starts[None, :] <= cuts[:, None], axis=1) - 1, 0, N_EXPERTS - 1).astype(jnp.int32)
    lo = cuts - tile * EXPERT_TILE
    hi = ends - tile * EXPERT_TILE
    return pos.T, (tile, exp, lo, hi)


def _pick_tile(n, pref):
    t = pref
    while n % t:
        t //= 2
    return t


def kernel(x, positions, norm1_w, w_in, q_norm_w, k_norm_w, attn_sinks, hgrn_lower_bounds, hgrn_norm_w,
           w_branch_attn, w_branch_hgrn, w_out, norm2_w, w_router_group, b_router_group, w_router_expert,
           b_router_expert, w_gate_experts, w_up_experts, w_down_experts):
    B, S, D = x.shape
    T = B * S
    x2 = x.reshape(T, D)
    tm = _pick_tile(T, 512)

    inv_freq = ROPE_THETA ** (-jnp.arange(0, ROT_DIM, 2, dtype=F32) / ROT_DIM)
    ang = positions.astype(F32).reshape(T, 1) * inv_freq[None, :]
    cs = jnp.concatenate(_split3(jnp.concatenate([jnp.cos(ang), jnp.sin(ang)], axis=-1)), axis=-1)

    w_in0 = w_in[0]
    w_in_a = w_in0[:, :_OFF_Z].astype(BF16)
    w_z = w_in0[:, _OFF_Z:].astype(BF16)

    q, k, v, hq, lf, hv, hg = _inproj(x2, norm1_w[0], w_in_a, cs, q_norm_w[0], k_norm_w[0],
                                      hgrn_lower_bounds.astype(F32), tm)
    attn = _attention(q, k, v, attn_sinks[0].astype(F32), B, S)
    hgrn = _hgrn(hq, lf, hv, hg, hgrn_norm_w[0], B, S, _pick_tile(S, 256))

    pad = ROUTER_ROWS - N_GROUPS - N_EXPERTS
    w_r = jnp.concatenate([w_router_group[0].T, w_router_expert[0].T, jnp.zeros((pad, D), F32)], axis=0)
    b_r =jnp.concatenate([b_router_group[0], b_router_expert[0], jnp.zeros((pad,), F32)]).astype(F32)
    b_r = jnp.broadcast_to(b_r[:, None], (ROUTER_ROWS, tm))

    x1, hn, route, counts = _merge(x2, attn, hgrn, norm1_w[0], w_z, w_branch_attn[0].astype(BF16),
                                   w_branch_hgrn[0].astype(BF16), w_out[0].astype(BF16), norm2_w[0],
                                   w_r.astype(BF16), b_r, tm)

    pos, items = _routing_tables(route, counts[N_GROUPS:N_GROUPS + N_EXPERTS, 0], T)
    tk = _pick_tile(T, 128)
    row_tok = _invmap(pos, _pick_tile(pos.size, 4096))
    y = _experts(items, row_tok, hn, w_gate_experts[0], w_up_experts[0], w_down_experts[0])
    out = _combine(pos, x1, route[2:2 + TOP_K].T, y, tk)
    return out.reshape(B, S, D)
```

```python
import functools

import numpy as np
import jax
import jax.numpy as jnp
from jax import lax
from jax.experimental import pallas as pl
from jax.experimental.pallas import tpu as pltpu

F32 = jnp.float32
BF16 = jnp.bfloat16

D_MODEL = 1024
N_Q_HEADS = 8
N_KV_HEADS = 2
GROUP = N_Q_HEADS // N_KV_HEADS
HEAD_DIM = 64
ROT_DIM = HEAD_DIM // 4
ROT_HALF = ROT_DIM // 2
ROPE_THETA = 500000.0
WINDOW = 128
ATTN_WIDTH = N_Q_HEADS * HEAD_DIM
KV_WIDTH = N_KV_HEADS * HEAD_DIM

HGRN_HEADS = 4
HGRN_DIM = 128
HGRN_WIDTH = HGRN_HEADS * HGRN_DIM
CHUNK = 64

N_GROUPS = 4
EXPERTS_PER_GROUP = 8
N_EXPERTS = N_GROUPS * EXPERTS_PER_GROUP
TOP_K = 2
EXPERT_FF = 512
MOE_BLOCK = 128
EXPERT_TILE = 512
NORM_EPS = 1e-6
ROUTER_ROWS = 40
ROUTE_ROWS = 8

LANES = 128
NEG_BIG = -1e30
LOG2_E = 1.4426950408889634

_OFF_Q, _OFF_K, _OFF_V = 0, ATTN_WIDTH, ATTN_WIDTH + KV_WIDTH
_OFF_HQ = ATTN_WIDTH + 2 * KV_WIDTH
_OFF_HF = _OFF_HQ + HGRN_WIDTH
_OFF_HI = _OFF_HF + HGRN_WIDTH
_OFF_HG = _OFF_HI + HGRN_WIDTH
_OFF_Z = _OFF_HG + HGRN_WIDTH
_A_Q, _A_K, _A_V = 0, ATTN_WIDTH, ATTN_WIDTH + 2 * KV_WIDTH
_A_HQ = ATTN_WIDTH + 4 * KV_WIDTH
_A_HF = _A_HQ + HGRN_WIDTH
_A_HI = _A_HF + HGRN_WIDTH
_A_HG = _A_HI + HGRN_WIDTH

VMEM_LIMIT = 56 * 1024 * 1024


def _split3(a):
    hi = a.astype(BF16)
    r1 = a - hi.astype(F32)
    mid = r1.astype(BF16)
    lo = (r1 - mid.astype(F32)).astype(BF16)
    return hi, mid, lo


def _dot(a, b):
    return jnp.dot(a, b, preferred_element_type=F32)


def _dot_nt(a, b):
    return lax.dot_general(a, b, (((1,), (1,)), ((), ())), preferred_element_type=F32)


def _dot_tn(a, b):
    return lax.dot_general(a, b, (((0,), (0,)), ((), ())), preferred_element_type=F32)


def _sigmoid(x):
    return 1.0 / (1.0 + jnp.exp(-x))


def _rms(x, w):
    ms = jnp.mean(x * x, axis=-1, keepdims=True)
    return x * lax.rsqrt(ms + NORM_EPS) * w


def _inproj_kernel(x_ref, n1w_ref, w_ref, cs_ref, rope_e_ref, rope_c0_ref, qw_ref, kw_ref,
                   mq_ref, mk_ref, lbp_ref,
                   q_out, k_out, v_out, hq_out, lf_out, hv_out, hg_out):
    xn = _rms(x_ref[...], n1w_ref[...]).astype(BF16)

    def proj(off, width):
        return _dot(xn, w_ref[:, off:off + width])

    tabs = _dot(cs_ref[...], rope_e_ref[...])
    c_tab = tabs[:, 0:LANES] + rope_c0_ref[...]
    s1_tab = tabs[:, LANES:2 * LANES]
    s2_tab = tabs[:, 2 * LANES:3 * LANES]

    def norm_rope(t, mavg_ref, w_row, scale):
        ms = _dot((t * t).astype(BF16), mavg_ref[...])
        tn = t * lax.rsqrt(ms + NORM_EPS) * w_row
        if scale != 1.0:
            tn = tn * scale
        outs = []
        for j in range(t.shape[1] // LANES):
            c = tn[:, j * LANES:(j + 1) * LANES]
            outs.append(c * c_tab
                        + pltpu.roll(c, LANES - ROT_HALF, 1) * s1_tab
                        + pltpu.roll(c, ROT_HALF, 1) * s2_tab)
        return outs[0] if len(outs) == 1 else jnp.concatenate(outs, axis=1)

    q_out[...] = norm_rope(proj(_A_Q, ATTN_WIDTH), mq_ref, qw_ref[...], HEAD_DIM ** -0.5).astype(BF16)
    k_out[...] = norm_rope(proj(_A_K, 2 * KV_WIDTH), mk_ref, kw_ref[...], 1.0).astype(BF16)
    v_out[...] = proj(_A_V, 2 * KV_WIDTH).astype(BF16)

    hq = proj(_A_HQ, HGRN_WIDTH)
    hq_out[...] = (hq * _sigmoid(hq)).astype(BF16)
    h0 = lbp_ref[0:1, :]
    h1 = lbp_ref[1:2, :]
    hm = jnp.maximum(h0, h1)
    e0 = jnp.exp(h0 - hm)
    e1 = jnp.exp(h1 - hm)
    lb = e0 / (e0 + e1)
    fg = lb + (1.0 - lb) * _sigmoid(proj(_A_HF, HGRN_WIDTH))
    lf_out[...] = jnp.log(fg) * LOG2_E
    hv_out[...] = proj(_A_HI, HGRN_WIDTH).astype(BF16)
    hg = proj(_A_HG, HGRN_WIDTH)
    hg_out[...] = (hg * _sigmoid(hg)).astype(BF16)


def _rope_constants():
    e = np.zeros((2 * ROT_HALF, 3 * LANES), np.float32)
    c0 = np.zeros((1, LANES), np.float32)
    for lane in range(LANES):
        d = lane % HEAD_DIM
        if d < ROT_HALF:
            e[d, lane] = 1.0
            e[ROT_HALF + d, LANES + lane] = -1.0
        elif d < ROT_DIM:
            e[d - ROT_HALF, lane] = 1.0
            e[ROT_HALF + d - ROT_HALF, 2 * LANES + lane] = 1.0
        else:
            c0[0, lane] = 1.0
    return jnp.asarray(np.concatenate([e, e, e], axis=0), BF16), jnp.asarray(c0, F32)


def _head_mean_matrix(width):
    idx = np.arange(width) // HEAD_DIM
    m = (idx[:, None] == idx[None, :]).astype(np.float32) / HEAD_DIM
    return jnp.asarray(m, BF16)


def _inproj(x2, norm1_w, w_in_a, cs, q_norm_w, k_norm_w, lbp, tm):
    T = x2.shape[0]
    rope_e, rope_c0 = _rope_constants()
    qw = jnp.tile(q_norm_w.astype(F32), N_Q_HEADS)[None, :]
    kw = jnp.tile(k_norm_w.astype(F32), 2 * N_KV_HEADS)[None, :]
    mq = _head_mean_matrix(ATTN_WIDTH)
    mk = _head_mean_matrix(2 * KV_WIDTH)
    row = lambda w: pl.BlockSpec((tm, w), lambda i: (i, 0))
    full = lambda a: pl.BlockSpec(a.shape, lambda i: (0,) * a.ndim)
    ins = [x2, norm1_w[None, :], w_in_a, cs, rope_e, rope_c0, qw, kw, mq, mk, lbp]
    in_specs = [row(D_MODEL), full(ins[1]), full(w_in_a), row(cs.shape[1])] + [full(a) for a in ins[4:]]
    outs = [(ATTN_WIDTH, BF16), (2 * KV_WIDTH, BF16), (2 * KV_WIDTH, BF16), (HGRN_WIDTH, BF16),
            (HGRN_WIDTH, F32), (HGRN_WIDTH, BF16), (HGRN_WIDTH, BF16)]
    return pl.pallas_call(
        _inproj_kernel,
        out_shape=[jax.ShapeDtypeStruct((T, w), dt) for w, dt in outs],
        grid=(T // tm,),
        in_specs=in_specs,
        out_specs=[row(w) for w, _ in outs],
        compiler_params=pltpu.CompilerParams(dimension_semantics=("arbitrary",),
                                             vmem_limit_bytes=VMEM_LIMIT),
        name="inproj",
    )(*ins)


ATTN_QBLOCKS = 8


def _attn_kernel(sink_ref, q_ref, kc_ref, kp_ref, vc_ref, vp_ref, half_ref, o_ref):
    n_qblocks = q_ref.shape[0] // WINDOW
    has_prev = pl.program_id(1) > 0
    qi = lax.broadcasted_iota(jnp.int32, (WINDOW, 2 * WINDOW), 0)
    kj = lax.broadcasted_iota(jnp.int32, (WINDOW, 2 * WINDOW), 1)
    in_window = ((kj < WINDOW) & (kj > qi)) | ((kj >= WINDOW) & (kj - WINDOW <= qi))
    first_valid = in_window & ((kj >= WINDOW) | has_prev)
    left = lax.broadcasted_iota(jnp.int32, (WINDOW, LANES), 1) < HEAD_DIM
    half = (half_ref[0], half_ref[1])

    ks, rhs = [], []
    for h in range(N_KV_HEADS):
        cols = slice(h * LANES, (h + 1) * LANES)
        kall = jnp.concatenate([kp_ref[:, cols], kc_ref[:, cols]], axis=0)
        vall = jnp.concatenate([vp_ref[:, cols], vc_ref[:, cols]], axis=0)
        ks.append([kall * hm for hm in half])
        rhs.append([jnp.concatenate([vall * hm, hm], axis=1) for hm in half])

    units = [(j, h, pr, side) for j in range(n_qblocks) for h in range(N_KV_HEADS)
             for pr in range(GROUP // 2) for side in range(2)]
    scores = []
    for j, h, pr, side in units:
        pair = h * (GROUP // 2) + pr
        qp = q_ref[j * WINDOW:(j + 1) * WINDOW, pair * LANES:(pair + 1) * LANES]
        s = _dot_nt(qp, ks[h][side][j * WINDOW:(j + 2) * WINDOW])
        scores.append(jnp.where(first_valid if j == 0 else in_window, s, NEG_BIG))
    probs, sink_terms = [], []
    for (j, h, pr, side), s in zip(units, scores):
        sink = sink_ref[2 * (h * (GROUP // 2) + pr) + side]
        m = jnp.maximum(jnp.max(s, axis=-1, keepdims=True), sink)
        probs.append(jnp.exp(s - m).astype(BF16))
        sink_terms.append(jnp.exp(sink - m))
    acc = []
    for (j, h, pr, side), p in zip(units, probs):
        acc.append(_dot(p, rhs[h][side][j * WINDOW:(j + 2) * WINDOW]))
    for j in range(n_qblocks):
        outs = []
        for u in range(0, len(units), 2):
            if units[u][0] == j:
                both = acc[u] + acc[u + 1]
                den = both[:, LANES:2 * LANES] + jnp.where(left, sink_terms[u], sink_terms[u + 1])
                outs.append(both[:, 0:LANES] / den)
        o_ref[j * WINDOW:(j + 1) * WINDOW, :] = jnp.concatenate(outs, axis=1).astype(BF16)


def _attention(q, k, v, sinks, B, S):
    qblocks = _pick_tile(S // WINDOW, ATTN_QBLOCKS)
    rows = qblocks * WINDOW
    nb = S // rows
    cur = lambda b, n: (b * nb + n, 0)
    prev = lambda b, n: (jnp.maximum((b * nb + n) * qblocks - 1, 0), 0)
    lane_left = np.arange(LANES) < HEAD_DIM
    half = jnp.asarray(np.broadcast_to(np.stack([lane_left, ~lane_left])[:, None, :],
                                       (2, rows + WINDOW, LANES)), BF16)
    return pl.pallas_call(
        _attn_kernel,
        out_shape=jax.ShapeDtypeStruct((B * S, ATTN_WIDTH), BF16),
        grid=(B, nb),
        in_specs=[pl.BlockSpec(memory_space=pltpu.SMEM),
                  pl.BlockSpec((rows, ATTN_WIDTH), cur),
                  pl.BlockSpec((rows, 2 * KV_WIDTH), cur),
                  pl.BlockSpec((WINDOW, 2 * KV_WIDTH), prev),
                  pl.BlockSpec((rows, 2 * KV_WIDTH), cur),
                  pl.BlockSpec((WINDOW, 2 * KV_WIDTH), prev),
                  pl.BlockSpec(half.shape, lambda b, n: (0, 0, 0))],
        out_specs=pl.BlockSpec((rows, ATTN_WIDTH), cur),
        compiler_params=pltpu.CompilerParams(dimension_semantics=("arbitrary", "arbitrary"),
                                             vmem_limit_bytes=VMEM_LIMIT),
        name="swa_attention",
    )(sinks, q, k, k, v, v, half)


_LEVEL_HALVES = (1, 2, 4, 8, 16, 32)


def _hgrn_level_masks():
    t = np.arange(CHUNK)[:, None]
    s = np.arange(CHUNK)[None, :]
    masks = [((t // (2 * h)) == (s // (2 * h))) & ((t & h) != 0) & ((s & h) == 0) for h in _LEVEL_HALVES]
    return jnp.asarray(np.stack(masks), F32)


def _level_reference(b_ref, slot, half):
    if half >= 4:
        span = max(2 * half, 8)
        pieces = [jnp.broadcast_to(b_ref[slot, s + half - 1:s + half, :], (span, HGRN_DIM))
                  for s in range(0, CHUNK, span)]
    else:
        r8 = lax.broadcasted_iota(jnp.int32, (8, HGRN_DIM), 0)
        pieces = [jnp.where(r8 < 4,
                            jnp.broadcast_to(b_ref[slot, s + 1:s + 2, :], (8, HGRN_DIM)),
                            jnp.broadcast_to(b_ref[slot, s + 5:s + 6, :], (8, HGRN_DIM)))
                  for s in range(0, CHUNK, 8)]
    return pieces[0] if len(pieces) == 1 else jnp.concatenate(pieces, axis=0)


def _hgrn_kernel(hq_ref, lf_ref, hv_ref, hg_ref, nw_ref, tri_ref, lm_ref, o_ref, st_ref, b_ref):
    @pl.when(pl.program_id(1) == 0)
    def _():
        st_ref[...] = jnp.zeros_like(st_ref)

    tri2 = tri_ref[...]
    odd = (lax.broadcasted_iota(jnp.int32, (CHUNK, HGRN_DIM), 0) & 1) != 0
    masks = [lm_ref[li] != 0.0 for li in range(len(_LEVEL_HALVES))]
    units = [(c, h) for c in range(hq_ref.shape[0] // CHUNK) for h in range(HGRN_HEADS)]
    sl = lambda c, h: (slice(c * CHUNK, (c + 1) * CHUNK), slice(h * HGRN_DIM, (h + 1) * HGRN_DIM))

    bs = []
    for u, (c, h) in enumerate(units):
        lf2 = lf_ref[sl(c, h)]
        hi = lf2.astype(BF16)
        lo = (lf2 - hi.astype(F32)).astype(BF16)
        b = _dot(tri2, jnp.concatenate([hi, lo], axis=0))
        b_ref[u] = b
        bs.append(b)

    outs, qs, ks, fs = [], [], [], []
    for u, (c, h) in enumerate(units):
        b = bs[u]
        q = hq_ref[sl(c, h)].astype(F32)
        v_bf = hv_ref[sl(c, h)]
        f = jnp.exp2(lf_ref[sl(c, h)])
        k = 1.0 - f
        b_last = b[CHUNK - 1:CHUNK, :]
        st = st_ref[h]
        o = _dot_nt((q * jnp.exp2(b)).astype(BF16), st.astype(BF16))
        k_out = (k * jnp.exp2(b_last - b)).astype(BF16)
        st_ref[h] = st * jnp.exp2(b_last) + _dot_tn(v_bf, k_out)
        outs.append(o + jnp.sum(q * k, axis=-1, keepdims=True) * v_bf.astype(F32))
        qs.append(q)
        ks.append(k)
        fs.append(f)

    accs = [jnp.zeros((CHUNK, CHUNK), F32) for _ in units]
    for li, half in enumerate(_LEVEL_HALVES):
        for u in range(len(units)):
            if half == 1:
                e = jnp.where(odd, fs[u], 1.0)
            else:
                e = jnp.exp2(-jnp.abs(bs[u] - _level_reference(b_ref, u, half)))
            accs[u] = jnp.where(masks[li], _dot_nt((qs[u] * e).astype(BF16), (ks[u] * e).astype(BF16)), accs[u])

    for u, (c, h) in enumerate(units):
        o = outs[u] + _dot(accs[u].astype(BF16), hv_ref[sl(c, h)])
        y = _rms(o, nw_ref[...]) * hg_ref[sl(c, h)].astype(F32)
        o_ref[sl(c, h)] = y.astype(BF16)


def _hgrn(hq, lf, hv, hg, hgrn_norm_w, B, S, tb):
    nt = S // tb
    blk = pl.BlockSpec((tb, HGRN_WIDTH), lambda b, n: (b * nt + n, 0))
    tri = np.tril(np.ones((CHUNK, CHUNK), np.float32))
    tri = jnp.asarray(np.concatenate([tri, tri], axis=1), BF16)
    lm = _hgrn_level_masks()
    return pl.pallas_call(
        _hgrn_kernel,
        out_shape=jax.ShapeDtypeStruct((B * S, HGRN_WIDTH), BF16),
        grid=(B, nt),
        in_specs=[blk, blk, blk, blk,
                  pl.BlockSpec((1, HGRN_DIM), lambda b, n: (0, 0)),
                  pl.BlockSpec(tri.shape, lambda b, n: (0, 0)),
                  pl.BlockSpec(lm.shape, lambda b, n: (0, 0, 0))],
        out_specs=blk,
        scratch_shapes=[pltpu.VMEM((HGRN_HEADS, HGRN_DIM, HGRN_DIM), F32),
                        pltpu.VMEM((tb // CHUNK * HGRN_HEADS, CHUNK, HGRN_DIM), F32)],
        compiler_params=pltpu.CompilerParams(dimension_semantics=("arbitrary", "arbitrary"),
                                             vmem_limit_bytes=VMEM_LIMIT),
        name="hgrn2",
    )(hq, lf, hv, hg, hgrn_norm_w[None, :].astype(F32), tri, lm)


def _merge_kernel(x_ref, attn_ref, hgrn_ref, n1w_ref, wz_ref, wba_ref, wbh_ref, wout_ref, n2w_ref,
                  wr_ref, br_ref, utri_ref, x1_out, hn_out, route_out, count_out, run_ref):
    @pl.when(pl.program_id(0) == 0)
    def _():
        run_ref[...] = jnp.zeros_like(run_ref)

    x = x_ref[...]
    xn = _rms(x, n1w_ref[...]).astype(BF16)
    za = _sigmoid(_dot(xn, wz_ref[:, 0:D_MODEL]))
    zb = _sigmoid(_dot(xn, wz_ref[:, D_MODEL:2 * D_MODEL]))
    mixed = za * _dot(attn_ref[...], wba_ref[...]) + zb * _dot(hgrn_ref[...], wbh_ref[...])
    x1 = x + _dot(mixed.astype(BF16), wout_ref[...])
    x1_out[...] = x1
    hn = _rms(x1, n2w_ref[...])
    hn_out[...] = _pack_bf16_pairs(hn)

    logits = _dot_nt(wr_ref[...], hn.astype(BF16)) + br_ref[...]
    r = lax.broadcasted_iota(jnp.int32, logits.shape, 0).astype(F32)
    far = float(ROUTER_ROWS)
    cmax = lambda a: jnp.max(a, axis=0, keepdims=True)
    cmin = lambda a: jnp.min(a, axis=0, keepdims=True)
    csum = lambda a: jnp.sum(a, axis=0, keepdims=True)

    lg = jnp.where(r < N_GROUPS, logits, NEG_BIG)
    mg = cmax(lg)
    gsel = cmin(jnp.where(lg == mg, r, far))
    pgsel = 1.0 / csum(jnp.exp(lg - mg))

    lo = N_GROUPS + EXPERTS_PER_GROUP * gsel
    le = jnp.where((r >= lo) & (r < lo + EXPERTS_PER_GROUP), logits, NEG_BIG)
    m1 = cmax(le)
    i1 = cmin(jnp.where(le == m1, r, far))
    se = csum(jnp.exp(le - m1))
    le2 = jnp.where(r == i1, NEG_BIG, le)
    m2 = cmax(le2)
    i2 = cmin(jnp.where(le2 == m2, r, far))
    top0 = 1.0 / se
    top1 = jnp.exp(m2 - m1) / se
    tsum = top0 + top1
    w0 = pgsel * top0 / tsum
    w1 = pgsel * top1 / tsum

    sel1 = r == i1
    sel2 = r == i2
    onehot = (sel1 | sel2).astype(BF16)
    before = _dot(onehot, utri_ref[...]) + run_ref[...]
    r0 = csum(jnp.where(sel1, before, 0.0))
    r1 = csum(jnp.where(sel2, before, 0.0))
    run_new = run_ref[...] + _dot(onehot, jnp.ones(utri_ref.shape, BF16))
    run_ref[...] = run_new
    count_out[...] = run_new

    row8 = lax.broadcasted_iota(jnp.int32, route_out.shape, 0)
    vals = (i1 - N_GROUPS, i2 - N_GROUPS, w0, w1, r0, r1)
    route = jnp.zeros(route_out.shape, F32)
    for j, val in enumerate(vals):
        route = jnp.where(row8 == j, val, route)
    route_out[...] = route


def _merge(x2, attn, hgrn, norm1_w, w_z, w_ba, w_bh, w_out, norm2_w, w_r, br, tm):
    T = x2.shape[0]
    row = lambda w: pl.BlockSpec((tm, w), lambda i: (i, 0))
    full = lambda a: pl.BlockSpec(a.shape, lambda i: (0,) * a.ndim)
    utri = jnp.asarray(np.triu(np.ones((tm, tm), np.float32), 1), BF16)
    ins = [x2, attn, hgrn, norm1_w[None, :], w_z, w_ba, w_bh, w_out, norm2_w[None, :], w_r, br, utri]
    in_specs = [row(D_MODEL), row(ATTN_WIDTH), row(HGRN_WIDTH)] + [full(a) for a in ins[3:]]
    return pl.pallas_call(
        _merge_kernel,
        out_shape=[jax.ShapeDtypeStruct((T, D_MODEL), F32), jax.ShapeDtypeStruct((T, HALF), jnp.uint32),
                   jax.ShapeDtypeStruct((ROUTE_ROWS, T), F32), jax.ShapeDtypeStruct((ROUTER_ROWS, tm), F32)],
        grid=(T // tm,),
        in_specs=in_specs,
        out_specs=[row(D_MODEL), row(HALF), pl.BlockSpec((ROUTE_ROWS, tm), lambda i: (0, i)),
                   pl.BlockSpec((ROUTER_ROWS, tm), lambda i: (0, 0))],
        scratch_shapes=[pltpu.VMEM((ROUTER_ROWS, tm), F32)],
        compiler_params=pltpu.CompilerParams(dimension_semantics=("arbitrary",),
                                             vmem_limit_bytes=VMEM_LIMIT),
        name="merge_router",
    )(*ins)


HALF = D_MODEL // 2


def _pack_bf16_pairs(x):
    bits = pltpu.bitcast(x.astype(BF16).astype(F32), jnp.uint32)
    return (bits[:, :HALF] >> 16) | bits[:, HALF:]


def _unpack_bf16_pairs(words):
    lo = pltpu.bitcast(words << 16, F32)
    hi = pltpu.bitcast(words & jnp.uint32(0xFFFF0000), F32)
    return lo, hi


def _issue_row_gather(idx_ref, n, src_hbm, dst_ref, sem):
    for r in range(n):
        pltpu.make_async_copy(src_hbm.at[pl.ds(idx_ref[0, 0, r], 1), :],
                              dst_ref.at[pl.ds(r, 1), :], sem).start()


def _wait_row_gather(n, src_hbm, dst_ref, sem):
    pltpu.make_async_copy(src_hbm.at[pl.ds(0, n), :], dst_ref.at[pl.ds(0, n), :], sem).wait()


def _tile_positions(pos, tk):
    nt = pos.shape[0] // tk
    return pos.reshape(nt, tk, TOP_K).transpose(0, 2, 1).reshape(nt, 1, TOP_K * tk)


def _invmap_kernel(pos_ref, tok_ref):
    tokens = pos_ref.shape[2] // TOP_K
    base = pl.program_id(0) * tokens

    def body(j, carry):
        for k in range(TOP_K):
            tok_ref[pos_ref[0, 0, j * TOP_K + k]] = base + j
        return carry
    lax.fori_loop(0, tokens, body, 0, unroll=8)


def _invmap(pos, nb):
    n = pos.size
    return pl.pallas_call(
        _invmap_kernel,
        out_shape=jax.ShapeDtypeStruct((n,), jnp.int32),
        grid=(n // nb,),
        in_specs=[pl.BlockSpec((1, 1, nb), lambda i: (i, 0, 0), memory_space=pltpu.SMEM)],
        out_specs=pl.BlockSpec(memory_space=pltpu.SMEM),
        compiler_params=pltpu.CompilerParams(dimension_semantics=("arbitrary",)),
        name="moe_row_tokens",
    )(pos.reshape(n // nb, 1, nb))


def _expert_kernel(tile_ref, exp_ref, lo_ref, hi_ref, tok_ref, tokn_ref, hn_hbm, wg_ref, wu_ref, wd_ref, y_ref,
                   xbuf, wg_s, wu_s, wd_s, sem):
    w = pl.program_id(0)
    lo = lo_ref[w]
    hi = hi_ref[w]
    tile = tile_ref[w]
    slot = tile % 2
    n_tiles = hn_hbm.shape[0] * TOP_K // EXPERT_TILE
    prev = jnp.maximum(w - 1, 0)

    entered = (w == 0) | (tile != tile_ref[prev])
    whole = (lo == 0) & (hi == EXPERT_TILE)

    @pl.when(w == 0)
    def _():
        _issue_row_gather(tok_ref, EXPERT_TILE, hn_hbm, xbuf.at[0], sem.at[0])

    @pl.when((w == 0) | (exp_ref[w] != exp_ref[prev]))
    def _():
        wg_s[...] = wg_ref[0].astype(BF16)
        wu_s[...] = wu_ref[0].astype(BF16)
        wd_s[...] = wd_ref[0].astype(BF16)

    def ffn(words):
        lo, hi = _unpack_bf16_pairs(words)
        lo = lo.astype(BF16)
        hi = hi.astype(BF16)
        gate = _dot(lo, wg_s[0:HALF, :]) + _dot(hi, wg_s[HALF:, :])
        up = _dot(lo, wu_s[0:HALF, :]) + _dot(hi, wu_s[HALF:, :])
        return _pack_bf16_pairs(_dot((gate * _sigmoid(gate) * up).astype(BF16), wd_s[...]))

    for p in range(2):
        @pl.when(entered & (slot == p))
        def _():
            _issue_row_gather(tokn_ref, EXPERT_TILE, hn_hbm, xbuf.at[1 - p], sem.at[1 - p])

    @pl.when(entered)
    def _():
        _wait_row_gather(EXPERT_TILE, hn_hbm, xbuf.at[slot], sem.at[slot])

    x_ref = xbuf.at[slot]

    @pl.when(whole)
    def _():
        y_ref[...] = ffn(x_ref[...])

    for j in range(EXPERT_TILE // MOE_BLOCK):
        r0 = j * MOE_BLOCK
        rows = slice(r0, r0 + MOE_BLOCK)

        @pl.when(jnp.logical_not(whole) & (lo < r0 + MOE_BLOCK) & (hi > r0))
        def _():
            y = ffn(x_ref[rows, :])
            rowi = lax.broadcasted_iota(jnp.int32, y.shape, 0) + r0
            mine = (rowi >= lo) & (rowi < hi)

            @pl.when(lo <= r0)
            def _():
                y_ref[rows, :] = jnp.where(mine, y, jnp.uint32(0))

            @pl.when(lo > r0)
            def _():
                y_ref[rows, :] = jnp.where(mine, y, y_ref[rows, :])

    @pl.when(w == pl.num_programs(0) - 1)
    def _():
        _wait_row_gather(EXPERT_TILE, hn_hbm, xbuf.at[1 - slot], sem.at[1 - slot])


def _experts(items, row_tok, hn, w_gate, w_up, w_down):
    tile, exp, lo, hi = items
    n_rows = row_tok.shape[0]
    n_tiles = n_rows // EXPERT_TILE
    tok3 = row_tok.reshape(n_tiles, 1, EXPERT_TILE)
    wspec = lambda shape: pl.BlockSpec((1,) + shape, lambda w, t, e, l, h: (e[w], 0, 0))
    tok_spec = lambda ahead: pl.BlockSpec(
        (1, 1, EXPERT_TILE), lambda w, t, e, l, h: (jnp.minimum(t[w] + ahead, n_tiles - 1), 0, 0),
        memory_space=pltpu.SMEM)
    return pl.pallas_call(
        _expert_kernel,
        out_shape=jax.ShapeDtypeStruct((n_rows, HALF), jnp.uint32),
        grid_spec=pltpu.PrefetchScalarGridSpec(
            num_scalar_prefetch=4,
            grid=(tile.shape[0],),
            in_specs=[tok_spec(0), tok_spec(1), pl.BlockSpec(memory_space=pl.ANY),
                      wspec((D_MODEL, EXPERT_FF)), wspec((D_MODEL, EXPERT_FF)), wspec((EXPERT_FF, D_MODEL))],
            out_specs=pl.BlockSpec((EXPERT_TILE, HALF), lambda w, t, e, l, h: (t[w], 0)),
            scratch_shapes=[pltpu.VMEM((2, EXPERT_TILE, HALF), jnp.uint32),
                            pltpu.VMEM((D_MODEL, EXPERT_FF), BF16),
                            pltpu.VMEM((D_MODEL, EXPERT_FF), BF16),
                            pltpu.VMEM((EXPERT_FF, D_MODEL), BF16),
                            pltpu.SemaphoreType.DMA((2,))]),
        compiler_params=pltpu.CompilerParams(dimension_semantics=("arbitrary",),
                                             vmem_limit_bytes=VMEM_LIMIT),
        name="moe_experts",
    )(tile, exp, lo, hi, tok3, tok3, hn, w_gate, w_up, w_down)


def _combine_kernel(pos_ref, posn_ref, x1_ref, w_ref, y_hbm, o_ref, ybuf0, ybuf1, sem):
    i = pl.program_id(0)
    tk = x1_ref.shape[0]
    rows = TOP_K * tk
    bufs = (ybuf0, ybuf1)

    def step(p):
        cur, nxt = bufs[p], bufs[1 - p]
        if p == 0:
            @pl.when(i == 0)
            def _():
                _issue_row_gather(pos_ref, rows, y_hbm, cur, sem.at[p])

        @pl.when(i + 1 < pl.num_programs(0))
        def _():
            _issue_row_gather(posn_ref, rows, y_hbm, nxt, sem.at[1 - p])

        _wait_row_gather(rows, y_hbm, cur, sem.at[p])
        w0 = w_ref[:, 0:1]
        w1 = w_ref[:, 1:2]
        lo0, hi0 = _unpack_bf16_pairs(cur[0:tk, :])
        lo1, hi1 = _unpack_bf16_pairs(cur[tk:2 * tk, :])
        o_ref[:, 0:HALF] = x1_ref[:, 0:HALF] + w0 * lo0 + w1 * lo1
        o_ref[:, HALF:] = x1_ref[:, HALF:] + w0 * hi0 + w1 * hi1

    for p in range(2):
        pl.when(i % 2 == p)(functools.partial(step, p))


def _combine(pos, x1, gate_w, y, tk):
    T = x1.shape[0]
    nt = T // tk
    pos3 = _tile_positions(pos, tk)
    pos_spec = lambda ahead: pl.BlockSpec((1, 1, TOP_K * tk), lambda i: (jnp.minimum(i + ahead, nt - 1), 0, 0),
                                          memory_space=pltpu.SMEM)
    return pl.pallas_call(
        _combine_kernel,
        out_shape=jax.ShapeDtypeStruct((T, D_MODEL), F32),
        grid=(nt,),
        in_specs=[pos_spec(0), pos_spec(1),
                  pl.BlockSpec((tk, D_MODEL), lambda i: (i, 0)),
                  pl.BlockSpec((tk, TOP_K), lambda i: (i, 0)),
                  pl.BlockSpec(memory_space=pl.ANY)],
        out_specs=pl.BlockSpec((tk, D_MODEL), lambda i: (i, 0)),
        scratch_shapes=[pltpu.VMEM((TOP_K * tk, HALF), jnp.uint32), pltpu.VMEM((TOP_K * tk, HALF), jnp.uint32),
                        pltpu.SemaphoreType.DMA((2,))],
        compiler_params=pltpu.CompilerParams(dimension_semantics=("arbitrary",),
                                             vmem_limit_bytes=VMEM_LIMIT),
        name="moe_combine",
    )(pos3, pos3, x1, gate_w, y)


def _routing_tables(route, counts, T):
    e = route[0:TOP_K].astype(jnp.int32)
    rank = route[4:4 + TOP_K].astype(jnp.int32)
    counts = counts.astype(jnp.int32)
    starts = jnp.cumsum(counts) - counts
    ids = jnp.arange(N_EXPERTS, dtype=jnp.int32)
    pos = rank + jnp.sum(jnp.where(e[:, :, None] == ids, starts, 0), axis=-1)
    n_rows = TOP_K * T
    cuts = jnp.sort(jnp.concatenate([jnp.arange(0, n_rows, EXPERT_TILE, dtype=jnp.int32), starts[1:]]))
    ends = jnp.concatenate([cuts[1:], jnp.full((1,), n_rows, jnp.int32)])
    tile = jnp.minimum(cuts // EXPERT_TILE, n_rows // EXPERT_TILE - 1)
    exp = jnp.clip(jnp.sum(starts[None, :] <= cuts[:, None], axis=1) - 1, 0, N_EXPERTS - 1).astype(jnp.int32)
    lo = cuts - tile * EXPERT_TILE
    hi = ends - tile * EXPERT_TILE
    return pos.T, (tile, exp, lo, hi)


def _pick_tile(n, pref):
    t = pref
    while n % t:
        t //= 2
    return t


def kernel(x, positions, norm1_w, w_in, q_norm_w, k_norm_w, attn_sinks, hgrn_lower_bounds, hgrn_norm_w,
           w_branch_attn, w_branch_hgrn, w_out, norm2_w, w_router_group, b_router_group, w_router_expert,
           b_router_expert, w_gate_experts, w_up_experts, w_down_experts):
    B, S, D = x.shape
    T = B * S
    x2 = x.reshape(T, D)
    tm = _pick_tile(T, 512)

    inv_freq = ROPE_THETA ** (-jnp.arange(0, ROT_DIM, 2, dtype=F32) / ROT_DIM)
    ang = positions.astype(F32).reshape(T, 1) * inv_freq[None, :]
    cs = jnp.concatenate(_split3(jnp.concatenate([jnp.cos(ang), jnp.sin(ang)], axis=-1)), axis=-1)

    w_in0 = w_in[0]
    dup = HEAD_DIM * (np.arange(2 * KV_WIDTH) // LANES) + np.arange(2 * KV_WIDTH) % HEAD_DIM
    w_in_a = jnp.concatenate([w_in0[:, :_OFF_K], w_in0[:, _OFF_K + dup], w_in0[:, _OFF_V + dup],
                              w_in0[:, _OFF_HQ:_OFF_Z]], axis=1).astype(BF16)
    w_z = w_in0[:, _OFF_Z:].astype(BF16)

    q, k, v, hq, lf, hv, hg = _inproj(x2, norm1_w[0], w_in_a, cs, q_norm_w[0], k_norm_w[0],
                                      hgrn_lower_bounds.astype(F32), tm)
    attn = _attention(q, k, v, attn_sinks[0].astype(F32), B, S)
    hgrn = _hgrn(hq, lf, hv, hg, hgrn_norm_w[0], B, S, _pick_tile(S, 512))

    pad = ROUTER_ROWS - N_GROUPS - N_EXPERTS
    w_r = jnp.concatenate([w_router_group[0].T, w_router_expert[0].T, jnp.zeros((pad, D), F32)], axis=0)
    b_r =jnp.concatenate([b_router_group[0], b_router_expert[0], jnp.zeros((pad,), F32)]).astype(F32)
    b_r = jnp.broadcast_to(b_r[:, None], (ROUTER_ROWS, tm))

    x1, hn, route, counts = _merge(x2, attn, hgrn, norm1_w[0], w_z, w_branch_attn[0].astype(BF16),
                                   w_branch_hgrn[0].astype(BF16), w_out[0].astype(BF16), norm2_w[0],
                                   w_r.astype(BF16), b_r, tm)

    pos, items = _routing_tables(route, counts[N_GROUPS:N_GROUPS + N_EXPERTS, 0], T)
    tk = _pick_tile(T, 256)
    row_tok = _invmap(pos, _pick_tile(pos.size, 4096))
    y = _experts(items, row_tok, hn, w_gate_experts[0], w_up_experts[0], w_down_experts[0])
    out = _combine(pos, x1, route[2:2 + TOP_K].T, y, tk)
    return out.reshape(B, S, D)
```

```python
import functools

import numpy as np
import jax
import jax.numpy as jnp
from jax import lax
from jax.experimental import pallas as pl
from jax.experimental.pallas import tpu as pltpu

F32 = jnp.float32
BF16 = jnp.bfloat16

D_MODEL = 1024
N_Q_HEADS = 8
N_KV_HEADS = 2
GROUP = N_Q_HEADS // N_KV_HEADS
HEAD_DIM = 64
ROT_DIM = HEAD_DIM // 4
ROT_HALF = ROT_DIM // 2
ROPE_THETA = 500000.0
WINDOW = 128
ATTN_WIDTH = N_Q_HEADS * HEAD_DIM
KV_WIDTH = N_KV_HEADS * HEAD_DIM

HGRN_HEADS = 4
HGRN_DIM = 128
HGRN_WIDTH = HGRN_HEADS * HGRN_DIM
CHUNK = 64

N_GROUPS = 4
EXPERTS_PER_GROUP = 8
N_EXPERTS = N_GROUPS * EXPERTS_PER_GROUP
TOP_K = 2
EXPERT_FF = 512
MOE_BLOCK = 128
EXPERT_TILE = 512
NORM_EPS = 1e-6
ROUTER_ROWS = 40
ROUTE_ROWS = 8

LANES = 128
NEG_BIG = -1e30
LOG2_E = 1.4426950408889634

_OFF_Q, _OFF_K, _OFF_V = 0, ATTN_WIDTH, ATTN_WIDTH + KV_WIDTH
_OFF_HQ = ATTN_WIDTH + 2 * KV_WIDTH
_OFF_HF = _OFF_HQ + HGRN_WIDTH
_OFF_HI = _OFF_HF + HGRN_WIDTH
_OFF_HG = _OFF_HI + HGRN_WIDTH
_OFF_Z = _OFF_HG + HGRN_WIDTH
_A_Q, _A_K, _A_V = 0, ATTN_WIDTH, ATTN_WIDTH + 2 * KV_WIDTH
_A_HQ = ATTN_WIDTH + 4 * KV_WIDTH
_A_HF = _A_HQ + HGRN_WIDTH
_A_HI = _A_HF + HGRN_WIDTH
_A_HG = _A_HI + HGRN_WIDTH

VMEM_LIMIT = 56 * 1024 * 1024


def _split3(a):
    hi = a.astype(BF16)
    r1 = a - hi.astype(F32)
    mid = r1.astype(BF16)
    lo = (r1 - mid.astype(F32)).astype(BF16)
    return hi, mid, lo


def _dot(a, b):
    return jnp.dot(a, b, preferred_element_type=F32)


def _dot_nt(a, b):
    return lax.dot_general(a, b, (((1,), (1,)), ((), ())), preferred_element_type=F32)


def _dot_tn(a, b):
    return lax.dot_general(a, b, (((0,), (0,)), ((), ())), preferred_element_type=F32)


def _sigmoid(x):
    return 1.0 / (1.0 + jnp.exp(-x))


def _rms(x, w):
    ms = jnp.mean(x * x, axis=-1, keepdims=True)
    return x * lax.rsqrt(ms + NORM_EPS) * w


def _inproj_kernel(x_ref, n1w_ref, w_ref, cs_ref, rope_e_ref, rope_c0_ref, qw_ref, kw_ref,
                   mq_ref, mk_ref, lbp_ref,
                   q_out, k_out, v_out, hq_out, lf_out, hv_out, hg_out):
    xn = _rms(x_ref[...], n1w_ref[...]).astype(BF16)

    def proj(off, width):
        return _dot(xn, w_ref[:, off:off + width])

    tabs = _dot(cs_ref[...], rope_e_ref[...])
    c_tab = tabs[:, 0:LANES] + rope_c0_ref[...]
    s1_tab = tabs[:, LANES:2 * LANES]
    s2_tab = tabs[:, 2 * LANES:3 * LANES]

    def norm_rope(t, mavg_ref, w_row, scale):
        ms = _dot((t * t).astype(BF16), mavg_ref[...])
        tn = t * lax.rsqrt(ms + NORM_EPS) * w_row
        if scale != 1.0:
            tn = tn * scale
        outs = []
        for j in range(t.shape[1] // LANES):
            c = tn[:, j * LANES:(j + 1) * LANES]
            outs.append(c * c_tab
                        + pltpu.roll(c, LANES - ROT_HALF, 1) * s1_tab
                        + pltpu.roll(c, ROT_HALF, 1) * s2_tab)
        return outs[0] if len(outs) == 1 else jnp.concatenate(outs, axis=1)

    q_out[...] = norm_rope(proj(_A_Q, ATTN_WIDTH), mq_ref, qw_ref[...], HEAD_DIM ** -0.5).astype(BF16)
    k_out[...] = norm_rope(proj(_A_K, 2 * KV_WIDTH), mk_ref, kw_ref[...], 1.0).astype(BF16)
    v_out[...] = proj(_A_V, 2 * KV_WIDTH).astype(BF16)

    hq = proj(_A_HQ, HGRN_WIDTH)
    hq_out[...] = (hq * _sigmoid(hq)).astype(BF16)
    h0 = lbp_ref[0:1, :]
    h1 = lbp_ref[1:2, :]
    hm = jnp.maximum(h0, h1)
    e0 = jnp.exp(h0 - hm)
    e1 = jnp.exp(h1 - hm)
    lb = e0 / (e0 + e1)
    fg = lb + (1.0 - lb) * _sigmoid(proj(_A_HF, HGRN_WIDTH))
    lf_out[...] = jnp.log(fg) * LOG2_E
    hv_out[...] = proj(_A_HI, HGRN_WIDTH).astype(BF16)
    hg = proj(_A_HG, HGRN_WIDTH)
    hg_out[...] = (hg * _sigmoid(hg)).astype(BF16)


def _rope_constants():
    e = np.zeros((2 * ROT_HALF, 3 * LANES), np.float32)
    c0 = np.zeros((1, LANES), np.float32)
    for lane in range(LANES):
        d = lane % HEAD_DIM
        if d < ROT_HALF:
            e[d, lane] = 1.0
            e[ROT_HALF + d, LANES + lane] = -1.0
        elif d < ROT_DIM:
            e[d - ROT_HALF, lane] = 1.0
            e[ROT_HALF + d - ROT_HALF, 2 * LANES + lane] = 1.0
        else:
            c0[0, lane] = 1.0
    return jnp.asarray(np.concatenate([e, e, e], axis=0), BF16), jnp.asarray(c0, F32)


def _head_mean_matrix(width):
    idx = np.arange(width) // HEAD_DIM
    m = (idx[:, None] == idx[None, :]).astype(np.float32) / HEAD_DIM
    return jnp.asarray(m, BF16)


def _inproj(x2, norm1_w, w_in_a, cs, q_norm_w, k_norm_w, lbp, tm):
    T = x2.shape[0]
    rope_e, rope_c0 = _rope_constants()
    qw = jnp.tile(q_norm_w.astype(F32), N_Q_HEADS)[None, :]
    kw = jnp.tile(k_norm_w.astype(F32), 2 * N_KV_HEADS)[None, :]
    mq = _head_mean_matrix(ATTN_WIDTH)
    mk = _head_mean_matrix(2 * KV_WIDTH)
    row = lambda w: pl.BlockSpec((tm, w), lambda i: (i, 0))
    full = lambda a: pl.BlockSpec(a.shape, lambda i: (0,) * a.ndim)
    ins = [x2, norm1_w[None, :], w_in_a, cs, rope_e, rope_c0, qw, kw, mq, mk, lbp]
    in_specs = [row(D_MODEL), full(ins[1]), full(w_in_a), row(cs.shape[1])] + [full(a) for a in ins[4:]]
    outs = [(ATTN_WIDTH, BF16), (2 * KV_WIDTH, BF16), (2 * KV_WIDTH, BF16), (HGRN_WIDTH, BF16),
            (HGRN_WIDTH, F32), (HGRN_WIDTH, BF16), (HGRN_WIDTH, BF16)]
    return pl.pallas_call(
        _inproj_kernel,
        out_shape=[jax.ShapeDtypeStruct((T, w), dt) for w, dt in outs],
        grid=(T // tm,),
        in_specs=in_specs,
        out_specs=[row(w) for w, _ in outs],
        compiler_params=pltpu.CompilerParams(dimension_semantics=("arbitrary",),
                                             vmem_limit_bytes=VMEM_LIMIT),
        name="inproj",
    )(*ins)


ATTN_QBLOCKS = 8


def _attn_kernel(sink_ref, q_ref, kc_ref, kp_ref, vc_ref, vp_ref, half_ref, o_ref):
    n_qblocks = q_ref.shape[0] // WINDOW
    has_prev = pl.program_id(1) > 0
    qi = lax.broadcasted_iota(jnp.int32, (WINDOW, 2 * WINDOW), 0)
    kj = lax.broadcasted_iota(jnp.int32, (WINDOW, 2 * WINDOW), 1)
    in_window = ((kj < WINDOW) & (kj > qi)) | ((kj >= WINDOW) & (kj - WINDOW <= qi))
    first_valid = in_window & ((kj >= WINDOW) | has_prev)
    left = lax.broadcasted_iota(jnp.int32, (WINDOW, LANES), 1) < HEAD_DIM
    half = (half_ref[0], half_ref[1])

    ks, rhs = [], []
    for h in range(N_KV_HEADS):
        cols = slice(h * LANES, (h + 1) * LANES)
        kall = jnp.concatenate([kp_ref[:, cols], kc_ref[:, cols]], axis=0)
        vall = jnp.concatenate([vp_ref[:, cols], vc_ref[:, cols]], axis=0)
        ks.append([kall * hm for hm in half])
        rhs.append([jnp.concatenate([vall * hm, hm], axis=1) for hm in half])

    units = [(j, h, pr, side) for j in range(n_qblocks) for h in range(N_KV_HEADS)
             for pr in range(GROUP // 2) for side in range(2)]
    scores = []
    for j, h, pr, side in units:
        pair = h * (GROUP // 2) + pr
        qp = q_ref[j * WINDOW:(j + 1) * WINDOW, pair * LANES:(pair + 1) * LANES]
        s = _dot_nt(qp, ks[h][side][j * WINDOW:(j + 2) * WINDOW])
        scores.append(jnp.where(first_valid if j == 0 else in_window, s, NEG_BIG))
    probs, sink_terms = [], []
    for (j, h, pr, side), s in zip(units, scores):
        sink = sink_ref[2 * (h * (GROUP // 2) + pr) + side]
        m = jnp.maximum(jnp.max(s, axis=-1, keepdims=True), sink)
        probs.append(jnp.exp(s - m).astype(BF16))
        sink_terms.append(jnp.exp(sink - m))
    acc = []
    for (j, h, pr, side), p in zip(units, probs):
        acc.append(_dot(p, rhs[h][side][j * WINDOW:(j + 2) * WINDOW]))
    for j in range(n_qblocks):
        outs = []
        for u in range(0, len(units), 2):
            if units[u][0] == j:
                both = acc[u] + acc[u + 1]
                den = both[:, LANES:2 * LANES] + jnp.where(left, sink_terms[u], sink_terms[u + 1])
                outs.append(both[:, 0:LANES] / den)
        o_ref[j * WINDOW:(j + 1) * WINDOW, :] = jnp.concatenate(outs, axis=1).astype(BF16)


def _attention(q, k, v, sinks, B, S):
    qblocks = _pick_tile(S // WINDOW, ATTN_QBLOCKS)
    rows = qblocks * WINDOW
    nb = S // rows
    cur = lambda b, n: (b * nb + n, 0)
    prev = lambda b, n: (jnp.maximum((b * nb + n) * qblocks - 1, 0), 0)
    lane_left = np.arange(LANES) < HEAD_DIM
    half = jnp.asarray(np.broadcast_to(np.stack([lane_left, ~lane_left])[:, None, :],
                                       (2, rows + WINDOW, LANES)), BF16)
    return pl.pallas_call(
        _attn_kernel,
        out_shape=jax.ShapeDtypeStruct((B * S, ATTN_WIDTH), BF16),
        grid=(B, nb),
        in_specs=[pl.BlockSpec(memory_space=pltpu.SMEM),
                  pl.BlockSpec((rows, ATTN_WIDTH), cur),
                  pl.BlockSpec((rows, 2 * KV_WIDTH), cur),
                  pl.BlockSpec((WINDOW, 2 * KV_WIDTH), prev),
                  pl.BlockSpec((rows, 2 * KV_WIDTH), cur),
                  pl.BlockSpec((WINDOW, 2 * KV_WIDTH), prev),
                  pl.BlockSpec(half.shape, lambda b, n: (0, 0, 0))],
        out_specs=pl.BlockSpec((rows, ATTN_WIDTH), cur),
        compiler_params=pltpu.CompilerParams(dimension_semantics=("arbitrary", "arbitrary"),
                                             vmem_limit_bytes=VMEM_LIMIT),
        name="swa_attention",
    )(sinks, q, k, k, v, v, half)


_LEVEL_HALVES = (1, 2, 4, 8, 16, 32)


def _hgrn_level_masks():
    t = np.arange(CHUNK)[:, None]
    s = np.arange(CHUNK)[None, :]
    masks = [((t // (2 * h)) == (s // (2 * h))) & ((t & h) != 0) & ((s & h) == 0) for h in _LEVEL_HALVES]
    return jnp.asarray(np.stack(masks), F32)


def _level_reference(b_ref, slot, half):
    if half >= 4:
        span = max(2 * half, 8)
        pieces = [jnp.broadcast_to(b_ref[slot, s + half - 1:s + half, :], (span, HGRN_DIM))
                  for s in range(0, CHUNK, span)]
    else:
        r8 = lax.broadcasted_iota(jnp.int32, (8, HGRN_DIM), 0)
        pieces = [jnp.where(r8 < 4,
                            jnp.broadcast_to(b_ref[slot, s + 1:s + 2, :], (8, HGRN_DIM)),
                            jnp.broadcast_to(b_ref[slot, s + 5:s + 6, :], (8, HGRN_DIM)))
                  for s in range(0, CHUNK, 8)]
    return pieces[0] if len(pieces) == 1 else jnp.concatenate(pieces, axis=0)


def _hgrn_kernel(hq_ref, lf_ref, hv_ref, hg_ref, nw_ref, tri_ref, lm_ref, o_ref, st_ref, b_ref):
    @pl.when(pl.program_id(1) == 0)
    def _():
        st_ref[...] = jnp.zeros_like(st_ref)

    tri2 = tri_ref[...]
    odd = (lax.broadcasted_iota(jnp.int32, (CHUNK, HGRN_DIM), 0) & 1) != 0
    masks = [lm_ref[li] != 0.0 for li in range(len(_LEVEL_HALVES))]
    units = [(c, h) for c in range(hq_ref.shape[0] // CHUNK) for h in range(HGRN_HEADS)]
    sl = lambda c, h: (slice(c * CHUNK, (c + 1) * CHUNK), slice(h * HGRN_DIM, (h + 1) * HGRN_DIM))

    bs = []
    for u, (c, h) in enumerate(units):
        lf2 = lf_ref[sl(c, h)]
        hi = lf2.astype(BF16)
        lo = (lf2 - hi.astype(F32)).astype(BF16)
        b = _dot(tri2, jnp.concatenate([hi, lo], axis=0))
        b_ref[u] = b
        bs.append(b)

    outs, qs, ks, fs = [], [], [], []
    for u, (c, h) in enumerate(units):
        b = bs[u]
        q = hq_ref[sl(c, h)].astype(F32)
        v_bf = hv_ref[sl(c, h)]
        f = jnp.exp2(lf_ref[sl(c, h)])
        k = 1.0 - f
        b_last = b[CHUNK - 1:CHUNK, :]
        st = st_ref[h]
        o = _dot_nt((q * jnp.exp2(b)).astype(BF16), st.astype(BF16))
        k_out = (k * jnp.exp2(b_last - b)).astype(BF16)
        st_ref[h] = st * jnp.exp2(b_last) + _dot_tn(v_bf, k_out)
        outs.append(o + jnp.sum(q * k, axis=-1, keepdims=True) * v_bf.astype(F32))
        qs.append(q)
        ks.append(k)
        fs.append(f)

    accs = [jnp.zeros((CHUNK, CHUNK), F32) for _ in units]
    for li, half in enumerate(_LEVEL_HALVES):
        for u in range(len(units)):
            if half == 1:
                e = jnp.where(odd, fs[u], 1.0)
            else:
                e = jnp.exp2(-jnp.abs(bs[u] - _level_reference(b_ref, u, half)))
            accs[u] = jnp.where(masks[li], _dot_nt((qs[u] * e).astype(BF16), (ks[u] * e).astype(BF16)), accs[u])

    for u, (c, h) in enumerate(units):
        o = outs[u] + _dot(accs[u].astype(BF16), hv_ref[sl(c, h)])
        y = _rms(o, nw_ref[...]) * hg_ref[sl(c, h)].astype(F32)
        o_ref[sl(c, h)] = y.astype(BF16)


def _hgrn(hq, lf, hv, hg, hgrn_norm_w, B, S, tb):
    nt = S // tb
    blk = pl.BlockSpec((tb, HGRN_WIDTH), lambda b, n: (b * nt + n, 0))
    tri = np.tril(np.ones((CHUNK, CHUNK), np.float32))
    tri = jnp.asarray(np.concatenate([tri, tri], axis=1), BF16)
    lm = _hgrn_level_masks()
    return pl.pallas_call(
        _hgrn_kernel,
        out_shape=jax.ShapeDtypeStruct((B * S, HGRN_WIDTH), BF16),
        grid=(B, nt),
        in_specs=[blk, blk, blk, blk,
                  pl.BlockSpec((1, HGRN_DIM), lambda b, n: (0, 0)),
                  pl.BlockSpec(tri.shape, lambda b, n: (0, 0)),
                  pl.BlockSpec(lm.shape, lambda b, n: (0, 0, 0))],
        out_specs=blk,
        scratch_shapes=[pltpu.VMEM((HGRN_HEADS, HGRN_DIM, HGRN_DIM), F32),
                        pltpu.VMEM((tb // CHUNK * HGRN_HEADS, CHUNK, HGRN_DIM), F32)],
        compiler_params=pltpu.CompilerParams(dimension_semantics=("arbitrary", "arbitrary"),
                                             vmem_limit_bytes=VMEM_LIMIT),
        name="hgrn2",
    )(hq, lf, hv, hg, hgrn_norm_w[None, :].astype(F32), tri, lm)


def _merge_kernel(x_ref, attn_ref, hgrn_ref, n1w_ref, wz_ref, wba_ref, wbh_ref, wout_ref, n2w_ref,
                  wr_ref, br_ref, utri_ref, x1_out, hn_out, route_out, count_out, run_ref):
    @pl.when(pl.program_id(0) == 0)
    def _():
        run_ref[...] = jnp.zeros_like(run_ref)

    x = x_ref[...]
    xn = _rms(x, n1w_ref[...]).astype(BF16)
    za = _sigmoid(_dot(xn, wz_ref[:, 0:D_MODEL]))
    zb = _sigmoid(_dot(xn, wz_ref[:, D_MODEL:2 * D_MODEL]))
    mixed = za * _dot(attn_ref[...], wba_ref[...]) + zb * _dot(hgrn_ref[...], wbh_ref[...])
    x1 = x + _dot(mixed.astype(BF16), wout_ref[...])
    x1_out[...] = x1
    hn = _rms(x1, n2w_ref[...])
    hn_out[...] = _pack_bf16_pairs(hn)

    logits = _dot_nt(wr_ref[...], hn.astype(BF16)) + br_ref[...]
    r = lax.broadcasted_iota(jnp.int32, logits.shape, 0).astype(F32)
    far = float(ROUTER_ROWS)
    cmax = lambda a: jnp.max(a, axis=0, keepdims=True)
    cmin = lambda a: jnp.min(a, axis=0, keepdims=True)
    csum = lambda a: jnp.sum(a, axis=0, keepdims=True)

    lg = jnp.where(r < N_GROUPS, logits, NEG_BIG)
    mg = cmax(lg)
    gsel = cmin(jnp.where(lg == mg, r, far))
    pgsel = 1.0 / csum(jnp.exp(lg - mg))

    lo = N_GROUPS + EXPERTS_PER_GROUP * gsel
    le = jnp.where((r >= lo) & (r < lo + EXPERTS_PER_GROUP), logits, NEG_BIG)
    m1 = cmax(le)
    i1 = cmin(jnp.where(le == m1, r, far))
    se = csum(jnp.exp(le - m1))
    le2 = jnp.where(r == i1, NEG_BIG, le)
    m2 = cmax(le2)
    i2 = cmin(jnp.where(le2 == m2, r, far))
    top0 = 1.0 / se
    top1 = jnp.exp(m2 - m1) / se
    tsum = top0 + top1
    w0 = pgsel * top0 / tsum
    w1 = pgsel * top1 / tsum

    sel1 = r == i1
    sel2 = r == i2
    onehot = (sel1 | sel2).astype(BF16)
    before = _dot(onehot, utri_ref[...]) + run_ref[...]
    r0 = csum(jnp.where(sel1, before, 0.0))
    r1 = csum(jnp.where(sel2, before, 0.0))
    run_new = run_ref[...] + _dot(onehot, jnp.ones(utri_ref.shape, BF16))
    run_ref[...] = run_new
    count_out[...] = run_new

    row8 = lax.broadcasted_iota(jnp.int32, route_out.shape, 0)
    vals = (i1 - N_GROUPS, i2 - N_GROUPS, w0, w1, r0, r1)
    route = jnp.zeros(route_out.shape, F32)
    for j, val in enumerate(vals):
        route = jnp.where(row8 == j, val, route)
    route_out[...] = route


def _merge(x2, attn, hgrn, norm1_w, w_z, w_ba, w_bh, w_out, norm2_w, w_r, br, tm):
    T = x2.shape[0]
    row = lambda w: pl.BlockSpec((tm, w), lambda i: (i, 0))
    full = lambda a: pl.BlockSpec(a.shape, lambda i: (0,) * a.ndim)
    utri = jnp.asarray(np.triu(np.ones((tm, tm), np.float32), 1), BF16)
    ins = [x2, attn, hgrn, norm1_w[None, :], w_z, w_ba, w_bh, w_out, norm2_w[None, :], w_r, br, utri]
    in_specs = [row(D_MODEL), row(ATTN_WIDTH), row(HGRN_WIDTH)] + [full(a) for a in ins[3:]]
    return pl.pallas_call(
        _merge_kernel,
        out_shape=[jax.ShapeDtypeStruct((T, D_MODEL), F32), jax.ShapeDtypeStruct((T, HALF), jnp.uint32),
                   jax.ShapeDtypeStruct((ROUTE_ROWS, T), F32), jax.ShapeDtypeStruct((ROUTER_ROWS, tm), F32)],
        grid=(T // tm,),
        in_specs=in_specs,
        out_specs=[row(D_MODEL), row(HALF), pl.BlockSpec((ROUTE_ROWS, tm), lambda i: (0, i)),
                   pl.BlockSpec((ROUTER_ROWS, tm), lambda i: (0, 0))],
        scratch_shapes=[pltpu.VMEM((ROUTER_ROWS, tm), F32)],
        compiler_params=pltpu.CompilerParams(dimension_semantics=("arbitrary",),
                                             vmem_limit_bytes=VMEM_LIMIT),
        name="merge_router",
    )(*ins)


HALF = D_MODEL // 2


def _pack_bf16_pairs(x):
    bits = pltpu.bitcast(x.astype(BF16).astype(F32), jnp.uint32)
    return (bits[:, :HALF] >> 16) | bits[:, HALF:]


def _unpack_bf16_pairs(words):
    lo = pltpu.bitcast(words << 16, F32)
    hi = pltpu.bitcast(words & jnp.uint32(0xFFFF0000), F32)
    return lo, hi


def _issue_row_gather(idx_ref, n, src_hbm, dst_ref, sem):
    for r in range(n):
        pltpu.make_async_copy(src_hbm.at[pl.ds(idx_ref[0, 0, r], 1), :],
                              dst_ref.at[pl.ds(r, 1), :], sem).start()


def _wait_row_gather(n, src_hbm, dst_ref, sem):
    pltpu.make_async_copy(src_hbm.at[pl.ds(0, n), :], dst_ref.at[pl.ds(0, n), :], sem).wait()


def _tile_positions(pos, tk):
    nt = pos.shape[0] // tk
    return pos.reshape(nt, tk, TOP_K).transpose(0, 2, 1).reshape(nt, 1, TOP_K * tk)


def _dispatch_kernel(pos_ref, hn_ref, xs_hbm, sem):
    tk = hn_ref.shape[0]
    for j in range(TOP_K * tk):
        pltpu.make_async_copy(hn_ref.at[pl.ds(j % tk, 1), :],
                              xs_hbm.at[pl.ds(pos_ref[0, 0, j], 1), :], sem).start()
    for _ in range(TOP_K):
        pltpu.make_async_copy(hn_ref, xs_hbm.at[pl.ds(0, tk), :], sem).wait()


def _dispatch(pos, hn, tk):
    T = hn.shape[0]
    return pl.pallas_call(
        _dispatch_kernel,
        out_shape=jax.ShapeDtypeStruct((TOP_K * T, HALF), jnp.uint32),
        grid=(T // tk,),
        in_specs=[pl.BlockSpec((1, 1, TOP_K * tk), lambda i: (i, 0, 0), memory_space=pltpu.SMEM),
                  pl.BlockSpec((tk, HALF), lambda i: (i, 0))],
        out_specs=pl.BlockSpec(memory_space=pl.ANY),
        scratch_shapes=[pltpu.SemaphoreType.DMA],
        compiler_params=pltpu.CompilerParams(dimension_semantics=("arbitrary",),
                                             vmem_limit_bytes=VMEM_LIMIT),
        name="moe_dispatch",
    )(_tile_positions(pos, tk), hn)


def _expert_kernel(tile_ref, exp_ref, lo_ref, hi_ref, x_ref, wg_ref, wu_ref, wd_ref, y_ref, wg_s, wu_s, wd_s):
    w = pl.program_id(0)
    lo = lo_ref[w]
    hi = hi_ref[w]
    prev = jnp.maximum(w - 1, 0)
    whole = (lo == 0) & (hi == EXPERT_TILE)

    @pl.when((w == 0) | (exp_ref[w] != exp_ref[prev]))
    def _():
        wg_s[...] = wg_ref[0].astype(BF16)
        wu_s[...] = wu_ref[0].astype(BF16)
        wd_s[...] = wd_ref[0].astype(BF16)

    def ffn(words):
        lo, hi = _unpack_bf16_pairs(words)
        lo = lo.astype(BF16)
        hi = hi.astype(BF16)
        gate = _dot(lo, wg_s[0:HALF, :]) + _dot(hi, wg_s[HALF:, :])
        up = _dot(lo, wu_s[0:HALF, :]) + _dot(hi, wu_s[HALF:, :])
        return _pack_bf16_pairs(_dot((gate * _sigmoid(gate) * up).astype(BF16), wd_s[...]))

    @pl.when(whole)
    def _():
        y_ref[...] = ffn(x_ref[...])

    for j in range(EXPERT_TILE // MOE_BLOCK):
        r0 = j * MOE_BLOCK
        rows = slice(r0, r0 + MOE_BLOCK)

        @pl.when(jnp.logical_not(whole) & (lo < r0 + MOE_BLOCK) & (hi > r0))
        def _():
            y = ffn(x_ref[rows, :])
            rowi = lax.broadcasted_iota(jnp.int32, y.shape, 0) + r0
            mine = (rowi >= lo) & (rowi < hi)

            @pl.when(lo <= r0)
            def _():
                y_ref[rows, :] = jnp.where(mine, y, jnp.uint32(0))

            @pl.when(lo > r0)
            def _():
                y_ref[rows, :] = jnp.where(mine, y, y_ref[rows, :])


def _experts(items, xs, w_gate, w_up, w_down):
    tile, exp, lo, hi = items
    wspec = lambda shape: pl.BlockSpec((1,) + shape, lambda w, t, e, l, h: (e[w], 0, 0))
    xspec = pl.BlockSpec((EXPERT_TILE, HALF), lambda w, t, e, l, h: (t[w], 0))
    return pl.pallas_call(
        _expert_kernel,
        out_shape=jax.ShapeDtypeStruct(xs.shape, jnp.uint32),
        grid_spec=pltpu.PrefetchScalarGridSpec(
            num_scalar_prefetch=4,
            grid=(tile.shape[0],),
            in_specs=[xspec, wspec((D_MODEL, EXPERT_FF)), wspec((D_MODEL, EXPERT_FF)),
                      wspec((EXPERT_FF, D_MODEL))],
            out_specs=xspec,
            scratch_shapes=[pltpu.VMEM((D_MODEL, EXPERT_FF), BF16),
                            pltpu.VMEM((D_MODEL, EXPERT_FF), BF16),
                            pltpu.VMEM((EXPERT_FF, D_MODEL), BF16)]),
        compiler_params=pltpu.CompilerParams(dimension_semantics=("arbitrary",),
                                             vmem_limit_bytes=VMEM_LIMIT),
        name="moe_experts",
    )(tile, exp, lo, hi, xs, w_gate, w_up, w_down)


def _combine_kernel(pos_ref, posn_ref, x1_ref, w_ref, y_hbm, o_ref, ybuf0, ybuf1, sem):
    i = pl.program_id(0)
    tk = x1_ref.shape[0]
    rows = TOP_K * tk
    bufs = (ybuf0, ybuf1)

    def step(p):
        cur, nxt = bufs[p], bufs[1 - p]
        if p == 0:
            @pl.when(i == 0)
            def _():
                _issue_row_gather(pos_ref, rows, y_hbm, cur, sem.at[p])

        @pl.when(i + 1 < pl.num_programs(0))
        def _():
            _issue_row_gather(posn_ref, rows, y_hbm, nxt, sem.at[1 - p])

        _wait_row_gather(rows, y_hbm, cur, sem.at[p])
        w0 = w_ref[:, 0:1]
        w1 = w_ref[:, 1:2]
        lo0, hi0 = _unpack_bf16_pairs(cur[0:tk, :])
        lo1, hi1 = _unpack_bf16_pairs(cur[tk:2 * tk, :])
        o_ref[:, 0:HALF] = x1_ref[:, 0:HALF] + w0 * lo0 + w1 * lo1
        o_ref[:, HALF:] = x1_ref[:, HALF:] + w0 * hi0 + w1 * hi1

    for p in range(2):
        pl.when(i % 2 == p)(functools.partial(step, p))


def _combine(pos, x1, gate_w, y, tk):
    T = x1.shape[0]
    nt = T // tk
    pos3 = _tile_positions(pos, tk)
    pos_spec = lambda ahead: pl.BlockSpec((1, 1, TOP_K * tk), lambda i: (jnp.minimum(i + ahead, nt - 1), 0, 0),
                                          memory_space=pltpu.SMEM)
    return pl.pallas_call(
        _combine_kernel,
        out_shape=jax.ShapeDtypeStruct((T, D_MODEL), F32),
        grid=(nt,),
        in_specs=[pos_spec(0), pos_spec(1),
                  pl.BlockSpec((tk, D_MODEL), lambda i: (i, 0)),
                  pl.BlockSpec((tk, TOP_K), lambda i: (i, 0)),
                  pl.BlockSpec(memory_space=pl.ANY)],
        out_specs=pl.BlockSpec((tk, D_MODEL), lambda i: (i, 0)),
        scratch_shapes=[pltpu.VMEM((TOP_K * tk, HALF), jnp.uint32), pltpu.VMEM((TOP_K * tk, HALF), jnp.uint32),
                        pltpu.SemaphoreType.DMA((2,))],
        compiler_params=pltpu.CompilerParams(dimension_semantics=("arbitrary",),
                                             vmem_limit_bytes=VMEM_LIMIT),
        name="moe_combine",
    )(pos3, pos3, x1, gate_w, y)


def _routing_tables(route, counts, T):
    e = route[0:TOP_K].astype(jnp.int32)
    rank = route[4:4 + TOP_K].astype(jnp.int32)
    counts = counts.astype(jnp.int32)
    starts = jnp.cumsum(counts) - counts
    ids = jnp.arange(N_EXPERTS, dtype=jnp.int32)
    pos = rank + jnp.sum(jnp.where(e[:, :, None] == ids, starts, 0), axis=-1)
    n_rows = TOP_K * T
    cuts = jnp.sort(jnp.concatenate([jnp.arange(0, n_rows, EXPERT_TILE, dtype=jnp.int32), starts[1:]]))
    ends = jnp.concatenate([cuts[1:], jnp.full((1,), n_rows, jnp.int32)])
    tile = jnp.minimum(cuts // EXPERT_TILE, n_rows // EXPERT_TILE - 1)
    exp = jnp.clip(jnp.sum(starts[None, :] <= cuts[:, None], axis=1) - 1, 0, N_EXPERTS - 1).astype(jnp.int32)
    lo = cuts - tile * EXPERT_TILE
    hi = ends - tile * EXPERT_TILE
    return pos.T, (tile, exp, lo, hi)


def _pick_tile(n, pref):
    t = pref
    while n % t:
        t //= 2
    return t


def kernel(x, positions, norm1_w, w_in, q_norm_w, k_norm_w, attn_sinks, hgrn_lower_bounds, hgrn_norm_w,
           w_branch_attn, w_branch_hgrn, w_out, norm2_w, w_router_group, b_router_group, w_router_expert,
           b_router_expert, w_gate_experts, w_up_experts, w_down_experts):
    B, S, D = x.shape
    T = B * S
    x2 = x.reshape(T, D)
    tm = _pick_tile(T, 512)

    inv_freq = ROPE_THETA ** (-jnp.arange(0, ROT_DIM, 2, dtype=F32) / ROT_DIM)
    ang = positions.astype(F32).reshape(T, 1) * inv_freq[None, :]
    cs = jnp.concatenate(_split3(jnp.concatenate([jnp.cos(ang), jnp.sin(ang)], axis=-1)), axis=-1)

    w_in0 = w_in[0]
    dup = HEAD_DIM * (np.arange(2 * KV_WIDTH) // LANES) + np.arange(2 * KV_WIDTH) % HEAD_DIM
    w_in_a = jnp.concatenate([w_in0[:, :_OFF_K], w_in0[:, _OFF_K + dup], w_in0[:, _OFF_V + dup],
                              w_in0[:, _OFF_HQ:_OFF_Z]], axis=1).astype(BF16)
    w_z = w_in0[:, _OFF_Z:].astype(BF16)

    q, k, v, hq, lf, hv, hg = _inproj(x2, norm1_w[0], w_in_a, cs, q_norm_w[0], k_norm_w[0],
                                      hgrn_lower_bounds.astype(F32), tm)
    attn = _attention(q, k, v, attn_sinks[0].astype(F32), B, S)
    hgrn = _hgrn(hq, lf, hv, hg, hgrn_norm_w[0], B, S, _pick_tile(S, 512))

    pad = ROUTER_ROWS - N_GROUPS - N_EXPERTS
    w_r = jnp.concatenate([w_router_group[0].T, w_router_expert[0].T, jnp.zeros((pad, D), F32)], axis=0)
    b_r =jnp.concatenate([b_router_group[0], b_router_expert[0], jnp.zeros((pad,), F32)]).astype(F32)
    b_r = jnp.broadcast_to(b_r[:, None], (ROUTER_ROWS, tm))

    x1, hn, route, counts = _merge(x2, attn, hgrn, norm1_w[0], w_z, w_branch_attn[0].astype(BF16),
                                   w_branch_hgrn[0].astype(BF16), w_out[0].astype(BF16), norm2_w[0],
                                   w_r.astype(BF16), b_r, tm)

    pos, items = _routing_tables(route, counts[N_GROUPS:N_GROUPS + N_EXPERTS, 0], T)
    tk = _pick_tile(T, 256)
    xs = _dispatch(pos, hn, tk)
    y = _experts(items, xs, w_gate_experts[0], w_up_experts[0], w_down_experts[0])
    out = _combine(pos, x1, route[2:2 + TOP_K].T, y, tk)
    return out.reshape(B, S, D)
```

```python
import functools

import numpy as np
import jax
import jax.numpy as jnp
from jax import lax
from jax.experimental import pallas as pl
from jax.experimental.pallas import tpu as pltpu
from jax.experimental.pallas import tpu_sc as plsc

F32 = jnp.float32
BF16 = jnp.bfloat16

D_MODEL = 1024
N_Q_HEADS = 8
N_KV_HEADS = 2
GROUP = N_Q_HEADS // N_KV_HEADS
HEAD_DIM = 64
ROT_DIM = HEAD_DIM // 4
ROT_HALF = ROT_DIM // 2
ROPE_THETA = 500000.0
WINDOW = 128
ATTN_WIDTH = N_Q_HEADS * HEAD_DIM
KV_WIDTH = N_KV_HEADS * HEAD_DIM

HGRN_HEADS = 4
HGRN_DIM = 128
HGRN_WIDTH = HGRN_HEADS * HGRN_DIM
CHUNK = 64

N_GROUPS = 4
EXPERTS_PER_GROUP = 8
N_EXPERTS = N_GROUPS * EXPERTS_PER_GROUP
TOP_K = 2
EXPERT_FF = 512
MOE_BLOCK = 128
EXPERT_TILE = 512
NORM_EPS = 1e-6
ROUTER_ROWS = 40
ROUTE_ROWS = 8

LANES = 128
NEG_BIG = -1e30
LOG2_E = 1.4426950408889634

_OFF_Q, _OFF_K, _OFF_V = 0, ATTN_WIDTH, ATTN_WIDTH + KV_WIDTH
_OFF_HQ = ATTN_WIDTH + 2 * KV_WIDTH
_OFF_HF = _OFF_HQ + HGRN_WIDTH
_OFF_HI = _OFF_HF + HGRN_WIDTH
_OFF_HG = _OFF_HI + HGRN_WIDTH
_OFF_Z = _OFF_HG + HGRN_WIDTH
_A_Q, _A_K, _A_V = 0, ATTN_WIDTH, ATTN_WIDTH + 2 * KV_WIDTH
_A_HQ = ATTN_WIDTH + 4 * KV_WIDTH
_A_HF = _A_HQ + HGRN_WIDTH
_A_HI = _A_HF + HGRN_WIDTH
_A_HG = _A_HI + HGRN_WIDTH

VMEM_LIMIT = 56 * 1024 * 1024


def _split3(a):
    hi = a.astype(BF16)
    r1 = a - hi.astype(F32)
    mid = r1.astype(BF16)
    lo = (r1 - mid.astype(F32)).astype(BF16)
    return hi, mid, lo


def _dot(a, b):
    return jnp.dot(a, b, preferred_element_type=F32)


def _dot_nt(a, b):
    return lax.dot_general(a, b, (((1,), (1,)), ((), ())), preferred_element_type=F32)


def _dot_tn(a, b):
    return lax.dot_general(a, b, (((0,), (0,)), ((), ())), preferred_element_type=F32)


def _sigmoid(x):
    return 1.0 / (1.0 + jnp.exp(-x))


def _rms(x, w):
    ms = jnp.mean(x * x, axis=-1, keepdims=True)
    return x * lax.rsqrt(ms + NORM_EPS) * w


def _inproj_kernel(x_ref, n1w_ref, w_ref, cs_ref, rope_e_ref, rope_c0_ref, qw_ref, kw_ref,
                   mq_ref, mk_ref, lbp_ref,
                   q_out, k_out, v_out, hq_out, lf_out, hv_out, hg_out):
    xn = _rms(x_ref[...], n1w_ref[...]).astype(BF16)

    def proj(off, width):
        return _dot(xn, w_ref[:, off:off + width])

    tabs = _dot(cs_ref[...], rope_e_ref[...])
    c_tab = tabs[:, 0:LANES] + rope_c0_ref[...]
    s1_tab = tabs[:, LANES:2 * LANES]
    s2_tab = tabs[:, 2 * LANES:3 * LANES]

    def norm_rope(t, mavg_ref, w_row, scale):
        ms = _dot((t * t).astype(BF16), mavg_ref[...])
        tn = t * lax.rsqrt(ms + NORM_EPS) * w_row
        if scale != 1.0:
            tn = tn * scale
        outs = []
        for j in range(t.shape[1] // LANES):
            c = tn[:, j * LANES:(j + 1) * LANES]
            outs.append(c * c_tab
                        + pltpu.roll(c, LANES - ROT_HALF, 1) * s1_tab
                        + pltpu.roll(c, ROT_HALF, 1) * s2_tab)
        return outs[0] if len(outs) == 1 else jnp.concatenate(outs, axis=1)

    q_out[...] = norm_rope(proj(_A_Q, ATTN_WIDTH), mq_ref, qw_ref[...], HEAD_DIM ** -0.5).astype(BF16)
    k_out[...] = norm_rope(proj(_A_K, 2 * KV_WIDTH), mk_ref, kw_ref[...], 1.0).astype(BF16)
    v_out[...] = proj(_A_V, 2 * KV_WIDTH).astype(BF16)

    hq = proj(_A_HQ, HGRN_WIDTH)
    hq_out[...] = (hq * _sigmoid(hq)).astype(BF16)
    h0 = lbp_ref[0:1, :]
    h1 = lbp_ref[1:2, :]
    hm = jnp.maximum(h0, h1)
    e0 = jnp.exp(h0 - hm)
    e1 = jnp.exp(h1 - hm)
    lb = e0 / (e0 + e1)
    fg = lb + (1.0 - lb) * _sigmoid(proj(_A_HF, HGRN_WIDTH))
    lf_out[...] = jnp.log(fg) * LOG2_E
    hv_out[...] = proj(_A_HI, HGRN_WIDTH).astype(BF16)
    hg = proj(_A_HG, HGRN_WIDTH)
    hg_out[...] = (hg * _sigmoid(hg)).astype(BF16)


def _rope_constants():
    e = np.zeros((2 * ROT_HALF, 3 * LANES), np.float32)
    c0 = np.zeros((1, LANES), np.float32)
    for lane in range(LANES):
        d = lane % HEAD_DIM
        if d < ROT_HALF:
            e[d, lane] = 1.0
            e[ROT_HALF + d, LANES + lane] = -1.0
        elif d < ROT_DIM:
            e[d - ROT_HALF, lane] = 1.0
            e[ROT_HALF + d - ROT_HALF, 2 * LANES + lane] = 1.0
        else:
            c0[0, lane] = 1.0
    return jnp.asarray(np.concatenate([e, e, e], axis=0), BF16), jnp.asarray(c0, F32)


def _head_mean_matrix(width):
    idx = np.arange(width) // HEAD_DIM
    m = (idx[:, None] == idx[None, :]).astype(np.float32) / HEAD_DIM
    return jnp.asarray(m, BF16)


def _inproj(x2, norm1_w, w_in_a, cs, q_norm_w, k_norm_w, lbp, tm):
    T = x2.shape[0]
    rope_e, rope_c0 = _rope_constants()
    qw = jnp.tile(q_norm_w.astype(F32), N_Q_HEADS)[None, :]
    kw = jnp.tile(k_norm_w.astype(F32), 2 * N_KV_HEADS)[None, :]
    mq = _head_mean_matrix(ATTN_WIDTH)
    mk = _head_mean_matrix(2 * KV_WIDTH)
    row = lambda w: pl.BlockSpec((tm, w), lambda i: (i, 0))
    full = lambda a: pl.BlockSpec(a.shape, lambda i: (0,) * a.ndim)
    ins = [x2, norm1_w[None, :], w_in_a, cs, rope_e, rope_c0, qw, kw, mq, mk, lbp]
    in_specs = [row(D_MODEL), full(ins[1]), full(w_in_a), row(cs.shape[1])] + [full(a) for a in ins[4:]]
    outs = [(ATTN_WIDTH, BF16), (2 * KV_WIDTH, BF16), (2 * KV_WIDTH, BF16), (HGRN_WIDTH, BF16),
            (HGRN_WIDTH, F32), (HGRN_WIDTH, BF16), (HGRN_WIDTH, BF16)]
    return pl.pallas_call(
        _inproj_kernel,
        out_shape=[jax.ShapeDtypeStruct((T, w), dt) for w, dt in outs],
        grid=(T // tm,),
        in_specs=in_specs,
        out_specs=[row(w) for w, _ in outs],
        compiler_params=pltpu.CompilerParams(dimension_semantics=("arbitrary",),
                                             vmem_limit_bytes=VMEM_LIMIT),
        name="inproj",
    )(*ins)


ATTN_QBLOCKS = 8


def _attn_kernel(sink_ref, q_ref, kc_ref, kp_ref, vc_ref, vp_ref, half_ref, o_ref):
    n_qblocks = q_ref.shape[0] // WINDOW
    has_prev = pl.program_id(1) > 0
    qi = lax.broadcasted_iota(jnp.int32, (WINDOW, 2 * WINDOW), 0)
    kj = lax.broadcasted_iota(jnp.int32, (WINDOW, 2 * WINDOW), 1)
    in_window = ((kj < WINDOW) & (kj > qi)) | ((kj >= WINDOW) & (kj - WINDOW <= qi))
    first_valid = in_window & ((kj >= WINDOW) | has_prev)
    left = lax.broadcasted_iota(jnp.int32, (WINDOW, LANES), 1) < HEAD_DIM
    half = (half_ref[0], half_ref[1])

    ks, rhs = [], []
    for h in range(N_KV_HEADS):
        cols = slice(h * LANES, (h + 1) * LANES)
        kall = jnp.concatenate([kp_ref[:, cols], kc_ref[:, cols]], axis=0)
        vall = jnp.concatenate([vp_ref[:, cols], vc_ref[:, cols]], axis=0)
        ks.append([kall * hm for hm in half])
        rhs.append([jnp.concatenate([vall * hm, hm], axis=1) for hm in half])

    units = [(j, h, pr, side) for j in range(n_qblocks) for h in range(N_KV_HEADS)
             for pr in range(GROUP // 2) for side in range(2)]
    scores = []
    for j, h, pr, side in units:
        pair = h * (GROUP // 2) + pr
        qp = q_ref[j * WINDOW:(j + 1) * WINDOW, pair * LANES:(pair + 1) * LANES]
        s = _dot_nt(qp, ks[h][side][j * WINDOW:(j + 2) * WINDOW])
        scores.append(jnp.where(first_valid if j == 0 else in_window, s, NEG_BIG))
    probs, sink_terms = [], []
    for (j, h, pr, side), s in zip(units, scores):
        sink = sink_ref[2 * (h * (GROUP // 2) + pr) + side]
        m = jnp.maximum(jnp.max(s, axis=-1, keepdims=True), sink)
        probs.append(jnp.exp(s - m).astype(BF16))
        sink_terms.append(jnp.exp(sink - m))
    acc = []
    for (j, h, pr, side), p in zip(units, probs):
        acc.append(_dot(p, rhs[h][side][j * WINDOW:(j + 2) * WINDOW]))
    for j in range(n_qblocks):
        outs = []
        for u in range(0, len(units), 2):
            if units[u][0] == j:
                both = acc[u] + acc[u + 1]
                den = both[:, LANES:2 * LANES] + jnp.where(left, sink_terms[u], sink_terms[u + 1])
                outs.append(both[:, 0:LANES] / den)
        o_ref[j * WINDOW:(j + 1) * WINDOW, :] = jnp.concatenate(outs, axis=1).astype(BF16)


def _attention(q, k, v, sinks, B, S):
    qblocks = _pick_tile(S // WINDOW, ATTN_QBLOCKS)
    rows = qblocks * WINDOW
    nb = S // rows
    cur = lambda b, n: (b * nb + n, 0)
    prev = lambda b, n: (jnp.maximum((b * nb + n) * qblocks - 1, 0), 0)
    lane_left = np.arange(LANES) < HEAD_DIM
    half = jnp.asarray(np.broadcast_to(np.stack([lane_left, ~lane_left])[:, None, :],
                                       (2, rows + WINDOW, LANES)), BF16)
    return pl.pallas_call(
        _attn_kernel,
        out_shape=jax.ShapeDtypeStruct((B * S, ATTN_WIDTH), BF16),
        grid=(B, nb),
        in_specs=[pl.BlockSpec(memory_space=pltpu.SMEM),
                  pl.BlockSpec((rows, ATTN_WIDTH), cur),
                  pl.BlockSpec((rows, 2 * KV_WIDTH), cur),
                  pl.BlockSpec((WINDOW, 2 * KV_WIDTH), prev),
                  pl.BlockSpec((rows, 2 * KV_WIDTH), cur),
                  pl.BlockSpec((WINDOW, 2 * KV_WIDTH), prev),
                  pl.BlockSpec(half.shape, lambda b, n: (0, 0, 0))],
        out_specs=pl.BlockSpec((rows, ATTN_WIDTH), cur),
        compiler_params=pltpu.CompilerParams(dimension_semantics=("arbitrary", "arbitrary"),
                                             vmem_limit_bytes=VMEM_LIMIT),
        name="swa_attention",
    )(sinks, q, k, k, v, v, half)


_LEVEL_HALVES = (1, 2, 4, 8, 16, 32)


def _hgrn_level_masks():
    t = np.arange(CHUNK)[:, None]
    s = np.arange(CHUNK)[None, :]
    masks = [((t // (2 * h)) == (s // (2 * h))) & ((t & h) != 0) & ((s & h) == 0) for h in _LEVEL_HALVES]
    return jnp.asarray(np.stack(masks), F32)


def _level_reference(b_ref, slot, half):
    if half >= 4:
        span = max(2 * half, 8)
        pieces = [jnp.broadcast_to(b_ref[slot, s + half - 1:s + half, :], (span, HGRN_DIM))
                  for s in range(0, CHUNK, span)]
    else:
        r8 = lax.broadcasted_iota(jnp.int32, (8, HGRN_DIM), 0)
        pieces = [jnp.where(r8 < 4,
                            jnp.broadcast_to(b_ref[slot, s + 1:s + 2, :], (8, HGRN_DIM)),
                            jnp.broadcast_to(b_ref[slot, s + 5:s + 6, :], (8, HGRN_DIM)))
                  for s in range(0, CHUNK, 8)]
    return pieces[0] if len(pieces) == 1 else jnp.concatenate(pieces, axis=0)


def _hgrn_kernel(hq_ref, lf_ref, hv_ref, hg_ref, nw_ref, tri_ref, lm_ref, o_ref, st_ref, b_ref):
    @pl.when(pl.program_id(1) == 0)
    def _():
        st_ref[...] = jnp.zeros_like(st_ref)

    tri2 = tri_ref[...]
    odd = (lax.broadcasted_iota(jnp.int32, (CHUNK, HGRN_DIM), 0) & 1) != 0
    masks = [lm_ref[li] != 0.0 for li in range(len(_LEVEL_HALVES))]
    units = [(c, h) for c in range(hq_ref.shape[0] // CHUNK) for h in range(HGRN_HEADS)]
    sl = lambda c, h: (slice(c * CHUNK, (c + 1) * CHUNK), slice(h * HGRN_DIM, (h + 1) * HGRN_DIM))

    bs = []
    for u, (c, h) in enumerate(units):
        lf2 = lf_ref[sl(c, h)]
        hi = lf2.astype(BF16)
        lo = (lf2 - hi.astype(F32)).astype(BF16)
        b = _dot(tri2, jnp.concatenate([hi, lo], axis=0))
        b_ref[u] = b
        bs.append(b)

    outs, qs, ks, fs = [], [], [], []
    for u, (c, h) in enumerate(units):
        b = bs[u]
        q = hq_ref[sl(c, h)].astype(F32)
        v_bf = hv_ref[sl(c, h)]
        f = jnp.exp2(lf_ref[sl(c, h)])
        k = 1.0 - f
        b_last = b[CHUNK - 1:CHUNK, :]
        st = st_ref[h]
        o = _dot_nt((q * jnp.exp2(b)).astype(BF16), st.astype(BF16))
        k_out = (k * jnp.exp2(b_last - b)).astype(BF16)
        st_ref[h] = st * jnp.exp2(b_last) + _dot_tn(v_bf, k_out)
        outs.append(o + jnp.sum(q * k, axis=-1, keepdims=True) * v_bf.astype(F32))
        qs.append(q)
        ks.append(k)
        fs.append(f)

    accs = [jnp.zeros((CHUNK, CHUNK), F32) for _ in units]
    for li, half in enumerate(_LEVEL_HALVES):
        for u in range(len(units)):
            if half == 1:
                e = jnp.where(odd, fs[u], 1.0)
            else:
                e = jnp.exp2(-jnp.abs(bs[u] - _level_reference(b_ref, u, half)))
            accs[u] = jnp.where(masks[li], _dot_nt((qs[u] * e).astype(BF16), (ks[u] * e).astype(BF16)), accs[u])

    for u, (c, h) in enumerate(units):
        o = outs[u] + _dot(accs[u].astype(BF16), hv_ref[sl(c, h)])
        y = _rms(o, nw_ref[...]) * hg_ref[sl(c, h)].astype(F32)
        o_ref[sl(c, h)] = y.astype(BF16)


def _hgrn(hq, lf, hv, hg, hgrn_norm_w, B, S, tb):
    nt = S // tb
    blk = pl.BlockSpec((tb, HGRN_WIDTH), lambda b, n: (b * nt + n, 0))
    tri = np.tril(np.ones((CHUNK, CHUNK), np.float32))
    tri = jnp.asarray(np.concatenate([tri, tri], axis=1), BF16)
    lm = _hgrn_level_masks()
    return pl.pallas_call(
        _hgrn_kernel,
        out_shape=jax.ShapeDtypeStruct((B * S, HGRN_WIDTH), BF16),
        grid=(B, nt),
        in_specs=[blk, blk, blk, blk,
                  pl.BlockSpec((1, HGRN_DIM), lambda b, n: (0, 0)),
                  pl.BlockSpec(tri.shape, lambda b, n: (0, 0)),
                  pl.BlockSpec(lm.shape, lambda b, n: (0, 0, 0))],
        out_specs=blk,
        scratch_shapes=[pltpu.VMEM((HGRN_HEADS, HGRN_DIM, HGRN_DIM), F32),
                        pltpu.VMEM((tb // CHUNK * HGRN_HEADS, CHUNK, HGRN_DIM), F32)],
        compiler_params=pltpu.CompilerParams(dimension_semantics=("arbitrary", "arbitrary"),
                                             vmem_limit_bytes=VMEM_LIMIT),
        name="hgrn2",
    )(hq, lf, hv, hg, hgrn_norm_w[None, :].astype(F32), tri, lm)


def _merge_kernel(x_ref, attn_ref, hgrn_ref, n1w_ref, wz_ref, wba_ref, wbh_ref, wout_ref, n2w_ref,
                  wr_ref, br_ref, utri_ref, x1_out, hn_out, route_out, count_out, run_ref):
    @pl.when(pl.program_id(0) == 0)
    def _():
        run_ref[...] = jnp.zeros_like(run_ref)

    x = x_ref[...]
    xn = _rms(x, n1w_ref[...]).astype(BF16)
    za = _sigmoid(_dot(xn, wz_ref[:, 0:D_MODEL]))
    zb = _sigmoid(_dot(xn, wz_ref[:, D_MODEL:2 * D_MODEL]))
    mixed = za * _dot(attn_ref[...], wba_ref[...]) + zb * _dot(hgrn_ref[...], wbh_ref[...])
    x1 = x + _dot(mixed.astype(BF16), wout_ref[...])
    x1_out[...] = x1
    hn = _rms(x1, n2w_ref[...])
    hn_out[...] = _pack_bf16_pairs(hn)

    logits = _dot_nt(wr_ref[...], hn.astype(BF16)) + br_ref[...]
    r = lax.broadcasted_iota(jnp.int32, logits.shape, 0).astype(F32)
    far = float(ROUTER_ROWS)
    cmax = lambda a: jnp.max(a, axis=0, keepdims=True)
    cmin = lambda a: jnp.min(a, axis=0, keepdims=True)
    csum = lambda a: jnp.sum(a, axis=0, keepdims=True)

    lg = jnp.where(r < N_GROUPS, logits, NEG_BIG)
    mg = cmax(lg)
    gsel = cmin(jnp.where(lg == mg, r, far))
    pgsel = 1.0 / csum(jnp.exp(lg - mg))

    lo = N_GROUPS + EXPERTS_PER_GROUP * gsel
    le = jnp.where((r >= lo) & (r < lo + EXPERTS_PER_GROUP), logits, NEG_BIG)
    m1 = cmax(le)
    i1 = cmin(jnp.where(le == m1, r, far))
    se = csum(jnp.exp(le - m1))
    le2 = jnp.where(r == i1, NEG_BIG, le)
    m2 = cmax(le2)
    i2 = cmin(jnp.where(le2 == m2, r, far))
    top0 = 1.0 / se
    top1 = jnp.exp(m2 - m1) / se
    tsum = top0 + top1
    w0 = pgsel * top0 / tsum
    w1 = pgsel * top1 / tsum

    sel1 = r == i1
    sel2 = r == i2
    onehot = (sel1 | sel2).astype(BF16)
    before = _dot(onehot, utri_ref[...]) + run_ref[...]
    r0 = csum(jnp.where(sel1, before, 0.0))
    r1 = csum(jnp.where(sel2, before, 0.0))
    run_new = run_ref[...] + _dot(onehot, jnp.ones(utri_ref.shape, BF16))
    run_ref[...] = run_new
    count_out[...] = run_new

    row8 = lax.broadcasted_iota(jnp.int32, route_out.shape, 0)
    vals = (i1 - N_GROUPS, i2 - N_GROUPS, w0, w1, r0, r1)
    route = jnp.zeros(route_out.shape, F32)
    for j, val in enumerate(vals):
        route = jnp.where(row8 == j, val, route)
    route_out[...] = route


def _merge(x2, attn, hgrn, norm1_w, w_z, w_ba, w_bh, w_out, norm2_w, w_r, br, tm):
    T = x2.shape[0]
    row = lambda w: pl.BlockSpec((tm, w), lambda i: (i, 0))
    full = lambda a: pl.BlockSpec(a.shape, lambda i: (0,) * a.ndim)
    utri = jnp.asarray(np.triu(np.ones((tm, tm), np.float32), 1), BF16)
    ins = [x2, attn, hgrn, norm1_w[None, :], w_z, w_ba, w_bh, w_out, norm2_w[None, :], w_r, br, utri]
    in_specs = [row(D_MODEL), row(ATTN_WIDTH), row(HGRN_WIDTH)] + [full(a) for a in ins[3:]]
    return pl.pallas_call(
        _merge_kernel,
        out_shape=[jax.ShapeDtypeStruct((T, D_MODEL), F32), jax.ShapeDtypeStruct((T, HALF), jnp.uint32),
                   jax.ShapeDtypeStruct((ROUTE_ROWS, T), F32), jax.ShapeDtypeStruct((ROUTER_ROWS, tm), F32)],
        grid=(T // tm,),
        in_specs=in_specs,
        out_specs=[row(D_MODEL), row(HALF), pl.BlockSpec((ROUTE_ROWS, tm), lambda i: (0, i)),
                   pl.BlockSpec((ROUTER_ROWS, tm), lambda i: (0, 0))],
        scratch_shapes=[pltpu.VMEM((ROUTER_ROWS, tm), F32)],
        compiler_params=pltpu.CompilerParams(dimension_semantics=("arbitrary",),
                                             vmem_limit_bytes=VMEM_LIMIT),
        name="merge_router",
    )(*ins)


HALF = D_MODEL // 2


def _pack_bf16_pairs(x):
    bits = pltpu.bitcast(x.astype(BF16).astype(F32), jnp.uint32)
    return (bits[:, :HALF] >> 16) | bits[:, HALF:]


def _unpack_bf16_pairs(words):
    lo = pltpu.bitcast(words << 16, F32)
    hi = pltpu.bitcast(words & jnp.uint32(0xFFFF0000), F32)
    return lo, hi


def _issue_row_gather(idx_ref, n, src_hbm, dst_ref, sem):
    for r in range(n):
        pltpu.make_async_copy(src_hbm.at[pl.ds(idx_ref[0, 0, r], 1), :],
                              dst_ref.at[pl.ds(r, 1), :], sem).start()


def _wait_row_gather(n, src_hbm, dst_ref, sem):
    pltpu.make_async_copy(src_hbm.at[pl.ds(0, n), :], dst_ref.at[pl.ds(0, n), :], sem).wait()


def _tile_positions(pos, tk):
    nt = pos.shape[0] // tk
    return pos.reshape(nt, tk, TOP_K).transpose(0, 2, 1).reshape(nt, 1, TOP_K * tk)


def _dispatch_kernel(pos_ref, hn_ref, xs_hbm, sem):
    tk = hn_ref.shape[0]
    for j in range(TOP_K * tk):
        pltpu.make_async_copy(hn_ref.at[pl.ds(j % tk, 1), :],
                              xs_hbm.at[pl.ds(pos_ref[0, 0, j], 1), :], sem).start()
    for _ in range(TOP_K):
        pltpu.make_async_copy(hn_ref, xs_hbm.at[pl.ds(0, tk), :], sem).wait()


def _dispatch(pos, hn, tk):
    T = hn.shape[0]
    return pl.pallas_call(
        _dispatch_kernel,
        out_shape=jax.ShapeDtypeStruct((TOP_K * T, HALF), jnp.uint32),
        grid=(T // tk,),
        in_specs=[pl.BlockSpec((1, 1, TOP_K * tk), lambda i: (i, 0, 0), memory_space=pltpu.SMEM),
                  pl.BlockSpec((tk, HALF), lambda i: (i, 0))],
        out_specs=pl.BlockSpec(memory_space=pl.ANY),
        scratch_shapes=[pltpu.SemaphoreType.DMA],
        compiler_params=pltpu.CompilerParams(dimension_semantics=("arbitrary",),
                                             vmem_limit_bytes=VMEM_LIMIT),
        name="moe_dispatch",
    )(_tile_positions(pos, tk), hn)


def _expert_kernel(tile_ref, exp_ref, lo_ref, hi_ref, x_ref, wg_ref, wu_ref, wd_ref, y_ref, wg_s, wu_s, wd_s):
    w = pl.program_id(0)
    lo = lo_ref[w]
    hi = hi_ref[w]
    prev = jnp.maximum(w - 1, 0)
    whole = (lo == 0) & (hi == EXPERT_TILE)

    @pl.when((w == 0) | (exp_ref[w] != exp_ref[prev]))
    def _():
        wg_s[...] = wg_ref[0].astype(BF16)
        wu_s[...] = wu_ref[0].astype(BF16)
        wd_s[...] = wd_ref[0].astype(BF16)

    def ffn(words):
        lo, hi = _unpack_bf16_pairs(words)
        lo = lo.astype(BF16)
        hi = hi.astype(BF16)
        gate = _dot(lo, wg_s[0:HALF, :]) + _dot(hi, wg_s[HALF:, :])
        up = _dot(lo, wu_s[0:HALF, :]) + _dot(hi, wu_s[HALF:, :])
        return _pack_bf16_pairs(_dot((gate * _sigmoid(gate) * up).astype(BF16), wd_s[...]))

    @pl.when(whole)
    def _():
        y_ref[...] = ffn(x_ref[...])

    for j in range(EXPERT_TILE // MOE_BLOCK):
        r0 = j * MOE_BLOCK
        rows = slice(r0, r0 + MOE_BLOCK)

        @pl.when(jnp.logical_not(whole) & (lo < r0 + MOE_BLOCK) & (hi > r0))
        def _():
            y = ffn(x_ref[rows, :])
            rowi = lax.broadcasted_iota(jnp.int32, y.shape, 0) + r0
            mine = (rowi >= lo) & (rowi < hi)

            @pl.when(lo <= r0)
            def _():
                y_ref[rows, :] = jnp.where(mine, y, jnp.uint32(0))

            @pl.when(lo > r0)
            def _():
                y_ref[rows, :] = jnp.where(mine, y, y_ref[rows, :])


def _experts(items, xs, w_gate, w_up, w_down):
    tile, exp, lo, hi = items
    wspec = lambda shape: pl.BlockSpec((1,) + shape, lambda w, t, e, l, h: (e[w], 0, 0))
    xspec = pl.BlockSpec((EXPERT_TILE, HALF), lambda w, t, e, l, h: (t[w], 0))
    return pl.pallas_call(
        _expert_kernel,
        out_shape=jax.ShapeDtypeStruct(xs.shape, jnp.uint32),
        grid_spec=pltpu.PrefetchScalarGridSpec(
            num_scalar_prefetch=4,
            grid=(tile.shape[0],),
            in_specs=[xspec, wspec((D_MODEL, EXPERT_FF)), wspec((D_MODEL, EXPERT_FF)),
                      wspec((EXPERT_FF, D_MODEL))],
            out_specs=xspec,
            scratch_shapes=[pltpu.VMEM((D_MODEL, EXPERT_FF), BF16),
                            pltpu.VMEM((D_MODEL, EXPERT_FF), BF16),
                            pltpu.VMEM((EXPERT_FF, D_MODEL), BF16)]),
        compiler_params=pltpu.CompilerParams(dimension_semantics=("arbitrary",),
                                             vmem_limit_bytes=VMEM_LIMIT),
        name="moe_experts",
    )(tile, exp, lo, hi, xs, w_gate, w_up, w_down)


SC_CORES = 2
SC_SUBCORES = 16
SC_CHUNK = 64


def _sc_gather_rows(table, idx):
    n = idx.shape[0]
    width = table.shape[1]
    per_worker = n // (SC_CORES * SC_SUBCORES)
    n_chunks = per_worker // SC_CHUNK
    assert per_worker * SC_CORES * SC_SUBCORES == n and n_chunks * SC_CHUNK == per_worker and n_chunks % 2 == 0
    mesh = plsc.VectorSubcoreMesh(core_axis_name="c", subcore_axis_name="s",
                                  num_cores=SC_CORES, num_subcores=SC_SUBCORES)

    @functools.partial(
        pl.kernel, mesh=mesh, name="moe_row_gather",
        out_type=jax.ShapeDtypeStruct((n, width), table.dtype),
        scratch_types=[pltpu.VMEM((per_worker,), jnp.int32),
                       pltpu.VMEM((2, SC_CHUNK, width), table.dtype),
                       pltpu.SemaphoreType.DMA((2,))])
    def gather_kernel(table_hbm, idx_hbm, out_hbm, idx_v, rows_v, sem):
        base = (lax.axis_index("s") * SC_CORES + lax.axis_index("c")) * per_worker
        pltpu.sync_copy(idx_hbm.at[pl.ds(base, per_worker)], idx_v)

        def gather(c, b):
            off = pl.multiple_of(c * SC_CHUNK, SC_CHUNK)
            return pltpu.make_async_copy(table_hbm.at[idx_v.at[pl.ds(off, SC_CHUNK)]], rows_v.at[b], sem.at[b])

        gather(0, 0).start()

        @pl.loop(0, n_chunks, step=2)
        def _(c0):
            for b in range(2):
                c = c0 + b

                @pl.when(c + 1 < n_chunks)
                def _():
                    gather(c + 1, 1 - b).start()

                gather(c, b).wait()
                off = pl.multiple_of(c * SC_CHUNK, SC_CHUNK)
                pltpu.sync_copy(rows_v.at[b], out_hbm.at[pl.ds(base + off, SC_CHUNK)])

    return gather_kernel(table, idx)


def _combine_kernel(x1_ref, w_ref, y0_ref, y1_ref, o_ref):
    w0 = w_ref[:, 0:1]
    w1 = w_ref[:, 1:2]
    lo0, hi0 = _unpack_bf16_pairs(y0_ref[...])
    lo1, hi1 = _unpack_bf16_pairs(y1_ref[...])
    o_ref[:, 0:HALF] = x1_ref[:, 0:HALF] + w0 * lo0 + w1 * lo1
    o_ref[:, HALF:] = x1_ref[:, HALF:] + w0 * hi0 + w1 * hi1


def _combine(pos, x1, gate_w, y, tk):
    T = x1.shape[0]
    nt = T // tk
    ysel = _sc_gather_rows(y, pos.T.reshape(-1))
    return pl.pallas_call(
        _combine_kernel,
        out_shape=jax.ShapeDtypeStruct((T, D_MODEL), F32),
        grid=(nt,),
        in_specs=[pl.BlockSpec((tk, D_MODEL), lambda i: (i, 0)),
                  pl.BlockSpec((tk, TOP_K), lambda i: (i, 0)),
                  pl.BlockSpec((tk, HALF), lambda i: (i, 0)),
                  pl.BlockSpec((tk, HALF), lambda i: (i + nt, 0))],
        out_specs=pl.BlockSpec((tk, D_MODEL), lambda i: (i, 0)),
        compiler_params=pltpu.CompilerParams(dimension_semantics=("arbitrary",),
                                             vmem_limit_bytes=VMEM_LIMIT),
        name="moe_combine",
    )(x1, gate_w, ysel, ysel)


def _routing_tables(route, counts, T):
    e = route[0:TOP_K].astype(jnp.int32)
    rank = route[4:4 + TOP_K].astype(jnp.int32)
    counts = counts.astype(jnp.int32)
    starts = jnp.cumsum(counts) - counts
    ids = jnp.arange(N_EXPERTS, dtype=jnp.int32)
    pos = rank + jnp.sum(jnp.where(e[:, :, None] == ids, starts, 0), axis=-1)
    n_rows = TOP_K * T
    cuts = jnp.sort(jnp.concatenate([jnp.arange(0, n_rows, EXPERT_TILE, dtype=jnp.int32), starts[1:]]))
    ends = jnp.concatenate([cuts[1:], jnp.full((1,), n_rows, jnp.int32)])
    tile = jnp.minimum(cuts // EXPERT_TILE, n_rows // EXPERT_TILE - 1)
    exp = jnp.clip(jnp.sum(starts[None, :] <= cuts[:, None], axis=1) - 1, 0, N_EXPERTS - 1).astype(jnp.int32)
    lo = cuts - tile * EXPERT_TILE
    hi = ends - tile * EXPERT_TILE
    return pos.T, (tile, exp, lo, hi)


def _pick_tile(n, pref):
    t = pref
    while n % t:
        t //= 2
    return t


def kernel(x, positions, norm1_w, w_in, q_norm_w, k_norm_w, attn_sinks, hgrn_lower_bounds, hgrn_norm_w,
           w_branch_attn, w_branch_hgrn, w_out, norm2_w, w_router_group, b_router_group, w_router_expert,
           b_router_expert, w_gate_experts, w_up_experts, w_down_experts):
    B, S, D = x.shape
    T = B * S
    x2 = x.reshape(T, D)
    tm = _pick_tile(T, 512)

    inv_freq = ROPE_THETA ** (-jnp.arange(0, ROT_DIM, 2, dtype=F32) / ROT_DIM)
    ang = positions.astype(F32).reshape(T, 1) * inv_freq[None, :]
    cs = jnp.concatenate(_split3(jnp.concatenate([jnp.cos(ang), jnp.sin(ang)], axis=-1)), axis=-1)

    w_in0 = w_in[0]
    dup = HEAD_DIM * (np.arange(2 * KV_WIDTH) // LANES) + np.arange(2 * KV_WIDTH) % HEAD_DIM
    w_in_a = jnp.concatenate([w_in0[:, :_OFF_K], w_in0[:, _OFF_K + dup], w_in0[:, _OFF_V + dup],
                              w_in0[:, _OFF_HQ:_OFF_Z]], axis=1).astype(BF16)
    w_z = w_in0[:, _OFF_Z:].astype(BF16)

    q, k, v, hq, lf, hv, hg = _inproj(x2, norm1_w[0], w_in_a, cs, q_norm_w[0], k_norm_w[0],
                                      hgrn_lower_bounds.astype(F32), tm)
    attn = _attention(q, k, v, attn_sinks[0].astype(F32), B, S)
    hgrn = _hgrn(hq, lf, hv, hg, hgrn_norm_w[0], B, S, _pick_tile(S, 512))

    pad = ROUTER_ROWS - N_GROUPS - N_EXPERTS
    w_r = jnp.concatenate([w_router_group[0].T, w_router_expert[0].T, jnp.zeros((pad, D), F32)], axis=0)
    b_r =jnp.concatenate([b_router_group[0], b_router_expert[0], jnp.zeros((pad,), F32)]).astype(F32)
    b_r = jnp.broadcast_to(b_r[:, None], (ROUTER_ROWS, tm))

    x1, hn, route, counts = _merge(x2, attn, hgrn, norm1_w[0], w_z, w_branch_attn[0].astype(BF16),
                                   w_branch_hgrn[0].astype(BF16), w_out[0].astype(BF16), norm2_w[0],
                                   w_r.astype(BF16), b_r, tm)

    pos, items = _routing_tables(route, counts[N_GROUPS:N_GROUPS + N_EXPERTS, 0], T)
    tk = _pick_tile(T, 256)
    xs = _dispatch(pos, hn, tk)
    y = _experts(items, xs, w_gate_experts[0], w_up_experts[0], w_down_experts[0])
    out = _combine(pos, x1, route[2:2 + TOP_K].T, y, tk)
    return out.reshape(B, S, D)
```

```python
import functools

import numpy as np
import jax
import jax.numpy as jnp
from jax import lax
from jax.experimental import pallas as pl
from jax.experimental.pallas import tpu as pltpu
from jax.experimental.pallas import tpu_sc as plsc

F32 = jnp.float32
BF16 = jnp.bfloat16

D_MODEL = 1024
N_Q_HEADS = 8
N_KV_HEADS = 2
GROUP = N_Q_HEADS // N_KV_HEADS
HEAD_DIM = 64
ROT_DIM = HEAD_DIM // 4
ROT_HALF = ROT_DIM // 2
ROPE_THETA = 500000.0
WINDOW = 128
ATTN_WIDTH = N_Q_HEADS * HEAD_DIM
KV_WIDTH = N_KV_HEADS * HEAD_DIM

HGRN_HEADS = 4
HGRN_DIM = 128
HGRN_WIDTH = HGRN_HEADS * HGRN_DIM
CHUNK = 64

N_GROUPS = 4
EXPERTS_PER_GROUP = 8
N_EXPERTS = N_GROUPS * EXPERTS_PER_GROUP
TOP_K = 2
EXPERT_FF = 512
MOE_BLOCK = 128
EXPERT_TILE = 512
NORM_EPS = 1e-6
ROUTER_ROWS = 40
ROUTE_ROWS = 8

LANES = 128
NEG_BIG = -1e30
LOG2_E = 1.4426950408889634

_OFF_Q, _OFF_K, _OFF_V = 0, ATTN_WIDTH, ATTN_WIDTH + KV_WIDTH
_OFF_HQ = ATTN_WIDTH + 2 * KV_WIDTH
_OFF_HF = _OFF_HQ + HGRN_WIDTH
_OFF_HI = _OFF_HF + HGRN_WIDTH
_OFF_HG = _OFF_HI + HGRN_WIDTH
_OFF_Z = _OFF_HG + HGRN_WIDTH
_A_Q, _A_K, _A_V = 0, ATTN_WIDTH, ATTN_WIDTH + 2 * KV_WIDTH
_A_HQ = ATTN_WIDTH + 4 * KV_WIDTH
_A_HF = _A_HQ + HGRN_WIDTH
_A_HI = _A_HF + HGRN_WIDTH
_A_HG = _A_HI + HGRN_WIDTH

VMEM_LIMIT = 56 * 1024 * 1024


def _split3(a):
    hi = a.astype(BF16)
    r1 = a - hi.astype(F32)
    mid = r1.astype(BF16)
    lo = (r1 - mid.astype(F32)).astype(BF16)
    return hi, mid, lo


def _dot(a, b):
    return jnp.dot(a, b, preferred_element_type=F32)


def _dot_nt(a, b):
    return lax.dot_general(a, b, (((1,), (1,)), ((), ())), preferred_element_type=F32)


def _dot_tn(a, b):
    return lax.dot_general(a, b, (((0,), (0,)), ((), ())), preferred_element_type=F32)


def _sigmoid(x):
    return 1.0 / (1.0 + jnp.exp(-x))


def _rms(x, w):
    ms = jnp.mean(x * x, axis=-1, keepdims=True)
    return x * lax.rsqrt(ms + NORM_EPS) * w


def _inproj_kernel(x_ref, n1w_ref, w_ref, cs_ref, rope_e_ref, rope_c0_ref, qw_ref, kw_ref,
                   mq_ref, mk_ref, lbp_ref,
                   q_out, k_out, v_out, hq_out, lf_out, hv_out, hg_out):
    xn = _rms(x_ref[...], n1w_ref[...]).astype(BF16)

    def proj(off, width):
        return _dot(xn, w_ref[:, off:off + width])

    tabs = _dot(cs_ref[...], rope_e_ref[...])
    c_tab = tabs[:, 0:LANES] + rope_c0_ref[...]
    s1_tab = tabs[:, LANES:2 * LANES]
    s2_tab = tabs[:, 2 * LANES:3 * LANES]

    def norm_rope(t, mavg_ref, w_row, scale):
        ms = _dot((t * t).astype(BF16), mavg_ref[...])
        tn = t * lax.rsqrt(ms + NORM_EPS) * w_row
        if scale != 1.0:
            tn = tn * scale
        outs = []
        for j in range(t.shape[1] // LANES):
            c = tn[:, j * LANES:(j + 1) * LANES]
            outs.append(c * c_tab
                        + pltpu.roll(c, LANES - ROT_HALF, 1) * s1_tab
                        + pltpu.roll(c, ROT_HALF, 1) * s2_tab)
        return outs[0] if len(outs) == 1 else jnp.concatenate(outs, axis=1)

    q_out[...] = norm_rope(proj(_A_Q, ATTN_WIDTH), mq_ref, qw_ref[...], HEAD_DIM ** -0.5).astype(BF16)
    k_out[...] = norm_rope(proj(_A_K, 2 * KV_WIDTH), mk_ref, kw_ref[...], 1.0).astype(BF16)
    v_out[...] = proj(_A_V, 2 * KV_WIDTH).astype(BF16)

    hq = proj(_A_HQ, HGRN_WIDTH)
    hq_out[...] = (hq * _sigmoid(hq)).astype(BF16)
    h0 = lbp_ref[0:1, :]
    h1 = lbp_ref[1:2, :]
    hm = jnp.maximum(h0, h1)
    e0 = jnp.exp(h0 - hm)
    e1 = jnp.exp(h1 - hm)
    lb = e0 / (e0 + e1)
    fg = lb + (1.0 - lb) * _sigmoid(proj(_A_HF, HGRN_WIDTH))
    lf_out[...] = jnp.log(fg) * LOG2_E
    hv_out[...] = proj(_A_HI, HGRN_WIDTH).astype(BF16)
    hg = proj(_A_HG, HGRN_WIDTH)
    hg_out[...] = (hg * _sigmoid(hg)).astype(BF16)


def _rope_constants():
    e = np.zeros((2 * ROT_HALF, 3 * LANES), np.float32)
    c0 = np.zeros((1, LANES), np.float32)
    for lane in range(LANES):
        d = lane % HEAD_DIM
        if d < ROT_HALF:
            e[d, lane] = 1.0
            e[ROT_HALF + d, LANES + lane] = -1.0
        elif d < ROT_DIM:
            e[d - ROT_HALF, lane] = 1.0
            e[ROT_HALF + d - ROT_HALF, 2 * LANES + lane] = 1.0
        else:
            c0[0, lane] = 1.0
    return jnp.asarray(np.concatenate([e, e, e], axis=0), BF16), jnp.asarray(c0, F32)


def _head_mean_matrix(width):
    idx = np.arange(width) // HEAD_DIM
    m = (idx[:, None] == idx[None, :]).astype(np.float32) / HEAD_DIM
    return jnp.asarray(m, BF16)


def _inproj(x2, norm1_w, w_in_a, cs, q_norm_w, k_norm_w, lbp, tm):
    T = x2.shape[0]
    rope_e, rope_c0 = _rope_constants()
    qw = jnp.tile(q_norm_w.astype(F32), N_Q_HEADS)[None, :]
    kw = jnp.tile(k_norm_w.astype(F32), 2 * N_KV_HEADS)[None, :]
    mq = _head_mean_matrix(ATTN_WIDTH)
    mk = _head_mean_matrix(2 * KV_WIDTH)
    row = lambda w: pl.BlockSpec((tm, w), lambda i: (i, 0))
    full = lambda a: pl.BlockSpec(a.shape, lambda i: (0,) * a.ndim)
    ins = [x2, norm1_w[None, :], w_in_a, cs, rope_e, rope_c0, qw, kw, mq, mk, lbp]
    in_specs = [row(D_MODEL), full(ins[1]), full(w_in_a), row(cs.shape[1])] + [full(a) for a in ins[4:]]
    outs = [(ATTN_WIDTH, BF16), (2 * KV_WIDTH, BF16), (2 * KV_WIDTH, BF16), (HGRN_WIDTH, BF16),
            (HGRN_WIDTH, F32), (HGRN_WIDTH, BF16), (HGRN_WIDTH, BF16)]
    return pl.pallas_call(
        _inproj_kernel,
        out_shape=[jax.ShapeDtypeStruct((T, w), dt) for w, dt in outs],
        grid=(T // tm,),
        in_specs=in_specs,
        out_specs=[row(w) for w, _ in outs],
        compiler_params=pltpu.CompilerParams(dimension_semantics=("arbitrary",),
                                             vmem_limit_bytes=VMEM_LIMIT),
        name="inproj",
    )(*ins)


ATTN_QBLOCKS = 8


def _attn_kernel(sink_ref, q_ref, kc_ref, kp_ref, vc_ref, vp_ref, half_ref, o_ref):
    n_qblocks = q_ref.shape[0] // WINDOW
    has_prev = pl.program_id(1) > 0
    qi = lax.broadcasted_iota(jnp.int32, (WINDOW, 2 * WINDOW), 0)
    kj = lax.broadcasted_iota(jnp.int32, (WINDOW, 2 * WINDOW), 1)
    in_window = ((kj < WINDOW) & (kj > qi)) | ((kj >= WINDOW) & (kj - WINDOW <= qi))
    first_valid = in_window & ((kj >= WINDOW) | has_prev)
    left = lax.broadcasted_iota(jnp.int32, (WINDOW, LANES), 1) < HEAD_DIM
    half = (half_ref[0], half_ref[1])

    ks, rhs = [], []
    for h in range(N_KV_HEADS):
        cols = slice(h * LANES, (h + 1) * LANES)
        kall = jnp.concatenate([kp_ref[:, cols], kc_ref[:, cols]], axis=0)
        vall = jnp.concatenate([vp_ref[:, cols], vc_ref[:, cols]], axis=0)
        ks.append([kall * hm for hm in half])
        rhs.append([jnp.concatenate([vall * hm, hm], axis=1) for hm in half])

    units = [(j, h, pr, side) for j in range(n_qblocks) for h in range(N_KV_HEADS)
             for pr in range(GROUP // 2) for side in range(2)]
    scores = []
    for j, h, pr, side in units:
        pair = h * (GROUP // 2) + pr
        qp = q_ref[j * WINDOW:(j + 1) * WINDOW, pair * LANES:(pair + 1) * LANES]
        s = _dot_nt(qp, ks[h][side][j * WINDOW:(j + 2) * WINDOW])
        scores.append(jnp.where(first_valid if j == 0 else in_window, s, NEG_BIG))
    probs, sink_terms = [], []
    for (j, h, pr, side), s in zip(units, scores):
        sink = sink_ref[2 * (h * (GROUP // 2) + pr) + side]
        m = jnp.maximum(jnp.max(s, axis=-1, keepdims=True), sink)
        probs.append(jnp.exp(s - m).astype(BF16))
        sink_terms.append(jnp.exp(sink - m))
    acc = []
    for (j, h, pr, side), p in zip(units, probs):
        acc.append(_dot(p, rhs[h][side][j * WINDOW:(j + 2) * WINDOW]))
    for j in range(n_qblocks):
        outs = []
        for u in range(0, len(units), 2):
            if units[u][0] == j:
                both = acc[u] + acc[u + 1]
                den = both[:, LANES:2 * LANES] + jnp.where(left, sink_terms[u], sink_terms[u + 1])
                outs.append(both[:, 0:LANES] / den)
        o_ref[j * WINDOW:(j + 1) * WINDOW, :] = jnp.concatenate(outs, axis=1).astype(BF16)


def _attention(q, k, v, sinks, B, S):
    qblocks = _pick_tile(S // WINDOW, ATTN_QBLOCKS)
    rows = qblocks * WINDOW
    nb = S // rows
    cur = lambda b, n: (b * nb + n, 0)
    prev = lambda b, n: (jnp.maximum((b * nb + n) * qblocks - 1, 0), 0)
    lane_left = np.arange(LANES) < HEAD_DIM
    half = jnp.asarray(np.broadcast_to(np.stack([lane_left, ~lane_left])[:, None, :],
                                       (2, rows + WINDOW, LANES)), BF16)
    return pl.pallas_call(
        _attn_kernel,
        out_shape=jax.ShapeDtypeStruct((B * S, ATTN_WIDTH), BF16),
        grid=(B, nb),
        in_specs=[pl.BlockSpec(memory_space=pltpu.SMEM),
                  pl.BlockSpec((rows, ATTN_WIDTH), cur),
                  pl.BlockSpec((rows, 2 * KV_WIDTH), cur),
                  pl.BlockSpec((WINDOW, 2 * KV_WIDTH), prev),
                  pl.BlockSpec((rows, 2 * KV_WIDTH), cur),
                  pl.BlockSpec((WINDOW, 2 * KV_WIDTH), prev),
                  pl.BlockSpec(half.shape, lambda b, n: (0, 0, 0))],
        out_specs=pl.BlockSpec((rows, ATTN_WIDTH), cur),
        compiler_params=pltpu.CompilerParams(dimension_semantics=("arbitrary", "arbitrary"),
                                             vmem_limit_bytes=VMEM_LIMIT),
        name="swa_attention",
    )(sinks, q, k, k, v, v, half)


_LEVEL_HALVES = (1, 2, 4, 8, 16, 32)


def _hgrn_level_masks():
    t = np.arange(CHUNK)[:, None]
    s = np.arange(CHUNK)[None, :]
    masks = [((t // (2 * h)) == (s // (2 * h))) & ((t & h) != 0) & ((s & h) == 0) for h in _LEVEL_HALVES]
    return jnp.asarray(np.stack(masks), F32)


def _level_reference(b_ref, slot, half):
    if half >= 4:
        span = max(2 * half, 8)
        pieces = [jnp.broadcast_to(b_ref[slot, s + half - 1:s + half, :], (span, HGRN_DIM))
                  for s in range(0, CHUNK, span)]
    else:
        r8 = lax.broadcasted_iota(jnp.int32, (8, HGRN_DIM), 0)
        pieces = [jnp.where(r8 < 4,
                            jnp.broadcast_to(b_ref[slot, s + 1:s + 2, :], (8, HGRN_DIM)),
                            jnp.broadcast_to(b_ref[slot, s + 5:s + 6, :], (8, HGRN_DIM)))
                  for s in range(0, CHUNK, 8)]
    return pieces[0] if len(pieces) == 1 else jnp.concatenate(pieces, axis=0)


def _hgrn_kernel(hq_ref, lf_ref, hv_ref, hg_ref, nw_ref, tri_ref, lm_ref, o_ref, st_ref, b_ref):
    @pl.when(pl.program_id(1) == 0)
    def _():
        st_ref[...] = jnp.zeros_like(st_ref)

    tri2 = tri_ref[...]
    odd = (lax.broadcasted_iota(jnp.int32, (CHUNK, HGRN_DIM), 0) & 1) != 0
    masks = [lm_ref[li] != 0.0 for li in range(len(_LEVEL_HALVES))]
    units = [(c, h) for c in range(hq_ref.shape[0] // CHUNK) for h in range(HGRN_HEADS)]
    sl = lambda c, h: (slice(c * CHUNK, (c + 1) * CHUNK), slice(h * HGRN_DIM, (h + 1) * HGRN_DIM))

    bs = []
    for u, (c, h) in enumerate(units):
        lf2 = lf_ref[sl(c, h)]
        hi = lf2.astype(BF16)
        lo = (lf2 - hi.astype(F32)).astype(BF16)
        b = _dot(tri2, jnp.concatenate([hi, lo], axis=0))
        b_ref[u] = b
        bs.append(b)

    outs, qs, ks, fs = [], [], [], []
    for u, (c, h) in enumerate(units):
        b = bs[u]
        q = hq_ref[sl(c, h)].astype(F32)
        v_bf = hv_ref[sl(c, h)]
        f = jnp.exp2(lf_ref[sl(c, h)])
        k = 1.0 - f
        b_last = b[CHUNK - 1:CHUNK, :]
        st = st_ref[h]
        o = _dot_nt((q * jnp.exp2(b)).astype(BF16), st.astype(BF16))
        k_out = (k * jnp.exp2(b_last - b)).astype(BF16)
        st_ref[h] = st * jnp.exp2(b_last) + _dot_tn(v_bf, k_out)
        outs.append(o + jnp.sum(q * k, axis=-1, keepdims=True) * v_bf.astype(F32))
        qs.append(q)
        ks.append(k)
        fs.append(f)

    accs = [jnp.zeros((CHUNK, CHUNK), F32) for _ in units]
    for li, half in enumerate(_LEVEL_HALVES):
        for u in range(len(units)):
            if half == 1:
                e = jnp.where(odd, fs[u], 1.0)
            else:
                e = jnp.exp2(-jnp.abs(bs[u] - _level_reference(b_ref, u, half)))
            accs[u] = jnp.where(masks[li], _dot_nt((qs[u] * e).astype(BF16), (ks[u] * e).astype(BF16)), accs[u])

    for u, (c, h) in enumerate(units):
        o = outs[u] + _dot(accs[u].astype(BF16), hv_ref[sl(c, h)])
        y = _rms(o, nw_ref[...]) * hg_ref[sl(c, h)].astype(F32)
        o_ref[sl(c, h)] = y.astype(BF16)


def _hgrn(hq, lf, hv, hg, hgrn_norm_w, B, S, tb):
    nt = S // tb
    blk = pl.BlockSpec((tb, HGRN_WIDTH), lambda b, n: (b * nt + n, 0))
    tri = np.tril(np.ones((CHUNK, CHUNK), np.float32))
    tri = jnp.asarray(np.concatenate([tri, tri], axis=1), BF16)
    lm = _hgrn_level_masks()
    return pl.pallas_call(
        _hgrn_kernel,
        out_shape=jax.ShapeDtypeStruct((B * S, HGRN_WIDTH), BF16),
        grid=(B, nt),
        in_specs=[blk, blk, blk, blk,
                  pl.BlockSpec((1, HGRN_DIM), lambda b, n: (0, 0)),
                  pl.BlockSpec(tri.shape, lambda b, n: (0, 0)),
                  pl.BlockSpec(lm.shape, lambda b, n: (0, 0, 0))],
        out_specs=blk,
        scratch_shapes=[pltpu.VMEM((HGRN_HEADS, HGRN_DIM, HGRN_DIM), F32),
                        pltpu.VMEM((tb // CHUNK * HGRN_HEADS, CHUNK, HGRN_DIM), F32)],
        compiler_params=pltpu.CompilerParams(dimension_semantics=("arbitrary", "arbitrary"),
                                             vmem_limit_bytes=VMEM_LIMIT),
        name="hgrn2",
    )(hq, lf, hv, hg, hgrn_norm_w[None, :].astype(F32), tri, lm)


def _merge_kernel(x_ref, attn_ref, hgrn_ref, n1w_ref, wz_ref, wba_ref, wbh_ref, wout_ref, n2w_ref,
                  wr_ref, br_ref, utri_ref, x1_out, hn_out, route_out, count_out, run_ref):
    @pl.when(pl.program_id(0) == 0)
    def _():
        run_ref[...] = jnp.zeros_like(run_ref)

    x = x_ref[...]
    xn = _rms(x, n1w_ref[...]).astype(BF16)
    za = _sigmoid(_dot(xn, wz_ref[:, 0:D_MODEL]))
    zb = _sigmoid(_dot(xn, wz_ref[:, D_MODEL:2 * D_MODEL]))
    mixed = za * _dot(attn_ref[...], wba_ref[...]) + zb * _dot(hgrn_ref[...], wbh_ref[...])
    x1 = x + _dot(mixed.astype(BF16), wout_ref[...])
    x1_out[...] = x1
    hn = _rms(x1, n2w_ref[...])
    hn_out[...] = _pack_bf16_pairs(hn)

    logits = _dot_nt(wr_ref[...], hn.astype(BF16)) + br_ref[...]
    r = lax.broadcasted_iota(jnp.int32, logits.shape, 0).astype(F32)
    far = float(ROUTER_ROWS)
    cmax = lambda a: jnp.max(a, axis=0, keepdims=True)
    cmin = lambda a: jnp.min(a, axis=0, keepdims=True)
    csum = lambda a: jnp.sum(a, axis=0, keepdims=True)

    lg = jnp.where(r < N_GROUPS, logits, NEG_BIG)
    mg = cmax(lg)
    gsel = cmin(jnp.where(lg == mg, r, far))
    pgsel = 1.0 / csum(jnp.exp(lg - mg))

    lo = N_GROUPS + EXPERTS_PER_GROUP * gsel
    le = jnp.where((r >= lo) & (r < lo + EXPERTS_PER_GROUP), logits, NEG_BIG)
    m1 = cmax(le)
    i1 = cmin(jnp.where(le == m1, r, far))
    se = csum(jnp.exp(le - m1))
    le2 = jnp.where(r == i1, NEG_BIG, le)
    m2 = cmax(le2)
    i2 = cmin(jnp.where(le2 == m2, r, far))
    top0 = 1.0 / se
    top1 = jnp.exp(m2 - m1) / se
    tsum = top0 + top1
    w0 = pgsel * top0 / tsum
    w1 = pgsel * top1 / tsum

    sel1 = r == i1
    sel2 = r == i2
    onehot = (sel1 | sel2).astype(BF16)
    before = _dot(onehot, utri_ref[...]) + run_ref[...]
    r0 = csum(jnp.where(sel1, before, 0.0))
    r1 = csum(jnp.where(sel2, before, 0.0))
    run_new = run_ref[...] + _dot(onehot, jnp.ones(utri_ref.shape, BF16))
    run_ref[...] = run_new
    count_out[...] = run_new

    row8 = lax.broadcasted_iota(jnp.int32, route_out.shape, 0)
    vals = (i1 - N_GROUPS, i2 - N_GROUPS, w0, w1, r0, r1)
    route = jnp.zeros(route_out.shape, F32)
    for j, val in enumerate(vals):
        route = jnp.where(row8 == j, val, route)
    route_out[...] = route


def _merge(x2, attn, hgrn, norm1_w, w_z, w_ba, w_bh, w_out, norm2_w, w_r, br, tm):
    T = x2.shape[0]
    row = lambda w: pl.BlockSpec((tm, w), lambda i: (i, 0))
    full = lambda a: pl.BlockSpec(a.shape, lambda i: (0,) * a.ndim)
    utri = jnp.asarray(np.triu(np.ones((tm, tm), np.float32), 1), BF16)
    ins = [x2, attn, hgrn, norm1_w[None, :], w_z, w_ba, w_bh, w_out, norm2_w[None, :], w_r, br, utri]
    in_specs = [row(D_MODEL), row(ATTN_WIDTH), row(HGRN_WIDTH)] + [full(a) for a in ins[3:]]
    return pl.pallas_call(
        _merge_kernel,
        out_shape=[jax.ShapeDtypeStruct((T, D_MODEL), F32), jax.ShapeDtypeStruct((T, HALF), jnp.uint32),
                   jax.ShapeDtypeStruct((ROUTE_ROWS, T), F32), jax.ShapeDtypeStruct((ROUTER_ROWS, tm), F32)],
        grid=(T // tm,),
        in_specs=in_specs,
        out_specs=[row(D_MODEL), row(HALF), pl.BlockSpec((ROUTE_ROWS, tm), lambda i: (0, i)),
                   pl.BlockSpec((ROUTER_ROWS, tm), lambda i: (0, 0))],
        scratch_shapes=[pltpu.VMEM((ROUTER_ROWS, tm), F32)],
        compiler_params=pltpu.CompilerParams(dimension_semantics=("arbitrary",),
                                             vmem_limit_bytes=VMEM_LIMIT),
        name="merge_router",
    )(*ins)


HALF = D_MODEL // 2


def _pack_bf16_pairs(x):
    bits = pltpu.bitcast(x.astype(BF16).astype(F32), jnp.uint32)
    return (bits[:, :HALF] >> 16) | bits[:, HALF:]


def _unpack_bf16_pairs(words):
    lo = pltpu.bitcast(words << 16, F32)
    hi = pltpu.bitcast(words & jnp.uint32(0xFFFF0000), F32)
    return lo, hi


SC_CORES = 2
SC_SUBCORES = 16
SC_CHUNK = 64


def _sc_mesh():
    return plsc.VectorSubcoreMesh(core_axis_name="c", subcore_axis_name="s",
                                  num_cores=SC_CORES, num_subcores=SC_SUBCORES)


def _sc_worker():
    return lax.axis_index("s") * SC_CORES + lax.axis_index("c")


def _dispatch(pos, hn):
    T, width = hn.shape
    workers = SC_CORES * SC_SUBCORES
    per_worker = T // workers
    n_chunks = per_worker // SC_CHUNK
    assert per_worker * workers == T and n_chunks * SC_CHUNK == per_worker and n_chunks % 2 == 0
    idx = pos.T.reshape(TOP_K, workers, n_chunks, SC_CHUNK).transpose(1, 0, 2, 3)

    @functools.partial(
        pl.kernel, mesh=_sc_mesh(), name="moe_dispatch",
        out_type=jax.ShapeDtypeStruct((TOP_K * T, width), hn.dtype),
        scratch_types=[pltpu.VMEM((TOP_K, n_chunks, SC_CHUNK), jnp.int32),
                       pltpu.VMEM((2, SC_CHUNK, width), hn.dtype),
                       pltpu.SemaphoreType.DMA((2,)), pltpu.SemaphoreType.DMA((2,))])
    def dispatch_kernel(rows_hbm, idx_hbm, out_hbm, idx_v, rows_v, load_sem, scatter_sem):
        wid = _sc_worker()
        base = wid * per_worker
        pltpu.sync_copy(idx_hbm.at[wid], idx_v)

        def load(c, b):
            off = pl.multiple_of(c * SC_CHUNK, SC_CHUNK)
            return pltpu.make_async_copy(rows_hbm.at[pl.ds(base + off, SC_CHUNK)], rows_v.at[b], load_sem.at[b])

        def scatter(c, b, k):
            return pltpu.make_async_copy(rows_v.at[b], out_hbm.at[idx_v.at[k, c]], scatter_sem.at[b])

        load(0, 0).start()

        @pl.loop(0, n_chunks, step=2)
        def _(c0):
            for b in range(2):
                c = c0 + b
                load(c, b).wait()
                for k in range(TOP_K):
                    scatter(c, b, k).start()

                @pl.when(c >= 1)
                def _():
                    for k in range(TOP_K):
                        scatter(c - 1, 1 - b, k).wait()

                @pl.when(c + 1 < n_chunks)
                def _():
                    load(c + 1, 1 - b).start()

        for k in range(TOP_K):
            scatter(n_chunks - 1, (n_chunks - 1) % 2, k).wait()

    return dispatch_kernel(hn, idx)


def _expert_kernel(tile_ref, exp_ref, lo_ref, hi_ref, x_ref, wg_ref, wu_ref, wd_ref, y_ref, wg_s, wu_s, wd_s):
    w = pl.program_id(0)
    lo = lo_ref[w]
    hi = hi_ref[w]
    prev = jnp.maximum(w - 1, 0)
    whole = (lo == 0) & (hi == EXPERT_TILE)

    @pl.when((w == 0) | (exp_ref[w] != exp_ref[prev]))
    def _():
        wg_s[...] = wg_ref[0].astype(BF16)
        wu_s[...] = wu_ref[0].astype(BF16)
        wd_s[...] = wd_ref[0].astype(BF16)

    def ffn(words):
        lo, hi = _unpack_bf16_pairs(words)
        lo = lo.astype(BF16)
        hi = hi.astype(BF16)
        gate = _dot(lo, wg_s[0:HALF, :]) + _dot(hi, wg_s[HALF:, :])
        up = _dot(lo, wu_s[0:HALF, :]) + _dot(hi, wu_s[HALF:, :])
        return _pack_bf16_pairs(_dot((gate * _sigmoid(gate) * up).astype(BF16), wd_s[...]))

    @pl.when(whole)
    def _():
        y_ref[...] = ffn(x_ref[...])

    for j in range(EXPERT_TILE // MOE_BLOCK):
        r0 = j * MOE_BLOCK
        rows = slice(r0, r0 + MOE_BLOCK)

        @pl.when(jnp.logical_not(whole) & (lo < r0 + MOE_BLOCK) & (hi > r0))
        def _():
            y = ffn(x_ref[rows, :])
            rowi = lax.broadcasted_iota(jnp.int32, y.shape, 0) + r0
            mine = (rowi >= lo) & (rowi < hi)

            @pl.when(lo <= r0)
            def _():
                y_ref[rows, :] = jnp.where(mine, y, jnp.uint32(0))

            @pl.when(lo > r0)
            def _():
                y_ref[rows, :] = jnp.where(mine, y, y_ref[rows, :])


def _experts(items, xs, w_gate, w_up, w_down):
    tile, exp, lo, hi = items
    wspec = lambda shape: pl.BlockSpec((1,) + shape, lambda w, t, e, l, h: (e[w], 0, 0))
    xspec = pl.BlockSpec((EXPERT_TILE, HALF), lambda w, t, e, l, h: (t[w], 0))
    return pl.pallas_call(
        _expert_kernel,
        out_shape=jax.ShapeDtypeStruct(xs.shape, jnp.uint32),
        grid_spec=pltpu.PrefetchScalarGridSpec(
            num_scalar_prefetch=4,
            grid=(tile.shape[0],),
            in_specs=[xspec, wspec((D_MODEL, EXPERT_FF)), wspec((D_MODEL, EXPERT_FF)),
                      wspec((EXPERT_FF, D_MODEL))],
            out_specs=xspec,
            scratch_shapes=[pltpu.VMEM((D_MODEL, EXPERT_FF), BF16),
                            pltpu.VMEM((D_MODEL, EXPERT_FF), BF16),
                            pltpu.VMEM((EXPERT_FF, D_MODEL), BF16)]),
        compiler_params=pltpu.CompilerParams(dimension_semantics=("arbitrary",),
                                             vmem_limit_bytes=VMEM_LIMIT),
        name="moe_experts",
    )(tile, exp, lo, hi, xs, w_gate, w_up, w_down)


def _sc_gather_rows(table, idx):
    n = idx.shape[0]
    width = table.shape[1]
    per_worker = n // (SC_CORES * SC_SUBCORES)
    n_chunks = per_worker // SC_CHUNK
    assert per_worker * SC_CORES * SC_SUBCORES == n and n_chunks * SC_CHUNK == per_worker and n_chunks % 2 == 0

    @functools.partial(
        pl.kernel, mesh=_sc_mesh(), name="moe_row_gather",
        out_type=jax.ShapeDtypeStruct((n, width), table.dtype),
        scratch_types=[pltpu.VMEM((per_worker,), jnp.int32),
                       pltpu.VMEM((2, SC_CHUNK, width), table.dtype),
                       pltpu.SemaphoreType.DMA((2,))])
    def gather_kernel(table_hbm, idx_hbm, out_hbm, idx_v, rows_v, sem):
        base = _sc_worker() * per_worker
        pltpu.sync_copy(idx_hbm.at[pl.ds(base, per_worker)], idx_v)

        def gather(c, b):
            off = pl.multiple_of(c * SC_CHUNK, SC_CHUNK)
            return pltpu.make_async_copy(table_hbm.at[idx_v.at[pl.ds(off, SC_CHUNK)]], rows_v.at[b], sem.at[b])

        gather(0, 0).start()

        @pl.loop(0, n_chunks, step=2)
        def _(c0):
            for b in range(2):
                c = c0 + b

                @pl.when(c + 1 < n_chunks)
                def _():
                    gather(c + 1, 1 - b).start()

                gather(c, b).wait()
                off = pl.multiple_of(c * SC_CHUNK, SC_CHUNK)
                pltpu.sync_copy(rows_v.at[b], out_hbm.at[pl.ds(base + off, SC_CHUNK)])

    return gather_kernel(table, idx)


def _combine_kernel(x1_ref, w_ref, y0_ref, y1_ref, o_ref):
    w0 = w_ref[:, 0:1]
    w1 = w_ref[:, 1:2]
    lo0, hi0 = _unpack_bf16_pairs(y0_ref[...])
    lo1, hi1 = _unpack_bf16_pairs(y1_ref[...])
    o_ref[:, 0:HALF] = x1_ref[:, 0:HALF] + w0 * lo0 + w1 * lo1
    o_ref[:, HALF:] = x1_ref[:, HALF:] + w0 * hi0 + w1 * hi1


def _combine(pos, x1, gate_w, y, tk):
    T = x1.shape[0]
    nt = T // tk
    ysel = _sc_gather_rows(y, pos.T.reshape(-1))
    return pl.pallas_call(
        _combine_kernel,
        out_shape=jax.ShapeDtypeStruct((T, D_MODEL), F32),
        grid=(nt,),
        in_specs=[pl.BlockSpec((tk, D_MODEL), lambda i: (i, 0)),
                  pl.BlockSpec((tk, TOP_K), lambda i: (i, 0)),
                  pl.BlockSpec((tk, HALF), lambda i: (i, 0)),
                  pl.BlockSpec((tk, HALF), lambda i: (i + nt, 0))],
        out_specs=pl.BlockSpec((tk, D_MODEL), lambda i: (i, 0)),
        compiler_params=pltpu.CompilerParams(dimension_semantics=("arbitrary",),
                                             vmem_limit_bytes=VMEM_LIMIT),
        name="moe_combine",
    )(x1, gate_w, ysel, ysel)


def _routing_tables(route, counts, T):
    e = route[0:TOP_K].astype(jnp.int32)
    rank = route[4:4 + TOP_K].astype(jnp.int32)
    counts = counts.astype(jnp.int32)
    starts = jnp.cumsum(counts) - counts
    ids = jnp.arange(N_EXPERTS, dtype=jnp.int32)
    pos = rank + jnp.sum(jnp.where(e[:, :, None] == ids, starts, 0), axis=-1)
    n_rows = TOP_K * T
    cuts = jnp.sort(jnp.concatenate([jnp.arange(0, n_rows, EXPERT_TILE, dtype=jnp.int32), starts[1:]]))
    ends = jnp.concatenate([cuts[1:], jnp.full((1,), n_rows, jnp.int32)])
    tile = jnp.minimum(cuts // EXPERT_TILE, n_rows // EXPERT_TILE - 1)
    exp = jnp.clip(jnp.sum(starts[None, :] <= cuts[:, None], axis=1) - 1, 0, N_EXPERTS - 1).astype(jnp.int32)
    lo = cuts - tile * EXPERT_TILE
    hi = ends - tile * EXPERT_TILE
    return pos.T, (tile, exp, lo, hi)


def _pick_tile(n, pref):
    t = pref
    while n % t:
        t //= 2
    return t


def kernel(x, positions, norm1_w, w_in, q_norm_w, k_norm_w, attn_sinks, hgrn_lower_bounds, hgrn_norm_w,
           w_branch_attn, w_branch_hgrn, w_out, norm2_w, w_router_group, b_router_group, w_router_expert,
           b_router_expert, w_gate_experts, w_up_experts, w_down_experts):
    B, S, D = x.shape
    T = B * S
    x2 = x.reshape(T, D)
    tm = _pick_tile(T, 512)

    inv_freq = ROPE_THETA ** (-jnp.arange(0, ROT_DIM, 2, dtype=F32) / ROT_DIM)
    ang = positions.astype(F32).reshape(T, 1) * inv_freq[None, :]
    cs = jnp.concatenate(_split3(jnp.concatenate([jnp.cos(ang), jnp.sin(ang)], axis=-1)), axis=-1)

    w_in0 = w_in[0]
    dup = HEAD_DIM * (np.arange(2 * KV_WIDTH) // LANES) + np.arange(2 * KV_WIDTH) % HEAD_DIM
    w_in_a = jnp.concatenate([w_in0[:, :_OFF_K], w_in0[:, _OFF_K + dup], w_in0[:, _OFF_V + dup],
                              w_in0[:, _OFF_HQ:_OFF_Z]], axis=1).astype(BF16)
    w_z = w_in0[:, _OFF_Z:].astype(BF16)

    q, k, v, hq, lf, hv, hg = _inproj(x2, norm1_w[0], w_in_a, cs, q_norm_w[0], k_norm_w[0],
                                      hgrn_lower_bounds.astype(F32), tm)
    attn = _attention(q, k, v, attn_sinks[0].astype(F32), B, S)
    hgrn = _hgrn(hq, lf, hv, hg, hgrn_norm_w[0], B, S, _pick_tile(S, 512))

    pad = ROUTER_ROWS - N_GROUPS - N_EXPERTS
    w_r = jnp.concatenate([w_router_group[0].T, w_router_expert[0].T, jnp.zeros((pad, D), F32)], axis=0)
    b_r =jnp.concatenate([b_router_group[0], b_router_expert[0], jnp.zeros((pad,), F32)]).astype(F32)
    b_r = jnp.broadcast_to(b_r[:, None], (ROUTER_ROWS, tm))

    x1, hn, route, counts = _merge(x2, attn, hgrn, norm1_w[0], w_z, w_branch_attn[0].astype(BF16),
                                   w_branch_hgrn[0].astype(BF16), w_out[0].astype(BF16), norm2_w[0],
                                   w_r.astype(BF16), b_r, tm)

    pos, items = _routing_tables(route, counts[N_GROUPS:N_GROUPS + N_EXPERTS, 0], T)
    tk = _pick_tile(T, 256)
    xs = _dispatch(pos, hn)
    y = _experts(items, xs, w_gate_experts[0], w_up_experts[0], w_down_experts[0])
    out = _combine(pos, x1, route[2:2 + TOP_K].T, y, tk)
    return out.reshape(B, S, D)
```

```python
import functools

import numpy as np
import jax
import jax.numpy as jnp
from jax import lax
from jax.experimental import pallas as pl
from jax.experimental.pallas import tpu as pltpu
from jax.experimental.pallas import tpu_sc as plsc

F32 = jnp.float32
BF16 = jnp.bfloat16

D_MODEL = 1024
N_Q_HEADS = 8
N_KV_HEADS = 2
GROUP = N_Q_HEADS // N_KV_HEADS
HEAD_DIM = 64
ROT_DIM = HEAD_DIM // 4
ROT_HALF = ROT_DIM // 2
ROPE_THETA = 500000.0
WINDOW = 128
ATTN_WIDTH = N_Q_HEADS * HEAD_DIM
KV_WIDTH = N_KV_HEADS * HEAD_DIM

HGRN_HEADS = 4
HGRN_DIM = 128
HGRN_WIDTH = HGRN_HEADS * HGRN_DIM
CHUNK = 64

N_GROUPS = 4
EXPERTS_PER_GROUP = 8
N_EXPERTS = N_GROUPS * EXPERTS_PER_GROUP
TOP_K = 2
EXPERT_FF = 512
MOE_BLOCK = 128
EXPERT_TILE = 512
NORM_EPS = 1e-6
ROUTER_ROWS = 40
ROUTE_ROWS = 8

LANES = 128
NEG_BIG = -1e30
LOG2_E = 1.4426950408889634

_OFF_Q, _OFF_K, _OFF_V = 0, ATTN_WIDTH, ATTN_WIDTH + KV_WIDTH
_OFF_HQ = ATTN_WIDTH + 2 * KV_WIDTH
_OFF_HF = _OFF_HQ + HGRN_WIDTH
_OFF_HI = _OFF_HF + HGRN_WIDTH
_OFF_HG = _OFF_HI + HGRN_WIDTH
_OFF_Z = _OFF_HG + HGRN_WIDTH
_A_Q, _A_K, _A_V = 0, ATTN_WIDTH, ATTN_WIDTH + 2 * KV_WIDTH
_A_HQ = ATTN_WIDTH + 4 * KV_WIDTH
_A_HF = _A_HQ + HGRN_WIDTH
_A_HI = _A_HF + HGRN_WIDTH
_A_HG = _A_HI + HGRN_WIDTH

VMEM_LIMIT = 56 * 1024 * 1024


def _split3(a):
    hi = a.astype(BF16)
    r1 = a - hi.astype(F32)
    mid = r1.astype(BF16)
    lo = (r1 - mid.astype(F32)).astype(BF16)
    return hi, mid, lo


def _dot(a, b):
    return jnp.dot(a, b, preferred_element_type=F32)


def _dot_nt(a, b):
    return lax.dot_general(a, b, (((1,), (1,)), ((), ())), preferred_element_type=F32)


def _dot_tn(a, b):
    return lax.dot_general(a, b, (((0,), (0,)), ((), ())), preferred_element_type=F32)


def _sigmoid(x):
    return 1.0 / (1.0 + jnp.exp(-x))


def _rms(x, w):
    ms = jnp.mean(x * x, axis=-1, keepdims=True)
    return x * lax.rsqrt(ms + NORM_EPS) * w


def _inproj_kernel(x_ref, n1w_ref, w_ref, cs_ref, rope_e_ref, rope_c0_ref, qw_ref, kw_ref,
                   mq_ref, mk_ref, lbp_ref,
                   q_out, k_out, v_out, hq_out, lf_out, hv_out, hg_out):
    xn = _rms(x_ref[...], n1w_ref[...]).astype(BF16)

    def proj(off, width):
        return _dot(xn, w_ref[:, off:off + width])

    tabs = _dot_tn(cs_ref[...], rope_e_ref[...])
    c_tab = tabs[:, 0:LANES] + rope_c0_ref[...]
    s1_tab = tabs[:, LANES:2 * LANES]
    s2_tab = tabs[:, 2 * LANES:3 * LANES]

    def norm_rope(t, mavg_ref, w_row, scale):
        ms = _dot((t * t).astype(BF16), mavg_ref[...])
        tn = t * lax.rsqrt(ms + NORM_EPS) * w_row
        if scale != 1.0:
            tn = tn * scale
        outs = []
        for j in range(t.shape[1] // LANES):
            c = tn[:, j * LANES:(j + 1) * LANES]
            outs.append(c * c_tab
                        + pltpu.roll(c, LANES - ROT_HALF, 1) * s1_tab
                        + pltpu.roll(c, ROT_HALF, 1) * s2_tab)
        return outs[0] if len(outs) == 1 else jnp.concatenate(outs, axis=1)

    q_out[...] = norm_rope(proj(_A_Q, ATTN_WIDTH), mq_ref, qw_ref[...], HEAD_DIM ** -0.5).astype(BF16)
    k_out[...] = norm_rope(proj(_A_K, 2 * KV_WIDTH), mk_ref, kw_ref[...], 1.0).astype(BF16)
    v_out[...] = proj(_A_V, 2 * KV_WIDTH).astype(BF16)

    hq = proj(_A_HQ, HGRN_WIDTH)
    hq_out[...] = (hq * _sigmoid(hq)).astype(BF16)
    h0 = lbp_ref[0:1, :]
    h1 = lbp_ref[1:2, :]
    hm = jnp.maximum(h0, h1)
    e0 = jnp.exp(h0 - hm)
    e1 = jnp.exp(h1 - hm)
    lb = e0 / (e0 + e1)
    fg = lb + (1.0 - lb) * _sigmoid(proj(_A_HF, HGRN_WIDTH))
    lf_out[...] = jnp.log(fg) * LOG2_E
    hv_out[...] = proj(_A_HI, HGRN_WIDTH).astype(BF16)
    hg = proj(_A_HG, HGRN_WIDTH)
    hg_out[...] = (hg * _sigmoid(hg)).astype(BF16)


def _rope_constants():
    e = np.zeros((2 * ROT_HALF, 3 * LANES), np.float32)
    c0 = np.zeros((1, LANES), np.float32)
    for lane in range(LANES):
        d = lane % HEAD_DIM
        if d < ROT_HALF:
            e[d, lane] = 1.0
            e[ROT_HALF + d, LANES + lane] = -1.0
        elif d < ROT_DIM:
            e[d - ROT_HALF, lane] = 1.0
            e[ROT_HALF + d - ROT_HALF, 2 * LANES + lane] = 1.0
        else:
            c0[0, lane] = 1.0
    return jnp.asarray(np.concatenate([e, e, e], axis=0), BF16), jnp.asarray(c0, F32)


def _head_mean_matrix(width):
    idx = np.arange(width) // HEAD_DIM
    m = (idx[:, None] == idx[None, :]).astype(np.float32) / HEAD_DIM
    return jnp.asarray(m, BF16)


def _inproj(x2, norm1_w, w_in_a, cs, q_norm_w, k_norm_w, lbp, tm):
    T = x2.shape[0]
    rope_e, rope_c0 = _rope_constants()
    qw = jnp.tile(q_norm_w.astype(F32), N_Q_HEADS)[None, :]
    kw = jnp.tile(k_norm_w.astype(F32), 2 * N_KV_HEADS)[None, :]
    mq = _head_mean_matrix(ATTN_WIDTH)
    mk = _head_mean_matrix(2 * KV_WIDTH)
    row = lambda w: pl.BlockSpec((tm, w), lambda i: (i, 0))
    full = lambda a: pl.BlockSpec(a.shape, lambda i: (0,) * a.ndim)
    ins = [x2, norm1_w[None, :], w_in_a, cs, rope_e, rope_c0, qw, kw, mq, mk, lbp]
    in_specs = ([row(D_MODEL), full(ins[1]), full(w_in_a), pl.BlockSpec((cs.shape[0], tm), lambda i: (0, i))]
                + [full(a) for a in ins[4:]])
    outs = [(ATTN_WIDTH, BF16), (2 * KV_WIDTH, BF16), (2 * KV_WIDTH, BF16), (HGRN_WIDTH, BF16),
            (HGRN_WIDTH, F32), (HGRN_WIDTH, BF16), (HGRN_WIDTH, BF16)]
    return pl.pallas_call(
        _inproj_kernel,
        out_shape=[jax.ShapeDtypeStruct((T, w), dt) for w, dt in outs],
        grid=(T // tm,),
        in_specs=in_specs,
        out_specs=[row(w) for w, _ in outs],
        compiler_params=pltpu.CompilerParams(dimension_semantics=("arbitrary",),
                                             vmem_limit_bytes=VMEM_LIMIT),
        name="inproj",
    )(*ins)


ATTN_QBLOCKS = 8


def _attn_kernel(sink_ref, q_ref, kc_ref, kp_ref, vc_ref, vp_ref, half_ref, o_ref):
    n_qblocks = q_ref.shape[0] // WINDOW
    has_prev = pl.program_id(1) > 0
    qi = lax.broadcasted_iota(jnp.int32, (WINDOW, 2 * WINDOW), 0)
    kj = lax.broadcasted_iota(jnp.int32, (WINDOW, 2 * WINDOW), 1)
    in_window = ((kj < WINDOW) & (kj > qi)) | ((kj >= WINDOW) & (kj - WINDOW <= qi))
    first_valid = in_window & ((kj >= WINDOW) | has_prev)
    left = lax.broadcasted_iota(jnp.int32, (WINDOW, LANES), 1) < HEAD_DIM
    half = (half_ref[0], half_ref[1])

    ks, rhs = [], []
    for h in range(N_KV_HEADS):
        cols = slice(h * LANES, (h + 1) * LANES)
        kall = jnp.concatenate([kp_ref[:, cols], kc_ref[:, cols]], axis=0)
        vall = jnp.concatenate([vp_ref[:, cols], vc_ref[:, cols]], axis=0)
        ks.append([kall * hm for hm in half])
        rhs.append([jnp.concatenate([vall * hm, hm], axis=1) for hm in half])

    units = [(j, h, pr, side) for j in range(n_qblocks) for h in range(N_KV_HEADS)
             for pr in range(GROUP // 2) for side in range(2)]
    scores = []
    for j, h, pr, side in units:
        pair = h * (GROUP // 2) + pr
        qp = q_ref[j * WINDOW:(j + 1) * WINDOW, pair * LANES:(pair + 1) * LANES]
        s = _dot_nt(qp, ks[h][side][j * WINDOW:(j + 2) * WINDOW])
        scores.append(jnp.where(first_valid if j == 0 else in_window, s, NEG_BIG))
    probs, sink_terms = [], []
    for (j, h, pr, side), s in zip(units, scores):
        sink = sink_ref[2 * (h * (GROUP // 2) + pr) + side]
        m = jnp.maximum(jnp.max(s, axis=-1, keepdims=True), sink)
        probs.append(jnp.exp(s - m).astype(BF16))
        sink_terms.append(jnp.exp(sink - m))
    acc = []
    for (j, h, pr, side), p in zip(units, probs):
        acc.append(_dot(p, rhs[h][side][j * WINDOW:(j + 2) * WINDOW]))
    for j in range(n_qblocks):
        outs = []
        for u in range(0, len(units), 2):
            if units[u][0] == j:
                both = acc[u] + acc[u + 1]
                den = both[:, LANES:2 * LANES] + jnp.where(left, sink_terms[u], sink_terms[u + 1])
                outs.append(both[:, 0:LANES] / den)
        o_ref[j * WINDOW:(j + 1) * WINDOW, :] = jnp.concatenate(outs, axis=1).astype(BF16)


def _attention(q, k, v, sinks, B, S):
    qblocks = _pick_tile(S // WINDOW, ATTN_QBLOCKS)
    rows = qblocks * WINDOW
    nb = S // rows
    cur = lambda b, n: (b * nb + n, 0)
    prev = lambda b, n: (jnp.maximum((b * nb + n) * qblocks - 1, 0), 0)
    lane_left = np.arange(LANES) < HEAD_DIM
    half = jnp.asarray(np.broadcast_to(np.stack([lane_left, ~lane_left])[:, None, :],
                                       (2, rows + WINDOW, LANES)), BF16)
    return pl.pallas_call(
        _attn_kernel,
        out_shape=jax.ShapeDtypeStruct((B * S, ATTN_WIDTH), BF16),
        grid=(B, nb),
        in_specs=[pl.BlockSpec(memory_space=pltpu.SMEM),
                  pl.BlockSpec((rows, ATTN_WIDTH), cur),
                  pl.BlockSpec((rows, 2 * KV_WIDTH), cur),
                  pl.BlockSpec((WINDOW, 2 * KV_WIDTH), prev),
                  pl.BlockSpec((rows, 2 * KV_WIDTH), cur),
                  pl.BlockSpec((WINDOW, 2 * KV_WIDTH), prev),
                  pl.BlockSpec(half.shape, lambda b, n: (0, 0, 0))],
        out_specs=pl.BlockSpec((rows, ATTN_WIDTH), cur),
        compiler_params=pltpu.CompilerParams(dimension_semantics=("arbitrary", "arbitrary"),
                                             vmem_limit_bytes=VMEM_LIMIT),
        name="swa_attention",
    )(sinks, q, k, k, v, v, half)


_LEVEL_HALVES = (1, 2, 4, 8, 16, 32)


def _hgrn_level_masks():
    t = np.arange(CHUNK)[:, None]
    s = np.arange(CHUNK)[None, :]
    masks = [((t // (2 * h)) == (s // (2 * h))) & ((t & h) != 0) & ((s & h) == 0) for h in _LEVEL_HALVES]
    return jnp.asarray(np.stack(masks), F32)


def _level_reference(b_ref, slot, half):
    if half >= 4:
        span = max(2 * half, 8)
        pieces = [jnp.broadcast_to(b_ref[slot, s + half - 1:s + half, :], (span, HGRN_DIM))
                  for s in range(0, CHUNK, span)]
    else:
        r8 = lax.broadcasted_iota(jnp.int32, (8, HGRN_DIM), 0)
        pieces = [jnp.where(r8 < 4,
                            jnp.broadcast_to(b_ref[slot, s + 1:s + 2, :], (8, HGRN_DIM)),
                            jnp.broadcast_to(b_ref[slot, s + 5:s + 6, :], (8, HGRN_DIM)))
                  for s in range(0, CHUNK, 8)]
    return pieces[0] if len(pieces) == 1 else jnp.concatenate(pieces, axis=0)


def _hgrn_kernel(hq_ref, lf_ref, hv_ref, hg_ref, nw_ref, tri_ref, lm_ref, o_ref, st_ref, b_ref):
    @pl.when(pl.program_id(1) == 0)
    def _():
        st_ref[...] = jnp.zeros_like(st_ref)

    tri2 = tri_ref[...]
    odd = (lax.broadcasted_iota(jnp.int32, (CHUNK, HGRN_DIM), 0) & 1) != 0
    masks = [lm_ref[li] != 0.0 for li in range(len(_LEVEL_HALVES))]
    units = [(c, h) for c in range(hq_ref.shape[0] // CHUNK) for h in range(HGRN_HEADS)]
    sl = lambda c, h: (slice(c * CHUNK, (c + 1) * CHUNK), slice(h * HGRN_DIM, (h + 1) * HGRN_DIM))

    bs = []
    for u, (c, h) in enumerate(units):
        lf2 = lf_ref[sl(c, h)]
        hi = lf2.astype(BF16)
        lo = (lf2 - hi.astype(F32)).astype(BF16)
        b = _dot(tri2, jnp.concatenate([hi, lo], axis=0))
        b_ref[u] = b
        bs.append(b)

    outs, qs, ks, fs = [], [], [], []
    for u, (c, h) in enumerate(units):
        b = bs[u]
        q = hq_ref[sl(c, h)].astype(F32)
        v_bf = hv_ref[sl(c, h)]
        f = jnp.exp2(lf_ref[sl(c, h)])
        k = 1.0 - f
        b_last = b[CHUNK - 1:CHUNK, :]
        st = st_ref[h]
        o = _dot_nt((q * jnp.exp2(b)).astype(BF16), st.astype(BF16))
        k_out = (k * jnp.exp2(b_last - b)).astype(BF16)
        st_ref[h] = st * jnp.exp2(b_last) + _dot_tn(v_bf, k_out)
        outs.append(o + jnp.sum(q * k, axis=-1, keepdims=True) * v_bf.astype(F32))
        qs.append(q)
        ks.append(k)
        fs.append(f)

    accs = [jnp.zeros((CHUNK, CHUNK), F32) for _ in units]
    for li, half in enumerate(_LEVEL_HALVES):
        for u in range(len(units)):
            if half == 1:
                e = jnp.where(odd, fs[u], 1.0)
            else:
                e = jnp.exp2(-jnp.abs(bs[u] - _level_reference(b_ref, u, half)))
            accs[u] = jnp.where(masks[li], _dot_nt((qs[u] * e).astype(BF16), (ks[u] * e).astype(BF16)), accs[u])

    for u, (c, h) in enumerate(units):
        o = outs[u] + _dot(accs[u].astype(BF16), hv_ref[sl(c, h)])
        y = _rms(o, nw_ref[...]) * hg_ref[sl(c, h)].astype(F32)
        o_ref[sl(c, h)] = y.astype(BF16)


def _hgrn(hq, lf, hv, hg, hgrn_norm_w, B, S, tb):
    nt = S // tb
    blk = pl.BlockSpec((tb, HGRN_WIDTH), lambda b, n: (b * nt + n, 0))
    tri = np.tril(np.ones((CHUNK, CHUNK), np.float32))
    tri = jnp.asarray(np.concatenate([tri, tri], axis=1), BF16)
    lm = _hgrn_level_masks()
    return pl.pallas_call(
        _hgrn_kernel,
        out_shape=jax.ShapeDtypeStruct((B * S, HGRN_WIDTH), BF16),
        grid=(B, nt),
        in_specs=[blk, blk, blk, blk,
                  pl.BlockSpec((1, HGRN_DIM), lambda b, n: (0, 0)),
                  pl.BlockSpec(tri.shape, lambda b, n: (0, 0)),
                  pl.BlockSpec(lm.shape, lambda b, n: (0, 0, 0))],
        out_specs=blk,
        scratch_shapes=[pltpu.VMEM((HGRN_HEADS, HGRN_DIM, HGRN_DIM), F32),
                        pltpu.VMEM((tb // CHUNK * HGRN_HEADS, CHUNK, HGRN_DIM), F32)],
        compiler_params=pltpu.CompilerParams(dimension_semantics=("arbitrary", "arbitrary"),
                                             vmem_limit_bytes=VMEM_LIMIT),
        name="hgrn2",
    )(hq, lf, hv, hg, hgrn_norm_w[None, :].astype(F32), tri, lm)


def _merge_kernel(x_ref, attn_ref, hgrn_ref, n1w_ref, wz_ref, wba_ref, wbh_ref, wout_ref, n2w_ref,
                  wr_ref, br_ref, utri_ref, x1_out, hn_out, route_out, count_out, run_ref):
    @pl.when(pl.program_id(0) == 0)
    def _():
        run_ref[...] = jnp.zeros_like(run_ref)

    x = x_ref[...]
    xn = _rms(x, n1w_ref[...]).astype(BF16)
    za = _sigmoid(_dot(xn, wz_ref[:, 0:D_MODEL]))
    zb = _sigmoid(_dot(xn, wz_ref[:, D_MODEL:2 * D_MODEL]))
    mixed = za * _dot(attn_ref[...], wba_ref[...]) + zb * _dot(hgrn_ref[...], wbh_ref[...])
    x1 = x + _dot(mixed.astype(BF16), wout_ref[...])
    x1_out[...] = x1
    hn = _rms(x1, n2w_ref[...])
    hn_out[...] = _pack_bf16_pairs(hn)

    logits = _dot_nt(wr_ref[...], hn.astype(BF16)) + br_ref[...]
    r = lax.broadcasted_iota(jnp.int32, logits.shape, 0).astype(F32)
    far = float(ROUTER_ROWS)
    cmax = lambda a: jnp.max(a, axis=0, keepdims=True)
    cmin = lambda a: jnp.min(a, axis=0, keepdims=True)
    csum = lambda a: jnp.sum(a, axis=0, keepdims=True)

    lg = jnp.where(r < N_GROUPS, logits, NEG_BIG)
    mg = cmax(lg)
    gsel = cmin(jnp.where(lg == mg, r, far))
    pgsel = 1.0 / csum(jnp.exp(lg - mg))

    lo = N_GROUPS + EXPERTS_PER_GROUP * gsel
    le = jnp.where((r >= lo) & (r < lo + EXPERTS_PER_GROUP), logits, NEG_BIG)
    m1 = cmax(le)
    i1 = cmin(jnp.where(le == m1, r, far))
    se = csum(jnp.exp(le - m1))
    le2 = jnp.where(r == i1, NEG_BIG, le)
    m2 = cmax(le2)
    i2 = cmin(jnp.where(le2 == m2, r, far))
    top0 = 1.0 / se
    top1 = jnp.exp(m2 - m1) / se
    tsum = top0 + top1
    w0 = pgsel * top0 / tsum
    w1 = pgsel * top1 / tsum

    sel1 = r == i1
    sel2 = r == i2
    onehot = (sel1 | sel2).astype(BF16)
    before = _dot(onehot, utri_ref[...]) + run_ref[...]
    r0 = csum(jnp.where(sel1, before, 0.0))
    r1 = csum(jnp.where(sel2, before, 0.0))
    run_new = run_ref[...] + _dot(onehot, jnp.ones(utri_ref.shape, BF16))
    run_ref[...] = run_new
    count_out[...] = run_new

    row8 = lax.broadcasted_iota(jnp.int32, route_out.shape, 0)
    vals = (i1 - N_GROUPS, i2 - N_GROUPS, w0, w1, r0, r1)
    route = jnp.zeros(route_out.shape, F32)
    for j, val in enumerate(vals):
        route = jnp.where(row8 == j, val, route)
    route_out[...] = route


def _merge(x2, attn, hgrn, norm1_w, w_z, w_ba, w_bh, w_out, norm2_w, w_r, br, tm):
    T = x2.shape[0]
    row = lambda w: pl.BlockSpec((tm, w), lambda i: (i, 0))
    full = lambda a: pl.BlockSpec(a.shape, lambda i: (0,) * a.ndim)
    utri = jnp.asarray(np.triu(np.ones((tm, tm), np.float32), 1), BF16)
    ins = [x2, attn, hgrn, norm1_w[None, :], w_z, w_ba, w_bh, w_out, norm2_w[None, :], w_r, br, utri]
    in_specs = [row(D_MODEL), row(ATTN_WIDTH), row(HGRN_WIDTH)] + [full(a) for a in ins[3:]]
    return pl.pallas_call(
        _merge_kernel,
        out_shape=[jax.ShapeDtypeStruct((T, D_MODEL), F32), jax.ShapeDtypeStruct((T, HALF), jnp.uint32),
                   jax.ShapeDtypeStruct((ROUTE_ROWS, T), F32), jax.ShapeDtypeStruct((ROUTER_ROWS, tm), F32)],
        grid=(T // tm,),
        in_specs=in_specs,
        out_specs=[row(D_MODEL), row(HALF), pl.BlockSpec((ROUTE_ROWS, tm), lambda i: (0, i)),
                   pl.BlockSpec((ROUTER_ROWS, tm), lambda i: (0, 0))],
        scratch_shapes=[pltpu.VMEM((ROUTER_ROWS, tm), F32)],
        compiler_params=pltpu.CompilerParams(dimension_semantics=("arbitrary",),
                                             vmem_limit_bytes=VMEM_LIMIT),
        name="merge_router",
    )(*ins)


HALF = D_MODEL // 2


def _pack_bf16_pairs(x):
    bits = pltpu.bitcast(x.astype(BF16).astype(F32), jnp.uint32)
    return (bits[:, :HALF] >> 16) | bits[:, HALF:]


def _unpack_bf16_pairs(words):
    lo = pltpu.bitcast(words << 16, F32)
    hi = pltpu.bitcast(words & jnp.uint32(0xFFFF0000), F32)
    return lo, hi


SC_CORES = 2
SC_SUBCORES = 16
SC_CHUNK = 64


def _sc_mesh():
    return plsc.VectorSubcoreMesh(core_axis_name="c", subcore_axis_name="s",
                                  num_cores=SC_CORES, num_subcores=SC_SUBCORES)


def _sc_worker():
    return lax.axis_index("s") * SC_CORES + lax.axis_index("c")


def _dispatch(pos, hn):
    T, width = hn.shape
    workers = SC_CORES * SC_SUBCORES
    per_worker = T // workers
    n_chunks = per_worker // SC_CHUNK
    assert per_worker * workers == T and n_chunks * SC_CHUNK == per_worker and n_chunks % 2 == 0
    idx = pos.T.reshape(TOP_K, workers, n_chunks, SC_CHUNK).transpose(1, 0, 2, 3)

    @functools.partial(
        pl.kernel, mesh=_sc_mesh(), name="moe_dispatch",
        out_type=jax.ShapeDtypeStruct((TOP_K * T, width), hn.dtype),
        scratch_types=[pltpu.VMEM((TOP_K, n_chunks, SC_CHUNK), jnp.int32),
                       pltpu.VMEM((2, SC_CHUNK, width), hn.dtype),
                       pltpu.SemaphoreType.DMA((2,)), pltpu.SemaphoreType.DMA((2,))])
    def dispatch_kernel(rows_hbm, idx_hbm, out_hbm, idx_v, rows_v, load_sem, scatter_sem):
        wid = _sc_worker()
        base = wid * per_worker
        pltpu.sync_copy(idx_hbm.at[wid], idx_v)

        def load(c, b):
            off = pl.multiple_of(c * SC_CHUNK, SC_CHUNK)
            return pltpu.make_async_copy(rows_hbm.at[pl.ds(base + off, SC_CHUNK)], rows_v.at[b], load_sem.at[b])

        def scatter(c, b, k):
            return pltpu.make_async_copy(rows_v.at[b], out_hbm.at[idx_v.at[k, c]], scatter_sem.at[b])

        load(0, 0).start()

        @pl.loop(0, n_chunks, step=2)
        def _(c0):
            for b in range(2):
                c = c0 + b
                load(c, b).wait()
                for k in range(TOP_K):
                    scatter(c, b, k).start()

                @pl.when(c >= 1)
                def _():
                    for k in range(TOP_K):
                        scatter(c - 1, 1 - b, k).wait()

                @pl.when(c + 1 < n_chunks)
                def _():
                    load(c + 1, 1 - b).start()

        for k in range(TOP_K):
            scatter(n_chunks - 1, (n_chunks - 1) % 2, k).wait()

    return dispatch_kernel(hn, idx)


def _expert_kernel(tile_ref, exp_ref, lo_ref, hi_ref, x_ref, wg_ref, wu_ref, wd_ref, y_ref, wg_s, wu_s, wd_s):
    w = pl.program_id(0)
    lo = lo_ref[w]
    hi = hi_ref[w]
    prev = jnp.maximum(w - 1, 0)
    whole = (lo == 0) & (hi == EXPERT_TILE)

    @pl.when((w == 0) | (exp_ref[w] != exp_ref[prev]))
    def _():
        wg_s[...] = wg_ref[0].astype(BF16)
        wu_s[...] = wu_ref[0].astype(BF16)
        wd_s[...] = wd_ref[0].astype(BF16)

    def ffn(words):
        lo, hi = _unpack_bf16_pairs(words)
        lo = lo.astype(BF16)
        hi = hi.astype(BF16)
        gate = _dot(lo, wg_s[0:HALF, :]) + _dot(hi, wg_s[HALF:, :])
        up = _dot(lo, wu_s[0:HALF, :]) + _dot(hi, wu_s[HALF:, :])
        return _pack_bf16_pairs(_dot((gate * _sigmoid(gate) * up).astype(BF16), wd_s[...]))

    @pl.when(whole)
    def _():
        y_ref[...] = ffn(x_ref[...])

    for j in range(EXPERT_TILE // MOE_BLOCK):
        r0 = j * MOE_BLOCK
        rows = slice(r0, r0 + MOE_BLOCK)

        @pl.when(jnp.logical_not(whole) & (lo < r0 + MOE_BLOCK) & (hi > r0))
        def _():
            y = ffn(x_ref[rows, :])
            rowi = lax.broadcasted_iota(jnp.int32, y.shape, 0) + r0
            mine = (rowi >= lo) & (rowi < hi)

            @pl.when(lo <= r0)
            def _():
                y_ref[rows, :] = jnp.where(mine, y, jnp.uint32(0))

            @pl.when(lo > r0)
            def _():
                y_ref[rows, :] = jnp.where(mine, y, y_ref[rows, :])


def _experts(items, xs, w_gate, w_up, w_down):
    tile, exp, lo, hi = items
    wspec = lambda shape: pl.BlockSpec((1,) + shape, lambda w, t, e, l, h: (e[w], 0, 0))
    xspec = pl.BlockSpec((EXPERT_TILE, HALF), lambda w, t, e, l, h: (t[w], 0))
    return pl.pallas_call(
        _expert_kernel,
        out_shape=jax.ShapeDtypeStruct(xs.shape, jnp.uint32),
        grid_spec=pltpu.PrefetchScalarGridSpec(
            num_scalar_prefetch=4,
            grid=(tile.shape[0],),
            in_specs=[xspec, wspec((D_MODEL, EXPERT_FF)), wspec((D_MODEL, EXPERT_FF)),
                      wspec((EXPERT_FF, D_MODEL))],
            out_specs=xspec,
            scratch_shapes=[pltpu.VMEM((D_MODEL, EXPERT_FF), BF16),
                            pltpu.VMEM((D_MODEL, EXPERT_FF), BF16),
                            pltpu.VMEM((EXPERT_FF, D_MODEL), BF16)]),
        compiler_params=pltpu.CompilerParams(dimension_semantics=("arbitrary",),
                                             vmem_limit_bytes=VMEM_LIMIT),
        name="moe_experts",
    )(tile, exp, lo, hi, xs, w_gate, w_up, w_down)


def _sc_gather_rows(table, idx):
    n = idx.shape[0]
    width = table.shape[1]
    per_worker = n // (SC_CORES * SC_SUBCORES)
    n_chunks = per_worker // SC_CHUNK
    assert per_worker * SC_CORES * SC_SUBCORES == n and n_chunks * SC_CHUNK == per_worker and n_chunks % 2 == 0

    @functools.partial(
        pl.kernel, mesh=_sc_mesh(), name="moe_row_gather",
        out_type=jax.ShapeDtypeStruct((n, width), table.dtype),
        scratch_types=[pltpu.VMEM((per_worker,), jnp.int32),
                       pltpu.VMEM((2, SC_CHUNK, width), table.dtype),
                       pltpu.SemaphoreType.DMA((2,))])
    def gather_kernel(table_hbm, idx_hbm, out_hbm, idx_v, rows_v, sem):
        base = _sc_worker() * per_worker
        pltpu.sync_copy(idx_hbm.at[pl.ds(base, per_worker)], idx_v)

        def gather(c, b):
            off = pl.multiple_of(c * SC_CHUNK, SC_CHUNK)
            return pltpu.make_async_copy(table_hbm.at[idx_v.at[pl.ds(off, SC_CHUNK)]], rows_v.at[b], sem.at[b])

        gather(0, 0).start()

        @pl.loop(0, n_chunks, step=2)
        def _(c0):
            for b in range(2):
                c = c0 + b

                @pl.when(c + 1 < n_chunks)
                def _():
                    gather(c + 1, 1 - b).start()

                gather(c, b).wait()
                off = pl.multiple_of(c * SC_CHUNK, SC_CHUNK)
                pltpu.sync_copy(rows_v.at[b], out_hbm.at[pl.ds(base + off, SC_CHUNK)])

    return gather_kernel(table, idx)


def _combine_kernel(x1_ref, route_ref, y0_ref, y1_ref, o_ref):
    gates = route_ref[...].T
    w0 = gates[:, 2:3]
    w1 = gates[:, 3:4]
    lo0, hi0 = _unpack_bf16_pairs(y0_ref[...])
    lo1, hi1 = _unpack_bf16_pairs(y1_ref[...])
    o_ref[:, 0:HALF] = x1_ref[:, 0:HALF] + w0 * lo0 + w1 * lo1
    o_ref[:, HALF:] = x1_ref[:, HALF:] + w0 * hi0 + w1 * hi1


def _combine(pos, x1, route, y, tk):
    T = x1.shape[0]
    nt = T // tk
    ysel = _sc_gather_rows(y, pos.T.reshape(-1))
    return pl.pallas_call(
        _combine_kernel,
        out_shape=jax.ShapeDtypeStruct((T, D_MODEL), F32),
        grid=(nt,),
        in_specs=[pl.BlockSpec((tk, D_MODEL), lambda i: (i, 0)),
                  pl.BlockSpec((ROUTE_ROWS, tk), lambda i: (0, i)),
                  pl.BlockSpec((tk, HALF), lambda i: (i, 0)),
                  pl.BlockSpec((tk, HALF), lambda i: (i + nt, 0))],
        out_specs=pl.BlockSpec((tk, D_MODEL), lambda i: (i, 0)),
        compiler_params=pltpu.CompilerParams(dimension_semantics=("arbitrary",),
                                             vmem_limit_bytes=VMEM_LIMIT),
        name="moe_combine",
    )(x1, route, ysel, ysel)


def _routing_tables(route, counts, T):
    e = route[0:TOP_K].astype(jnp.int32)
    rank = route[4:4 + TOP_K].astype(jnp.int32)
    counts = counts.astype(jnp.int32)
    starts = jnp.cumsum(counts) - counts
    ids = jnp.arange(N_EXPERTS, dtype=jnp.int32)
    pos = rank + jnp.sum(jnp.where(e[:, :, None] == ids, starts, 0), axis=-1)
    n_rows = TOP_K * T
    cuts = jnp.sort(jnp.concatenate([jnp.arange(0, n_rows, EXPERT_TILE, dtype=jnp.int32), starts[1:]]))
    ends = jnp.concatenate([cuts[1:], jnp.full((1,), n_rows, jnp.int32)])
    tile = jnp.minimum(cuts // EXPERT_TILE, n_rows // EXPERT_TILE - 1)
    exp = jnp.clip(jnp.sum(starts[None, :] <= cuts[:, None], axis=1) - 1, 0, N_EXPERTS - 1).astype(jnp.int32)
    lo = cuts - tile * EXPERT_TILE
    hi = ends - tile * EXPERT_TILE
    return pos.T, (tile, exp, lo, hi)


def _pick_tile(n, pref):
    t = pref
    while n % t:
        t //= 2
    return t


def kernel(x, positions, norm1_w, w_in, q_norm_w, k_norm_w, attn_sinks, hgrn_lower_bounds, hgrn_norm_w,
           w_branch_attn, w_branch_hgrn, w_out, norm2_w, w_router_group, b_router_group, w_router_expert,
           b_router_expert, w_gate_experts, w_up_experts, w_down_experts):
    B, S, D = x.shape
    T = B * S
    x2 = x.reshape(T, D)
    tm = _pick_tile(T, 512)

    inv_freq = ROPE_THETA ** (-jnp.arange(0, ROT_DIM, 2, dtype=F32) / ROT_DIM)
    ang = inv_freq[:, None] * positions.astype(F32).reshape(1, T)
    cs = jnp.concatenate(_split3(jnp.concatenate([jnp.cos(ang), jnp.sin(ang)], axis=0)), axis=0)

    w_in0 = w_in[0]
    heads = lambda off: [w_in0[:, off + h * HEAD_DIM:off + (h + 1) * HEAD_DIM]
                         for h in range(N_KV_HEADS) for _ in range(2)]
    w_in_a = jnp.concatenate([w_in0[:, :_OFF_K]] + heads(_OFF_K) + heads(_OFF_V) + [w_in0[:, _OFF_HQ:_OFF_Z]],
                             axis=1).astype(BF16)
    w_z = w_in0[:, _OFF_Z:].astype(BF16)

    q, k, v, hq, lf, hv, hg = _inproj(x2, norm1_w[0], w_in_a, cs, q_norm_w[0], k_norm_w[0],
                                      hgrn_lower_bounds.astype(F32), tm)
    attn = _attention(q, k, v, attn_sinks[0].astype(F32), B, S)
    hgrn = _hgrn(hq, lf, hv, hg, hgrn_norm_w[0], B, S, _pick_tile(S, 512))

    pad = ROUTER_ROWS - N_GROUPS - N_EXPERTS
    w_r = jnp.concatenate([w_router_group[0].T, w_router_expert[0].T, jnp.zeros((pad, D), F32)], axis=0)
    b_r =jnp.concatenate([b_router_group[0], b_router_expert[0], jnp.zeros((pad,), F32)]).astype(F32)
    b_r = jnp.broadcast_to(b_r[:, None], (ROUTER_ROWS, tm))

    x1, hn, route, counts = _merge(x2, attn, hgrn, norm1_w[0], w_z, w_branch_attn[0].astype(BF16),
                                   w_branch_hgrn[0].astype(BF16), w_out[0].astype(BF16), norm2_w[0],
                                   w_r.astype(BF16), b_r, tm)

    pos, items = _routing_tables(route, counts[N_GROUPS:N_GROUPS + N_EXPERTS, 0], T)
    tk = _pick_tile(T, 512)
    xs = _dispatch(pos, hn)
    y = _experts(items, xs, w_gate_experts[0], w_up_experts[0], w_down_experts[0])
    out = _combine(pos, x1, route, y, tk)
    return out.reshape(B, S, D)
```

```python
import functools

import numpy as np
import jax
import jax.numpy as jnp
from jax import lax
from jax.experimental import pallas as pl
from jax.experimental.pallas import tpu as pltpu
from jax.experimental.pallas import tpu_sc as plsc

F32 = jnp.float32
BF16 = jnp.bfloat16

D_MODEL = 1024
N_Q_HEADS = 8
N_KV_HEADS = 2
GROUP = N_Q_HEADS // N_KV_HEADS
HEAD_DIM = 64
ROT_DIM = HEAD_DIM // 4
ROT_HALF = ROT_DIM // 2
ROPE_THETA = 500000.0
WINDOW = 128
ATTN_WIDTH = N_Q_HEADS * HEAD_DIM
KV_WIDTH = N_KV_HEADS * HEAD_DIM

HGRN_HEADS = 4
HGRN_DIM = 128
HGRN_WIDTH = HGRN_HEADS * HGRN_DIM
CHUNK = 64

N_GROUPS = 4
EXPERTS_PER_GROUP = 8
N_EXPERTS = N_GROUPS * EXPERTS_PER_GROUP
TOP_K = 2
EXPERT_FF = 512
MOE_BLOCK = 128
EXPERT_TILE = 512
NORM_EPS = 1e-6
ROUTER_ROWS = 40
ROUTE_ROWS = 8

LANES = 128
NEG_BIG = -1e30
LOG2_E = 1.4426950408889634

_OFF_Q, _OFF_K, _OFF_V = 0, ATTN_WIDTH, ATTN_WIDTH + KV_WIDTH
_OFF_HQ = ATTN_WIDTH + 2 * KV_WIDTH
_OFF_HF = _OFF_HQ + HGRN_WIDTH
_OFF_HI = _OFF_HF + HGRN_WIDTH
_OFF_HG = _OFF_HI + HGRN_WIDTH
_OFF_Z = _OFF_HG + HGRN_WIDTH
_A_Q, _A_K, _A_V = 0, ATTN_WIDTH, ATTN_WIDTH + 2 * KV_WIDTH
_A_HQ = ATTN_WIDTH + 4 * KV_WIDTH
_A_HF = _A_HQ + HGRN_WIDTH
_A_HI = _A_HF + HGRN_WIDTH
_A_HG = _A_HI + HGRN_WIDTH

VMEM_LIMIT = 56 * 1024 * 1024


def _split3(a):
    hi = a.astype(BF16)
    r1 = a - hi.astype(F32)
    mid = r1.astype(BF16)
    lo = (r1 - mid.astype(F32)).astype(BF16)
    return hi, mid, lo


def _dot(a, b):
    return jnp.dot(a, b, preferred_element_type=F32)


def _dot_nt(a, b):
    return lax.dot_general(a, b, (((1,), (1,)), ((), ())), preferred_element_type=F32)


def _dot_tn(a, b):
    return lax.dot_general(a, b, (((0,), (0,)), ((), ())), preferred_element_type=F32)


def _sigmoid(x):
    return 1.0 / (1.0 + jnp.exp(-x))


def _rms(x, w):
    ms = jnp.mean(x * x, axis=-1, keepdims=True)
    return x * lax.rsqrt(ms + NORM_EPS) * w


def _inproj_kernel(x_ref, n1w_ref, w_ref, cs_ref, rope_e_ref, rope_c0_ref, qw_ref, kw_ref,
                   mq_ref, mk_ref, lbp_ref,
                   q_out, k_out, v_out, hq_out, lf_out, hv_out, hg_out):
    xn = _rms(x_ref[...], n1w_ref[...]).astype(BF16)

    def proj(off, width):
        return _dot(xn, w_ref[:, off:off + width])

    tabs = _dot_tn(cs_ref[...], rope_e_ref[...])
    c_tab = tabs[:, 0:LANES] + rope_c0_ref[...]
    s1_tab = tabs[:, LANES:2 * LANES]
    s2_tab = tabs[:, 2 * LANES:3 * LANES]

    def norm_rope(t, mavg_ref, w_row, scale):
        ms = _dot((t * t).astype(BF16), mavg_ref[...])
        tn = t * lax.rsqrt(ms + NORM_EPS) * w_row
        if scale != 1.0:
            tn = tn * scale
        outs = []
        for j in range(t.shape[1] // LANES):
            c = tn[:, j * LANES:(j + 1) * LANES]
            outs.append(c * c_tab
                        + pltpu.roll(c, LANES - ROT_HALF, 1) * s1_tab
                        + pltpu.roll(c, ROT_HALF, 1) * s2_tab)
        return outs[0] if len(outs) == 1 else jnp.concatenate(outs, axis=1)

    q_out[...] = norm_rope(proj(_A_Q, ATTN_WIDTH), mq_ref, qw_ref[...], HEAD_DIM ** -0.5).astype(BF16)
    k_out[...] = norm_rope(proj(_A_K, 2 * KV_WIDTH), mk_ref, kw_ref[...], 1.0).astype(BF16)
    v_out[...] = proj(_A_V, 2 * KV_WIDTH).astype(BF16)

    hq = proj(_A_HQ, HGRN_WIDTH)
    hq_out[...] = (hq * _sigmoid(hq)).astype(BF16)
    h0 = lbp_ref[0:1, :]
    h1 = lbp_ref[1:2, :]
    hm = jnp.maximum(h0, h1)
    e0 = jnp.exp(h0 - hm)
    e1 = jnp.exp(h1 - hm)
    lb = e0 / (e0 + e1)
    fg = lb + (1.0 - lb) * _sigmoid(proj(_A_HF, HGRN_WIDTH))
    lf_out[...] = jnp.log(fg) * LOG2_E
    hv_out[...] = proj(_A_HI, HGRN_WIDTH).astype(BF16)
    hg = proj(_A_HG, HGRN_WIDTH)
    hg_out[...] = (hg * _sigmoid(hg)).astype(BF16)


def _rope_constants():
    e = np.zeros((2 * ROT_HALF, 3 * LANES), np.float32)
    c0 = np.zeros((1, LANES), np.float32)
    for lane in range(LANES):
        d = lane % HEAD_DIM
        if d < ROT_HALF:
            e[d, lane] = 1.0
            e[ROT_HALF + d, LANES + lane] = -1.0
        elif d < ROT_DIM:
            e[d - ROT_HALF, lane] = 1.0
            e[ROT_HALF + d - ROT_HALF, 2 * LANES + lane] = 1.0
        else:
            c0[0, lane] = 1.0
    return jnp.asarray(np.concatenate([e, e, e], axis=0), BF16), jnp.asarray(c0, F32)


def _head_mean_matrix(width):
    idx = np.arange(width) // HEAD_DIM
    m = (idx[:, None] == idx[None, :]).astype(np.float32) / HEAD_DIM
    return jnp.asarray(m, BF16)


def _inproj(x2, norm1_w, w_in_a, cs, q_norm_w, k_norm_w, lbp, tm):
    T = x2.shape[0]
    rope_e, rope_c0 = _rope_constants()
    qw = jnp.tile(q_norm_w.astype(F32), N_Q_HEADS)[None, :]
    kw = jnp.tile(k_norm_w.astype(F32), 2 * N_KV_HEADS)[None, :]
    mq = _head_mean_matrix(ATTN_WIDTH)
    mk = _head_mean_matrix(2 * KV_WIDTH)
    row = lambda w: pl.BlockSpec((tm, w), lambda i: (i, 0))
    full = lambda a: pl.BlockSpec(a.shape, lambda i: (0,) * a.ndim)
    ins = [x2, norm1_w[None, :], w_in_a, cs, rope_e, rope_c0, qw, kw, mq, mk, lbp]
    in_specs = ([row(D_MODEL), full(ins[1]), full(w_in_a), pl.BlockSpec((cs.shape[0], tm), lambda i: (0, i))]
                + [full(a) for a in ins[4:]])
    outs = [(ATTN_WIDTH, BF16), (2 * KV_WIDTH, BF16), (2 * KV_WIDTH, BF16), (HGRN_WIDTH, BF16),
            (HGRN_WIDTH, F32), (HGRN_WIDTH, BF16), (HGRN_WIDTH, BF16)]
    return pl.pallas_call(
        _inproj_kernel,
        out_shape=[jax.ShapeDtypeStruct((T, w), dt) for w, dt in outs],
        grid=(T // tm,),
        in_specs=in_specs,
        out_specs=[row(w) for w, _ in outs],
        compiler_params=pltpu.CompilerParams(dimension_semantics=("arbitrary",),
                                             vmem_limit_bytes=VMEM_LIMIT),
        name="inproj",
    )(*ins)


ATTN_QBLOCKS = 8


def _attn_kernel(sink_ref, q_ref, kc_ref, kp_ref, vc_ref, vp_ref, half_ref, o_ref):
    n_qblocks = q_ref.shape[0] // WINDOW
    has_prev = pl.program_id(1) > 0
    qi = lax.broadcasted_iota(jnp.int32, (WINDOW, 2 * WINDOW), 0)
    kj = lax.broadcasted_iota(jnp.int32, (WINDOW, 2 * WINDOW), 1)
    in_window = ((kj < WINDOW) & (kj > qi)) | ((kj >= WINDOW) & (kj - WINDOW <= qi))
    first_valid = in_window & ((kj >= WINDOW) | has_prev)
    left = lax.broadcasted_iota(jnp.int32, (WINDOW, LANES), 1) < HEAD_DIM
    half = (half_ref[0], half_ref[1])

    ks, rhs = [], []
    for h in range(N_KV_HEADS):
        cols = slice(h * LANES, (h + 1) * LANES)
        kall = jnp.concatenate([kp_ref[:, cols], kc_ref[:, cols]], axis=0)
        vall = jnp.concatenate([vp_ref[:, cols], vc_ref[:, cols]], axis=0)
        ks.append([kall * hm for hm in half])
        rhs.append([jnp.concatenate([vall * hm, hm], axis=1) for hm in half])

    units = [(j, h, pr, side) for j in range(n_qblocks) for h in range(N_KV_HEADS)
             for pr in range(GROUP // 2) for side in range(2)]
    scores = []
    for j, h, pr, side in units:
        pair = h * (GROUP // 2) + pr
        qp = q_ref[j * WINDOW:(j + 1) * WINDOW, pair * LANES:(pair + 1) * LANES]
        s = _dot_nt(qp, ks[h][side][j * WINDOW:(j + 2) * WINDOW])
        scores.append(jnp.where(first_valid if j == 0 else in_window, s, NEG_BIG))
    probs, sink_terms = [], []
    for (j, h, pr, side), s in zip(units, scores):
        sink = sink_ref[2 * (h * (GROUP // 2) + pr) + side]
        m = jnp.maximum(jnp.max(s, axis=-1, keepdims=True), sink)
        probs.append(jnp.exp(s - m).astype(BF16))
        sink_terms.append(jnp.exp(sink - m))
    acc = []
    for (j, h, pr, side), p in zip(units, probs):
        acc.append(_dot(p, rhs[h][side][j * WINDOW:(j + 2) * WINDOW]))
    for j in range(n_qblocks):
        outs = []
        for u in range(0, len(units), 2):
            if units[u][0] == j:
                both = acc[u] + acc[u + 1]
                den = both[:, LANES:2 * LANES] + jnp.where(left, sink_terms[u], sink_terms[u + 1])
                outs.append(both[:, 0:LANES] / den)
        o_ref[j * WINDOW:(j + 1) * WINDOW, :] = jnp.concatenate(outs, axis=1).astype(BF16)


def _attention(q, k, v, sinks, B, S):
    qblocks = _pick_tile(S // WINDOW, ATTN_QBLOCKS)
    rows = qblocks * WINDOW
    nb = S // rows
    cur = lambda b, n: (b * nb + n, 0)
    prev = lambda b, n: (jnp.maximum((b * nb + n) * qblocks - 1, 0), 0)
    lane_left = np.arange(LANES) < HEAD_DIM
    half = jnp.asarray(np.broadcast_to(np.stack([lane_left, ~lane_left])[:, None, :],
                                       (2, rows + WINDOW, LANES)), BF16)
    return pl.pallas_call(
        _attn_kernel,
        out_shape=jax.ShapeDtypeStruct((B * S, ATTN_WIDTH), BF16),
        grid=(B, nb),
        in_specs=[pl.BlockSpec(memory_space=pltpu.SMEM),
                  pl.BlockSpec((rows, ATTN_WIDTH), cur),
                  pl.BlockSpec((rows, 2 * KV_WIDTH), cur),
                  pl.BlockSpec((WINDOW, 2 * KV_WIDTH), prev),
                  pl.BlockSpec((rows, 2 * KV_WIDTH), cur),
                  pl.BlockSpec((WINDOW, 2 * KV_WIDTH), prev),
                  pl.BlockSpec(half.shape, lambda b, n: (0, 0, 0))],
        out_specs=pl.BlockSpec((rows, ATTN_WIDTH), cur),
        compiler_params=pltpu.CompilerParams(dimension_semantics=("arbitrary", "arbitrary"),
                                             vmem_limit_bytes=VMEM_LIMIT),
        name="swa_attention",
    )(sinks, q, k, k, v, v, half)


_LEVEL_HALVES = (1, 2, 4, 8, 16, 32)
HGRN_UNIT_GROUP = 16


def _hgrn_level_masks():
    t = np.arange(CHUNK)[:, None]
    s = np.arange(CHUNK)[None, :]
    masks = [((t // (2 * h)) == (s // (2 * h))) & ((t & h) != 0) & ((s & h) == 0) for h in _LEVEL_HALVES]
    return jnp.asarray(np.stack(masks), F32)


def _level_reference(b_ref, slot, half):
    if half >= 4:
        span = max(2 * half, 8)
        pieces = [jnp.broadcast_to(b_ref[slot, s + half - 1:s + half, :], (span, HGRN_DIM))
                  for s in range(0, CHUNK, span)]
    else:
        r8 = lax.broadcasted_iota(jnp.int32, (8, HGRN_DIM), 0)
        pieces = [jnp.where(r8 < 4,
                            jnp.broadcast_to(b_ref[slot, s + 1:s + 2, :], (8, HGRN_DIM)),
                            jnp.broadcast_to(b_ref[slot, s + 5:s + 6, :], (8, HGRN_DIM)))
                  for s in range(0, CHUNK, 8)]
    return pieces[0] if len(pieces) == 1 else jnp.concatenate(pieces, axis=0)


def _hgrn_kernel(hq_ref, lf_ref, hv_ref, hg_ref, nw_ref, tri_ref, lm_ref, o_ref, st_ref, b_ref):
    @pl.when(pl.program_id(1) == 0)
    def _():
        st_ref[...] = jnp.zeros_like(st_ref)

    tri2 = tri_ref[...]
    odd = (lax.broadcasted_iota(jnp.int32, (CHUNK, HGRN_DIM), 0) & 1) != 0
    masks = [lm_ref[li] != 0.0 for li in range(len(_LEVEL_HALVES))]
    units = [(c, h) for c in range(hq_ref.shape[0] // CHUNK) for h in range(HGRN_HEADS)]
    sl = lambda c, h: (slice(c * CHUNK, (c + 1) * CHUNK), slice(h * HGRN_DIM, (h + 1) * HGRN_DIM))

    bs = []
    for u, (c, h) in enumerate(units):
        lf2 = lf_ref[sl(c, h)]
        hi = lf2.astype(BF16)
        lo = (lf2 - hi.astype(F32)).astype(BF16)
        b = _dot(tri2, jnp.concatenate([hi, lo], axis=0))
        b_ref[u] = b
        bs.append(b)

    outs, qs, ks, fs = [], [], [], []
    for u, (c, h) in enumerate(units):
        b = bs[u]
        q = hq_ref[sl(c, h)].astype(F32)
        v_bf = hv_ref[sl(c, h)]
        f = jnp.exp2(lf_ref[sl(c, h)])
        k = 1.0 - f
        b_last = b[CHUNK - 1:CHUNK, :]
        st = st_ref[h]
        o = _dot_nt((q * jnp.exp2(b)).astype(BF16), st.astype(BF16))
        k_out = (k * jnp.exp2(b_last - b)).astype(BF16)
        st_ref[h] = st * jnp.exp2(b_last) + _dot_tn(v_bf, k_out)
        outs.append(o + jnp.sum(q * k, axis=-1, keepdims=True) * v_bf.astype(F32))
        qs.append(q)
        ks.append(k)
        fs.append(f)

    for g0 in range(0, len(units), HGRN_UNIT_GROUP):
        group = range(g0, min(g0 + HGRN_UNIT_GROUP, len(units)))
        accs = {u: jnp.zeros((CHUNK, CHUNK), F32) for u in group}
        for li, half in enumerate(_LEVEL_HALVES):
            for u in group:
                if half == 1:
                    e = jnp.where(odd, fs[u], 1.0)
                else:
                    e = jnp.exp2(-jnp.abs(bs[u] - _level_reference(b_ref, u, half)))
                accs[u] = jnp.where(masks[li], _dot_nt((qs[u] * e).astype(BF16), (ks[u] * e).astype(BF16)),
                                    accs[u])
        for u in group:
            c, h = units[u]
            o = outs[u] + _dot(accs[u].astype(BF16), hv_ref[sl(c, h)])
            y = _rms(o, nw_ref[...]) * hg_ref[sl(c, h)].astype(F32)
            o_ref[sl(c, h)] = y.astype(BF16)


def _hgrn(hq, lf, hv, hg, hgrn_norm_w, B, S, tb):
    nt = S // tb
    blk = pl.BlockSpec((tb, HGRN_WIDTH), lambda b, n: (b * nt + n, 0))
    tri = np.tril(np.ones((CHUNK, CHUNK), np.float32))
    tri = jnp.asarray(np.concatenate([tri, tri], axis=1), BF16)
    lm = _hgrn_level_masks()
    return pl.pallas_call(
        _hgrn_kernel,
        out_shape=jax.ShapeDtypeStruct((B * S, HGRN_WIDTH), BF16),
        grid=(B, nt),
        in_specs=[blk, blk, blk, blk,
                  pl.BlockSpec((1, HGRN_DIM), lambda b, n: (0, 0)),
                  pl.BlockSpec(tri.shape, lambda b, n: (0, 0)),
                  pl.BlockSpec(lm.shape, lambda b, n: (0, 0, 0))],
        out_specs=blk,
        scratch_shapes=[pltpu.VMEM((HGRN_HEADS, HGRN_DIM, HGRN_DIM), F32),
                        pltpu.VMEM((tb // CHUNK * HGRN_HEADS, CHUNK, HGRN_DIM), F32)],
        compiler_params=pltpu.CompilerParams(dimension_semantics=("arbitrary", "arbitrary"),
                                             vmem_limit_bytes=VMEM_LIMIT),
        name="hgrn2",
    )(hq, lf, hv, hg, hgrn_norm_w[None, :].astype(F32), tri, lm)


def _merge_kernel(x_ref, attn_ref, hgrn_ref, n1w_ref, wz_ref, wba_ref, wbh_ref, wout_ref, n2w_ref,
                  wr_ref, br_ref, utri_ref, x1_out, hn_out, route_out, count_out, run_ref):
    @pl.when(pl.program_id(0) == 0)
    def _():
        run_ref[...] = jnp.zeros_like(run_ref)

    x = x_ref[...]
    xn = _rms(x, n1w_ref[...]).astype(BF16)
    za = _sigmoid(_dot(xn, wz_ref[:, 0:D_MODEL]))
    zb = _sigmoid(_dot(xn, wz_ref[:, D_MODEL:2 * D_MODEL]))
    mixed = za * _dot(attn_ref[...], wba_ref[...]) + zb * _dot(hgrn_ref[...], wbh_ref[...])
    x1 = x + _dot(mixed.astype(BF16), wout_ref[...])
    x1_out[...] = x1
    hn = _rms(x1, n2w_ref[...])
    hn_out[...] = _pack_bf16_pairs(hn)

    logits = _dot_nt(wr_ref[...], hn.astype(BF16)) + br_ref[...]
    r = lax.broadcasted_iota(jnp.int32, logits.shape, 0).astype(F32)
    far = float(ROUTER_ROWS)
    cmax = lambda a: jnp.max(a, axis=0, keepdims=True)
    cmin = lambda a: jnp.min(a, axis=0, keepdims=True)
    csum = lambda a: jnp.sum(a, axis=0, keepdims=True)

    lg = jnp.where(r < N_GROUPS, logits, NEG_BIG)
    mg = cmax(lg)
    gsel = cmin(jnp.where(lg == mg, r, far))
    pgsel = 1.0 / csum(jnp.exp(lg - mg))

    lo = N_GROUPS + EXPERTS_PER_GROUP * gsel
    le = jnp.where((r >= lo) & (r < lo + EXPERTS_PER_GROUP), logits, NEG_BIG)
    m1 = cmax(le)
    i1 = cmin(jnp.where(le == m1, r, far))
    se = csum(jnp.exp(le - m1))
    le2 = jnp.where(r == i1, NEG_BIG, le)
    m2 = cmax(le2)
    i2 = cmin(jnp.where(le2 == m2, r, far))
    top0 = 1.0 / se
    top1 = jnp.exp(m2 - m1) / se
    tsum = top0 + top1
    w0 = pgsel * top0 / tsum
    w1 = pgsel * top1 / tsum

    sel1 = r == i1
    sel2 = r == i2
    onehot = (sel1 | sel2).astype(BF16)
    before = _dot(onehot, utri_ref[...]) + run_ref[...]
    r0 = csum(jnp.where(sel1, before, 0.0))
    r1 = csum(jnp.where(sel2, before, 0.0))
    run_new = run_ref[...] + _dot(onehot, jnp.ones(utri_ref.shape, BF16))
    run_ref[...] = run_new
    count_out[...] = run_new

    row8 = lax.broadcasted_iota(jnp.int32, route_out.shape, 0)
    vals = (i1 - N_GROUPS, i2 - N_GROUPS, w0, w1, r0, r1)
    route = jnp.zeros(route_out.shape, F32)
    for j, val in enumerate(vals):
        route = jnp.where(row8 == j, val, route)
    route_out[...] = route


def _merge(x2, attn, hgrn, norm1_w, w_z, w_ba, w_bh, w_out, norm2_w, w_r, br, tm):
    T = x2.shape[0]
    row = lambda w: pl.BlockSpec((tm, w), lambda i: (i, 0))
    full = lambda a: pl.BlockSpec(a.shape, lambda i: (0,) * a.ndim)
    utri = jnp.asarray(np.triu(np.ones((tm, tm), np.float32), 1), BF16)
    ins = [x2, attn, hgrn, norm1_w[None, :], w_z, w_ba, w_bh, w_out, norm2_w[None, :], w_r, br, utri]
    in_specs = [row(D_MODEL), row(ATTN_WIDTH), row(HGRN_WIDTH)] + [full(a) for a in ins[3:]]
    return pl.pallas_call(
        _merge_kernel,
        out_shape=[jax.ShapeDtypeStruct((T, D_MODEL), F32), jax.ShapeDtypeStruct((T, HALF), jnp.uint32),
                   jax.ShapeDtypeStruct((ROUTE_ROWS, T), F32), jax.ShapeDtypeStruct((ROUTER_ROWS, tm), F32)],
        grid=(T // tm,),
        in_specs=in_specs,
        out_specs=[row(D_MODEL), row(HALF), pl.BlockSpec((ROUTE_ROWS, tm), lambda i: (0, i)),
                   pl.BlockSpec((ROUTER_ROWS, tm), lambda i: (0, 0))],
        scratch_shapes=[pltpu.VMEM((ROUTER_ROWS, tm), F32)],
        compiler_params=pltpu.CompilerParams(dimension_semantics=("arbitrary",),
                                             vmem_limit_bytes=VMEM_LIMIT),
        name="merge_router",
    )(*ins)


HALF = D_MODEL // 2


def _pack_bf16_pairs(x):
    bits = pltpu.bitcast(x.astype(BF16).astype(F32), jnp.uint32)
    return (bits[:, :HALF] >> 16) | bits[:, HALF:]


def _unpack_bf16_pairs(words):
    lo = pltpu.bitcast(words << 16, F32)
    hi = pltpu.bitcast(words & jnp.uint32(0xFFFF0000), F32)
    return lo, hi


SC_CORES = 2
SC_SUBCORES = 16
SC_CHUNK = 64


def _sc_mesh():
    return plsc.VectorSubcoreMesh(core_axis_name="c", subcore_axis_name="s",
                                  num_cores=SC_CORES, num_subcores=SC_SUBCORES)


def _sc_worker():
    return lax.axis_index("s") * SC_CORES + lax.axis_index("c")


def _dispatch(pos, pad_rows, hn, n_rows):
    T, width = hn.shape
    workers = SC_CORES * SC_SUBCORES
    per_worker = T // workers
    n_chunks = per_worker // SC_CHUNK
    assert per_worker * workers == T and n_chunks * SC_CHUNK == per_worker and n_chunks % 2 == 0
    idx = pos.T.reshape(TOP_K, workers, n_chunks, SC_CHUNK).transpose(1, 0, 2, 3)
    pad_chunks = pad_rows.size // (workers * SC_CHUNK)
    pad_idx = pad_rows.reshape(workers, pad_chunks, SC_CHUNK)
    zeros = jnp.zeros((SC_CHUNK, width), hn.dtype)

    @functools.partial(
        pl.kernel, mesh=_sc_mesh(), name="moe_dispatch",
        out_type=jax.ShapeDtypeStruct((n_rows + SC_CHUNK, width), hn.dtype),
        scratch_types=[pltpu.VMEM((TOP_K, n_chunks, SC_CHUNK), jnp.int32),
                       pltpu.VMEM((2, SC_CHUNK, width), hn.dtype),
                       pltpu.VMEM((pad_chunks, SC_CHUNK), jnp.int32),
                       pltpu.VMEM((SC_CHUNK, width), hn.dtype),
                       pltpu.SemaphoreType.DMA((2,)), pltpu.SemaphoreType.DMA((2,)), pltpu.SemaphoreType.DMA])
    def dispatch_kernel(rows_hbm, idx_hbm, pad_hbm, zeros_hbm, out_hbm,
                        idx_v, rows_v, pad_v, zeros_v, load_sem, scatter_sem, pad_sem):
        wid = _sc_worker()
        base = wid * per_worker
        pltpu.sync_copy(idx_hbm.at[wid], idx_v)
        pltpu.sync_copy(pad_hbm.at[wid], pad_v)
        pltpu.sync_copy(zeros_hbm, zeros_v)

        def zero_fill(j):
            return pltpu.make_async_copy(zeros_v, out_hbm.at[pad_v.at[j]], pad_sem)

        for j in range(pad_chunks):
            zero_fill(j).start()

        def load(c, b):
            off = pl.multiple_of(c * SC_CHUNK, SC_CHUNK)
            return pltpu.make_async_copy(rows_hbm.at[pl.ds(base + off, SC_CHUNK)], rows_v.at[b], load_sem.at[b])

        def scatter(c, b, k):
            return pltpu.make_async_copy(rows_v.at[b], out_hbm.at[idx_v.at[k, c]], scatter_sem.at[b])

        load(0, 0).start()

        @pl.loop(0, n_chunks, step=2)
        def _(c0):
            for b in range(2):
                c = c0 + b
                load(c, b).wait()
                for k in range(TOP_K):
                    scatter(c, b, k).start()

                @pl.when(c >= 1)
                def _():
                    for k in range(TOP_K):
                        scatter(c - 1, 1 - b, k).wait()

                @pl.when(c + 1 < n_chunks)
                def _():
                    load(c + 1, 1 - b).start()

        for k in range(TOP_K):
            scatter(n_chunks - 1, (n_chunks - 1) % 2, k).wait()
        for j in range(pad_chunks):
            zero_fill(j).wait()

    return dispatch_kernel(hn, idx, pad_idx, zeros)


def _expert_kernel(tile_ref, exp_ref, valid_ref, x_ref, wg_ref, wu_ref, wd_ref, y_ref, wg_s, wu_s, wd_s):
    w = pl.program_id(0)
    prev = jnp.maximum(w - 1, 0)

    @pl.when((w == 0) | (exp_ref[w] != exp_ref[prev]))
    def _():
        wg_s[...] = wg_ref[0].astype(BF16)
        wu_s[...] = wu_ref[0].astype(BF16)
        wd_s[...] = wd_ref[0].astype(BF16)

    def ffn(words):
        lo, hi = _unpack_bf16_pairs(words)
        lo = lo.astype(BF16)
        hi = hi.astype(BF16)
        gate = _dot(lo, wg_s[0:HALF, :]) + _dot(hi, wg_s[HALF:, :])
        up = _dot(lo, wu_s[0:HALF, :]) + _dot(hi, wu_s[HALF:, :])
        return _pack_bf16_pairs(_dot((gate * _sigmoid(gate) * up).astype(BF16), wd_s[...]))

    @pl.when(valid_ref[w] != 0)
    def _():
        y_ref[...] = ffn(x_ref[...])


def _experts(items, xs, w_gate, w_up, w_down):
    tile, exp, valid = items
    wspec = lambda shape: pl.BlockSpec((1,) + shape, lambda w, t, e, v: (e[w], 0, 0))
    xspec = pl.BlockSpec((EXPERT_TILE, HALF), lambda w, t, e, v: (t[w], 0))
    return pl.pallas_call(
        _expert_kernel,
        out_shape=jax.ShapeDtypeStruct((tile.shape[0] * EXPERT_TILE, HALF), jnp.uint32),
        grid_spec=pltpu.PrefetchScalarGridSpec(
            num_scalar_prefetch=3,
            grid=(tile.shape[0],),
            in_specs=[xspec, wspec((D_MODEL, EXPERT_FF)), wspec((D_MODEL, EXPERT_FF)),
                      wspec((EXPERT_FF, D_MODEL))],
            out_specs=xspec,
            scratch_shapes=[pltpu.VMEM((D_MODEL, EXPERT_FF), BF16),
                            pltpu.VMEM((D_MODEL, EXPERT_FF), BF16),
                            pltpu.VMEM((EXPERT_FF, D_MODEL), BF16)]),
        compiler_params=pltpu.CompilerParams(dimension_semantics=("arbitrary",),
                                             vmem_limit_bytes=VMEM_LIMIT),
        name="moe_experts",
    )(tile, exp, valid, xs, w_gate, w_up, w_down)


def _sc_gather_rows(table, idx):
    n = idx.shape[0]
    width = table.shape[1]
    per_worker = n // (SC_CORES * SC_SUBCORES)
    n_chunks = per_worker // SC_CHUNK
    assert per_worker * SC_CORES * SC_SUBCORES == n and n_chunks * SC_CHUNK == per_worker and n_chunks % 2 == 0

    @functools.partial(
        pl.kernel, mesh=_sc_mesh(), name="moe_row_gather",
        out_type=jax.ShapeDtypeStruct((n, width), table.dtype),
        scratch_types=[pltpu.VMEM((per_worker,), jnp.int32),
                       pltpu.VMEM((2, SC_CHUNK, width), table.dtype),
                       pltpu.SemaphoreType.DMA((2,))])
    def gather_kernel(table_hbm, idx_hbm, out_hbm, idx_v, rows_v, sem):
        base = _sc_worker() * per_worker
        pltpu.sync_copy(idx_hbm.at[pl.ds(base, per_worker)], idx_v)

        def gather(c, b):
            off = pl.multiple_of(c * SC_CHUNK, SC_CHUNK)
            return pltpu.make_async_copy(table_hbm.at[idx_v.at[pl.ds(off, SC_CHUNK)]], rows_v.at[b], sem.at[b])

        gather(0, 0).start()

        @pl.loop(0, n_chunks, step=2)
        def _(c0):
            for b in range(2):
                c = c0 + b

                @pl.when(c + 1 < n_chunks)
                def _():
                    gather(c + 1, 1 - b).start()

                gather(c, b).wait()
                off = pl.multiple_of(c * SC_CHUNK, SC_CHUNK)
                pltpu.sync_copy(rows_v.at[b], out_hbm.at[pl.ds(base + off, SC_CHUNK)])

    return gather_kernel(table, idx)


def _combine_kernel(x1_ref, route_ref, y0_ref, y1_ref, o_ref):
    gates = route_ref[...].T
    w0 = gates[:, 2:3]
    w1 = gates[:, 3:4]
    lo0, hi0 = _unpack_bf16_pairs(y0_ref[...])
    lo1, hi1 = _unpack_bf16_pairs(y1_ref[...])
    o_ref[:, 0:HALF] = x1_ref[:, 0:HALF] + w0 * lo0 + w1 * lo1
    o_ref[:, HALF:] = x1_ref[:, HALF:] + w0 * hi0 + w1 * hi1


def _combine(pos, x1, route, y, tk):
    T = x1.shape[0]
    nt = T // tk
    ysel = _sc_gather_rows(y, pos.T.reshape(-1))
    return pl.pallas_call(
        _combine_kernel,
        out_shape=jax.ShapeDtypeStruct((T, D_MODEL), F32),
        grid=(nt,),
        in_specs=[pl.BlockSpec((tk, D_MODEL), lambda i: (i, 0)),
                  pl.BlockSpec((ROUTE_ROWS, tk), lambda i: (0, i)),
                  pl.BlockSpec((tk, HALF), lambda i: (i, 0)),
                  pl.BlockSpec((tk, HALF), lambda i: (i + nt, 0))],
        out_specs=pl.BlockSpec((tk, D_MODEL), lambda i: (i, 0)),
        compiler_params=pltpu.CompilerParams(dimension_semantics=("arbitrary",),
                                             vmem_limit_bytes=VMEM_LIMIT),
        name="moe_combine",
    )(x1, route, ysel, ysel)


def _routing_tables(route, counts, T):
    e = route[0:TOP_K].astype(jnp.int32)
    rank = route[4:4 + TOP_K].astype(jnp.int32)
    counts = counts.astype(jnp.int32)
    padded = (counts + EXPERT_TILE - 1) // EXPERT_TILE * EXPERT_TILE
    ends = jnp.cumsum(padded)
    starts = ends - padded
    ids = jnp.arange(N_EXPERTS, dtype=jnp.int32)
    pos = rank + jnp.sum(jnp.where(e[:, :, None] == ids, starts, 0), axis=-1)
    n_tiles = TOP_K * T // EXPERT_TILE + N_EXPERTS
    tile0 = jnp.arange(n_tiles, dtype=jnp.int32)
    valid = (tile0 * EXPERT_TILE < ends[-1]).astype(jnp.int32)
    tile = jnp.minimum(tile0, ends[-1] // EXPERT_TILE - 1)
    exp = jnp.minimum(jnp.sum(ends[None, :] <= (tile * EXPERT_TILE)[:, None], axis=1), N_EXPERTS - 1).astype(jnp.int32)
    j = jnp.arange(EXPERT_TILE, dtype=jnp.int32)[None, :]
    spare = n_tiles * EXPERT_TILE + j % SC_CHUNK
    pad_rows = jnp.where(j < (padded - counts)[:, None], (starts + counts)[:, None] + j, spare)
    return pos.T, pad_rows, (tile, exp, valid)


def _pick_tile(n, pref):
    t = pref
    while n % t:
        t //= 2
    return t


def kernel(x, positions, norm1_w, w_in, q_norm_w, k_norm_w, attn_sinks, hgrn_lower_bounds, hgrn_norm_w,
           w_branch_attn, w_branch_hgrn, w_out, norm2_w, w_router_group, b_router_group, w_router_expert,
           b_router_expert, w_gate_experts, w_up_experts, w_down_experts):
    B, S, D = x.shape
    T = B * S
    x2 = x.reshape(T, D)
    tm = _pick_tile(T, 512)

    inv_freq = ROPE_THETA ** (-jnp.arange(0, ROT_DIM, 2, dtype=F32) / ROT_DIM)
    ang = inv_freq[:, None] * positions.astype(F32).reshape(1, T)
    cs = jnp.concatenate(_split3(jnp.concatenate([jnp.cos(ang), jnp.sin(ang)], axis=0)), axis=0)

    w_in0 = w_in[0]
    heads = lambda off: [w_in0[:, off + h * HEAD_DIM:off + (h + 1) * HEAD_DIM]
                         for h in range(N_KV_HEADS) for _ in range(2)]
    w_in_a = jnp.concatenate([w_in0[:, :_OFF_K]] + heads(_OFF_K) + heads(_OFF_V) + [w_in0[:, _OFF_HQ:_OFF_Z]],
                             axis=1).astype(BF16)
    w_z = w_in0[:, _OFF_Z:].astype(BF16)

    q, k, v, hq, lf, hv, hg = _inproj(x2, norm1_w[0], w_in_a, cs, q_norm_w[0], k_norm_w[0],
                                      hgrn_lower_bounds.astype(F32), tm)
    attn = _attention(q, k, v, attn_sinks[0].astype(F32), B, S)
    hgrn = _hgrn(hq, lf, hv, hg, hgrn_norm_w[0], B, S, _pick_tile(S, 512))

    pad = ROUTER_ROWS - N_GROUPS - N_EXPERTS
    w_r = jnp.concatenate([w_router_group[0].T, w_router_expert[0].T, jnp.zeros((pad, D), F32)], axis=0)
    b_r =jnp.concatenate([b_router_group[0], b_router_expert[0], jnp.zeros((pad,), F32)]).astype(F32)
    b_r = jnp.broadcast_to(b_r[:, None], (ROUTER_ROWS, tm))

    x1, hn, route, counts = _merge(x2, attn, hgrn, norm1_w[0], w_z, w_branch_attn[0].astype(BF16),
                                   w_branch_hgrn[0].astype(BF16), w_out[0].astype(BF16), norm2_w[0],
                                   w_r.astype(BF16), b_r, tm)

    pos, pad_rows, items = _routing_tables(route, counts[N_GROUPS:N_GROUPS + N_EXPERTS, 0], T)
    tk = _pick_tile(T, 512)
    xs = _dispatch(pos, pad_rows, hn, items[0].shape[0] * EXPERT_TILE)
    y = _experts(items, xs, w_gate_experts[0], w_up_experts[0], w_down_experts[0])
    out = _combine(pos, x1, route, y, tk)
    return out.reshape(B, S, D)
```

```python
import functools

import numpy as np
import jax
import jax.numpy as jnp
from jax import lax
from jax.experimental import pallas as pl
from jax.experimental.pallas import tpu as pltpu
from jax.experimental.pallas import tpu_sc as plsc

F32 = jnp.float32
BF16 = jnp.bfloat16

D_MODEL = 1024
N_Q_HEADS = 8
N_KV_HEADS = 2
GROUP = N_Q_HEADS // N_KV_HEADS
HEAD_DIM = 64
ROT_DIM = HEAD_DIM // 4
ROT_HALF = ROT_DIM // 2
ROPE_THETA = 500000.0
WINDOW = 128
ATTN_WIDTH = N_Q_HEADS * HEAD_DIM
KV_WIDTH = N_KV_HEADS * HEAD_DIM

HGRN_HEADS = 4
HGRN_DIM = 128
HGRN_WIDTH = HGRN_HEADS * HGRN_DIM
CHUNK = 64

N_GROUPS = 4
EXPERTS_PER_GROUP = 8
N_EXPERTS = N_GROUPS * EXPERTS_PER_GROUP
TOP_K = 2
EXPERT_FF = 512
MOE_BLOCK = 128
EXPERT_TILE = 512
NORM_EPS = 1e-6
ROUTER_ROWS = 40
ROUTE_ROWS = 8

LANES = 128
NEG_BIG = -1e30
LOG2_E = 1.4426950408889634

_OFF_Q, _OFF_K, _OFF_V = 0, ATTN_WIDTH, ATTN_WIDTH + KV_WIDTH
_OFF_HQ = ATTN_WIDTH + 2 * KV_WIDTH
_OFF_HF = _OFF_HQ + HGRN_WIDTH
_OFF_HI = _OFF_HF + HGRN_WIDTH
_OFF_HG = _OFF_HI + HGRN_WIDTH
_OFF_Z = _OFF_HG + HGRN_WIDTH
_A_Q, _A_K, _A_V = 0, ATTN_WIDTH, ATTN_WIDTH + 2 * KV_WIDTH
_A_HQ = ATTN_WIDTH + 4 * KV_WIDTH
_A_HF = _A_HQ + HGRN_WIDTH
_A_HI = _A_HF + HGRN_WIDTH
_A_HG = _A_HI + HGRN_WIDTH

VMEM_LIMIT = 56 * 1024 * 1024


def _split3(a):
    hi = a.astype(BF16)
    r1 = a - hi.astype(F32)
    mid = r1.astype(BF16)
    lo = (r1 - mid.astype(F32)).astype(BF16)
    return hi, mid, lo


def _dot(a, b):
    return jnp.dot(a, b, preferred_element_type=F32)


def _dot_nt(a, b):
    return lax.dot_general(a, b, (((1,), (1,)), ((), ())), preferred_element_type=F32)


def _dot_tn(a, b):
    return lax.dot_general(a, b, (((0,), (0,)), ((), ())), preferred_element_type=F32)


def _sigmoid(x):
    return 1.0 / (1.0 + jnp.exp(-x))


def _rms(x, w):
    ms = jnp.mean(x * x, axis=-1, keepdims=True)
    return x * lax.rsqrt(ms + NORM_EPS) * w


def _inproj_kernel(x_ref, n1w_ref, w_ref, cs_ref, rope_e_ref, rope_c0_ref, qw_ref, kw_ref,
                   mq_ref, mk_ref, lbp_ref,
                   q_out, k_out, v_out, hq_out, lf_out, hv_out, hg_out):
    xn = _rms(x_ref[...], n1w_ref[...]).astype(BF16)

    def proj(off, width):
        return _dot(xn, w_ref[:, off:off + width])

    tabs = _dot_tn(cs_ref[...], rope_e_ref[...])
    c_tab = tabs[:, 0:LANES] + rope_c0_ref[...]
    s1_tab = tabs[:, LANES:2 * LANES]
    s2_tab = tabs[:, 2 * LANES:3 * LANES]

    def norm_rope(t, mavg_ref, w_row, scale):
        ms = _dot((t * t).astype(BF16), mavg_ref[...])
        tn = t * lax.rsqrt(ms + NORM_EPS) * w_row
        if scale != 1.0:
            tn = tn * scale
        outs = []
        for j in range(t.shape[1] // LANES):
            c = tn[:, j * LANES:(j + 1) * LANES]
            outs.append(c * c_tab
                        + pltpu.roll(c, LANES - ROT_HALF, 1) * s1_tab
                        + pltpu.roll(c, ROT_HALF, 1) * s2_tab)
        return outs[0] if len(outs) == 1 else jnp.concatenate(outs, axis=1)

    q_out[...] = norm_rope(proj(_A_Q, ATTN_WIDTH), mq_ref, qw_ref[...], HEAD_DIM ** -0.5).astype(BF16)
    k_out[...] = norm_rope(proj(_A_K, 2 * KV_WIDTH), mk_ref, kw_ref[...], 1.0).astype(BF16)
    v_out[...] = proj(_A_V, 2 * KV_WIDTH).astype(BF16)

    hq = proj(_A_HQ, HGRN_WIDTH)
    hq_out[...] = (hq * _sigmoid(hq)).astype(BF16)
    h0 = lbp_ref[0:1, :]
    h1 = lbp_ref[1:2, :]
    hm = jnp.maximum(h0, h1)
    e0 = jnp.exp(h0 - hm)
    e1 = jnp.exp(h1 - hm)
    lb = e0 / (e0 + e1)
    fg = lb + (1.0 - lb) * _sigmoid(proj(_A_HF, HGRN_WIDTH))
    lf_out[...] = jnp.log(fg) * LOG2_E
    hv_out[...] = proj(_A_HI, HGRN_WIDTH).astype(BF16)
    hg = proj(_A_HG, HGRN_WIDTH)
    hg_out[...] = (hg * _sigmoid(hg)).astype(BF16)


def _rope_constants():
    e = np.zeros((2 * ROT_HALF, 3 * LANES), np.float32)
    c0 = np.zeros((1, LANES), np.float32)
    for lane in range(LANES):
        d = lane % HEAD_DIM
        if d < ROT_HALF:
            e[d, lane] = 1.0
            e[ROT_HALF + d, LANES + lane] = -1.0
        elif d < ROT_DIM:
            e[d - ROT_HALF, lane] = 1.0
            e[ROT_HALF + d - ROT_HALF, 2 * LANES + lane] = 1.0
        else:
            c0[0, lane] = 1.0
    return jnp.asarray(np.concatenate([e, e, e], axis=0), BF16), jnp.asarray(c0, F32)


def _head_mean_matrix(width):
    idx = np.arange(width) // HEAD_DIM
    m = (idx[:, None] == idx[None, :]).astype(np.float32) / HEAD_DIM
    return jnp.asarray(m, BF16)


def _inproj(x2, norm1_w, w_in_a, cs, q_norm_w, k_norm_w, lbp, tm):
    T = x2.shape[0]
    rope_e, rope_c0 = _rope_constants()
    qw = jnp.tile(q_norm_w.astype(F32), N_Q_HEADS)[None, :]
    kw = jnp.tile(k_norm_w.astype(F32), 2 * N_KV_HEADS)[None, :]
    mq = _head_mean_matrix(ATTN_WIDTH)
    mk = _head_mean_matrix(2 * KV_WIDTH)
    row = lambda w: pl.BlockSpec((tm, w), lambda i: (i, 0))
    full = lambda a: pl.BlockSpec(a.shape, lambda i: (0,) * a.ndim)
    ins = [x2, norm1_w[None, :], w_in_a, cs, rope_e, rope_c0, qw, kw, mq, mk, lbp]
    in_specs = ([row(D_MODEL), full(ins[1]), full(w_in_a), pl.BlockSpec((cs.shape[0], tm), lambda i: (0, i))]
                + [full(a) for a in ins[4:]])
    outs = [(ATTN_WIDTH, BF16), (2 * KV_WIDTH, BF16), (2 * KV_WIDTH, BF16), (HGRN_WIDTH, BF16),
            (HGRN_WIDTH, F32), (HGRN_WIDTH, BF16), (HGRN_WIDTH, BF16)]
    return pl.pallas_call(
        _inproj_kernel,
        out_shape=[jax.ShapeDtypeStruct((T, w), dt) for w, dt in outs],
        grid=(T // tm,),
        in_specs=in_specs,
        out_specs=[row(w) for w, _ in outs],
        compiler_params=pltpu.CompilerParams(dimension_semantics=("arbitrary",),
                                             vmem_limit_bytes=VMEM_LIMIT),
        name="inproj",
    )(*ins)


ATTN_QBLOCKS = 8


def _attn_kernel(sink_ref, q_ref, kc_ref, kp_ref, vc_ref, vp_ref, half_ref, o_ref):
    n_qblocks = q_ref.shape[0] // WINDOW
    has_prev = pl.program_id(1) > 0
    qi = lax.broadcasted_iota(jnp.int32, (WINDOW, 2 * WINDOW), 0)
    kj = lax.broadcasted_iota(jnp.int32, (WINDOW, 2 * WINDOW), 1)
    in_window = ((kj < WINDOW) & (kj > qi)) | ((kj >= WINDOW) & (kj - WINDOW <= qi))
    first_valid = in_window & ((kj >= WINDOW) | has_prev)
    left = lax.broadcasted_iota(jnp.int32, (WINDOW, LANES), 1) < HEAD_DIM
    half = (half_ref[0], half_ref[1])

    ks, rhs = [], []
    for h in range(N_KV_HEADS):
        cols = slice(h * LANES, (h + 1) * LANES)
        kall = jnp.concatenate([kp_ref[:, cols], kc_ref[:, cols]], axis=0)
        vall = jnp.concatenate([vp_ref[:, cols], vc_ref[:, cols]], axis=0)
        ks.append([kall * hm for hm in half])
        rhs.append([jnp.concatenate([vall * hm, hm], axis=1) for hm in half])

    units = [(j, h, pr, side) for j in range(n_qblocks) for h in range(N_KV_HEADS)
             for pr in range(GROUP // 2) for side in range(2)]
    scores = []
    for j, h, pr, side in units:
        pair = h * (GROUP // 2) + pr
        qp = q_ref[j * WINDOW:(j + 1) * WINDOW, pair * LANES:(pair + 1) * LANES]
        s = _dot_nt(qp, ks[h][side][j * WINDOW:(j + 2) * WINDOW])
        scores.append(jnp.where(first_valid if j == 0 else in_window, s, NEG_BIG))
    probs, sink_terms = [], []
    for (j, h, pr, side), s in zip(units, scores):
        sink = sink_ref[2 * (h * (GROUP // 2) + pr) + side]
        m = jnp.maximum(jnp.max(s, axis=-1, keepdims=True), sink)
        probs.append(jnp.exp(s - m).astype(BF16))
        sink_terms.append(jnp.exp(sink - m))
    acc = []
    for (j, h, pr, side), p in zip(units, probs):
        acc.append(_dot(p, rhs[h][side][j * WINDOW:(j + 2) * WINDOW]))
    for j in range(n_qblocks):
        outs = []
        for u in range(0, len(units), 2):
            if units[u][0] == j:
                both = acc[u] + acc[u + 1]
                den = both[:, LANES:2 * LANES] + jnp.where(left, sink_terms[u], sink_terms[u + 1])
                outs.append(both[:, 0:LANES] / den)
        o_ref[j * WINDOW:(j + 1) * WINDOW, :] = jnp.concatenate(outs, axis=1).astype(BF16)


def _attention(q, k, v, sinks, B, S):
    qblocks = _pick_tile(S // WINDOW, ATTN_QBLOCKS)
    rows = qblocks * WINDOW
    nb = S // rows
    cur = lambda b, n: (b * nb + n, 0)
    prev = lambda b, n: (jnp.maximum((b * nb + n) * qblocks - 1, 0), 0)
    lane_left = np.arange(LANES) < HEAD_DIM
    half = jnp.asarray(np.broadcast_to(np.stack([lane_left, ~lane_left])[:, None, :],
                                       (2, rows + WINDOW, LANES)), BF16)
    return pl.pallas_call(
        _attn_kernel,
        out_shape=jax.ShapeDtypeStruct((B * S, ATTN_WIDTH), BF16),
        grid=(B, nb),
        in_specs=[pl.BlockSpec(memory_space=pltpu.SMEM),
                  pl.BlockSpec((rows, ATTN_WIDTH), cur),
                  pl.BlockSpec((rows, 2 * KV_WIDTH), cur),
                  pl.BlockSpec((WINDOW, 2 * KV_WIDTH), prev),
                  pl.BlockSpec((rows, 2 * KV_WIDTH), cur),
                  pl.BlockSpec((WINDOW, 2 * KV_WIDTH), prev),
                  pl.BlockSpec(half.shape, lambda b, n: (0, 0, 0))],
        out_specs=pl.BlockSpec((rows, ATTN_WIDTH), cur),
        compiler_params=pltpu.CompilerParams(dimension_semantics=("arbitrary", "arbitrary"),
                                             vmem_limit_bytes=VMEM_LIMIT),
        name="swa_attention",
    )(sinks, q, k, k, v, v, half)


_LEVEL_HALVES = (1, 2, 4, 8, 16, 32)
HGRN_UNIT_GROUP = 32


def _hgrn_level_masks():
    t = np.arange(CHUNK)[:, None]
    s = np.arange(CHUNK)[None, :]
    masks = [((t // (2 * h)) == (s // (2 * h))) & ((t & h) != 0) & ((s & h) == 0) for h in _LEVEL_HALVES]
    return jnp.asarray(np.stack(masks), F32)


def _level_reference(b_ref, slot, half):
    if half >= 4:
        span = max(2 * half, 8)
        pieces = [jnp.broadcast_to(b_ref[slot, s + half - 1:s + half, :], (span, HGRN_DIM))
                  for s in range(0, CHUNK, span)]
    else:
        r8 = lax.broadcasted_iota(jnp.int32, (8, HGRN_DIM), 0)
        pieces = [jnp.where(r8 < 4,
                            jnp.broadcast_to(b_ref[slot, s + 1:s + 2, :], (8, HGRN_DIM)),
                            jnp.broadcast_to(b_ref[slot, s + 5:s + 6, :], (8, HGRN_DIM)))
                  for s in range(0, CHUNK, 8)]
    return pieces[0] if len(pieces) == 1 else jnp.concatenate(pieces, axis=0)


def _hgrn_kernel(hq_ref, lf_ref, hv_ref, hg_ref, nw_ref, tri_ref, lm_ref, o_ref, st_ref, b_ref):
    @pl.when(pl.program_id(1) == 0)
    def _():
        st_ref[...] = jnp.zeros_like(st_ref)

    tri2 = tri_ref[...]
    odd = (lax.broadcasted_iota(jnp.int32, (CHUNK, HGRN_DIM), 0) & 1) != 0
    masks = [lm_ref[li] != 0.0 for li in range(len(_LEVEL_HALVES))]
    units = [(c, h) for c in range(hq_ref.shape[0] // CHUNK) for h in range(HGRN_HEADS)]
    sl = lambda c, h: (slice(c * CHUNK, (c + 1) * CHUNK), slice(h * HGRN_DIM, (h + 1) * HGRN_DIM))

    bs = []
    for u, (c, h) in enumerate(units):
        lf2 = lf_ref[sl(c, h)]
        hi = lf2.astype(BF16)
        lo = (lf2 - hi.astype(F32)).astype(BF16)
        b = _dot(tri2, jnp.concatenate([hi, lo], axis=0))
        b_ref[u] = b
        bs.append(b)

    outs, qs, ks, fs = [], [], [], []
    for u, (c, h) in enumerate(units):
        b = bs[u]
        q = hq_ref[sl(c, h)].astype(F32)
        v_bf = hv_ref[sl(c, h)]
        f = jnp.exp2(lf_ref[sl(c, h)])
        k = 1.0 - f
        b_last = b[CHUNK - 1:CHUNK, :]
        st = st_ref[h]
        o = _dot_nt((q * jnp.exp2(b)).astype(BF16), st.astype(BF16))
        k_out = (k * jnp.exp2(b_last - b)).astype(BF16)
        st_ref[h] = st * jnp.exp2(b_last) + _dot_tn(v_bf, k_out)
        outs.append(o + jnp.sum(q * k, axis=-1, keepdims=True) * v_bf.astype(F32))
        qs.append(q)
        ks.append(k)
        fs.append(f)

    for g0 in range(0, len(units), HGRN_UNIT_GROUP):
        group = range(g0, min(g0 + HGRN_UNIT_GROUP, len(units)))
        accs = {u: jnp.zeros((CHUNK, CHUNK), F32) for u in group}
        for li, half in enumerate(_LEVEL_HALVES):
            for u in group:
                if half == 1:
                    e = jnp.where(odd, fs[u], 1.0)
                else:
                    e = jnp.exp2(-jnp.abs(bs[u] - _level_reference(b_ref, u, half)))
                accs[u] = jnp.where(masks[li], _dot_nt((qs[u] * e).astype(BF16), (ks[u] * e).astype(BF16)),
                                    accs[u])
        for u in group:
            c, h = units[u]
            o = outs[u] + _dot(accs[u].astype(BF16), hv_ref[sl(c, h)])
            y = _rms(o, nw_ref[...]) * hg_ref[sl(c, h)].astype(F32)
            o_ref[sl(c, h)] = y.astype(BF16)


def _hgrn(hq, lf, hv, hg, hgrn_norm_w, B, S, tb):
    nt = S // tb
    blk = pl.BlockSpec((tb, HGRN_WIDTH), lambda b, n: (b * nt + n, 0))
    tri = np.tril(np.ones((CHUNK, CHUNK), np.float32))
    tri = jnp.asarray(np.concatenate([tri, tri], axis=1), BF16)
    lm = _hgrn_level_masks()
    return pl.pallas_call(
        _hgrn_kernel,
        out_shape=jax.ShapeDtypeStruct((B * S, HGRN_WIDTH), BF16),
        grid=(B, nt),
        in_specs=[blk, blk, blk, blk,
                  pl.BlockSpec((1, HGRN_DIM), lambda b, n: (0, 0)),
                  pl.BlockSpec(tri.shape, lambda b, n: (0, 0)),
                  pl.BlockSpec(lm.shape, lambda b, n: (0, 0, 0))],
        out_specs=blk,
        scratch_shapes=[pltpu.VMEM((HGRN_HEADS, HGRN_DIM, HGRN_DIM), F32),
                        pltpu.VMEM((tb // CHUNK * HGRN_HEADS, CHUNK, HGRN_DIM), F32)],
        compiler_params=pltpu.CompilerParams(dimension_semantics=("arbitrary", "arbitrary"),
                                             vmem_limit_bytes=VMEM_LIMIT),
        name="hgrn2",
    )(hq, lf, hv, hg, hgrn_norm_w[None, :].astype(F32), tri, lm)


def _merge_kernel(x_ref, attn_ref, hgrn_ref, n1w_ref, wz_ref, wba_ref, wbh_ref, wout_ref, n2w_ref,
                  wr_ref, br_ref, utri_ref, x1_out, hn_out, route_out, count_out, run_ref):
    @pl.when(pl.program_id(0) == 0)
    def _():
        run_ref[...] = jnp.zeros_like(run_ref)

    x = x_ref[...]
    xn = _rms(x, n1w_ref[...]).astype(BF16)
    za = _sigmoid(_dot(xn, wz_ref[:, 0:D_MODEL]))
    zb = _sigmoid(_dot(xn, wz_ref[:, D_MODEL:2 * D_MODEL]))
    mixed = za * _dot(attn_ref[...], wba_ref[...]) + zb * _dot(hgrn_ref[...], wbh_ref[...])
    x1 = x + _dot(mixed.astype(BF16), wout_ref[...])
    x1_out[...] = x1
    hn = _rms(x1, n2w_ref[...])
    hn_out[...] = _pack_bf16_pairs(hn)

    logits = _dot_nt(wr_ref[...], hn.astype(BF16)) + br_ref[...]
    r = lax.broadcasted_iota(jnp.int32, logits.shape, 0).astype(F32)
    far = float(ROUTER_ROWS)
    cmax = lambda a: jnp.max(a, axis=0, keepdims=True)
    cmin = lambda a: jnp.min(a, axis=0, keepdims=True)
    csum = lambda a: jnp.sum(a, axis=0, keepdims=True)

    lg = jnp.where(r < N_GROUPS, logits, NEG_BIG)
    mg = cmax(lg)
    gsel = cmin(jnp.where(lg == mg, r, far))
    pgsel = 1.0 / csum(jnp.exp(lg - mg))

    lo = N_GROUPS + EXPERTS_PER_GROUP * gsel
    le = jnp.where((r >= lo) & (r < lo + EXPERTS_PER_GROUP), logits, NEG_BIG)
    m1 = cmax(le)
    i1 = cmin(jnp.where(le == m1, r, far))
    se = csum(jnp.exp(le - m1))
    le2 = jnp.where(r == i1, NEG_BIG, le)
    m2 = cmax(le2)
    i2 = cmin(jnp.where(le2 == m2, r, far))
    top0 = 1.0 / se
    top1 = jnp.exp(m2 - m1) / se
    tsum = top0 + top1
    w0 = pgsel * top0 / tsum
    w1 = pgsel * top1 / tsum

    sel1 = r == i1
    sel2 = r == i2
    onehot = jnp.where(sel1 | sel2, 1.0, 0.0)
    before = _dot(onehot.astype(BF16), utri_ref[...]) + run_ref[...]
    r0 = csum(jnp.where(sel1, before, 0.0))
    r1 = csum(jnp.where(sel2, before, 0.0))
    run_new = run_ref[...] + jnp.sum(onehot, axis=1, keepdims=True)
    run_ref[...] = run_new
    count_out[...] = run_new

    row8 = lax.broadcasted_iota(jnp.int32, route_out.shape, 0)
    vals = (i1 - N_GROUPS, i2 - N_GROUPS, w0, w1, r0, r1)
    route = jnp.zeros(route_out.shape, F32)
    for j, val in enumerate(vals):
        route = jnp.where(row8 == j, val, route)
    route_out[...] = route


def _merge(x2, attn, hgrn, norm1_w, w_z, w_ba, w_bh, w_out, norm2_w, w_r, br, tm):
    T = x2.shape[0]
    row = lambda w: pl.BlockSpec((tm, w), lambda i: (i, 0))
    full = lambda a: pl.BlockSpec(a.shape, lambda i: (0,) * a.ndim)
    utri = jnp.asarray(np.triu(np.ones((tm, tm), np.float32), 1), BF16)
    ins = [x2, attn, hgrn, norm1_w[None, :], w_z, w_ba, w_bh, w_out, norm2_w[None, :], w_r, br, utri]
    in_specs = [row(D_MODEL), row(ATTN_WIDTH), row(HGRN_WIDTH)] + [full(a) for a in ins[3:]]
    return pl.pallas_call(
        _merge_kernel,
        out_shape=[jax.ShapeDtypeStruct((T, D_MODEL), F32), jax.ShapeDtypeStruct((T, HALF), jnp.uint32),
                   jax.ShapeDtypeStruct((ROUTE_ROWS, T), F32), jax.ShapeDtypeStruct((ROUTER_ROWS, tm), F32)],
        grid=(T // tm,),
        in_specs=in_specs,
        out_specs=[row(D_MODEL), row(HALF), pl.BlockSpec((ROUTE_ROWS, tm), lambda i: (0, i)),
                   pl.BlockSpec((ROUTER_ROWS, tm), lambda i: (0, 0))],
        scratch_shapes=[pltpu.VMEM((ROUTER_ROWS, tm), F32)],
        compiler_params=pltpu.CompilerParams(dimension_semantics=("arbitrary",),
                                             vmem_limit_bytes=VMEM_LIMIT),
        name="merge_router",
    )(*ins)


HALF = D_MODEL // 2


def _pack_bf16_pairs(x):
    bits = pltpu.bitcast(x.astype(BF16).astype(F32), jnp.uint32)
    return (bits[:, :HALF] >> 16) | bits[:, HALF:]


def _unpack_bf16_pairs(words):
    lo = pltpu.bitcast(words << 16, F32)
    hi = pltpu.bitcast(words & jnp.uint32(0xFFFF0000), F32)
    return lo, hi


SC_CORES = 2
SC_SUBCORES = 16
SC_CHUNK = 64


def _sc_mesh():
    return plsc.VectorSubcoreMesh(core_axis_name="c", subcore_axis_name="s",
                                  num_cores=SC_CORES, num_subcores=SC_SUBCORES)


def _sc_worker():
    return lax.axis_index("s") * SC_CORES + lax.axis_index("c")


def _dispatch(pos, pad_rows, hn, n_rows):
    T, width = hn.shape
    workers = SC_CORES * SC_SUBCORES
    per_worker = T // workers
    n_chunks = per_worker // SC_CHUNK
    assert per_worker * workers == T and n_chunks * SC_CHUNK == per_worker and n_chunks % 2 == 0
    idx = pos.T.reshape(TOP_K, workers, n_chunks, SC_CHUNK).transpose(1, 0, 2, 3)
    pad_chunks = pad_rows.size // (workers * SC_CHUNK)
    pad_idx = pad_rows.reshape(workers, pad_chunks, SC_CHUNK)
    zeros = jnp.zeros((SC_CHUNK, width), hn.dtype)

    @functools.partial(
        pl.kernel, mesh=_sc_mesh(), name="moe_dispatch",
        out_type=jax.ShapeDtypeStruct((n_rows + SC_CHUNK, width), hn.dtype),
        scratch_types=[pltpu.VMEM((TOP_K, n_chunks, SC_CHUNK), jnp.int32),
                       pltpu.VMEM((2, SC_CHUNK, width), hn.dtype),
                       pltpu.VMEM((pad_chunks, SC_CHUNK), jnp.int32),
                       pltpu.VMEM((SC_CHUNK, width), hn.dtype),
                       pltpu.SemaphoreType.DMA((2,)), pltpu.SemaphoreType.DMA((2,)), pltpu.SemaphoreType.DMA])
    def dispatch_kernel(rows_hbm, idx_hbm, pad_hbm, zeros_hbm, out_hbm,
                        idx_v, rows_v, pad_v, zeros_v, load_sem, scatter_sem, pad_sem):
        wid = _sc_worker()
        base = wid * per_worker
        pltpu.sync_copy(idx_hbm.at[wid], idx_v)
        pltpu.sync_copy(pad_hbm.at[wid], pad_v)
        pltpu.sync_copy(zeros_hbm, zeros_v)

        def zero_fill(j):
            return pltpu.make_async_copy(zeros_v, out_hbm.at[pad_v.at[j]], pad_sem)

        for j in range(pad_chunks):
            zero_fill(j).start()

        def load(c, b):
            off = pl.multiple_of(c * SC_CHUNK, SC_CHUNK)
            return pltpu.make_async_copy(rows_hbm.at[pl.ds(base + off, SC_CHUNK)], rows_v.at[b], load_sem.at[b])

        def scatter(c, b, k):
            return pltpu.make_async_copy(rows_v.at[b], out_hbm.at[idx_v.at[k, c]], scatter_sem.at[b])

        load(0, 0).start()

        @pl.loop(0, n_chunks, step=2)
        def _(c0):
            for b in range(2):
                c = c0 + b
                load(c, b).wait()
                for k in range(TOP_K):
                    scatter(c, b, k).start()

                @pl.when(c >= 1)
                def _():
                    for k in range(TOP_K):
                        scatter(c - 1, 1 - b, k).wait()

                @pl.when(c + 1 < n_chunks)
                def _():
                    load(c + 1, 1 - b).start()

        for k in range(TOP_K):
            scatter(n_chunks - 1, (n_chunks - 1) % 2, k).wait()
        for j in range(pad_chunks):
            zero_fill(j).wait()

    return dispatch_kernel(hn, idx, pad_idx, zeros)


def _expert_kernel(tile_ref, exp_ref, valid_ref, x_ref, wg_ref, wu_ref, wd_ref, y_ref, wg_s, wu_s, wd_s):
    w = pl.program_id(0)
    prev = jnp.maximum(w - 1, 0)

    @pl.when((w == 0) | (exp_ref[w] != exp_ref[prev]))
    def _():
        wg_s[...] = wg_ref[0].astype(BF16)
        wu_s[...] = wu_ref[0].astype(BF16)
        wd_s[...] = wd_ref[0].astype(BF16)

    def ffn(words):
        lo, hi = _unpack_bf16_pairs(words)
        lo = lo.astype(BF16)
        hi = hi.astype(BF16)
        gate = _dot(lo, wg_s[0:HALF, :]) + _dot(hi, wg_s[HALF:, :])
        up = _dot(lo, wu_s[0:HALF, :]) + _dot(hi, wu_s[HALF:, :])
        return _pack_bf16_pairs(_dot((gate * _sigmoid(gate) * up).astype(BF16), wd_s[...]))

    @pl.when(valid_ref[w] != 0)
    def _():
        y_ref[...] = ffn(x_ref[...])


def _experts(items, xs, w_gate, w_up, w_down):
    tile, exp, valid = items
    wspec = lambda shape: pl.BlockSpec((1,) + shape, lambda w, t, e, v: (e[w], 0, 0))
    xspec = pl.BlockSpec((EXPERT_TILE, HALF), lambda w, t, e, v: (t[w], 0))
    return pl.pallas_call(
        _expert_kernel,
        out_shape=jax.ShapeDtypeStruct((tile.shape[0] * EXPERT_TILE, HALF), jnp.uint32),
        grid_spec=pltpu.PrefetchScalarGridSpec(
            num_scalar_prefetch=3,
            grid=(tile.shape[0],),
            in_specs=[xspec, wspec((D_MODEL, EXPERT_FF)), wspec((D_MODEL, EXPERT_FF)),
                      wspec((EXPERT_FF, D_MODEL))],
            out_specs=xspec,
            scratch_shapes=[pltpu.VMEM((D_MODEL, EXPERT_FF), BF16),
                            pltpu.VMEM((D_MODEL, EXPERT_FF), BF16),
                            pltpu.VMEM((EXPERT_FF, D_MODEL), BF16)]),
        compiler_params=pltpu.CompilerParams(dimension_semantics=("arbitrary",),
                                             vmem_limit_bytes=VMEM_LIMIT),
        name="moe_experts",
    )(tile, exp, valid, xs, w_gate, w_up, w_down)


def _sc_gather_rows(table, idx):
    n = idx.shape[0]
    width = table.shape[1]
    per_worker = n // (SC_CORES * SC_SUBCORES)
    n_chunks = per_worker // SC_CHUNK
    assert per_worker * SC_CORES * SC_SUBCORES == n and n_chunks * SC_CHUNK == per_worker and n_chunks % 2 == 0

    @functools.partial(
        pl.kernel, mesh=_sc_mesh(), name="moe_row_gather",
        out_type=jax.ShapeDtypeStruct((n, width), table.dtype),
        scratch_types=[pltpu.VMEM((per_worker,), jnp.int32),
                       pltpu.VMEM((2, SC_CHUNK, width), table.dtype),
                       pltpu.SemaphoreType.DMA((2,))])
    def gather_kernel(table_hbm, idx_hbm, out_hbm, idx_v, rows_v, sem):
        base = _sc_worker() * per_worker
        pltpu.sync_copy(idx_hbm.at[pl.ds(base, per_worker)], idx_v)

        def gather(c, b):
            off = pl.multiple_of(c * SC_CHUNK, SC_CHUNK)
            return pltpu.make_async_copy(table_hbm.at[idx_v.at[pl.ds(off, SC_CHUNK)]], rows_v.at[b], sem.at[b])

        gather(0, 0).start()

        @pl.loop(0, n_chunks, step=2)
        def _(c0):
            for b in range(2):
                c = c0 + b

                @pl.when(c + 1 < n_chunks)
                def _():
                    gather(c + 1, 1 - b).start()

                gather(c, b).wait()
                off = pl.multiple_of(c * SC_CHUNK, SC_CHUNK)
                pltpu.sync_copy(rows_v.at[b], out_hbm.at[pl.ds(base + off, SC_CHUNK)])

    return gather_kernel(table, idx)


def _combine_kernel(x1_ref, route_ref, y0_ref, y1_ref, o_ref):
    gates = route_ref[...].T
    w0 = gates[:, 2:3]
    w1 = gates[:, 3:4]
    lo0, hi0 = _unpack_bf16_pairs(y0_ref[...])
    lo1, hi1 = _unpack_bf16_pairs(y1_ref[...])
    o_ref[:, 0:HALF] = x1_ref[:, 0:HALF] + w0 * lo0 + w1 * lo1
    o_ref[:, HALF:] = x1_ref[:, HALF:] + w0 * hi0 + w1 * hi1


def _combine(pos, x1, route, y, tk):
    T = x1.shape[0]
    nt = T // tk
    ysel = _sc_gather_rows(y, pos.T.reshape(-1))
    return pl.pallas_call(
        _combine_kernel,
        out_shape=jax.ShapeDtypeStruct((T, D_MODEL), F32),
        grid=(nt,),
        in_specs=[pl.BlockSpec((tk, D_MODEL), lambda i: (i, 0)),
                  pl.BlockSpec((ROUTE_ROWS, tk), lambda i: (0, i)),
                  pl.BlockSpec((tk, HALF), lambda i: (i, 0)),
                  pl.BlockSpec((tk, HALF), lambda i: (i + nt, 0))],
        out_specs=pl.BlockSpec((tk, D_MODEL), lambda i: (i, 0)),
        compiler_params=pltpu.CompilerParams(dimension_semantics=("arbitrary",),
                                             vmem_limit_bytes=VMEM_LIMIT),
        name="moe_combine",
    )(x1, route, ysel, ysel)


def _routing_tables(route, counts, T):
    e = route[0:TOP_K].astype(jnp.int32)
    rank = route[4:4 + TOP_K].astype(jnp.int32)
    counts = counts.astype(jnp.int32)
    padded = (counts + EXPERT_TILE - 1) // EXPERT_TILE * EXPERT_TILE
    ends = jnp.cumsum(padded)
    starts = ends - padded
    ids = jnp.arange(N_EXPERTS, dtype=jnp.int32)
    pos = rank + jnp.sum(jnp.where(e[:, :, None] == ids, starts, 0), axis=-1)
    n_tiles = TOP_K * T // EXPERT_TILE + N_EXPERTS
    tile0 = jnp.arange(n_tiles, dtype=jnp.int32)
    valid = (tile0 * EXPERT_TILE < ends[-1]).astype(jnp.int32)
    tile = jnp.minimum(tile0, ends[-1] // EXPERT_TILE - 1)
    exp = jnp.minimum(jnp.sum(ends[None, :] <= (tile * EXPERT_TILE)[:, None], axis=1), N_EXPERTS - 1).astype(jnp.int32)
    j = jnp.arange(EXPERT_TILE, dtype=jnp.int32)[None, :]
    spare = n_tiles * EXPERT_TILE + j % SC_CHUNK
    pad_rows = jnp.where(j < (padded - counts)[:, None], (starts + counts)[:, None] + j, spare)
    return pos.T, pad_rows, (tile, exp, valid)


def _pick_tile(n, pref):
    t = pref
    while n % t:
        t //= 2
    return t


def kernel(x, positions, norm1_w, w_in, q_norm_w, k_norm_w, attn_sinks, hgrn_lower_bounds, hgrn_norm_w,
           w_branch_attn, w_branch_hgrn, w_out, norm2_w, w_router_group, b_router_group, w_router_expert,
           b_router_expert, w_gate_experts, w_up_experts, w_down_experts):
    B, S, D = x.shape
    T = B * S
    x2 = x.reshape(T, D)
    tm = _pick_tile(T, 512)

    inv_freq = ROPE_THETA ** (-jnp.arange(0, ROT_DIM, 2, dtype=F32) / ROT_DIM)
    ang = inv_freq[:, None] * positions.astype(F32).reshape(1, T)
    cs = jnp.concatenate(_split3(jnp.concatenate([jnp.cos(ang), jnp.sin(ang)], axis=0)), axis=0)

    w_in0 = w_in[0]
    heads = lambda off: [w_in0[:, off + h * HEAD_DIM:off + (h + 1) * HEAD_DIM]
                         for h in range(N_KV_HEADS) for _ in range(2)]
    w_in_a = jnp.concatenate([w_in0[:, :_OFF_K]] + heads(_OFF_K) + heads(_OFF_V) + [w_in0[:, _OFF_HQ:_OFF_Z]],
                             axis=1).astype(BF16)
    w_z = w_in0[:, _OFF_Z:].astype(BF16)

    q, k, v, hq, lf, hv, hg = _inproj(x2, norm1_w[0], w_in_a, cs, q_norm_w[0], k_norm_w[0],
                                      hgrn_lower_bounds.astype(F32), _pick_tile(T, 1024))
    attn = _attention(q, k, v, attn_sinks[0].astype(F32), B, S)
    hgrn = _hgrn(hq, lf, hv, hg, hgrn_norm_w[0], B, S, _pick_tile(S, 512))

    pad = ROUTER_ROWS - N_GROUPS - N_EXPERTS
    w_r = jnp.concatenate([w_router_group[0].T, w_router_expert[0].T, jnp.zeros((pad, D), F32)], axis=0)
    b_r =jnp.concatenate([b_router_group[0], b_router_expert[0], jnp.zeros((pad,), F32)]).astype(F32)
    b_r = jnp.broadcast_to(b_r[:, None], (ROUTER_ROWS, tm))

    x1, hn, route, counts = _merge(x2, attn, hgrn, norm1_w[0], w_z, w_branch_attn[0].astype(BF16),
                                   w_branch_hgrn[0].astype(BF16), w_out[0].astype(BF16), norm2_w[0],
                                   w_r.astype(BF16), b_r, tm)

    pos, pad_rows, items = _routing_tables(route, counts[N_GROUPS:N_GROUPS + N_EXPERTS, 0], T)
    tk = _pick_tile(T, 512)
    xs = _dispatch(pos, pad_rows, hn, items[0].shape[0] * EXPERT_TILE)
    y = _experts(items, xs, w_gate_experts[0], w_up_experts[0], w_down_experts[0])
    out = _combine(pos, x1, route, y, tk)
    return out.reshape(B, S, D)
```

```python
import functools

import numpy as np
import jax
import jax.numpy as jnp
from jax import lax
from jax.experimental import pallas as pl
from jax.experimental.pallas import tpu as pltpu
from jax.experimental.pallas import tpu_sc as plsc

F32 = jnp.float32
BF16 = jnp.bfloat16

D_MODEL = 1024
N_Q_HEADS = 8
N_KV_HEADS = 2
GROUP = N_Q_HEADS // N_KV_HEADS
HEAD_DIM = 64
ROT_DIM = HEAD_DIM // 4
ROT_HALF = ROT_DIM // 2
ROPE_THETA = 500000.0
WINDOW = 128
ATTN_WIDTH = N_Q_HEADS * HEAD_DIM
KV_WIDTH = N_KV_HEADS * HEAD_DIM

HGRN_HEADS = 4
HGRN_DIM = 128
HGRN_WIDTH = HGRN_HEADS * HGRN_DIM
CHUNK = 64

N_GROUPS = 4
EXPERTS_PER_GROUP = 8
N_EXPERTS = N_GROUPS * EXPERTS_PER_GROUP
TOP_K = 2
EXPERT_FF = 512
MOE_BLOCK = 128
EXPERT_TILE = 512
MOE_SPLITS = 2
NORM_EPS = 1e-6
ROUTER_ROWS = 40
ROUTE_ROWS = 8

LANES = 128
NEG_BIG = -1e30
LOG2_E = 1.4426950408889634

_OFF_Q, _OFF_K, _OFF_V = 0, ATTN_WIDTH, ATTN_WIDTH + KV_WIDTH
_OFF_HQ = ATTN_WIDTH + 2 * KV_WIDTH
_OFF_HF = _OFF_HQ + HGRN_WIDTH
_OFF_HI = _OFF_HF + HGRN_WIDTH
_OFF_HG = _OFF_HI + HGRN_WIDTH
_OFF_Z = _OFF_HG + HGRN_WIDTH
_A_Q, _A_K, _A_V = 0, ATTN_WIDTH, ATTN_WIDTH + 2 * KV_WIDTH
_A_HQ = ATTN_WIDTH + 4 * KV_WIDTH
_A_HF = _A_HQ + HGRN_WIDTH
_A_HI = _A_HF + HGRN_WIDTH
_A_HG = _A_HI + HGRN_WIDTH

VMEM_LIMIT = 56 * 1024 * 1024


def _split3(a):
    hi = a.astype(BF16)
    r1 = a - hi.astype(F32)
    mid = r1.astype(BF16)
    lo = (r1 - mid.astype(F32)).astype(BF16)
    return hi, mid, lo


def _dot(a, b):
    return jnp.dot(a, b, preferred_element_type=F32)


def _dot_nt(a, b):
    return lax.dot_general(a, b, (((1,), (1,)), ((), ())), preferred_element_type=F32)


def _dot_tn(a, b):
    return lax.dot_general(a, b, (((0,), (0,)), ((), ())), preferred_element_type=F32)


def _sigmoid(x):
    return 1.0 / (1.0 + jnp.exp(-x))


def _rms(x, w):
    ms = jnp.mean(x * x, axis=-1, keepdims=True)
    return x * lax.rsqrt(ms + NORM_EPS) * w


def _inproj_kernel(x_ref, n1w_ref, w_ref, cs_ref, rope_e_ref, rope_c0_ref, qw_ref, kw_ref,
                   mq_ref, mk_ref, lbp_ref,
                   q_out, k_out, v_out, hq_out, lf_out, hv_out, hg_out):
    xn = _rms(x_ref[...], n1w_ref[...]).astype(BF16)

    def proj(off, width):
        return _dot(xn, w_ref[:, off:off + width])

    tabs = _dot_tn(cs_ref[...], rope_e_ref[...])
    c_tab = tabs[:, 0:LANES] + rope_c0_ref[...]
    s1_tab = tabs[:, LANES:2 * LANES]
    s2_tab = tabs[:, 2 * LANES:3 * LANES]

    def norm_rope(t, mavg_ref, w_row, scale):
        ms = _dot((t * t).astype(BF16), mavg_ref[...])
        tn = t * lax.rsqrt(ms + NORM_EPS) * w_row
        if scale != 1.0:
            tn = tn * scale
        outs = []
        for j in range(t.shape[1] // LANES):
            c = tn[:, j * LANES:(j + 1) * LANES]
            outs.append(c * c_tab
                        + pltpu.roll(c, LANES - ROT_HALF, 1) * s1_tab
                        + pltpu.roll(c, ROT_HALF, 1) * s2_tab)
        return outs[0] if len(outs) == 1 else jnp.concatenate(outs, axis=1)

    q_out[...] = norm_rope(proj(_A_Q, ATTN_WIDTH), mq_ref, qw_ref[...], HEAD_DIM ** -0.5).astype(BF16)
    k_out[...] = norm_rope(proj(_A_K, 2 * KV_WIDTH), mk_ref, kw_ref[...], 1.0).astype(BF16)
    v_out[...] = proj(_A_V, 2 * KV_WIDTH).astype(BF16)

    hq = proj(_A_HQ, HGRN_WIDTH)
    hq_out[...] = (hq * _sigmoid(hq)).astype(BF16)
    h0 = lbp_ref[0:1, :]
    h1 = lbp_ref[1:2, :]
    hm = jnp.maximum(h0, h1)
    e0 = jnp.exp(h0 - hm)
    e1 = jnp.exp(h1 - hm)
    lb = e0 / (e0 + e1)
    fg = lb + (1.0 - lb) * _sigmoid(proj(_A_HF, HGRN_WIDTH))
    lf_out[...] = jnp.log(fg) * LOG2_E
    hv_out[...] = proj(_A_HI, HGRN_WIDTH).astype(BF16)
    hg = proj(_A_HG, HGRN_WIDTH)
    hg_out[...] = (hg * _sigmoid(hg)).astype(BF16)


def _rope_constants():
    e = np.zeros((2 * ROT_HALF, 3 * LANES), np.float32)
    c0 = np.zeros((1, LANES), np.float32)
    for lane in range(LANES):
        d = lane % HEAD_DIM
        if d < ROT_HALF:
            e[d, lane] = 1.0
            e[ROT_HALF + d, LANES + lane] = -1.0
        elif d < ROT_DIM:
            e[d - ROT_HALF, lane] = 1.0
            e[ROT_HALF + d - ROT_HALF, 2 * LANES + lane] = 1.0
        else:
            c0[0, lane] = 1.0
    return jnp.asarray(np.concatenate([e, e, e], axis=0), BF16), jnp.asarray(c0, F32)


def _head_mean_matrix(width):
    idx = np.arange(width) // HEAD_DIM
    m = (idx[:, None] == idx[None, :]).astype(np.float32) / HEAD_DIM
    return jnp.asarray(m, BF16)


def _inproj(x2, norm1_w, w_in_a, cs, q_norm_w, k_norm_w, lbp, tm):
    T = x2.shape[0]
    rope_e, rope_c0 = _rope_constants()
    qw = jnp.tile(q_norm_w.astype(F32), N_Q_HEADS)[None, :]
    kw = jnp.tile(k_norm_w.astype(F32), 2 * N_KV_HEADS)[None, :]
    mq = _head_mean_matrix(ATTN_WIDTH)
    mk = _head_mean_matrix(2 * KV_WIDTH)
    row = lambda w: pl.BlockSpec((tm, w), lambda i: (i, 0))
    full = lambda a: pl.BlockSpec(a.shape, lambda i: (0,) * a.ndim)
    ins = [x2, norm1_w[None, :], w_in_a, cs, rope_e, rope_c0, qw, kw, mq, mk, lbp]
    in_specs = ([row(D_MODEL), full(ins[1]), full(w_in_a), pl.BlockSpec((cs.shape[0], tm), lambda i: (0, i))]
                + [full(a) for a in ins[4:]])
    outs = [(ATTN_WIDTH, BF16), (2 * KV_WIDTH, BF16), (2 * KV_WIDTH, BF16), (HGRN_WIDTH, BF16),
            (HGRN_WIDTH, F32), (HGRN_WIDTH, BF16), (HGRN_WIDTH, BF16)]
    return pl.pallas_call(
        _inproj_kernel,
        out_shape=[jax.ShapeDtypeStruct((T, w), dt) for w, dt in outs],
        grid=(T // tm,),
        in_specs=in_specs,
        out_specs=[row(w) for w, _ in outs],
        compiler_params=pltpu.CompilerParams(dimension_semantics=("arbitrary",),
                                             vmem_limit_bytes=VMEM_LIMIT),
        name="inproj",
    )(*ins)


ATTN_QBLOCKS = 8


def _attn_kernel(sink_ref, q_ref, kc_ref, kp_ref, vc_ref, vp_ref, half_ref, o_ref):
    n_qblocks = q_ref.shape[0] // WINDOW
    has_prev = pl.program_id(1) > 0
    qi = lax.broadcasted_iota(jnp.int32, (WINDOW, 2 * WINDOW), 0)
    kj = lax.broadcasted_iota(jnp.int32, (WINDOW, 2 * WINDOW), 1)
    in_window = ((kj < WINDOW) & (kj > qi)) | ((kj >= WINDOW) & (kj - WINDOW <= qi))
    first_valid = in_window & ((kj >= WINDOW) | has_prev)
    left = lax.broadcasted_iota(jnp.int32, (WINDOW, LANES), 1) < HEAD_DIM
    half = (half_ref[0], half_ref[1])

    ks, rhs = [], []
    for h in range(N_KV_HEADS):
        cols = slice(h * LANES, (h + 1) * LANES)
        kall = jnp.concatenate([kp_ref[:, cols], kc_ref[:, cols]], axis=0)
        vall = jnp.concatenate([vp_ref[:, cols], vc_ref[:, cols]], axis=0)
        ks.append([kall * hm for hm in half])
        rhs.append([jnp.concatenate([vall * hm, hm], axis=1) for hm in half])

    units = [(j, h, pr, side) for j in range(n_qblocks) for h in range(N_KV_HEADS)
             for pr in range(GROUP // 2) for side in range(2)]
    scores = []
    for j, h, pr, side in units:
        pair = h * (GROUP // 2) + pr
        qp = q_ref[j * WINDOW:(j + 1) * WINDOW, pair * LANES:(pair + 1) * LANES]
        s = _dot_nt(qp, ks[h][side][j * WINDOW:(j + 2) * WINDOW])
        scores.append(jnp.where(first_valid if j == 0 else in_window, s, NEG_BIG))
    probs, sink_terms = [], []
    for (j, h, pr, side), s in zip(units, scores):
        sink = sink_ref[2 * (h * (GROUP // 2) + pr) + side]
        m = jnp.maximum(jnp.max(s, axis=-1, keepdims=True), sink)
        probs.append(jnp.exp(s - m).astype(BF16))
        sink_terms.append(jnp.exp(sink - m))
    acc = []
    for (j, h, pr, side), p in zip(units, probs):
        acc.append(_dot(p, rhs[h][side][j * WINDOW:(j + 2) * WINDOW]))
    for j in range(n_qblocks):
        outs = []
        for u in range(0, len(units), 2):
            if units[u][0] == j:
                both = acc[u] + acc[u + 1]
                den = both[:, LANES:2 * LANES] + jnp.where(left, sink_terms[u], sink_terms[u + 1])
                outs.append(both[:, 0:LANES] / den)
        o_ref[j * WINDOW:(j + 1) * WINDOW, :] = jnp.concatenate(outs, axis=1).astype(BF16)


def _attention(q, k, v, sinks, B, S):
    qblocks = _pick_tile(S // WINDOW, ATTN_QBLOCKS)
    rows = qblocks * WINDOW
    nb = S // rows
    cur = lambda b, n: (b * nb + n, 0)
    prev = lambda b, n: (jnp.maximum((b * nb + n) * qblocks - 1, 0), 0)
    lane_left = np.arange(LANES) < HEAD_DIM
    half = jnp.asarray(np.broadcast_to(np.stack([lane_left, ~lane_left])[:, None, :],
                                       (2, rows + WINDOW, LANES)), BF16)
    return pl.pallas_call(
        _attn_kernel,
        out_shape=jax.ShapeDtypeStruct((B * S, ATTN_WIDTH), BF16),
        grid=(B, nb),
        in_specs=[pl.BlockSpec(memory_space=pltpu.SMEM),
                  pl.BlockSpec((rows, ATTN_WIDTH), cur),
                  pl.BlockSpec((rows, 2 * KV_WIDTH), cur),
                  pl.BlockSpec((WINDOW, 2 * KV_WIDTH), prev),
                  pl.BlockSpec((rows, 2 * KV_WIDTH), cur),
                  pl.BlockSpec((WINDOW, 2 * KV_WIDTH), prev),
                  pl.BlockSpec(half.shape, lambda b, n: (0, 0, 0))],
        out_specs=pl.BlockSpec((rows, ATTN_WIDTH), cur),
        compiler_params=pltpu.CompilerParams(dimension_semantics=("arbitrary", "arbitrary"),
                                             vmem_limit_bytes=VMEM_LIMIT),
        name="swa_attention",
    )(sinks, q, k, k, v, v, half)


_LEVEL_HALVES = (1, 2, 4, 8, 16, 32)
HGRN_UNIT_GROUP = 32


def _hgrn_level_masks():
    t = np.arange(CHUNK)[:, None]
    s = np.arange(CHUNK)[None, :]
    masks = [((t // (2 * h)) == (s // (2 * h))) & ((t & h) != 0) & ((s & h) == 0) for h in _LEVEL_HALVES]
    return jnp.asarray(np.stack(masks), F32)


def _level_reference(b_ref, slot, half):
    if half >= 4:
        span = max(2 * half, 8)
        pieces = [jnp.broadcast_to(b_ref[slot, s + half - 1:s + half, :], (span, HGRN_DIM))
                  for s in range(0, CHUNK, span)]
    else:
        r8 = lax.broadcasted_iota(jnp.int32, (8, HGRN_DIM), 0)
        pieces = [jnp.where(r8 < 4,
                            jnp.broadcast_to(b_ref[slot, s + 1:s + 2, :], (8, HGRN_DIM)),
                            jnp.broadcast_to(b_ref[slot, s + 5:s + 6, :], (8, HGRN_DIM)))
                  for s in range(0, CHUNK, 8)]
    return pieces[0] if len(pieces) == 1 else jnp.concatenate(pieces, axis=0)


def _hgrn_kernel(hq_ref, lf_ref, hv_ref, hg_ref, nw_ref, tri_ref, lm_ref, o_ref, st_ref, b_ref):
    @pl.when(pl.program_id(1) == 0)
    def _():
        st_ref[...] = jnp.zeros_like(st_ref)

    tri2 = tri_ref[...]
    odd = (lax.broadcasted_iota(jnp.int32, (CHUNK, HGRN_DIM), 0) & 1) != 0
    masks = [lm_ref[li] != 0.0 for li in range(len(_LEVEL_HALVES))]
    units = [(c, h) for c in range(hq_ref.shape[0] // CHUNK) for h in range(HGRN_HEADS)]
    sl = lambda c, h: (slice(c * CHUNK, (c + 1) * CHUNK), slice(h * HGRN_DIM, (h + 1) * HGRN_DIM))

    bs = []
    for u, (c, h) in enumerate(units):
        lf2 = lf_ref[sl(c, h)]
        hi = lf2.astype(BF16)
        lo = (lf2 - hi.astype(F32)).astype(BF16)
        b = _dot(tri2, jnp.concatenate([hi, lo], axis=0))
        b_ref[u] = b
        bs.append(b)

    outs, qs, ks, fs = [], [], [], []
    for u, (c, h) in enumerate(units):
        b = bs[u]
        q = hq_ref[sl(c, h)].astype(F32)
        v_bf = hv_ref[sl(c, h)]
        f = jnp.exp2(lf_ref[sl(c, h)])
        k = 1.0 - f
        b_last = b[CHUNK - 1:CHUNK, :]
        st = st_ref[h]
        o = _dot_nt((q * jnp.exp2(b)).astype(BF16), st.astype(BF16))
        k_out = (k * jnp.exp2(b_last - b)).astype(BF16)
        st_ref[h] = st * jnp.exp2(b_last) + _dot_tn(v_bf, k_out)
        outs.append(o + jnp.sum(q * k, axis=-1, keepdims=True) * v_bf.astype(F32))
        qs.append(q)
        ks.append(k)
        fs.append(f)

    for g0 in range(0, len(units), HGRN_UNIT_GROUP):
        group = range(g0, min(g0 + HGRN_UNIT_GROUP, len(units)))
        accs = {u: jnp.zeros((CHUNK, CHUNK), F32) for u in group}
        for li, half in enumerate(_LEVEL_HALVES):
            for u in group:
                if half == 1:
                    e = jnp.where(odd, fs[u], 1.0)
                else:
                    e = jnp.exp2(-jnp.abs(bs[u] - _level_reference(b_ref, u, half)))
                accs[u] = jnp.where(masks[li], _dot_nt((qs[u] * e).astype(BF16), (ks[u] * e).astype(BF16)),
                                    accs[u])
        for u in group:
            c, h = units[u]
            o = outs[u] + _dot(accs[u].astype(BF16), hv_ref[sl(c, h)])
            y = _rms(o, nw_ref[...]) * hg_ref[sl(c, h)].astype(F32)
            o_ref[sl(c, h)] = y.astype(BF16)


def _hgrn(hq, lf, hv, hg, hgrn_norm_w, B, S, tb):
    nt = S // tb
    blk = pl.BlockSpec((tb, HGRN_WIDTH), lambda b, n: (b * nt + n, 0))
    tri = np.tril(np.ones((CHUNK, CHUNK), np.float32))
    tri = jnp.asarray(np.concatenate([tri, tri], axis=1), BF16)
    lm = _hgrn_level_masks()
    return pl.pallas_call(
        _hgrn_kernel,
        out_shape=jax.ShapeDtypeStruct((B * S, HGRN_WIDTH), BF16),
        grid=(B, nt),
        in_specs=[blk, blk, blk, blk,
                  pl.BlockSpec((1, HGRN_DIM), lambda b, n: (0, 0)),
                  pl.BlockSpec(tri.shape, lambda b, n: (0, 0)),
                  pl.BlockSpec(lm.shape, lambda b, n: (0, 0, 0))],
        out_specs=blk,
        scratch_shapes=[pltpu.VMEM((HGRN_HEADS, HGRN_DIM, HGRN_DIM), F32),
                        pltpu.VMEM((tb // CHUNK * HGRN_HEADS, CHUNK, HGRN_DIM), F32)],
        compiler_params=pltpu.CompilerParams(dimension_semantics=("arbitrary", "arbitrary"),
                                             vmem_limit_bytes=VMEM_LIMIT),
        name="hgrn2",
    )(hq, lf, hv, hg, hgrn_norm_w[None, :].astype(F32), tri, lm)


def _merge_kernel(x_ref, attn_ref, hgrn_ref, n1w_ref, wz_ref, wba_ref, wbh_ref, wout_ref, n2w_ref,
                  wr_ref, br_ref, utri_ref, x1_out, hn_out, route_out, count_out, run_ref):
    @pl.when(pl.program_id(0) == 0)
    def _():
        run_ref[...] = jnp.zeros_like(run_ref)

    x = x_ref[...]
    xn = _rms(x, n1w_ref[...]).astype(BF16)
    za = _sigmoid(_dot(xn, wz_ref[:, 0:D_MODEL]))
    zb = _sigmoid(_dot(xn, wz_ref[:, D_MODEL:2 * D_MODEL]))
    mixed = za * _dot(attn_ref[...], wba_ref[...]) + zb * _dot(hgrn_ref[...], wbh_ref[...])
    x1 = x + _dot(mixed.astype(BF16), wout_ref[...])
    x1_out[...] = x1
    hn = _rms(x1, n2w_ref[...])
    hn_out[...] = _pack_bf16_pairs(hn)

    logits = _dot_nt(wr_ref[...], hn.astype(BF16)) + br_ref[...]
    r = lax.broadcasted_iota(jnp.int32, logits.shape, 0).astype(F32)
    far = float(ROUTER_ROWS)
    cmax = lambda a: jnp.max(a, axis=0, keepdims=True)
    cmin = lambda a: jnp.min(a, axis=0, keepdims=True)
    csum = lambda a: jnp.sum(a, axis=0, keepdims=True)

    lg = jnp.where(r < N_GROUPS, logits, NEG_BIG)
    mg = cmax(lg)
    gsel = cmin(jnp.where(lg == mg, r, far))
    pgsel = 1.0 / csum(jnp.exp(lg - mg))

    lo = N_GROUPS + EXPERTS_PER_GROUP * gsel
    le = jnp.where((r >= lo) & (r < lo + EXPERTS_PER_GROUP), logits, NEG_BIG)
    m1 = cmax(le)
    i1 = cmin(jnp.where(le == m1, r, far))
    se = csum(jnp.exp(le - m1))
    le2 = jnp.where(r == i1, NEG_BIG, le)
    m2 = cmax(le2)
    i2 = cmin(jnp.where(le2 == m2, r, far))
    top0 = 1.0 / se
    top1 = jnp.exp(m2 - m1) / se
    tsum = top0 + top1
    w0 = pgsel * top0 / tsum
    w1 = pgsel * top1 / tsum

    sel1 = r == i1
    sel2 = r == i2
    onehot = jnp.where(sel1 | sel2, 1.0, 0.0)
    before = _dot(onehot.astype(BF16), utri_ref[...]) + run_ref[...]
    r0 = csum(jnp.where(sel1, before, 0.0))
    r1 = csum(jnp.where(sel2, before, 0.0))
    run_new = run_ref[...] + jnp.sum(onehot, axis=1, keepdims=True)
    run_ref[...] = run_new
    count_out[...] = run_new

    row8 = lax.broadcasted_iota(jnp.int32, route_out.shape, 0)
    vals = (i1 - N_GROUPS, i2 - N_GROUPS, w0, w1, r0, r1)
    route = jnp.zeros(route_out.shape, F32)
    for j, val in enumerate(vals):
        route = jnp.where(row8 == j, val, route)
    route_out[...] = route


def _merge(x2, attn, hgrn, norm1_w, w_z, w_ba, w_bh, w_out, norm2_w, w_r, br, tm):
    T = x2.shape[0]
    row = lambda w: pl.BlockSpec((tm, w), lambda i: (i, 0))
    full = lambda a: pl.BlockSpec(a.shape, lambda i: (0,) * a.ndim)
    utri = jnp.asarray(np.triu(np.ones((tm, tm), np.float32), 1), BF16)
    ins = [x2, attn, hgrn, norm1_w[None, :], w_z, w_ba, w_bh, w_out, norm2_w[None, :], w_r, br, utri]
    in_specs = [row(D_MODEL), row(ATTN_WIDTH), row(HGRN_WIDTH)] + [full(a) for a in ins[3:]]
    return pl.pallas_call(
        _merge_kernel,
        out_shape=[jax.ShapeDtypeStruct((T, D_MODEL), F32), jax.ShapeDtypeStruct((T, HALF), jnp.uint32),
                   jax.ShapeDtypeStruct((ROUTE_ROWS, T), F32),
                   jax.ShapeDtypeStruct((MOE_SPLITS * ROUTER_ROWS, tm), F32)],
        grid=(T // tm,),
        in_specs=in_specs,
        out_specs=[row(D_MODEL), row(HALF), pl.BlockSpec((ROUTE_ROWS, tm), lambda i: (0, i)),
                   pl.BlockSpec((ROUTER_ROWS, tm), lambda i: (i // (T // tm // MOE_SPLITS), 0))],
        scratch_shapes=[pltpu.VMEM((ROUTER_ROWS, tm), F32)],
        compiler_params=pltpu.CompilerParams(dimension_semantics=("arbitrary",),
                                             vmem_limit_bytes=VMEM_LIMIT),
        name="merge_router",
    )(*ins)


HALF = D_MODEL // 2


def _pack_bf16_pairs(x):
    bits = pltpu.bitcast(x.astype(BF16).astype(F32), jnp.uint32)
    return (bits[:, :HALF] >> 16) | bits[:, HALF:]


def _unpack_bf16_pairs(words):
    lo = pltpu.bitcast(words << 16, F32)
    hi = pltpu.bitcast(words & jnp.uint32(0xFFFF0000), F32)
    return lo, hi


SC_CORES = 2
SC_SUBCORES = 16
SC_CHUNK = 64


def _sc_mesh():
    return plsc.VectorSubcoreMesh(core_axis_name="c", subcore_axis_name="s",
                                  num_cores=SC_CORES, num_subcores=SC_SUBCORES)


def _sc_worker():
    return lax.axis_index("s") * SC_CORES + lax.axis_index("c")


def _dispatch(pos, pad_rows, hn, row0, n_rows):
    T, width = pos.shape[0], hn.shape[1]
    workers = SC_CORES * SC_SUBCORES
    per_worker = T // workers
    n_chunks = per_worker // SC_CHUNK
    assert per_worker * workers == T and n_chunks * SC_CHUNK == per_worker and n_chunks % 2 == 0
    idx = pos.T.reshape(TOP_K, workers, n_chunks, SC_CHUNK).transpose(1, 0, 2, 3)
    pad_chunks = pad_rows.size // (workers * SC_CHUNK)
    pad_idx = pad_rows.reshape(workers, pad_chunks, SC_CHUNK)
    zeros = jnp.zeros((SC_CHUNK, width), hn.dtype)

    @functools.partial(
        pl.kernel, mesh=_sc_mesh(), name="moe_dispatch",
        out_type=jax.ShapeDtypeStruct((n_rows + SC_CHUNK, width), hn.dtype),
        scratch_types=[pltpu.VMEM((TOP_K, n_chunks, SC_CHUNK), jnp.int32),
                       pltpu.VMEM((2, SC_CHUNK, width), hn.dtype),
                       pltpu.VMEM((pad_chunks, SC_CHUNK), jnp.int32),
                       pltpu.VMEM((SC_CHUNK, width), hn.dtype),
                       pltpu.SemaphoreType.DMA((2,)), pltpu.SemaphoreType.DMA((2,)), pltpu.SemaphoreType.DMA])
    def dispatch_kernel(rows_hbm, idx_hbm, pad_hbm, zeros_hbm, out_hbm,
                        idx_v, rows_v, pad_v, zeros_v, load_sem, scatter_sem, pad_sem):
        wid = _sc_worker()
        base = row0 + wid * per_worker
        pltpu.sync_copy(idx_hbm.at[wid], idx_v)
        pltpu.sync_copy(pad_hbm.at[wid], pad_v)
        pltpu.sync_copy(zeros_hbm, zeros_v)

        def zero_fill(j):
            return pltpu.make_async_copy(zeros_v, out_hbm.at[pad_v.at[j]], pad_sem)

        for j in range(pad_chunks):
            zero_fill(j).start()

        def load(c, b):
            off = pl.multiple_of(c * SC_CHUNK, SC_CHUNK)
            return pltpu.make_async_copy(rows_hbm.at[pl.ds(base + off, SC_CHUNK)], rows_v.at[b], load_sem.at[b])

        def scatter(c, b, k):
            return pltpu.make_async_copy(rows_v.at[b], out_hbm.at[idx_v.at[k, c]], scatter_sem.at[b])

        load(0, 0).start()

        @pl.loop(0, n_chunks, step=2)
        def _(c0):
            for b in range(2):
                c = c0 + b
                load(c, b).wait()
                for k in range(TOP_K):
                    scatter(c, b, k).start()

                @pl.when(c >= 1)
                def _():
                    for k in range(TOP_K):
                        scatter(c - 1, 1 - b, k).wait()

                @pl.when(c + 1 < n_chunks)
                def _():
                    load(c + 1, 1 - b).start()

        for k in range(TOP_K):
            scatter(n_chunks - 1, (n_chunks - 1) % 2, k).wait()
        for j in range(pad_chunks):
            zero_fill(j).wait()

    return dispatch_kernel(hn, idx, pad_idx, zeros)


def _expert_kernel(tile_ref, exp_ref, valid_ref, x_ref, wg_ref, wu_ref, wd_ref, y_ref, wg_s, wu_s, wd_s):
    w = pl.program_id(0)
    prev = jnp.maximum(w - 1, 0)

    @pl.when((w == 0) | (exp_ref[w] != exp_ref[prev]))
    def _():
        wg_s[...] = wg_ref[0].astype(BF16)
        wu_s[...] = wu_ref[0].astype(BF16)
        wd_s[...] = wd_ref[0].astype(BF16)

    def ffn(words):
        lo, hi = _unpack_bf16_pairs(words)
        lo = lo.astype(BF16)
        hi = hi.astype(BF16)
        gate = _dot(lo, wg_s[0:HALF, :]) + _dot(hi, wg_s[HALF:, :])
        up = _dot(lo, wu_s[0:HALF, :]) + _dot(hi, wu_s[HALF:, :])
        return _pack_bf16_pairs(_dot((gate * _sigmoid(gate) * up).astype(BF16), wd_s[...]))

    @pl.when(valid_ref[w] != 0)
    def _():
        y_ref[...] = ffn(x_ref[...])


def _experts(items, xs, w_gate, w_up, w_down):
    tile, exp, valid = items
    wspec = lambda shape: pl.BlockSpec((1,) + shape, lambda w, t, e, v: (e[w], 0, 0))
    xspec = pl.BlockSpec((EXPERT_TILE, HALF), lambda w, t, e, v: (t[w], 0))
    return pl.pallas_call(
        _expert_kernel,
        out_shape=jax.ShapeDtypeStruct((tile.shape[0] * EXPERT_TILE, HALF), jnp.uint32),
        grid_spec=pltpu.PrefetchScalarGridSpec(
            num_scalar_prefetch=3,
            grid=(tile.shape[0],),
            in_specs=[xspec, wspec((D_MODEL, EXPERT_FF)), wspec((D_MODEL, EXPERT_FF)),
                      wspec((EXPERT_FF, D_MODEL))],
            out_specs=xspec,
            scratch_shapes=[pltpu.VMEM((D_MODEL, EXPERT_FF), BF16),
                            pltpu.VMEM((D_MODEL, EXPERT_FF), BF16),
                            pltpu.VMEM((EXPERT_FF, D_MODEL), BF16)]),
        compiler_params=pltpu.CompilerParams(dimension_semantics=("arbitrary",),
                                             vmem_limit_bytes=VMEM_LIMIT),
        name="moe_experts",
    )(tile, exp, valid, xs, w_gate, w_up, w_down)


def _sc_gather_rows(table, idx):
    n = idx.shape[0]
    width = table.shape[1]
    per_worker = n // (SC_CORES * SC_SUBCORES)
    n_chunks = per_worker // SC_CHUNK
    assert per_worker * SC_CORES * SC_SUBCORES == n and n_chunks * SC_CHUNK == per_worker and n_chunks % 2 == 0

    @functools.partial(
        pl.kernel, mesh=_sc_mesh(), name="moe_row_gather",
        out_type=jax.ShapeDtypeStruct((n, width), table.dtype),
        scratch_types=[pltpu.VMEM((per_worker,), jnp.int32),
                       pltpu.VMEM((2, SC_CHUNK, width), table.dtype),
                       pltpu.SemaphoreType.DMA((2,))])
    def gather_kernel(table_hbm, idx_hbm, out_hbm, idx_v, rows_v, sem):
        base = _sc_worker() * per_worker
        pltpu.sync_copy(idx_hbm.at[pl.ds(base, per_worker)], idx_v)

        def gather(c, b):
            off = pl.multiple_of(c * SC_CHUNK, SC_CHUNK)
            return pltpu.make_async_copy(table_hbm.at[idx_v.at[pl.ds(off, SC_CHUNK)]], rows_v.at[b], sem.at[b])

        gather(0, 0).start()

        @pl.loop(0, n_chunks, step=2)
        def _(c0):
            for b in range(2):
                c = c0 + b

                @pl.when(c + 1 < n_chunks)
                def _():
                    gather(c + 1, 1 - b).start()

                gather(c, b).wait()
                off = pl.multiple_of(c * SC_CHUNK, SC_CHUNK)
                pltpu.sync_copy(rows_v.at[b], out_hbm.at[pl.ds(base + off, SC_CHUNK)])

    return gather_kernel(table, idx)


def _combine_kernel(x1_ref, route_ref, y0_ref, y1_ref, o_ref):
    gates = route_ref[...].T
    w0 = gates[:, 2:3]
    w1 = gates[:, 3:4]
    lo0, hi0 = _unpack_bf16_pairs(y0_ref[...])
    lo1, hi1 = _unpack_bf16_pairs(y1_ref[...])
    o_ref[:, 0:HALF] = x1_ref[:, 0:HALF] + w0 * lo0 + w1 * lo1
    o_ref[:, HALF:] = x1_ref[:, HALF:] + w0 * hi0 + w1 * hi1


def _combine(pos, x1, route, y, row0, tk, out_so_far):
    T = x1.shape[0]
    nt = pos.shape[0] // tk
    t0 = row0 // tk
    ysel = _sc_gather_rows(y, pos.T.reshape(-1))
    in_specs = [pl.BlockSpec((tk, D_MODEL), lambda i: (i + t0, 0)),
                pl.BlockSpec((ROUTE_ROWS, tk), lambda i: (0, i + t0)),
                pl.BlockSpec((tk, HALF), lambda i: (i, 0)),
                pl.BlockSpec((tk, HALF), lambda i: (i + nt, 0))]
    args = [x1, route, ysel, ysel]
    aliases = {}
    kern = _combine_kernel
    if out_so_far is not None:
        in_specs.append(pl.BlockSpec(memory_space=pl.ANY))
        args.append(out_so_far)
        aliases = {len(args) - 1: 0}
        kern = lambda x1_ref, route_ref, y0_ref, y1_ref, prev_ref, o_ref: _combine_kernel(
            x1_ref, route_ref, y0_ref, y1_ref, o_ref)
    return pl.pallas_call(
        kern,
        out_shape=jax.ShapeDtypeStruct((T, D_MODEL), F32),
        grid=(nt,),
        in_specs=in_specs,
        out_specs=pl.BlockSpec((tk, D_MODEL), lambda i: (i + t0, 0)),
        input_output_aliases=aliases,
        compiler_params=pltpu.CompilerParams(dimension_semantics=("arbitrary",),
                                             vmem_limit_bytes=VMEM_LIMIT),
        name="moe_combine",
    )(*args)


def _routing_tables(route, counts_before, counts_after, T):
    e = route[0:TOP_K].astype(jnp.int32)
    rank = route[4:4 + TOP_K].astype(jnp.int32)
    before = counts_before.astype(jnp.int32)
    counts = counts_after.astype(jnp.int32) - before
    padded = (counts + EXPERT_TILE - 1) // EXPERT_TILE * EXPERT_TILE
    ends = jnp.cumsum(padded)
    starts = ends - padded
    ids = jnp.arange(N_EXPERTS, dtype=jnp.int32)
    pos = rank + jnp.sum(jnp.where(e[:, :, None] == ids, starts - before, 0), axis=-1)
    n_tiles = TOP_K * T // EXPERT_TILE + N_EXPERTS
    tile0 = jnp.arange(n_tiles, dtype=jnp.int32)
    valid = (tile0 * EXPERT_TILE < ends[-1]).astype(jnp.int32)
    tile = jnp.minimum(tile0, ends[-1] // EXPERT_TILE - 1)
    exp = jnp.minimum(jnp.sum(ends[None, :] <= (tile * EXPERT_TILE)[:, None], axis=1), N_EXPERTS - 1).astype(jnp.int32)
    j = jnp.arange(EXPERT_TILE, dtype=jnp.int32)[None, :]
    spare = n_tiles * EXPERT_TILE + j % SC_CHUNK
    pad_rows = jnp.where(j < (padded - counts)[:, None], (starts + counts)[:, None] + j, spare)
    return pos.T, pad_rows, (tile, exp, valid)


def _pick_tile(n, pref):
    t = pref
    while n % t:
        t //= 2
    return t


def kernel(x, positions, norm1_w, w_in, q_norm_w, k_norm_w, attn_sinks, hgrn_lower_bounds, hgrn_norm_w,
           w_branch_attn, w_branch_hgrn, w_out, norm2_w, w_router_group, b_router_group, w_router_expert,
           b_router_expert, w_gate_experts, w_up_experts, w_down_experts):
    B, S, D = x.shape
    T = B * S
    x2 = x.reshape(T, D)
    tm = _pick_tile(T, 512)

    inv_freq = ROPE_THETA ** (-jnp.arange(0, ROT_DIM, 2, dtype=F32) / ROT_DIM)
    ang = inv_freq[:, None] * positions.astype(F32).reshape(1, T)
    cs = jnp.concatenate(_split3(jnp.concatenate([jnp.cos(ang), jnp.sin(ang)], axis=0)), axis=0)

    w_in0 = w_in[0]
    heads = lambda off: [w_in0[:, off + h * HEAD_DIM:off + (h + 1) * HEAD_DIM]
                         for h in range(N_KV_HEADS) for _ in range(2)]
    w_in_a = jnp.concatenate([w_in0[:, :_OFF_K]] + heads(_OFF_K) + heads(_OFF_V) + [w_in0[:, _OFF_HQ:_OFF_Z]],
                             axis=1).astype(BF16)
    w_z = w_in0[:, _OFF_Z:].astype(BF16)

    q, k, v, hq, lf, hv, hg = _inproj(x2, norm1_w[0], w_in_a, cs, q_norm_w[0], k_norm_w[0],
                                      hgrn_lower_bounds.astype(F32), _pick_tile(T, 1024))
    attn = _attention(q, k, v, attn_sinks[0].astype(F32), B, S)
    hgrn = _hgrn(hq, lf, hv, hg, hgrn_norm_w[0], B, S, _pick_tile(S, 512))

    pad = ROUTER_ROWS - N_GROUPS - N_EXPERTS
    w_r = jnp.concatenate([w_router_group[0].T, w_router_expert[0].T, jnp.zeros((pad, D), F32)], axis=0)
    b_r =jnp.concatenate([b_router_group[0], b_router_expert[0], jnp.zeros((pad,), F32)]).astype(F32)
    b_r = jnp.broadcast_to(b_r[:, None], (ROUTER_ROWS, tm))

    x1, hn, route, counts = _merge(x2, attn, hgrn, norm1_w[0], w_z, w_branch_attn[0].astype(BF16),
                                   w_branch_hgrn[0].astype(BF16), w_out[0].astype(BF16), norm2_w[0],
                                   w_r.astype(BF16), b_r, tm)

    t_split = T // MOE_SPLITS
    tk = _pick_tile(t_split, 512)
    counts = counts.reshape(MOE_SPLITS, ROUTER_ROWS, tm)[:, N_GROUPS:N_GROUPS + N_EXPERTS, 0]
    counts = jnp.concatenate([jnp.zeros((1, N_EXPERTS), F32), counts], axis=0)
    out = None
    for s in range(MOE_SPLITS):
        row0 = s * t_split
        pos, pad_rows, items = _routing_tables(route[:, row0:row0 + t_split], counts[s], counts[s + 1], t_split)
        xs = _dispatch(pos, pad_rows, hn, row0, items[0].shape[0] * EXPERT_TILE)
        y = _experts(items, xs, w_gate_experts[0], w_up_experts[0], w_down_experts[0])
        out = _combine(pos, x1, route, y, row0, tk, out)
    return out.reshape(B, S, D)
```

```python
import functools

import numpy as np
import jax
import jax.numpy as jnp
from jax import lax
from jax.experimental import pallas as pl
from jax.experimental.pallas import tpu as pltpu
from jax.experimental.pallas import tpu_sc as plsc

F32 = jnp.float32
BF16 = jnp.bfloat16

D_MODEL = 1024
N_Q_HEADS = 8
N_KV_HEADS = 2
GROUP = N_Q_HEADS // N_KV_HEADS
HEAD_DIM = 64
ROT_DIM = HEAD_DIM // 4
ROT_HALF = ROT_DIM // 2
ROPE_THETA = 500000.0
WINDOW = 128
ATTN_WIDTH = N_Q_HEADS * HEAD_DIM
KV_WIDTH = N_KV_HEADS * HEAD_DIM

HGRN_HEADS = 4
HGRN_DIM = 128
HGRN_WIDTH = HGRN_HEADS * HGRN_DIM
CHUNK = 64

N_GROUPS = 4
EXPERTS_PER_GROUP = 8
N_EXPERTS = N_GROUPS * EXPERTS_PER_GROUP
TOP_K = 2
EXPERT_FF = 512
MOE_BLOCK = 128
EXPERT_TILE = 512
NORM_EPS = 1e-6
ROUTER_ROWS = 40
ROUTE_ROWS = 8

LANES = 128
NEG_BIG = -1e30
LOG2_E = 1.4426950408889634

_OFF_Q, _OFF_K, _OFF_V = 0, ATTN_WIDTH, ATTN_WIDTH + KV_WIDTH
_OFF_HQ = ATTN_WIDTH + 2 * KV_WIDTH
_OFF_HF = _OFF_HQ + HGRN_WIDTH
_OFF_HI = _OFF_HF + HGRN_WIDTH
_OFF_HG = _OFF_HI + HGRN_WIDTH
_OFF_Z = _OFF_HG + HGRN_WIDTH
_A_Q, _A_K, _A_V = 0, ATTN_WIDTH, ATTN_WIDTH + 2 * KV_WIDTH
_A_HQ = ATTN_WIDTH + 4 * KV_WIDTH
_A_HF = _A_HQ + HGRN_WIDTH
_A_HI = _A_HF + HGRN_WIDTH
_A_HG = _A_HI + HGRN_WIDTH

VMEM_LIMIT = 56 * 1024 * 1024


def _split3(a):
    hi = a.astype(BF16)
    r1 = a - hi.astype(F32)
    mid = r1.astype(BF16)
    lo = (r1 - mid.astype(F32)).astype(BF16)
    return hi, mid, lo


def _dot(a, b):
    return jnp.dot(a, b, preferred_element_type=F32)


def _dot_nt(a, b):
    return lax.dot_general(a, b, (((1,), (1,)), ((), ())), preferred_element_type=F32)


def _dot_tn(a, b):
    return lax.dot_general(a, b, (((0,), (0,)), ((), ())), preferred_element_type=F32)


def _sigmoid(x):
    return 1.0 / (1.0 + jnp.exp(-x))


def _rms(x, w):
    ms = jnp.mean(x * x, axis=-1, keepdims=True)
    return x * lax.rsqrt(ms + NORM_EPS) * w


def _inproj_kernel(x_ref, n1w_ref, w_ref, cs_ref, rope_e_ref, rope_c0_ref, qw_ref, kw_ref,
                   mq_ref, mk_ref, lbp_ref,
                   q_out, k_out, v_out, hq_out, lf_out, hv_out, hg_out):
    xn = _rms(x_ref[...], n1w_ref[...]).astype(BF16)

    def proj(off, width):
        return _dot(xn, w_ref[:, off:off + width])

    tabs = _dot_tn(cs_ref[...], rope_e_ref[...])
    c_tab = tabs[:, 0:LANES] + rope_c0_ref[...]
    s1_tab = tabs[:, LANES:2 * LANES]
    s2_tab = tabs[:, 2 * LANES:3 * LANES]

    def norm_rope(t, mavg_ref, w_row, scale):
        ms = _dot((t * t).astype(BF16), mavg_ref[...])
        tn = t * lax.rsqrt(ms + NORM_EPS) * w_row
        if scale != 1.0:
            tn = tn * scale
        outs = []
        for j in range(t.shape[1] // LANES):
            c = tn[:, j * LANES:(j + 1) * LANES]
            outs.append(c * c_tab
                        + pltpu.roll(c, LANES - ROT_HALF, 1) * s1_tab
                        + pltpu.roll(c, ROT_HALF, 1) * s2_tab)
        return outs[0] if len(outs) == 1 else jnp.concatenate(outs, axis=1)

    q_out[...] = norm_rope(proj(_A_Q, ATTN_WIDTH), mq_ref, qw_ref[...], HEAD_DIM ** -0.5).astype(BF16)
    k_out[...] = norm_rope(proj(_A_K, 2 * KV_WIDTH), mk_ref, kw_ref[...], 1.0).astype(BF16)
    v_out[...] = proj(_A_V, 2 * KV_WIDTH).astype(BF16)

    hq = proj(_A_HQ, HGRN_WIDTH)
    hq_out[...] = (hq * _sigmoid(hq)).astype(BF16)
    h0 = lbp_ref[0:1, :]
    h1 = lbp_ref[1:2, :]
    hm = jnp.maximum(h0, h1)
    e0 = jnp.exp(h0 - hm)
    e1 = jnp.exp(h1 - hm)
    lb = e0 / (e0 + e1)
    fg = lb + (1.0 - lb) * _sigmoid(proj(_A_HF, HGRN_WIDTH))
    lf_out[...] = jnp.log(fg) * LOG2_E
    hv_out[...] = proj(_A_HI, HGRN_WIDTH).astype(BF16)
    hg = proj(_A_HG, HGRN_WIDTH)
    hg_out[...] = (hg * _sigmoid(hg)).astype(BF16)


def _rope_constants():
    e = np.zeros((2 * ROT_HALF, 3 * LANES), np.float32)
    c0 = np.zeros((1, LANES), np.float32)
    for lane in range(LANES):
        d = lane % HEAD_DIM
        if d < ROT_HALF:
            e[d, lane] = 1.0
            e[ROT_HALF + d, LANES + lane] = -1.0
        elif d < ROT_DIM:
            e[d - ROT_HALF, lane] = 1.0
            e[ROT_HALF + d - ROT_HALF, 2 * LANES + lane] = 1.0
        else:
            c0[0, lane] = 1.0
    return jnp.asarray(np.concatenate([e, e, e], axis=0), BF16), jnp.asarray(c0, F32)


def _head_mean_matrix(width):
    idx = np.arange(width) // HEAD_DIM
    m = (idx[:, None] == idx[None, :]).astype(np.float32) / HEAD_DIM
    return jnp.asarray(m, BF16)


def _inproj(x2, norm1_w, w_in_a, cs, q_norm_w, k_norm_w, lbp, tm):
    T = x2.shape[0]
    rope_e, rope_c0 = _rope_constants()
    qw = jnp.tile(q_norm_w.astype(F32), N_Q_HEADS)[None, :]
    kw = jnp.tile(k_norm_w.astype(F32), 2 * N_KV_HEADS)[None, :]
    mq = _head_mean_matrix(ATTN_WIDTH)
    mk = _head_mean_matrix(2 * KV_WIDTH)
    row = lambda w: pl.BlockSpec((tm, w), lambda i: (i, 0))
    full = lambda a: pl.BlockSpec(a.shape, lambda i: (0,) * a.ndim)
    ins = [x2, norm1_w[None, :], w_in_a, cs, rope_e, rope_c0, qw, kw, mq, mk, lbp]
    in_specs = ([row(D_MODEL), full(ins[1]), full(w_in_a), pl.BlockSpec((cs.shape[0], tm), lambda i: (0, i))]
                + [full(a) for a in ins[4:]])
    outs = [(ATTN_WIDTH, BF16), (2 * KV_WIDTH, BF16), (2 * KV_WIDTH, BF16), (HGRN_WIDTH, BF16),
            (HGRN_WIDTH, F32), (HGRN_WIDTH, BF16), (HGRN_WIDTH, BF16)]
    return pl.pallas_call(
        _inproj_kernel,
        out_shape=[jax.ShapeDtypeStruct((T, w), dt) for w, dt in outs],
        grid=(T // tm,),
        in_specs=in_specs,
        out_specs=[row(w) for w, _ in outs],
        compiler_params=pltpu.CompilerParams(dimension_semantics=("arbitrary",),
                                             vmem_limit_bytes=VMEM_LIMIT),
        name="inproj",
    )(*ins)


ATTN_QBLOCKS = 8


def _attn_kernel(sink_ref, q_ref, kc_ref, kp_ref, vc_ref, vp_ref, half_ref, o_ref):
    n_qblocks = q_ref.shape[0] // WINDOW
    has_prev = pl.program_id(1) > 0
    qi = lax.broadcasted_iota(jnp.int32, (WINDOW, 2 * WINDOW), 0)
    kj = lax.broadcasted_iota(jnp.int32, (WINDOW, 2 * WINDOW), 1)
    in_window = ((kj < WINDOW) & (kj > qi)) | ((kj >= WINDOW) & (kj - WINDOW <= qi))
    first_valid = in_window & ((kj >= WINDOW) | has_prev)
    left = lax.broadcasted_iota(jnp.int32, (WINDOW, LANES), 1) < HEAD_DIM
    half = (half_ref[0], half_ref[1])

    ks, rhs = [], []
    for h in range(N_KV_HEADS):
        cols = slice(h * LANES, (h + 1) * LANES)
        kall = jnp.concatenate([kp_ref[:, cols], kc_ref[:, cols]], axis=0)
        vall = jnp.concatenate([vp_ref[:, cols], vc_ref[:, cols]], axis=0)
        ks.append([kall * hm for hm in half])
        rhs.append([jnp.concatenate([vall * hm, hm], axis=1) for hm in half])

    units = [(j, h, pr, side) for j in range(n_qblocks) for h in range(N_KV_HEADS)
             for pr in range(GROUP // 2) for side in range(2)]
    scores = []
    for j, h, pr, side in units:
        pair = h * (GROUP // 2) + pr
        qp = q_ref[j * WINDOW:(j + 1) * WINDOW, pair * LANES:(pair + 1) * LANES]
        s = _dot_nt(qp, ks[h][side][j * WINDOW:(j + 2) * WINDOW])
        scores.append(jnp.where(first_valid if j == 0 else in_window, s, NEG_BIG))
    probs, sink_terms = [], []
    for (j, h, pr, side), s in zip(units, scores):
        sink = sink_ref[2 * (h * (GROUP // 2) + pr) + side]
        m = jnp.maximum(jnp.max(s, axis=-1, keepdims=True), sink)
        probs.append(jnp.exp(s - m).astype(BF16))
        sink_terms.append(jnp.exp(sink - m))
    acc = []
    for (j, h, pr, side), p in zip(units, probs):
        acc.append(_dot(p, rhs[h][side][j * WINDOW:(j + 2) * WINDOW]))
    for j in range(n_qblocks):
        outs = []
        for u in range(0, len(units), 2):
            if units[u][0] == j:
                both = acc[u] + acc[u + 1]
                den = both[:, LANES:2 * LANES] + jnp.where(left, sink_terms[u], sink_terms[u + 1])
                outs.append(both[:, 0:LANES] / den)
        o_ref[j * WINDOW:(j + 1) * WINDOW, :] = jnp.concatenate(outs, axis=1).astype(BF16)


def _attention(q, k, v, sinks, B, S):
    qblocks = _pick_tile(S // WINDOW, ATTN_QBLOCKS)
    rows = qblocks * WINDOW
    nb = S // rows
    cur = lambda b, n: (b * nb + n, 0)
    prev = lambda b, n: (jnp.maximum((b * nb + n) * qblocks - 1, 0), 0)
    lane_left = np.arange(LANES) < HEAD_DIM
    half = jnp.asarray(np.broadcast_to(np.stack([lane_left, ~lane_left])[:, None, :],
                                       (2, rows + WINDOW, LANES)), BF16)
    return pl.pallas_call(
        _attn_kernel,
        out_shape=jax.ShapeDtypeStruct((B * S, ATTN_WIDTH), BF16),
        grid=(B, nb),
        in_specs=[pl.BlockSpec(memory_space=pltpu.SMEM),
                  pl.BlockSpec((rows, ATTN_WIDTH), cur),
                  pl.BlockSpec((rows, 2 * KV_WIDTH), cur),
                  pl.BlockSpec((WINDOW, 2 * KV_WIDTH), prev),
                  pl.BlockSpec((rows, 2 * KV_WIDTH), cur),
                  pl.BlockSpec((WINDOW, 2 * KV_WIDTH), prev),
                  pl.BlockSpec(half.shape, lambda b, n: (0, 0, 0))],
        out_specs=pl.BlockSpec((rows, ATTN_WIDTH), cur),
        compiler_params=pltpu.CompilerParams(dimension_semantics=("arbitrary", "arbitrary"),
                                             vmem_limit_bytes=VMEM_LIMIT),
        name="swa_attention",
    )(sinks, q, k, k, v, v, half)


_LEVEL_HALVES = (1, 2, 4, 8, 16, 32)
HGRN_UNIT_GROUP = 32


def _hgrn_level_masks():
    t = np.arange(CHUNK)[:, None]
    s = np.arange(CHUNK)[None, :]
    masks = [((t // (2 * h)) == (s // (2 * h))) & ((t & h) != 0) & ((s & h) == 0) for h in _LEVEL_HALVES]
    return jnp.asarray(np.stack(masks), F32)


def _level_reference(b_ref, slot, half):
    if half >= 4:
        span = max(2 * half, 8)
        pieces = [jnp.broadcast_to(b_ref[slot, s + half - 1:s + half, :], (span, HGRN_DIM))
                  for s in range(0, CHUNK, span)]
    else:
        r8 = lax.broadcasted_iota(jnp.int32, (8, HGRN_DIM), 0)
        pieces = [jnp.where(r8 < 4,
                            jnp.broadcast_to(b_ref[slot, s + 1:s + 2, :], (8, HGRN_DIM)),
                            jnp.broadcast_to(b_ref[slot, s + 5:s + 6, :], (8, HGRN_DIM)))
                  for s in range(0, CHUNK, 8)]
    return pieces[0] if len(pieces) == 1 else jnp.concatenate(pieces, axis=0)


def _hgrn_kernel(hq_ref, lf_ref, hv_ref, hg_ref, nw_ref, tri_ref, lm_ref, o_ref, st_ref, b_ref):
    @pl.when(pl.program_id(1) == 0)
    def _():
        st_ref[...] = jnp.zeros_like(st_ref)

    tri2 = tri_ref[...]
    odd = (lax.broadcasted_iota(jnp.int32, (CHUNK, HGRN_DIM), 0) & 1) != 0
    masks = [lm_ref[li] != 0.0 for li in range(len(_LEVEL_HALVES))]
    units = [(c, h) for c in range(hq_ref.shape[0] // CHUNK) for h in range(HGRN_HEADS)]
    sl = lambda c, h: (slice(c * CHUNK, (c + 1) * CHUNK), slice(h * HGRN_DIM, (h + 1) * HGRN_DIM))

    bs = []
    for u, (c, h) in enumerate(units):
        lf2 = lf_ref[sl(c, h)]
        hi = lf2.astype(BF16)
        lo = (lf2 - hi.astype(F32)).astype(BF16)
        b = _dot(tri2, jnp.concatenate([hi, lo], axis=0))
        b_ref[u] = b
        bs.append(b)

    outs, qs, ks, fs = [], [], [], []
    for u, (c, h) in enumerate(units):
        b = bs[u]
        q = hq_ref[sl(c, h)].astype(F32)
        v_bf = hv_ref[sl(c, h)]
        f = jnp.exp2(lf_ref[sl(c, h)])
        k = 1.0 - f
        b_last = b[CHUNK - 1:CHUNK, :]
        st = st_ref[h]
        o = _dot_nt((q * jnp.exp2(b)).astype(BF16), st.astype(BF16))
        k_out = (k * jnp.exp2(b_last - b)).astype(BF16)
        st_ref[h] = st * jnp.exp2(b_last) + _dot_tn(v_bf, k_out)
        outs.append(o + jnp.sum(q * k, axis=-1, keepdims=True) * v_bf.astype(F32))
        qs.append(q)
        ks.append(k)
        fs.append(f)

    for g0 in range(0, len(units), HGRN_UNIT_GROUP):
        group = range(g0, min(g0 + HGRN_UNIT_GROUP, len(units)))
        accs = {u: jnp.zeros((CHUNK, CHUNK), F32) for u in group}
        for li, half in enumerate(_LEVEL_HALVES):
            for u in group:
                if half == 1:
                    e = jnp.where(odd, fs[u], 1.0)
                else:
                    e = jnp.exp2(-jnp.abs(bs[u] - _level_reference(b_ref, u, half)))
                accs[u] = jnp.where(masks[li], _dot_nt((qs[u] * e).astype(BF16), (ks[u] * e).astype(BF16)),
                                    accs[u])
        for u in group:
            c, h = units[u]
            o = outs[u] + _dot(accs[u].astype(BF16), hv_ref[sl(c, h)])
            y = _rms(o, nw_ref[...]) * hg_ref[sl(c, h)].astype(F32)
            o_ref[sl(c, h)] = y.astype(BF16)


def _hgrn(hq, lf, hv, hg, hgrn_norm_w, B, S, tb):
    nt = S // tb
    blk = pl.BlockSpec((tb, HGRN_WIDTH), lambda b, n: (b * nt + n, 0))
    tri = np.tril(np.ones((CHUNK, CHUNK), np.float32))
    tri = jnp.asarray(np.concatenate([tri, tri], axis=1), BF16)
    lm = _hgrn_level_masks()
    return pl.pallas_call(
        _hgrn_kernel,
        out_shape=jax.ShapeDtypeStruct((B * S, HGRN_WIDTH), BF16),
        grid=(B, nt),
        in_specs=[blk, blk, blk, blk,
                  pl.BlockSpec((1, HGRN_DIM), lambda b, n: (0, 0)),
                  pl.BlockSpec(tri.shape, lambda b, n: (0, 0)),
                  pl.BlockSpec(lm.shape, lambda b, n: (0, 0, 0))],
        out_specs=blk,
        scratch_shapes=[pltpu.VMEM((HGRN_HEADS, HGRN_DIM, HGRN_DIM), F32),
                        pltpu.VMEM((tb // CHUNK * HGRN_HEADS, CHUNK, HGRN_DIM), F32)],
        compiler_params=pltpu.CompilerParams(dimension_semantics=("arbitrary", "arbitrary"),
                                             vmem_limit_bytes=VMEM_LIMIT),
        name="hgrn2",
    )(hq, lf, hv, hg, hgrn_norm_w[None, :].astype(F32), tri, lm)


def _merge_kernel(x_ref, attn_ref, hgrn_ref, n1w_ref, wz_ref, wba_ref, wbh_ref, wout_ref, n2w_ref,
                  wr_ref, br_ref, utri_ref, x1_out, hn_out, route_out, count_out, run_ref):
    @pl.when(pl.program_id(0) == 0)
    def _():
        run_ref[...] = jnp.zeros_like(run_ref)

    x = x_ref[...]
    xn = _rms(x, n1w_ref[...]).astype(BF16)
    za = _sigmoid(_dot(xn, wz_ref[:, 0:D_MODEL]))
    zb = _sigmoid(_dot(xn, wz_ref[:, D_MODEL:2 * D_MODEL]))
    mixed = za * _dot(attn_ref[...], wba_ref[...]) + zb * _dot(hgrn_ref[...], wbh_ref[...])
    x1 = x + _dot(mixed.astype(BF16), wout_ref[...])
    x1_out[...] = x1
    hn = _rms(x1, n2w_ref[...])
    hn_out[...] = _pack_bf16_pairs(hn)

    logits = _dot_nt(wr_ref[...], hn.astype(BF16)) + br_ref[...]
    r = lax.broadcasted_iota(jnp.int32, logits.shape, 0).astype(F32)
    far = float(ROUTER_ROWS)
    cmax = lambda a: jnp.max(a, axis=0, keepdims=True)
    cmin = lambda a: jnp.min(a, axis=0, keepdims=True)
    csum = lambda a: jnp.sum(a, axis=0, keepdims=True)

    lg = jnp.where(r < N_GROUPS, logits, NEG_BIG)
    mg = cmax(lg)
    gsel = cmin(jnp.where(lg == mg, r, far))
    pgsel = 1.0 / csum(jnp.exp(lg - mg))

    lo = N_GROUPS + EXPERTS_PER_GROUP * gsel
    le = jnp.where((r >= lo) & (r < lo + EXPERTS_PER_GROUP), logits, NEG_BIG)
    m1 = cmax(le)
    i1 = cmin(jnp.where(le == m1, r, far))
    se = csum(jnp.exp(le - m1))
    le2 = jnp.where(r == i1, NEG_BIG, le)
    m2 = cmax(le2)
    i2 = cmin(jnp.where(le2 == m2, r, far))
    top0 = 1.0 / se
    top1 = jnp.exp(m2 - m1) / se
    tsum = top0 + top1
    w0 = pgsel * top0 / tsum
    w1 = pgsel * top1 / tsum

    sel1 = r == i1
    sel2 = r == i2
    onehot = jnp.where(sel1 | sel2, 1.0, 0.0)
    before = _dot(onehot.astype(BF16), utri_ref[...]) + run_ref[...]
    r0 = csum(jnp.where(sel1, before, 0.0))
    r1 = csum(jnp.where(sel2, before, 0.0))
    run_new = run_ref[...] + jnp.sum(onehot, axis=1, keepdims=True)
    run_ref[...] = run_new
    count_out[...] = run_new

    row8 = lax.broadcasted_iota(jnp.int32, route_out.shape, 0)
    vals = (i1 - N_GROUPS, i2 - N_GROUPS, w0, w1, r0, r1)
    route = jnp.zeros(route_out.shape, F32)
    for j, val in enumerate(vals):
        route = jnp.where(row8 == j, val, route)
    route_out[...] = route


def _merge(x2, attn, hgrn, norm1_w, w_z, w_ba, w_bh, w_out, norm2_w, w_r, br, tm):
    T = x2.shape[0]
    row = lambda w: pl.BlockSpec((tm, w), lambda i: (i, 0))
    full = lambda a: pl.BlockSpec(a.shape, lambda i: (0,) * a.ndim)
    utri = jnp.asarray(np.triu(np.ones((tm, tm), np.float32), 1), BF16)
    ins = [x2, attn, hgrn, norm1_w[None, :], w_z, w_ba, w_bh, w_out, norm2_w[None, :], w_r, br, utri]
    in_specs = [row(D_MODEL), row(ATTN_WIDTH), row(HGRN_WIDTH)] + [full(a) for a in ins[3:]]
    return pl.pallas_call(
        _merge_kernel,
        out_shape=[jax.ShapeDtypeStruct((T, D_MODEL), F32), jax.ShapeDtypeStruct((T, HALF), jnp.uint32),
                   jax.ShapeDtypeStruct((ROUTE_ROWS, T), F32), jax.ShapeDtypeStruct((ROUTER_ROWS, tm), F32)],
        grid=(T // tm,),
        in_specs=in_specs,
        out_specs=[row(D_MODEL), row(HALF), pl.BlockSpec((ROUTE_ROWS, tm), lambda i: (0, i)),
                   pl.BlockSpec((ROUTER_ROWS, tm), lambda i: (0, 0))],
        scratch_shapes=[pltpu.VMEM((ROUTER_ROWS, tm), F32)],
        compiler_params=pltpu.CompilerParams(dimension_semantics=("arbitrary",),
                                             vmem_limit_bytes=VMEM_LIMIT),
        name="merge_router",
    )(*ins)


HALF = D_MODEL // 2


def _pack_bf16_pairs(x):
    bits = pltpu.bitcast(x.astype(BF16).astype(F32), jnp.uint32)
    return (bits[:, :HALF] >> 16) | bits[:, HALF:]


def _unpack_bf16_pairs(words):
    lo = pltpu.bitcast(words << 16, F32)
    hi = pltpu.bitcast(words & jnp.uint32(0xFFFF0000), F32)
    return lo, hi


SC_CORES = 2
SC_SUBCORES = 16
SC_CHUNK = 64


def _sc_mesh():
    return plsc.VectorSubcoreMesh(core_axis_name="c", subcore_axis_name="s",
                                  num_cores=SC_CORES, num_subcores=SC_SUBCORES)


def _sc_worker():
    return lax.axis_index("s") * SC_CORES + lax.axis_index("c")


def _dispatch(pos, pad_rows, hn, n_rows):
    T, width = hn.shape
    workers = SC_CORES * SC_SUBCORES
    per_worker = T // workers
    n_chunks = per_worker // SC_CHUNK
    assert per_worker * workers == T and n_chunks * SC_CHUNK == per_worker and n_chunks % 2 == 0
    idx = pos.T.reshape(TOP_K, workers, n_chunks, SC_CHUNK).transpose(1, 0, 2, 3)
    pad_chunks = pad_rows.size // (workers * SC_CHUNK)
    pad_idx = pad_rows.reshape(workers, pad_chunks, SC_CHUNK)
    zeros = jnp.zeros((SC_CHUNK, width), hn.dtype)

    @functools.partial(
        pl.kernel, mesh=_sc_mesh(), name="moe_dispatch",
        out_type=jax.ShapeDtypeStruct((n_rows + SC_CHUNK, width), hn.dtype),
        scratch_types=[pltpu.VMEM((TOP_K, n_chunks, SC_CHUNK), jnp.int32),
                       pltpu.VMEM((2, SC_CHUNK, width), hn.dtype),
                       pltpu.VMEM((pad_chunks, SC_CHUNK), jnp.int32),
                       pltpu.VMEM((SC_CHUNK, width), hn.dtype),
                       pltpu.SemaphoreType.DMA((2,)), pltpu.SemaphoreType.DMA((2,)), pltpu.SemaphoreType.DMA])
    def dispatch_kernel(rows_hbm, idx_hbm, pad_hbm, zeros_hbm, out_hbm,
                        idx_v, rows_v, pad_v, zeros_v, load_sem, scatter_sem, pad_sem):
        wid = _sc_worker()
        base = wid * per_worker
        pltpu.sync_copy(idx_hbm.at[wid], idx_v)
        pltpu.sync_copy(pad_hbm.at[wid], pad_v)
        pltpu.sync_copy(zeros_hbm, zeros_v)

        def zero_fill(j):
            return pltpu.make_async_copy(zeros_v, out_hbm.at[pad_v.at[j]], pad_sem)

        for j in range(pad_chunks):
            zero_fill(j).start()

        def load(c, b):
            off = pl.multiple_of(c * SC_CHUNK, SC_CHUNK)
            return pltpu.make_async_copy(rows_hbm.at[pl.ds(base + off, SC_CHUNK)], rows_v.at[b], load_sem.at[b])

        def scatter(c, b, k):
            return pltpu.make_async_copy(rows_v.at[b], out_hbm.at[idx_v.at[k, c]], scatter_sem.at[b])

        load(0, 0).start()

        @pl.loop(0, n_chunks, step=2)
        def _(c0):
            for b in range(2):
                c = c0 + b
                load(c, b).wait()
                for k in range(TOP_K):
                    scatter(c, b, k).start()

                @pl.when(c >= 1)
                def _():
                    for k in range(TOP_K):
                        scatter(c - 1, 1 - b, k).wait()

                @pl.when(c + 1 < n_chunks)
                def _():
                    load(c + 1, 1 - b).start()

        for k in range(TOP_K):
            scatter(n_chunks - 1, (n_chunks - 1) % 2, k).wait()
        for j in range(pad_chunks):
            zero_fill(j).wait()

    return dispatch_kernel(hn, idx, pad_idx, zeros)


def _expert_kernel(tile_ref, exp_ref, valid_ref, x_ref, wg_ref, wu_ref, wd_ref, y_ref, wg_s, wu_s, wd_s):
    w = pl.program_id(0)
    prev = jnp.maximum(w - 1, 0)

    @pl.when((w == 0) | (exp_ref[w] != exp_ref[prev]))
    def _():
        wg_s[...] = wg_ref[0].astype(BF16)
        wu_s[...] = wu_ref[0].astype(BF16)
        wd_s[...] = wd_ref[0].astype(BF16)

    def ffn(words):
        lo, hi = _unpack_bf16_pairs(words)
        lo = lo.astype(BF16)
        hi = hi.astype(BF16)
        gate = _dot(lo, wg_s[0:HALF, :]) + _dot(hi, wg_s[HALF:, :])
        up = _dot(lo, wu_s[0:HALF, :]) + _dot(hi, wu_s[HALF:, :])
        return _pack_bf16_pairs(_dot((gate * _sigmoid(gate) * up).astype(BF16), wd_s[...]))

    @pl.when(valid_ref[w] != 0)
    def _():
        y_ref[...] = ffn(x_ref[...])


def _experts(items, xs, w_gate, w_up, w_down):
    tile, exp, valid = items
    wspec = lambda shape: pl.BlockSpec((1,) + shape, lambda w, t, e, v: (e[w], 0, 0))
    xspec = pl.BlockSpec((EXPERT_TILE, HALF), lambda w, t, e, v: (t[w], 0))
    return pl.pallas_call(
        _expert_kernel,
        out_shape=jax.ShapeDtypeStruct((tile.shape[0] * EXPERT_TILE, HALF), jnp.uint32),
        grid_spec=pltpu.PrefetchScalarGridSpec(
            num_scalar_prefetch=3,
            grid=(tile.shape[0],),
            in_specs=[xspec, wspec((D_MODEL, EXPERT_FF)), wspec((D_MODEL, EXPERT_FF)),
                      wspec((EXPERT_FF, D_MODEL))],
            out_specs=xspec,
            scratch_shapes=[pltpu.VMEM((D_MODEL, EXPERT_FF), BF16),
                            pltpu.VMEM((D_MODEL, EXPERT_FF), BF16),
                            pltpu.VMEM((EXPERT_FF, D_MODEL), BF16)]),
        compiler_params=pltpu.CompilerParams(dimension_semantics=("arbitrary",),
                                             vmem_limit_bytes=VMEM_LIMIT),
        name="moe_experts",
    )(tile, exp, valid, xs, w_gate, w_up, w_down)


def _sc_gather_rows(table, idx):
    n = idx.shape[0]
    width = table.shape[1]
    per_worker = n // (SC_CORES * SC_SUBCORES)
    n_chunks = per_worker // SC_CHUNK
    assert per_worker * SC_CORES * SC_SUBCORES == n and n_chunks * SC_CHUNK == per_worker and n_chunks % 2 == 0

    @functools.partial(
        pl.kernel, mesh=_sc_mesh(), name="moe_row_gather",
        out_type=jax.ShapeDtypeStruct((n, width), table.dtype),
        scratch_types=[pltpu.VMEM((per_worker,), jnp.int32),
                       pltpu.VMEM((2, SC_CHUNK, width), table.dtype),
                       pltpu.SemaphoreType.DMA((2,))])
    def gather_kernel(table_hbm, idx_hbm, out_hbm, idx_v, rows_v, sem):
        base = _sc_worker() * per_worker
        pltpu.sync_copy(idx_hbm.at[pl.ds(base, per_worker)], idx_v)

        def gather(c, b):
            off = pl.multiple_of(c * SC_CHUNK, SC_CHUNK)
            return pltpu.make_async_copy(table_hbm.at[idx_v.at[pl.ds(off, SC_CHUNK)]], rows_v.at[b], sem.at[b])

        gather(0, 0).start()

        @pl.loop(0, n_chunks, step=2)
        def _(c0):
            for b in range(2):
                c = c0 + b

                @pl.when(c + 1 < n_chunks)
                def _():
                    gather(c + 1, 1 - b).start()

                gather(c, b).wait()
                off = pl.multiple_of(c * SC_CHUNK, SC_CHUNK)
                pltpu.sync_copy(rows_v.at[b], out_hbm.at[pl.ds(base + off, SC_CHUNK)])

    return gather_kernel(table, idx)


def _combine_kernel(x1_ref, route_ref, y0_ref, y1_ref, o_ref):
    gates = route_ref[...].T
    w0 = gates[:, 2:3]
    w1 = gates[:, 3:4]
    lo0, hi0 = _unpack_bf16_pairs(y0_ref[...])
    lo1, hi1 = _unpack_bf16_pairs(y1_ref[...])
    o_ref[:, 0:HALF] = x1_ref[:, 0:HALF] + w0 * lo0 + w1 * lo1
    o_ref[:, HALF:] = x1_ref[:, HALF:] + w0 * hi0 + w1 * hi1


def _combine(pos, x1, route, y, tk):
    T = x1.shape[0]
    nt = T // tk
    ysel = _sc_gather_rows(y, pos.T.reshape(-1))
    return pl.pallas_call(
        _combine_kernel,
        out_shape=jax.ShapeDtypeStruct((T, D_MODEL), F32),
        grid=(nt,),
        in_specs=[pl.BlockSpec((tk, D_MODEL), lambda i: (i, 0)),
                  pl.BlockSpec((ROUTE_ROWS, tk), lambda i: (0, i)),
                  pl.BlockSpec((tk, HALF), lambda i: (i, 0)),
                  pl.BlockSpec((tk, HALF), lambda i: (i + nt, 0))],
        out_specs=pl.BlockSpec((tk, D_MODEL), lambda i: (i, 0)),
        compiler_params=pltpu.CompilerParams(dimension_semantics=("arbitrary",),
                                             vmem_limit_bytes=VMEM_LIMIT),
        name="moe_combine",
    )(x1, route, ysel, ysel)


def _routing_tables(route, counts, T):
    e = route[0:TOP_K].astype(jnp.int32)
    rank = route[4:4 + TOP_K].astype(jnp.int32)
    counts = counts.astype(jnp.int32)
    padded = (counts + EXPERT_TILE - 1) // EXPERT_TILE * EXPERT_TILE
    ends = jnp.cumsum(padded)
    starts = ends - padded
    ids = jnp.arange(N_EXPERTS, dtype=jnp.int32)
    pos = rank + jnp.sum(jnp.where(e[:, :, None] == ids, starts, 0), axis=-1)
    n_tiles = TOP_K * T // EXPERT_TILE + N_EXPERTS
    tile0 = jnp.arange(n_tiles, dtype=jnp.int32)
    valid = (tile0 * EXPERT_TILE < ends[-1]).astype(jnp.int32)
    tile = jnp.minimum(tile0, ends[-1] // EXPERT_TILE - 1)
    exp = jnp.minimum(jnp.sum(ends[None, :] <= (tile * EXPERT_TILE)[:, None], axis=1), N_EXPERTS - 1).astype(jnp.int32)
    j = jnp.arange(EXPERT_TILE, dtype=jnp.int32)[None, :]
    n_pad = (padded - counts)[:, None]
    spare = n_tiles * EXPERT_TILE + j % SC_CHUNK
    pad_rows = jnp.where(n_pad > 0, (starts + counts)[:, None] + j % jnp.maximum(n_pad, 1), spare)
    return pos.T, pad_rows, (tile, exp, valid)


def _pick_tile(n, pref):
    t = pref
    while n % t:
        t //= 2
    return t


def kernel(x, positions, norm1_w, w_in, q_norm_w, k_norm_w, attn_sinks, hgrn_lower_bounds, hgrn_norm_w,
           w_branch_attn, w_branch_hgrn, w_out, norm2_w, w_router_group, b_router_group, w_router_expert,
           b_router_expert, w_gate_experts, w_up_experts, w_down_experts):
    B, S, D = x.shape
    T = B * S
    x2 = x.reshape(T, D)
    tm = _pick_tile(T, 512)

    inv_freq = ROPE_THETA ** (-jnp.arange(0, ROT_DIM, 2, dtype=F32) / ROT_DIM)
    ang = inv_freq[:, None] * positions.astype(F32).reshape(1, T)
    cs = jnp.concatenate(_split3(jnp.concatenate([jnp.cos(ang), jnp.sin(ang)], axis=0)), axis=0)

    w_in0 = w_in[0]
    heads = lambda off: [w_in0[:, off + h * HEAD_DIM:off + (h + 1) * HEAD_DIM]
                         for h in range(N_KV_HEADS) for _ in range(2)]
    w_in_a = jnp.concatenate([w_in0[:, :_OFF_K]] + heads(_OFF_K) + heads(_OFF_V) + [w_in0[:, _OFF_HQ:_OFF_Z]],
                             axis=1).astype(BF16)
    w_z = w_in0[:, _OFF_Z:].astype(BF16)

    q, k, v, hq, lf, hv, hg = _inproj(x2, norm1_w[0], w_in_a, cs, q_norm_w[0], k_norm_w[0],
                                      hgrn_lower_bounds.astype(F32), _pick_tile(T, 1024))
    attn = _attention(q, k, v, attn_sinks[0].astype(F32), B, S)
    hgrn = _hgrn(hq, lf, hv, hg, hgrn_norm_w[0], B, S, _pick_tile(S, 1024))

    pad = ROUTER_ROWS - N_GROUPS - N_EXPERTS
    w_r = jnp.concatenate([w_router_group[0].T, w_router_expert[0].T, jnp.zeros((pad, D), F32)], axis=0)
    b_r =jnp.concatenate([b_router_group[0], b_router_expert[0], jnp.zeros((pad,), F32)]).astype(F32)
    b_r = jnp.broadcast_to(b_r[:, None], (ROUTER_ROWS, tm))

    x1, hn, route, counts = _merge(x2, attn, hgrn, norm1_w[0], w_z, w_branch_attn[0].astype(BF16),
                                   w_branch_hgrn[0].astype(BF16), w_out[0].astype(BF16), norm2_w[0],
                                   w_r.astype(BF16), b_r, tm)

    pos, pad_rows, items = _routing_tables(route, counts[N_GROUPS:N_GROUPS + N_EXPERTS, 0], T)
    tk = _pick_tile(T, 512)
    xs = _dispatch(pos, pad_rows, hn, items[0].shape[0] * EXPERT_TILE)
    y = _experts(items, xs, w_gate_experts[0], w_up_experts[0], w_down_experts[0])
    out = _combine(pos, x1, route, y, tk)
    return out.reshape(B, S, D)
```

```python
import functools

import numpy as np
import jax
import jax.numpy as jnp
from jax import lax
from jax.experimental import pallas as pl
from jax.experimental.pallas import tpu as pltpu
from jax.experimental.pallas import tpu_sc as plsc

F32 = jnp.float32
BF16 = jnp.bfloat16

D_MODEL = 1024
N_Q_HEADS = 8
N_KV_HEADS = 2
GROUP = N_Q_HEADS // N_KV_HEADS
HEAD_DIM = 64
ROT_DIM = HEAD_DIM // 4
ROT_HALF = ROT_DIM // 2
ROPE_THETA = 500000.0
WINDOW = 128
ATTN_WIDTH = N_Q_HEADS * HEAD_DIM
KV_WIDTH = N_KV_HEADS * HEAD_DIM

HGRN_HEADS = 4
HGRN_DIM = 128
HGRN_WIDTH = HGRN_HEADS * HGRN_DIM
CHUNK = 64

N_GROUPS = 4
EXPERTS_PER_GROUP = 8
N_EXPERTS = N_GROUPS * EXPERTS_PER_GROUP
TOP_K = 2
EXPERT_FF = 512
EXPERT_TILE = 512
NORM_EPS = 1e-6
ROUTER_ROWS = 40
ROUTE_ROWS = 8

LANES = 128
NEG_BIG = -1e30
LOG2_E = 1.4426950408889634

_OFF_Q, _OFF_K, _OFF_V = 0, ATTN_WIDTH, ATTN_WIDTH + KV_WIDTH
_OFF_HQ = ATTN_WIDTH + 2 * KV_WIDTH
_OFF_HF = _OFF_HQ + HGRN_WIDTH
_OFF_HI = _OFF_HF + HGRN_WIDTH
_OFF_HG = _OFF_HI + HGRN_WIDTH
_OFF_Z = _OFF_HG + HGRN_WIDTH
_A_Q, _A_K, _A_V = 0, ATTN_WIDTH, ATTN_WIDTH + 2 * KV_WIDTH
_A_HQ = ATTN_WIDTH + 4 * KV_WIDTH
_A_HF = _A_HQ + HGRN_WIDTH
_A_HI = _A_HF + HGRN_WIDTH
_A_HG = _A_HI + HGRN_WIDTH

VMEM_LIMIT = 56 * 1024 * 1024


def _split3(a):
    hi = a.astype(BF16)
    r1 = a - hi.astype(F32)
    mid = r1.astype(BF16)
    lo = (r1 - mid.astype(F32)).astype(BF16)
    return hi, mid, lo


def _dot(a, b):
    return jnp.dot(a, b, preferred_element_type=F32)


def _dot_nt(a, b):
    return lax.dot_general(a, b, (((1,), (1,)), ((), ())), preferred_element_type=F32)


def _dot_tn(a, b):
    return lax.dot_general(a, b, (((0,), (0,)), ((), ())), preferred_element_type=F32)


def _sigmoid(x):
    return 1.0 / (1.0 + jnp.exp(-x))


def _rms(x, w):
    ms = jnp.mean(x * x, axis=-1, keepdims=True)
    return x * lax.rsqrt(ms + NORM_EPS) * w


def _inproj_kernel(x_ref, n1w_ref, w_ref, cs_ref, rope_e_ref, rope_c0_ref, qw_ref, kw_ref,
                   mq_ref, mk_ref, lbp_ref,
                   q_out, k_out, v_out, hq_out, lf_out, hv_out, hg_out):
    xn = _rms(x_ref[...], n1w_ref[...]).astype(BF16)

    def proj(off, width):
        return _dot(xn, w_ref[:, off:off + width])

    tabs = _dot_tn(cs_ref[...], rope_e_ref[...])
    c_tab = tabs[:, 0:LANES] + rope_c0_ref[...]
    s1_tab = tabs[:, LANES:2 * LANES]
    s2_tab = tabs[:, 2 * LANES:3 * LANES]

    def norm_rope(t, mavg_ref, w_row, scale):
        ms = _dot((t * t).astype(BF16), mavg_ref[...])
        tn = t * lax.rsqrt(ms + NORM_EPS) * w_row
        if scale != 1.0:
            tn = tn * scale
        outs = []
        for j in range(t.shape[1] // LANES):
            c = tn[:, j * LANES:(j + 1) * LANES]
            outs.append(c * c_tab
                        + pltpu.roll(c, LANES - ROT_HALF, 1) * s1_tab
                        + pltpu.roll(c, ROT_HALF, 1) * s2_tab)
        return outs[0] if len(outs) == 1 else jnp.concatenate(outs, axis=1)

    q_out[...] = norm_rope(proj(_A_Q, ATTN_WIDTH), mq_ref, qw_ref[...], HEAD_DIM ** -0.5).astype(BF16)
    k_out[...] = norm_rope(proj(_A_K, 2 * KV_WIDTH), mk_ref, kw_ref[...], 1.0).astype(BF16)
    v_out[...] = proj(_A_V, 2 * KV_WIDTH).astype(BF16)

    hq = proj(_A_HQ, HGRN_WIDTH)
    hq_out[...] = (hq * _sigmoid(hq)).astype(BF16)
    h0 = lbp_ref[0:1, :]
    h1 = lbp_ref[1:2, :]
    hm = jnp.maximum(h0, h1)
    e0 = jnp.exp(h0 - hm)
    e1 = jnp.exp(h1 - hm)
    lb = e0 / (e0 + e1)
    fg = lb + (1.0 - lb) * _sigmoid(proj(_A_HF, HGRN_WIDTH))
    lf_out[...] = jnp.log(fg) * LOG2_E
    hv_out[...] = proj(_A_HI, HGRN_WIDTH).astype(BF16)
    hg = proj(_A_HG, HGRN_WIDTH)
    hg_out[...] = (hg * _sigmoid(hg)).astype(BF16)


def _rope_constants():
    e = np.zeros((2 * ROT_HALF, 3 * LANES), np.float32)
    c0 = np.zeros((1, LANES), np.float32)
    for lane in range(LANES):
        d = lane % HEAD_DIM
        if d < ROT_HALF:
            e[d, lane] = 1.0
            e[ROT_HALF + d, LANES + lane] = -1.0
        elif d < ROT_DIM:
            e[d - ROT_HALF, lane] = 1.0
            e[ROT_HALF + d - ROT_HALF, 2 * LANES + lane] = 1.0
        else:
            c0[0, lane] = 1.0
    return jnp.asarray(np.concatenate([e, e, e], axis=0), BF16), jnp.asarray(c0, F32)


def _head_mean_matrix(width):
    idx = np.arange(width) // HEAD_DIM
    m = (idx[:, None] == idx[None, :]).astype(np.float32) / HEAD_DIM
    return jnp.asarray(m, BF16)


def _inproj(x2, norm1_w, w_in_a, cs, q_norm_w, k_norm_w, lbp, tm):
    T = x2.shape[0]
    rope_e, rope_c0 = _rope_constants()
    qw = jnp.tile(q_norm_w.astype(F32), N_Q_HEADS)[None, :]
    kw = jnp.tile(k_norm_w.astype(F32), 2 * N_KV_HEADS)[None, :]
    mq = _head_mean_matrix(ATTN_WIDTH)
    mk = _head_mean_matrix(2 * KV_WIDTH)
    row = lambda w: pl.BlockSpec((tm, w), lambda i: (i, 0))
    full = lambda a: pl.BlockSpec(a.shape, lambda i: (0,) * a.ndim)
    ins = [x2, norm1_w[None, :], w_in_a, cs, rope_e, rope_c0, qw, kw, mq, mk, lbp]
    in_specs = ([row(D_MODEL), full(ins[1]), full(w_in_a), pl.BlockSpec((cs.shape[0], tm), lambda i: (0, i))]
                + [full(a) for a in ins[4:]])
    outs = [(ATTN_WIDTH, BF16), (2 * KV_WIDTH, BF16), (2 * KV_WIDTH, BF16), (HGRN_WIDTH, BF16),
            (HGRN_WIDTH, F32), (HGRN_WIDTH, BF16), (HGRN_WIDTH, BF16)]
    return pl.pallas_call(
        _inproj_kernel,
        out_shape=[jax.ShapeDtypeStruct((T, w), dt) for w, dt in outs],
        grid=(T // tm,),
        in_specs=in_specs,
        out_specs=[row(w) for w, _ in outs],
        compiler_params=pltpu.CompilerParams(dimension_semantics=("arbitrary",),
                                             vmem_limit_bytes=VMEM_LIMIT),
        name="inproj",
    )(*ins)


ATTN_QBLOCKS = 8


def _attn_kernel(sink_ref, q_ref, kc_ref, kp_ref, vc_ref, vp_ref, half_ref, o_ref):
    n_qblocks = q_ref.shape[0] // WINDOW
    has_prev = pl.program_id(1) > 0
    qi = lax.broadcasted_iota(jnp.int32, (WINDOW, 2 * WINDOW), 0)
    kj = lax.broadcasted_iota(jnp.int32, (WINDOW, 2 * WINDOW), 1)
    in_window = ((kj < WINDOW) & (kj > qi)) | ((kj >= WINDOW) & (kj - WINDOW <= qi))
    first_valid = in_window & ((kj >= WINDOW) | has_prev)
    left = lax.broadcasted_iota(jnp.int32, (WINDOW, LANES), 1) < HEAD_DIM
    half = (half_ref[0], half_ref[1])

    ks, rhs = [], []
    for h in range(N_KV_HEADS):
        cols = slice(h * LANES, (h + 1) * LANES)
        kall = jnp.concatenate([kp_ref[:, cols], kc_ref[:, cols]], axis=0)
        vall = jnp.concatenate([vp_ref[:, cols], vc_ref[:, cols]], axis=0)
        ks.append([kall * hm for hm in half])
        rhs.append([jnp.concatenate([vall * hm, hm], axis=1) for hm in half])

    units = [(j, h, pr, side) for j in range(n_qblocks) for h in range(N_KV_HEADS)
             for pr in range(GROUP // 2) for side in range(2)]
    scores = []
    for j, h, pr, side in units:
        pair = h * (GROUP // 2) + pr
        qp = q_ref[j * WINDOW:(j + 1) * WINDOW, pair * LANES:(pair + 1) * LANES]
        s = _dot_nt(qp, ks[h][side][j * WINDOW:(j + 2) * WINDOW])
        scores.append(jnp.where(first_valid if j == 0 else in_window, s, NEG_BIG))
    probs, sink_terms = [], []
    for (j, h, pr, side), s in zip(units, scores):
        sink = sink_ref[2 * (h * (GROUP // 2) + pr) + side]
        m = jnp.maximum(jnp.max(s, axis=-1, keepdims=True), sink)
        probs.append(jnp.exp(s - m).astype(BF16))
        sink_terms.append(jnp.exp(sink - m))
    acc = []
    for (j, h, pr, side), p in zip(units, probs):
        acc.append(_dot(p, rhs[h][side][j * WINDOW:(j + 2) * WINDOW]))
    for j in range(n_qblocks):
        outs = []
        for u in range(0, len(units), 2):
            if units[u][0] == j:
                both = acc[u] + acc[u + 1]
                den = both[:, LANES:2 * LANES] + jnp.where(left, sink_terms[u], sink_terms[u + 1])
                outs.append(both[:, 0:LANES] / den)
        o_ref[j * WINDOW:(j + 1) * WINDOW, :] = jnp.concatenate(outs, axis=1).astype(BF16)


def _attention(q, k, v, sinks, B, S):
    qblocks = _pick_tile(S // WINDOW, ATTN_QBLOCKS)
    rows = qblocks * WINDOW
    nb = S // rows
    cur = lambda b, n: (b * nb + n, 0)
    prev = lambda b, n: (jnp.maximum((b * nb + n) * qblocks - 1, 0), 0)
    lane_left = np.arange(LANES) < HEAD_DIM
    half = jnp.asarray(np.broadcast_to(np.stack([lane_left, ~lane_left])[:, None, :],
                                       (2, rows + WINDOW, LANES)), BF16)
    return pl.pallas_call(
        _attn_kernel,
        out_shape=jax.ShapeDtypeStruct((B * S, ATTN_WIDTH), BF16),
        grid=(B, nb),
        in_specs=[pl.BlockSpec(memory_space=pltpu.SMEM),
                  pl.BlockSpec((rows, ATTN_WIDTH), cur),
                  pl.BlockSpec((rows, 2 * KV_WIDTH), cur),
                  pl.BlockSpec((WINDOW, 2 * KV_WIDTH), prev),
                  pl.BlockSpec((rows, 2 * KV_WIDTH), cur),
                  pl.BlockSpec((WINDOW, 2 * KV_WIDTH), prev),
                  pl.BlockSpec(half.shape, lambda b, n: (0, 0, 0))],
        out_specs=pl.BlockSpec((rows, ATTN_WIDTH), cur),
        compiler_params=pltpu.CompilerParams(dimension_semantics=("arbitrary", "arbitrary"),
                                             vmem_limit_bytes=VMEM_LIMIT),
        name="swa_attention",
    )(sinks, q, k, k, v, v, half)


_LEVEL_HALVES = (1, 2, 4, 8, 16, 32)
HGRN_UNIT_GROUP = 32


def _hgrn_level_masks():
    t = np.arange(CHUNK)[:, None]
    s = np.arange(CHUNK)[None, :]
    masks = [((t // (2 * h)) == (s // (2 * h))) & ((t & h) != 0) & ((s & h) == 0) for h in _LEVEL_HALVES]
    return jnp.asarray(np.stack(masks), F32)


def _level_reference(b_ref, slot, half):
    if half >= 4:
        span = max(2 * half, 8)
        pieces = [jnp.broadcast_to(b_ref[slot, s + half - 1:s + half, :], (span, HGRN_DIM))
                  for s in range(0, CHUNK, span)]
    else:
        r8 = lax.broadcasted_iota(jnp.int32, (8, HGRN_DIM), 0)
        pieces = [jnp.where(r8 < 4,
                            jnp.broadcast_to(b_ref[slot, s + 1:s + 2, :], (8, HGRN_DIM)),
                            jnp.broadcast_to(b_ref[slot, s + 5:s + 6, :], (8, HGRN_DIM)))
                  for s in range(0, CHUNK, 8)]
    return pieces[0] if len(pieces) == 1 else jnp.concatenate(pieces, axis=0)


def _hgrn_kernel(hq_ref, lf_ref, hv_ref, hg_ref, nw_ref, tri_ref, lm_ref, o_ref, st_ref, b_ref):
    @pl.when(pl.program_id(1) == 0)
    def _():
        st_ref[...] = jnp.zeros_like(st_ref)

    tri2 = tri_ref[...]
    odd = (lax.broadcasted_iota(jnp.int32, (CHUNK, HGRN_DIM), 0) & 1) != 0
    masks = [lm_ref[li] != 0.0 for li in range(len(_LEVEL_HALVES))]
    units = [(c, h) for c in range(hq_ref.shape[0] // CHUNK) for h in range(HGRN_HEADS)]
    sl = lambda c, h: (slice(c * CHUNK, (c + 1) * CHUNK), slice(h * HGRN_DIM, (h + 1) * HGRN_DIM))

    bs = []
    for u, (c, h) in enumerate(units):
        lf2 = lf_ref[sl(c, h)]
        hi = lf2.astype(BF16)
        lo = (lf2 - hi.astype(F32)).astype(BF16)
        b = _dot(tri2, jnp.concatenate([hi, lo], axis=0))
        b_ref[u] = b
        bs.append(b)

    outs, qs, ks, fs = [], [], [], []
    for u, (c, h) in enumerate(units):
        b = bs[u]
        q = hq_ref[sl(c, h)].astype(F32)
        v_bf = hv_ref[sl(c, h)]
        f = jnp.exp2(lf_ref[sl(c, h)])
        k = 1.0 - f
        b_last = b[CHUNK - 1:CHUNK, :]
        st = st_ref[h]
        o = _dot_nt((q * jnp.exp2(b)).astype(BF16), st.astype(BF16))
        k_out = (k * jnp.exp2(b_last - b)).astype(BF16)
        st_ref[h] = st * jnp.exp2(b_last) + _dot_tn(v_bf, k_out)
        outs.append(o + jnp.sum(q * k, axis=-1, keepdims=True) * v_bf.astype(F32))
        qs.append(q)
        ks.append(k)
        fs.append(f)

    for g0 in range(0, len(units), HGRN_UNIT_GROUP):
        group = range(g0, min(g0 + HGRN_UNIT_GROUP, len(units)))
        accs = {u: jnp.zeros((CHUNK, CHUNK), F32) for u in group}
        for li, half in enumerate(_LEVEL_HALVES):
            for u in group:
                if half == 1:
                    e = jnp.where(odd, fs[u], 1.0)
                else:
                    e = jnp.exp2(-jnp.abs(bs[u] - _level_reference(b_ref, u, half)))
                accs[u] = jnp.where(masks[li], _dot_nt((qs[u] * e).astype(BF16), (ks[u] * e).astype(BF16)),
                                    accs[u])
        for u in group:
            c, h = units[u]
            o = outs[u] + _dot(accs[u].astype(BF16), hv_ref[sl(c, h)])
            y = _rms(o, nw_ref[...]) * hg_ref[sl(c, h)].astype(F32)
            o_ref[sl(c, h)] = y.astype(BF16)


def _hgrn(hq, lf, hv, hg, hgrn_norm_w, B, S, tb):
    nt = S // tb
    blk = pl.BlockSpec((tb, HGRN_WIDTH), lambda b, n: (b * nt + n, 0))
    tri = np.tril(np.ones((CHUNK, CHUNK), np.float32))
    tri = jnp.asarray(np.concatenate([tri, tri], axis=1), BF16)
    lm = _hgrn_level_masks()
    return pl.pallas_call(
        _hgrn_kernel,
        out_shape=jax.ShapeDtypeStruct((B * S, HGRN_WIDTH), BF16),
        grid=(B, nt),
        in_specs=[blk, blk, blk, blk,
                  pl.BlockSpec((1, HGRN_DIM), lambda b, n: (0, 0)),
                  pl.BlockSpec(tri.shape, lambda b, n: (0, 0)),
                  pl.BlockSpec(lm.shape, lambda b, n: (0, 0, 0))],
        out_specs=blk,
        scratch_shapes=[pltpu.VMEM((HGRN_HEADS, HGRN_DIM, HGRN_DIM), F32),
                        pltpu.VMEM((tb // CHUNK * HGRN_HEADS, CHUNK, HGRN_DIM), F32)],
        compiler_params=pltpu.CompilerParams(dimension_semantics=("arbitrary", "arbitrary"),
                                             vmem_limit_bytes=VMEM_LIMIT),
        name="hgrn2",
    )(hq, lf, hv, hg, hgrn_norm_w[None, :].astype(F32), tri, lm)


def _merge_kernel(x_ref, attn_ref, hgrn_ref, n1w_ref, wz_ref, wba_ref, wbh_ref, wout_ref, n2w_ref,
                  wr_ref, br_ref, utri_ref, x1_out, hn_out, route_out, count_out, run_ref):
    @pl.when(pl.program_id(0) == 0)
    def _():
        run_ref[...] = jnp.zeros_like(run_ref)

    x = x_ref[...]
    xn = _rms(x, n1w_ref[...]).astype(BF16)
    za = _sigmoid(_dot(xn, wz_ref[:, 0:D_MODEL]))
    zb = _sigmoid(_dot(xn, wz_ref[:, D_MODEL:2 * D_MODEL]))
    mixed = za * _dot(attn_ref[...], wba_ref[...]) + zb * _dot(hgrn_ref[...], wbh_ref[...])
    x1 = x + _dot(mixed.astype(BF16), wout_ref[...])
    x1_out[...] = x1
    hn = _rms(x1, n2w_ref[...])
    hn_out[...] = _pack_bf16_pairs(hn)

    logits = _dot_nt(wr_ref[...], hn.astype(BF16)) + br_ref[...]
    r = lax.broadcasted_iota(jnp.int32, logits.shape, 0).astype(F32)
    far = float(ROUTER_ROWS)
    cmax = lambda a: jnp.max(a, axis=0, keepdims=True)
    cmin = lambda a: jnp.min(a, axis=0, keepdims=True)
    csum = lambda a: jnp.sum(a, axis=0, keepdims=True)

    lg = jnp.where(r < N_GROUPS, logits, NEG_BIG)
    mg = cmax(lg)
    gsel = cmin(jnp.where(lg == mg, r, far))
    pgsel = 1.0 / csum(jnp.exp(lg - mg))

    lo = N_GROUPS + EXPERTS_PER_GROUP * gsel
    le = jnp.where((r >= lo) & (r < lo + EXPERTS_PER_GROUP), logits, NEG_BIG)
    m1 = cmax(le)
    i1 = cmin(jnp.where(le == m1, r, far))
    se = csum(jnp.exp(le - m1))
    le2 = jnp.where(r == i1, NEG_BIG, le)
    m2 = cmax(le2)
    i2 = cmin(jnp.where(le2 == m2, r, far))
    top0 = 1.0 / se
    top1 = jnp.exp(m2 - m1) / se
    tsum = top0 + top1
    w0 = pgsel * top0 / tsum
    w1 = pgsel * top1 / tsum

    sel1 = r == i1
    sel2 = r == i2
    onehot = jnp.where(sel1 | sel2, 1.0, 0.0)
    before = _dot(onehot.astype(BF16), utri_ref[...]) + run_ref[...]
    r0 = csum(jnp.where(sel1, before, 0.0))
    r1 = csum(jnp.where(sel2, before, 0.0))
    run_new = run_ref[...] + jnp.sum(onehot, axis=1, keepdims=True)
    run_ref[...] = run_new
    count_out[...] = run_new

    row8 = lax.broadcasted_iota(jnp.int32, route_out.shape, 0)
    vals = (i1 - N_GROUPS, i2 - N_GROUPS, w0, w1, r0, r1)
    route = jnp.zeros(route_out.shape, F32)
    for j, val in enumerate(vals):
        route = jnp.where(row8 == j, val, route)
    route_out[...] = route


def _merge(x2, attn, hgrn, norm1_w, w_z, w_ba, w_bh, w_out, norm2_w, w_r, br, tm):
    T = x2.shape[0]
    row = lambda w: pl.BlockSpec((tm, w), lambda i: (i, 0))
    full = lambda a: pl.BlockSpec(a.shape, lambda i: (0,) * a.ndim)
    utri = jnp.asarray(np.triu(np.ones((tm, tm), np.float32), 1), BF16)
    ins = [x2, attn, hgrn, norm1_w[None, :], w_z, w_ba, w_bh, w_out, norm2_w[None, :], w_r, br, utri]
    in_specs = [row(D_MODEL), row(ATTN_WIDTH), row(HGRN_WIDTH)] + [full(a) for a in ins[3:]]
    return pl.pallas_call(
        _merge_kernel,
        out_shape=[jax.ShapeDtypeStruct((T, D_MODEL), F32), jax.ShapeDtypeStruct((T, HALF), jnp.uint32),
                   jax.ShapeDtypeStruct((ROUTE_ROWS, T), F32), jax.ShapeDtypeStruct((ROUTER_ROWS, tm), F32)],
        grid=(T // tm,),
        in_specs=in_specs,
        out_specs=[row(D_MODEL), row(HALF), pl.BlockSpec((ROUTE_ROWS, tm), lambda i: (0, i)),
                   pl.BlockSpec((ROUTER_ROWS, tm), lambda i: (0, 0))],
        scratch_shapes=[pltpu.VMEM((ROUTER_ROWS, tm), F32)],
        compiler_params=pltpu.CompilerParams(dimension_semantics=("arbitrary",),
                                             vmem_limit_bytes=VMEM_LIMIT),
        name="merge_router",
    )(*ins)


HALF = D_MODEL // 2


def _pack_bf16_pairs(x):
    bits = pltpu.bitcast(x.astype(BF16).astype(F32), jnp.uint32)
    return (bits[:, :HALF] >> 16) | bits[:, HALF:]


def _unpack_bf16_pairs(words):
    lo = pltpu.bitcast(words << 16, F32)
    hi = pltpu.bitcast(words & jnp.uint32(0xFFFF0000), F32)
    return lo, hi


SC_CORES = 2
SC_SUBCORES = 16
SC_CHUNK = 64


def _sc_mesh():
    return plsc.VectorSubcoreMesh(core_axis_name="c", subcore_axis_name="s",
                                  num_cores=SC_CORES, num_subcores=SC_SUBCORES)


def _sc_worker():
    return lax.axis_index("s") * SC_CORES + lax.axis_index("c")


def _dispatch(pos, pad_rows, hn, n_rows):
    T, width = hn.shape
    workers = SC_CORES * SC_SUBCORES
    per_worker = T // workers
    n_chunks = per_worker // SC_CHUNK
    assert per_worker * workers == T and n_chunks * SC_CHUNK == per_worker and n_chunks % 2 == 0
    idx = pos.T.reshape(TOP_K, workers, n_chunks, SC_CHUNK).transpose(1, 0, 2, 3)
    pad_chunks = pad_rows.size // (workers * SC_CHUNK)
    pad_idx = pad_rows.reshape(workers, pad_chunks, SC_CHUNK)
    zeros = jnp.zeros((SC_CHUNK, width), hn.dtype)

    @functools.partial(
        pl.kernel, mesh=_sc_mesh(), name="moe_dispatch",
        out_type=jax.ShapeDtypeStruct((n_rows + SC_CHUNK, width), hn.dtype),
        scratch_types=[pltpu.VMEM((TOP_K, n_chunks, SC_CHUNK), jnp.int32),
                       pltpu.VMEM((2, SC_CHUNK, width), hn.dtype),
                       pltpu.VMEM((pad_chunks, SC_CHUNK), jnp.int32),
                       pltpu.VMEM((SC_CHUNK, width), hn.dtype),
                       pltpu.SemaphoreType.DMA((2,)), pltpu.SemaphoreType.DMA((2,)), pltpu.SemaphoreType.DMA])
    def dispatch_kernel(rows_hbm, idx_hbm, pad_hbm, zeros_hbm, out_hbm,
                        idx_v, rows_v, pad_v, zeros_v, load_sem, scatter_sem, pad_sem):
        wid = _sc_worker()
        base = wid * per_worker
        pltpu.sync_copy(idx_hbm.at[wid], idx_v)
        pltpu.sync_copy(pad_hbm.at[wid], pad_v)
        pltpu.sync_copy(zeros_hbm, zeros_v)

        def zero_fill(j):
            return pltpu.make_async_copy(zeros_v, out_hbm.at[pad_v.at[j]], pad_sem)

        for j in range(pad_chunks):
            zero_fill(j).start()

        def load(c, b):
            off = pl.multiple_of(c * SC_CHUNK, SC_CHUNK)
            return pltpu.make_async_copy(rows_hbm.at[pl.ds(base + off, SC_CHUNK)], rows_v.at[b], load_sem.at[b])

        def scatter(c, b, k):
            return pltpu.make_async_copy(rows_v.at[b], out_hbm.at[idx_v.at[k, c]], scatter_sem.at[b])

        load(0, 0).start()

        @pl.loop(0, n_chunks, step=2)
        def _(c0):
            for b in range(2):
                c = c0 + b
                load(c, b).wait()
                for k in range(TOP_K):
                    scatter(c, b, k).start()

                @pl.when(c >= 1)
                def _():
                    for k in range(TOP_K):
                        scatter(c - 1, 1 - b, k).wait()

                @pl.when(c + 1 < n_chunks)
                def _():
                    load(c + 1, 1 - b).start()

        for k in range(TOP_K):
            scatter(n_chunks - 1, (n_chunks - 1) % 2, k).wait()
        for j in range(pad_chunks):
            zero_fill(j).wait()

    return dispatch_kernel(hn, idx, pad_idx, zeros)


def _expert_kernel(tile_ref, exp_ref, valid_ref, x_ref, wg_ref, wu_ref, wd_ref, y_ref, wg_s, wu_s, wd_s):
    w = pl.program_id(0)
    prev = jnp.maximum(w - 1, 0)

    @pl.when((w == 0) | (exp_ref[w] != exp_ref[prev]))
    def _():
        wg_s[...] = wg_ref[0].astype(BF16)
        wu_s[...] = wu_ref[0].astype(BF16)
        wd_s[...] = wd_ref[0].astype(BF16)

    def ffn(words):
        lo, hi = _unpack_bf16_pairs(words)
        lo = lo.astype(BF16)
        hi = hi.astype(BF16)
        gate = _dot(lo, wg_s[0:HALF, :]) + _dot(hi, wg_s[HALF:, :])
        up = _dot(lo, wu_s[0:HALF, :]) + _dot(hi, wu_s[HALF:, :])
        return _pack_bf16_pairs(_dot((gate * _sigmoid(gate) * up).astype(BF16), wd_s[...]))

    @pl.when(valid_ref[w] != 0)
    def _():
        y_ref[...] = ffn(x_ref[...])


def _experts(items, xs, w_gate, w_up, w_down):
    tile, exp, valid = items
    wspec = lambda shape: pl.BlockSpec((1,) + shape, lambda w, t, e, v: (e[w], 0, 0))
    xspec = pl.BlockSpec((EXPERT_TILE, HALF), lambda w, t, e, v: (t[w], 0))
    return pl.pallas_call(
        _expert_kernel,
        out_shape=jax.ShapeDtypeStruct((tile.shape[0] * EXPERT_TILE, HALF), jnp.uint32),
        grid_spec=pltpu.PrefetchScalarGridSpec(
            num_scalar_prefetch=3,
            grid=(tile.shape[0],),
            in_specs=[xspec, wspec((D_MODEL, EXPERT_FF)), wspec((D_MODEL, EXPERT_FF)),
                      wspec((EXPERT_FF, D_MODEL))],
            out_specs=xspec,
            scratch_shapes=[pltpu.VMEM((D_MODEL, EXPERT_FF), BF16),
                            pltpu.VMEM((D_MODEL, EXPERT_FF), BF16),
                            pltpu.VMEM((EXPERT_FF, D_MODEL), BF16)]),
        compiler_params=pltpu.CompilerParams(dimension_semantics=("arbitrary",),
                                             vmem_limit_bytes=VMEM_LIMIT),
        name="moe_experts",
    )(tile, exp, valid, xs, w_gate, w_up, w_down)


def _sc_gather_rows(table, idx):
    n = idx.shape[0]
    width = table.shape[1]
    per_worker = n // (SC_CORES * SC_SUBCORES)
    n_chunks = per_worker // SC_CHUNK
    assert per_worker * SC_CORES * SC_SUBCORES == n and n_chunks * SC_CHUNK == per_worker and n_chunks % 2 == 0

    @functools.partial(
        pl.kernel, mesh=_sc_mesh(), name="moe_row_gather",
        out_type=jax.ShapeDtypeStruct((n, width), table.dtype),
        scratch_types=[pltpu.VMEM((per_worker,), jnp.int32),
                       pltpu.VMEM((2, SC_CHUNK, width), table.dtype),
                       pltpu.SemaphoreType.DMA((2,))])
    def gather_kernel(table_hbm, idx_hbm, out_hbm, idx_v, rows_v, sem):
        base = _sc_worker() * per_worker
        pltpu.sync_copy(idx_hbm.at[pl.ds(base, per_worker)], idx_v)

        def gather(c, b):
            off = pl.multiple_of(c * SC_CHUNK, SC_CHUNK)
            return pltpu.make_async_copy(table_hbm.at[idx_v.at[pl.ds(off, SC_CHUNK)]], rows_v.at[b], sem.at[b])

        gather(0, 0).start()

        @pl.loop(0, n_chunks, step=2)
        def _(c0):
            for b in range(2):
                c = c0 + b

                @pl.when(c + 1 < n_chunks)
                def _():
                    gather(c + 1, 1 - b).start()

                gather(c, b).wait()
                off = pl.multiple_of(c * SC_CHUNK, SC_CHUNK)
                pltpu.sync_copy(rows_v.at[b], out_hbm.at[pl.ds(base + off, SC_CHUNK)])

    return gather_kernel(table, idx)


def _combine_kernel(x1_ref, route_ref, y0_ref, y1_ref, o_ref):
    gates = route_ref[...].T
    w0 = gates[:, 2:3]
    w1 = gates[:, 3:4]
    lo0, hi0 = _unpack_bf16_pairs(y0_ref[...])
    lo1, hi1 = _unpack_bf16_pairs(y1_ref[...])
    o_ref[:, 0:HALF] = x1_ref[:, 0:HALF] + w0 * lo0 + w1 * lo1
    o_ref[:, HALF:] = x1_ref[:, HALF:] + w0 * hi0 + w1 * hi1


def _combine(pos, x1, route, y, tk):
    T = x1.shape[0]
    nt = T // tk
    ysel = _sc_gather_rows(y, pos.T.reshape(-1))
    return pl.pallas_call(
        _combine_kernel,
        out_shape=jax.ShapeDtypeStruct((T, D_MODEL), F32),
        grid=(nt,),
        in_specs=[pl.BlockSpec((tk, D_MODEL), lambda i: (i, 0)),
                  pl.BlockSpec((ROUTE_ROWS, tk), lambda i: (0, i)),
                  pl.BlockSpec((tk, HALF), lambda i: (i, 0)),
                  pl.BlockSpec((tk, HALF), lambda i: (i + nt, 0))],
        out_specs=pl.BlockSpec((tk, D_MODEL), lambda i: (i, 0)),
        compiler_params=pltpu.CompilerParams(dimension_semantics=("arbitrary",),
                                             vmem_limit_bytes=VMEM_LIMIT),
        name="moe_combine",
    )(x1, route, ysel, ysel)


def _routing_tables(route, counts, T):
    e = route[0:TOP_K].astype(jnp.int32)
    rank = route[4:4 + TOP_K].astype(jnp.int32)
    counts = counts.astype(jnp.int32)
    padded = (counts + EXPERT_TILE - 1) // EXPERT_TILE * EXPERT_TILE
    ends = jnp.cumsum(padded)
    starts = ends - padded
    ids = jnp.arange(N_EXPERTS, dtype=jnp.int32)
    pos = rank + jnp.sum(jnp.where(e[:, :, None] == ids, starts, 0), axis=-1)
    n_tiles = TOP_K * T // EXPERT_TILE + N_EXPERTS
    tile0 = jnp.arange(n_tiles, dtype=jnp.int32)
    valid = (tile0 * EXPERT_TILE < ends[-1]).astype(jnp.int32)
    tile = jnp.minimum(tile0, ends[-1] // EXPERT_TILE - 1)
    exp = jnp.minimum(jnp.sum(ends[None, :] <= (tile * EXPERT_TILE)[:, None], axis=1), N_EXPERTS - 1).astype(jnp.int32)
    j = jnp.arange(EXPERT_TILE, dtype=jnp.int32)[None, :]
    n_pad = (padded - counts)[:, None]
    spare = n_tiles * EXPERT_TILE + j % SC_CHUNK
    pad_rows = jnp.where(n_pad > 0, (starts + counts)[:, None] + j % jnp.maximum(n_pad, 1), spare)
    return pos.T, pad_rows, (tile, exp, valid)


def _pick_tile(n, pref):
    t = pref
    while n % t:
        t //= 2
    return t


def kernel(x, positions, norm1_w, w_in, q_norm_w, k_norm_w, attn_sinks, hgrn_lower_bounds, hgrn_norm_w,
           w_branch_attn, w_branch_hgrn, w_out, norm2_w, w_router_group, b_router_group, w_router_expert,
           b_router_expert, w_gate_experts, w_up_experts, w_down_experts):
    B, S, D = x.shape
    T = B * S
    x2 = x.reshape(T, D)
    tm = _pick_tile(T, 512)

    inv_freq = ROPE_THETA ** (-jnp.arange(0, ROT_DIM, 2, dtype=F32) / ROT_DIM)
    ang = inv_freq[:, None] * positions.astype(F32).reshape(1, T)
    cs = jnp.concatenate(_split3(jnp.concatenate([jnp.cos(ang), jnp.sin(ang)], axis=0)), axis=0)

    w_in0 = w_in[0]
    heads = lambda off: [w_in0[:, off + h * HEAD_DIM:off + (h + 1) * HEAD_DIM]
                         for h in range(N_KV_HEADS) for _ in range(2)]
    w_in_a = jnp.concatenate([w_in0[:, :_OFF_K]] + heads(_OFF_K) + heads(_OFF_V) + [w_in0[:, _OFF_HQ:_OFF_Z]],
                             axis=1).astype(BF16)
    w_z = w_in0[:, _OFF_Z:].astype(BF16)

    q, k, v, hq, lf, hv, hg = _inproj(x2, norm1_w[0], w_in_a, cs, q_norm_w[0], k_norm_w[0],
                                      hgrn_lower_bounds.astype(F32), _pick_tile(T, 1024))
    attn = _attention(q, k, v, attn_sinks[0].astype(F32), B, S)
    hgrn = _hgrn(hq, lf, hv, hg, hgrn_norm_w[0], B, S, _pick_tile(S, 1024))

    pad = ROUTER_ROWS - N_GROUPS - N_EXPERTS
    w_r = jnp.concatenate([w_router_group[0].T, w_router_expert[0].T, jnp.zeros((pad, D), F32)], axis=0)
    b_r =jnp.concatenate([b_router_group[0], b_router_expert[0], jnp.zeros((pad,), F32)]).astype(F32)
    b_r = jnp.broadcast_to(b_r[:, None], (ROUTER_ROWS, tm))

    x1, hn, route, counts = _merge(x2, attn, hgrn, norm1_w[0], w_z, w_branch_attn[0].astype(BF16),
                                   w_branch_hgrn[0].astype(BF16), w_out[0].astype(BF16), norm2_w[0],
                                   w_r.astype(BF16), b_r, tm)

    pos, pad_rows, items = _routing_tables(route, counts[N_GROUPS:N_GROUPS + N_EXPERTS, 0], T)
    tk = _pick_tile(T, 512)
    xs = _dispatch(pos, pad_rows, hn, items[0].shape[0] * EXPERT_TILE)
    y = _experts(items, xs, w_gate_experts[0], w_up_experts[0], w_down_experts[0])
    out = _combine(pos, x1, route, y, tk)
    return out.reshape(B, S, D)
```

```python
import functools

import numpy as np
import jax
import jax.numpy as jnp
from jax import lax
from jax.experimental import pallas as pl
from jax.experimental.pallas import tpu as pltpu
from jax.experimental.pallas import tpu_sc as plsc

F32 = jnp.float32
BF16 = jnp.bfloat16

D_MODEL = 1024
N_Q_HEADS = 8
N_KV_HEADS = 2
GROUP = N_Q_HEADS // N_KV_HEADS
HEAD_DIM = 64
ROT_DIM = HEAD_DIM // 4
ROT_HALF = ROT_DIM // 2
ROPE_THETA = 500000.0
WINDOW = 128
ATTN_WIDTH = N_Q_HEADS * HEAD_DIM
KV_WIDTH = N_KV_HEADS * HEAD_DIM

HGRN_HEADS = 4
HGRN_DIM = 128
HGRN_WIDTH = HGRN_HEADS * HGRN_DIM
CHUNK = 64

N_GROUPS = 4
EXPERTS_PER_GROUP = 8
N_EXPERTS = N_GROUPS * EXPERTS_PER_GROUP
TOP_K = 2
EXPERT_FF = 512
EXPERT_TILE = 512
NORM_EPS = 1e-6
ROUTER_ROWS = 40
ROUTE_ROWS = 8

LANES = 128
NEG_BIG = -1e30
LOG2_E = 1.4426950408889634

_OFF_Q, _OFF_K, _OFF_V = 0, ATTN_WIDTH, ATTN_WIDTH + KV_WIDTH
_OFF_HQ = ATTN_WIDTH + 2 * KV_WIDTH
_OFF_HF = _OFF_HQ + HGRN_WIDTH
_OFF_HI = _OFF_HF + HGRN_WIDTH
_OFF_HG = _OFF_HI + HGRN_WIDTH
_OFF_Z = _OFF_HG + HGRN_WIDTH
_A_Q, _A_K, _A_V = 0, ATTN_WIDTH, ATTN_WIDTH + 2 * KV_WIDTH
_A_HQ = ATTN_WIDTH + 4 * KV_WIDTH
_A_HF = _A_HQ + HGRN_WIDTH
_A_HI = _A_HF + HGRN_WIDTH
_A_HG = _A_HI + HGRN_WIDTH

VMEM_LIMIT = 56 * 1024 * 1024


def _split3(a):
    hi = a.astype(BF16)
    r1 = a - hi.astype(F32)
    mid = r1.astype(BF16)
    lo = (r1 - mid.astype(F32)).astype(BF16)
    return hi, mid, lo


def _dot(a, b):
    return jnp.dot(a, b, preferred_element_type=F32)


def _dot_nt(a, b):
    return lax.dot_general(a, b, (((1,), (1,)), ((), ())), preferred_element_type=F32)


def _dot_tn(a, b):
    return lax.dot_general(a, b, (((0,), (0,)), ((), ())), preferred_element_type=F32)


def _sigmoid(x):
    return 1.0 / (1.0 + jnp.exp(-x))


def _rms(x, w):
    ms = jnp.mean(x * x, axis=-1, keepdims=True)
    return x * lax.rsqrt(ms + NORM_EPS) * w


def _inproj_kernel(x_ref, n1w_ref, w_ref, cs_ref, rope_e_ref, rope_c0_ref, qw_ref, kw_ref,
                   mq_ref, mk_ref, lbp_ref,
                   q_out, k_out, v_out, hq_out, lf_out, hv_out, hg_out):
    xn = _rms(x_ref[...], n1w_ref[...]).astype(BF16)

    def proj(off, width):
        return _dot(xn, w_ref[:, off:off + width])

    tabs = _dot_tn(cs_ref[...], rope_e_ref[...])
    c_tab = tabs[:, 0:LANES] + rope_c0_ref[...]
    s1_tab = tabs[:, LANES:2 * LANES]
    s2_tab = tabs[:, 2 * LANES:3 * LANES]

    def norm_rope(t, mavg_ref, w_row, scale):
        ms = _dot((t * t).astype(BF16), mavg_ref[...])
        tn = t * lax.rsqrt(ms + NORM_EPS) * w_row
        if scale != 1.0:
            tn = tn * scale
        outs = []
        for j in range(t.shape[1] // LANES):
            c = tn[:, j * LANES:(j + 1) * LANES]
            outs.append(c * c_tab
                        + pltpu.roll(c, LANES - ROT_HALF, 1) * s1_tab
                        + pltpu.roll(c, ROT_HALF, 1) * s2_tab)
        return outs[0] if len(outs) == 1 else jnp.concatenate(outs, axis=1)

    q_out[...] = norm_rope(proj(_A_Q, ATTN_WIDTH), mq_ref, qw_ref[...], HEAD_DIM ** -0.5).astype(BF16)
    k_out[...] = norm_rope(proj(_A_K, 2 * KV_WIDTH), mk_ref, kw_ref[...], 1.0).astype(BF16)
    v_out[...] = proj(_A_V, 2 * KV_WIDTH).astype(BF16)

    hq = proj(_A_HQ, HGRN_WIDTH)
    hq_out[...] = (hq * _sigmoid(hq)).astype(BF16)
    h0 = lbp_ref[0:1, :]
    h1 = lbp_ref[1:2, :]
    hm = jnp.maximum(h0, h1)
    e0 = jnp.exp(h0 - hm)
    e1 = jnp.exp(h1 - hm)
    lb = e0 / (e0 + e1)
    fg = lb + (1.0 - lb) * _sigmoid(proj(_A_HF, HGRN_WIDTH))
    lf_out[...] = jnp.log(fg) * LOG2_E
    hv_out[...] = proj(_A_HI, HGRN_WIDTH).astype(BF16)
    hg = proj(_A_HG, HGRN_WIDTH)
    hg_out[...] = (hg * _sigmoid(hg)).astype(BF16)


def _rope_constants():
    e = np.zeros((2 * ROT_HALF, 3 * LANES), np.float32)
    c0 = np.zeros((1, LANES), np.float32)
    for lane in range(LANES):
        d = lane % HEAD_DIM
        if d < ROT_HALF:
            e[d, lane] = 1.0
            e[ROT_HALF + d, LANES + lane] = -1.0
        elif d < ROT_DIM:
            e[d - ROT_HALF, lane] = 1.0
            e[ROT_HALF + d - ROT_HALF, 2 * LANES + lane] = 1.0
        else:
            c0[0, lane] = 1.0
    return jnp.asarray(np.concatenate([e, e, e], axis=0), BF16), jnp.asarray(c0, F32)


def _head_mean_matrix(width):
    idx = np.arange(width) // HEAD_DIM
    m = (idx[:, None] == idx[None, :]).astype(np.float32) / HEAD_DIM
    return jnp.asarray(m, BF16)


def _inproj(x2, norm1_w, w_in_a, cs, q_norm_w, k_norm_w, lbp, tm):
    T = x2.shape[0]
    rope_e, rope_c0 = _rope_constants()
    qw = jnp.tile(q_norm_w.astype(F32), N_Q_HEADS)[None, :]
    kw = jnp.tile(k_norm_w.astype(F32), 2 * N_KV_HEADS)[None, :]
    mq = _head_mean_matrix(ATTN_WIDTH)
    mk = _head_mean_matrix(2 * KV_WIDTH)
    row = lambda w: pl.BlockSpec((tm, w), lambda i: (i, 0))
    full = lambda a: pl.BlockSpec(a.shape, lambda i: (0,) * a.ndim)
    ins = [x2, norm1_w[None, :], w_in_a, cs, rope_e, rope_c0, qw, kw, mq, mk, lbp]
    in_specs = ([row(D_MODEL), full(ins[1]), full(w_in_a), pl.BlockSpec((cs.shape[0], tm), lambda i: (0, i))]
                + [full(a) for a in ins[4:]])
    outs = [(ATTN_WIDTH, BF16), (2 * KV_WIDTH, BF16), (2 * KV_WIDTH, BF16), (HGRN_WIDTH, BF16),
            (HGRN_WIDTH, F32), (HGRN_WIDTH, BF16), (HGRN_WIDTH, BF16)]
    return pl.pallas_call(
        _inproj_kernel,
        out_shape=[jax.ShapeDtypeStruct((T, w), dt) for w, dt in outs],
        grid=(T // tm,),
        in_specs=in_specs,
        out_specs=[row(w) for w, _ in outs],
        compiler_params=pltpu.CompilerParams(dimension_semantics=("arbitrary",),
                                             vmem_limit_bytes=VMEM_LIMIT),
        name="inproj",
    )(*ins)


ATTN_QBLOCKS = 8


def _attn_kernel(sink_ref, q_ref, kc_ref, kp_ref, vc_ref, vp_ref, half_ref, o_ref):
    n_qblocks = q_ref.shape[0] // WINDOW
    has_prev = pl.program_id(1) > 0
    qi = lax.broadcasted_iota(jnp.int32, (WINDOW, 2 * WINDOW), 0)
    kj = lax.broadcasted_iota(jnp.int32, (WINDOW, 2 * WINDOW), 1)
    in_window = ((kj < WINDOW) & (kj > qi)) | ((kj >= WINDOW) & (kj - WINDOW <= qi))
    first_valid = in_window & ((kj >= WINDOW) | has_prev)
    left = lax.broadcasted_iota(jnp.int32, (WINDOW, LANES), 1) < HEAD_DIM
    half = (half_ref[0], half_ref[1])

    ks, rhs = [], []
    for h in range(N_KV_HEADS):
        cols = slice(h * LANES, (h + 1) * LANES)
        kall = jnp.concatenate([kp_ref[:, cols], kc_ref[:, cols]], axis=0)
        vall = jnp.concatenate([vp_ref[:, cols], vc_ref[:, cols]], axis=0)
        ks.append([kall * hm for hm in half])
        rhs.append([jnp.concatenate([vall * hm, hm], axis=1) for hm in half])

    units = [(j, h, pr, side) for j in range(n_qblocks) for h in range(N_KV_HEADS)
             for pr in range(GROUP // 2) for side in range(2)]
    scores = []
    for j, h, pr, side in units:
        pair = h * (GROUP // 2) + pr
        qp = q_ref[j * WINDOW:(j + 1) * WINDOW, pair * LANES:(pair + 1) * LANES]
        s = _dot_nt(qp, ks[h][side][j * WINDOW:(j + 2) * WINDOW])
        scores.append(jnp.where(first_valid if j == 0 else in_window, s, NEG_BIG))
    probs, sink_terms = [], []
    for (j, h, pr, side), s in zip(units, scores):
        sink = sink_ref[2 * (h * (GROUP // 2) + pr) + side]
        m = jnp.maximum(jnp.max(s, axis=-1, keepdims=True), sink)
        probs.append(jnp.exp(s - m).astype(BF16))
        sink_terms.append(jnp.exp(sink - m))
    acc = []
    for (j, h, pr, side), p in zip(units, probs):
        acc.append(_dot(p, rhs[h][side][j * WINDOW:(j + 2) * WINDOW]))
    for j in range(n_qblocks):
        outs = []
        for u in range(0, len(units), 2):
            if units[u][0] == j:
                both = acc[u] + acc[u + 1]
                den = both[:, LANES:2 * LANES] + jnp.where(left, sink_terms[u], sink_terms[u + 1])
                outs.append(both[:, 0:LANES] / den)
        o_ref[j * WINDOW:(j + 1) * WINDOW, :] = jnp.concatenate(outs, axis=1).astype(BF16)


def _attention(q, k, v, sinks, B, S):
    qblocks = _pick_tile(S // WINDOW, ATTN_QBLOCKS)
    rows = qblocks * WINDOW
    nb = S // rows
    cur = lambda b, n: (b * nb + n, 0)
    prev = lambda b, n: (jnp.maximum((b * nb + n) * qblocks - 1, 0), 0)
    lane_left = np.arange(LANES) < HEAD_DIM
    half = jnp.asarray(np.broadcast_to(np.stack([lane_left, ~lane_left])[:, None, :],
                                       (2, rows + WINDOW, LANES)), BF16)
    return pl.pallas_call(
        _attn_kernel,
        out_shape=jax.ShapeDtypeStruct((B * S, ATTN_WIDTH), BF16),
        grid=(B, nb),
        in_specs=[pl.BlockSpec(memory_space=pltpu.SMEM),
                  pl.BlockSpec((rows, ATTN_WIDTH), cur),
                  pl.BlockSpec((rows, 2 * KV_WIDTH), cur),
                  pl.BlockSpec((WINDOW, 2 * KV_WIDTH), prev),
                  pl.BlockSpec((rows, 2 * KV_WIDTH), cur),
                  pl.BlockSpec((WINDOW, 2 * KV_WIDTH), prev),
                  pl.BlockSpec(half.shape, lambda b, n: (0, 0, 0))],
        out_specs=pl.BlockSpec((rows, ATTN_WIDTH), cur),
        compiler_params=pltpu.CompilerParams(dimension_semantics=("arbitrary", "arbitrary"),
                                             vmem_limit_bytes=VMEM_LIMIT),
        name="swa_attention",
    )(sinks, q, k, k, v, v, half)


_LEVEL_HALVES = (1, 2, 4, 8, 16, 32)


def _hgrn_level_masks():
    t = np.arange(CHUNK)[:, None]
    s = np.arange(CHUNK)[None, :]
    masks = [((t // (2 * h)) == (s // (2 * h))) & ((t & h) != 0) & ((s & h) == 0) for h in _LEVEL_HALVES]
    return jnp.asarray(np.stack(masks), F32)


def _level_reference(b_ref, slot, half):
    if half >= 4:
        span = max(2 * half, 8)
        pieces = [jnp.broadcast_to(b_ref[slot, s + half - 1:s + half, :], (span, HGRN_DIM))
                  for s in range(0, CHUNK, span)]
    else:
        r8 = lax.broadcasted_iota(jnp.int32, (8, HGRN_DIM), 0)
        pieces = [jnp.where(r8 < 4,
                            jnp.broadcast_to(b_ref[slot, s + 1:s + 2, :], (8, HGRN_DIM)),
                            jnp.broadcast_to(b_ref[slot, s + 5:s + 6, :], (8, HGRN_DIM)))
                  for s in range(0, CHUNK, 8)]
    return pieces[0] if len(pieces) == 1 else jnp.concatenate(pieces, axis=0)


def _hgrn_kernel(hq_ref, lf_ref, hv_ref, hg_ref, nw_ref, tri_ref, lm_ref, o_ref, st_ref, b_ref):
    @pl.when(pl.program_id(1) == 0)
    def _():
        st_ref[...] = jnp.zeros_like(st_ref)

    tri2 = tri_ref[...]
    odd = (lax.broadcasted_iota(jnp.int32, (CHUNK, HGRN_DIM), 0) & 1) != 0
    masks = [lm_ref[li] != 0.0 for li in range(len(_LEVEL_HALVES))]
    units = [(c, h) for c in range(hq_ref.shape[0] // CHUNK) for h in range(HGRN_HEADS)]
    sl = lambda c, h: (slice(c * CHUNK, (c + 1) * CHUNK), slice(h * HGRN_DIM, (h + 1) * HGRN_DIM))

    bs = []
    for u, (c, h) in enumerate(units):
        lf2 = lf_ref[sl(c, h)]
        hi = lf2.astype(BF16)
        lo = (lf2 - hi.astype(F32)).astype(BF16)
        b = _dot(tri2, jnp.concatenate([hi, lo], axis=0))
        b_ref[u] = b
        bs.append(b)

    outs, accs = [], []
    for u, (c, h) in enumerate(units):
        b = bs[u]
        q = hq_ref[sl(c, h)].astype(F32)
        v_bf = hv_ref[sl(c, h)]
        f = jnp.exp2(lf_ref[sl(c, h)])
        k = 1.0 - f
        b_last = b[CHUNK - 1:CHUNK, :]
        st = st_ref[h]
        o = _dot_nt((q * jnp.exp2(b)).astype(BF16), st.astype(BF16))
        k_out = (k * jnp.exp2(b_last - b)).astype(BF16)
        st_ref[h] = st * jnp.exp2(b_last) + _dot_tn(v_bf, k_out)
        outs.append(o + jnp.sum(q * k, axis=-1, keepdims=True) * v_bf.astype(F32))
        acc = jnp.zeros((CHUNK, CHUNK), F32)
        for li, half in enumerate(_LEVEL_HALVES):
            if half == 1:
                e = jnp.where(odd, f, 1.0)
            else:
                e = jnp.exp2(-jnp.abs(b - _level_reference(b_ref, u, half)))
            acc = jnp.where(masks[li], _dot_nt((q * e).astype(BF16), (k * e).astype(BF16)), acc)
        accs.append(acc)

    for u, (c, h) in enumerate(units):
        o = outs[u] + _dot(accs[u].astype(BF16), hv_ref[sl(c, h)])
        y = _rms(o, nw_ref[...]) * hg_ref[sl(c, h)].astype(F32)
        o_ref[sl(c, h)] = y.astype(BF16)


def _hgrn(hq, lf, hv, hg, hgrn_norm_w, B, S, tb):
    nt = S // tb
    blk = pl.BlockSpec((tb, HGRN_WIDTH), lambda b, n: (b * nt + n, 0))
    tri = np.tril(np.ones((CHUNK, CHUNK), np.float32))
    tri = jnp.asarray(np.concatenate([tri, tri], axis=1), BF16)
    lm = _hgrn_level_masks()
    return pl.pallas_call(
        _hgrn_kernel,
        out_shape=jax.ShapeDtypeStruct((B * S, HGRN_WIDTH), BF16),
        grid=(B, nt),
        in_specs=[blk, blk, blk, blk,
                  pl.BlockSpec((1, HGRN_DIM), lambda b, n: (0, 0)),
                  pl.BlockSpec(tri.shape, lambda b, n: (0, 0)),
                  pl.BlockSpec(lm.shape, lambda b, n: (0, 0, 0))],
        out_specs=blk,
        scratch_shapes=[pltpu.VMEM((HGRN_HEADS, HGRN_DIM, HGRN_DIM), F32),
                        pltpu.VMEM((tb // CHUNK * HGRN_HEADS, CHUNK, HGRN_DIM), F32)],
        compiler_params=pltpu.CompilerParams(dimension_semantics=("arbitrary", "arbitrary"),
                                             vmem_limit_bytes=VMEM_LIMIT),
        name="hgrn2",
    )(hq, lf, hv, hg, hgrn_norm_w[None, :].astype(F32), tri, lm)


def _merge_kernel(x_ref, attn_ref, hgrn_ref, n1w_ref, wz_ref, wba_ref, wbh_ref, wout_ref, n2w_ref,
                  wr_ref, br_ref, utri_ref, x1_out, hn_out, route_out, count_out, run_ref):
    @pl.when(pl.program_id(0) == 0)
    def _():
        run_ref[...] = jnp.zeros_like(run_ref)

    x = x_ref[...]
    xn = _rms(x, n1w_ref[...]).astype(BF16)
    za = _sigmoid(_dot(xn, wz_ref[:, 0:D_MODEL]))
    zb = _sigmoid(_dot(xn, wz_ref[:, D_MODEL:2 * D_MODEL]))
    mixed = za * _dot(attn_ref[...], wba_ref[...]) + zb * _dot(hgrn_ref[...], wbh_ref[...])
    x1 = x + _dot(mixed.astype(BF16), wout_ref[...])
    x1_out[...] = x1
    hn = _rms(x1, n2w_ref[...])
    hn_out[...] = _pack_bf16_pairs(hn)

    logits = _dot_nt(wr_ref[...], hn.astype(BF16)) + br_ref[...]
    r = lax.broadcasted_iota(jnp.int32, logits.shape, 0).astype(F32)
    far = float(ROUTER_ROWS)
    cmax = lambda a: jnp.max(a, axis=0, keepdims=True)
    cmin = lambda a: jnp.min(a, axis=0, keepdims=True)
    csum = lambda a: jnp.sum(a, axis=0, keepdims=True)

    lg = jnp.where(r < N_GROUPS, logits, NEG_BIG)
    mg = cmax(lg)
    gsel = cmin(jnp.where(lg == mg, r, far))
    pgsel = 1.0 / csum(jnp.exp(lg - mg))

    lo = N_GROUPS + EXPERTS_PER_GROUP * gsel
    le = jnp.where((r >= lo) & (r < lo + EXPERTS_PER_GROUP), logits, NEG_BIG)
    m1 = cmax(le)
    i1 = cmin(jnp.where(le == m1, r, far))
    se = csum(jnp.exp(le - m1))
    le2 = jnp.where(r == i1, NEG_BIG, le)
    m2 = cmax(le2)
    i2 = cmin(jnp.where(le2 == m2, r, far))
    top0 = 1.0 / se
    top1 = jnp.exp(m2 - m1) / se
    tsum = top0 + top1
    w0 = pgsel * top0 / tsum
    w1 = pgsel * top1 / tsum

    sel1 = r == i1
    sel2 = r == i2
    onehot = jnp.where(sel1 | sel2, 1.0, 0.0)
    before = _dot(onehot.astype(BF16), utri_ref[...]) + run_ref[...]
    r0 = csum(jnp.where(sel1, before, 0.0))
    r1 = csum(jnp.where(sel2, before, 0.0))
    run_new = run_ref[...] + jnp.sum(onehot, axis=1, keepdims=True)
    run_ref[...] = run_new
    count_out[...] = run_new

    row8 = lax.broadcasted_iota(jnp.int32, route_out.shape, 0)
    vals = (i1 - N_GROUPS, i2 - N_GROUPS, w0, w1, r0, r1)
    route = jnp.zeros(route_out.shape, F32)
    for j, val in enumerate(vals):
        route = jnp.where(row8 == j, val, route)
    route_out[...] = route


def _merge(x2, attn, hgrn, norm1_w, w_z, w_ba, w_bh, w_out, norm2_w, w_r, br, tm):
    T = x2.shape[0]
    row = lambda w: pl.BlockSpec((tm, w), lambda i: (i, 0))
    full = lambda a: pl.BlockSpec(a.shape, lambda i: (0,) * a.ndim)
    utri = jnp.asarray(np.triu(np.ones((tm, tm), np.float32), 1), BF16)
    ins = [x2, attn, hgrn, norm1_w[None, :], w_z, w_ba, w_bh, w_out, norm2_w[None, :], w_r, br, utri]
    in_specs = [row(D_MODEL), row(ATTN_WIDTH), row(HGRN_WIDTH)] + [full(a) for a in ins[3:]]
    return pl.pallas_call(
        _merge_kernel,
        out_shape=[jax.ShapeDtypeStruct((T, D_MODEL), F32), jax.ShapeDtypeStruct((T, HALF), jnp.uint32),
                   jax.ShapeDtypeStruct((ROUTE_ROWS, T), F32), jax.ShapeDtypeStruct((ROUTER_ROWS, tm), F32)],
        grid=(T // tm,),
        in_specs=in_specs,
        out_specs=[row(D_MODEL), row(HALF), pl.BlockSpec((ROUTE_ROWS, tm), lambda i: (0, i)),
                   pl.BlockSpec((ROUTER_ROWS, tm), lambda i: (0, 0))],
        scratch_shapes=[pltpu.VMEM((ROUTER_ROWS, tm), F32)],
        compiler_params=pltpu.CompilerParams(dimension_semantics=("arbitrary",),
                                             vmem_limit_bytes=VMEM_LIMIT),
        name="merge_router",
    )(*ins)


HALF = D_MODEL // 2


def _pack_bf16_pairs(x):
    bits = pltpu.bitcast(x.astype(BF16).astype(F32), jnp.uint32)
    return (bits[:, :HALF] >> 16) | bits[:, HALF:]


def _unpack_bf16_pairs(words):
    lo = pltpu.bitcast(words << 16, F32)
    hi = pltpu.bitcast(words & jnp.uint32(0xFFFF0000), F32)
    return lo, hi


SC_CORES = 2
SC_SUBCORES = 16
SC_CHUNK = 64


def _sc_mesh():
    return plsc.VectorSubcoreMesh(core_axis_name="c", subcore_axis_name="s",
                                  num_cores=SC_CORES, num_subcores=SC_SUBCORES)


def _sc_worker():
    return lax.axis_index("s") * SC_CORES + lax.axis_index("c")


def _dispatch(pos, pad_rows, hn, n_rows):
    T, width = hn.shape
    workers = SC_CORES * SC_SUBCORES
    per_worker = T // workers
    n_chunks = per_worker // SC_CHUNK
    assert per_worker * workers == T and n_chunks * SC_CHUNK == per_worker and n_chunks % 2 == 0
    idx = pos.T.reshape(TOP_K, workers, n_chunks, SC_CHUNK).transpose(1, 0, 2, 3)
    pad_chunks = pad_rows.size // (workers * SC_CHUNK)
    pad_idx = pad_rows.reshape(workers, pad_chunks, SC_CHUNK)
    zeros = jnp.zeros((SC_CHUNK, width), hn.dtype)

    @functools.partial(
        pl.kernel, mesh=_sc_mesh(), name="moe_dispatch",
        out_type=jax.ShapeDtypeStruct((n_rows + SC_CHUNK, width), hn.dtype),
        scratch_types=[pltpu.VMEM((TOP_K, n_chunks, SC_CHUNK), jnp.int32),
                       pltpu.VMEM((2, SC_CHUNK, width), hn.dtype),
                       pltpu.VMEM((pad_chunks, SC_CHUNK), jnp.int32),
                       pltpu.VMEM((SC_CHUNK, width), hn.dtype),
                       pltpu.SemaphoreType.DMA((2,)), pltpu.SemaphoreType.DMA((2,)), pltpu.SemaphoreType.DMA])
    def dispatch_kernel(rows_hbm, idx_hbm, pad_hbm, zeros_hbm, out_hbm,
                        idx_v, rows_v, pad_v, zeros_v, load_sem, scatter_sem, pad_sem):
        wid = _sc_worker()
        base = wid * per_worker
        pltpu.sync_copy(idx_hbm.at[wid], idx_v)
        pltpu.sync_copy(pad_hbm.at[wid], pad_v)
        pltpu.sync_copy(zeros_hbm, zeros_v)

        def zero_fill(j):
            return pltpu.make_async_copy(zeros_v, out_hbm.at[pad_v.at[j]], pad_sem)

        for j in range(pad_chunks):
            zero_fill(j).start()

        def load(c, b):
            off = pl.multiple_of(c * SC_CHUNK, SC_CHUNK)
            return pltpu.make_async_copy(rows_hbm.at[pl.ds(base + off, SC_CHUNK)], rows_v.at[b], load_sem.at[b])

        def scatter(c, b, k):
            return pltpu.make_async_copy(rows_v.at[b], out_hbm.at[idx_v.at[k, c]], scatter_sem.at[b])

        load(0, 0).start()

        @pl.loop(0, n_chunks, step=2)
        def _(c0):
            for b in range(2):
                c = c0 + b
                load(c, b).wait()
                for k in range(TOP_K):
                    scatter(c, b, k).start()

                @pl.when(c >= 1)
                def _():
                    for k in range(TOP_K):
                        scatter(c - 1, 1 - b, k).wait()

                @pl.when(c + 1 < n_chunks)
                def _():
                    load(c + 1, 1 - b).start()

        for k in range(TOP_K):
            scatter(n_chunks - 1, (n_chunks - 1) % 2, k).wait()
        for j in range(pad_chunks):
            zero_fill(j).wait()

    return dispatch_kernel(hn, idx, pad_idx, zeros)


def _expert_kernel(tile_ref, exp_ref, valid_ref, x_ref, wg_ref, wu_ref, wd_ref, y_ref, wg_s, wu_s, wd_s):
    w = pl.program_id(0)
    prev = jnp.maximum(w - 1, 0)

    @pl.when((w == 0) | (exp_ref[w] != exp_ref[prev]))
    def _():
        wg_s[...] = wg_ref[0].astype(BF16)
        wu_s[...] = wu_ref[0].astype(BF16)
        wd_s[...] = wd_ref[0].astype(BF16)

    def ffn(words):
        lo, hi = _unpack_bf16_pairs(words)
        lo = lo.astype(BF16)
        hi = hi.astype(BF16)
        gate = _dot(lo, wg_s[0:HALF, :]) + _dot(hi, wg_s[HALF:, :])
        up = _dot(lo, wu_s[0:HALF, :]) + _dot(hi, wu_s[HALF:, :])
        return _pack_bf16_pairs(_dot((gate * _sigmoid(gate) * up).astype(BF16), wd_s[...]))

    @pl.when(valid_ref[w] != 0)
    def _():
        y_ref[...] = ffn(x_ref[...])


def _experts(items, xs, w_gate, w_up, w_down):
    tile, exp, valid = items
    wspec = lambda shape: pl.BlockSpec((1,) + shape, lambda w, t, e, v: (e[w], 0, 0))
    xspec = pl.BlockSpec((EXPERT_TILE, HALF), lambda w, t, e, v: (t[w], 0))
    return pl.pallas_call(
        _expert_kernel,
        out_shape=jax.ShapeDtypeStruct((tile.shape[0] * EXPERT_TILE, HALF), jnp.uint32),
        grid_spec=pltpu.PrefetchScalarGridSpec(
            num_scalar_prefetch=3,
            grid=(tile.shape[0],),
            in_specs=[xspec, wspec((D_MODEL, EXPERT_FF)), wspec((D_MODEL, EXPERT_FF)),
                      wspec((EXPERT_FF, D_MODEL))],
            out_specs=xspec,
            scratch_shapes=[pltpu.VMEM((D_MODEL, EXPERT_FF), BF16),
                            pltpu.VMEM((D_MODEL, EXPERT_FF), BF16),
                            pltpu.VMEM((EXPERT_FF, D_MODEL), BF16)]),
        compiler_params=pltpu.CompilerParams(dimension_semantics=("arbitrary",),
                                             vmem_limit_bytes=VMEM_LIMIT),
        name="moe_experts",
    )(tile, exp, valid, xs, w_gate, w_up, w_down)


def _sc_gather_rows(table, idx):
    n = idx.shape[0]
    width = table.shape[1]
    per_worker = n // (SC_CORES * SC_SUBCORES)
    n_chunks = per_worker // SC_CHUNK
    assert per_worker * SC_CORES * SC_SUBCORES == n and n_chunks * SC_CHUNK == per_worker and n_chunks % 2 == 0

    @functools.partial(
        pl.kernel, mesh=_sc_mesh(), name="moe_row_gather",
        out_type=jax.ShapeDtypeStruct((n, width), table.dtype),
        scratch_types=[pltpu.VMEM((per_worker,), jnp.int32),
                       pltpu.VMEM((2, SC_CHUNK, width), table.dtype),
                       pltpu.SemaphoreType.DMA((2,))])
    def gather_kernel(table_hbm, idx_hbm, out_hbm, idx_v, rows_v, sem):
        base = _sc_worker() * per_worker
        pltpu.sync_copy(idx_hbm.at[pl.ds(base, per_worker)], idx_v)

        def gather(c, b):
            off = pl.multiple_of(c * SC_CHUNK, SC_CHUNK)
            return pltpu.make_async_copy(table_hbm.at[idx_v.at[pl.ds(off, SC_CHUNK)]], rows_v.at[b], sem.at[b])

        gather(0, 0).start()

        @pl.loop(0, n_chunks, step=2)
        def _(c0):
            for b in range(2):
                c = c0 + b

                @pl.when(c + 1 < n_chunks)
                def _():
                    gather(c + 1, 1 - b).start()

                gather(c, b).wait()
                off = pl.multiple_of(c * SC_CHUNK, SC_CHUNK)
                pltpu.sync_copy(rows_v.at[b], out_hbm.at[pl.ds(base + off, SC_CHUNK)])

    return gather_kernel(table, idx)


def _combine_kernel(x1_ref, route_ref, y0_ref, y1_ref, o_ref):
    gates = route_ref[...].T
    w0 = gates[:, 2:3]
    w1 = gates[:, 3:4]
    lo0, hi0 = _unpack_bf16_pairs(y0_ref[...])
    lo1, hi1 = _unpack_bf16_pairs(y1_ref[...])
    o_ref[:, 0:HALF] = x1_ref[:, 0:HALF] + w0 * lo0 + w1 * lo1
    o_ref[:, HALF:] = x1_ref[:, HALF:] + w0 * hi0 + w1 * hi1


def _combine(pos, x1, route, y, tk):
    T = x1.shape[0]
    nt = T // tk
    ysel = _sc_gather_rows(y, pos.T.reshape(-1))
    return pl.pallas_call(
        _combine_kernel,
        out_shape=jax.ShapeDtypeStruct((T, D_MODEL), F32),
        grid=(nt,),
        in_specs=[pl.BlockSpec((tk, D_MODEL), lambda i: (i, 0)),
                  pl.BlockSpec((ROUTE_ROWS, tk), lambda i: (0, i)),
                  pl.BlockSpec((tk, HALF), lambda i: (i, 0)),
                  pl.BlockSpec((tk, HALF), lambda i: (i + nt, 0))],
        out_specs=pl.BlockSpec((tk, D_MODEL), lambda i: (i, 0)),
        compiler_params=pltpu.CompilerParams(dimension_semantics=("arbitrary",),
                                             vmem_limit_bytes=VMEM_LIMIT),
        name="moe_combine",
    )(x1, route, ysel, ysel)


def _routing_tables(route, counts, T):
    e = route[0:TOP_K].astype(jnp.int32)
    rank = route[4:4 + TOP_K].astype(jnp.int32)
    counts = counts.astype(jnp.int32)
    padded = (counts + EXPERT_TILE - 1) // EXPERT_TILE * EXPERT_TILE
    ends = jnp.cumsum(padded)
    starts = ends - padded
    ids = jnp.arange(N_EXPERTS, dtype=jnp.int32)
    pos = rank + jnp.sum(jnp.where(e[:, :, None] == ids, starts, 0), axis=-1)
    n_tiles = TOP_K * T // EXPERT_TILE + N_EXPERTS
    tile0 = jnp.arange(n_tiles, dtype=jnp.int32)
    valid = (tile0 * EXPERT_TILE < ends[-1]).astype(jnp.int32)
    tile = jnp.minimum(tile0, ends[-1] // EXPERT_TILE - 1)
    exp = jnp.minimum(jnp.sum(ends[None, :] <= (tile * EXPERT_TILE)[:, None], axis=1), N_EXPERTS - 1).astype(jnp.int32)
    j = jnp.arange(EXPERT_TILE, dtype=jnp.int32)[None, :]
    n_pad = (padded - counts)[:, None]
    spare = n_tiles * EXPERT_TILE + j % SC_CHUNK
    pad_rows = jnp.where(n_pad > 0, (starts + counts)[:, None] + j % jnp.maximum(n_pad, 1), spare)
    return pos.T, pad_rows, (tile, exp, valid)


def _pick_tile(n, pref):
    t = pref
    while n % t:
        t //= 2
    return t


def kernel(x, positions, norm1_w, w_in, q_norm_w, k_norm_w, attn_sinks, hgrn_lower_bounds, hgrn_norm_w,
           w_branch_attn, w_branch_hgrn, w_out, norm2_w, w_router_group, b_router_group, w_router_expert,
           b_router_expert, w_gate_experts, w_up_experts, w_down_experts):
    B, S, D = x.shape
    T = B * S
    x2 = x.reshape(T, D)
    tm = _pick_tile(T, 512)

    inv_freq = ROPE_THETA ** (-jnp.arange(0, ROT_DIM, 2, dtype=F32) / ROT_DIM)
    ang = inv_freq[:, None] * positions.astype(F32).reshape(1, T)
    cs = jnp.concatenate(_split3(jnp.concatenate([jnp.cos(ang), jnp.sin(ang)], axis=0)), axis=0)

    w_in0 = w_in[0]
    heads = lambda off: [w_in0[:, off + h * HEAD_DIM:off + (h + 1) * HEAD_DIM]
                         for h in range(N_KV_HEADS) for _ in range(2)]
    w_in_a = jnp.concatenate([w_in0[:, :_OFF_K]] + heads(_OFF_K) + heads(_OFF_V) + [w_in0[:, _OFF_HQ:_OFF_Z]],
                             axis=1).astype(BF16)
    w_z = w_in0[:, _OFF_Z:].astype(BF16)

    q, k, v, hq, lf, hv, hg = _inproj(x2, norm1_w[0], w_in_a, cs, q_norm_w[0], k_norm_w[0],
                                      hgrn_lower_bounds.astype(F32), _pick_tile(T, 1024))
    attn = _attention(q, k, v, attn_sinks[0].astype(F32), B, S)
    hgrn = _hgrn(hq, lf, hv, hg, hgrn_norm_w[0], B, S, _pick_tile(S, 1024))

    pad = ROUTER_ROWS - N_GROUPS - N_EXPERTS
    w_r = jnp.concatenate([w_router_group[0].T, w_router_expert[0].T, jnp.zeros((pad, D), F32)], axis=0)
    b_r =jnp.concatenate([b_router_group[0], b_router_expert[0], jnp.zeros((pad,), F32)]).astype(F32)
    b_r = jnp.broadcast_to(b_r[:, None], (ROUTER_ROWS, tm))

    x1, hn, route, counts = _merge(x2, attn, hgrn, norm1_w[0], w_z, w_branch_attn[0].astype(BF16),
                                   w_branch_hgrn[0].astype(BF16), w_out[0].astype(BF16), norm2_w[0],
                                   w_r.astype(BF16), b_r, tm)

    pos, pad_rows, items = _routing_tables(route, counts[N_GROUPS:N_GROUPS + N_EXPERTS, 0], T)
    tk = _pick_tile(T, 512)
    xs = _dispatch(pos, pad_rows, hn, items[0].shape[0] * EXPERT_TILE)
    y = _experts(items, xs, w_gate_experts[0], w_up_experts[0], w_down_experts[0])
    out = _combine(pos, x1, route, y, tk)
    return out.reshape(B, S, D)
```

```python
import functools

import numpy as np
import jax
import jax.numpy as jnp
from jax import lax
from jax.experimental import pallas as pl
from jax.experimental.pallas import tpu as pltpu
from jax.experimental.pallas import tpu_sc as plsc

F32 = jnp.float32
BF16 = jnp.bfloat16

D_MODEL = 1024
N_Q_HEADS = 8
N_KV_HEADS = 2
GROUP = N_Q_HEADS // N_KV_HEADS
HEAD_DIM = 64
ROT_DIM = HEAD_DIM // 4
ROT_HALF = ROT_DIM // 2
ROPE_THETA = 500000.0
WINDOW = 128
ATTN_WIDTH = N_Q_HEADS * HEAD_DIM
KV_WIDTH = N_KV_HEADS * HEAD_DIM

HGRN_HEADS = 4
HGRN_DIM = 128
HGRN_WIDTH = HGRN_HEADS * HGRN_DIM
CHUNK = 64

N_GROUPS = 4
EXPERTS_PER_GROUP = 8
N_EXPERTS = N_GROUPS * EXPERTS_PER_GROUP
TOP_K = 2
EXPERT_FF = 512
EXPERT_TILE = 512
NORM_EPS = 1e-6
ROUTER_ROWS = 40
ROUTE_ROWS = 8

LANES = 128
NEG_BIG = -1e30
LOG2_E = 1.4426950408889634

_OFF_Q, _OFF_K, _OFF_V = 0, ATTN_WIDTH, ATTN_WIDTH + KV_WIDTH
_OFF_HQ = ATTN_WIDTH + 2 * KV_WIDTH
_OFF_HF = _OFF_HQ + HGRN_WIDTH
_OFF_HI = _OFF_HF + HGRN_WIDTH
_OFF_HG = _OFF_HI + HGRN_WIDTH
_OFF_Z = _OFF_HG + HGRN_WIDTH
_A_Q, _A_K, _A_V = 0, ATTN_WIDTH, ATTN_WIDTH + 2 * KV_WIDTH
_A_HQ = ATTN_WIDTH + 4 * KV_WIDTH
_A_HF = _A_HQ + HGRN_WIDTH
_A_HI = _A_HF + HGRN_WIDTH
_A_HG = _A_HI + HGRN_WIDTH

VMEM_LIMIT = 56 * 1024 * 1024


def _split3(a):
    hi = a.astype(BF16)
    r1 = a - hi.astype(F32)
    mid = r1.astype(BF16)
    lo = (r1 - mid.astype(F32)).astype(BF16)
    return hi, mid, lo


def _dot(a, b):
    return jnp.dot(a, b, preferred_element_type=F32)


def _dot_nt(a, b):
    return lax.dot_general(a, b, (((1,), (1,)), ((), ())), preferred_element_type=F32)


def _dot_tn(a, b):
    return lax.dot_general(a, b, (((0,), (0,)), ((), ())), preferred_element_type=F32)


def _sigmoid(x):
    return 1.0 / (1.0 + jnp.exp(-x))


def _rms(x, w):
    ms = jnp.mean(x * x, axis=-1, keepdims=True)
    return x * lax.rsqrt(ms + NORM_EPS) * w


def _inproj_kernel(x_ref, n1w_ref, w_ref, cs_ref, rope_e_ref, rope_c0_ref, qw_ref, kw_ref,
                   mq_ref, mk_ref, lbp_ref,
                   q_out, k_out, v_out, hq_out, lf_out, hv_out, hg_out):
    xn = _rms(x_ref[...], n1w_ref[...]).astype(BF16)

    def proj(off, width):
        return _dot(xn, w_ref[:, off:off + width])

    tabs = _dot_tn(cs_ref[...], rope_e_ref[...])
    c_tab = tabs[:, 0:LANES] + rope_c0_ref[...]
    s1_tab = tabs[:, LANES:2 * LANES]
    s2_tab = tabs[:, 2 * LANES:3 * LANES]

    def norm_rope(t, mavg_ref, w_row, scale):
        ms = _dot((t * t).astype(BF16), mavg_ref[...])
        tn = t * lax.rsqrt(ms + NORM_EPS) * w_row
        if scale != 1.0:
            tn = tn * scale
        outs = []
        for j in range(t.shape[1] // LANES):
            c = tn[:, j * LANES:(j + 1) * LANES]
            outs.append(c * c_tab
                        + pltpu.roll(c, LANES - ROT_HALF, 1) * s1_tab
                        + pltpu.roll(c, ROT_HALF, 1) * s2_tab)
        return outs[0] if len(outs) == 1 else jnp.concatenate(outs, axis=1)

    q_out[...] = norm_rope(proj(_A_Q, ATTN_WIDTH), mq_ref, qw_ref[...], HEAD_DIM ** -0.5).astype(BF16)
    k_out[...] = norm_rope(proj(_A_K, 2 * KV_WIDTH), mk_ref, kw_ref[...], 1.0).astype(BF16)
    v_out[...] = proj(_A_V, 2 * KV_WIDTH).astype(BF16)

    hq = proj(_A_HQ, HGRN_WIDTH)
    hq_out[...] = (hq * _sigmoid(hq)).astype(BF16)
    h0 = lbp_ref[0:1, :]
    h1 = lbp_ref[1:2, :]
    hm = jnp.maximum(h0, h1)
    e0 = jnp.exp(h0 - hm)
    e1 = jnp.exp(h1 - hm)
    lb = e0 / (e0 + e1)
    fg = lb + (1.0 - lb) * _sigmoid(proj(_A_HF, HGRN_WIDTH))
    lf_out[...] = jnp.log(fg) * LOG2_E
    hv_out[...] = proj(_A_HI, HGRN_WIDTH).astype(BF16)
    hg = proj(_A_HG, HGRN_WIDTH)
    hg_out[...] = (hg * _sigmoid(hg)).astype(BF16)


def _rope_constants():
    e = np.zeros((2 * ROT_HALF, 3 * LANES), np.float32)
    c0 = np.zeros((1, LANES), np.float32)
    for lane in range(LANES):
        d = lane % HEAD_DIM
        if d < ROT_HALF:
            e[d, lane] = 1.0
            e[ROT_HALF + d, LANES + lane] = -1.0
        elif d < ROT_DIM:
            e[d - ROT_HALF, lane] = 1.0
            e[ROT_HALF + d - ROT_HALF, 2 * LANES + lane] = 1.0
        else:
            c0[0, lane] = 1.0
    return jnp.asarray(np.concatenate([e, e, e], axis=0), BF16), jnp.asarray(c0, F32)


def _head_mean_matrix(width):
    idx = np.arange(width) // HEAD_DIM
    m = (idx[:, None] == idx[None, :]).astype(np.float32) / HEAD_DIM
    return jnp.asarray(m, BF16)


def _inproj(x2, norm1_w, w_in_a, cs, q_norm_w, k_norm_w, lbp, tm):
    T = x2.shape[0]
    rope_e, rope_c0 = _rope_constants()
    qw = jnp.tile(q_norm_w.astype(F32), N_Q_HEADS)[None, :]
    kw = jnp.tile(k_norm_w.astype(F32), 2 * N_KV_HEADS)[None, :]
    mq = _head_mean_matrix(ATTN_WIDTH)
    mk = _head_mean_matrix(2 * KV_WIDTH)
    row = lambda w: pl.BlockSpec((tm, w), lambda i: (i, 0))
    full = lambda a: pl.BlockSpec(a.shape, lambda i: (0,) * a.ndim)
    ins = [x2, norm1_w[None, :], w_in_a, cs, rope_e, rope_c0, qw, kw, mq, mk, lbp]
    in_specs = ([row(D_MODEL), full(ins[1]), full(w_in_a), pl.BlockSpec((cs.shape[0], tm), lambda i: (0, i))]
                + [full(a) for a in ins[4:]])
    outs = [(ATTN_WIDTH, BF16), (2 * KV_WIDTH, BF16), (2 * KV_WIDTH, BF16), (HGRN_WIDTH, BF16),
            (HGRN_WIDTH, F32), (HGRN_WIDTH, BF16), (HGRN_WIDTH, BF16)]
    return pl.pallas_call(
        _inproj_kernel,
        out_shape=[jax.ShapeDtypeStruct((T, w), dt) for w, dt in outs],
        grid=(T // tm,),
        in_specs=in_specs,
        out_specs=[row(w) for w, _ in outs],
        compiler_params=pltpu.CompilerParams(dimension_semantics=("arbitrary",),
                                             vmem_limit_bytes=VMEM_LIMIT),
        name="inproj",
    )(*ins)


ATTN_QBLOCKS = 8


def _attn_kernel(sink_ref, q_ref, kc_ref, kp_ref, vc_ref, vp_ref, half_ref, o_ref):
    n_qblocks = q_ref.shape[0] // WINDOW
    has_prev = pl.program_id(1) > 0
    qi = lax.broadcasted_iota(jnp.int32, (WINDOW, 2 * WINDOW), 0)
    kj = lax.broadcasted_iota(jnp.int32, (WINDOW, 2 * WINDOW), 1)
    in_window = ((kj < WINDOW) & (kj > qi)) | ((kj >= WINDOW) & (kj - WINDOW <= qi))
    first_valid = in_window & ((kj >= WINDOW) | has_prev)
    left = lax.broadcasted_iota(jnp.int32, (WINDOW, LANES), 1) < HEAD_DIM
    half = (half_ref[0], half_ref[1])

    ks, rhs = [], []
    for h in range(N_KV_HEADS):
        cols = slice(h * LANES, (h + 1) * LANES)
        kall = jnp.concatenate([kp_ref[:, cols], kc_ref[:, cols]], axis=0)
        vall = jnp.concatenate([vp_ref[:, cols], vc_ref[:, cols]], axis=0)
        ks.append([kall * hm for hm in half])
        rhs.append([jnp.concatenate([vall * hm, hm], axis=1) for hm in half])

    units = [(j, h, pr, side) for j in range(n_qblocks) for h in range(N_KV_HEADS)
             for pr in range(GROUP // 2) for side in range(2)]
    scores = []
    for j, h, pr, side in units:
        pair = h * (GROUP // 2) + pr
        qp = q_ref[j * WINDOW:(j + 1) * WINDOW, pair * LANES:(pair + 1) * LANES]
        s = _dot_nt(qp, ks[h][side][j * WINDOW:(j + 2) * WINDOW])
        scores.append(jnp.where(first_valid if j == 0 else in_window, s, NEG_BIG))
    probs, sink_terms = [], []
    for (j, h, pr, side), s in zip(units, scores):
        sink = sink_ref[2 * (h * (GROUP // 2) + pr) + side]
        m = jnp.maximum(jnp.max(s, axis=-1, keepdims=True), sink)
        probs.append(jnp.exp(s - m).astype(BF16))
        sink_terms.append(jnp.exp(sink - m))
    acc = []
    for (j, h, pr, side), p in zip(units, probs):
        acc.append(_dot(p, rhs[h][side][j * WINDOW:(j + 2) * WINDOW]))
    for j in range(n_qblocks):
        outs = []
        for u in range(0, len(units), 2):
            if units[u][0] == j:
                both = acc[u] + acc[u + 1]
                den = both[:, LANES:2 * LANES] + jnp.where(left, sink_terms[u], sink_terms[u + 1])
                outs.append(both[:, 0:LANES] / den)
        o_ref[j * WINDOW:(j + 1) * WINDOW, :] = jnp.concatenate(outs, axis=1).astype(BF16)


def _attention(q, k, v, sinks, B, S):
    qblocks = _pick_tile(S // WINDOW, ATTN_QBLOCKS)
    rows = qblocks * WINDOW
    nb = S // rows
    cur = lambda b, n: (b * nb + n, 0)
    prev = lambda b, n: (jnp.maximum((b * nb + n) * qblocks - 1, 0), 0)
    lane_left = np.arange(LANES) < HEAD_DIM
    half = jnp.asarray(np.broadcast_to(np.stack([lane_left, ~lane_left])[:, None, :],
                                       (2, rows + WINDOW, LANES)), BF16)
    return pl.pallas_call(
        _attn_kernel,
        out_shape=jax.ShapeDtypeStruct((B * S, ATTN_WIDTH), BF16),
        grid=(B, nb),
        in_specs=[pl.BlockSpec(memory_space=pltpu.SMEM),
                  pl.BlockSpec((rows, ATTN_WIDTH), cur),
                  pl.BlockSpec((rows, 2 * KV_WIDTH), cur),
                  pl.BlockSpec((WINDOW, 2 * KV_WIDTH), prev),
                  pl.BlockSpec((rows, 2 * KV_WIDTH), cur),
                  pl.BlockSpec((WINDOW, 2 * KV_WIDTH), prev),
                  pl.BlockSpec(half.shape, lambda b, n: (0, 0, 0))],
        out_specs=pl.BlockSpec((rows, ATTN_WIDTH), cur),
        compiler_params=pltpu.CompilerParams(dimension_semantics=("arbitrary", "arbitrary"),
                                             vmem_limit_bytes=VMEM_LIMIT),
        name="swa_attention",
    )(sinks, q, k, k, v, v, half)


_LEVEL_HALVES = (1, 2, 4, 8, 16, 32)


def _hgrn_level_masks():
    t = np.arange(CHUNK)[:, None]
    s = np.arange(CHUNK)[None, :]
    masks = [((t // (2 * h)) == (s // (2 * h))) & ((t & h) != 0) & ((s & h) == 0) for h in _LEVEL_HALVES]
    return jnp.asarray(np.stack(masks), F32)


def _level_reference(b_ref, slot, half):
    if half >= 4:
        span = max(2 * half, 8)
        pieces = [jnp.broadcast_to(b_ref[slot, s + half - 1:s + half, :], (span, HGRN_DIM))
                  for s in range(0, CHUNK, span)]
    else:
        r8 = lax.broadcasted_iota(jnp.int32, (8, HGRN_DIM), 0)
        pieces = [jnp.where(r8 < 4,
                            jnp.broadcast_to(b_ref[slot, s + 1:s + 2, :], (8, HGRN_DIM)),
                            jnp.broadcast_to(b_ref[slot, s + 5:s + 6, :], (8, HGRN_DIM)))
                  for s in range(0, CHUNK, 8)]
    return pieces[0] if len(pieces) == 1 else jnp.concatenate(pieces, axis=0)


def _hgrn_kernel(hq_ref, lf_ref, hv_ref, hg_ref, nw_ref, tri_ref, lm_ref, o_ref, st_ref, b_ref):
    @pl.when(pl.program_id(1) == 0)
    def _():
        st_ref[...] = jnp.zeros_like(st_ref)

    tri2 = tri_ref[...]
    odd = (lax.broadcasted_iota(jnp.int32, (CHUNK, HGRN_DIM), 0) & 1) != 0
    masks = [lm_ref[li] != 0.0 for li in range(len(_LEVEL_HALVES))]
    units = [(c, h) for c in range(hq_ref.shape[0] // CHUNK) for h in range(HGRN_HEADS)]
    sl = lambda c, h: (slice(c * CHUNK, (c + 1) * CHUNK), slice(h * HGRN_DIM, (h + 1) * HGRN_DIM))

    bs = []
    for u, (c, h) in enumerate(units):
        lf2 = lf_ref[sl(c, h)]
        hi = lf2.astype(BF16)
        lo = (lf2 - hi.astype(F32)).astype(BF16)
        b = _dot(tri2, jnp.concatenate([hi, lo], axis=0))
        b_ref[u] = b
        bs.append(b)

    outs, accs = [], []
    for u, (c, h) in enumerate(units):
        b = bs[u]
        q = hq_ref[sl(c, h)].astype(F32)
        v_bf = hv_ref[sl(c, h)]
        f = jnp.exp2(lf_ref[sl(c, h)])
        k = 1.0 - f
        b_last = b[CHUNK - 1:CHUNK, :]
        st = st_ref[h]
        o = _dot_nt((q * jnp.exp2(b)).astype(BF16), st.astype(BF16))
        k_out = (k * jnp.exp2(b_last - b)).astype(BF16)
        st_ref[h] = st * jnp.exp2(b_last) + _dot_tn(v_bf, k_out)
        outs.append(o + jnp.sum(q * k, axis=-1, keepdims=True) * v_bf.astype(F32))
        acc = jnp.zeros((CHUNK, CHUNK), F32)
        for li, half in enumerate(_LEVEL_HALVES):
            if half == 1:
                e = jnp.where(odd, f, 1.0)
            else:
                e = jnp.exp2(-jnp.abs(b - _level_reference(b_ref, u, half)))
            acc = jnp.where(masks[li], _dot_nt((q * e).astype(BF16), (k * e).astype(BF16)), acc)
        accs.append(acc)

    for u, (c, h) in enumerate(units):
        o = outs[u] + _dot(accs[u].astype(BF16), hv_ref[sl(c, h)])
        y = _rms(o, nw_ref[...]) * hg_ref[sl(c, h)].astype(F32)
        o_ref[sl(c, h)] = y.astype(BF16)


def _hgrn(hq, lf, hv, hg, hgrn_norm_w, B, S, tb):
    nt = S // tb
    blk = pl.BlockSpec((tb, HGRN_WIDTH), lambda b, n: (b * nt + n, 0))
    tri = np.tril(np.ones((CHUNK, CHUNK), np.float32))
    tri = jnp.asarray(np.concatenate([tri, tri], axis=1), BF16)
    lm = _hgrn_level_masks()
    return pl.pallas_call(
        _hgrn_kernel,
        out_shape=jax.ShapeDtypeStruct((B * S, HGRN_WIDTH), BF16),
        grid=(B, nt),
        in_specs=[blk, blk, blk, blk,
                  pl.BlockSpec((1, HGRN_DIM), lambda b, n: (0, 0)),
                  pl.BlockSpec(tri.shape, lambda b, n: (0, 0)),
                  pl.BlockSpec(lm.shape, lambda b, n: (0, 0, 0))],
        out_specs=blk,
        scratch_shapes=[pltpu.VMEM((HGRN_HEADS, HGRN_DIM, HGRN_DIM), F32),
                        pltpu.VMEM((tb // CHUNK * HGRN_HEADS, CHUNK, HGRN_DIM), F32)],
        compiler_params=pltpu.CompilerParams(dimension_semantics=("arbitrary", "arbitrary"),
                                             vmem_limit_bytes=VMEM_LIMIT),
        name="hgrn2",
    )(hq, lf, hv, hg, hgrn_norm_w[None, :].astype(F32), tri, lm)


def _merge_kernel(x_ref, attn_ref, hgrn_ref, n1w_ref, wz_ref, wba_ref, wbh_ref, wout_ref, n2w_ref,
                  wr_ref, br_ref, utri_ref, x1_out, hn_out, route_out, count_out, run_ref):
    @pl.when(pl.program_id(0) == 0)
    def _():
        run_ref[...] = jnp.zeros_like(run_ref)

    x = x_ref[...]
    xn = _rms(x, n1w_ref[...]).astype(BF16)
    za = _sigmoid(_dot(xn, wz_ref[:, 0:D_MODEL]))
    zb = _sigmoid(_dot(xn, wz_ref[:, D_MODEL:2 * D_MODEL]))
    mixed = za * _dot(attn_ref[...], wba_ref[...]) + zb * _dot(hgrn_ref[...], wbh_ref[...])
    x1 = x + _dot(mixed.astype(BF16), wout_ref[...])
    x1_out[...] = x1
    hn = _rms(x1, n2w_ref[...])
    hn_out[...] = _pack_bf16_pairs(hn)

    logits = _dot_nt(wr_ref[...], hn.astype(BF16)) + br_ref[...]
    r = lax.broadcasted_iota(jnp.int32, logits.shape, 0).astype(F32)
    far = float(ROUTER_ROWS)
    cmax = lambda a: jnp.max(a, axis=0, keepdims=True)
    cmin = lambda a: jnp.min(a, axis=0, keepdims=True)
    csum = lambda a: jnp.sum(a, axis=0, keepdims=True)

    lg = jnp.where(r < N_GROUPS, logits, NEG_BIG)
    mg = cmax(lg)
    gsel = cmin(jnp.where(lg == mg, r, far))
    pgsel = 1.0 / csum(jnp.exp(lg - mg))

    lo = N_GROUPS + EXPERTS_PER_GROUP * gsel
    le = jnp.where((r >= lo) & (r < lo + EXPERTS_PER_GROUP), logits, NEG_BIG)
    m1 = cmax(le)
    i1 = cmin(jnp.where(le == m1, r, far))
    se = csum(jnp.exp(le - m1))
    le2 = jnp.where(r == i1, NEG_BIG, le)
    m2 = cmax(le2)
    i2 = cmin(jnp.where(le2 == m2, r, far))
    top0 = 1.0 / se
    top1 = jnp.exp(m2 - m1) / se
    tsum = top0 + top1
    w0 = pgsel * top0 / tsum
    w1 = pgsel * top1 / tsum

    sel1 = r == i1
    sel2 = r == i2
    onehot = jnp.where(sel1 | sel2, 1.0, 0.0)
    before = _dot(onehot.astype(BF16), utri_ref[...]) + run_ref[...]
    r0 = csum(jnp.where(sel1, before, 0.0))
    r1 = csum(jnp.where(sel2, before, 0.0))
    run_new = run_ref[...] + jnp.sum(onehot, axis=1, keepdims=True)
    run_ref[...] = run_new
    count_out[...] = run_new

    row8 = lax.broadcasted_iota(jnp.int32, route_out.shape, 0)
    vals = (i1 - N_GROUPS, i2 - N_GROUPS, w0, w1, r0, r1)
    route = jnp.zeros(route_out.shape, F32)
    for j, val in enumerate(vals):
        route = jnp.where(row8 == j, val, route)
    route_out[...] = route


def _merge(x2, attn, hgrn, norm1_w, w_z, w_ba, w_bh, w_out, norm2_w, w_r, br, tm):
    T = x2.shape[0]
    row = lambda w: pl.BlockSpec((tm, w), lambda i: (i, 0))
    full = lambda a: pl.BlockSpec(a.shape, lambda i: (0,) * a.ndim)
    utri = jnp.asarray(np.triu(np.ones((tm, tm), np.float32), 1), BF16)
    ins = [x2, attn, hgrn, norm1_w[None, :], w_z, w_ba, w_bh, w_out, norm2_w[None, :], w_r, br, utri]
    in_specs = [row(D_MODEL), row(ATTN_WIDTH), row(HGRN_WIDTH)] + [full(a) for a in ins[3:]]
    return pl.pallas_call(
        _merge_kernel,
        out_shape=[jax.ShapeDtypeStruct((T, D_MODEL), F32), jax.ShapeDtypeStruct((T, HALF), jnp.uint32),
                   jax.ShapeDtypeStruct((ROUTE_ROWS, T), F32), jax.ShapeDtypeStruct((ROUTER_ROWS, tm), F32)],
        grid=(T // tm,),
        in_specs=in_specs,
        out_specs=[row(D_MODEL), row(HALF), pl.BlockSpec((ROUTE_ROWS, tm), lambda i: (0, i)),
                   pl.BlockSpec((ROUTER_ROWS, tm), lambda i: (0, 0))],
        scratch_shapes=[pltpu.VMEM((ROUTER_ROWS, tm), F32)],
        compiler_params=pltpu.CompilerParams(dimension_semantics=("arbitrary",),
                                             vmem_limit_bytes=VMEM_LIMIT),
        name="merge_router",
    )(*ins)


HALF = D_MODEL // 2


def _pack_bf16_pairs(x):
    bits = pltpu.bitcast(x.astype(BF16).astype(F32), jnp.uint32)
    return (bits[:, :HALF] >> 16) | bits[:, HALF:]


def _unpack_bf16_pairs(words):
    lo = pltpu.bitcast(words << 16, F32)
    hi = pltpu.bitcast(words & jnp.uint32(0xFFFF0000), F32)
    return lo, hi


SC_CORES = 2
SC_SUBCORES = 16
SC_CHUNK = 64


def _sc_mesh():
    return plsc.VectorSubcoreMesh(core_axis_name="c", subcore_axis_name="s",
                                  num_cores=SC_CORES, num_subcores=SC_SUBCORES)


def _sc_worker():
    return lax.axis_index("s") * SC_CORES + lax.axis_index("c")


def _dispatch(pos, pad_rows, hn, n_rows):
    T, width = hn.shape
    workers = SC_CORES * SC_SUBCORES
    per_worker = T // workers
    n_chunks = per_worker // SC_CHUNK
    assert per_worker * workers == T and n_chunks * SC_CHUNK == per_worker and n_chunks % 2 == 0
    idx = pos.T.reshape(TOP_K, workers, n_chunks, SC_CHUNK).transpose(1, 0, 2, 3)
    pad_chunks = pad_rows.size // (workers * SC_CHUNK)
    pad_idx = pad_rows.reshape(workers, pad_chunks, SC_CHUNK)
    zeros = jnp.zeros((SC_CHUNK, width), hn.dtype)

    @functools.partial(
        pl.kernel, mesh=_sc_mesh(), name="moe_dispatch",
        out_type=jax.ShapeDtypeStruct((n_rows + SC_CHUNK, width), hn.dtype),
        scratch_types=[pltpu.VMEM((TOP_K, n_chunks, SC_CHUNK), jnp.int32),
                       pltpu.VMEM((2, SC_CHUNK, width), hn.dtype),
                       pltpu.VMEM((pad_chunks, SC_CHUNK), jnp.int32),
                       pltpu.VMEM((SC_CHUNK, width), hn.dtype),
                       pltpu.SemaphoreType.DMA((2,)), pltpu.SemaphoreType.DMA((2,)), pltpu.SemaphoreType.DMA])
    def dispatch_kernel(rows_hbm, idx_hbm, pad_hbm, zeros_hbm, out_hbm,
                        idx_v, rows_v, pad_v, zeros_v, load_sem, scatter_sem, pad_sem):
        wid = _sc_worker()
        base = wid * per_worker
        pltpu.sync_copy(idx_hbm.at[wid], idx_v)
        pltpu.sync_copy(pad_hbm.at[wid], pad_v)
        pltpu.sync_copy(zeros_hbm, zeros_v)

        def zero_fill(j):
            return pltpu.make_async_copy(zeros_v, out_hbm.at[pad_v.at[j]], pad_sem)

        for j in range(pad_chunks):
            zero_fill(j).start()

        def load(c, b):
            off = pl.multiple_of(c * SC_CHUNK, SC_CHUNK)
            return pltpu.make_async_copy(rows_hbm.at[pl.ds(base + off, SC_CHUNK)], rows_v.at[b], load_sem.at[b])

        def scatter(c, b, k):
            return pltpu.make_async_copy(rows_v.at[b], out_hbm.at[idx_v.at[k, c]], scatter_sem.at[b])

        load(0, 0).start()

        @pl.loop(0, n_chunks, step=2)
        def _(c0):
            for b in range(2):
                c = c0 + b
                load(c, b).wait()
                for k in range(TOP_K):
                    scatter(c, b, k).start()

                @pl.when(c >= 1)
                def _():
                    for k in range(TOP_K):
                        scatter(c - 1, 1 - b, k).wait()

                @pl.when(c + 1 < n_chunks)
                def _():
                    load(c + 1, 1 - b).start()

        for k in range(TOP_K):
            scatter(n_chunks - 1, (n_chunks - 1) % 2, k).wait()
        for j in range(pad_chunks):
            zero_fill(j).wait()

    return dispatch_kernel(hn, idx, pad_idx, zeros)


def _expert_kernel(tile_ref, exp_ref, valid_ref, x_ref, wg_ref, wu_ref, wd_ref, y_ref, wg_s, wu_s, wd_s):
    w = pl.program_id(0)
    prev = jnp.maximum(w - 1, 0)

    @pl.when((w == 0) | (exp_ref[w] != exp_ref[prev]))
    def _():
        wg_s[...] = wg_ref[0].astype(BF16)
        wu_s[...] = wu_ref[0].astype(BF16)
        wd_s[...] = wd_ref[0].astype(BF16)

    def ffn(words):
        lo, hi = _unpack_bf16_pairs(words)
        lo = lo.astype(BF16)
        hi = hi.astype(BF16)
        gate = _dot(lo, wg_s[0:HALF, :]) + _dot(hi, wg_s[HALF:, :])
        up = _dot(lo, wu_s[0:HALF, :]) + _dot(hi, wu_s[HALF:, :])
        return _pack_bf16_pairs(_dot((gate * _sigmoid(gate) * up).astype(BF16), wd_s[...]))

    @pl.when(valid_ref[w] != 0)
    def _():
        y_ref[...] = ffn(x_ref[...])


def _experts(items, xs, w_gate, w_up, w_down):
    tile, exp, valid = items
    wspec = lambda shape: pl.BlockSpec((1,) + shape, lambda w, t, e, v: (e[w], 0, 0))
    xspec = pl.BlockSpec((EXPERT_TILE, HALF), lambda w, t, e, v: (t[w], 0))
    return pl.pallas_call(
        _expert_kernel,
        out_shape=jax.ShapeDtypeStruct((tile.shape[0] * EXPERT_TILE, HALF), jnp.uint32),
        grid_spec=pltpu.PrefetchScalarGridSpec(
            num_scalar_prefetch=3,
            grid=(tile.shape[0],),
            in_specs=[xspec, wspec((D_MODEL, EXPERT_FF)), wspec((D_MODEL, EXPERT_FF)),
                      wspec((EXPERT_FF, D_MODEL))],
            out_specs=xspec,
            scratch_shapes=[pltpu.VMEM((D_MODEL, EXPERT_FF), BF16),
                            pltpu.VMEM((D_MODEL, EXPERT_FF), BF16),
                            pltpu.VMEM((EXPERT_FF, D_MODEL), BF16)]),
        compiler_params=pltpu.CompilerParams(dimension_semantics=("arbitrary",),
                                             vmem_limit_bytes=VMEM_LIMIT),
        name="moe_experts",
    )(tile, exp, valid, xs, w_gate, w_up, w_down)


def _sc_gather_rows(table, idx):
    n = idx.shape[0]
    width = table.shape[1]
    per_worker = n // (SC_CORES * SC_SUBCORES)
    n_chunks = per_worker // SC_CHUNK
    assert per_worker * SC_CORES * SC_SUBCORES == n and n_chunks * SC_CHUNK == per_worker and n_chunks % 2 == 0

    @functools.partial(
        pl.kernel, mesh=_sc_mesh(), name="moe_row_gather",
        out_type=jax.ShapeDtypeStruct((n, width), table.dtype),
        scratch_types=[pltpu.VMEM((per_worker,), jnp.int32),
                       pltpu.VMEM((2, SC_CHUNK, width), table.dtype),
                       pltpu.SemaphoreType.DMA((2,))])
    def gather_kernel(table_hbm, idx_hbm, out_hbm, idx_v, rows_v, sem):
        base = _sc_worker() * per_worker
        pltpu.sync_copy(idx_hbm.at[pl.ds(base, per_worker)], idx_v)

        def gather(c, b):
            off = pl.multiple_of(c * SC_CHUNK, SC_CHUNK)
            return pltpu.make_async_copy(table_hbm.at[idx_v.at[pl.ds(off, SC_CHUNK)]], rows_v.at[b], sem.at[b])

        gather(0, 0).start()

        @pl.loop(0, n_chunks, step=2)
        def _(c0):
            for b in range(2):
                c = c0 + b

                @pl.when(c + 1 < n_chunks)
                def _():
                    gather(c + 1, 1 - b).start()

                gather(c, b).wait()
                off = pl.multiple_of(c * SC_CHUNK, SC_CHUNK)
                pltpu.sync_copy(rows_v.at[b], out_hbm.at[pl.ds(base + off, SC_CHUNK)])

    return gather_kernel(table, idx)


def _combine_kernel(x1_ref, route_ref, y0_ref, y1_ref, o_ref):
    gates = route_ref[...].T
    w0 = gates[:, 2:3]
    w1 = gates[:, 3:4]
    lo0, hi0 = _unpack_bf16_pairs(y0_ref[...])
    lo1, hi1 = _unpack_bf16_pairs(y1_ref[...])
    o_ref[:, 0:HALF] = x1_ref[:, 0:HALF] + w0 * lo0 + w1 * lo1
    o_ref[:, HALF:] = x1_ref[:, HALF:] + w0 * hi0 + w1 * hi1


def _combine(pos, x1, route, y, tk):
    T = x1.shape[0]
    nt = T // tk
    ysel = _sc_gather_rows(y, pos.T.reshape(-1))
    return pl.pallas_call(
        _combine_kernel,
        out_shape=jax.ShapeDtypeStruct((T, D_MODEL), F32),
        grid=(nt,),
        in_specs=[pl.BlockSpec((tk, D_MODEL), lambda i: (i, 0)),
                  pl.BlockSpec((ROUTE_ROWS, tk), lambda i: (0, i)),
                  pl.BlockSpec((tk, HALF), lambda i: (i, 0)),
                  pl.BlockSpec((tk, HALF), lambda i: (i + nt, 0))],
        out_specs=pl.BlockSpec((tk, D_MODEL), lambda i: (i, 0)),
        compiler_params=pltpu.CompilerParams(dimension_semantics=("arbitrary",),
                                             vmem_limit_bytes=VMEM_LIMIT),
        name="moe_combine",
    )(x1, route, ysel, ysel)


def _routing_tables(route, counts, T):
    e = route[0:TOP_K].astype(jnp.int32)
    rank = route[4:4 + TOP_K].astype(jnp.int32)
    counts = counts.astype(jnp.int32)
    padded = (counts + EXPERT_TILE - 1) // EXPERT_TILE * EXPERT_TILE
    ends = jnp.cumsum(padded)
    starts = ends - padded
    ids = jnp.arange(N_EXPERTS, dtype=jnp.int32)
    pos = rank + jnp.sum(jnp.where(e[:, :, None] == ids, starts, 0), axis=-1)
    n_tiles = TOP_K * T // EXPERT_TILE + N_EXPERTS
    tile0 = jnp.arange(n_tiles, dtype=jnp.int32)
    valid = (tile0 * EXPERT_TILE < ends[-1]).astype(jnp.int32)
    tile = jnp.minimum(tile0, ends[-1] // EXPERT_TILE - 1)
    exp = jnp.minimum(jnp.sum(ends[None, :] <= (tile * EXPERT_TILE)[:, None], axis=1), N_EXPERTS - 1).astype(jnp.int32)
    j = jnp.arange(EXPERT_TILE, dtype=jnp.int32)[None, :]
    n_pad = (padded - counts)[:, None]
    spare = n_tiles * EXPERT_TILE + j % SC_CHUNK
    pad_rows = jnp.where(n_pad > 0, (starts + counts)[:, None] + j % jnp.maximum(n_pad, 1), spare)
    return pos.T, pad_rows, (tile, exp, valid)


def _pick_tile(n, pref):
    t = pref
    while n % t:
        t //= 2
    return t


def kernel(x, positions, norm1_w, w_in, q_norm_w, k_norm_w, attn_sinks, hgrn_lower_bounds, hgrn_norm_w,
           w_branch_attn, w_branch_hgrn, w_out, norm2_w, w_router_group, b_router_group, w_router_expert,
           b_router_expert, w_gate_experts, w_up_experts, w_down_experts):
    B, S, D = x.shape
    T = B * S
    x2 = x.reshape(T, D)
    tm = _pick_tile(T, 512)

    inv_freq = ROPE_THETA ** (-jnp.arange(0, ROT_DIM, 2, dtype=F32) / ROT_DIM)
    ang = inv_freq[:, None] * positions.astype(F32).reshape(1, T)
    cs = jnp.concatenate(_split3(jnp.concatenate([jnp.cos(ang), jnp.sin(ang)], axis=0)), axis=0)

    w_in0 = w_in[0].astype(BF16)
    heads = lambda off: [w_in0[:, off + h * HEAD_DIM:off + (h + 1) * HEAD_DIM]
                         for h in range(N_KV_HEADS) for _ in range(2)]
    w_in_a = jnp.concatenate([w_in0[:, :_OFF_K]] + heads(_OFF_K) + heads(_OFF_V) + [w_in0[:, _OFF_HQ:_OFF_Z]],
                             axis=1)
    w_z = w_in0[:, _OFF_Z:]

    q, k, v, hq, lf, hv, hg = _inproj(x2, norm1_w[0], w_in_a, cs, q_norm_w[0], k_norm_w[0],
                                      hgrn_lower_bounds.astype(F32), _pick_tile(T, 1024))
    attn = _attention(q, k, v, attn_sinks[0].astype(F32), B, S)
    hgrn = _hgrn(hq, lf, hv, hg, hgrn_norm_w[0], B, S, _pick_tile(S, 1024))

    pad = ROUTER_ROWS - N_GROUPS - N_EXPERTS
    w_r = jnp.concatenate([w_router_group[0].T, w_router_expert[0].T, jnp.zeros((pad, D), F32)], axis=0)
    b_r =jnp.concatenate([b_router_group[0], b_router_expert[0], jnp.zeros((pad,), F32)]).astype(F32)
    b_r = jnp.broadcast_to(b_r[:, None], (ROUTER_ROWS, tm))

    x1, hn, route, counts = _merge(x2, attn, hgrn, norm1_w[0], w_z, w_branch_attn[0].astype(BF16),
                                   w_branch_hgrn[0].astype(BF16), w_out[0].astype(BF16), norm2_w[0],
                                   w_r.astype(BF16), b_r, tm)

    pos, pad_rows, items = _routing_tables(route, counts[N_GROUPS:N_GROUPS + N_EXPERTS, 0], T)
    tk = _pick_tile(T, 1024)
    xs = _dispatch(pos, pad_rows, hn, items[0].shape[0] * EXPERT_TILE)
    y = _experts(items, xs, w_gate_experts[0], w_up_experts[0], w_down_experts[0])
    out = _combine(pos, x1, route, y, tk)
    return out.reshape(B, S, D)
```

```python
import functools

import numpy as np
import jax
import jax.numpy as jnp
from jax import lax
from jax.experimental import pallas as pl
from jax.experimental.pallas import tpu as pltpu
from jax.experimental.pallas import tpu_sc as plsc

F32 = jnp.float32
BF16 = jnp.bfloat16

D_MODEL = 1024
N_Q_HEADS = 8
N_KV_HEADS = 2
GROUP = N_Q_HEADS // N_KV_HEADS
HEAD_DIM = 64
ROT_DIM = HEAD_DIM // 4
ROT_HALF = ROT_DIM // 2
ROPE_THETA = 500000.0
WINDOW = 128
ATTN_WIDTH = N_Q_HEADS * HEAD_DIM
KV_WIDTH = N_KV_HEADS * HEAD_DIM

HGRN_HEADS = 4
HGRN_DIM = 128
HGRN_WIDTH = HGRN_HEADS * HGRN_DIM
CHUNK = 64

N_GROUPS = 4
EXPERTS_PER_GROUP = 8
N_EXPERTS = N_GROUPS * EXPERTS_PER_GROUP
TOP_K = 2
EXPERT_FF = 512
EXPERT_TILE = 512
NORM_EPS = 1e-6
ROUTER_ROWS = 40
ROUTE_ROWS = 8

LANES = 128
NEG_BIG = -1e30
LOG2_E = 1.4426950408889634

_OFF_Q, _OFF_K, _OFF_V = 0, ATTN_WIDTH, ATTN_WIDTH + KV_WIDTH
_OFF_HQ = ATTN_WIDTH + 2 * KV_WIDTH
_OFF_HF = _OFF_HQ + HGRN_WIDTH
_OFF_HI = _OFF_HF + HGRN_WIDTH
_OFF_HG = _OFF_HI + HGRN_WIDTH
_OFF_Z = _OFF_HG + HGRN_WIDTH
_A_Q, _A_K, _A_V = 0, ATTN_WIDTH, ATTN_WIDTH + 2 * KV_WIDTH
_A_HQ = ATTN_WIDTH + 4 * KV_WIDTH
_A_HF = _A_HQ + HGRN_WIDTH
_A_HI = _A_HF + HGRN_WIDTH
_A_HG = _A_HI + HGRN_WIDTH

VMEM_LIMIT = 56 * 1024 * 1024


def _split3(a):
    hi = a.astype(BF16)
    r1 = a - hi.astype(F32)
    mid = r1.astype(BF16)
    lo = (r1 - mid.astype(F32)).astype(BF16)
    return hi, mid, lo


def _dot(a, b):
    return jnp.dot(a, b, preferred_element_type=F32)


def _dot_nt(a, b):
    return lax.dot_general(a, b, (((1,), (1,)), ((), ())), preferred_element_type=F32)


def _dot_tn(a, b):
    return lax.dot_general(a, b, (((0,), (0,)), ((), ())), preferred_element_type=F32)


def _sigmoid(x):
    return 1.0 / (1.0 + jnp.exp(-x))


def _rms(x, w):
    ms = jnp.mean(x * x, axis=-1, keepdims=True)
    return x * lax.rsqrt(ms + NORM_EPS) * w


def _inproj_kernel(x_ref, n1w_ref, w_ref, cs_ref, rope_e_ref, rope_c0_ref, qw_ref, kw_ref,
                   mq_ref, mk_ref, lbp_ref,
                   q_out, k_out, v_out, hq_out, lf_out, hv_out, hg_out):
    xn = _rms(x_ref[...], n1w_ref[...]).astype(BF16)

    def proj(off, width):
        return _dot(xn, w_ref[:, off:off + width])

    tabs = _dot_tn(cs_ref[...], rope_e_ref[...])
    c_tab = tabs[:, 0:LANES] + rope_c0_ref[...]
    s1_tab = tabs[:, LANES:2 * LANES]
    s2_tab = tabs[:, 2 * LANES:3 * LANES]

    def norm_rope(t, mavg_ref, w_row, scale):
        ms = _dot((t * t).astype(BF16), mavg_ref[...])
        tn = t * lax.rsqrt(ms + NORM_EPS) * w_row
        if scale != 1.0:
            tn = tn * scale
        outs = []
        for j in range(t.shape[1] // LANES):
            c = tn[:, j * LANES:(j + 1) * LANES]
            outs.append(c * c_tab
                        + pltpu.roll(c, LANES - ROT_HALF, 1) * s1_tab
                        + pltpu.roll(c, ROT_HALF, 1) * s2_tab)
        return outs[0] if len(outs) == 1 else jnp.concatenate(outs, axis=1)

    q_out[...] = norm_rope(proj(_A_Q, ATTN_WIDTH), mq_ref, qw_ref[...], HEAD_DIM ** -0.5).astype(BF16)
    k_out[...] = norm_rope(proj(_A_K, 2 * KV_WIDTH), mk_ref, kw_ref[...], 1.0).astype(BF16)
    v_out[...] = proj(_A_V, 2 * KV_WIDTH).astype(BF16)

    hq = proj(_A_HQ, HGRN_WIDTH)
    hq_out[...] = (hq * _sigmoid(hq)).astype(BF16)
    h0 = lbp_ref[0:1, :]
    h1 = lbp_ref[1:2, :]
    hm = jnp.maximum(h0, h1)
    e0 = jnp.exp(h0 - hm)
    e1 = jnp.exp(h1 - hm)
    lb = e0 / (e0 + e1)
    fg = lb + (1.0 - lb) * _sigmoid(proj(_A_HF, HGRN_WIDTH))
    lf_out[...] = jnp.log(fg) * LOG2_E
    hv_out[...] = proj(_A_HI, HGRN_WIDTH).astype(BF16)
    hg = proj(_A_HG, HGRN_WIDTH)
    hg_out[...] = (hg * _sigmoid(hg)).astype(BF16)


def _rope_constants():
    e = np.zeros((2 * ROT_HALF, 3 * LANES), np.float32)
    c0 = np.zeros((1, LANES), np.float32)
    for lane in range(LANES):
        d = lane % HEAD_DIM
        if d < ROT_HALF:
            e[d, lane] = 1.0
            e[ROT_HALF + d, LANES + lane] = -1.0
        elif d < ROT_DIM:
            e[d - ROT_HALF, lane] = 1.0
            e[ROT_HALF + d - ROT_HALF, 2 * LANES + lane] = 1.0
        else:
            c0[0, lane] = 1.0
    return jnp.asarray(np.concatenate([e, e, e], axis=0), BF16), jnp.asarray(c0, F32)


def _head_mean_matrix(width):
    idx = np.arange(width) // HEAD_DIM
    m = (idx[:, None] == idx[None, :]).astype(np.float32) / HEAD_DIM
    return jnp.asarray(m, BF16)


def _inproj(x2, norm1_w, w_in_a, cs, q_norm_w, k_norm_w, lbp, tm):
    T = x2.shape[0]
    rope_e, rope_c0 = _rope_constants()
    qw = jnp.tile(q_norm_w.astype(F32), N_Q_HEADS)[None, :]
    kw = jnp.tile(k_norm_w.astype(F32), 2 * N_KV_HEADS)[None, :]
    mq = _head_mean_matrix(ATTN_WIDTH)
    mk = _head_mean_matrix(2 * KV_WIDTH)
    row = lambda w: pl.BlockSpec((tm, w), lambda i: (i, 0))
    full = lambda a: pl.BlockSpec(a.shape, lambda i: (0,) * a.ndim)
    ins = [x2, norm1_w[None, :], w_in_a, cs, rope_e, rope_c0, qw, kw, mq, mk, lbp]
    in_specs = ([row(D_MODEL), full(ins[1]), full(w_in_a), pl.BlockSpec((cs.shape[0], tm), lambda i: (0, i))]
                + [full(a) for a in ins[4:]])
    outs = [(ATTN_WIDTH, BF16), (2 * KV_WIDTH, BF16), (2 * KV_WIDTH, BF16), (HGRN_WIDTH, BF16),
            (HGRN_WIDTH, F32), (HGRN_WIDTH, BF16), (HGRN_WIDTH, BF16)]
    return pl.pallas_call(
        _inproj_kernel,
        out_shape=[jax.ShapeDtypeStruct((T, w), dt) for w, dt in outs],
        grid=(T // tm,),
        in_specs=in_specs,
        out_specs=[row(w) for w, _ in outs],
        compiler_params=pltpu.CompilerParams(dimension_semantics=("arbitrary",),
                                             vmem_limit_bytes=VMEM_LIMIT),
        name="inproj",
    )(*ins)


ATTN_QBLOCKS = 8


def _attn_kernel(sink_ref, q_ref, kc_ref, kp_ref, vc_ref, vp_ref, half_ref, o_ref):
    n_qblocks = q_ref.shape[0] // WINDOW
    has_prev = pl.program_id(1) > 0
    qi = lax.broadcasted_iota(jnp.int32, (WINDOW, 2 * WINDOW), 0)
    kj = lax.broadcasted_iota(jnp.int32, (WINDOW, 2 * WINDOW), 1)
    in_window = ((kj < WINDOW) & (kj > qi)) | ((kj >= WINDOW) & (kj - WINDOW <= qi))
    first_valid = in_window & ((kj >= WINDOW) | has_prev)
    left = lax.broadcasted_iota(jnp.int32, (WINDOW, LANES), 1) < HEAD_DIM
    half = (half_ref[0], half_ref[1])

    ks, rhs = [], []
    for h in range(N_KV_HEADS):
        cols = slice(h * LANES, (h + 1) * LANES)
        kall = jnp.concatenate([kp_ref[:, cols], kc_ref[:, cols]], axis=0)
        vall = jnp.concatenate([vp_ref[:, cols], vc_ref[:, cols]], axis=0)
        ks.append([kall * hm for hm in half])
        rhs.append([jnp.concatenate([vall * hm, hm], axis=1) for hm in half])

    units = [(j, h, pr, side) for j in range(n_qblocks) for h in range(N_KV_HEADS)
             for pr in range(GROUP // 2) for side in range(2)]
    scores = []
    for j, h, pr, side in units:
        pair = h * (GROUP // 2) + pr
        qp = q_ref[j * WINDOW:(j + 1) * WINDOW, pair * LANES:(pair + 1) * LANES]
        s = _dot_nt(qp, ks[h][side][j * WINDOW:(j + 2) * WINDOW])
        scores.append(jnp.where(first_valid if j == 0 else in_window, s, NEG_BIG))
    probs, sink_terms = [], []
    for (j, h, pr, side), s in zip(units, scores):
        sink = sink_ref[2 * (h * (GROUP // 2) + pr) + side]
        m = jnp.maximum(jnp.max(s, axis=-1, keepdims=True), sink)
        probs.append(jnp.exp(s - m).astype(BF16))
        sink_terms.append(jnp.exp(sink - m))
    acc = []
    for (j, h, pr, side), p in zip(units, probs):
        acc.append(_dot(p, rhs[h][side][j * WINDOW:(j + 2) * WINDOW]))
    for j in range(n_qblocks):
        outs = []
        for u in range(0, len(units), 2):
            if units[u][0] == j:
                both = acc[u] + acc[u + 1]
                den = both[:, LANES:2 * LANES] + jnp.where(left, sink_terms[u], sink_terms[u + 1])
                outs.append(both[:, 0:LANES] / den)
        o_ref[j * WINDOW:(j + 1) * WINDOW, :] = jnp.concatenate(outs, axis=1).astype(BF16)


def _attention(q, k, v, sinks, B, S):
    qblocks = _pick_tile(S // WINDOW, ATTN_QBLOCKS)
    rows = qblocks * WINDOW
    nb = S // rows
    cur = lambda b, n: (b * nb + n, 0)
    prev = lambda b, n: (jnp.maximum((b * nb + n) * qblocks - 1, 0), 0)
    lane_left = np.arange(LANES) < HEAD_DIM
    half = jnp.asarray(np.broadcast_to(np.stack([lane_left, ~lane_left])[:, None, :],
                                       (2, rows + WINDOW, LANES)), BF16)
    return pl.pallas_call(
        _attn_kernel,
        out_shape=jax.ShapeDtypeStruct((B * S, ATTN_WIDTH), BF16),
        grid=(B, nb),
        in_specs=[pl.BlockSpec(memory_space=pltpu.SMEM),
                  pl.BlockSpec((rows, ATTN_WIDTH), cur),
                  pl.BlockSpec((rows, 2 * KV_WIDTH), cur),
                  pl.BlockSpec((WINDOW, 2 * KV_WIDTH), prev),
                  pl.BlockSpec((rows, 2 * KV_WIDTH), cur),
                  pl.BlockSpec((WINDOW, 2 * KV_WIDTH), prev),
                  pl.BlockSpec(half.shape, lambda b, n: (0, 0, 0))],
        out_specs=pl.BlockSpec((rows, ATTN_WIDTH), cur),
        compiler_params=pltpu.CompilerParams(dimension_semantics=("arbitrary", "arbitrary"),
                                             vmem_limit_bytes=VMEM_LIMIT),
        name="swa_attention",
    )(sinks, q, k, k, v, v, half)


_LEVEL_HALVES = (1, 2, 4, 8, 16, 32)


def _hgrn_level_masks():
    t = np.arange(CHUNK)[:, None]
    s = np.arange(CHUNK)[None, :]
    masks = [((t // (2 * h)) == (s // (2 * h))) & ((t & h) != 0) & ((s & h) == 0) for h in _LEVEL_HALVES]
    level = -np.ones((CHUNK, CHUNK), np.float32)
    for li, m in enumerate(masks):
        level[m] = li
    return jnp.asarray(level[None], F32)


def _level_reference(b, half):
    if half >= 4:
        span = max(2 * half, 8)
        pieces = [jnp.broadcast_to(b[s + half - 1:s + half, :], (span, HGRN_DIM))
                  for s in range(0, CHUNK, span)]
    else:
        r8 = lax.broadcasted_iota(jnp.int32, (8, HGRN_DIM), 0)
        pieces = [jnp.where(r8 < 4,
                            jnp.broadcast_to(b[s + 1:s + 2, :], (8, HGRN_DIM)),
                            jnp.broadcast_to(b[s + 5:s + 6, :], (8, HGRN_DIM)))
                  for s in range(0, CHUNK, 8)]
    return pieces[0] if len(pieces) == 1 else jnp.concatenate(pieces, axis=0)


def _hgrn_kernel(hq_ref, lf_ref, hv_ref, hg_ref, nw_ref, tri_ref, lm_ref, o_ref, st_ref):
    @pl.when(pl.program_id(1) == 0)
    def _():
        st_ref[...] = jnp.zeros_like(st_ref)

    tri2 = tri_ref[...]
    odd = (lax.broadcasted_iota(jnp.int32, (CHUNK, HGRN_DIM), 0) & 1) != 0
    level_of = lm_ref[0]
    units = [(c, h) for c in range(hq_ref.shape[0] // CHUNK) for h in range(HGRN_HEADS)]
    sl = lambda c, h: (slice(c * CHUNK, (c + 1) * CHUNK), slice(h * HGRN_DIM, (h + 1) * HGRN_DIM))

    bs = []
    for u, (c, h) in enumerate(units):
        lf2 = lf_ref[sl(c, h)]
        hi = lf2.astype(BF16)
        lo = (lf2 - hi.astype(F32)).astype(BF16)
        bs.append(_dot(tri2, jnp.concatenate([hi, lo], axis=0)))

    outs, accs = [], []
    for u, (c, h) in enumerate(units):
        b = bs[u]
        q = hq_ref[sl(c, h)].astype(F32)
        v_bf = hv_ref[sl(c, h)]
        f = jnp.exp2(lf_ref[sl(c, h)])
        k = 1.0 - f
        b_last = b[CHUNK - 1:CHUNK, :]
        st = st_ref[h]
        o = _dot_nt((q * jnp.exp2(b)).astype(BF16), st.astype(BF16))
        k_out = (k * jnp.exp2(b_last - b)).astype(BF16)
        st_ref[h] = st * jnp.exp2(b_last) + _dot_tn(v_bf, k_out)
        outs.append(o + jnp.sum(q * k, axis=-1, keepdims=True) * v_bf.astype(F32))
        acc = jnp.zeros((CHUNK, CHUNK), F32)
        for li, half in enumerate(_LEVEL_HALVES):
            if half == 1:
                e = jnp.where(odd, f, 1.0)
            else:
                e = jnp.exp2(-jnp.abs(b - _level_reference(b, half)))
            acc = jnp.where(level_of == float(li), _dot_nt((q * e).astype(BF16), (k * e).astype(BF16)), acc)
        accs.append(acc)

    for u, (c, h) in enumerate(units):
        o = outs[u] + _dot(accs[u].astype(BF16), hv_ref[sl(c, h)])
        y = _rms(o, nw_ref[...]) * hg_ref[sl(c, h)].astype(F32)
        o_ref[sl(c, h)] = y.astype(BF16)


def _hgrn(hq, lf, hv, hg, hgrn_norm_w, B, S, tb):
    nt = S // tb
    blk = pl.BlockSpec((tb, HGRN_WIDTH), lambda b, n: (b * nt + n, 0))
    tri = np.tril(np.ones((CHUNK, CHUNK), np.float32))
    tri = jnp.asarray(np.concatenate([tri, tri], axis=1), BF16)
    lm = _hgrn_level_masks()
    return pl.pallas_call(
        _hgrn_kernel,
        out_shape=jax.ShapeDtypeStruct((B * S, HGRN_WIDTH), BF16),
        grid=(B, nt),
        in_specs=[blk, blk, blk, blk,
                  pl.BlockSpec((1, HGRN_DIM), lambda b, n: (0, 0)),
                  pl.BlockSpec(tri.shape, lambda b, n: (0, 0)),
                  pl.BlockSpec(lm.shape, lambda b, n: (0, 0, 0))],
        out_specs=blk,
        scratch_shapes=[pltpu.VMEM((HGRN_HEADS, HGRN_DIM, HGRN_DIM), F32)],
        compiler_params=pltpu.CompilerParams(dimension_semantics=("arbitrary", "arbitrary"),
                                             vmem_limit_bytes=VMEM_LIMIT),
        name="hgrn2",
    )(hq, lf, hv, hg, hgrn_norm_w[None, :].astype(F32), tri, lm)


def _merge_kernel(x_ref, attn_ref, hgrn_ref, n1w_ref, wz_ref, wba_ref, wbh_ref, wout_ref, n2w_ref,
                  wr_ref, br_ref, utri_ref, x1_out, hn_out, route_out, count_out, run_ref):
    @pl.when(pl.program_id(0) == 0)
    def _():
        run_ref[...] = jnp.zeros_like(run_ref)

    x = x_ref[...]
    xn = _rms(x, n1w_ref[...]).astype(BF16)
    za = _sigmoid(_dot(xn, wz_ref[:, 0:D_MODEL]))
    zb = _sigmoid(_dot(xn, wz_ref[:, D_MODEL:2 * D_MODEL]))
    mixed = za * _dot(attn_ref[...], wba_ref[...]) + zb * _dot(hgrn_ref[...], wbh_ref[...])
    x1 = x + _dot(mixed.astype(BF16), wout_ref[...])
    x1_out[...] = x1
    hn = _rms(x1, n2w_ref[...])
    hn_out[...] = _pack_bf16_pairs(hn)

    logits = _dot_nt(wr_ref[...], hn.astype(BF16)) + br_ref[...]
    r = lax.broadcasted_iota(jnp.int32, logits.shape, 0).astype(F32)
    far = float(ROUTER_ROWS)
    cmax = lambda a: jnp.max(a, axis=0, keepdims=True)
    cmin = lambda a: jnp.min(a, axis=0, keepdims=True)
    csum = lambda a: jnp.sum(a, axis=0, keepdims=True)

    lg = jnp.where(r < N_GROUPS, logits, NEG_BIG)
    mg = cmax(lg)
    gsel = cmin(jnp.where(lg == mg, r, far))
    pgsel = 1.0 / csum(jnp.exp(lg - mg))

    lo = N_GROUPS + EXPERTS_PER_GROUP * gsel
    le = jnp.where((r >= lo) & (r < lo + EXPERTS_PER_GROUP), logits, NEG_BIG)
    m1 = cmax(le)
    i1 = cmin(jnp.where(le == m1, r, far))
    se = csum(jnp.exp(le - m1))
    le2 = jnp.where(r == i1, NEG_BIG, le)
    m2 = cmax(le2)
    i2 = cmin(jnp.where(le2 == m2, r, far))
    top0 = 1.0 / se
    top1 = jnp.exp(m2 - m1) / se
    tsum = top0 + top1
    w0 = pgsel * top0 / tsum
    w1 = pgsel * top1 / tsum

    sel1 = r == i1
    sel2 = r == i2
    onehot = jnp.where(sel1 | sel2, 1.0, 0.0)
    before = _dot(onehot.astype(BF16), utri_ref[...]) + run_ref[...]
    r0 = csum(jnp.where(sel1, before, 0.0))
    r1 = csum(jnp.where(sel2, before, 0.0))
    run_new = run_ref[...] + jnp.sum(onehot, axis=1, keepdims=True)
    run_ref[...] = run_new
    count_out[...] = run_new

    row8 = lax.broadcasted_iota(jnp.int32, route_out.shape, 0)
    vals = (i1 - N_GROUPS, i2 - N_GROUPS, w0, w1, r0, r1)
    route = jnp.zeros(route_out.shape, F32)
    for j, val in enumerate(vals):
        route = jnp.where(row8 == j, val, route)
    route_out[...] = route


def _merge(x2, attn, hgrn, norm1_w, w_z, w_ba, w_bh, w_out, norm2_w, w_r, br, tm):
    T = x2.shape[0]
    row = lambda w: pl.BlockSpec((tm, w), lambda i: (i, 0))
    full = lambda a: pl.BlockSpec(a.shape, lambda i: (0,) * a.ndim)
    utri = jnp.asarray(np.triu(np.ones((tm, tm), np.float32), 1), BF16)
    ins = [x2, attn, hgrn, norm1_w[None, :], w_z, w_ba, w_bh, w_out, norm2_w[None, :], w_r, br, utri]
    in_specs = [row(D_MODEL), row(ATTN_WIDTH), row(HGRN_WIDTH)] + [full(a) for a in ins[3:]]
    return pl.pallas_call(
        _merge_kernel,
        out_shape=[jax.ShapeDtypeStruct((T, D_MODEL), F32), jax.ShapeDtypeStruct((T, HALF), jnp.uint32),
                   jax.ShapeDtypeStruct((ROUTE_ROWS, T), F32), jax.ShapeDtypeStruct((ROUTER_ROWS, tm), F32)],
        grid=(T // tm,),
        in_specs=in_specs,
        out_specs=[row(D_MODEL), row(HALF), pl.BlockSpec((ROUTE_ROWS, tm), lambda i: (0, i)),
                   pl.BlockSpec((ROUTER_ROWS, tm), lambda i: (0, 0))],
        scratch_shapes=[pltpu.VMEM((ROUTER_ROWS, tm), F32)],
        compiler_params=pltpu.CompilerParams(dimension_semantics=("arbitrary",),
                                             vmem_limit_bytes=VMEM_LIMIT),
        name="merge_router",
    )(*ins)


HALF = D_MODEL // 2


def _pack_bf16_pairs(x):
    bits = pltpu.bitcast(x.astype(BF16).astype(F32), jnp.uint32)
    return (bits[:, :HALF] >> 16) | bits[:, HALF:]


def _unpack_bf16_pairs(words):
    lo = pltpu.bitcast(words << 16, F32)
    hi = pltpu.bitcast(words & jnp.uint32(0xFFFF0000), F32)
    return lo, hi


SC_CORES = 2
SC_SUBCORES = 16
SC_CHUNK = 64


def _sc_mesh():
    return plsc.VectorSubcoreMesh(core_axis_name="c", subcore_axis_name="s",
                                  num_cores=SC_CORES, num_subcores=SC_SUBCORES)


def _sc_worker():
    return lax.axis_index("s") * SC_CORES + lax.axis_index("c")


def _dispatch(pos, pad_rows, hn, n_rows):
    T, width = hn.shape
    workers = SC_CORES * SC_SUBCORES
    per_worker = T // workers
    n_chunks = per_worker // SC_CHUNK
    assert per_worker * workers == T and n_chunks * SC_CHUNK == per_worker and n_chunks % 2 == 0
    idx = pos.T.reshape(TOP_K, workers, n_chunks, SC_CHUNK).transpose(1, 0, 2, 3)
    pad_chunks = pad_rows.size // (workers * SC_CHUNK)
    pad_idx = pad_rows.reshape(workers, pad_chunks, SC_CHUNK)
    zeros = jnp.zeros((SC_CHUNK, width), hn.dtype)

    @functools.partial(
        pl.kernel, mesh=_sc_mesh(), name="moe_dispatch",
        out_type=jax.ShapeDtypeStruct((n_rows + SC_CHUNK, width), hn.dtype),
        scratch_types=[pltpu.VMEM((TOP_K, n_chunks, SC_CHUNK), jnp.int32),
                       pltpu.VMEM((2, SC_CHUNK, width), hn.dtype),
                       pltpu.VMEM((pad_chunks, SC_CHUNK), jnp.int32),
                       pltpu.VMEM((SC_CHUNK, width), hn.dtype),
                       pltpu.SemaphoreType.DMA((2,)), pltpu.SemaphoreType.DMA((2,)), pltpu.SemaphoreType.DMA])
    def dispatch_kernel(rows_hbm, idx_hbm, pad_hbm, zeros_hbm, out_hbm,
                        idx_v, rows_v, pad_v, zeros_v, load_sem, scatter_sem, pad_sem):
        wid = _sc_worker()
        base = wid * per_worker
        pltpu.sync_copy(idx_hbm.at[wid], idx_v)
        pltpu.sync_copy(pad_hbm.at[wid], pad_v)
        pltpu.sync_copy(zeros_hbm, zeros_v)

        def zero_fill(j):
            return pltpu.make_async_copy(zeros_v, out_hbm.at[pad_v.at[j]], pad_sem)

        for j in range(pad_chunks):
            zero_fill(j).start()

        def load(c, b):
            off = pl.multiple_of(c * SC_CHUNK, SC_CHUNK)
            return pltpu.make_async_copy(rows_hbm.at[pl.ds(base + off, SC_CHUNK)], rows_v.at[b], load_sem.at[b])

        def scatter(c, b, k):
            return pltpu.make_async_copy(rows_v.at[b], out_hbm.at[idx_v.at[k, c]], scatter_sem.at[b])

        load(0, 0).start()

        @pl.loop(0, n_chunks, step=2)
        def _(c0):
            for b in range(2):
                c = c0 + b
                load(c, b).wait()
                for k in range(TOP_K):
                    scatter(c, b, k).start()

                @pl.when(c >= 1)
                def _():
                    for k in range(TOP_K):
                        scatter(c - 1, 1 - b, k).wait()

                @pl.when(c + 1 < n_chunks)
                def _():
                    load(c + 1, 1 - b).start()

        for k in range(TOP_K):
            scatter(n_chunks - 1, (n_chunks - 1) % 2, k).wait()
        for j in range(pad_chunks):
            zero_fill(j).wait()

    return dispatch_kernel(hn, idx, pad_idx, zeros)


def _expert_kernel(tile_ref, exp_ref, valid_ref, x_ref, wg_ref, wu_ref, wd_ref, y_ref, wg_s, wu_s, wd_s):
    w = pl.program_id(0)
    prev = jnp.maximum(w - 1, 0)

    @pl.when((w == 0) | (exp_ref[w] != exp_ref[prev]))
    def _():
        wg_s[...] = wg_ref[0].astype(BF16)
        wu_s[...] = wu_ref[0].astype(BF16)
        wd_s[...] = wd_ref[0].astype(BF16)

    def ffn(words):
        lo, hi = _unpack_bf16_pairs(words)
        lo = lo.astype(BF16)
        hi = hi.astype(BF16)
        gate = _dot(lo, wg_s[0:HALF, :]) + _dot(hi, wg_s[HALF:, :])
        up = _dot(lo, wu_s[0:HALF, :]) + _dot(hi, wu_s[HALF:, :])
        return _pack_bf16_pairs(_dot((gate * _sigmoid(gate) * up).astype(BF16), wd_s[...]))

    @pl.when(valid_ref[w] != 0)
    def _():
        y_ref[...] = ffn(x_ref[...])


def _experts(items, xs, w_gate, w_up, w_down):
    tile, exp, valid = items
    wspec = lambda shape: pl.BlockSpec((1,) + shape, lambda w, t, e, v: (e[w], 0, 0))
    xspec = pl.BlockSpec((EXPERT_TILE, HALF), lambda w, t, e, v: (t[w], 0))
    return pl.pallas_call(
        _expert_kernel,
        out_shape=jax.ShapeDtypeStruct((tile.shape[0] * EXPERT_TILE, HALF), jnp.uint32),
        grid_spec=pltpu.PrefetchScalarGridSpec(
            num_scalar_prefetch=3,
            grid=(tile.shape[0],),
            in_specs=[xspec, wspec((D_MODEL, EXPERT_FF)), wspec((D_MODEL, EXPERT_FF)),
                      wspec((EXPERT_FF, D_MODEL))],
            out_specs=xspec,
            scratch_shapes=[pltpu.VMEM((D_MODEL, EXPERT_FF), BF16),
                            pltpu.VMEM((D_MODEL, EXPERT_FF), BF16),
                            pltpu.VMEM((EXPERT_FF, D_MODEL), BF16)]),
        compiler_params=pltpu.CompilerParams(dimension_semantics=("arbitrary",),
                                             vmem_limit_bytes=VMEM_LIMIT),
        name="moe_experts",
    )(tile, exp, valid, xs, w_gate, w_up, w_down)


def _sc_gather_rows(table, idx):
    n = idx.shape[0]
    width = table.shape[1]
    per_worker = n // (SC_CORES * SC_SUBCORES)
    n_chunks = per_worker // SC_CHUNK
    assert per_worker * SC_CORES * SC_SUBCORES == n and n_chunks * SC_CHUNK == per_worker and n_chunks % 2 == 0

    @functools.partial(
        pl.kernel, mesh=_sc_mesh(), name="moe_row_gather",
        out_type=jax.ShapeDtypeStruct((n, width), table.dtype),
        scratch_types=[pltpu.VMEM((per_worker,), jnp.int32),
                       pltpu.VMEM((2, SC_CHUNK, width), table.dtype),
                       pltpu.SemaphoreType.DMA((2,))])
    def gather_kernel(table_hbm, idx_hbm, out_hbm, idx_v, rows_v, sem):
        base = _sc_worker() * per_worker
        pltpu.sync_copy(idx_hbm.at[pl.ds(base, per_worker)], idx_v)

        def gather(c, b):
            off = pl.multiple_of(c * SC_CHUNK, SC_CHUNK)
            return pltpu.make_async_copy(table_hbm.at[idx_v.at[pl.ds(off, SC_CHUNK)]], rows_v.at[b], sem.at[b])

        gather(0, 0).start()

        @pl.loop(0, n_chunks, step=2)
        def _(c0):
            for b in range(2):
                c = c0 + b

                @pl.when(c + 1 < n_chunks)
                def _():
                    gather(c + 1, 1 - b).start()

                gather(c, b).wait()
                off = pl.multiple_of(c * SC_CHUNK, SC_CHUNK)
                pltpu.sync_copy(rows_v.at[b], out_hbm.at[pl.ds(base + off, SC_CHUNK)])

    return gather_kernel(table, idx)


def _combine_kernel(x1_ref, route_ref, y0_ref, y1_ref, o_ref):
    gates = route_ref[...].T
    w0 = gates[:, 2:3]
    w1 = gates[:, 3:4]
    lo0, hi0 = _unpack_bf16_pairs(y0_ref[...])
    lo1, hi1 = _unpack_bf16_pairs(y1_ref[...])
    o_ref[:, 0:HALF] = x1_ref[:, 0:HALF] + w0 * lo0 + w1 * lo1
    o_ref[:, HALF:] = x1_ref[:, HALF:] + w0 * hi0 + w1 * hi1


def _combine(pos, x1, route, y, tk):
    T = x1.shape[0]
    nt = T // tk
    ysel = _sc_gather_rows(y, pos.T.reshape(-1))
    return pl.pallas_call(
        _combine_kernel,
        out_shape=jax.ShapeDtypeStruct((T, D_MODEL), F32),
        grid=(nt,),
        in_specs=[pl.BlockSpec((tk, D_MODEL), lambda i: (i, 0)),
                  pl.BlockSpec((ROUTE_ROWS, tk), lambda i: (0, i)),
                  pl.BlockSpec((tk, HALF), lambda i: (i, 0)),
                  pl.BlockSpec((tk, HALF), lambda i: (i + nt, 0))],
        out_specs=pl.BlockSpec((tk, D_MODEL), lambda i: (i, 0)),
        compiler_params=pltpu.CompilerParams(dimension_semantics=("arbitrary",),
                                             vmem_limit_bytes=VMEM_LIMIT),
        name="moe_combine",
    )(x1, route, ysel, ysel)


def _routing_tables(route, counts, T):
    e = route[0:TOP_K].astype(jnp.int32)
    rank = route[4:4 + TOP_K].astype(jnp.int32)
    counts = counts.astype(jnp.int32)
    padded = (counts + EXPERT_TILE - 1) // EXPERT_TILE * EXPERT_TILE
    ends = jnp.cumsum(padded)
    starts = ends - padded
    ids = jnp.arange(N_EXPERTS, dtype=jnp.int32)
    pos = rank + jnp.sum(jnp.where(e[:, :, None] == ids, starts, 0), axis=-1)
    n_tiles = TOP_K * T // EXPERT_TILE + N_EXPERTS
    tile0 = jnp.arange(n_tiles, dtype=jnp.int32)
    valid = (tile0 * EXPERT_TILE < ends[-1]).astype(jnp.int32)
    tile = jnp.minimum(tile0, ends[-1] // EXPERT_TILE - 1)
    exp = jnp.minimum(jnp.sum(ends[None, :] <= (tile * EXPERT_TILE)[:, None], axis=1), N_EXPERTS - 1).astype(jnp.int32)
    j = jnp.arange(EXPERT_TILE, dtype=jnp.int32)[None, :]
    n_pad = (padded - counts)[:, None]
    spare = n_tiles * EXPERT_TILE + j % SC_CHUNK
    pad_rows = jnp.where(n_pad > 0, (starts + counts)[:, None] + j % jnp.maximum(n_pad, 1), spare)
    return pos.T, pad_rows, (tile, exp, valid)


def _pick_tile(n, pref):
    t = pref
    while n % t:
        t //= 2
    return t


def kernel(x, positions, norm1_w, w_in, q_norm_w, k_norm_w, attn_sinks, hgrn_lower_bounds, hgrn_norm_w,
           w_branch_attn, w_branch_hgrn, w_out, norm2_w, w_router_group, b_router_group, w_router_expert,
           b_router_expert, w_gate_experts, w_up_experts, w_down_experts):
    B, S, D = x.shape
    T = B * S
    x2 = x.reshape(T, D)
    tm = _pick_tile(T, 512)

    inv_freq = ROPE_THETA ** (-jnp.arange(0, ROT_DIM, 2, dtype=F32) / ROT_DIM)
    ang = inv_freq[:, None] * positions.astype(F32).reshape(1, T)
    cs = jnp.concatenate(_split3(jnp.concatenate([jnp.cos(ang), jnp.sin(ang)], axis=0)), axis=0)

    w_in0 = w_in[0].astype(BF16)
    heads = lambda off: [w_in0[:, off + h * HEAD_DIM:off + (h + 1) * HEAD_DIM]
                         for h in range(N_KV_HEADS) for _ in range(2)]
    w_in_a = jnp.concatenate([w_in0[:, :_OFF_K]] + heads(_OFF_K) + heads(_OFF_V) + [w_in0[:, _OFF_HQ:_OFF_Z]],
                             axis=1)
    w_z = w_in0[:, _OFF_Z:]

    q, k, v, hq, lf, hv, hg = _inproj(x2, norm1_w[0], w_in_a, cs, q_norm_w[0], k_norm_w[0],
                                      hgrn_lower_bounds.astype(F32), _pick_tile(T, 1024))
    attn = _attention(q, k, v, attn_sinks[0].astype(F32), B, S)
    hgrn = _hgrn(hq, lf, hv, hg, hgrn_norm_w[0], B, S, _pick_tile(S, 1024))

    pad = ROUTER_ROWS - N_GROUPS - N_EXPERTS
    w_r = jnp.concatenate([w_router_group[0].T, w_router_expert[0].T, jnp.zeros((pad, D), F32)], axis=0)
    b_r =jnp.concatenate([b_router_group[0], b_router_expert[0], jnp.zeros((pad,), F32)]).astype(F32)
    b_r = jnp.broadcast_to(b_r[:, None], (ROUTER_ROWS, tm))

    x1, hn, route, counts = _merge(x2, attn, hgrn, norm1_w[0], w_z, w_branch_attn[0].astype(BF16),
                                   w_branch_hgrn[0].astype(BF16), w_out[0].astype(BF16), norm2_w[0],
                                   w_r.astype(BF16), b_r, tm)

    pos, pad_rows, items = _routing_tables(route, counts[N_GROUPS:N_GROUPS + N_EXPERTS, 0], T)
    tk = _pick_tile(T, 1024)
    xs = _dispatch(pos, pad_rows, hn, items[0].shape[0] * EXPERT_TILE)
    y = _experts(items, xs, w_gate_experts[0], w_up_experts[0], w_down_experts[0])
    out = _combine(pos, x1, route, y, tk)
    return out.reshape(B, S, D)
```

```python
import functools

import numpy as np
import jax
import jax.numpy as jnp
from jax import lax
from jax.experimental import pallas as pl
from jax.experimental.pallas import tpu as pltpu
from jax.experimental.pallas import tpu_sc as plsc

F32 = jnp.float32
BF16 = jnp.bfloat16

D_MODEL = 1024
N_Q_HEADS = 8
N_KV_HEADS = 2
GROUP = N_Q_HEADS // N_KV_HEADS
HEAD_DIM = 64
ROT_DIM = HEAD_DIM // 4
ROT_HALF = ROT_DIM // 2
ROPE_THETA = 500000.0
WINDOW = 128
ATTN_WIDTH = N_Q_HEADS * HEAD_DIM
KV_WIDTH = N_KV_HEADS * HEAD_DIM

HGRN_HEADS = 4
HGRN_DIM = 128
HGRN_WIDTH = HGRN_HEADS * HGRN_DIM
CHUNK = 64

N_GROUPS = 4
EXPERTS_PER_GROUP = 8
N_EXPERTS = N_GROUPS * EXPERTS_PER_GROUP
TOP_K = 2
EXPERT_FF = 512
EXPERT_TILE = 512
NORM_EPS = 1e-6
ROUTER_ROWS = 40
ROUTE_ROWS = 8

LANES = 128
NEG_BIG = -1e30
LOG2_E = 1.4426950408889634

_OFF_Q, _OFF_K, _OFF_V = 0, ATTN_WIDTH, ATTN_WIDTH + KV_WIDTH
_OFF_HQ = ATTN_WIDTH + 2 * KV_WIDTH
_OFF_HF = _OFF_HQ + HGRN_WIDTH
_OFF_HI = _OFF_HF + HGRN_WIDTH
_OFF_HG = _OFF_HI + HGRN_WIDTH
_OFF_Z = _OFF_HG + HGRN_WIDTH
_A_Q, _A_K, _A_V = 0, ATTN_WIDTH, ATTN_WIDTH + 2 * KV_WIDTH
_A_HQ = ATTN_WIDTH + 4 * KV_WIDTH
_A_HF = _A_HQ + HGRN_WIDTH
_A_HI = _A_HF + HGRN_WIDTH
_A_HG = _A_HI + HGRN_WIDTH

VMEM_LIMIT = 56 * 1024 * 1024


def _split3(a):
    hi = a.astype(BF16)
    r1 = a - hi.astype(F32)
    mid = r1.astype(BF16)
    lo = (r1 - mid.astype(F32)).astype(BF16)
    return hi, mid, lo


def _dot(a, b):
    return jnp.dot(a, b, preferred_element_type=F32)


def _dot_nt(a, b):
    return lax.dot_general(a, b, (((1,), (1,)), ((), ())), preferred_element_type=F32)


def _dot_tn(a, b):
    return lax.dot_general(a, b, (((0,), (0,)), ((), ())), preferred_element_type=F32)


def _sigmoid(x):
    return 1.0 / (1.0 + jnp.exp(-x))


def _rms(x, w):
    ms = jnp.mean(x * x, axis=-1, keepdims=True)
    return x * lax.rsqrt(ms + NORM_EPS) * w


def _inproj_kernel(x_ref, n1w_ref, w_ref, cs_ref, rope_e_ref, rope_c0_ref, qw_ref, kw_ref,
                   mq_ref, mk_ref, lbp_ref,
                   q_out, k_out, v_out, hq_out, lf_out, hv_out, hg_out):
    xn = _rms(x_ref[...], n1w_ref[...]).astype(BF16)

    def proj(off, width):
        return _dot(xn, w_ref[:, off:off + width])

    tabs = _dot_tn(cs_ref[...], rope_e_ref[...])
    c_tab = tabs[:, 0:LANES] + rope_c0_ref[...]
    s1_tab = tabs[:, LANES:2 * LANES]
    s2_tab = tabs[:, 2 * LANES:3 * LANES]

    def norm_rope(t, mavg_ref, w_row, scale):
        ms = _dot((t * t).astype(BF16), mavg_ref[...])
        tn = t * lax.rsqrt(ms + NORM_EPS) * w_row
        if scale != 1.0:
            tn = tn * scale
        outs = []
        for j in range(t.shape[1] // LANES):
            c = tn[:, j * LANES:(j + 1) * LANES]
            outs.append(c * c_tab
                        + pltpu.roll(c, LANES - ROT_HALF, 1) * s1_tab
                        + pltpu.roll(c, ROT_HALF, 1) * s2_tab)
        return outs[0] if len(outs) == 1 else jnp.concatenate(outs, axis=1)

    q_out[...] = norm_rope(proj(_A_Q, ATTN_WIDTH), mq_ref, qw_ref[...], HEAD_DIM ** -0.5).astype(BF16)
    k_out[...] = norm_rope(proj(_A_K, 2 * KV_WIDTH), mk_ref, kw_ref[...], 1.0).astype(BF16)
    v_out[...] = proj(_A_V, 2 * KV_WIDTH).astype(BF16)

    hq = proj(_A_HQ, HGRN_WIDTH)
    hq_out[...] = (hq * _sigmoid(hq)).astype(BF16)
    h0 = lbp_ref[0:1, :]
    h1 = lbp_ref[1:2, :]
    hm = jnp.maximum(h0, h1)
    e0 = jnp.exp(h0 - hm)
    e1 = jnp.exp(h1 - hm)
    lb = e0 / (e0 + e1)
    fg = lb + (1.0 - lb) * _sigmoid(proj(_A_HF, HGRN_WIDTH))
    lf_out[...] = jnp.log(fg) * LOG2_E
    hv_out[...] = proj(_A_HI, HGRN_WIDTH).astype(BF16)
    hg = proj(_A_HG, HGRN_WIDTH)
    hg_out[...] = (hg * _sigmoid(hg)).astype(BF16)


def _rope_constants():
    e = np.zeros((2 * ROT_HALF, 3 * LANES), np.float32)
    c0 = np.zeros((1, LANES), np.float32)
    for lane in range(LANES):
        d = lane % HEAD_DIM
        if d < ROT_HALF:
            e[d, lane] = 1.0
            e[ROT_HALF + d, LANES + lane] = -1.0
        elif d < ROT_DIM:
            e[d - ROT_HALF, lane] = 1.0
            e[ROT_HALF + d - ROT_HALF, 2 * LANES + lane] = 1.0
        else:
            c0[0, lane] = 1.0
    return jnp.asarray(np.concatenate([e, e, e], axis=0), BF16), jnp.asarray(c0, F32)


def _head_mean_matrix(width):
    idx = np.arange(width) // HEAD_DIM
    m = (idx[:, None] == idx[None, :]).astype(np.float32) / HEAD_DIM
    return jnp.asarray(m, BF16)


def _inproj(x2, norm1_w, w_in_a, cs, q_norm_w, k_norm_w, lbp, tm):
    T = x2.shape[0]
    rope_e, rope_c0 = _rope_constants()
    qw = jnp.tile(q_norm_w.astype(F32), N_Q_HEADS)[None, :]
    kw = jnp.tile(k_norm_w.astype(F32), 2 * N_KV_HEADS)[None, :]
    mq = _head_mean_matrix(ATTN_WIDTH)
    mk = _head_mean_matrix(2 * KV_WIDTH)
    row = lambda w: pl.BlockSpec((tm, w), lambda i: (i, 0))
    full = lambda a: pl.BlockSpec(a.shape, lambda i: (0,) * a.ndim)
    ins = [x2, norm1_w[None, :], w_in_a, cs, rope_e, rope_c0, qw, kw, mq, mk, lbp]
    in_specs = ([row(D_MODEL), full(ins[1]), full(w_in_a), pl.BlockSpec((cs.shape[0], tm), lambda i: (0, i))]
                + [full(a) for a in ins[4:]])
    outs = [(ATTN_WIDTH, BF16), (2 * KV_WIDTH, BF16), (2 * KV_WIDTH, BF16), (HGRN_WIDTH, BF16),
            (HGRN_WIDTH, F32), (HGRN_WIDTH, BF16), (HGRN_WIDTH, BF16)]
    return pl.pallas_call(
        _inproj_kernel,
        out_shape=[jax.ShapeDtypeStruct((T, w), dt) for w, dt in outs],
        grid=(T // tm,),
        in_specs=in_specs,
        out_specs=[row(w) for w, _ in outs],
        compiler_params=pltpu.CompilerParams(dimension_semantics=("arbitrary",),
                                             vmem_limit_bytes=VMEM_LIMIT),
        name="inproj",
    )(*ins)


ATTN_QBLOCKS = 8


def _attn_kernel(sink_ref, q_ref, kc_ref, kp_ref, vc_ref, vp_ref, half_ref, o_ref):
    n_qblocks = q_ref.shape[0] // WINDOW
    has_prev = pl.program_id(1) > 0
    above = (lax.broadcasted_iota(jnp.int32, (WINDOW, WINDOW), 1)
             > lax.broadcasted_iota(jnp.int32, (WINDOW, WINDOW), 0))
    left = lax.broadcasted_iota(jnp.int32, (WINDOW, LANES), 1) < HEAD_DIM
    half = (half_ref[0], half_ref[1])

    ks, rhs = [], []
    for h in range(N_KV_HEADS):
        cols = slice(h * LANES, (h + 1) * LANES)
        kall = jnp.concatenate([kp_ref[:, cols], kc_ref[:, cols]], axis=0)
        vall = jnp.concatenate([vp_ref[:, cols], vc_ref[:, cols]], axis=0)
        ks.append([kall * hm for hm in half])
        rhs.append([jnp.concatenate([vall * hm, hm], axis=1) for hm in half])

    units = [(j, h, pr, side) for j in range(n_qblocks) for h in range(N_KV_HEADS)
             for pr in range(GROUP // 2) for side in range(2)]
    scores = []
    for j, h, pr, side in units:
        pair = h * (GROUP // 2) + pr
        qp = q_ref[j * WINDOW:(j + 1) * WINDOW, pair * LANES:(pair + 1) * LANES]
        s = _dot_nt(qp, ks[h][side][j * WINDOW:(j + 2) * WINDOW])
        s_prev = s[:, 0:WINDOW]
        if j == 0:
            s_prev = jnp.where(has_prev, s_prev, NEG_BIG)
        scores.append(jnp.where(above, s_prev, s[:, WINDOW:2 * WINDOW]))
    probs, sink_terms = [], []
    for (j, h, pr, side), s in zip(units, scores):
        sink = sink_ref[2 * (h * (GROUP // 2) + pr) + side]
        m = jnp.maximum(jnp.max(s, axis=-1, keepdims=True), sink)
        p = jnp.exp(s - m)
        probs.append(jnp.concatenate([jnp.where(above, p, 0.0), jnp.where(above, 0.0, p)], axis=1).astype(BF16))
        sink_terms.append(jnp.exp(sink - m))
    acc = []
    for (j, h, pr, side), p in zip(units, probs):
        acc.append(_dot(p, rhs[h][side][j * WINDOW:(j + 2) * WINDOW]))
    for j in range(n_qblocks):
        outs = []
        for u in range(0, len(units), 2):
            if units[u][0] == j:
                both = acc[u] + acc[u + 1]
                den = both[:, LANES:2 * LANES] + jnp.where(left, sink_terms[u], sink_terms[u + 1])
                outs.append(both[:, 0:LANES] / den)
        o_ref[j * WINDOW:(j + 1) * WINDOW, :] = jnp.concatenate(outs, axis=1).astype(BF16)


def _attention(q, k, v, sinks, B, S):
    qblocks = _pick_tile(S // WINDOW, ATTN_QBLOCKS)
    rows = qblocks * WINDOW
    nb = S // rows
    cur = lambda b, n: (b * nb + n, 0)
    prev = lambda b, n: (jnp.maximum((b * nb + n) * qblocks - 1, 0), 0)
    lane_left = np.arange(LANES) < HEAD_DIM
    half = jnp.asarray(np.broadcast_to(np.stack([lane_left, ~lane_left])[:, None, :],
                                       (2, rows + WINDOW, LANES)), BF16)
    return pl.pallas_call(
        _attn_kernel,
        out_shape=jax.ShapeDtypeStruct((B * S, ATTN_WIDTH), BF16),
        grid=(B, nb),
        in_specs=[pl.BlockSpec(memory_space=pltpu.SMEM),
                  pl.BlockSpec((rows, ATTN_WIDTH), cur),
                  pl.BlockSpec((rows, 2 * KV_WIDTH), cur),
                  pl.BlockSpec((WINDOW, 2 * KV_WIDTH), prev),
                  pl.BlockSpec((rows, 2 * KV_WIDTH), cur),
                  pl.BlockSpec((WINDOW, 2 * KV_WIDTH), prev),
                  pl.BlockSpec(half.shape, lambda b, n: (0, 0, 0))],
        out_specs=pl.BlockSpec((rows, ATTN_WIDTH), cur),
        compiler_params=pltpu.CompilerParams(dimension_semantics=("arbitrary", "arbitrary"),
                                             vmem_limit_bytes=VMEM_LIMIT),
        name="swa_attention",
    )(sinks, q, k, k, v, v, half)


_LEVEL_HALVES = (1, 2, 4, 8, 16, 32)


def _hgrn_level_masks():
    t = np.arange(CHUNK)[:, None]
    s = np.arange(CHUNK)[None, :]
    masks = [((t // (2 * h)) == (s // (2 * h))) & ((t & h) != 0) & ((s & h) == 0) for h in _LEVEL_HALVES]
    level = -np.ones((CHUNK, CHUNK), np.float32)
    for li, m in enumerate(masks):
        level[m] = li
    return jnp.asarray(level[None], F32)


def _level_reference(b, half):
    if half >= 4:
        span = max(2 * half, 8)
        pieces = [jnp.broadcast_to(b[s + half - 1:s + half, :], (span, HGRN_DIM))
                  for s in range(0, CHUNK, span)]
    else:
        r8 = lax.broadcasted_iota(jnp.int32, (8, HGRN_DIM), 0)
        pieces = [jnp.where(r8 < 4,
                            jnp.broadcast_to(b[s + 1:s + 2, :], (8, HGRN_DIM)),
                            jnp.broadcast_to(b[s + 5:s + 6, :], (8, HGRN_DIM)))
                  for s in range(0, CHUNK, 8)]
    return pieces[0] if len(pieces) == 1 else jnp.concatenate(pieces, axis=0)


def _hgrn_kernel(hq_ref, lf_ref, hv_ref, hg_ref, nw_ref, tri_ref, lm_ref, o_ref, st_ref):
    @pl.when(pl.program_id(1) == 0)
    def _():
        st_ref[...] = jnp.zeros_like(st_ref)

    tri2 = tri_ref[...]
    odd = (lax.broadcasted_iota(jnp.int32, (CHUNK, HGRN_DIM), 0) & 1) != 0
    level_of = lm_ref[0]
    units = [(c, h) for c in range(hq_ref.shape[0] // CHUNK) for h in range(HGRN_HEADS)]
    sl = lambda c, h: (slice(c * CHUNK, (c + 1) * CHUNK), slice(h * HGRN_DIM, (h + 1) * HGRN_DIM))

    bs = []
    for u, (c, h) in enumerate(units):
        lf2 = lf_ref[sl(c, h)]
        hi = lf2.astype(BF16)
        lo = (lf2 - hi.astype(F32)).astype(BF16)
        bs.append(_dot(tri2, jnp.concatenate([hi, lo], axis=0)))

    outs, accs = [], []
    for u, (c, h) in enumerate(units):
        b = bs[u]
        q = hq_ref[sl(c, h)].astype(F32)
        v_bf = hv_ref[sl(c, h)]
        f = jnp.exp2(lf_ref[sl(c, h)])
        k = 1.0 - f
        b_last = b[CHUNK - 1:CHUNK, :]
        st = st_ref[h]
        o = _dot_nt((q * jnp.exp2(b)).astype(BF16), st.astype(BF16))
        k_out = (k * jnp.exp2(b_last - b)).astype(BF16)
        st_ref[h] = st * jnp.exp2(b_last) + _dot_tn(v_bf, k_out)
        outs.append(o + jnp.sum(q * k, axis=-1, keepdims=True) * v_bf.astype(F32))
        acc = jnp.zeros((CHUNK, CHUNK), F32)
        for li, half in enumerate(_LEVEL_HALVES):
            if half == 1:
                e = jnp.where(odd, f, 1.0)
            else:
                e = jnp.exp2(-jnp.abs(b - _level_reference(b, half)))
            acc = jnp.where(level_of == float(li), _dot_nt((q * e).astype(BF16), (k * e).astype(BF16)), acc)
        accs.append(acc)

    for u, (c, h) in enumerate(units):
        o = outs[u] + _dot(accs[u].astype(BF16), hv_ref[sl(c, h)])
        y = _rms(o, nw_ref[...]) * hg_ref[sl(c, h)].astype(F32)
        o_ref[sl(c, h)] = y.astype(BF16)


def _hgrn(hq, lf, hv, hg, hgrn_norm_w, B, S, tb):
    nt = S // tb
    blk = pl.BlockSpec((tb, HGRN_WIDTH), lambda b, n: (b * nt + n, 0))
    tri = np.tril(np.ones((CHUNK, CHUNK), np.float32))
    tri = jnp.asarray(np.concatenate([tri, tri], axis=1), BF16)
    lm = _hgrn_level_masks()
    return pl.pallas_call(
        _hgrn_kernel,
        out_shape=jax.ShapeDtypeStruct((B * S, HGRN_WIDTH), BF16),
        grid=(B, nt),
        in_specs=[blk, blk, blk, blk,
                  pl.BlockSpec((1, HGRN_DIM), lambda b, n: (0, 0)),
                  pl.BlockSpec(tri.shape, lambda b, n: (0, 0)),
                  pl.BlockSpec(lm.shape, lambda b, n: (0, 0, 0))],
        out_specs=blk,
        scratch_shapes=[pltpu.VMEM((HGRN_HEADS, HGRN_DIM, HGRN_DIM), F32)],
        compiler_params=pltpu.CompilerParams(dimension_semantics=("arbitrary", "arbitrary"),
                                             vmem_limit_bytes=VMEM_LIMIT),
        name="hgrn2",
    )(hq, lf, hv, hg, hgrn_norm_w[None, :].astype(F32), tri, lm)


def _merge_kernel(x_ref, attn_ref, hgrn_ref, n1w_ref, wz_ref, wba_ref, wbh_ref, wout_ref, n2w_ref,
                  wr_ref, br_ref, utri_ref, x1_out, hn_out, route_out, count_out, run_ref):
    @pl.when(pl.program_id(0) == 0)
    def _():
        run_ref[...] = jnp.zeros_like(run_ref)

    x = x_ref[...]
    xn = _rms(x, n1w_ref[...]).astype(BF16)
    za = _sigmoid(_dot(xn, wz_ref[:, 0:D_MODEL]))
    zb = _sigmoid(_dot(xn, wz_ref[:, D_MODEL:2 * D_MODEL]))
    mixed = za * _dot(attn_ref[...], wba_ref[...]) + zb * _dot(hgrn_ref[...], wbh_ref[...])
    x1 = x + _dot(mixed.astype(BF16), wout_ref[...])
    x1_out[...] = x1
    hn = _rms(x1, n2w_ref[...])
    hn_out[...] = _pack_bf16_pairs(hn)

    logits = _dot_nt(wr_ref[...], hn.astype(BF16)) + br_ref[...]
    r = lax.broadcasted_iota(jnp.int32, logits.shape, 0).astype(F32)
    far = float(ROUTER_ROWS)
    cmax = lambda a: jnp.max(a, axis=0, keepdims=True)
    cmin = lambda a: jnp.min(a, axis=0, keepdims=True)
    csum = lambda a: jnp.sum(a, axis=0, keepdims=True)

    lg = jnp.where(r < N_GROUPS, logits, NEG_BIG)
    mg = cmax(lg)
    gsel = cmin(jnp.where(lg == mg, r, far))
    pgsel = 1.0 / csum(jnp.exp(lg - mg))

    lo = N_GROUPS + EXPERTS_PER_GROUP * gsel
    le = jnp.where((r >= lo) & (r < lo + EXPERTS_PER_GROUP), logits, NEG_BIG)
    m1 = cmax(le)
    i1 = cmin(jnp.where(le == m1, r, far))
    se = csum(jnp.exp(le - m1))
    le2 = jnp.where(r == i1, NEG_BIG, le)
    m2 = cmax(le2)
    i2 = cmin(jnp.where(le2 == m2, r, far))
    top0 = 1.0 / se
    top1 = jnp.exp(m2 - m1) / se
    tsum = top0 + top1
    w0 = pgsel * top0 / tsum
    w1 = pgsel * top1 / tsum

    sel1 = r == i1
    sel2 = r == i2
    onehot = jnp.where(sel1 | sel2, 1.0, 0.0)
    before = _dot(onehot.astype(BF16), utri_ref[...]) + run_ref[...]
    r0 = csum(jnp.where(sel1, before, 0.0))
    r1 = csum(jnp.where(sel2, before, 0.0))
    run_new = run_ref[...] + jnp.sum(onehot, axis=1, keepdims=True)
    run_ref[...] = run_new
    count_out[...] = run_new

    row8 = lax.broadcasted_iota(jnp.int32, route_out.shape, 0)
    vals = (i1 - N_GROUPS, i2 - N_GROUPS, w0, w1, r0, r1)
    route = jnp.zeros(route_out.shape, F32)
    for j, val in enumerate(vals):
        route = jnp.where(row8 == j, val, route)
    route_out[...] = route


def _merge(x2, attn, hgrn, norm1_w, w_z, w_ba, w_bh, w_out, norm2_w, w_r, br, tm):
    T = x2.shape[0]
    row = lambda w: pl.BlockSpec((tm, w), lambda i: (i, 0))
    full = lambda a: pl.BlockSpec(a.shape, lambda i: (0,) * a.ndim)
    utri = jnp.asarray(np.triu(np.ones((tm, tm), np.float32), 1), BF16)
    ins = [x2, attn, hgrn, norm1_w[None, :], w_z, w_ba, w_bh, w_out, norm2_w[None, :], w_r, br, utri]
    in_specs = [row(D_MODEL), row(ATTN_WIDTH), row(HGRN_WIDTH)] + [full(a) for a in ins[3:]]
    return pl.pallas_call(
        _merge_kernel,
        out_shape=[jax.ShapeDtypeStruct((T, D_MODEL), F32), jax.ShapeDtypeStruct((T, HALF), jnp.uint32),
                   jax.ShapeDtypeStruct((ROUTE_ROWS, T), F32), jax.ShapeDtypeStruct((ROUTER_ROWS, tm), F32)],
        grid=(T // tm,),
        in_specs=in_specs,
        out_specs=[row(D_MODEL), row(HALF), pl.BlockSpec((ROUTE_ROWS, tm), lambda i: (0, i)),
                   pl.BlockSpec((ROUTER_ROWS, tm), lambda i: (0, 0))],
        scratch_shapes=[pltpu.VMEM((ROUTER_ROWS, tm), F32)],
        compiler_params=pltpu.CompilerParams(dimension_semantics=("arbitrary",),
                                             vmem_limit_bytes=VMEM_LIMIT),
        name="merge_router",
    )(*ins)


HALF = D_MODEL // 2


def _pack_bf16_pairs(x):
    bits = pltpu.bitcast(x.astype(BF16).astype(F32), jnp.uint32)
    return (bits[:, :HALF] >> 16) | bits[:, HALF:]


def _unpack_bf16_pairs(words):
    lo = pltpu.bitcast(words << 16, F32)
    hi = pltpu.bitcast(words & jnp.uint32(0xFFFF0000), F32)
    return lo, hi


SC_CORES = 2
SC_SUBCORES = 16
SC_CHUNK = 64


def _sc_mesh():
    return plsc.VectorSubcoreMesh(core_axis_name="c", subcore_axis_name="s",
                                  num_cores=SC_CORES, num_subcores=SC_SUBCORES)


def _sc_worker():
    return lax.axis_index("s") * SC_CORES + lax.axis_index("c")


def _dispatch(pos, pad_rows, hn, n_rows):
    T, width = hn.shape
    workers = SC_CORES * SC_SUBCORES
    per_worker = T // workers
    n_chunks = per_worker // SC_CHUNK
    assert per_worker * workers == T and n_chunks * SC_CHUNK == per_worker and n_chunks % 2 == 0
    idx = pos.T.reshape(TOP_K, workers, n_chunks, SC_CHUNK).transpose(1, 0, 2, 3)
    pad_chunks = pad_rows.size // (workers * SC_CHUNK)
    pad_idx = pad_rows.reshape(workers, pad_chunks, SC_CHUNK)
    zeros = jnp.zeros((SC_CHUNK, width), hn.dtype)

    @functools.partial(
        pl.kernel, mesh=_sc_mesh(), name="moe_dispatch",
        out_type=jax.ShapeDtypeStruct((n_rows + SC_CHUNK, width), hn.dtype),
        scratch_types=[pltpu.VMEM((TOP_K, n_chunks, SC_CHUNK), jnp.int32),
                       pltpu.VMEM((2, SC_CHUNK, width), hn.dtype),
                       pltpu.VMEM((pad_chunks, SC_CHUNK), jnp.int32),
                       pltpu.VMEM((SC_CHUNK, width), hn.dtype),
                       pltpu.SemaphoreType.DMA((2,)), pltpu.SemaphoreType.DMA((2,)), pltpu.SemaphoreType.DMA])
    def dispatch_kernel(rows_hbm, idx_hbm, pad_hbm, zeros_hbm, out_hbm,
                        idx_v, rows_v, pad_v, zeros_v, load_sem, scatter_sem, pad_sem):
        wid = _sc_worker()
        base = wid * per_worker
        pltpu.sync_copy(idx_hbm.at[wid], idx_v)
        pltpu.sync_copy(pad_hbm.at[wid], pad_v)
        pltpu.sync_copy(zeros_hbm, zeros_v)

        def zero_fill(j):
            return pltpu.make_async_copy(zeros_v, out_hbm.at[pad_v.at[j]], pad_sem)

        for j in range(pad_chunks):
            zero_fill(j).start()

        def load(c, b):
            off = pl.multiple_of(c * SC_CHUNK, SC_CHUNK)
            return pltpu.make_async_copy(rows_hbm.at[pl.ds(base + off, SC_CHUNK)], rows_v.at[b], load_sem.at[b])

        def scatter(c, b, k):
            return pltpu.make_async_copy(rows_v.at[b], out_hbm.at[idx_v.at[k, c]], scatter_sem.at[b])

        load(0, 0).start()

        @pl.loop(0, n_chunks, step=2)
        def _(c0):
            for b in range(2):
                c = c0 + b
                load(c, b).wait()
                for k in range(TOP_K):
                    scatter(c, b, k).start()

                @pl.when(c >= 1)
                def _():
                    for k in range(TOP_K):
                        scatter(c - 1, 1 - b, k).wait()

                @pl.when(c + 1 < n_chunks)
                def _():
                    load(c + 1, 1 - b).start()

        for k in range(TOP_K):
            scatter(n_chunks - 1, (n_chunks - 1) % 2, k).wait()
        for j in range(pad_chunks):
            zero_fill(j).wait()

    return dispatch_kernel(hn, idx, pad_idx, zeros)


def _expert_kernel(tile_ref, exp_ref, valid_ref, x_ref, wg_ref, wu_ref, wd_ref, y_ref, wg_s, wu_s, wd_s):
    w = pl.program_id(0)
    prev = jnp.maximum(w - 1, 0)

    @pl.when((w == 0) | (exp_ref[w] != exp_ref[prev]))
    def _():
        wg_s[...] = wg_ref[0].astype(BF16)
        wu_s[...] = wu_ref[0].astype(BF16)
        wd_s[...] = wd_ref[0].astype(BF16)

    def ffn(words):
        lo, hi = _unpack_bf16_pairs(words)
        lo = lo.astype(BF16)
        hi = hi.astype(BF16)
        gate = _dot(lo, wg_s[0:HALF, :]) + _dot(hi, wg_s[HALF:, :])
        up = _dot(lo, wu_s[0:HALF, :]) + _dot(hi, wu_s[HALF:, :])
        return _pack_bf16_pairs(_dot((gate * _sigmoid(gate) * up).astype(BF16), wd_s[...]))

    @pl.when(valid_ref[w] != 0)
    def _():
        y_ref[...] = ffn(x_ref[...])


def _experts(items, xs, w_gate, w_up, w_down):
    tile, exp, valid = items
    wspec = lambda shape: pl.BlockSpec((1,) + shape, lambda w, t, e, v: (e[w], 0, 0))
    xspec = pl.BlockSpec((EXPERT_TILE, HALF), lambda w, t, e, v: (t[w], 0))
    return pl.pallas_call(
        _expert_kernel,
        out_shape=jax.ShapeDtypeStruct((tile.shape[0] * EXPERT_TILE, HALF), jnp.uint32),
        grid_spec=pltpu.PrefetchScalarGridSpec(
            num_scalar_prefetch=3,
            grid=(tile.shape[0],),
            in_specs=[xspec, wspec((D_MODEL, EXPERT_FF)), wspec((D_MODEL, EXPERT_FF)),
                      wspec((EXPERT_FF, D_MODEL))],
            out_specs=xspec,
            scratch_shapes=[pltpu.VMEM((D_MODEL, EXPERT_FF), BF16),
                            pltpu.VMEM((D_MODEL, EXPERT_FF), BF16),
                            pltpu.VMEM((EXPERT_FF, D_MODEL), BF16)]),
        compiler_params=pltpu.CompilerParams(dimension_semantics=("arbitrary",),
                                             vmem_limit_bytes=VMEM_LIMIT),
        name="moe_experts",
    )(tile, exp, valid, xs, w_gate, w_up, w_down)


def _sc_gather_rows(table, idx):
    n = idx.shape[0]
    width = table.shape[1]
    per_worker = n // (SC_CORES * SC_SUBCORES)
    n_chunks = per_worker // SC_CHUNK
    assert per_worker * SC_CORES * SC_SUBCORES == n and n_chunks * SC_CHUNK == per_worker and n_chunks % 2 == 0

    @functools.partial(
        pl.kernel, mesh=_sc_mesh(), name="moe_row_gather",
        out_type=jax.ShapeDtypeStruct((n, width), table.dtype),
        scratch_types=[pltpu.VMEM((per_worker,), jnp.int32),
                       pltpu.VMEM((2, SC_CHUNK, width), table.dtype),
                       pltpu.SemaphoreType.DMA((2,))])
    def gather_kernel(table_hbm, idx_hbm, out_hbm, idx_v, rows_v, sem):
        base = _sc_worker() * per_worker
        pltpu.sync_copy(idx_hbm.at[pl.ds(base, per_worker)], idx_v)

        def gather(c, b):
            off = pl.multiple_of(c * SC_CHUNK, SC_CHUNK)
            return pltpu.make_async_copy(table_hbm.at[idx_v.at[pl.ds(off, SC_CHUNK)]], rows_v.at[b], sem.at[b])

        gather(0, 0).start()

        @pl.loop(0, n_chunks, step=2)
        def _(c0):
            for b in range(2):
                c = c0 + b

                @pl.when(c + 1 < n_chunks)
                def _():
                    gather(c + 1, 1 - b).start()

                gather(c, b).wait()
                off = pl.multiple_of(c * SC_CHUNK, SC_CHUNK)
                pltpu.sync_copy(rows_v.at[b], out_hbm.at[pl.ds(base + off, SC_CHUNK)])

    return gather_kernel(table, idx)


def _combine_kernel(x1_ref, route_ref, y0_ref, y1_ref, o_ref):
    gates = route_ref[...].T
    w0 = gates[:, 2:3]
    w1 = gates[:, 3:4]
    lo0, hi0 = _unpack_bf16_pairs(y0_ref[...])
    lo1, hi1 = _unpack_bf16_pairs(y1_ref[...])
    o_ref[:, 0:HALF] = x1_ref[:, 0:HALF] + w0 * lo0 + w1 * lo1
    o_ref[:, HALF:] = x1_ref[:, HALF:] + w0 * hi0 + w1 * hi1


def _combine(pos, x1, route, y, tk):
    T = x1.shape[0]
    nt = T // tk
    ysel = _sc_gather_rows(y, pos.T.reshape(-1))
    return pl.pallas_call(
        _combine_kernel,
        out_shape=jax.ShapeDtypeStruct((T, D_MODEL), F32),
        grid=(nt,),
        in_specs=[pl.BlockSpec((tk, D_MODEL), lambda i: (i, 0)),
                  pl.BlockSpec((ROUTE_ROWS, tk), lambda i: (0, i)),
                  pl.BlockSpec((tk, HALF), lambda i: (i, 0)),
                  pl.BlockSpec((tk, HALF), lambda i: (i + nt, 0))],
        out_specs=pl.BlockSpec((tk, D_MODEL), lambda i: (i, 0)),
        compiler_params=pltpu.CompilerParams(dimension_semantics=("arbitrary",),
                                             vmem_limit_bytes=VMEM_LIMIT),
        name="moe_combine",
    )(x1, route, ysel, ysel)


def _routing_tables(route, counts, T):
    e = route[0:TOP_K].astype(jnp.int32)
    rank = route[4:4 + TOP_K].astype(jnp.int32)
    counts = counts.astype(jnp.int32)
    padded = (counts + EXPERT_TILE - 1) // EXPERT_TILE * EXPERT_TILE
    ends = jnp.cumsum(padded)
    starts = ends - padded
    ids = jnp.arange(N_EXPERTS, dtype=jnp.int32)
    pos = rank + jnp.sum(jnp.where(e[:, :, None] == ids, starts, 0), axis=-1)
    n_tiles = TOP_K * T // EXPERT_TILE + N_EXPERTS
    tile0 = jnp.arange(n_tiles, dtype=jnp.int32)
    valid = (tile0 * EXPERT_TILE < ends[-1]).astype(jnp.int32)
    tile = jnp.minimum(tile0, ends[-1] // EXPERT_TILE - 1)
    exp = jnp.minimum(jnp.sum(ends[None, :] <= (tile * EXPERT_TILE)[:, None], axis=1), N_EXPERTS - 1).astype(jnp.int32)
    j = jnp.arange(EXPERT_TILE, dtype=jnp.int32)[None, :]
    n_pad = (padded - counts)[:, None]
    spare = n_tiles * EXPERT_TILE + j % SC_CHUNK
    pad_rows = jnp.where(n_pad > 0, (starts + counts)[:, None] + j % jnp.maximum(n_pad, 1), spare)
    return pos.T, pad_rows, (tile, exp, valid)


def _pick_tile(n, pref):
    t = pref
    while n % t:
        t //= 2
    return t


def kernel(x, positions, norm1_w, w_in, q_norm_w, k_norm_w, attn_sinks, hgrn_lower_bounds, hgrn_norm_w,
           w_branch_attn, w_branch_hgrn, w_out, norm2_w, w_router_group, b_router_group, w_router_expert,
           b_router_expert, w_gate_experts, w_up_experts, w_down_experts):
    B, S, D = x.shape
    T = B * S
    x2 = x.reshape(T, D)
    tm = _pick_tile(T, 512)

    inv_freq = ROPE_THETA ** (-jnp.arange(0, ROT_DIM, 2, dtype=F32) / ROT_DIM)
    ang = inv_freq[:, None] * positions.astype(F32).reshape(1, T)
    cs = jnp.concatenate(_split3(jnp.concatenate([jnp.cos(ang), jnp.sin(ang)], axis=0)), axis=0)

    w_in0 = w_in[0].astype(BF16)
    heads = lambda off: [w_in0[:, off + h * HEAD_DIM:off + (h + 1) * HEAD_DIM]
                         for h in range(N_KV_HEADS) for _ in range(2)]
    w_in_a = jnp.concatenate([w_in0[:, :_OFF_K]] + heads(_OFF_K) + heads(_OFF_V) + [w_in0[:, _OFF_HQ:_OFF_Z]],
                             axis=1)
    w_z = w_in0[:, _OFF_Z:]

    q, k, v, hq, lf, hv, hg = _inproj(x2, norm1_w[0], w_in_a, cs, q_norm_w[0], k_norm_w[0],
                                      hgrn_lower_bounds.astype(F32), _pick_tile(T, 1024))
    attn = _attention(q, k, v, attn_sinks[0].astype(F32), B, S)
    hgrn = _hgrn(hq, lf, hv, hg, hgrn_norm_w[0], B, S, _pick_tile(S, 1024))

    pad = ROUTER_ROWS - N_GROUPS - N_EXPERTS
    w_r = jnp.concatenate([w_router_group[0].T, w_router_expert[0].T, jnp.zeros((pad, D), F32)], axis=0)
    b_r =jnp.concatenate([b_router_group[0], b_router_expert[0], jnp.zeros((pad,), F32)]).astype(F32)
    b_r = jnp.broadcast_to(b_r[:, None], (ROUTER_ROWS, tm))

    x1, hn, route, counts = _merge(x2, attn, hgrn, norm1_w[0], w_z, w_branch_attn[0].astype(BF16),
                                   w_branch_hgrn[0].astype(BF16), w_out[0].astype(BF16), norm2_w[0],
                                   w_r.astype(BF16), b_r, tm)

    pos, pad_rows, items = _routing_tables(route, counts[N_GROUPS:N_GROUPS + N_EXPERTS, 0], T)
    tk = _pick_tile(T, 1024)
    xs = _dispatch(pos, pad_rows, hn, items[0].shape[0] * EXPERT_TILE)
    y = _experts(items, xs, w_gate_experts[0], w_up_experts[0], w_down_experts[0])
    out = _combine(pos, x1, route, y, tk)
    return out.reshape(B, S, D)
```

```python
import functools

import numpy as np
import jax
import jax.numpy as jnp
from jax import lax
from jax.experimental import pallas as pl
from jax.experimental.pallas import tpu as pltpu
from jax.experimental.pallas import tpu_sc as plsc

F32 = jnp.float32
BF16 = jnp.bfloat16

D_MODEL = 1024
N_Q_HEADS = 8
N_KV_HEADS = 2
GROUP = N_Q_HEADS // N_KV_HEADS
HEAD_DIM = 64
ROT_DIM = HEAD_DIM // 4
ROT_HALF = ROT_DIM // 2
ROPE_THETA = 500000.0
WINDOW = 128
ATTN_WIDTH = N_Q_HEADS * HEAD_DIM
KV_WIDTH = N_KV_HEADS * HEAD_DIM

HGRN_HEADS = 4
HGRN_DIM = 128
HGRN_WIDTH = HGRN_HEADS * HGRN_DIM
CHUNK = 64

N_GROUPS = 4
EXPERTS_PER_GROUP = 8
N_EXPERTS = N_GROUPS * EXPERTS_PER_GROUP
TOP_K = 2
EXPERT_FF = 512
EXPERT_TILE = 512
NORM_EPS = 1e-6
ROUTER_ROWS = 40
ROUTE_ROWS = 8

LANES = 128
NEG_BIG = -1e30
LOG2_E = 1.4426950408889634

_OFF_Q, _OFF_K, _OFF_V = 0, ATTN_WIDTH, ATTN_WIDTH + KV_WIDTH
_OFF_HQ = ATTN_WIDTH + 2 * KV_WIDTH
_OFF_HF = _OFF_HQ + HGRN_WIDTH
_OFF_HI = _OFF_HF + HGRN_WIDTH
_OFF_HG = _OFF_HI + HGRN_WIDTH
_OFF_Z = _OFF_HG + HGRN_WIDTH
_A_Q, _A_K, _A_V = 0, ATTN_WIDTH, ATTN_WIDTH + 2 * KV_WIDTH
_A_HQ = ATTN_WIDTH + 4 * KV_WIDTH
_A_HF = _A_HQ + HGRN_WIDTH
_A_HI = _A_HF + HGRN_WIDTH
_A_HG = _A_HI + HGRN_WIDTH

VMEM_LIMIT = 56 * 1024 * 1024


def _split3(a):
    hi = a.astype(BF16)
    r1 = a - hi.astype(F32)
    mid = r1.astype(BF16)
    lo = (r1 - mid.astype(F32)).astype(BF16)
    return hi, mid, lo


def _dot(a, b):
    return jnp.dot(a, b, preferred_element_type=F32)


def _dot_nt(a, b):
    return lax.dot_general(a, b, (((1,), (1,)), ((), ())), preferred_element_type=F32)


def _dot_tn(a, b):
    return lax.dot_general(a, b, (((0,), (0,)), ((), ())), preferred_element_type=F32)


def _sigmoid(x):
    return 1.0 / (1.0 + jnp.exp(-x))


def _rms(x, w):
    ms = jnp.mean(x * x, axis=-1, keepdims=True)
    return x * lax.rsqrt(ms + NORM_EPS) * w


def _inproj_kernel(x_ref, n1w_ref, w_ref, cs_ref, rope_e_ref, rope_c0_ref, qw_ref, kw_ref,
                   mq_ref, mk_ref, lbp_ref,
                   q_out, k_out, v_out, hq_out, lf_out, hv_out, hg_out):
    xn = _rms(x_ref[...], n1w_ref[...]).astype(BF16)

    def proj(off, width):
        return _dot(xn, w_ref[:, off:off + width])

    tabs = _dot_tn(cs_ref[...], rope_e_ref[...])
    c_tab = tabs[:, 0:LANES] + rope_c0_ref[...]
    s1_tab = tabs[:, LANES:2 * LANES]
    s2_tab = tabs[:, 2 * LANES:3 * LANES]

    def norm_rope(t, mavg_ref, w_row, scale):
        ms = _dot((t * t).astype(BF16), mavg_ref[...])
        tn = t * lax.rsqrt(ms + NORM_EPS) * w_row
        if scale != 1.0:
            tn = tn * scale
        outs = []
        for j in range(t.shape[1] // LANES):
            c = tn[:, j * LANES:(j + 1) * LANES]
            outs.append(c * c_tab
                        + pltpu.roll(c, LANES - ROT_HALF, 1) * s1_tab
                        + pltpu.roll(c, ROT_HALF, 1) * s2_tab)
        return outs[0] if len(outs) == 1 else jnp.concatenate(outs, axis=1)

    q_out[...] = norm_rope(proj(_A_Q, ATTN_WIDTH), mq_ref, qw_ref[...], HEAD_DIM ** -0.5).astype(BF16)
    k_out[...] = norm_rope(proj(_A_K, 2 * KV_WIDTH), mk_ref, kw_ref[...], 1.0).astype(BF16)
    v_out[...] = proj(_A_V, 2 * KV_WIDTH).astype(BF16)

    hq = proj(_A_HQ, HGRN_WIDTH)
    hq_out[...] = (hq * _sigmoid(hq)).astype(BF16)
    h0 = lbp_ref[0:1, :]
    h1 = lbp_ref[1:2, :]
    hm = jnp.maximum(h0, h1)
    e0 = jnp.exp(h0 - hm)
    e1 = jnp.exp(h1 - hm)
    lb = e0 / (e0 + e1)
    fg = lb + (1.0 - lb) * _sigmoid(proj(_A_HF, HGRN_WIDTH))
    lf_out[...] = jnp.log(fg) * LOG2_E
    hv_out[...] = proj(_A_HI, HGRN_WIDTH).astype(BF16)
    hg = proj(_A_HG, HGRN_WIDTH)
    hg_out[...] = (hg * _sigmoid(hg)).astype(BF16)


def _rope_constants():
    e = np.zeros((2 * ROT_HALF, 3 * LANES), np.float32)
    c0 = np.zeros((1, LANES), np.float32)
    for lane in range(LANES):
        d = lane % HEAD_DIM
        if d < ROT_HALF:
            e[d, lane] = 1.0
            e[ROT_HALF + d, LANES + lane] = -1.0
        elif d < ROT_DIM:
            e[d - ROT_HALF, lane] = 1.0
            e[ROT_HALF + d - ROT_HALF, 2 * LANES + lane] = 1.0
        else:
            c0[0, lane] = 1.0
    return jnp.asarray(np.concatenate([e, e, e], axis=0), BF16), jnp.asarray(c0, F32)


def _head_mean_matrix(width):
    idx = np.arange(width) // HEAD_DIM
    m = (idx[:, None] == idx[None, :]).astype(np.float32) / HEAD_DIM
    return jnp.asarray(m, BF16)


def _inproj(x2, norm1_w, w_in_a, cs, q_norm_w, k_norm_w, lbp, tm):
    T = x2.shape[0]
    rope_e, rope_c0 = _rope_constants()
    qw = jnp.tile(q_norm_w.astype(F32), N_Q_HEADS)[None, :]
    kw = jnp.tile(k_norm_w.astype(F32), 2 * N_KV_HEADS)[None, :]
    mq = _head_mean_matrix(ATTN_WIDTH)
    mk = _head_mean_matrix(2 * KV_WIDTH)
    row = lambda w: pl.BlockSpec((tm, w), lambda i: (i, 0))
    full = lambda a: pl.BlockSpec(a.shape, lambda i: (0,) * a.ndim)
    ins = [x2, norm1_w[None, :], w_in_a, cs, rope_e, rope_c0, qw, kw, mq, mk, lbp]
    in_specs = ([row(D_MODEL), full(ins[1]), full(w_in_a), pl.BlockSpec((cs.shape[0], tm), lambda i: (0, i))]
                + [full(a) for a in ins[4:]])
    outs = [(ATTN_WIDTH, BF16), (2 * KV_WIDTH, BF16), (2 * KV_WIDTH, BF16), (HGRN_WIDTH, BF16),
            (HGRN_WIDTH, F32), (HGRN_WIDTH, BF16), (HGRN_WIDTH, BF16)]
    return pl.pallas_call(
        _inproj_kernel,
        out_shape=[jax.ShapeDtypeStruct((T, w), dt) for w, dt in outs],
        grid=(T // tm,),
        in_specs=in_specs,
        out_specs=[row(w) for w, _ in outs],
        compiler_params=pltpu.CompilerParams(dimension_semantics=("arbitrary",),
                                             vmem_limit_bytes=VMEM_LIMIT),
        name="inproj",
    )(*ins)


ATTN_QBLOCKS = 8


def _attn_kernel(sink_ref, q_ref, kc_ref, kp_ref, vc_ref, vp_ref, half_ref, o_ref):
    n_qblocks = q_ref.shape[0] // WINDOW
    has_prev = pl.program_id(1) > 0
    above = (lax.broadcasted_iota(jnp.int32, (WINDOW, WINDOW), 1)
             > lax.broadcasted_iota(jnp.int32, (WINDOW, WINDOW), 0))
    left = lax.broadcasted_iota(jnp.int32, (WINDOW, LANES), 1) < HEAD_DIM
    half = (half_ref[0], half_ref[1])

    ks, rhs = [], []
    for h in range(N_KV_HEADS):
        cols = slice(h * LANES, (h + 1) * LANES)
        kall = jnp.concatenate([kp_ref[:, cols], kc_ref[:, cols]], axis=0)
        vall = jnp.concatenate([vp_ref[:, cols], vc_ref[:, cols]], axis=0)
        ks.append([kall * hm for hm in half])
        rhs.append([jnp.concatenate([vall * hm, hm], axis=1) for hm in half])

    units = [(j, h, pr, side) for j in range(n_qblocks) for h in range(N_KV_HEADS)
             for pr in range(GROUP // 2) for side in range(2)]
    scores = []
    for j, h, pr, side in units:
        pair = h * (GROUP // 2) + pr
        qp = q_ref[j * WINDOW:(j + 1) * WINDOW, pair * LANES:(pair + 1) * LANES]
        s = _dot_nt(qp, ks[h][side][j * WINDOW:(j + 2) * WINDOW])
        s_prev = s[:, 0:WINDOW]
        if j == 0:
            s_prev = jnp.where(has_prev, s_prev, NEG_BIG)
        scores.append(jnp.where(above, s_prev, s[:, WINDOW:2 * WINDOW]))
    probs, sink_terms = [], []
    for (j, h, pr, side), s in zip(units, scores):
        sink = sink_ref[2 * (h * (GROUP // 2) + pr) + side]
        m = jnp.maximum(jnp.max(s, axis=-1, keepdims=True), sink)
        p = jnp.exp(s - m)
        probs.append(jnp.concatenate([jnp.where(above, p, 0.0), jnp.where(above, 0.0, p)], axis=1).astype(BF16))
        sink_terms.append(jnp.exp(sink - m))
    acc = []
    for (j, h, pr, side), p in zip(units, probs):
        acc.append(_dot(p, rhs[h][side][j * WINDOW:(j + 2) * WINDOW]))
    for j in range(n_qblocks):
        outs = []
        for u in range(0, len(units), 2):
            if units[u][0] == j:
                both = acc[u] + acc[u + 1]
                den = both[:, LANES:2 * LANES] + jnp.where(left, sink_terms[u], sink_terms[u + 1])
                outs.append(both[:, 0:LANES] / den)
        o_ref[j * WINDOW:(j + 1) * WINDOW, :] = jnp.concatenate(outs, axis=1).astype(BF16)


def _attention(q, k, v, sinks, B, S):
    qblocks = _pick_tile(S // WINDOW, ATTN_QBLOCKS)
    rows = qblocks * WINDOW
    nb = S // rows
    cur = lambda b, n: (b * nb + n, 0)
    prev = lambda b, n: (jnp.maximum((b * nb + n) * qblocks - 1, 0), 0)
    lane_left = np.arange(LANES) < HEAD_DIM
    half = jnp.asarray(np.broadcast_to(np.stack([lane_left, ~lane_left])[:, None, :],
                                       (2, rows + WINDOW, LANES)), BF16)
    return pl.pallas_call(
        _attn_kernel,
        out_shape=jax.ShapeDtypeStruct((B * S, ATTN_WIDTH), BF16),
        grid=(B, nb),
        in_specs=[pl.BlockSpec(memory_space=pltpu.SMEM),
                  pl.BlockSpec((rows, ATTN_WIDTH), cur),
                  pl.BlockSpec((rows, 2 * KV_WIDTH), cur),
                  pl.BlockSpec((WINDOW, 2 * KV_WIDTH), prev),
                  pl.BlockSpec((rows, 2 * KV_WIDTH), cur),
                  pl.BlockSpec((WINDOW, 2 * KV_WIDTH), prev),
                  pl.BlockSpec(half.shape, lambda b, n: (0, 0, 0))],
        out_specs=pl.BlockSpec((rows, ATTN_WIDTH), cur),
        compiler_params=pltpu.CompilerParams(dimension_semantics=("arbitrary", "arbitrary"),
                                             vmem_limit_bytes=VMEM_LIMIT),
        name="swa_attention",
    )(sinks, q, k, k, v, v, half)


_LEVEL_HALVES = (1, 2, 4, 8, 16, 32)


def _hgrn_level_masks():
    t = np.arange(CHUNK)[:, None]
    s = np.arange(CHUNK)[None, :]
    masks = [((t // (2 * h)) == (s // (2 * h))) & ((t & h) != 0) & ((s & h) == 0) for h in _LEVEL_HALVES]
    level = -np.ones((CHUNK, CHUNK), np.float32)
    for li, m in enumerate(masks):
        level[m] = li
    return jnp.asarray(level[None], F32)


def _level_reference(b, half):
    if half >= 4:
        span = max(2 * half, 8)
        pieces = [jnp.broadcast_to(b[s + half - 1:s + half, :], (span, HGRN_DIM))
                  for s in range(0, CHUNK, span)]
    else:
        r8 = lax.broadcasted_iota(jnp.int32, (8, HGRN_DIM), 0)
        pieces = [jnp.where(r8 < 4,
                            jnp.broadcast_to(b[s + 1:s + 2, :], (8, HGRN_DIM)),
                            jnp.broadcast_to(b[s + 5:s + 6, :], (8, HGRN_DIM)))
                  for s in range(0, CHUNK, 8)]
    return pieces[0] if len(pieces) == 1 else jnp.concatenate(pieces, axis=0)


def _hgrn_kernel(hq_ref, lf_ref, hv_ref, hg_ref, nw_ref, tri_ref, lm_ref, o_ref, st_ref):
    @pl.when(pl.program_id(1) == 0)
    def _():
        st_ref[...] = jnp.zeros_like(st_ref)

    tri2 = tri_ref[...]
    odd = (lax.broadcasted_iota(jnp.int32, (CHUNK, HGRN_DIM), 0) & 1) != 0
    level_of = lm_ref[0]
    units = [(c, h) for c in range(hq_ref.shape[0] // CHUNK) for h in range(HGRN_HEADS)]
    sl = lambda c, h: (slice(c * CHUNK, (c + 1) * CHUNK), slice(h * HGRN_DIM, (h + 1) * HGRN_DIM))

    bs = []
    for u, (c, h) in enumerate(units):
        lf2 = lf_ref[sl(c, h)]
        hi = lf2.astype(BF16)
        lo = (lf2 - hi.astype(F32)).astype(BF16)
        bs.append(_dot(tri2, jnp.concatenate([hi, lo], axis=0)))

    outs, accs = [], []
    for u, (c, h) in enumerate(units):
        b = bs[u]
        q = hq_ref[sl(c, h)].astype(F32)
        v_bf = hv_ref[sl(c, h)]
        f = jnp.exp2(lf_ref[sl(c, h)])
        k = 1.0 - f
        b_last = b[CHUNK - 1:CHUNK, :]
        st = st_ref[h]
        o = _dot_nt((q * jnp.exp2(b)).astype(BF16), st.astype(BF16))
        k_out = (k * jnp.exp2(b_last - b)).astype(BF16)
        st_ref[h] = st * jnp.exp2(b_last) + _dot_tn(v_bf, k_out)
        outs.append(o + jnp.sum(q * k, axis=-1, keepdims=True) * v_bf.astype(F32))
        acc = jnp.zeros((CHUNK, CHUNK), F32)
        for li, half in enumerate(_LEVEL_HALVES):
            if half == 1:
                e = jnp.where(odd, f, 1.0)
            else:
                e = jnp.exp2(-jnp.abs(b - _level_reference(b, half)))
            acc = jnp.where(level_of == float(li), _dot_nt((q * e).astype(BF16), (k * e).astype(BF16)), acc)
        accs.append(acc)

    for u, (c, h) in enumerate(units):
        o = outs[u] + _dot(accs[u].astype(BF16), hv_ref[sl(c, h)])
        y = _rms(o, nw_ref[...]) * hg_ref[sl(c, h)].astype(F32)
        o_ref[sl(c, h)] = y.astype(BF16)


def _hgrn(hq, lf, hv, hg, hgrn_norm_w, B, S, tb):
    nt = S // tb
    blk = pl.BlockSpec((tb, HGRN_WIDTH), lambda b, n: (b * nt + n, 0))
    tri = np.tril(np.ones((CHUNK, CHUNK), np.float32))
    tri = jnp.asarray(np.concatenate([tri, tri], axis=1), BF16)
    lm = _hgrn_level_masks()
    return pl.pallas_call(
        _hgrn_kernel,
        out_shape=jax.ShapeDtypeStruct((B * S, HGRN_WIDTH), BF16),
        grid=(B, nt),
        in_specs=[blk, blk, blk, blk,
                  pl.BlockSpec((1, HGRN_DIM), lambda b, n: (0, 0)),
                  pl.BlockSpec(tri.shape, lambda b, n: (0, 0)),
                  pl.BlockSpec(lm.shape, lambda b, n: (0, 0, 0))],
        out_specs=blk,
        scratch_shapes=[pltpu.VMEM((HGRN_HEADS, HGRN_DIM, HGRN_DIM), F32)],
        compiler_params=pltpu.CompilerParams(dimension_semantics=("arbitrary", "arbitrary"),
                                             vmem_limit_bytes=VMEM_LIMIT),
        name="hgrn2",
    )(hq, lf, hv, hg, hgrn_norm_w[None, :].astype(F32), tri, lm)


def _merge_kernel(x_ref, attn_ref, hgrn_ref, n1w_ref, wz_ref, wba_ref, wbh_ref, wout_ref, n2w_ref,
                  wr_ref, br_ref, utri_ref, x1_out, hn_out, route_out, count_out, run_ref):
    @pl.when(pl.program_id(0) == 0)
    def _():
        run_ref[...] = jnp.zeros_like(run_ref)

    x = x_ref[...]
    xn = _rms(x, n1w_ref[...]).astype(BF16)
    za = _sigmoid(_dot(xn, wz_ref[:, 0:D_MODEL]))
    zb = _sigmoid(_dot(xn, wz_ref[:, D_MODEL:2 * D_MODEL]))
    mixed = za * _dot(attn_ref[...], wba_ref[...]) + zb * _dot(hgrn_ref[...], wbh_ref[...])
    x1 = x + _dot(mixed.astype(BF16), wout_ref[...])
    x1_out[...] = x1
    hn = _rms(x1, n2w_ref[...])
    hn_out[...] = _pack_bf16_pairs(hn)

    logits = _dot_nt(wr_ref[...], hn.astype(BF16)) + br_ref[...]
    r = lax.broadcasted_iota(jnp.int32, logits.shape, 0).astype(F32)
    far = float(ROUTER_ROWS)
    cmax = lambda a: jnp.max(a, axis=0, keepdims=True)
    cmin = lambda a: jnp.min(a, axis=0, keepdims=True)
    csum = lambda a: jnp.sum(a, axis=0, keepdims=True)

    lg = jnp.where(r < N_GROUPS, logits, NEG_BIG)
    mg = cmax(lg)
    gsel = cmin(jnp.where(lg == mg, r, far))
    pgsel = 1.0 / csum(jnp.exp(lg - mg))

    lo = N_GROUPS + EXPERTS_PER_GROUP * gsel
    le = jnp.where((r >= lo) & (r < lo + EXPERTS_PER_GROUP), logits, NEG_BIG)
    m1 = cmax(le)
    i1 = cmin(jnp.where(le == m1, r, far))
    se = csum(jnp.exp(le - m1))
    le2 = jnp.where(r == i1, NEG_BIG, le)
    m2 = cmax(le2)
    i2 = cmin(jnp.where(le2 == m2, r, far))
    top0 = 1.0 / se
    top1 = jnp.exp(m2 - m1) / se
    tsum = top0 + top1
    w0 = pgsel * top0 / tsum
    w1 = pgsel * top1 / tsum

    sel1 = r == i1
    sel2 = r == i2
    onehot = jnp.where(sel1 | sel2, 1.0, 0.0)
    before = _dot(onehot.astype(BF16), utri_ref[...]) + run_ref[...]
    r0 = csum(jnp.where(sel1, before, 0.0))
    r1 = csum(jnp.where(sel2, before, 0.0))
    run_new = run_ref[...] + jnp.sum(onehot, axis=1, keepdims=True)
    run_ref[...] = run_new
    count_out[...] = run_new

    row8 = lax.broadcasted_iota(jnp.int32, route_out.shape, 0)
    vals = (i1 - N_GROUPS, i2 - N_GROUPS, w0, w1, r0, r1)
    route = jnp.zeros(route_out.shape, F32)
    for j, val in enumerate(vals):
        route = jnp.where(row8 == j, val, route)
    route_out[...] = route


def _merge(x2, attn, hgrn, norm1_w, w_z, w_ba, w_bh, w_out, norm2_w, w_r, br, tm):
    T = x2.shape[0]
    row = lambda w: pl.BlockSpec((tm, w), lambda i: (i, 0))
    full = lambda a: pl.BlockSpec(a.shape, lambda i: (0,) * a.ndim)
    utri = jnp.asarray(np.triu(np.ones((tm, tm), np.float32), 1), BF16)
    ins = [x2, attn, hgrn, norm1_w[None, :], w_z, w_ba, w_bh, w_out, norm2_w[None, :], w_r, br, utri]
    in_specs = [row(D_MODEL), row(ATTN_WIDTH), row(HGRN_WIDTH)] + [full(a) for a in ins[3:]]
    return pl.pallas_call(
        _merge_kernel,
        out_shape=[jax.ShapeDtypeStruct((T, D_MODEL), F32), jax.ShapeDtypeStruct((T, HALF), jnp.uint32),
                   jax.ShapeDtypeStruct((ROUTE_ROWS, T), F32), jax.ShapeDtypeStruct((ROUTER_ROWS, tm), F32)],
        grid=(T // tm,),
        in_specs=in_specs,
        out_specs=[row(D_MODEL), row(HALF), pl.BlockSpec((ROUTE_ROWS, tm), lambda i: (0, i)),
                   pl.BlockSpec((ROUTER_ROWS, tm), lambda i: (0, 0))],
        scratch_shapes=[pltpu.VMEM((ROUTER_ROWS, tm), F32)],
        compiler_params=pltpu.CompilerParams(dimension_semantics=("arbitrary",),
                                             vmem_limit_bytes=VMEM_LIMIT),
        name="merge_router",
    )(*ins)


HALF = D_MODEL // 2


def _pack_bf16_pairs(x):
    bits = pltpu.bitcast(x.astype(BF16).astype(F32), jnp.uint32)
    return (bits[:, :HALF] >> 16) | bits[:, HALF:]


def _unpack_bf16_pairs(words):
    lo = pltpu.bitcast(words << 16, F32)
    hi = pltpu.bitcast(words & jnp.uint32(0xFFFF0000), F32)
    return lo, hi


SC_CORES = 2
SC_SUBCORES = 16
SC_CHUNK = 64


def _sc_mesh():
    return plsc.VectorSubcoreMesh(core_axis_name="c", subcore_axis_name="s",
                                  num_cores=SC_CORES, num_subcores=SC_SUBCORES)


def _sc_worker():
    return lax.axis_index("s") * SC_CORES + lax.axis_index("c")


def _dispatch(pos, pad_rows, hn, n_rows):
    T, width = hn.shape
    workers = SC_CORES * SC_SUBCORES
    per_worker = T // workers
    n_chunks = per_worker // SC_CHUNK
    assert per_worker * workers == T and n_chunks * SC_CHUNK == per_worker and n_chunks % 2 == 0
    idx = pos.T.reshape(TOP_K, workers, n_chunks, SC_CHUNK).transpose(1, 0, 2, 3)
    pad_chunks = pad_rows.size // (workers * SC_CHUNK)
    pad_idx = pad_rows.reshape(workers, pad_chunks, SC_CHUNK)
    zeros = jnp.zeros((SC_CHUNK, width), hn.dtype)

    @functools.partial(
        pl.kernel, mesh=_sc_mesh(), name="moe_dispatch",
        out_type=jax.ShapeDtypeStruct((n_rows + SC_CHUNK, width), hn.dtype),
        scratch_types=[pltpu.VMEM((TOP_K, n_chunks, SC_CHUNK), jnp.int32),
                       pltpu.VMEM((2, SC_CHUNK, width), hn.dtype),
                       pltpu.VMEM((pad_chunks, SC_CHUNK), jnp.int32),
                       pltpu.VMEM((SC_CHUNK, width), hn.dtype),
                       pltpu.SemaphoreType.DMA((2,)), pltpu.SemaphoreType.DMA((2,)), pltpu.SemaphoreType.DMA])
    def dispatch_kernel(rows_hbm, idx_hbm, pad_hbm, zeros_hbm, out_hbm,
                        idx_v, rows_v, pad_v, zeros_v, load_sem, scatter_sem, pad_sem):
        wid = _sc_worker()
        base = wid * per_worker
        pltpu.sync_copy(idx_hbm.at[wid], idx_v)
        pltpu.sync_copy(pad_hbm.at[wid], pad_v)
        pltpu.sync_copy(zeros_hbm, zeros_v)

        def zero_fill(j):
            return pltpu.make_async_copy(zeros_v, out_hbm.at[pad_v.at[j]], pad_sem)

        for j in range(pad_chunks):
            zero_fill(j).start()

        def load(c, b):
            off = pl.multiple_of(c * SC_CHUNK, SC_CHUNK)
            return pltpu.make_async_copy(rows_hbm.at[pl.ds(base + off, SC_CHUNK)], rows_v.at[b], load_sem.at[b])

        def scatter(c, b, k):
            return pltpu.make_async_copy(rows_v.at[b], out_hbm.at[idx_v.at[k, c]], scatter_sem.at[b])

        load(0, 0).start()

        @pl.loop(0, n_chunks, step=2)
        def _(c0):
            for b in range(2):
                c = c0 + b
                load(c, b).wait()
                for k in range(TOP_K):
                    scatter(c, b, k).start()

                @pl.when(c >= 1)
                def _():
                    for k in range(TOP_K):
                        scatter(c - 1, 1 - b, k).wait()

                @pl.when(c + 1 < n_chunks)
                def _():
                    load(c + 1, 1 - b).start()

        for k in range(TOP_K):
            scatter(n_chunks - 1, (n_chunks - 1) % 2, k).wait()
        for j in range(pad_chunks):
            zero_fill(j).wait()

    return dispatch_kernel(hn, idx, pad_idx, zeros)


def _expert_kernel(tile_ref, exp_ref, valid_ref, x_ref, wg_ref, wu_ref, wd_ref, y_ref, wg_s, wu_s, wd_s):
    w = pl.program_id(0)
    prev = jnp.maximum(w - 1, 0)

    @pl.when((w == 0) | (exp_ref[w] != exp_ref[prev]))
    def _():
        wg_s[...] = wg_ref[0].astype(BF16)
        wu_s[...] = wu_ref[0].astype(BF16)
        wd_s[...] = wd_ref[0].astype(BF16)

    def ffn(words):
        lo, hi = _unpack_bf16_pairs(words)
        lo = lo.astype(BF16)
        hi = hi.astype(BF16)
        gate = _dot(lo, wg_s[0:HALF, :]) + _dot(hi, wg_s[HALF:, :])
        up = _dot(lo, wu_s[0:HALF, :]) + _dot(hi, wu_s[HALF:, :])
        return _pack_bf16_pairs(_dot((gate * _sigmoid(gate) * up).astype(BF16), wd_s[...]))

    @pl.when(valid_ref[w] != 0)
    def _():
        y_ref[...] = ffn(x_ref[...])


def _experts(items, xs, w_gate, w_up, w_down):
    tile, exp, valid = items
    wspec = lambda shape: pl.BlockSpec((1,) + shape, lambda w, t, e, v: (e[w], 0, 0))
    xspec = pl.BlockSpec((EXPERT_TILE, HALF), lambda w, t, e, v: (t[w], 0))
    return pl.pallas_call(
        _expert_kernel,
        out_shape=jax.ShapeDtypeStruct((tile.shape[0] * EXPERT_TILE, HALF), jnp.uint32),
        grid_spec=pltpu.PrefetchScalarGridSpec(
            num_scalar_prefetch=3,
            grid=(tile.shape[0],),
            in_specs=[xspec, wspec((D_MODEL, EXPERT_FF)), wspec((D_MODEL, EXPERT_FF)),
                      wspec((EXPERT_FF, D_MODEL))],
            out_specs=xspec,
            scratch_shapes=[pltpu.VMEM((D_MODEL, EXPERT_FF), BF16),
                            pltpu.VMEM((D_MODEL, EXPERT_FF), BF16),
                            pltpu.VMEM((EXPERT_FF, D_MODEL), BF16)]),
        compiler_params=pltpu.CompilerParams(dimension_semantics=("arbitrary",),
                                             vmem_limit_bytes=VMEM_LIMIT),
        name="moe_experts",
    )(tile, exp, valid, xs, w_gate, w_up, w_down)


def _sc_gather_rows(table, idx):
    n = idx.shape[0]
    width = table.shape[1]
    per_worker = n // (SC_CORES * SC_SUBCORES)
    n_chunks = per_worker // SC_CHUNK
    assert per_worker * SC_CORES * SC_SUBCORES == n and n_chunks * SC_CHUNK == per_worker and n_chunks % 2 == 0

    @functools.partial(
        pl.kernel, mesh=_sc_mesh(), name="moe_row_gather",
        out_type=jax.ShapeDtypeStruct((n, width), table.dtype),
        scratch_types=[pltpu.VMEM((per_worker,), jnp.int32),
                       pltpu.VMEM((2, SC_CHUNK, width), table.dtype),
                       pltpu.SemaphoreType.DMA((2,))])
    def gather_kernel(table_hbm, idx_hbm, out_hbm, idx_v, rows_v, sem):
        base = _sc_worker() * per_worker
        pltpu.sync_copy(idx_hbm.at[pl.ds(base, per_worker)], idx_v)

        def gather(c, b):
            off = pl.multiple_of(c * SC_CHUNK, SC_CHUNK)
            return pltpu.make_async_copy(table_hbm.at[idx_v.at[pl.ds(off, SC_CHUNK)]], rows_v.at[b], sem.at[b])

        gather(0, 0).start()

        @pl.loop(0, n_chunks, step=2)
        def _(c0):
            for b in range(2):
                c = c0 + b

                @pl.when(c + 1 < n_chunks)
                def _():
                    gather(c + 1, 1 - b).start()

                gather(c, b).wait()
                off = pl.multiple_of(c * SC_CHUNK, SC_CHUNK)
                pltpu.sync_copy(rows_v.at[b], out_hbm.at[pl.ds(base + off, SC_CHUNK)])

    return gather_kernel(table, idx)


def _combine_kernel(x1_ref, route_ref, y0_ref, y1_ref, o_ref):
    gates = route_ref[...].T
    w0 = gates[:, 2:3]
    w1 = gates[:, 3:4]
    lo0, hi0 = _unpack_bf16_pairs(y0_ref[...])
    lo1, hi1 = _unpack_bf16_pairs(y1_ref[...])
    o_ref[:, 0:HALF] = x1_ref[:, 0:HALF] + w0 * lo0 + w1 * lo1
    o_ref[:, HALF:] = x1_ref[:, HALF:] + w0 * hi0 + w1 * hi1


def _combine(pos, x1, route, y, tk):
    T = x1.shape[0]
    nt = T // tk
    ysel = _sc_gather_rows(y, pos.T.reshape(-1))
    return pl.pallas_call(
        _combine_kernel,
        out_shape=jax.ShapeDtypeStruct((T, D_MODEL), F32),
        grid=(nt,),
        in_specs=[pl.BlockSpec((tk, D_MODEL), lambda i: (i, 0)),
                  pl.BlockSpec((ROUTE_ROWS, tk), lambda i: (0, i)),
                  pl.BlockSpec((tk, HALF), lambda i: (i, 0)),
                  pl.BlockSpec((tk, HALF), lambda i: (i + nt, 0))],
        out_specs=pl.BlockSpec((tk, D_MODEL), lambda i: (i, 0)),
        compiler_params=pltpu.CompilerParams(dimension_semantics=("arbitrary",),
                                             vmem_limit_bytes=VMEM_LIMIT),
        name="moe_combine",
    )(x1, route, ysel, ysel)


def _routing_tables(route, counts, T):
    e = route[0:TOP_K].astype(jnp.int32)
    rank = route[4:4 + TOP_K].astype(jnp.int32)
    counts = counts.astype(jnp.int32)
    padded = (counts + EXPERT_TILE - 1) // EXPERT_TILE * EXPERT_TILE
    ends = jnp.cumsum(padded)
    starts = ends - padded
    ids = jnp.arange(N_EXPERTS, dtype=jnp.int32)
    pos = rank + jnp.sum(jnp.where(e[:, :, None] == ids, starts, 0), axis=-1)
    n_tiles = TOP_K * T // EXPERT_TILE + N_EXPERTS
    tile0 = jnp.arange(n_tiles, dtype=jnp.int32)
    valid = (tile0 * EXPERT_TILE < ends[-1]).astype(jnp.int32)
    tile = jnp.minimum(tile0, ends[-1] // EXPERT_TILE - 1)
    exp = jnp.minimum(jnp.sum(ends[None, :] <= (tile * EXPERT_TILE)[:, None], axis=1), N_EXPERTS - 1).astype(jnp.int32)
    j = jnp.arange(EXPERT_TILE, dtype=jnp.int32)[None, :]
    n_pad = (padded - counts)[:, None]
    spare = n_tiles * EXPERT_TILE + j % SC_CHUNK
    pad_rows = jnp.where(n_pad > 0, (starts + counts)[:, None] + j % jnp.maximum(n_pad, 1), spare)
    return pos.T, pad_rows, (tile, exp, valid)


def _pick_tile(n, pref):
    t = pref
    while n % t:
        t //= 2
    return t


def kernel(x, positions, norm1_w, w_in, q_norm_w, k_norm_w, attn_sinks, hgrn_lower_bounds, hgrn_norm_w,
           w_branch_attn, w_branch_hgrn, w_out, norm2_w, w_router_group, b_router_group, w_router_expert,
           b_router_expert, w_gate_experts, w_up_experts, w_down_experts):
    B, S, D = x.shape
    T = B * S
    x2 = x.reshape(T, D)
    tm = _pick_tile(T, 1024)

    inv_freq = ROPE_THETA ** (-jnp.arange(0, ROT_DIM, 2, dtype=F32) / ROT_DIM)
    ang = inv_freq[:, None] * positions.astype(F32).reshape(1, T)
    cs = jnp.concatenate(_split3(jnp.concatenate([jnp.cos(ang), jnp.sin(ang)], axis=0)), axis=0)

    w_in0 = w_in[0].astype(BF16)
    heads = lambda off: [w_in0[:, off + h * HEAD_DIM:off + (h + 1) * HEAD_DIM]
                         for h in range(N_KV_HEADS) for _ in range(2)]
    w_in_a = jnp.concatenate([w_in0[:, :_OFF_K]] + heads(_OFF_K) + heads(_OFF_V) + [w_in0[:, _OFF_HQ:_OFF_Z]],
                             axis=1)
    w_z = w_in0[:, _OFF_Z:]

    q, k, v, hq, lf, hv, hg = _inproj(x2, norm1_w[0], w_in_a, cs, q_norm_w[0], k_norm_w[0],
                                      hgrn_lower_bounds.astype(F32), _pick_tile(T, 1024))
    attn = _attention(q, k, v, attn_sinks[0].astype(F32), B, S)
    hgrn = _hgrn(hq, lf, hv, hg, hgrn_norm_w[0], B, S, _pick_tile(S, 1024))

    pad = ROUTER_ROWS - N_GROUPS - N_EXPERTS
    w_r = jnp.concatenate([w_router_group[0].T, w_router_expert[0].T, jnp.zeros((pad, D), F32)], axis=0)
    b_r =jnp.concatenate([b_router_group[0], b_router_expert[0], jnp.zeros((pad,), F32)]).astype(F32)
    b_r = jnp.broadcast_to(b_r[:, None], (ROUTER_ROWS, tm))

    x1, hn, route, counts = _merge(x2, attn, hgrn, norm1_w[0], w_z, w_branch_attn[0].astype(BF16),
                                   w_branch_hgrn[0].astype(BF16), w_out[0].astype(BF16), norm2_w[0],
                                   w_r.astype(BF16), b_r, tm)

    pos, pad_rows, items = _routing_tables(route, counts[N_GROUPS:N_GROUPS + N_EXPERTS, 0], T)
    tk = _pick_tile(T, 1024)
    xs = _dispatch(pos, pad_rows, hn, items[0].shape[0] * EXPERT_TILE)
    y = _experts(items, xs, w_gate_experts[0], w_up_experts[0], w_down_experts[0])
    out = _combine(pos, x1, route, y, tk)
    return out.reshape(B, S, D)
```

```python
import functools

import numpy as np
import jax
import jax.numpy as jnp
from jax import lax
from jax.experimental import pallas as pl
from jax.experimental.pallas import tpu as pltpu
from jax.experimental.pallas import tpu_sc as plsc

F32 = jnp.float32
BF16 = jnp.bfloat16

D_MODEL = 1024
N_Q_HEADS = 8
N_KV_HEADS = 2
GROUP = N_Q_HEADS // N_KV_HEADS
HEAD_DIM = 64
ROT_DIM = HEAD_DIM // 4
ROT_HALF = ROT_DIM // 2
ROPE_THETA = 500000.0
WINDOW = 128
ATTN_WIDTH = N_Q_HEADS * HEAD_DIM
KV_WIDTH = N_KV_HEADS * HEAD_DIM

HGRN_HEADS = 4
HGRN_DIM = 128
HGRN_WIDTH = HGRN_HEADS * HGRN_DIM
CHUNK = 64

N_GROUPS = 4
EXPERTS_PER_GROUP = 8
N_EXPERTS = N_GROUPS * EXPERTS_PER_GROUP
TOP_K = 2
EXPERT_FF = 512
EXPERT_TILE = 512
NORM_EPS = 1e-6
ROUTER_ROWS = 40
ROUTE_ROWS = 8

LANES = 128
NEG_BIG = -1e30
LOG2_E = 1.4426950408889634

_OFF_Q, _OFF_K, _OFF_V = 0, ATTN_WIDTH, ATTN_WIDTH + KV_WIDTH
_OFF_HQ = ATTN_WIDTH + 2 * KV_WIDTH
_OFF_HF = _OFF_HQ + HGRN_WIDTH
_OFF_HI = _OFF_HF + HGRN_WIDTH
_OFF_HG = _OFF_HI + HGRN_WIDTH
_OFF_Z = _OFF_HG + HGRN_WIDTH
_A_Q, _A_K, _A_V = 0, ATTN_WIDTH, ATTN_WIDTH + 2 * KV_WIDTH
_A_HQ = ATTN_WIDTH + 4 * KV_WIDTH
_A_HF = _A_HQ + HGRN_WIDTH
_A_HI = _A_HF + HGRN_WIDTH
_A_HG = _A_HI + HGRN_WIDTH

VMEM_LIMIT = 56 * 1024 * 1024


def _split3(a):
    hi = a.astype(BF16)
    r1 = a - hi.astype(F32)
    mid = r1.astype(BF16)
    lo = (r1 - mid.astype(F32)).astype(BF16)
    return hi, mid, lo


def _dot(a, b):
    return jnp.dot(a, b, preferred_element_type=F32)


def _dot_nt(a, b):
    return lax.dot_general(a, b, (((1,), (1,)), ((), ())), preferred_element_type=F32)


def _dot_tn(a, b):
    return lax.dot_general(a, b, (((0,), (0,)), ((), ())), preferred_element_type=F32)


def _sigmoid(x):
    return 1.0 / (1.0 + jnp.exp(-x))


def _rms(x, w):
    ms = jnp.mean(x * x, axis=-1, keepdims=True)
    return x * lax.rsqrt(ms + NORM_EPS) * w


def _inproj_kernel(x_ref, n1w_ref, w_ref, cs_ref, rope_e_ref, rope_c0_ref, qw_ref, kw_ref,
                   mq_ref, mk_ref, lbp_ref,
                   q_out, k_out, v_out, hq_out, lf_out, hv_out, hg_out):
    xn = _rms(x_ref[...], n1w_ref[...]).astype(BF16)

    def proj(off, width):
        return _dot(xn, w_ref[:, off:off + width])

    tabs = _dot_tn(cs_ref[...], rope_e_ref[...])
    c_tab = tabs[:, 0:LANES] + rope_c0_ref[...]
    s1_tab = tabs[:, LANES:2 * LANES]
    s2_tab = tabs[:, 2 * LANES:3 * LANES]

    def norm_rope(t, mavg_ref, w_row, scale):
        ms = _dot((t * t).astype(BF16), mavg_ref[...])
        tn = t * lax.rsqrt(ms + NORM_EPS) * w_row
        if scale != 1.0:
            tn = tn * scale
        outs = []
        for j in range(t.shape[1] // LANES):
            c = tn[:, j * LANES:(j + 1) * LANES]
            outs.append(c * c_tab
                        + pltpu.roll(c, LANES - ROT_HALF, 1) * s1_tab
                        + pltpu.roll(c, ROT_HALF, 1) * s2_tab)
        return outs[0] if len(outs) == 1 else jnp.concatenate(outs, axis=1)

    q_out[...] = norm_rope(proj(_A_Q, ATTN_WIDTH), mq_ref, qw_ref[...], HEAD_DIM ** -0.5).astype(BF16)
    k_out[...] = norm_rope(proj(_A_K, 2 * KV_WIDTH), mk_ref, kw_ref[...], 1.0).astype(BF16)
    v_out[...] = proj(_A_V, 2 * KV_WIDTH).astype(BF16)

    hq = proj(_A_HQ, HGRN_WIDTH)
    hq_out[...] = (hq * _sigmoid(hq)).astype(BF16)
    h0 = lbp_ref[0:1, :]
    h1 = lbp_ref[1:2, :]
    hm = jnp.maximum(h0, h1)
    e0 = jnp.exp(h0 - hm)
    e1 = jnp.exp(h1 - hm)
    lb = e0 / (e0 + e1)
    fg = lb + (1.0 - lb) * _sigmoid(proj(_A_HF, HGRN_WIDTH))
    lf_out[...] = jnp.log(fg) * LOG2_E
    hv_out[...] = proj(_A_HI, HGRN_WIDTH).astype(BF16)
    hg = proj(_A_HG, HGRN_WIDTH)
    hg_out[...] = (hg * _sigmoid(hg)).astype(BF16)


def _rope_constants():
    e = np.zeros((2 * ROT_HALF, 3 * LANES), np.float32)
    c0 = np.zeros((1, LANES), np.float32)
    for lane in range(LANES):
        d = lane % HEAD_DIM
        if d < ROT_HALF:
            e[d, lane] = 1.0
            e[ROT_HALF + d, LANES + lane] = -1.0
        elif d < ROT_DIM:
            e[d - ROT_HALF, lane] = 1.0
            e[ROT_HALF + d - ROT_HALF, 2 * LANES + lane] = 1.0
        else:
            c0[0, lane] = 1.0
    return jnp.asarray(np.concatenate([e, e, e], axis=0), BF16), jnp.asarray(c0, F32)


def _head_mean_matrix(width):
    idx = np.arange(width) // HEAD_DIM
    m = (idx[:, None] == idx[None, :]).astype(np.float32) / HEAD_DIM
    return jnp.asarray(m, BF16)


def _inproj(x2, norm1_w, w_in_a, cs, q_norm_w, k_norm_w, lbp, tm):
    T = x2.shape[0]
    rope_e, rope_c0 = _rope_constants()
    qw = jnp.tile(q_norm_w.astype(F32), N_Q_HEADS)[None, :]
    kw = jnp.tile(k_norm_w.astype(F32), 2 * N_KV_HEADS)[None, :]
    mq = _head_mean_matrix(ATTN_WIDTH)
    mk = _head_mean_matrix(2 * KV_WIDTH)
    row = lambda w: pl.BlockSpec((tm, w), lambda i: (i, 0))
    full = lambda a: pl.BlockSpec(a.shape, lambda i: (0,) * a.ndim)
    ins = [x2, norm1_w[None, :], w_in_a, cs, rope_e, rope_c0, qw, kw, mq, mk, lbp]
    in_specs = ([row(D_MODEL), full(ins[1]), full(w_in_a), pl.BlockSpec((cs.shape[0], tm), lambda i: (0, i))]
                + [full(a) for a in ins[4:]])
    outs = [(ATTN_WIDTH, BF16), (2 * KV_WIDTH, BF16), (2 * KV_WIDTH, BF16), (HGRN_WIDTH, BF16),
            (HGRN_WIDTH, F32), (HGRN_WIDTH, BF16), (HGRN_WIDTH, BF16)]
    return pl.pallas_call(
        _inproj_kernel,
        out_shape=[jax.ShapeDtypeStruct((T, w), dt) for w, dt in outs],
        grid=(T // tm,),
        in_specs=in_specs,
        out_specs=[row(w) for w, _ in outs],
        compiler_params=pltpu.CompilerParams(dimension_semantics=("arbitrary",),
                                             vmem_limit_bytes=VMEM_LIMIT),
        name="inproj",
    )(*ins)


ATTN_QBLOCKS = 8


def _attn_kernel(sink_ref, q_ref, kc_ref, kp_ref, vc_ref, vp_ref, half_ref, o_ref):
    n_qblocks = q_ref.shape[0] // WINDOW
    has_prev = pl.program_id(1) > 0
    above = (lax.broadcasted_iota(jnp.int32, (WINDOW, WINDOW), 1)
             > lax.broadcasted_iota(jnp.int32, (WINDOW, WINDOW), 0))
    left = lax.broadcasted_iota(jnp.int32, (WINDOW, LANES), 1) < HEAD_DIM
    half = (half_ref[0], half_ref[1])

    ks, rhs = [], []
    for h in range(N_KV_HEADS):
        cols = slice(h * LANES, (h + 1) * LANES)
        kall = jnp.concatenate([kp_ref[:, cols], kc_ref[:, cols]], axis=0)
        vall = jnp.concatenate([vp_ref[:, cols], vc_ref[:, cols]], axis=0)
        ks.append([kall * hm for hm in half])
        rhs.append([jnp.concatenate([vall * hm, hm], axis=1) for hm in half])

    units = [(j, h, pr, side) for j in range(n_qblocks) for h in range(N_KV_HEADS)
             for pr in range(GROUP // 2) for side in range(2)]
    scores = []
    for j, h, pr, side in units:
        pair = h * (GROUP // 2) + pr
        qp = q_ref[j * WINDOW:(j + 1) * WINDOW, pair * LANES:(pair + 1) * LANES]
        s = _dot_nt(qp, ks[h][side][j * WINDOW:(j + 2) * WINDOW])
        s_prev = s[:, 0:WINDOW]
        if j == 0:
            s_prev = jnp.where(has_prev, s_prev, NEG_BIG)
        scores.append(jnp.where(above, s_prev, s[:, WINDOW:2 * WINDOW]))
    probs, sink_terms = [], []
    for (j, h, pr, side), s in zip(units, scores):
        sink = sink_ref[2 * (h * (GROUP // 2) + pr) + side]
        m = jnp.maximum(jnp.max(s, axis=-1, keepdims=True), sink)
        p = jnp.exp(s - m)
        probs.append(jnp.concatenate([jnp.where(above, p, 0.0), jnp.where(above, 0.0, p)], axis=1).astype(BF16))
        sink_terms.append(jnp.exp(sink - m))
    acc = []
    for (j, h, pr, side), p in zip(units, probs):
        acc.append(_dot(p, rhs[h][side][j * WINDOW:(j + 2) * WINDOW]))
    for j in range(n_qblocks):
        outs = []
        for u in range(0, len(units), 2):
            if units[u][0] == j:
                both = acc[u] + acc[u + 1]
                den = both[:, LANES:2 * LANES] + jnp.where(left, sink_terms[u], sink_terms[u + 1])
                outs.append(both[:, 0:LANES] / den)
        o_ref[j * WINDOW:(j + 1) * WINDOW, :] = jnp.concatenate(outs, axis=1).astype(BF16)


def _attention(q, k, v, sinks, B, S):
    qblocks = _pick_tile(S // WINDOW, ATTN_QBLOCKS)
    rows = qblocks * WINDOW
    nb = S // rows
    cur = lambda b, n: (b * nb + n, 0)
    prev = lambda b, n: (jnp.maximum((b * nb + n) * qblocks - 1, 0), 0)
    lane_left = np.arange(LANES) < HEAD_DIM
    half = jnp.asarray(np.broadcast_to(np.stack([lane_left, ~lane_left])[:, None, :],
                                       (2, rows + WINDOW, LANES)), BF16)
    return pl.pallas_call(
        _attn_kernel,
        out_shape=jax.ShapeDtypeStruct((B * S, ATTN_WIDTH), BF16),
        grid=(B, nb),
        in_specs=[pl.BlockSpec(memory_space=pltpu.SMEM),
                  pl.BlockSpec((rows, ATTN_WIDTH), cur),
                  pl.BlockSpec((rows, 2 * KV_WIDTH), cur),
                  pl.BlockSpec((WINDOW, 2 * KV_WIDTH), prev),
                  pl.BlockSpec((rows, 2 * KV_WIDTH), cur),
                  pl.BlockSpec((WINDOW, 2 * KV_WIDTH), prev),
                  pl.BlockSpec(half.shape, lambda b, n: (0, 0, 0))],
        out_specs=pl.BlockSpec((rows, ATTN_WIDTH), cur),
        compiler_params=pltpu.CompilerParams(dimension_semantics=("arbitrary", "arbitrary"),
                                             vmem_limit_bytes=VMEM_LIMIT),
        name="swa_attention",
    )(sinks, q, k, k, v, v, half)


_LEVEL_HALVES = (1, 2, 4, 8, 16, 32)


def _hgrn_level_masks():
    t = np.arange(CHUNK)[:, None]
    s = np.arange(CHUNK)[None, :]
    masks = [((t // (2 * h)) == (s // (2 * h))) & ((t & h) != 0) & ((s & h) == 0) for h in _LEVEL_HALVES]
    level = -np.ones((CHUNK, CHUNK), np.float32)
    for li, m in enumerate(masks):
        level[m] = li
    return jnp.asarray(level[None], F32)


def _level_reference(b, half):
    if half >= 4:
        span = max(2 * half, 8)
        pieces = [jnp.broadcast_to(b[s + half - 1:s + half, :], (span, HGRN_DIM))
                  for s in range(0, CHUNK, span)]
    else:
        r8 = lax.broadcasted_iota(jnp.int32, (8, HGRN_DIM), 0)
        pieces = [jnp.where(r8 < 4,
                            jnp.broadcast_to(b[s + 1:s + 2, :], (8, HGRN_DIM)),
                            jnp.broadcast_to(b[s + 5:s + 6, :], (8, HGRN_DIM)))
                  for s in range(0, CHUNK, 8)]
    return pieces[0] if len(pieces) == 1 else jnp.concatenate(pieces, axis=0)


def _hgrn_kernel(hq_ref, lf_ref, hv_ref, hg_ref, nw_ref, tri_ref, lm_ref, o_ref, st_ref):
    @pl.when(pl.program_id(1) == 0)
    def _():
        st_ref[...] = jnp.zeros_like(st_ref)

    tri2 = tri_ref[...]
    odd = (lax.broadcasted_iota(jnp.int32, (CHUNK, HGRN_DIM), 0) & 1) != 0
    level_of = lm_ref[0]
    units = [(c, h) for c in range(hq_ref.shape[0] // CHUNK) for h in range(HGRN_HEADS)]
    sl = lambda c, h: (slice(c * CHUNK, (c + 1) * CHUNK), slice(h * HGRN_DIM, (h + 1) * HGRN_DIM))

    bs = []
    for u, (c, h) in enumerate(units):
        lf2 = lf_ref[sl(c, h)]
        hi = lf2.astype(BF16)
        lo = (lf2 - hi.astype(F32)).astype(BF16)
        bs.append(_dot(tri2, jnp.concatenate([hi, lo], axis=0)))

    outs, accs = [], []
    for u, (c, h) in enumerate(units):
        b = bs[u]
        q = hq_ref[sl(c, h)].astype(F32)
        v_bf = hv_ref[sl(c, h)]
        f = jnp.exp2(lf_ref[sl(c, h)])
        k = 1.0 - f
        b_last = b[CHUNK - 1:CHUNK, :]
        st = st_ref[h]
        o = _dot_nt((q * jnp.exp2(b)).astype(BF16), st.astype(BF16))
        k_out = (k * jnp.exp2(b_last - b)).astype(BF16)
        st_ref[h] = st * jnp.exp2(b_last) + _dot_tn(v_bf, k_out)
        outs.append(o + jnp.sum(q * k, axis=-1, keepdims=True) * v_bf.astype(F32))
        acc = jnp.zeros((CHUNK, CHUNK), F32)
        for li, half in enumerate(_LEVEL_HALVES):
            if half == 1:
                e = jnp.where(odd, f, 1.0)
            else:
                e = jnp.exp2(-jnp.abs(b - _level_reference(b, half)))
            acc = jnp.where(level_of == float(li), _dot_nt((q * e).astype(BF16), (k * e).astype(BF16)), acc)
        accs.append(acc)

    for u, (c, h) in enumerate(units):
        o = outs[u] + _dot(accs[u].astype(BF16), hv_ref[sl(c, h)])
        y = _rms(o, nw_ref[...]) * hg_ref[sl(c, h)].astype(F32)
        o_ref[sl(c, h)] = y.astype(BF16)


def _hgrn(hq, lf, hv, hg, hgrn_norm_w, B, S, tb):
    nt = S // tb
    blk = pl.BlockSpec((tb, HGRN_WIDTH), lambda b, n: (b * nt + n, 0))
    tri = np.tril(np.ones((CHUNK, CHUNK), np.float32))
    tri = jnp.asarray(np.concatenate([tri, tri], axis=1), BF16)
    lm = _hgrn_level_masks()
    return pl.pallas_call(
        _hgrn_kernel,
        out_shape=jax.ShapeDtypeStruct((B * S, HGRN_WIDTH), BF16),
        grid=(B, nt),
        in_specs=[blk, blk, blk, blk,
                  pl.BlockSpec((1, HGRN_DIM), lambda b, n: (0, 0)),
                  pl.BlockSpec(tri.shape, lambda b, n: (0, 0)),
                  pl.BlockSpec(lm.shape, lambda b, n: (0, 0, 0))],
        out_specs=blk,
        scratch_shapes=[pltpu.VMEM((HGRN_HEADS, HGRN_DIM, HGRN_DIM), F32)],
        compiler_params=pltpu.CompilerParams(dimension_semantics=("arbitrary", "arbitrary"),
                                             vmem_limit_bytes=VMEM_LIMIT),
        name="hgrn2",
    )(hq, lf, hv, hg, hgrn_norm_w[None, :].astype(F32), tri, lm)


def _merge_kernel(x_ref, attn_ref, hgrn_ref, n1w_ref, wz_ref, wba_ref, wbh_ref, wout_ref, n2w_ref,
                  wr_ref, br_ref, utri_ref, x1_out, hn_out, route_out, count_out, run_ref):
    @pl.when(pl.program_id(0) == 0)
    def _():
        run_ref[...] = jnp.zeros_like(run_ref)

    x = x_ref[...]
    xn = _rms(x, n1w_ref[...]).astype(BF16)
    za = _sigmoid(_dot(xn, wz_ref[:, 0:D_MODEL]))
    zb = _sigmoid(_dot(xn, wz_ref[:, D_MODEL:2 * D_MODEL]))
    mixed = za * _dot(attn_ref[...], wba_ref[...]) + zb * _dot(hgrn_ref[...], wbh_ref[...])
    x1 = x + _dot(mixed.astype(BF16), wout_ref[...])
    x1_out[...] = x1
    hn = _rms(x1, n2w_ref[...])
    hn_out[...] = _pack_bf16_pairs(hn)

    logits = _dot_nt(wr_ref[...], hn.astype(BF16)) + br_ref[...]
    r = lax.broadcasted_iota(jnp.int32, logits.shape, 0).astype(F32)
    far = float(ROUTER_ROWS)
    cmax = lambda a: jnp.max(a, axis=0, keepdims=True)
    cmin = lambda a: jnp.min(a, axis=0, keepdims=True)
    csum = lambda a: jnp.sum(a, axis=0, keepdims=True)

    lg = jnp.where(r < N_GROUPS, logits, NEG_BIG)
    mg = cmax(lg)
    gsel = cmin(jnp.where(lg == mg, r, far))
    pgsel = 1.0 / csum(jnp.exp(lg - mg))

    lo = N_GROUPS + EXPERTS_PER_GROUP * gsel
    le = jnp.where((r >= lo) & (r < lo + EXPERTS_PER_GROUP), logits, NEG_BIG)
    m1 = cmax(le)
    i1 = cmin(jnp.where(le == m1, r, far))
    se = csum(jnp.exp(le - m1))
    le2 = jnp.where(r == i1, NEG_BIG, le)
    m2 = cmax(le2)
    i2 = cmin(jnp.where(le2 == m2, r, far))
    top0 = 1.0 / se
    top1 = jnp.exp(m2 - m1) / se
    tsum = top0 + top1
    w0 = pgsel * top0 / tsum
    w1 = pgsel * top1 / tsum

    sel1 = r == i1
    sel2 = r == i2
    onehot = jnp.where(sel1 | sel2, 1.0, 0.0)
    before = _dot(onehot.astype(BF16), utri_ref[...]) + run_ref[...]
    r0 = csum(jnp.where(sel1, before, 0.0))
    r1 = csum(jnp.where(sel2, before, 0.0))
    run_new = run_ref[...] + jnp.sum(onehot, axis=1, keepdims=True)
    run_ref[...] = run_new
    count_out[...] = run_new

    row8 = lax.broadcasted_iota(jnp.int32, route_out.shape, 0)
    vals = (i1 - N_GROUPS, i2 - N_GROUPS, w0, w1, r0, r1)
    route = jnp.zeros(route_out.shape, F32)
    for j, val in enumerate(vals):
        route = jnp.where(row8 == j, val, route)
    route_out[...] = route


def _merge(x2, attn, hgrn, norm1_w, w_z, w_ba, w_bh, w_out, norm2_w, w_r, br, tm):
    T = x2.shape[0]
    row = lambda w: pl.BlockSpec((tm, w), lambda i: (i, 0))
    full = lambda a: pl.BlockSpec(a.shape, lambda i: (0,) * a.ndim)
    utri = jnp.asarray(np.triu(np.ones((tm, tm), np.float32), 1), BF16)
    ins = [x2, attn, hgrn, norm1_w[None, :], w_z, w_ba, w_bh, w_out, norm2_w[None, :], w_r, br, utri]
    in_specs = [row(D_MODEL), row(ATTN_WIDTH), row(HGRN_WIDTH)] + [full(a) for a in ins[3:]]
    return pl.pallas_call(
        _merge_kernel,
        out_shape=[jax.ShapeDtypeStruct((T, D_MODEL), F32), jax.ShapeDtypeStruct((T, HALF), jnp.uint32),
                   jax.ShapeDtypeStruct((ROUTE_ROWS, T), F32), jax.ShapeDtypeStruct((ROUTER_ROWS, tm), F32)],
        grid=(T // tm,),
        in_specs=in_specs,
        out_specs=[row(D_MODEL), row(HALF), pl.BlockSpec((ROUTE_ROWS, tm), lambda i: (0, i)),
                   pl.BlockSpec((ROUTER_ROWS, tm), lambda i: (0, 0))],
        scratch_shapes=[pltpu.VMEM((ROUTER_ROWS, tm), F32)],
        compiler_params=pltpu.CompilerParams(dimension_semantics=("arbitrary",),
                                             vmem_limit_bytes=VMEM_LIMIT),
        name="merge_router",
    )(*ins)


HALF = D_MODEL // 2


def _pack_bf16_pairs(x):
    bits = pltpu.bitcast(x.astype(BF16).astype(F32), jnp.uint32)
    return (bits[:, :HALF] >> 16) | bits[:, HALF:]


def _unpack_bf16_pairs(words):
    lo = pltpu.bitcast(words << 16, F32)
    hi = pltpu.bitcast(words & jnp.uint32(0xFFFF0000), F32)
    return lo, hi


SC_CORES = 2
SC_SUBCORES = 16
SC_CHUNK = 64
SC_GATHER_RING = 4


def _sc_mesh():
    return plsc.VectorSubcoreMesh(core_axis_name="c", subcore_axis_name="s",
                                  num_cores=SC_CORES, num_subcores=SC_SUBCORES)


def _sc_worker():
    return lax.axis_index("s") * SC_CORES + lax.axis_index("c")


def _dispatch(pos, pad_rows, hn, n_rows):
    T, width = hn.shape
    workers = SC_CORES * SC_SUBCORES
    per_worker = T // workers
    n_chunks = per_worker // SC_CHUNK
    assert per_worker * workers == T and n_chunks * SC_CHUNK == per_worker and n_chunks % 2 == 0
    idx = pos.T.reshape(TOP_K, workers, n_chunks, SC_CHUNK).transpose(1, 0, 2, 3)
    pad_chunks = pad_rows.size // (workers * SC_CHUNK)
    pad_idx = pad_rows.reshape(workers, pad_chunks, SC_CHUNK)
    zeros = jnp.zeros((SC_CHUNK, width), hn.dtype)

    @functools.partial(
        pl.kernel, mesh=_sc_mesh(), name="moe_dispatch",
        out_type=jax.ShapeDtypeStruct((n_rows + SC_CHUNK, width), hn.dtype),
        scratch_types=[pltpu.VMEM((TOP_K, n_chunks, SC_CHUNK), jnp.int32),
                       pltpu.VMEM((2, SC_CHUNK, width), hn.dtype),
                       pltpu.VMEM((pad_chunks, SC_CHUNK), jnp.int32),
                       pltpu.VMEM((SC_CHUNK, width), hn.dtype),
                       pltpu.SemaphoreType.DMA((2,)), pltpu.SemaphoreType.DMA((2,)), pltpu.SemaphoreType.DMA])
    def dispatch_kernel(rows_hbm, idx_hbm, pad_hbm, zeros_hbm, out_hbm,
                        idx_v, rows_v, pad_v, zeros_v, load_sem, scatter_sem, pad_sem):
        wid = _sc_worker()
        base = wid * per_worker
        pltpu.sync_copy(idx_hbm.at[wid], idx_v)
        pltpu.sync_copy(pad_hbm.at[wid], pad_v)
        pltpu.sync_copy(zeros_hbm, zeros_v)

        def zero_fill(j):
            return pltpu.make_async_copy(zeros_v, out_hbm.at[pad_v.at[j]], pad_sem)

        for j in range(pad_chunks):
            zero_fill(j).start()

        def load(c, b):
            off = pl.multiple_of(c * SC_CHUNK, SC_CHUNK)
            return pltpu.make_async_copy(rows_hbm.at[pl.ds(base + off, SC_CHUNK)], rows_v.at[b], load_sem.at[b])

        def scatter(c, b, k):
            return pltpu.make_async_copy(rows_v.at[b], out_hbm.at[idx_v.at[k, c]], scatter_sem.at[b])

        load(0, 0).start()

        @pl.loop(0, n_chunks, step=2)
        def _(c0):
            for b in range(2):
                c = c0 + b
                load(c, b).wait()
                for k in range(TOP_K):
                    scatter(c, b, k).start()

                @pl.when(c >= 1)
                def _():
                    for k in range(TOP_K):
                        scatter(c - 1, 1 - b, k).wait()

                @pl.when(c + 1 < n_chunks)
                def _():
                    load(c + 1, 1 - b).start()

        for k in range(TOP_K):
            scatter(n_chunks - 1, (n_chunks - 1) % 2, k).wait()
        for j in range(pad_chunks):
            zero_fill(j).wait()

    return dispatch_kernel(hn, idx, pad_idx, zeros)


def _expert_kernel(tile_ref, exp_ref, valid_ref, x_ref, wg_ref, wu_ref, wd_ref, y_ref, wg_s, wu_s, wd_s):
    w = pl.program_id(0)
    prev = jnp.maximum(w - 1, 0)

    @pl.when((w == 0) | (exp_ref[w] != exp_ref[prev]))
    def _():
        wg_s[...] = wg_ref[0].astype(BF16)
        wu_s[...] = wu_ref[0].astype(BF16)
        wd_s[...] = wd_ref[0].astype(BF16)

    def ffn(words):
        lo, hi = _unpack_bf16_pairs(words)
        lo = lo.astype(BF16)
        hi = hi.astype(BF16)
        gate = _dot(lo, wg_s[0:HALF, :]) + _dot(hi, wg_s[HALF:, :])
        up = _dot(lo, wu_s[0:HALF, :]) + _dot(hi, wu_s[HALF:, :])
        return _pack_bf16_pairs(_dot((gate * _sigmoid(gate) * up).astype(BF16), wd_s[...]))

    @pl.when(valid_ref[w] != 0)
    def _():
        y_ref[...] = ffn(x_ref[...])


def _experts(items, xs, w_gate, w_up, w_down):
    tile, exp, valid = items
    wspec = lambda shape: pl.BlockSpec((1,) + shape, lambda w, t, e, v: (e[w], 0, 0))
    xspec = pl.BlockSpec((EXPERT_TILE, HALF), lambda w, t, e, v: (t[w], 0))
    return pl.pallas_call(
        _expert_kernel,
        out_shape=jax.ShapeDtypeStruct((tile.shape[0] * EXPERT_TILE, HALF), jnp.uint32),
        grid_spec=pltpu.PrefetchScalarGridSpec(
            num_scalar_prefetch=3,
            grid=(tile.shape[0],),
            in_specs=[xspec, wspec((D_MODEL, EXPERT_FF)), wspec((D_MODEL, EXPERT_FF)),
                      wspec((EXPERT_FF, D_MODEL))],
            out_specs=xspec,
            scratch_shapes=[pltpu.VMEM((D_MODEL, EXPERT_FF), BF16),
                            pltpu.VMEM((D_MODEL, EXPERT_FF), BF16),
                            pltpu.VMEM((EXPERT_FF, D_MODEL), BF16)]),
        compiler_params=pltpu.CompilerParams(dimension_semantics=("arbitrary",),
                                             vmem_limit_bytes=VMEM_LIMIT),
        name="moe_experts",
    )(tile, exp, valid, xs, w_gate, w_up, w_down)


def _sc_gather_rows(table, idx):
    n = idx.shape[0]
    width = table.shape[1]
    chunk = SC_CHUNK // 2
    ring = SC_GATHER_RING
    per_worker = n // (SC_CORES * SC_SUBCORES)
    n_chunks = per_worker // chunk
    assert per_worker * SC_CORES * SC_SUBCORES == n and n_chunks * chunk == per_worker and n_chunks % ring == 0

    @functools.partial(
        pl.kernel, mesh=_sc_mesh(), name="moe_row_gather",
        out_type=jax.ShapeDtypeStruct((n, width), table.dtype),
        scratch_types=[pltpu.VMEM((per_worker,), jnp.int32),
                       pltpu.VMEM((ring, chunk, width), table.dtype),
                       pltpu.SemaphoreType.DMA((ring,))])
    def gather_kernel(table_hbm, idx_hbm, out_hbm, idx_v, rows_v, sem):
        base = _sc_worker() * per_worker
        pltpu.sync_copy(idx_hbm.at[pl.ds(base, per_worker)], idx_v)

        def gather(c, b):
            off = pl.multiple_of(c * chunk, chunk)
            return pltpu.make_async_copy(table_hbm.at[idx_v.at[pl.ds(off, chunk)]], rows_v.at[b], sem.at[b])

        for c in range(ring - 1):
            gather(c, c).start()

        @pl.loop(0, n_chunks, step=ring)
        def _(c0):
            for b in range(ring):
                c = c0 + b

                @pl.when(c + ring - 1 < n_chunks)
                def _():
                    gather(c + ring - 1, (b + ring - 1) % ring).start()

                gather(c, b).wait()
                off = pl.multiple_of(c * chunk, chunk)
                pltpu.sync_copy(rows_v.at[b], out_hbm.at[pl.ds(base + off, chunk)])

    return gather_kernel(table, idx)


def _combine_kernel(x1_ref, route_ref, y0_ref, y1_ref, o_ref):
    gates = route_ref[...].T
    w0 = gates[:, 2:3]
    w1 = gates[:, 3:4]
    lo0, hi0 = _unpack_bf16_pairs(y0_ref[...])
    lo1, hi1 = _unpack_bf16_pairs(y1_ref[...])
    o_ref[:, 0:HALF] = x1_ref[:, 0:HALF] + w0 * lo0 + w1 * lo1
    o_ref[:, HALF:] = x1_ref[:, HALF:] + w0 * hi0 + w1 * hi1


def _combine(pos, x1, route, y, tk):
    T = x1.shape[0]
    nt = T // tk
    ysel = _sc_gather_rows(y, pos.T.reshape(-1))
    return pl.pallas_call(
        _combine_kernel,
        out_shape=jax.ShapeDtypeStruct((T, D_MODEL), F32),
        grid=(nt,),
        in_specs=[pl.BlockSpec((tk, D_MODEL), lambda i: (i, 0)),
                  pl.BlockSpec((ROUTE_ROWS, tk), lambda i: (0, i)),
                  pl.BlockSpec((tk, HALF), lambda i: (i, 0)),
                  pl.BlockSpec((tk, HALF), lambda i: (i + nt, 0))],
        out_specs=pl.BlockSpec((tk, D_MODEL), lambda i: (i, 0)),
        compiler_params=pltpu.CompilerParams(dimension_semantics=("arbitrary",),
                                             vmem_limit_bytes=VMEM_LIMIT),
        name="moe_combine",
    )(x1, route, ysel, ysel)


def _routing_tables(route, counts, T):
    e = route[0:TOP_K].astype(jnp.int32)
    rank = route[4:4 + TOP_K].astype(jnp.int32)
    counts = counts.astype(jnp.int32)
    padded = (counts + EXPERT_TILE - 1) // EXPERT_TILE * EXPERT_TILE
    ends = jnp.cumsum(padded)
    starts = ends - padded
    ids = jnp.arange(N_EXPERTS, dtype=jnp.int32)
    pos = rank + jnp.sum(jnp.where(e[:, :, None] == ids, starts, 0), axis=-1)
    n_tiles = TOP_K * T // EXPERT_TILE + N_EXPERTS
    tile0 = jnp.arange(n_tiles, dtype=jnp.int32)
    valid = (tile0 * EXPERT_TILE < ends[-1]).astype(jnp.int32)
    tile = jnp.minimum(tile0, ends[-1] // EXPERT_TILE - 1)
    exp = jnp.minimum(jnp.sum(ends[None, :] <= (tile * EXPERT_TILE)[:, None], axis=1), N_EXPERTS - 1).astype(jnp.int32)
    j = jnp.arange(EXPERT_TILE, dtype=jnp.int32)[None, :]
    n_pad = (padded - counts)[:, None]
    spare = n_tiles * EXPERT_TILE + j % SC_CHUNK
    pad_rows = jnp.where(n_pad > 0, (starts + counts)[:, None] + j % jnp.maximum(n_pad, 1), spare)
    return pos.T, pad_rows, (tile, exp, valid)


def _pick_tile(n, pref):
    t = pref
    while n % t:
        t //= 2
    return t


def kernel(x, positions, norm1_w, w_in, q_norm_w, k_norm_w, attn_sinks, hgrn_lower_bounds, hgrn_norm_w,
           w_branch_attn, w_branch_hgrn, w_out, norm2_w, w_router_group, b_router_group, w_router_expert,
           b_router_expert, w_gate_experts, w_up_experts, w_down_experts):
    B, S, D = x.shape
    T = B * S
    x2 = x.reshape(T, D)
    tm = _pick_tile(T, 1024)

    inv_freq = ROPE_THETA ** (-jnp.arange(0, ROT_DIM, 2, dtype=F32) / ROT_DIM)
    ang = inv_freq[:, None] * positions.astype(F32).reshape(1, T)
    cs = jnp.concatenate(_split3(jnp.concatenate([jnp.cos(ang), jnp.sin(ang)], axis=0)), axis=0)

    w_in0 = w_in[0].astype(BF16)
    heads = lambda off: [w_in0[:, off + h * HEAD_DIM:off + (h + 1) * HEAD_DIM]
                         for h in range(N_KV_HEADS) for _ in range(2)]
    w_in_a = jnp.concatenate([w_in0[:, :_OFF_K]] + heads(_OFF_K) + heads(_OFF_V) + [w_in0[:, _OFF_HQ:_OFF_Z]],
                             axis=1)
    w_z = w_in0[:, _OFF_Z:]

    q, k, v, hq, lf, hv, hg = _inproj(x2, norm1_w[0], w_in_a, cs, q_norm_w[0], k_norm_w[0],
                                      hgrn_lower_bounds.astype(F32), _pick_tile(T, 1024))
    attn = _attention(q, k, v, attn_sinks[0].astype(F32), B, S)
    hgrn = _hgrn(hq, lf, hv, hg, hgrn_norm_w[0], B, S, _pick_tile(S, 1024))

    pad = ROUTER_ROWS - N_GROUPS - N_EXPERTS
    w_r = jnp.concatenate([w_router_group[0].T, w_router_expert[0].T, jnp.zeros((pad, D), F32)], axis=0)
    b_r =jnp.concatenate([b_router_group[0], b_router_expert[0], jnp.zeros((pad,), F32)]).astype(F32)
    b_r = jnp.broadcast_to(b_r[:, None], (ROUTER_ROWS, tm))

    x1, hn, route, counts = _merge(x2, attn, hgrn, norm1_w[0], w_z, w_branch_attn[0].astype(BF16),
                                   w_branch_hgrn[0].astype(BF16), w_out[0].astype(BF16), norm2_w[0],
                                   w_r.astype(BF16), b_r, tm)

    pos, pad_rows, items = _routing_tables(route, counts[N_GROUPS:N_GROUPS + N_EXPERTS, 0], T)
    tk = _pick_tile(T, 1024)
    xs = _dispatch(pos, pad_rows, hn, items[0].shape[0] * EXPERT_TILE)
    y = _experts(items, xs, w_gate_experts[0], w_up_experts[0], w_down_experts[0])
    out = _combine(pos, x1, route, y, tk)
    return out.reshape(B, S, D)
```
